```python
import math
import jax
import jax.numpy as jnp
from jax import lax
import numpy as np

D_MODEL = 2048
BATCH = 8
SEQ = 4096
DEPTH = 4

ATT_PATTERNS = ((128, 1), (512, 4), (2048, 16))
N_GROUPS_A = len(ATT_PATTERNS)
HEADS_PER_GROUP = 8
HEAD_DIM = 64
N_HEADS_A = N_GROUPS_A * HEADS_PER_GROUP
QKV_WIDTH_A = N_HEADS_A * HEAD_DIM
WIDTH_A = HEADS_PER_GROUP * HEAD_DIM
ATT_BLOCK = 128
N_REL_BUCKETS = 32
REL_MAX_DIST = 2048
NEG_INF = -1e30
CHUNK = 128
WIDTH_B = 768
N_GROUPS_B = 6
GROUP_B = WIDTH_B // N_GROUPS_B
WIDTH_C = 768
SSM_GROUP = 16
N_GROUPS_C = WIDTH_C // SSM_GROUP
SSM_STATE = 64
DT_MIN = 1e-3
DT_MAX = 1e-1
N_BRANCH = 3
D_FF = -(-8 * D_MODEL // (3 * 256)) * 256
ALPHA = (2 * DEPTH) ** 0.25
BETA = (8 * DEPTH) ** -0.25
IN_SPLIT = (QKV_WIDTH_A, QKV_WIDTH_A, QKV_WIDTH_A, 2 * WIDTH_B, WIDTH_C, N_BRANCH * D_MODEL)
IN_COLS = sum(IN_SPLIT)

kernel_name = 'hybrid_dilated_attn_gmlp_s5_deepnorm'


def _layer_norm(x, g, b, eps=1e-5):
    xf = x.astype(jnp.float32)
    mu = jnp.mean(xf, axis=-1, keepdims=True)
    var = jnp.mean(jnp.square(xf - mu), axis=-1, keepdims=True)
    return ((xf - mu) * lax.rsqrt(var + eps) * g + b).astype(x.dtype)


def _t5_bucket(dist):
    max_exact = N_REL_BUCKETS // 2
    d = np.maximum(dist, 1).astype(np.float32)
    scale = (N_REL_BUCKETS - max_exact) / math.log(REL_MAX_DIST / max_exact)
    large = max_exact + (np.log(d / max_exact) * scale).astype(np.int32)
    large = np.minimum(large, N_REL_BUCKETS - 1)
    return np.where(dist < max_exact, dist, large).astype(np.int32)


def _band_steps():
    i = np.arange(ATT_BLOCK)[:, None]
    kk = np.arange(2 * ATT_BLOCK)[None, :]
    return ATT_BLOCK + i - kk


def _group_rel_bias(rel_bias, g, dilation):
    bucket = _t5_bucket(np.maximum(_band_steps(), 0) * dilation)
    cols = rel_bias[:, g * HEADS_PER_GROUP:(g + 1) * HEADS_PER_GROUP]
    return jnp.transpose(cols[bucket], (2, 0, 1)).astype(jnp.float32)


def _dilated_window_attention(q, k, v, bias, dilation, n_steps):
    bsz, s, h, hd = q.shape
    L = s // dilation
    nb = -(-L // ATT_BLOCK)
    lp = nb * ATT_BLOCK

    def to_streams(t):
        t = t.reshape(bsz, L, dilation, h, hd).transpose(0, 2, 1, 3, 4)
        return jnp.pad(t, ((0, 0), (0, 0), (0, lp - L), (0, 0), (0, 0)))

    def to_band(t):
        t = jnp.pad(t, ((0, 0), (0, 0), (ATT_BLOCK, 0), (0, 0), (0, 0)))
        t = t.reshape(bsz, dilation, nb + 1, ATT_BLOCK, h, hd)
        return jnp.concatenate([t[:, :, :-1], t[:, :, 1:]], axis=3)

    qb = to_streams(q).reshape(bsz, dilation, nb, ATT_BLOCK, h, hd)
    kb = to_band(to_streams(k))
    vb = to_band(to_streams(v))
    steps = _band_steps()
    key_idx = (np.arange(nb)[:, None] - 1) * ATT_BLOCK + np.arange(2 * ATT_BLOCK)[None, :]
    mask = ((steps >= 0) & (steps <= n_steps))[None, None] & (key_idx >= 0)[:, None, None, :]
    logits = jnp.einsum('brnqhd,brnkhd->brnhqk', qb, kb, preferred_element_type=jnp.float32)
    logits = jnp.where(mask, logits * (hd ** -0.5) + bias, NEG_INF)
    m = jnp.max(logits, axis=-1, keepdims=True)
    p = jnp.exp(logits - m)
    den = jnp.sum(p, axis=-1, keepdims=True)
    o = jnp.einsum('brnhqk,brnkhd->brnqhd', p, vb.astype(jnp.float32)) / jnp.swapaxes(den, 3, 4)
    lse = jnp.swapaxes((m + jnp.log(den))[..., 0], 3, 4)
    o = o.reshape(bsz, dilation, lp, h, hd)[:, :, :L].transpose(0, 2, 1, 3, 4).reshape(bsz, s, h, hd)
    lse = lse.reshape(bsz, dilation, lp, h)[:, :, :L].transpose(0, 2, 1, 3).reshape(bsz, s, h)
    return o, lse


def _spatial_gating(z, ln_g, ln_b, w_s, b_s):
    bsz, s, _ = z.shape
    u, v = jnp.split(z, 2, axis=-1)
    v = _layer_norm(v, ln_g, ln_b)
    vc = v.reshape(bsz, s // CHUNK, CHUNK, N_GROUPS_B, GROUP_B)
    w = jnp.tril(w_s)
    mixed = jnp.einsum('gts,bnsgc->bntgc', w, vc) + jnp.transpose(b_s)[:, :, None]
    return u * mixed.reshape(bsz, s, WIDTH_B)


def _ssm_combine(e1, e2):
    a1r, a1i, b1r, b1i = e1
    a2r, a2i, b2r, b2i = e2
    return (a2r * a1r - a2i * a1i,
            a2r * a1i + a2i * a1r,
            a2r * b1r - a2i * b1i + b2r,
            a2r * b1i + a2i * b1r + b2i)


def _s5(u, lam_re, lam_im, log_dt, b_re, b_im, c_re, c_im, d_skip):
    bsz, s, _ = u.shape
    f32 = jnp.float32
    uf = u.astype(f32)
    ug = uf.reshape(bsz, s, N_GROUPS_C, SSM_GROUP)
    lr = lam_re.astype(f32)
    li = lam_im.astype(f32)
    dt = jnp.exp(log_dt.astype(f32))[:, None]
    mag = jnp.exp(lr * dt)
    ab_re = mag * jnp.cos(li * dt)
    ab_im = mag * jnp.sin(li * dt)
    nrm = lr * lr + li * li
    cr = ((ab_re - 1.0) * lr + ab_im * li) / nrm
    ci = (ab_im * lr - (ab_re - 1.0) * li) / nrm
    bb_re = cr[..., None] * b_re - ci[..., None] * b_im
    bb_im = cr[..., None] * b_im + ci[..., None] * b_re
    bu_re = jnp.einsum('bsgh,gph->bsgp', ug, bb_re.astype(f32))
    bu_im = jnp.einsum('bsgh,gph->bsgp', ug, bb_im.astype(f32))
    a_re = jnp.broadcast_to(ab_re[None, None], (1, s, N_GROUPS_C, SSM_STATE))
    a_im = jnp.broadcast_to(ab_im[None, None], (1, s, N_GROUPS_C, SSM_STATE))
    _, _, xr, xi = lax.associative_scan(_ssm_combine, (a_re, a_im, bu_re, bu_im), axis=1)
    y = (jnp.einsum('bsgp,ghp->bsgh', xr, c_re.astype(f32))
         - jnp.einsum('bsgp,ghp->bsgh', xi, c_im.astype(f32)))
    return y.reshape(bsz, s, WIDTH_C) + d_skip.astype(f32) * uf


def _fwd_setup_inputs(seed: int = 0) -> dict:
    key = jax.random.key(seed)
    ks = iter(jax.random.split(key, 32))
    nrm = lambda shape, scale: jax.random.normal(next(ks), shape, jnp.float32) * scale
    gain = lambda shape: 1.0 + nrm(shape, 0.02)
    n_idx = jnp.arange(SSM_STATE, dtype=jnp.float32)
    return {
        'x': nrm((BATCH, SEQ, D_MODEL), 1.0),
        'w_in': nrm((DEPTH, D_MODEL, IN_COLS), D_MODEL ** -0.5),
        'b_in': nrm((DEPTH, IN_COLS), 0.02),
        'rel_bias': nrm((N_REL_BUCKETS, N_HEADS_A), 0.1),
        'sgu_ln_g': gain((DEPTH, WIDTH_B)),
        'sgu_ln_b': nrm((DEPTH, WIDTH_B), 0.02),
        'w_s': nrm((DEPTH, N_GROUPS_B, CHUNK, CHUNK), CHUNK ** -0.5),
        'b_s': gain((DEPTH, N_GROUPS_B, CHUNK)),
        'lam_re': -0.5 + nrm((DEPTH, N_GROUPS_C, SSM_STATE), 0.01),
        'lam_im': math.pi * n_idx + nrm((DEPTH, N_GROUPS_C, SSM_STATE), 0.01),
        'log_dt': jax.random.uniform(next(ks), (DEPTH, N_GROUPS_C), jnp.float32,
                                     minval=math.log(DT_MIN), maxval=math.log(DT_MAX)),
        'b_re': nrm((DEPTH, N_GROUPS_C, SSM_STATE, SSM_GROUP), (2 * SSM_GROUP) ** -0.5),
        'b_im': nrm((DEPTH, N_GROUPS_C, SSM_STATE, SSM_GROUP), (2 * SSM_GROUP) ** -0.5),
        'c_re': nrm((DEPTH, N_GROUPS_C, SSM_GROUP, SSM_STATE), SSM_STATE ** -0.5),
        'c_im': nrm((DEPTH, N_GROUPS_C, SSM_GROUP, SSM_STATE), SSM_STATE ** -0.5),
        'd_skip': nrm((DEPTH, WIDTH_C), 1.0),
        'w_glu': nrm((DEPTH, WIDTH_C, WIDTH_C), WIDTH_C ** -0.5),
        'b_glu': nrm((DEPTH, WIDTH_C), 0.02),
        'w_pa': nrm((DEPTH, WIDTH_A, D_MODEL), WIDTH_A ** -0.5),
        'w_pb': nrm((DEPTH, WIDTH_B, D_MODEL), WIDTH_B ** -0.5),
        'w_pc': nrm((DEPTH, WIDTH_C, D_MODEL), WIDTH_C ** -0.5),
        'w_o': nrm((DEPTH, D_MODEL, D_MODEL), BETA * D_MODEL ** -0.5),
        'ln1_g': gain((DEPTH, D_MODEL)),
        'ln1_b': nrm((DEPTH, D_MODEL), 0.02),
        'w_ffn_in': nrm((DEPTH, D_MODEL, 2 * D_FF), D_MODEL ** -0.5),
        'w_ffn_out': nrm((DEPTH, D_FF, D_MODEL), BETA * D_FF ** -0.5),
        'ln2_g': gain((DEPTH, D_MODEL)),
        'ln2_b': nrm((DEPTH, D_MODEL), 0.02),
    }


def _fwd_reference(x, w_in, b_in, rel_bias, sgu_ln_g, sgu_ln_b, w_s, b_s, lam_re, lam_im, log_dt,
              b_re, b_im, c_re, c_im, d_skip, w_glu, b_glu, w_pa, w_pb, w_pc, w_o,
              ln1_g, ln1_b, w_ffn_in, w_ffn_out, ln2_g, ln2_b):
    dt = x.dtype
    bsz, s, _ = x.shape
    offs = np.cumsum(IN_SPLIT)[:-1].tolist()
    group_bias = [_group_rel_bias(rel_bias, g, dil) for g, (_, dil) in enumerate(ATT_PATTERNS)]
    for l in range(DEPTH):
        proj = x @ w_in[l] + b_in[l]
        q, k, v, zb, uc, gl = jnp.split(proj, offs, axis=-1)
        q = q.reshape(bsz, s, N_HEADS_A, HEAD_DIM)
        k = k.reshape(bsz, s, N_HEADS_A, HEAD_DIM)
        v = v.reshape(bsz, s, N_HEADS_A, HEAD_DIM)
        outs, lses = [], []
        for g, (window, dil) in enumerate(ATT_PATTERNS):
            sl = slice(g * HEADS_PER_GROUP, (g + 1) * HEADS_PER_GROUP)
            o_g, lse_g = _dilated_window_attention(q[:, :, sl], k[:, :, sl], v[:, :, sl],
                                                   group_bias[g], dil, window // dil)
            outs.append(o_g)
            lses.append(lse_g)
        wts = jax.nn.softmax(jnp.stack(lses, axis=0), axis=0)
        ya = jnp.sum(wts[..., None] * jnp.stack(outs, axis=0), axis=0)
        ya = ya.reshape(bsz, s, WIDTH_A).astype(dt)
        yb = _spatial_gating(jax.nn.gelu(zb), sgu_ln_g[l], sgu_ln_b[l], w_s[l], b_s[l]).astype(dt)
        yc = jax.nn.gelu(_s5(uc, lam_re[l], lam_im[l], log_dt[l], b_re[l], b_im[l],
                             c_re[l], c_im[l], d_skip[l]))
        yc = (yc * jax.nn.sigmoid(yc @ w_glu[l] + b_glu[l])).astype(dt)
        gates = jax.nn.sigmoid(gl.reshape(bsz, s, N_BRANCH, D_MODEL))
        merged = (gates[:, :, 0] * (ya @ w_pa[l]) + gates[:, :, 1] * (yb @ w_pb[l])
                  + gates[:, :, 2] * (yc @ w_pc[l]))
        x = _layer_norm(ALPHA * x + merged @ w_o[l], ln1_g[l], ln1_b[l]).astype(dt)
        gate_f, up = jnp.split(x @ w_ffn_in[l], 2, axis=-1)
        f = (jax.nn.silu(gate_f) * up) @ w_ffn_out[l]
        x = _layer_norm(ALPHA * x + f, ln2_g[l], ln2_b[l]).astype(dt)
    return x


import jax as _jax
import jax.numpy as _jnp

TWIN_FORMAT = 'train_step'
FWD_PARAMS = ['x', 'w_in', 'b_in', 'rel_bias', 'sgu_ln_g', 'sgu_ln_b', 'w_s', 'b_s', 'lam_re', 'lam_im', 'log_dt', 'b_re', 'b_im', 'c_re', 'c_im', 'd_skip', 'w_glu', 'b_glu', 'w_pa', 'w_pb', 'w_pc', 'w_o', 'ln1_g', 'ln1_b', 'w_ffn_in', 'w_ffn_out', 'ln2_g', 'ln2_b']
TWIN_WEIGHTS = ['w_in', 'b_in', 'rel_bias', 'sgu_ln_g', 'sgu_ln_b', 'w_s', 'b_s', 'lam_re', 'lam_im', 'log_dt', 'b_re', 'b_im', 'c_re', 'c_im', 'd_skip', 'w_glu', 'b_glu', 'w_pa', 'w_pb', 'w_pc', 'w_o', 'ln1_g', 'ln1_b', 'w_ffn_in', 'w_ffn_out', 'ln2_g', 'ln2_b']
TWIN_DIFF_INPUT = 'x'
TWIN_INPUTS = ['x', 'w_in', 'b_in', 'rel_bias', 'sgu_ln_g', 'sgu_ln_b', 'w_s', 'b_s', 'lam_re', 'lam_im', 'log_dt', 'b_re', 'b_im', 'c_re', 'c_im', 'd_skip', 'w_glu', 'b_glu', 'w_pa', 'w_pb', 'w_pc', 'w_o', 'ln1_g', 'ln1_b', 'w_ffn_in', 'w_ffn_out', 'ln2_g', 'ln2_b', 'loss_target', 'm_w_in', 'm_b_in', 'm_rel_bias', 'm_sgu_ln_g', 'm_sgu_ln_b', 'm_w_s', 'm_b_s', 'm_lam_re', 'm_lam_im', 'm_log_dt', 'm_b_re', 'm_b_im', 'm_c_re', 'm_c_im', 'm_d_skip', 'm_w_glu', 'm_b_glu', 'm_w_pa', 'm_w_pb', 'm_w_pc', 'm_w_o', 'm_ln1_g', 'm_ln1_b', 'm_w_ffn_in', 'm_w_ffn_out', 'm_ln2_g', 'm_ln2_b', 'v_w_in', 'v_b_in', 'v_rel_bias', 'v_sgu_ln_g', 'v_sgu_ln_b', 'v_w_s', 'v_b_s', 'v_lam_re', 'v_lam_im', 'v_log_dt', 'v_b_re', 'v_b_im', 'v_c_re', 'v_c_im', 'v_d_skip', 'v_w_glu', 'v_b_glu', 'v_w_pa', 'v_w_pb', 'v_w_pc', 'v_w_o', 'v_ln1_g', 'v_ln1_b', 'v_w_ffn_in', 'v_w_ffn_out', 'v_ln2_g', 'v_ln2_b']
TWIN_OUTPUTS = ['loss', 'grad_x', 'grad_w_in', 'grad_b_in', 'grad_rel_bias', 'grad_sgu_ln_g', 'grad_sgu_ln_b', 'grad_w_s', 'grad_b_s', 'grad_lam_re', 'grad_lam_im', 'grad_log_dt', 'grad_b_re', 'grad_b_im', 'grad_c_re', 'grad_c_im', 'grad_d_skip', 'grad_w_glu', 'grad_b_glu', 'grad_w_pa', 'grad_w_pb', 'grad_w_pc', 'grad_w_o', 'grad_ln1_g', 'grad_ln1_b', 'grad_w_ffn_in', 'grad_w_ffn_out', 'grad_ln2_g', 'grad_ln2_b', 'delta_w_in', 'delta_b_in', 'delta_rel_bias', 'delta_sgu_ln_g', 'delta_sgu_ln_b', 'delta_w_s', 'delta_b_s', 'delta_lam_re', 'delta_lam_im', 'delta_log_dt', 'delta_b_re', 'delta_b_im', 'delta_c_re', 'delta_c_im', 'delta_d_skip', 'delta_w_glu', 'delta_b_glu', 'delta_w_pa', 'delta_w_pb', 'delta_w_pc', 'delta_w_o', 'delta_ln1_g', 'delta_ln1_b', 'delta_w_ffn_in', 'delta_w_ffn_out', 'delta_ln2_g', 'delta_ln2_b', 'new_m_w_in', 'new_m_b_in', 'new_m_rel_bias', 'new_m_sgu_ln_g', 'new_m_sgu_ln_b', 'new_m_w_s', 'new_m_b_s', 'new_m_lam_re', 'new_m_lam_im', 'new_m_log_dt', 'new_m_b_re', 'new_m_b_im', 'new_m_c_re', 'new_m_c_im', 'new_m_d_skip', 'new_m_w_glu', 'new_m_b_glu', 'new_m_w_pa', 'new_m_w_pb', 'new_m_w_pc', 'new_m_w_o', 'new_m_ln1_g', 'new_m_ln1_b', 'new_m_w_ffn_in', 'new_m_w_ffn_out', 'new_m_ln2_g', 'new_m_ln2_b', 'new_v_w_in', 'new_v_b_in', 'new_v_rel_bias', 'new_v_sgu_ln_g', 'new_v_sgu_ln_b', 'new_v_w_s', 'new_v_b_s', 'new_v_lam_re', 'new_v_lam_im', 'new_v_log_dt', 'new_v_b_re', 'new_v_b_im', 'new_v_c_re', 'new_v_c_im', 'new_v_d_skip', 'new_v_w_glu', 'new_v_b_glu', 'new_v_w_pa', 'new_v_w_pb', 'new_v_w_pc', 'new_v_w_o', 'new_v_ln1_g', 'new_v_ln1_b', 'new_v_w_ffn_in', 'new_v_w_ffn_out', 'new_v_ln2_g', 'new_v_ln2_b']
TWIN_LEAF_KINDS = {'loss': 'loss', 'grad_x': 'grad_x', 'grad_w_in': 'grad_w', 'grad_b_in': 'grad_w', 'grad_rel_bias': 'grad_w', 'grad_sgu_ln_g': 'grad_w', 'grad_sgu_ln_b': 'grad_w', 'grad_w_s': 'grad_w', 'grad_b_s': 'grad_w', 'grad_lam_re': 'grad_w', 'grad_lam_im': 'grad_w', 'grad_log_dt': 'grad_w', 'grad_b_re': 'grad_w', 'grad_b_im': 'grad_w', 'grad_c_re': 'grad_w', 'grad_c_im': 'grad_w', 'grad_d_skip': 'grad_w', 'grad_w_glu': 'grad_w', 'grad_b_glu': 'grad_w', 'grad_w_pa': 'grad_w', 'grad_w_pb': 'grad_w', 'grad_w_pc': 'grad_w', 'grad_w_o': 'grad_w', 'grad_ln1_g': 'grad_w', 'grad_ln1_b': 'grad_w', 'grad_w_ffn_in': 'grad_w', 'grad_w_ffn_out': 'grad_w', 'grad_ln2_g': 'grad_w', 'grad_ln2_b': 'grad_w', 'delta_w_in': 'delta_w', 'delta_b_in': 'delta_w', 'delta_rel_bias': 'delta_w', 'delta_sgu_ln_g': 'delta_w', 'delta_sgu_ln_b': 'delta_w', 'delta_w_s': 'delta_w', 'delta_b_s': 'delta_w', 'delta_lam_re': 'delta_w', 'delta_lam_im': 'delta_w', 'delta_log_dt': 'delta_w', 'delta_b_re': 'delta_w', 'delta_b_im': 'delta_w', 'delta_c_re': 'delta_w', 'delta_c_im': 'delta_w', 'delta_d_skip': 'delta_w', 'delta_w_glu': 'delta_w', 'delta_b_glu': 'delta_w', 'delta_w_pa': 'delta_w', 'delta_w_pb': 'delta_w', 'delta_w_pc': 'delta_w', 'delta_w_o': 'delta_w', 'delta_ln1_g': 'delta_w', 'delta_ln1_b': 'delta_w', 'delta_w_ffn_in': 'delta_w', 'delta_w_ffn_out': 'delta_w', 'delta_ln2_g': 'delta_w', 'delta_ln2_b': 'delta_w', 'new_m_w_in': 'new_m', 'new_m_b_in': 'new_m', 'new_m_rel_bias': 'new_m', 'new_m_sgu_ln_g': 'new_m', 'new_m_sgu_ln_b': 'new_m', 'new_m_w_s': 'new_m', 'new_m_b_s': 'new_m', 'new_m_lam_re': 'new_m', 'new_m_lam_im': 'new_m', 'new_m_log_dt': 'new_m', 'new_m_b_re': 'new_m', 'new_m_b_im': 'new_m', 'new_m_c_re': 'new_m', 'new_m_c_im': 'new_m', 'new_m_d_skip': 'new_m', 'new_m_w_glu': 'new_m', 'new_m_b_glu': 'new_m', 'new_m_w_pa': 'new_m', 'new_m_w_pb': 'new_m', 'new_m_w_pc': 'new_m', 'new_m_w_o': 'new_m', 'new_m_ln1_g': 'new_m', 'new_m_ln1_b': 'new_m', 'new_m_w_ffn_in': 'new_m', 'new_m_w_ffn_out': 'new_m', 'new_m_ln2_g': 'new_m', 'new_m_ln2_b': 'new_m', 'new_v_w_in': 'new_v', 'new_v_b_in': 'new_v', 'new_v_rel_bias': 'new_v', 'new_v_sgu_ln_g': 'new_v', 'new_v_sgu_ln_b': 'new_v', 'new_v_w_s': 'new_v', 'new_v_b_s': 'new_v', 'new_v_lam_re': 'new_v', 'new_v_lam_im': 'new_v', 'new_v_log_dt': 'new_v', 'new_v_b_re': 'new_v', 'new_v_b_im': 'new_v', 'new_v_c_re': 'new_v', 'new_v_c_im': 'new_v', 'new_v_d_skip': 'new_v', 'new_v_w_glu': 'new_v', 'new_v_b_glu': 'new_v', 'new_v_w_pa': 'new_v', 'new_v_w_pb': 'new_v', 'new_v_w_pc': 'new_v', 'new_v_w_o': 'new_v', 'new_v_ln1_g': 'new_v', 'new_v_ln1_b': 'new_v', 'new_v_w_ffn_in': 'new_v', 'new_v_w_ffn_out': 'new_v', 'new_v_ln2_g': 'new_v', 'new_v_ln2_b': 'new_v'}


def _forward(args):
    return _fwd_reference(*[args[k] for k in FWD_PARAMS])


def _output_shape():
    out = _jax.eval_shape(lambda: _forward(_fwd_setup_inputs(0)))
    return out.shape, out.dtype

N_MICROBATCH = 1
ADAM_LR = 0.001
ADAM_B1 = 0.9
ADAM_B2 = 0.999
ADAM_EPS = 1e-08
ADAM_WD = 0.01
ADAM_STEP = 10
PER_EXAMPLE_BATCH_AXIS = {'x': 0, 'loss_target': 0}
SHARED_INPUTS = []
_WEIGHT_DTYPES = {'w_in': _jnp.float32, 'b_in': _jnp.float32, 'rel_bias': _jnp.float32, 'sgu_ln_g': _jnp.float32, 'sgu_ln_b': _jnp.float32, 'w_s': _jnp.float32, 'b_s': _jnp.float32, 'lam_re': _jnp.float32, 'lam_im': _jnp.float32, 'log_dt': _jnp.float32, 'b_re': _jnp.float32, 'b_im': _jnp.float32, 'c_re': _jnp.float32, 'c_im': _jnp.float32, 'd_skip': _jnp.float32, 'w_glu': _jnp.float32, 'b_glu': _jnp.float32, 'w_pa': _jnp.float32, 'w_pb': _jnp.float32, 'w_pc': _jnp.float32, 'w_o': _jnp.float32, 'ln1_g': _jnp.float32, 'ln1_b': _jnp.float32, 'w_ffn_in': _jnp.float32, 'w_ffn_out': _jnp.float32, 'ln2_g': _jnp.float32, 'ln2_b': _jnp.float32}
MOMENT_SCALE = {'w_in': 5.615085e-03, 'b_in': 8.983490e-03, 'rel_bias': 6.362627e-03, 'sgu_ln_g': 9.500288e-03, 'sgu_ln_b': 9.407287e-03, 'w_s': 9.308789e-03, 'b_s': 1.309730e-02, 'lam_re': 5.399354e-04, 'lam_im': 5.506996e-04, 'log_dt': 4.102029e-01, 'b_re': 3.631379e-04, 'b_im': 3.566253e-04, 'c_re': 5.024461e-04, 'c_im': 5.016430e-04, 'd_skip': 1.007611e-02, 'w_glu': 2.355066e-03, 'b_glu': 4.344060e-03, 'w_pa': 2.438084e-03, 'w_pb': 1.138789e-02, 'w_pc': 5.597535e-03, 'w_o': 3.000016e-02, 'ln1_g': 4.945544e-01, 'ln1_b': 2.243245e-01, 'w_ffn_in': 8.194105e-03, 'w_ffn_out': 3.184742e-02, 'ln2_g': 8.037583e+00, 'ln2_b': 7.376724e-01}


def _to_microbatches(a, axis):
    t = _jnp.moveaxis(a, axis, 0)
    t = t.reshape((N_MICROBATCH, t.shape[0] // N_MICROBATCH) + t.shape[1:])
    return _jnp.moveaxis(t, 1, axis + 1)


def setup_inputs(seed: int = 0) -> dict:
    inp = _fwd_setup_inputs(seed)
    key = _jax.random.fold_in(_jax.random.key(seed), 7919)
    shape, _ = _output_shape()
    out = dict(inp)
    out["loss_target"] = _jax.random.normal(_jax.random.fold_in(key, 0), shape, _jnp.float32)
    for i, name in enumerate(TWIN_WEIGHTS):
        w = inp[name].astype(_jnp.float32)
        if MOMENT_SCALE is None:
            s = _jnp.sqrt(_jnp.mean(_jnp.square(w)) + 1e-30)
        else:
            s = MOMENT_SCALE[name]
        km, kv = _jax.random.split(_jax.random.fold_in(key, i + 1))
        out[name] = w
        out["m_" + name] = s * _jax.random.normal(km, w.shape, _jnp.float32)
        out["v_" + name] = (s * s) * _jax.random.uniform(kv, w.shape, _jnp.float32, 0.5, 1.5)
    if N_MICROBATCH > 1:
        for name, axis in PER_EXAMPLE_BATCH_AXIS.items():
            out[name] = _to_microbatches(out[name], axis)
    return {'x': out['x'], 'w_in': out['w_in'], 'b_in': out['b_in'], 'rel_bias': out['rel_bias'], 'sgu_ln_g': out['sgu_ln_g'], 'sgu_ln_b': out['sgu_ln_b'], 'w_s': out['w_s'], 'b_s': out['b_s'], 'lam_re': out['lam_re'], 'lam_im': out['lam_im'], 'log_dt': out['log_dt'], 'b_re': out['b_re'], 'b_im': out['b_im'], 'c_re': out['c_re'], 'c_im': out['c_im'], 'd_skip': out['d_skip'], 'w_glu': out['w_glu'], 'b_glu': out['b_glu'], 'w_pa': out['w_pa'], 'w_pb': out['w_pb'], 'w_pc': out['w_pc'], 'w_o': out['w_o'], 'ln1_g': out['ln1_g'], 'ln1_b': out['ln1_b'], 'w_ffn_in': out['w_ffn_in'], 'w_ffn_out': out['w_ffn_out'], 'ln2_g': out['ln2_g'], 'ln2_b': out['ln2_b'], 'loss_target': out['loss_target'], 'm_w_in': out['m_w_in'], 'm_b_in': out['m_b_in'], 'm_rel_bias': out['m_rel_bias'], 'm_sgu_ln_g': out['m_sgu_ln_g'], 'm_sgu_ln_b': out['m_sgu_ln_b'], 'm_w_s': out['m_w_s'], 'm_b_s': out['m_b_s'], 'm_lam_re': out['m_lam_re'], 'm_lam_im': out['m_lam_im'], 'm_log_dt': out['m_log_dt'], 'm_b_re': out['m_b_re'], 'm_b_im': out['m_b_im'], 'm_c_re': out['m_c_re'], 'm_c_im': out['m_c_im'], 'm_d_skip': out['m_d_skip'], 'm_w_glu': out['m_w_glu'], 'm_b_glu': out['m_b_glu'], 'm_w_pa': out['m_w_pa'], 'm_w_pb': out['m_w_pb'], 'm_w_pc': out['m_w_pc'], 'm_w_o': out['m_w_o'], 'm_ln1_g': out['m_ln1_g'], 'm_ln1_b': out['m_ln1_b'], 'm_w_ffn_in': out['m_w_ffn_in'], 'm_w_ffn_out': out['m_w_ffn_out'], 'm_ln2_g': out['m_ln2_g'], 'm_ln2_b': out['m_ln2_b'], 'v_w_in': out['v_w_in'], 'v_b_in': out['v_b_in'], 'v_rel_bias': out['v_rel_bias'], 'v_sgu_ln_g': out['v_sgu_ln_g'], 'v_sgu_ln_b': out['v_sgu_ln_b'], 'v_w_s': out['v_w_s'], 'v_b_s': out['v_b_s'], 'v_lam_re': out['v_lam_re'], 'v_lam_im': out['v_lam_im'], 'v_log_dt': out['v_log_dt'], 'v_b_re': out['v_b_re'], 'v_b_im': out['v_b_im'], 'v_c_re': out['v_c_re'], 'v_c_im': out['v_c_im'], 'v_d_skip': out['v_d_skip'], 'v_w_glu': out['v_w_glu'], 'v_b_glu': out['v_b_glu'], 'v_w_pa': out['v_w_pa'], 'v_w_pb': out['v_w_pb'], 'v_w_pc': out['v_w_pc'], 'v_w_o': out['v_w_o'], 'v_ln1_g': out['v_ln1_g'], 'v_ln1_b': out['v_ln1_b'], 'v_w_ffn_in': out['v_w_ffn_in'], 'v_w_ffn_out': out['v_w_ffn_out'], 'v_ln2_g': out['v_ln2_g'], 'v_ln2_b': out['v_ln2_b']}


def _loss(weights, diff, rest, loss_target):
    with _jax.named_scope("forward"):
        args = {**rest, TWIN_DIFF_INPUT: diff, **{k: w.astype(_WEIGHT_DTYPES[k]) for k, w in weights.items()}}
        y = _forward(args)
    with _jax.named_scope("loss_head"):
        err = _jnp.square(y.astype(_jnp.float32) - loss_target)
        return 0.5 * _jnp.sum(_jnp.mean(err, axis=-1)) if err.ndim else 0.5 * err


def _adamw(w, g, m, v):
    m = ADAM_B1 * m + (1.0 - ADAM_B1) * g
    v = ADAM_B2 * v + (1.0 - ADAM_B2) * _jnp.square(g)
    m_hat = m / (1.0 - ADAM_B1 ** ADAM_STEP)
    v_hat = v / (1.0 - ADAM_B2 ** ADAM_STEP)
    delta = -ADAM_LR * (m_hat / (_jnp.sqrt(v_hat) + ADAM_EPS) + ADAM_WD * w)
    return delta, m, v


def reference(x, w_in, b_in, rel_bias, sgu_ln_g, sgu_ln_b, w_s, b_s, lam_re, lam_im, log_dt, b_re, b_im, c_re, c_im, d_skip, w_glu, b_glu, w_pa, w_pb, w_pc, w_o, ln1_g, ln1_b, w_ffn_in, w_ffn_out, ln2_g, ln2_b, loss_target, m_w_in, m_b_in, m_rel_bias, m_sgu_ln_g, m_sgu_ln_b, m_w_s, m_b_s, m_lam_re, m_lam_im, m_log_dt, m_b_re, m_b_im, m_c_re, m_c_im, m_d_skip, m_w_glu, m_b_glu, m_w_pa, m_w_pb, m_w_pc, m_w_o, m_ln1_g, m_ln1_b, m_w_ffn_in, m_w_ffn_out, m_ln2_g, m_ln2_b, v_w_in, v_b_in, v_rel_bias, v_sgu_ln_g, v_sgu_ln_b, v_w_s, v_b_s, v_lam_re, v_lam_im, v_log_dt, v_b_re, v_b_im, v_c_re, v_c_im, v_d_skip, v_w_glu, v_b_glu, v_w_pa, v_w_pb, v_w_pc, v_w_o, v_ln1_g, v_ln1_b, v_w_ffn_in, v_w_ffn_out, v_ln2_g, v_ln2_b):
    given = dict(x=x, w_in=w_in, b_in=b_in, rel_bias=rel_bias, sgu_ln_g=sgu_ln_g, sgu_ln_b=sgu_ln_b, w_s=w_s, b_s=b_s, lam_re=lam_re, lam_im=lam_im, log_dt=log_dt, b_re=b_re, b_im=b_im, c_re=c_re, c_im=c_im, d_skip=d_skip, w_glu=w_glu, b_glu=b_glu, w_pa=w_pa, w_pb=w_pb, w_pc=w_pc, w_o=w_o, ln1_g=ln1_g, ln1_b=ln1_b, w_ffn_in=w_ffn_in, w_ffn_out=w_ffn_out, ln2_g=ln2_g, ln2_b=ln2_b, loss_target=loss_target, m_w_in=m_w_in, m_b_in=m_b_in, m_rel_bias=m_rel_bias, m_sgu_ln_g=m_sgu_ln_g, m_sgu_ln_b=m_sgu_ln_b, m_w_s=m_w_s, m_b_s=m_b_s, m_lam_re=m_lam_re, m_lam_im=m_lam_im, m_log_dt=m_log_dt, m_b_re=m_b_re, m_b_im=m_b_im, m_c_re=m_c_re, m_c_im=m_c_im, m_d_skip=m_d_skip, m_w_glu=m_w_glu, m_b_glu=m_b_glu, m_w_pa=m_w_pa, m_w_pb=m_w_pb, m_w_pc=m_w_pc, m_w_o=m_w_o, m_ln1_g=m_ln1_g, m_ln1_b=m_ln1_b, m_w_ffn_in=m_w_ffn_in, m_w_ffn_out=m_w_ffn_out, m_ln2_g=m_ln2_g, m_ln2_b=m_ln2_b, v_w_in=v_w_in, v_b_in=v_b_in, v_rel_bias=v_rel_bias, v_sgu_ln_g=v_sgu_ln_g, v_sgu_ln_b=v_sgu_ln_b, v_w_s=v_w_s, v_b_s=v_b_s, v_lam_re=v_lam_re, v_lam_im=v_lam_im, v_log_dt=v_log_dt, v_b_re=v_b_re, v_b_im=v_b_im, v_c_re=v_c_re, v_c_im=v_c_im, v_d_skip=v_d_skip, v_w_glu=v_w_glu, v_b_glu=v_b_glu, v_w_pa=v_w_pa, v_w_pb=v_w_pb, v_w_pc=v_w_pc, v_w_o=v_w_o, v_ln1_g=v_ln1_g, v_ln1_b=v_ln1_b, v_w_ffn_in=v_w_ffn_in, v_w_ffn_out=v_w_ffn_out, v_ln2_g=v_ln2_g, v_ln2_b=v_ln2_b)
    weights = {n: given[n] for n in TWIN_WEIGHTS}
    shared = {n: given[n] for n in SHARED_INPUTS}
    per_example = {n: given[n] for n in ['x']}
    grad_fn = _jax.value_and_grad(_loss, argnums=(0, 1))

    def one_microbatch(ex, loss_target):
        ex = dict(ex)
        diff = ex.pop(TWIN_DIFF_INPUT)
        return grad_fn(weights, diff, {**shared, **ex}, loss_target)

    if N_MICROBATCH == 1:
        loss, (grad_w, grad_x) = one_microbatch(per_example, given["loss_target"])
    else:
        def body(carry, xs):
            loss_sum, grad_sum = carry
            l_k, (gw_k, gx_k) = one_microbatch(xs[0], xs[1])
            with _jax.named_scope("update"):
                return (loss_sum + l_k, _jax.tree.map(_jnp.add, grad_sum, gw_k)), gx_k

        init = (_jnp.zeros((), _jnp.float32), _jax.tree.map(_jnp.zeros_like, weights))
        (loss, grad_w), grad_x = _jax.lax.scan(body, init, (per_example, given["loss_target"]))
    with _jax.named_scope("update"):
        delta_w, new_m, new_v = {}, {}, {}
        for n in TWIN_WEIGHTS:
            delta_w[n], new_m[n], new_v[n] = _adamw(weights[n], grad_w[n], given["m_" + n], given["v_" + n])
    return (loss, grad_x, *[grad_w[n] for n in TWIN_WEIGHTS], *[delta_w[n] for n in TWIN_WEIGHTS],
            *[new_m[n] for n in TWIN_WEIGHTS], *[new_v[n] for n in TWIN_WEIGHTS])
```

```python
import functools
import math

import numpy as np
import jax
import jax.numpy as jnp
from jax import lax
from jax.experimental import pallas as pl
from jax.experimental.pallas import tpu as pltpu

F32 = jnp.float32
BF16 = jnp.bfloat16

MESH_AXES = ("x", "y", "c")
N_DEV = 8
DEPTH = 4

ATT_DILATIONS = (1, 4, 16)
ATT_STEPS = 128
HEADS_PER_GROUP = 8
HEAD_DIM = 64
QKV_WIDTH = 1536
WIDTH_A = HEADS_PER_GROUP * HEAD_DIM
ATT_BLOCK = 128
N_REL_BUCKETS = 32
REL_MAX_DIST = 2048
NEG_INF = -1e30
CHUNK = 128
WIDTH_B = 768
N_GROUPS_B = 6
WIDTH_C = 768
SSM_GROUP = 16
N_GROUPS_C = 48
SSM_STATE = 64
SSM_PACK = 8
N_SSM_BLOCKS = N_GROUPS_C // SSM_PACK
SSM_COLS = N_GROUPS_C * SSM_STATE
ALPHA = (2 * DEPTH) ** 0.25

ADAM_LR = 0.001
ADAM_B1 = 0.9
ADAM_B2 = 0.999
ADAM_EPS = 1e-08
ADAM_WD = 0.01
ADAM_STEP = 10

LANES = 128
SUBLANES = 8
VMEM_LIMIT = 48 * 1024 * 1024

SHARDED = ("w_in", "w_glu", "w_pa", "w_pb", "w_pc", "w_o", "w_ffn_in", "w_ffn_out")
COL_SHARDED = ("w_in", "w_pa", "w_pb", "w_pc", "w_ffn_in")
SMALL = ("b_in", "rel_bias", "sgu_ln_g", "sgu_ln_b", "w_s", "b_s", "lam_re", "lam_im", "log_dt",
         "b_re", "b_im", "c_re", "c_im", "d_skip", "b_glu", "ln1_g", "ln1_b", "ln2_g", "ln2_b")
WEIGHTS = ("w_in", "b_in", "rel_bias", "sgu_ln_g", "sgu_ln_b", "w_s", "b_s", "lam_re", "lam_im",
           "log_dt", "b_re", "b_im", "c_re", "c_im", "d_skip", "w_glu", "b_glu", "w_pa", "w_pb",
           "w_pc", "w_o", "ln1_g", "ln1_b", "w_ffn_in", "w_ffn_out", "ln2_g", "ln2_b")


def _pick(dim, target, mult):
    best = None
    for t in range(mult, min(dim, target) + 1, mult):
        if dim % t == 0:
            best = t
    return dim if best is None else best


def _cparams(*sem):
    return pltpu.CompilerParams(dimension_semantics=sem, vmem_limit_bytes=VMEM_LIMIT)


def _zero_map(ndim):
    return lambda *_: (0,) * ndim


def _rowmap(fn, rows, consts=(), stacks=(), row_outs=(), red_outs=(), tr=256, name=None):
    t = rows[0].shape[0]
    tr = _pick(t, tr, SUBLANES)
    n_r, n_c, n_s, n_o = len(rows), len(consts), len(stacks), len(row_outs)

    def body(*refs):
        ins = [r[...] for r in refs[:n_r + n_c + n_s]]
        outs = refs[n_r + n_c + n_s:n_r + n_c + n_s + n_o]
        reds = refs[n_r + n_c + n_s + n_o:]
        res = fn(*ins)
        if not isinstance(res, (tuple, list)):
            res = (res,)
        for o, v in zip(outs, res[:n_o]):
            o[...] = v.astype(o.dtype)
        if reds:
            @pl.when(pl.program_id(0) == 0)
            def _():
                for r in reds:
                    r[...] = jnp.zeros_like(r)
            for r, v in zip(reds, res[n_o:]):
                r[...] += v

    in_specs = [pl.BlockSpec((tr, r.shape[1]), lambda i: (i, 0)) for r in rows]
    in_specs += [pl.BlockSpec(c.shape, _zero_map(c.ndim)) for c in consts]
    in_specs += [pl.BlockSpec((s.shape[0], tr, s.shape[2]), lambda i: (0, i, 0)) for s in stacks]
    out_specs = [pl.BlockSpec((tr, w), lambda i: (i, 0)) for w, _ in row_outs]
    out_specs += [pl.BlockSpec(s, _zero_map(len(s))) for s in red_outs]
    out_shape = [jax.ShapeDtypeStruct((t, w), dt) for w, dt in row_outs]
    out_shape += [jax.ShapeDtypeStruct(s, F32) for s in red_outs]
    return pl.pallas_call(body, grid=(t // tr,), in_specs=in_specs, out_specs=out_specs, out_shape=out_shape,
                          compiler_params=_cparams("arbitrary"), name=name)(*rows, *consts, *stacks)


def _mm(a, b, *, ta=False, tb=False, bias=None, add=None, out_dtype=F32, tm=1024, tn=768, tk=2048, name=None):
    k, m = a.shape if ta else a.shape[::-1]
    n = b.shape[0] if tb else b.shape[1]
    assert (b.shape[1] if tb else b.shape[0]) == k
    tm = _pick(m, tm, LANES if ta else SUBLANES)
    tn = _pick(n, tn, LANES)
    tk = _pick(k, tk, LANES)
    nk = k // tk
    dims = (((0 if ta else 1,), (1 if tb else 0,)), ((), ()))

    def body(*refs):
        a_ref, b_ref = refs[0], refs[1]
        rest = list(refs[2:])
        bias_ref = rest.pop(0) if bias is not None else None
        add_ref = rest.pop(0) if add is not None else None
        o_ref, acc_ref = rest
        kk = pl.program_id(2)

        @pl.when(kk == 0)
        def _():
            acc_ref[...] = jnp.zeros_like(acc_ref)

        acc_ref[...] += lax.dot_general(a_ref[...].astype(BF16), b_ref[...].astype(BF16), dims,
                                        preferred_element_type=F32)

        @pl.when(kk == nk - 1)
        def _():
            r = acc_ref[...]
            if bias_ref is not None:
                r = r + bias_ref[...]
            if add_ref is not None:
                r = r + add_ref[...]
            o_ref[...] = r.astype(o_ref.dtype)

    a_spec = pl.BlockSpec((tk, tm), lambda i, j, q: (q, i)) if ta else pl.BlockSpec((tm, tk), lambda i, j, q: (i, q))
    b_spec = pl.BlockSpec((tn, tk), lambda i, j, q: (j, q)) if tb else pl.BlockSpec((tk, tn), lambda i, j, q: (q, j))
    in_specs, args = [a_spec, b_spec], [a, b]
    if bias is not None:
        in_specs.append(pl.BlockSpec((1, tn), lambda i, j, q: (0, j)))
        args.append(bias)
    if add is not None:
        in_specs.append(pl.BlockSpec((tm, tn), lambda i, j, q: (i, j)))
        args.append(add)
    return pl.pallas_call(
        body, grid=(m // tm, n // tn, nk), in_specs=in_specs,
        out_specs=pl.BlockSpec((tm, tn), lambda i, j, q: (i, j)),
        out_shape=jax.ShapeDtypeStruct((m, n), out_dtype),
        scratch_shapes=[pltpu.VMEM((tm, tn), F32)],
        compiler_params=_cparams("parallel", "parallel", "arbitrary"), name=name)(*args)


def _bdmm(a, bm, *, tb=False, scale=1.0, add=None, name=None):
    t = a.shape[0]
    nj, ka, kb = bm.shape
    kin, kout = (kb, ka) if tb else (ka, kb)
    tm = _pick(t, 512, SUBLANES)
    dims = (((1,), (1 if tb else 0,)), ((), ()))

    def body(*refs):
        a_ref, b_ref = refs[0], refs[1]
        add_ref = refs[2] if add is not None else None
        o_ref = refs[-1]
        r = lax.dot_general(a_ref[...].astype(BF16), b_ref[0].astype(BF16), dims, preferred_element_type=F32)
        if scale != 1.0:
            r = r * scale
        if add_ref is not None:
            r = r + add_ref[...]
        o_ref[...] = r

    in_specs = [pl.BlockSpec((tm, kin), lambda i, j: (i, j)), pl.BlockSpec((1, ka, kb), lambda i, j: (j, 0, 0))]
    args = [a, bm]
    if add is not None:
        in_specs.append(pl.BlockSpec((tm, kout), lambda i, j: (i, j)))
        args.append(add)
    return pl.pallas_call(
        body, grid=(t // tm, nj), in_specs=in_specs, out_specs=pl.BlockSpec((tm, kout), lambda i, j: (i, j)),
        out_shape=jax.ShapeDtypeStruct((t, nj * kout), F32),
        compiler_params=_cparams("parallel", "parallel"), name=name)(*args)


def _bdmm_tn(a, b, nj, *, scale=1.0, name=None):
    t = a.shape[0]
    ka, kb = a.shape[1] // nj, b.shape[1] // nj
    tm = _pick(t, 512, LANES)
    nt = t // tm

    def body(a_ref, b_ref, o_ref):
        @pl.when(pl.program_id(1) == 0)
        def _():
            o_ref[...] = jnp.zeros_like(o_ref)

        r = lax.dot_general(a_ref[...].astype(BF16), b_ref[...].astype(BF16), (((0,), (0,)), ((), ())),
                            preferred_element_type=F32)
        o_ref[0] += r if scale == 1.0 else r * scale

    return pl.pallas_call(
        body, grid=(nj, nt),
        in_specs=[pl.BlockSpec((tm, ka), lambda j, i: (i, j)), pl.BlockSpec((tm, kb), lambda j, i: (i, j))],
        out_specs=pl.BlockSpec((1, ka, kb), lambda j, i: (j, 0, 0)),
        out_shape=jax.ShapeDtypeStruct((nj, ka, kb), F32),
        compiler_params=_cparams("parallel", "arbitrary"), name=name)(a, b)


def _ln(x, g, b, eps=1e-5):
    mu = jnp.mean(x, axis=-1, keepdims=True)
    var = jnp.mean(jnp.square(x - mu), axis=-1, keepdims=True)
    return (x - mu) * lax.rsqrt(var + eps) * g + b


def _post_norm(x, f, g, b):
    return _ln(ALPHA * x + f, g, b)


def _post_norm_bwd(x, f, dy, g, b):
    _, vjp = jax.vjp(_post_norm, x, f, g, b)
    return vjp(dy)


def _merge3(g0, g1, g2, pa, pb, pc):
    return jax.nn.sigmoid(g0) * pa + jax.nn.sigmoid(g1) * pb + jax.nn.sigmoid(g2) * pc


def _merge(gl, pa, pb, pc):
    d = pa.shape[1]
    return _merge3(gl[:, :d], gl[:, d:2 * d], gl[:, 2 * d:], pa, pb, pc)


def _merge_bwd(gl, pa, pb, pc, dm):
    d = pa.shape[1]
    _, vjp = jax.vjp(_merge3, gl[:, :d], gl[:, d:2 * d], gl[:, 2 * d:], pa, pb, pc)
    d0, d1, d2, dpa, dpb, dpc = vjp(dm)
    dgl = jnp.concatenate([d0, d1, d2], axis=1)
    return dgl, dpa, dpb, dpc, jnp.sum(dgl, axis=0, keepdims=True)


def _swiglu2(gate, up):
    return jax.nn.silu(gate) * up


def _swiglu(gu):
    h = gu.shape[1] // 2
    return _swiglu2(gu[:, :h], gu[:, h:])


def _swiglu_bwd(gu, dact):
    h = gu.shape[1] // 2
    _, vjp = jax.vjp(_swiglu2, gu[:, :h], gu[:, h:])
    dg, du = vjp(dact)
    return jnp.concatenate([dg, du], axis=1)


def _glu(ycp, lin):
    return ycp * jax.nn.sigmoid(lin)


def _glu_bwd(ycp, lin, dyc):
    _, vjp = jax.vjp(_glu, ycp, lin)
    dycp, dlin = vjp(dyc)
    return dycp, dlin, jnp.sum(dlin, axis=0, keepdims=True)


def _s5_out(yre, yim, uc, dskip):
    ys = yre + yim + dskip * uc
    return ys, jax.nn.gelu(ys)


def _s5_out_bwd(ys, uc, dycp, dskip):
    _, vjp = jax.vjp(jax.nn.gelu, ys)
    dys = vjp(dycp)[0]
    return dys, dys * dskip, jnp.sum(dys * uc, axis=0, keepdims=True)


def _combine(o0, l0, o1, l1, o2, l2):
    m = jnp.maximum(jnp.maximum(l0, l1), l2)
    e0, e1, e2 = jnp.exp(l0 - m), jnp.exp(l1 - m), jnp.exp(l2 - m)
    s = e0 + e1 + e2
    return (e0 / s) * o0 + (e1 / s) * o1 + (e2 / s) * o2


def _combine_bwd(o0, l0, o1, l1, o2, l2, ya, dya, head_ones):
    m = jnp.maximum(jnp.maximum(l0, l1), l2)
    e0, e1, e2 = jnp.exp(l0 - m), jnp.exp(l1 - m), jnp.exp(l2 - m)
    s = e0 + e1 + e2
    dot_ya = jnp.dot(dya * ya, head_ones, precision=lax.Precision.HIGHEST, preferred_element_type=F32)
    w0, w1, w2 = e0 / s, e1 / s, e2 / s
    return w0 * dya, w1 * dya, w2 * dya, -w0 * dot_ya, -w1 * dot_ya, -w2 * dot_ya


def _loss_fn(y, tgt):
    err = y - tgt
    part = jnp.sum(jnp.sum(jnp.square(err), axis=1, keepdims=True), axis=0, keepdims=True) * (0.5 / y.shape[1])
    return err * (1.0 / y.shape[1]), jnp.broadcast_to(part, (1, LANES))


def _adamw(w, g, m, v):
    m = ADAM_B1 * m + (1.0 - ADAM_B1) * g
    v = ADAM_B2 * v + (1.0 - ADAM_B2) * jnp.square(g)
    m_hat = m / (1.0 - ADAM_B1 ** ADAM_STEP)
    v_hat = v / (1.0 - ADAM_B2 ** ADAM_STEP)
    delta = -ADAM_LR * (m_hat / (jnp.sqrt(v_hat) + ADAM_EPS) + ADAM_WD * w)
    return delta, m, v


def _adamw_sum(w, m, v, parts):
    g = parts[0].astype(F32)
    for j in range(1, N_DEV):
        g = g + parts[j].astype(F32)
    delta, m2, v2 = _adamw(w, g, m, v)
    return g, delta, m2, v2


def _t5_bucket(dist):
    max_exact = N_REL_BUCKETS // 2
    d = np.maximum(dist, 1).astype(np.float32)
    scale = (N_REL_BUCKETS - max_exact) / math.log(REL_MAX_DIST / max_exact)
    large = max_exact + (np.log(d / max_exact) * scale).astype(np.int32)
    large = np.minimum(large, N_REL_BUCKETS - 1)
    return np.where(dist < max_exact, dist, large).astype(np.int32)


def _bucket_table(dilation):
    i = np.arange(ATT_BLOCK)[:, None]
    kk = np.arange(2 * ATT_BLOCK)[None, :]
    steps = ATT_BLOCK + i - kk
    return _t5_bucket(np.maximum(steps, 0) * dilation)


def _bias_fwd(rel_bias, buckets, g):
    def body(rel_ref, bk_ref, o_ref):
        bk = bk_ref[...]
        for h in range(HEADS_PER_GROUP):
            acc = jnp.zeros(bk.shape, F32)
            for b in range(N_REL_BUCKETS):
                acc = jnp.where(bk == b, rel_ref[b, g * HEADS_PER_GROUP + h], acc)
            o_ref[h] = acc

    return pl.pallas_call(
        body, in_specs=[pl.BlockSpec(memory_space=pltpu.SMEM), pl.BlockSpec(memory_space=pltpu.VMEM)],
        out_specs=pl.BlockSpec(memory_space=pltpu.VMEM),
        out_shape=jax.ShapeDtypeStruct((HEADS_PER_GROUP, ATT_BLOCK, 2 * ATT_BLOCK), F32),
        name=f"rel_bias_fwd{g}")(rel_bias, buckets)


def _bias_bwd(dbias, buckets, g):
    def body(db_ref, bk_ref, o_ref):
        bk = bk_ref[...]
        row = lax.broadcasted_iota(jnp.int32, (N_REL_BUCKETS, LANES), 0)
        col = lax.broadcasted_iota(jnp.int32, (N_REL_BUCKETS, LANES), 1)
        acc = jnp.zeros((N_REL_BUCKETS, LANES), F32)
        for h in range(HEADS_PER_GROUP):
            d = db_ref[h]
            for b in range(N_REL_BUCKETS):
                s = jnp.sum(jnp.sum(jnp.where(bk == b, d, 0.0), axis=1, keepdims=True), axis=0, keepdims=True)
                acc = acc + jnp.where((row == b) & (col == g * HEADS_PER_GROUP + h), s, 0.0)
        o_ref[...] = acc

    return pl.pallas_call(
        body, in_specs=[pl.BlockSpec(memory_space=pltpu.VMEM), pl.BlockSpec(memory_space=pltpu.VMEM)],
        out_specs=pl.BlockSpec(memory_space=pltpu.VMEM),
        out_shape=jax.ShapeDtypeStruct((N_REL_BUCKETS, LANES), F32), name=f"rel_bias_bwd{g}")(dbias, buckets)


_NT = (((1,), (1,)), ((), ()))
_TN = (((0,), (0,)), ((), ()))
_QKV_BLOCKS = 3 * QKV_WIDTH // WIDTH_A


def _band_mask(n_is_first):
    i = lax.broadcasted_iota(jnp.int32, (ATT_BLOCK, 2 * ATT_BLOCK), 0)
    kk = lax.broadcasted_iota(jnp.int32, (ATT_BLOCK, 2 * ATT_BLOCK), 1)
    return (kk >= i) & (kk <= i + ATT_STEPS) & ((kk >= ATT_BLOCK) | jnp.logical_not(n_is_first))


def _head(ref, h):
    return ref[:, h * HEAD_DIM:(h + 1) * HEAD_DIM]


def _attn_specs(g, d):
    blk = (ATT_BLOCK, WIDTH_A)
    q = pl.BlockSpec(blk, lambda c, n: (n, c * _QKV_BLOCKS + g))
    kp = pl.BlockSpec(blk, lambda c, n: (jnp.maximum(n - 1, 0), c * _QKV_BLOCKS + 3 + g))
    kc = pl.BlockSpec(blk, lambda c, n: (n, c * _QKV_BLOCKS + 3 + g))
    vp = pl.BlockSpec(blk, lambda c, n: (jnp.maximum(n - 1, 0), c * _QKV_BLOCKS + 6 + g))
    vc = pl.BlockSpec(blk, lambda c, n: (n, c * _QKV_BLOCKS + 6 + g))
    return [q, kp, kc, vp, vc]


def _attn_fwd(qkv, bias, g, d):
    t = qkv.shape[0]
    lq = t // d
    nb = lq // ATT_BLOCK
    scale = HEAD_DIM ** -0.5

    def body(q_ref, kp_ref, kc_ref, vp_ref, vc_ref, b_ref, o_ref, l_ref):
        mask = _band_mask(pl.program_id(1) == 0)
        for h in range(HEADS_PER_GROUP):
            qh = _head(q_ref, h).astype(BF16)
            kh = jnp.concatenate([_head(kp_ref, h), _head(kc_ref, h)], axis=0).astype(BF16)
            vh = jnp.concatenate([_head(vp_ref, h), _head(vc_ref, h)], axis=0).astype(BF16)
            s = lax.dot_general(qh, kh, _NT, preferred_element_type=F32) * scale + b_ref[h]
            s = jnp.where(mask, s, NEG_INF)
            m = jnp.max(s, axis=1, keepdims=True)
            p = jnp.exp(s - m)
            den = jnp.sum(p, axis=1, keepdims=True)
            o = jnp.dot(p.astype(BF16), vh, preferred_element_type=F32) / den
            o_ref[:, h * HEAD_DIM:(h + 1) * HEAD_DIM] = o
            l_ref[:, h * HEAD_DIM:(h + 1) * HEAD_DIM] = jnp.broadcast_to(m + jnp.log(den), (ATT_BLOCK, HEAD_DIM))

    out_spec = pl.BlockSpec((ATT_BLOCK, WIDTH_A), lambda c, n: (n, c))
    o, lse = pl.pallas_call(
        body, grid=(d, nb),
        in_specs=_attn_specs(g, d) + [pl.BlockSpec(bias.shape, _zero_map(3))],
        out_specs=[out_spec, out_spec],
        out_shape=[jax.ShapeDtypeStruct((lq, d * WIDTH_A), F32)] * 2,
        compiler_params=_cparams("parallel", "parallel"), name=f"attn_fwd{g}",
    )(*([qkv.reshape(lq, d * 3 * QKV_WIDTH)] * 5), bias)
    return o.reshape(t, WIDTH_A), lse.reshape(t, WIDTH_A)


def _attn_dq(qkv, bias, do, lse, corr, g, d):
    t = qkv.shape[0]
    lq = t // d
    nb = lq // ATT_BLOCK
    scale = HEAD_DIM ** -0.5

    def body(q_ref, kp_ref, kc_ref, vp_ref, vc_ref, b_ref, do_ref, l_ref, c_ref, dq_ref):
        mask = _band_mask(pl.program_id(1) == 0)
        for h in range(HEADS_PER_GROUP):
            qh = _head(q_ref, h).astype(BF16)
            kh = jnp.concatenate([_head(kp_ref, h), _head(kc_ref, h)], axis=0).astype(BF16)
            vh = jnp.concatenate([_head(vp_ref, h), _head(vc_ref, h)], axis=0).astype(BF16)
            s = lax.dot_general(qh, kh, _NT, preferred_element_type=F32) * scale + b_ref[h]
            s = jnp.where(mask, s, NEG_INF)
            p = jnp.exp(s - l_ref[:, h * HEAD_DIM:h * HEAD_DIM + 1])
            dp = lax.dot_general(_head(do_ref, h).astype(BF16), vh, _NT, preferred_element_type=F32)
            ds = p * (dp + c_ref[:, h * HEAD_DIM:h * HEAD_DIM + 1])
            dq_ref[:, h * HEAD_DIM:(h + 1) * HEAD_DIM] = jnp.dot(ds.astype(BF16), kh, preferred_element_type=F32) * scale

    row_spec = pl.BlockSpec((ATT_BLOCK, WIDTH_A), lambda c, n: (n, c))
    view = lambda a: a.reshape(lq, d * WIDTH_A)
    dq = pl.pallas_call(
        body, grid=(d, nb),
        in_specs=_attn_specs(g, d) + [pl.BlockSpec(bias.shape, _zero_map(3)), row_spec, row_spec, row_spec],
        out_specs=row_spec, out_shape=jax.ShapeDtypeStruct((lq, d * WIDTH_A), F32),
        compiler_params=_cparams("parallel", "parallel"), name=f"attn_dq{g}",
    )(*([qkv.reshape(lq, d * 3 * QKV_WIDTH)] * 5), bias, view(do), view(lse), view(corr))
    return dq.reshape(t, WIDTH_A)


def _attn_dkv(qkv, bias, do, lse, corr, g, d):
    t = qkv.shape[0]
    lq = t // d
    nb = lq // ATT_BLOCK
    scale = HEAD_DIM ** -0.5

    def body(k_ref, v_ref, q0_ref, q1_ref, do0_ref, do1_ref, l0_ref, l1_ref, c0_ref, c1_ref, b_ref,
             dk_ref, dv_ref, db_ref):
        c, j = pl.program_id(0), pl.program_id(1)

        @pl.when((c == 0) & (j == 0))
        def _():
            db_ref[...] = jnp.zeros_like(db_ref)

        i = lax.broadcasted_iota(jnp.int32, (ATT_BLOCK, ATT_BLOCK), 0)
        kk = lax.broadcasted_iota(jnp.int32, (ATT_BLOCK, ATT_BLOCK), 1)
        mask0 = kk <= i
        mask1 = (kk >= i) & (j + 1 < nb)
        for h in range(HEADS_PER_GROUP):
            kh = _head(k_ref, h).astype(BF16)
            vh = _head(v_ref, h).astype(BF16)
            dk = jnp.zeros((ATT_BLOCK, HEAD_DIM), F32)
            dv = jnp.zeros((ATT_BLOCK, HEAD_DIM), F32)
            parts = ((q0_ref, do0_ref, l0_ref, c0_ref, mask0, ATT_BLOCK), (q1_ref, do1_ref, l1_ref, c1_ref, mask1, 0))
            for q_ref, do_ref, l_ref, c_ref, mask, off in parts:
                qh = _head(q_ref, h).astype(BF16)
                doh = _head(do_ref, h).astype(BF16)
                s = lax.dot_general(qh, kh, _NT, preferred_element_type=F32) * scale + b_ref[h, :, off:off + ATT_BLOCK]
                s = jnp.where(mask, s, NEG_INF)
                p = jnp.exp(s - l_ref[:, h * HEAD_DIM:h * HEAD_DIM + 1])
                dp = lax.dot_general(doh, vh, _NT, preferred_element_type=F32)
                ds = p * (dp + c_ref[:, h * HEAD_DIM:h * HEAD_DIM + 1])
                dv = dv + lax.dot_general(p.astype(BF16), doh, _TN, preferred_element_type=F32)
                dk = dk + lax.dot_general(ds.astype(BF16), qh, _TN, preferred_element_type=F32)
                db_ref[h, :, off:off + ATT_BLOCK] += ds
            dk_ref[:, h * HEAD_DIM:(h + 1) * HEAD_DIM] = dk * scale
            dv_ref[:, h * HEAD_DIM:(h + 1) * HEAD_DIM] = dv

    blk = (ATT_BLOCK, WIDTH_A)
    nxt = lambda n: jnp.minimum(n + 1, nb - 1)
    k_spec = pl.BlockSpec(blk, lambda c, n: (n, c * _QKV_BLOCKS + 3 + g))
    v_spec = pl.BlockSpec(blk, lambda c, n: (n, c * _QKV_BLOCKS + 6 + g))
    q0_spec = pl.BlockSpec(blk, lambda c, n: (n, c * _QKV_BLOCKS + g))
    q1_spec = pl.BlockSpec(blk, lambda c, n: (nxt(n), c * _QKV_BLOCKS + g))
    r0 = pl.BlockSpec(blk, lambda c, n: (n, c))
    r1 = pl.BlockSpec(blk, lambda c, n: (nxt(n), c))
    view = lambda a: a.reshape(lq, d * WIDTH_A)
    qv = qkv.reshape(lq, d * 3 * QKV_WIDTH)
    dk, dv, dbias = pl.pallas_call(
        body, grid=(d, nb),
        in_specs=[k_spec, v_spec, q0_spec, q1_spec, r0, r1, r0, r1, r0, r1, pl.BlockSpec(bias.shape, _zero_map(3))],
        out_specs=[r0, r0, pl.BlockSpec(bias.shape, _zero_map(3))],
        out_shape=[jax.ShapeDtypeStruct((lq, d * WIDTH_A), F32)] * 2 + [jax.ShapeDtypeStruct(bias.shape, F32)],
        compiler_params=_cparams("arbitrary", "arbitrary"), name=f"attn_dkv{g}",
    )(qv, qv, qv, qv, view(do), view(do), view(lse), view(lse), view(corr), view(corr), bias)
    return dk.reshape(t, WIDTH_A), dv.reshape(t, WIDTH_A), dbias


def _tril_mask():
    r = lax.broadcasted_iota(jnp.int32, (CHUNK, CHUNK), 0)
    c = lax.broadcasted_iota(jnp.int32, (CHUNK, CHUNK), 1)
    return c <= r


def _gmlp_fwd(zb, ln_g, ln_b, w_s, b_s_t):
    t = zb.shape[0]
    tr = _pick(t, 2 * CHUNK, CHUNK)

    def body(z_ref, g_ref, b_ref, ws_ref, bs_ref, o_ref):
        tri = _tril_mask()
        z = jax.nn.gelu(z_ref[...])
        u = z[:, :WIDTH_B]
        vn = _ln(z[:, WIDTH_B:], g_ref[...], b_ref[...])
        for ch in range(tr // CHUNK):
            rows = slice(ch * CHUNK, (ch + 1) * CHUNK)
            for gi in range(N_GROUPS_B):
                cols = slice(gi * CHUNK, (gi + 1) * CHUNK)
                w = jnp.where(tri, ws_ref[gi], 0.0).astype(BF16)
                mixed = jnp.dot(w, vn[rows, cols].astype(BF16), preferred_element_type=F32) + bs_ref[:, gi:gi + 1]
                o_ref[rows, cols] = u[rows, cols] * mixed

    return pl.pallas_call(
        body, grid=(t // tr,),
        in_specs=[pl.BlockSpec((tr, 2 * WIDTH_B), lambda i: (i, 0)), pl.BlockSpec(ln_g.shape, _zero_map(2)),
                  pl.BlockSpec(ln_b.shape, _zero_map(2)), pl.BlockSpec(w_s.shape, _zero_map(3)),
                  pl.BlockSpec(b_s_t.shape, _zero_map(2))],
        out_specs=pl.BlockSpec((tr, WIDTH_B), lambda i: (i, 0)),
        out_shape=jax.ShapeDtypeStruct((t, WIDTH_B), F32),
        compiler_params=_cparams("parallel"), name="gmlp_fwd")(zb, ln_g, ln_b, w_s, b_s_t)


def _gmlp_bwd(zb, dyb, ln_g, ln_b, w_s, b_s_t, group_sel):
    t = zb.shape[0]
    tr = _pick(t, 2 * CHUNK, CHUNK)

    def body(z_ref, dy_ref, g_ref, b_ref, ws_ref, bs_ref, sel_ref, dz_ref, dg_ref, db_ref, dws_ref, dbs_ref,
             du_s, dvn_s, dm_s):
        @pl.when(pl.program_id(0) == 0)
        def _():
            dg_ref[...] = jnp.zeros_like(dg_ref)
            db_ref[...] = jnp.zeros_like(db_ref)
            dws_ref[...] = jnp.zeros_like(dws_ref)
            dbs_ref[...] = jnp.zeros_like(dbs_ref)

        tri = _tril_mask()
        z, gelu_vjp = jax.vjp(jax.nn.gelu, z_ref[...])
        u = z[:, :WIDTH_B]
        vn, ln_vjp = jax.vjp(_ln, z[:, WIDTH_B:], g_ref[...], b_ref[...])
        dy = dy_ref[...]
        for ch in range(tr // CHUNK):
            rows = slice(ch * CHUNK, (ch + 1) * CHUNK)
            for gi in range(N_GROUPS_B):
                cols = slice(gi * CHUNK, (gi + 1) * CHUNK)
                w = jnp.where(tri, ws_ref[gi], 0.0).astype(BF16)
                vg = vn[rows, cols].astype(BF16)
                mixed = jnp.dot(w, vg, preferred_element_type=F32) + bs_ref[:, gi:gi + 1]
                dyg = dy[rows, cols]
                dm = dyg * u[rows, cols]
                dmb = dm.astype(BF16)
                du_s[rows, cols] = dyg * mixed
                dm_s[rows, cols] = dm
                dvn_s[rows, cols] = lax.dot_general(w, dmb, _TN, preferred_element_type=F32)
                dws_ref[gi] += jnp.where(tri, lax.dot_general(dmb, vg, _NT, preferred_element_type=F32), 0.0)
            dbs_ref[...] += jnp.dot(dm_s[rows, :], sel_ref[...], precision=lax.Precision.HIGHEST,
                                    preferred_element_type=F32)
        dv, dg, db = ln_vjp(dvn_s[...])
        dg_ref[...] += dg
        db_ref[...] += db
        dz_ref[...] = gelu_vjp(jnp.concatenate([du_s[...], dv], axis=1))[0]

    full = lambda a: pl.BlockSpec(a.shape, _zero_map(a.ndim))
    return pl.pallas_call(
        body, grid=(t // tr,),
        in_specs=[pl.BlockSpec((tr, 2 * WIDTH_B), lambda i: (i, 0)), pl.BlockSpec((tr, WIDTH_B), lambda i: (i, 0)),
                  full(ln_g), full(ln_b), full(w_s), full(b_s_t), full(group_sel)],
        out_specs=[pl.BlockSpec((tr, 2 * WIDTH_B), lambda i: (i, 0)), full(ln_g), full(ln_b), full(w_s),
                   pl.BlockSpec((CHUNK, LANES), _zero_map(2))],
        out_shape=[jax.ShapeDtypeStruct((t, 2 * WIDTH_B), F32), jax.ShapeDtypeStruct(ln_g.shape, F32),
                   jax.ShapeDtypeStruct(ln_b.shape, F32), jax.ShapeDtypeStruct(w_s.shape, F32),
                   jax.ShapeDtypeStruct((CHUNK, LANES), F32)],
        scratch_shapes=[pltpu.VMEM((tr, WIDTH_B), F32)] * 3,
        compiler_params=_cparams("arbitrary"), name="gmlp_bwd")(zb, dyb, ln_g, ln_b, w_s, b_s_t, group_sel)


def _s5_disc(lr, li, ldt, br_t, bi_t):
    dt = jnp.exp(ldt)
    mag = jnp.exp(lr * dt)
    ab_re = mag * jnp.cos(li * dt)
    ab_im = mag * jnp.sin(li * dt)
    nrm = lr * lr + li * li
    cr = ((ab_re - 1.0) * lr + ab_im * li) / nrm
    ci = (ab_im * lr - (ab_re - 1.0) * li) / nrm
    return ab_re, ab_im, cr * br_t - ci * bi_t, cr * bi_t + ci * br_t


def _vmem_call(fn, args, out_shape, name):
    def body(*refs):
        res = fn(*[r[...] for r in refs[:len(args)]])
        for o, v in zip(refs[len(args):], res):
            o[...] = v

    vm = pl.BlockSpec(memory_space=pltpu.VMEM)
    return pl.pallas_call(body, in_specs=[vm] * len(args), out_specs=[vm] * len(out_shape),
                          out_shape=out_shape, name=name)(*args)


def _s5_disc_fwd(lr, li, ldt, br_t, bi_t):
    s1 = jax.ShapeDtypeStruct(lr.shape, F32)
    s2 = jax.ShapeDtypeStruct(br_t.shape, F32)
    return _vmem_call(_s5_disc, [lr, li, ldt, br_t, bi_t], [s1, s1, s2, s2], "s5_disc_fwd")


def _s5_disc_bwd(lr, li, ldt, br_t, bi_t, cts):
    def fn(lr, li, ldt, br_t, bi_t, d0, d1, d2, d3):
        _, vjp = jax.vjp(_s5_disc, lr, li, ldt, br_t, bi_t)
        return vjp((d0, d1, d2, d3))

    shp = [jax.ShapeDtypeStruct(a.shape, F32) for a in (lr, li, ldt, br_t, bi_t)]
    return _vmem_call(fn, [lr, li, ldt, br_t, bi_t, *cts], shp, "s5_disc_bwd")


_SCAN_ROWS = SSM_COLS // LANES
_SCAN_CHUNK = 256


def _scan_fwd(bre, bim, are, aim):
    t = bre.shape[0]
    tc = _pick(t, _SCAN_CHUNK, SUBLANES)

    def body(br_ref, bi_ref, ar_ref, ai_ref, xr_ref, xi_ref, st_ref):
        @pl.when(pl.program_id(0) == 0)
        def _():
            st_ref[...] = jnp.zeros_like(st_ref)

        ar, ai = ar_ref[...], ai_ref[...]

        def step(i, carry):
            xr, xi = carry
            nr = ar * xr - ai * xi + br_ref[i]
            ni = ar * xi + ai * xr + bi_ref[i]
            xr_ref[i] = nr
            xi_ref[i] = ni
            return nr, ni

        xr, xi = lax.fori_loop(0, tc, step, (st_ref[0], st_ref[1]), unroll=8)
        st_ref[0] = xr
        st_ref[1] = xi

    blk = pl.BlockSpec((tc, _SCAN_ROWS, LANES), lambda i: (i, 0, 0))
    par = pl.BlockSpec((_SCAN_ROWS, LANES), _zero_map(2))
    shp = jax.ShapeDtypeStruct(bre.shape, F32)
    return pl.pallas_call(
        body, grid=(t // tc,), in_specs=[blk, blk, par, par], out_specs=[blk, blk], out_shape=[shp, shp],
        scratch_shapes=[pltpu.VMEM((2, _SCAN_ROWS, LANES), F32)],
        compiler_params=_cparams("arbitrary"), name="s5_scan_fwd")(bre, bim, are, aim)


def _scan_bwd(dxr, dxi, xr, xi, are, aim):
    t = dxr.shape[0]
    tc = _pick(t, _SCAN_CHUNK, SUBLANES)
    nc = t // tc

    def body(dr_ref, di_ref, xr_ref, xi_ref, pr_ref, pi_ref, ar_ref, ai_ref, gr_ref, gi_ref, dar_ref, dai_ref, st_ref):
        step_id = pl.program_id(0)

        @pl.when(step_id == 0)
        def _():
            st_ref[...] = jnp.zeros_like(st_ref)
            dar_ref[...] = jnp.zeros_like(dar_ref)
            dai_ref[...] = jnp.zeros_like(dai_ref)

        ar, ai = ar_ref[...], ai_ref[...]

        def update(i, carry, pxr, pxi):
            gr, gi, dar, dai = carry
            ngr = dr_ref[i] + ar * gr + ai * gi
            ngi = di_ref[i] - ai * gr + ar * gi
            gr_ref[i] = ngr
            gi_ref[i] = ngi
            return ngr, ngi, dar + ngr * pxr + ngi * pxi, dai - ngr * pxi + ngi * pxr

        def step(s, carry):
            i = tc - 1 - s
            return update(i, carry, xr_ref[i - 1], xi_ref[i - 1])

        zero = jnp.zeros((_SCAN_ROWS, LANES), F32)
        carry = lax.fori_loop(0, tc - 1, step, (st_ref[0], st_ref[1], zero, zero), unroll=8)
        has_prev = (step_id < nc - 1).astype(F32)
        gr, gi, dar, dai = update(0, carry, pr_ref[0] * has_prev, pi_ref[0] * has_prev)
        st_ref[0] = gr
        st_ref[1] = gi
        dar_ref[...] += dar
        dai_ref[...] += dai

    blk = pl.BlockSpec((tc, _SCAN_ROWS, LANES), lambda i: (nc - 1 - i, 0, 0))
    prev = pl.BlockSpec((1, _SCAN_ROWS, LANES), lambda i: (jnp.maximum((nc - 1 - i) * tc - 1, 0), 0, 0))
    par = pl.BlockSpec((_SCAN_ROWS, LANES), _zero_map(2))
    shp = jax.ShapeDtypeStruct(dxr.shape, F32)
    psh = jax.ShapeDtypeStruct((_SCAN_ROWS, LANES), F32)
    return pl.pallas_call(
        body, grid=(nc,), in_specs=[blk, blk, blk, blk, prev, prev, par, par],
        out_specs=[blk, blk, par, par], out_shape=[shp, shp, psh, psh],
        scratch_shapes=[pltpu.VMEM((2, _SCAN_ROWS, LANES), F32)],
        compiler_params=_cparams("arbitrary"), name="s5_scan_bwd")(dxr, dxi, xr, xi, xr, xi, are, aim)


def _block_diag(m):
    _, a, b = m.shape
    m4 = m.reshape(N_SSM_BLOCKS, SSM_PACK, a, b)
    eye = jnp.eye(SSM_PACK, dtype=m.dtype)
    return (m4[:, :, :, None, :] * eye[None, :, None, :, None]).reshape(N_SSM_BLOCKS, SSM_PACK * a, SSM_PACK * b)


def _block_diag_extract(m, a, b):
    m5 = m.reshape(N_SSM_BLOCKS, SSM_PACK, a, SSM_PACK, b)
    idx = np.arange(SSM_PACK)
    return m5[:, idx, :, idx, :].transpose(1, 0, 2, 3).reshape(N_GROUPS_C, a, b)


def _exchange(src, *, gather, name):
    shape = src.shape if gather else src.shape[1:]

    def body(src_ref, out_ref, send_sems, recv_sems, local_sem):
        x, y, c = lax.axis_index("x"), lax.axis_index("y"), lax.axis_index("c")
        me = 4 * x + 2 * y + c
        copies = []
        for r in range(1, N_DEV):
            px = 1 - x if r & 4 else x
            py = 1 - y if r & 2 else y
            pc = 1 - c if r & 1 else c
            piece = src_ref if gather else src_ref.at[4 * px + 2 * py + pc]
            cp = pltpu.make_async_remote_copy(
                src_ref=piece, dst_ref=out_ref.at[me], send_sem=send_sems.at[r - 1], recv_sem=recv_sems.at[r - 1],
                device_id=(px, py, pc), device_id_type=pl.DeviceIdType.MESH)
            cp.start()
            copies.append(cp)
        mine = pltpu.make_async_copy(src_ref if gather else src_ref.at[me], out_ref.at[me], local_sem)
        mine.start()
        for cp in copies:
            cp.wait()
        mine.wait()

    hbm = pl.BlockSpec(memory_space=pl.ANY)
    return pl.pallas_call(
        body, in_specs=[hbm], out_specs=hbm, out_shape=jax.ShapeDtypeStruct((N_DEV,) + tuple(shape), src.dtype),
        scratch_shapes=[pltpu.SemaphoreType.DMA((N_DEV - 1,)), pltpu.SemaphoreType.DMA((N_DEV - 1,)),
                        pltpu.SemaphoreType.DMA(())],
        name=name)(src)


def _assemble(name, gathered):
    _, depth, r, c = gathered.shape
    if name in COL_SHARDED:
        return gathered.transpose(1, 2, 0, 3).reshape(depth, r, N_DEV * c)
    return gathered.transpose(1, 0, 2, 3).reshape(depth, N_DEV * r, c)


def _split(name, full):
    depth, r, c = full.shape
    if name in COL_SHARDED:
        return full.reshape(depth, r, N_DEV, c // N_DEV).transpose(2, 0, 1, 3)
    return full.reshape(depth, N_DEV, r // N_DEV, c).transpose(1, 0, 2, 3)


def _small_sizes(shapes):
    return [int(np.prod(shapes[n])) for n in SMALL]


def _pack_small(vals):
    flat = jnp.concatenate([vals[n].reshape(-1).astype(F32) for n in SMALL])
    rows = -(-flat.shape[0] // (LANES * N_DEV * SUBLANES)) * (N_DEV * SUBLANES)
    return jnp.pad(flat, (0, rows * LANES - flat.shape[0])).reshape(rows, LANES)


def _unpack_small(packed, shapes):
    flat = packed.reshape(-1)
    out, off = {}, 0
    for n, size in zip(SMALL, _small_sizes(shapes)):
        out[n] = flat[off:off + size].reshape(shapes[n])
        off += size
    return out


def _row(v):
    return v.reshape(1, -1)


def _layer_params(l, full, small):
    o1, o2, o3 = 3 * QKV_WIDTH, 3 * QKV_WIDTH + 2 * WIDTH_B, 3 * QKV_WIDTH + 2 * WIDTH_B + WIDTH_C
    w_in, b_in = full["w_in"][l], small["b_in"][l]
    p = {
        "w_qkv": w_in[:, :o1], "w_zb": w_in[:, o1:o2], "w_uc": w_in[:, o2:o3], "w_gl": w_in[:, o3:],
        "b_qkv": _row(b_in[:o1]), "b_zb": _row(b_in[o1:o2]), "b_uc": _row(b_in[o2:o3]), "b_gl": _row(b_in[o3:]),
        "sgu_ln_g": _row(small["sgu_ln_g"][l]), "sgu_ln_b": _row(small["sgu_ln_b"][l]),
        "w_s": small["w_s"][l], "b_s_t": small["b_s"][l].T,
        "lam_re": small["lam_re"][l][:, None, :], "lam_im": small["lam_im"][l][:, None, :],
        "log_dt": small["log_dt"][l][:, None, None],
        "b_re_t": small["b_re"][l].transpose(0, 2, 1), "b_im_t": small["b_im"][l].transpose(0, 2, 1),
        "c_re_t": small["c_re"][l].transpose(0, 2, 1), "c_im_t": small["c_im"][l].transpose(0, 2, 1),
        "d_skip": _row(small["d_skip"][l]), "b_glu": _row(small["b_glu"][l]),
        "ln1_g": _row(small["ln1_g"][l]), "ln1_b": _row(small["ln1_b"][l]),
        "ln2_g": _row(small["ln2_g"][l]), "ln2_b": _row(small["ln2_b"][l]),
    }
    for n in ("w_glu", "w_pa", "w_pb", "w_pc", "w_o", "w_ffn_in", "w_ffn_out"):
        p[n] = full[n][l]
    return p


def _scan_view(a):
    return a.reshape(a.shape[0], _SCAN_ROWS, LANES)


def _layer_fwd(x, p, biases):
    t, d = x.shape
    r = {"x": x}
    r["qkv"] = _mm(x, p["w_qkv"], bias=p["b_qkv"], name="mm_qkv")
    r["zb"] = _mm(x, p["w_zb"], bias=p["b_zb"], name="mm_zb")
    r["uc"] = _mm(x, p["w_uc"], bias=p["b_uc"], name="mm_uc")
    r["gl"] = _mm(x, p["w_gl"], bias=p["b_gl"], name="mm_gl")
    ol = []
    for g, dil in enumerate(ATT_DILATIONS):
        ol += list(_attn_fwd(r["qkv"], biases[g], g, dil))
    r["ol"] = ol
    r["ya"] = _rowmap(_combine, ol, row_outs=[(WIDTH_A, F32)], tr=512, name="attn_combine")[0]
    r["yb"] = _gmlp_fwd(r["zb"], p["sgu_ln_g"], p["sgu_ln_b"], p["w_s"], p["b_s_t"])
    ab_re, ab_im, bb_re_t, bb_im_t = _s5_disc_fwd(p["lam_re"], p["lam_im"], p["log_dt"], p["b_re_t"], p["b_im_t"])
    r["a_re"], r["a_im"] = ab_re.reshape(_SCAN_ROWS, LANES), ab_im.reshape(_SCAN_ROWS, LANES)
    r["bmat_re"], r["bmat_im"] = _block_diag(bb_re_t), _block_diag(bb_im_t)
    r["cmat_re"], r["cmat_im"] = _block_diag(p["c_re_t"]), _block_diag(p["c_im_t"])
    bu_re = _bdmm(r["uc"], r["bmat_re"], name="s5_in_re")
    bu_im = _bdmm(r["uc"], r["bmat_im"], name="s5_in_im")
    xr, xi = _scan_fwd(_scan_view(bu_re), _scan_view(bu_im), r["a_re"], r["a_im"])
    r["xr"], r["xi"] = xr.reshape(t, SSM_COLS), xi.reshape(t, SSM_COLS)
    y_re = _bdmm(r["xr"], r["cmat_re"], name="s5_out_re")
    y_im = _bdmm(r["xi"], r["cmat_im"], scale=-1.0, name="s5_out_im")
    r["ys"], r["ycp"] = _rowmap(_s5_out, [y_re, y_im, r["uc"]], consts=[p["d_skip"]],
                                row_outs=[(WIDTH_C, F32)] * 2, tr=512, name="s5_out_act")
    r["glin"] = _mm(r["ycp"], p["w_glu"], bias=p["b_glu"], name="mm_glu")
    r["yc"] = _rowmap(_glu, [r["ycp"], r["glin"]], row_outs=[(WIDTH_C, F32)], tr=512, name="glu")[0]
    r["pa"] = _mm(r["ya"], p["w_pa"], name="mm_pa")
    r["pb"] = _mm(r["yb"], p["w_pb"], name="mm_pb")
    r["pc"] = _mm(r["yc"], p["w_pc"], name="mm_pc")
    r["merged"] = _rowmap(_merge, [r["gl"], r["pa"], r["pb"], r["pc"]], row_outs=[(d, F32)], name="merge")[0]
    r["mo"] = _mm(r["merged"], p["w_o"], name="mm_o")
    r["xm"] = _rowmap(_post_norm, [x, r["mo"]], consts=[p["ln1_g"], p["ln1_b"]], row_outs=[(d, F32)], name="norm1")[0]
    r["gu"] = _mm(r["xm"], p["w_ffn_in"], tn=1024, name="mm_ffn_in")
    r["act"] = _rowmap(_swiglu, [r["gu"]], row_outs=[(r["gu"].shape[1] // 2, F32)], tr=128, name="swiglu")[0]
    r["f"] = _mm(r["act"], p["w_ffn_out"], name="mm_ffn_out")
    out = _rowmap(_post_norm, [r["xm"], r["f"]], consts=[p["ln2_g"], p["ln2_b"]], row_outs=[(d, F32)], name="norm2")[0]
    return out, r


def _layer_bwd(dout, r, p, biases, consts):
    t, d = dout.shape
    gw, gs = {}, {}
    ffw = r["gu"].shape[1]
    dxm, df, gs["ln2_g"], gs["ln2_b"] = _rowmap(
        _post_norm_bwd, [r["xm"], r["f"], dout], consts=[p["ln2_g"], p["ln2_b"]],
        row_outs=[(d, F32)] * 2, red_outs=[(1, d)] * 2, name="norm2_bwd")
    gw["w_ffn_out"] = _mm(r["act"], df, ta=True, tk=512, name="mm_ffn_out_dw")
    dact = _mm(df, p["w_ffn_out"], tb=True, name="mm_ffn_out_dx")
    dgu = _rowmap(_swiglu_bwd, [r["gu"], dact], row_outs=[(ffw, F32)], tr=128, name="swiglu_bwd")[0]
    gw["w_ffn_in"] = _mm(r["xm"], dgu, ta=True, tn=1024, tk=512, name="mm_ffn_in_dw")
    dxm = _mm(dgu, p["w_ffn_in"], tb=True, add=dxm, tk=1024, name="mm_ffn_in_dx")
    dx, dmo, gs["ln1_g"], gs["ln1_b"] = _rowmap(
        _post_norm_bwd, [r["x"], r["mo"], dxm], consts=[p["ln1_g"], p["ln1_b"]],
        row_outs=[(d, F32)] * 2, red_outs=[(1, d)] * 2, name="norm1_bwd")
    gw["w_o"] = _mm(r["merged"], dmo, ta=True, tk=512, name="mm_o_dw")
    dmerged = _mm(dmo, p["w_o"], tb=True, name="mm_o_dx")
    dgl, dpa, dpb, dpc, db_gl = _rowmap(
        _merge_bwd, [r["gl"], r["pa"], r["pb"], r["pc"], dmerged],
        row_outs=[(3 * d, F32), (d, F32), (d, F32), (d, F32)], red_outs=[(1, 3 * d)], tr=128, name="merge_bwd")
    gw["w_pa"] = _mm(r["ya"], dpa, ta=True, tk=512, name="mm_pa_dw")
    gw["w_pb"] = _mm(r["yb"], dpb, ta=True, tk=512, name="mm_pb_dw")
    gw["w_pc"] = _mm(r["yc"], dpc, ta=True, tk=512, name="mm_pc_dw")
    dya = _mm(dpa, p["w_pa"], tb=True, name="mm_pa_dx")
    dyb = _mm(dpb, p["w_pb"], tb=True, name="mm_pb_dx")
    dyc = _mm(dpc, p["w_pc"], tb=True, name="mm_pc_dx")
    dycp, dglin, gs["b_glu"] = _rowmap(_glu_bwd, [r["ycp"], r["glin"], dyc], row_outs=[(WIDTH_C, F32)] * 2,
                                       red_outs=[(1, WIDTH_C)], tr=512, name="glu_bwd")
    gw["w_glu"] = _mm(r["ycp"], dglin, ta=True, tk=512, name="mm_glu_dw")
    dycp = _mm(dglin, p["w_glu"], tb=True, add=dycp, name="mm_glu_dx")
    dys, duc, gs["d_skip"] = _rowmap(_s5_out_bwd, [r["ys"], r["uc"], dycp], consts=[p["d_skip"]],
                                     row_outs=[(WIDTH_C, F32)] * 2, red_outs=[(1, WIDTH_C)], tr=512, name="s5_out_act_bwd")
    dxr = _bdmm(dys, r["cmat_re"], tb=True, name="s5_out_re_dx")
    dxi = _bdmm(dys, r["cmat_im"], tb=True, scale=-1.0, name="s5_out_im_dx")
    d_cmat_re = _bdmm_tn(r["xr"], dys, N_SSM_BLOCKS, name="s5_out_re_dw")
    d_cmat_im = _bdmm_tn(r["xi"], dys, N_SSM_BLOCKS, scale=-1.0, name="s5_out_im_dw")
    g_re, g_im, da_re, da_im = _scan_bwd(_scan_view(dxr), _scan_view(dxi), _scan_view(r["xr"]), _scan_view(r["xi"]),
                                         r["a_re"], r["a_im"])
    g_re, g_im = g_re.reshape(t, SSM_COLS), g_im.reshape(t, SSM_COLS)
    duc = _bdmm(g_re, r["bmat_re"], tb=True, add=duc, name="s5_in_re_dx")
    duc = _bdmm(g_im, r["bmat_im"], tb=True, add=duc, name="s5_in_im_dx")
    d_bmat_re = _bdmm_tn(r["uc"], g_re, N_SSM_BLOCKS, name="s5_in_re_dw")
    d_bmat_im = _bdmm_tn(r["uc"], g_im, N_SSM_BLOCKS, name="s5_in_im_dw")
    cts = (da_re.reshape(N_GROUPS_C, 1, SSM_STATE), da_im.reshape(N_GROUPS_C, 1, SSM_STATE),
           _block_diag_extract(d_bmat_re, SSM_GROUP, SSM_STATE), _block_diag_extract(d_bmat_im, SSM_GROUP, SSM_STATE))
    d_lr, d_li, d_ldt, d_br_t, d_bi_t = _s5_disc_bwd(p["lam_re"], p["lam_im"], p["log_dt"], p["b_re_t"], p["b_im_t"], cts)
    gs["lam_re"], gs["lam_im"], gs["log_dt"] = d_lr[:, 0, :], d_li[:, 0, :], d_ldt[:, 0, 0]
    gs["b_re"], gs["b_im"] = d_br_t.transpose(0, 2, 1), d_bi_t.transpose(0, 2, 1)
    gs["c_re"] = _block_diag_extract(d_cmat_re, SSM_STATE, SSM_GROUP).transpose(0, 2, 1)
    gs["c_im"] = _block_diag_extract(d_cmat_im, SSM_STATE, SSM_GROUP).transpose(0, 2, 1)
    dzb, gs["sgu_ln_g"], gs["sgu_ln_b"], gs["w_s"], dbs_t = _gmlp_bwd(
        r["zb"], dyb, p["sgu_ln_g"], p["sgu_ln_b"], p["w_s"], p["b_s_t"], consts["group_sel"])
    gs["b_s"] = dbs_t[:, :N_GROUPS_B].T
    do_corr = _rowmap(_combine_bwd, r["ol"] + [r["ya"], dya], consts=[consts["head_ones"]],
                      row_outs=[(WIDTH_A, F32)] * 6, tr=512, name="attn_combine_bwd")
    dq, dk, dv, dbias = [], [], [], []
    for g, dil in enumerate(ATT_DILATIONS):
        do_g, corr_g, lse_g = do_corr[g], do_corr[3 + g], r["ol"][2 * g + 1]
        dq.append(_attn_dq(r["qkv"], biases[g], do_g, lse_g, corr_g, g, dil))
        dk_g, dv_g, db_g = _attn_dkv(r["qkv"], biases[g], do_g, lse_g, corr_g, g, dil)
        dk.append(dk_g)
        dv.append(dv_g)
        dbias.append(db_g)
    dqkv = jnp.concatenate(dq + dk + dv, axis=1)
    pieces = (("w_qkv", dqkv), ("w_zb", dzb), ("w_uc", duc), ("w_gl", dgl))
    dws, dbs = [], []
    for n, dpiece in pieces:
        dws.append(_mm(r["x"], dpiece, ta=True, tk=512, name="mm_in_dw_" + n))
        dx = _mm(dpiece, p[n], tb=True, add=dx, tk=1024, name="mm_in_dx_" + n)
        if n == "w_gl":
            dbs.append(db_gl)
        else:
            dbs.append(_rowmap(lambda a: jnp.sum(a, axis=0, keepdims=True), [dpiece],
                               red_outs=[(1, dpiece.shape[1])], tr=512, name="colsum_" + n)[0])
    gw["w_in"] = jnp.concatenate(dws, axis=1)
    gs["b_in"] = jnp.concatenate(dbs, axis=1)[0]
    for n in ("sgu_ln_g", "sgu_ln_b", "d_skip", "b_glu", "ln1_g", "ln1_b", "ln2_g", "ln2_b"):
        gs[n] = gs[n][0]
    return dx, gw, gs, dbias


def _cast_bf16(w):
    w2 = w.reshape(-1, w.shape[-1])
    out = _rowmap(lambda a: a, [w2], row_outs=[(w2.shape[1], BF16)], tr=512, name="cast_bf16")[0]
    return out.reshape(w.shape)


def _static_consts():
    head_ones = np.kron(np.eye(HEADS_PER_GROUP, dtype=np.float32), np.ones((HEAD_DIM, HEAD_DIM), np.float32))
    group_sel = np.zeros((WIDTH_B, LANES), np.float32)
    group_sel[np.arange(WIDTH_B), np.arange(WIDTH_B) // CHUNK] = 1.0
    return {"head_ones": jnp.asarray(head_ones), "group_sel": jnp.asarray(group_sel)}


def _step(x, tgt, w, m, v):
    shapes = {n: w[n].shape for n in WEIGHTS}
    consts = _static_consts()
    full = {n: _assemble(n, _exchange(_cast_bf16(w[n]), gather=True, name="gather_" + n)) for n in SHARDED}
    small = {n: w[n] for n in SMALL}
    buckets = [jnp.asarray(_bucket_table(dil)) for dil in ATT_DILATIONS]
    biases = [_bias_fwd(w["rel_bias"], buckets[g], g) for g in range(len(ATT_DILATIONS))]
    params, saved = [], []
    h = x
    for l in range(DEPTH):
        p = _layer_params(l, full, small)
        h, r = _layer_fwd(h, p, biases)
        params.append(p)
        saved.append(r)
    dy, loss_part = _rowmap(_loss_fn, [h, tgt], row_outs=[(h.shape[1], F32)], red_outs=[(1, LANES)], name="loss")
    loss = lax.psum(loss_part[0, 0], MESH_AXES)
    gw_layers, gs_layers = [None] * DEPTH, [None] * DEPTH
    dbias_sum = None
    for l in reversed(range(DEPTH)):
        dy, gw_layers[l], gs_layers[l], dbias = _layer_bwd(dy, saved[l], params[l], biases, consts)
        saved[l] = None
        if dbias_sum is None:
            dbias_sum = dbias
        else:
            dbias_sum = [_rowmap(lambda a, b: a + b, [a.reshape(-1, 2 * ATT_BLOCK), b.reshape(-1, 2 * ATT_BLOCK)],
                                 row_outs=[(2 * ATT_BLOCK, F32)], name="dbias_add")[0].reshape(a.shape)
                         for a, b in zip(dbias_sum, dbias)]
    drel = [_bias_bwd(dbias_sum[g], buckets[g], g) for g in range(len(ATT_DILATIONS))]
    drel = _rowmap(lambda a, b, c: a + b + c, drel, row_outs=[(LANES, F32)], name="drel_add")[0]
    grad_small_local = {n: jnp.stack([gs_layers[l][n] for l in range(DEPTH)]) for n in SMALL if n != "rel_bias"}
    grad_small_local["rel_bias"] = drel[:, :shapes["rel_bias"][1]]
    out_g, out_d, out_m, out_v = {}, {}, {}, {}
    for n in SHARDED:
        g_full = jnp.stack([gw_layers[l][n] for l in range(DEPTH)])
        parts = _exchange(_split(n, g_full).astype(BF16), gather=False, name="scatter_" + n)
        cols = shapes[n][-1]
        res = _rowmap(lambda w_, m_, v_, parts_: _adamw_sum(w_, m_, v_, parts_),
                      [a.reshape(-1, cols) for a in (w[n], m[n], v[n])], stacks=[parts.reshape(N_DEV, -1, cols)],
                      row_outs=[(cols, F32)] * 4, tr=128, name="adamw_" + n)
        out_g[n], out_d[n], out_m[n], out_v[n] = [a.reshape(shapes[n]) for a in res]
    packed = _pack_small(grad_small_local)
    rows = packed.shape[0] // N_DEV
    parts = _exchange(packed.reshape(N_DEV, rows, LANES), gather=False, name="scatter_small")
    mine = _sum_parts(parts)
    g_small = _exchange(mine, gather=True, name="gather_small").reshape(-1, LANES)
    res = _rowmap(lambda w_, g_, m_, v_: _adamw(w_, g_, m_, v_),
                  [_pack_small(w), g_small, _pack_small(m), _pack_small(v)],
                  row_outs=[(LANES, F32)] * 3, name="adamw_small")
    small_shapes = {n: shapes[n] for n in SMALL}
    out_g.update(_unpack_small(g_small, small_shapes))
    for dst, packed_res in zip((out_d, out_m, out_v), res):
        dst.update(_unpack_small(packed_res, small_shapes))
    return loss, dy, out_g, out_d, out_m, out_v


def _sum_parts(parts):
    def body(p_ref, o_ref):
        acc = p_ref[0]
        for j in range(1, N_DEV):
            acc = acc + p_ref[j]
        o_ref[...] = acc

    _, rows, lanes = parts.shape
    tr = _pick(rows, 256, SUBLANES)
    return pl.pallas_call(
        body, grid=(rows // tr,), in_specs=[pl.BlockSpec((N_DEV, tr, lanes), lambda i: (0, i, 0))],
        out_specs=pl.BlockSpec((tr, lanes), lambda i: (i, 0)), out_shape=jax.ShapeDtypeStruct((rows, lanes), F32),
        compiler_params=_cparams("parallel"), name="sum_small")(parts)


def kernel(x, w_in, b_in, rel_bias, sgu_ln_g, sgu_ln_b, w_s, b_s, lam_re, lam_im, log_dt, b_re, b_im, c_re, c_im, d_skip, w_glu, b_glu, w_pa, w_pb, w_pc, w_o, ln1_g, ln1_b, w_ffn_in, w_ffn_out, ln2_g, ln2_b, loss_target, m_w_in, m_b_in, m_rel_bias, m_sgu_ln_g, m_sgu_ln_b, m_w_s, m_b_s, m_lam_re, m_lam_im, m_log_dt, m_b_re, m_b_im, m_c_re, m_c_im, m_d_skip, m_w_glu, m_b_glu, m_w_pa, m_w_pb, m_w_pc, m_w_o, m_ln1_g, m_ln1_b, m_w_ffn_in, m_w_ffn_out, m_ln2_g, m_ln2_b, v_w_in, v_b_in, v_rel_bias, v_sgu_ln_g, v_sgu_ln_b, v_w_s, v_b_s, v_lam_re, v_lam_im, v_log_dt, v_b_re, v_b_im, v_c_re, v_c_im, v_d_skip, v_w_glu, v_b_glu, v_w_pa, v_w_pb, v_w_pc, v_w_o, v_ln1_g, v_ln1_b, v_w_ffn_in, v_w_ffn_out, v_ln2_g, v_ln2_b):
    args = dict(locals())
    w = {n: args[n] for n in WEIGHTS}
    m = {n: args["m_" + n] for n in WEIGHTS}
    v = {n: args["v_" + n] for n in WEIGHTS}
    loss, dx, g, d, nm, nv = _step(x[0], loss_target[0], w, m, v)
    return (loss, dx[None], *[g[n] for n in WEIGHTS], *[d[n] for n in WEIGHTS],
            *[nm[n] for n in WEIGHTS], *[nv[n] for n in WEIGHTS])
```

```python
import functools
import math

import numpy as np
import jax
import jax.numpy as jnp
from jax import lax
from jax.experimental import pallas as pl
from jax.experimental.pallas import tpu as pltpu

F32 = jnp.float32
BF16 = jnp.bfloat16

MESH_AXES = ("x", "y", "c")
N_DEV = 8
DEPTH = 4

ATT_DILATIONS = (1, 4, 16)
ATT_STEPS = 128
HEADS_PER_GROUP = 8
HEAD_DIM = 64
QKV_WIDTH = 1536
WIDTH_A = HEADS_PER_GROUP * HEAD_DIM
ATT_BLOCK = 128
N_REL_BUCKETS = 32
REL_MAX_DIST = 2048
NEG_INF = -1e30
CHUNK = 128
WIDTH_B = 768
N_GROUPS_B = 6
WIDTH_C = 768
SSM_GROUP = 16
N_GROUPS_C = 48
SSM_STATE = 64
SSM_PACK = 8
N_SSM_BLOCKS = N_GROUPS_C // SSM_PACK
SSM_COLS = N_GROUPS_C * SSM_STATE
ALPHA = (2 * DEPTH) ** 0.25

ADAM_LR = 0.001
ADAM_B1 = 0.9
ADAM_B2 = 0.999
ADAM_EPS = 1e-08
ADAM_WD = 0.01
ADAM_STEP = 10

LANES = 128
SUBLANES = 8
VMEM_LIMIT = 48 * 1024 * 1024

SHARDED = ("w_in", "w_glu", "w_pa", "w_pb", "w_pc", "w_o", "w_ffn_in", "w_ffn_out")
TRANSPOSED = ("w_in", "w_pa", "w_pb", "w_pc", "w_ffn_in")
SMALL = ("b_in", "rel_bias", "sgu_ln_g", "sgu_ln_b", "w_s", "b_s", "lam_re", "lam_im", "log_dt",
         "b_re", "b_im", "c_re", "c_im", "d_skip", "b_glu", "ln1_g", "ln1_b", "ln2_g", "ln2_b")
WEIGHTS = ("w_in", "b_in", "rel_bias", "sgu_ln_g", "sgu_ln_b", "w_s", "b_s", "lam_re", "lam_im",
           "log_dt", "b_re", "b_im", "c_re", "c_im", "d_skip", "w_glu", "b_glu", "w_pa", "w_pb",
           "w_pc", "w_o", "ln1_g", "ln1_b", "w_ffn_in", "w_ffn_out", "ln2_g", "ln2_b")


def _pick(dim, target, mult):
    best = None
    for t in range(mult, min(dim, target) + 1, mult):
        if dim % t == 0:
            best = t
    return dim if best is None else best


def _cparams(*sem):
    return pltpu.CompilerParams(dimension_semantics=sem, vmem_limit_bytes=VMEM_LIMIT)


def _zero_map(ndim):
    return lambda *_: (0,) * ndim


def _rowmap(fn, rows, consts=(), stacks=(), row_outs=(), red_outs=(), tr=256, name=None):
    t = rows[0].shape[0] if rows else stacks[0].shape[1]
    dtypes = [a.dtype for a in (*rows, *stacks)] + [dt for _, dt in row_outs]
    packed = any(jnp.dtype(dt).itemsize < 4 for dt in dtypes)
    tr = _pick(t, tr, 2 * SUBLANES if packed else SUBLANES)
    n_r, n_c, n_s, n_o = len(rows), len(consts), len(stacks), len(row_outs)

    def body(*refs):
        ins = [r[...] for r in refs[:n_r + n_c + n_s]]
        outs = refs[n_r + n_c + n_s:n_r + n_c + n_s + n_o]
        reds = refs[n_r + n_c + n_s + n_o:]
        res = fn(*ins)
        if not isinstance(res, (tuple, list)):
            res = (res,)
        for o, v in zip(outs, res[:n_o]):
            o[...] = v.astype(o.dtype)
        if reds:
            @pl.when(pl.program_id(0) == 0)
            def _():
                for r in reds:
                    r[...] = jnp.zeros_like(r)
            for r, v in zip(reds, res[n_o:]):
                r[...] += v

    in_specs = [pl.BlockSpec((tr, r.shape[1]), lambda i: (i, 0)) for r in rows]
    in_specs += [pl.BlockSpec(c.shape, _zero_map(c.ndim)) for c in consts]
    in_specs += [pl.BlockSpec((s.shape[0], tr, s.shape[2]), lambda i: (0, i, 0)) for s in stacks]
    out_specs = [pl.BlockSpec((tr, w), lambda i: (i, 0)) for w, _ in row_outs]
    out_specs += [pl.BlockSpec(s, _zero_map(len(s))) for s in red_outs]
    out_shape = [jax.ShapeDtypeStruct((t, w), dt) for w, dt in row_outs]
    out_shape += [jax.ShapeDtypeStruct(s, F32) for s in red_outs]
    return pl.pallas_call(body, grid=(t // tr,), in_specs=in_specs, out_specs=out_specs, out_shape=out_shape,
                          compiler_params=_cparams("arbitrary"), name=name)(*rows, *consts, *stacks)


MM_VMEM_BUDGET = 36 * 1024 * 1024


def _divisors(dim, mult, must_divide=0):
    out = [t for t in range(dim, 0, -mult) if t % mult == 0 and dim % t == 0 and must_divide % t == 0]
    return out or [dim]


def _mm_tiles(m, n, k, a_bytes, b_bytes, out_bytes, extra_bytes, ta, b_off_n, b_off_k, tm, tn):
    tms = _divisors(m, LANES if ta else SUBLANES)
    tm = next((t for t in tms if t <= tm), tms[-1])
    tns = [t for t in _divisors(n, LANES, b_off_n) if t <= tn] or [_divisors(n, LANES, b_off_n)[-1]]
    for tn_ in tns:
        for tk in _divisors(k, LANES, b_off_k):
            acc = 0 if tk == k else tm * tn_ * 4
            need = 2 * (tm * tk * a_bytes + tk * tn_ * b_bytes + tm * tn_ * (out_bytes + extra_bytes)) + acc
            if need <= MM_VMEM_BUDGET:
                return tm, tn_, tk
    return tm, tns[-1], _divisors(k, LANES, b_off_k)[-1]


def _mm(a, b, *, ta=False, tb=False, bias=None, add=None, out_dtype=F32, b_off=0, n=None, tm=1024, tn=1024, name=None):
    k, m = a.shape if ta else a.shape[::-1]
    if tb:
        n = b.shape[0] if n is None else n
        assert b.shape[1] == k and b_off + n <= b.shape[0]
    else:
        n = b.shape[1]
        assert b_off + k <= b.shape[0]
    extra = 4 if add is not None else 0
    tm, tn, tk = _mm_tiles(m, n, k, a.dtype.itemsize, b.dtype.itemsize, jnp.dtype(out_dtype).itemsize, extra, ta,
                           b_off if tb else 0, 0 if tb else b_off, tm, tn)
    nk = k // tk
    off_n, off_k = (b_off // tn, 0) if tb else (0, b_off // tk)
    dims = (((0 if ta else 1,), (1 if tb else 0,)), ((), ()))

    def body(*refs):
        a_ref, b_ref = refs[0], refs[1]
        rest = list(refs[2:])
        bias_ref = rest.pop(0) if bias is not None else None
        add_ref = rest.pop(0) if add is not None else None
        o_ref = rest.pop(0)
        part = lax.dot_general(a_ref[...].astype(BF16), b_ref[...].astype(BF16), dims, preferred_element_type=F32)

        def finish(r):
            if bias_ref is not None:
                r = r + bias_ref[...]
            if add_ref is not None:
                r = r + add_ref[...]
            o_ref[...] = r.astype(o_ref.dtype)

        if nk == 1:
            finish(part)
        else:
            acc_ref = rest.pop(0)
            kk = pl.program_id(2)

            @pl.when(kk == 0)
            def _():
                acc_ref[...] = part

            @pl.when(kk > 0)
            def _():
                acc_ref[...] += part

            @pl.when(kk == nk - 1)
            def _():
                finish(acc_ref[...])

    a_spec = pl.BlockSpec((tk, tm), lambda i, j, q: (q, i)) if ta else pl.BlockSpec((tm, tk), lambda i, j, q: (i, q))
    if tb:
        b_spec = pl.BlockSpec((tn, tk), lambda i, j, q: (j + off_n, q))
    else:
        b_spec = pl.BlockSpec((tk, tn), lambda i, j, q: (q + off_k, j))
    in_specs, args = [a_spec, b_spec], [a, b]
    if bias is not None:
        in_specs.append(pl.BlockSpec((1, tn), lambda i, j, q: (0, j)))
        args.append(bias)
    if add is not None:
        in_specs.append(pl.BlockSpec((tm, tn), lambda i, j, q: (i, j)))
        args.append(add)
    return pl.pallas_call(
        body, grid=(m // tm, n // tn, nk), in_specs=in_specs,
        out_specs=pl.BlockSpec((tm, tn), lambda i, j, q: (i, j)),
        out_shape=jax.ShapeDtypeStruct((m, n), out_dtype),
        scratch_shapes=[] if nk == 1 else [pltpu.VMEM((tm, tn), F32)],
        compiler_params=_cparams("parallel", "parallel", "arbitrary"), name=name)(*args)


def _bdmm(a, bm, *, tb=False, scale=1.0, add=None, name=None):
    t = a.shape[0]
    nj, ka, kb = bm.shape
    kin, kout = (kb, ka) if tb else (ka, kb)
    tm = _pick(t, 512, SUBLANES)
    dims = (((1,), (1 if tb else 0,)), ((), ()))

    def body(*refs):
        a_ref, b_ref = refs[0], refs[1]
        add_ref = refs[2] if add is not None else None
        o_ref = refs[-1]
        r = lax.dot_general(a_ref[...].astype(BF16), b_ref[0].astype(BF16), dims, preferred_element_type=F32)
        if scale != 1.0:
            r = r * scale
        if add_ref is not None:
            r = r + add_ref[...]
        o_ref[...] = r

    in_specs = [pl.BlockSpec((tm, kin), lambda i, j: (i, j)), pl.BlockSpec((1, ka, kb), lambda i, j: (j, 0, 0))]
    args = [a, bm]
    if add is not None:
        in_specs.append(pl.BlockSpec((tm, kout), lambda i, j: (i, j)))
        args.append(add)
    return pl.pallas_call(
        body, grid=(t // tm, nj), in_specs=in_specs, out_specs=pl.BlockSpec((tm, kout), lambda i, j: (i, j)),
        out_shape=jax.ShapeDtypeStruct((t, nj * kout), F32),
        compiler_params=_cparams("parallel", "parallel"), name=name)(*args)


def _bdmm_tn(a, b, nj, *, scale=1.0, name=None):
    t = a.shape[0]
    ka, kb = a.shape[1] // nj, b.shape[1] // nj
    tm = _pick(t, 512, LANES)
    nt = t // tm

    def body(a_ref, b_ref, o_ref):
        @pl.when(pl.program_id(1) == 0)
        def _():
            o_ref[...] = jnp.zeros_like(o_ref)

        r = lax.dot_general(a_ref[...].astype(BF16), b_ref[...].astype(BF16), (((0,), (0,)), ((), ())),
                            preferred_element_type=F32)
        o_ref[0] += r if scale == 1.0 else r * scale

    return pl.pallas_call(
        body, grid=(nj, nt),
        in_specs=[pl.BlockSpec((tm, ka), lambda j, i: (i, j)), pl.BlockSpec((tm, kb), lambda j, i: (i, j))],
        out_specs=pl.BlockSpec((1, ka, kb), lambda j, i: (j, 0, 0)),
        out_shape=jax.ShapeDtypeStruct((nj, ka, kb), F32),
        compiler_params=_cparams("parallel", "arbitrary"), name=name)(a, b)


def _ln(x, g, b, eps=1e-5):
    mu = jnp.mean(x, axis=-1, keepdims=True)
    var = jnp.mean(jnp.square(x - mu), axis=-1, keepdims=True)
    return (x - mu) * lax.rsqrt(var + eps) * g + b


def _post_norm(x, f, g, b):
    return _ln(ALPHA * x + f, g, b)


def _post_norm_bwd(x, f, dy, g, b):
    _, vjp = jax.vjp(_post_norm, x, f, g, b)
    return vjp(dy)


def _merge3(g0, g1, g2, pa, pb, pc):
    return jax.nn.sigmoid(g0) * pa + jax.nn.sigmoid(g1) * pb + jax.nn.sigmoid(g2) * pc


def _merge(gl, pa, pb, pc):
    d = pa.shape[1]
    return _merge3(gl[:, :d], gl[:, d:2 * d], gl[:, 2 * d:], pa, pb, pc)


def _merge_bwd(gl, pa, pb, pc, dm):
    d = pa.shape[1]
    _, vjp = jax.vjp(_merge3, gl[:, :d], gl[:, d:2 * d], gl[:, 2 * d:], pa, pb, pc)
    d0, d1, d2, dpa, dpb, dpc = vjp(dm)
    dgl = jnp.concatenate([d0, d1, d2], axis=1)
    return dgl, dpa, dpb, dpc, jnp.sum(dgl, axis=0, keepdims=True)


def _swiglu2(gate, up):
    return jax.nn.silu(gate) * up


def _swiglu(gu):
    h = gu.shape[1] // 2
    return _swiglu2(gu[:, :h], gu[:, h:])


def _swiglu_bwd(gu, dact):
    h = gu.shape[1] // 2
    _, vjp = jax.vjp(_swiglu2, gu[:, :h], gu[:, h:])
    dg, du = vjp(dact)
    return jnp.concatenate([dg, du], axis=1)


def _glu(ycp, lin):
    return ycp * jax.nn.sigmoid(lin)


def _glu_bwd(ycp, lin, dyc):
    _, vjp = jax.vjp(_glu, ycp, lin)
    dycp, dlin = vjp(dyc)
    return dycp, dlin, jnp.sum(dlin, axis=0, keepdims=True)


def _s5_out(yre, yim, uc, dskip):
    ys = yre + yim + dskip * uc
    return ys, jax.nn.gelu(ys)


def _s5_out_bwd(ys, uc, dycp, dskip):
    _, vjp = jax.vjp(jax.nn.gelu, ys)
    dys = vjp(dycp)[0]
    return dys, dys * dskip, jnp.sum(dys * uc, axis=0, keepdims=True)


def _combine(o0, l0, o1, l1, o2, l2):
    m = jnp.maximum(jnp.maximum(l0, l1), l2)
    e0, e1, e2 = jnp.exp(l0 - m), jnp.exp(l1 - m), jnp.exp(l2 - m)
    s = e0 + e1 + e2
    return (e0 / s) * o0 + (e1 / s) * o1 + (e2 / s) * o2


def _combine_bwd(o0, l0, o1, l1, o2, l2, ya, dya, head_ones):
    m = jnp.maximum(jnp.maximum(l0, l1), l2)
    e0, e1, e2 = jnp.exp(l0 - m), jnp.exp(l1 - m), jnp.exp(l2 - m)
    s = e0 + e1 + e2
    dot_ya = jnp.dot(dya * ya, head_ones, precision=lax.Precision.HIGHEST, preferred_element_type=F32)
    w0, w1, w2 = e0 / s, e1 / s, e2 / s
    return w0 * dya, w1 * dya, w2 * dya, -w0 * dot_ya, -w1 * dot_ya, -w2 * dot_ya


def _loss_fn(y, tgt):
    err = y - tgt
    part = jnp.sum(jnp.sum(jnp.square(err), axis=1, keepdims=True), axis=0, keepdims=True) * (0.5 / y.shape[1])
    return err * (1.0 / y.shape[1]), jnp.broadcast_to(part, (1, LANES))


def _adamw(w, g, m, v):
    m = ADAM_B1 * m + (1.0 - ADAM_B1) * g
    v = ADAM_B2 * v + (1.0 - ADAM_B2) * jnp.square(g)
    m_hat = m / (1.0 - ADAM_B1 ** ADAM_STEP)
    v_hat = v / (1.0 - ADAM_B2 ** ADAM_STEP)
    delta = -ADAM_LR * (m_hat / (jnp.sqrt(v_hat) + ADAM_EPS) + ADAM_WD * w)
    return delta, m, v


def _sum_parts_fn(parts):
    g = parts[0].astype(F32)
    for j in range(1, parts.shape[0]):
        g = g + parts[j].astype(F32)
    return g


def _t5_bucket(dist):
    max_exact = N_REL_BUCKETS // 2
    d = np.maximum(dist, 1).astype(np.float32)
    scale = (N_REL_BUCKETS - max_exact) / math.log(REL_MAX_DIST / max_exact)
    large = max_exact + (np.log(d / max_exact) * scale).astype(np.int32)
    large = np.minimum(large, N_REL_BUCKETS - 1)
    return np.where(dist < max_exact, dist, large).astype(np.int32)


def _bucket_table(dilation):
    i = np.arange(ATT_BLOCK)[:, None]
    kk = np.arange(2 * ATT_BLOCK)[None, :]
    steps = ATT_BLOCK + i - kk
    return _t5_bucket(np.maximum(steps, 0) * dilation)


def _bias_fwd(rel_bias, buckets, g):
    def body(rel_ref, bk_ref, o_ref):
        bk = bk_ref[...]
        for h in range(HEADS_PER_GROUP):
            acc = jnp.zeros(bk.shape, F32)
            for b in range(N_REL_BUCKETS):
                acc = jnp.where(bk == b, rel_ref[b, g * HEADS_PER_GROUP + h], acc)
            o_ref[h] = acc

    return pl.pallas_call(
        body, in_specs=[pl.BlockSpec(memory_space=pltpu.SMEM), pl.BlockSpec(memory_space=pltpu.VMEM)],
        out_specs=pl.BlockSpec(memory_space=pltpu.VMEM),
        out_shape=jax.ShapeDtypeStruct((HEADS_PER_GROUP, ATT_BLOCK, 2 * ATT_BLOCK), F32),
        name=f"rel_bias_fwd{g}")(rel_bias, buckets)


def _bias_bwd(dbias, buckets, g):
    def body(db_ref, bk_ref, o_ref):
        bk = bk_ref[...]
        row = lax.broadcasted_iota(jnp.int32, (N_REL_BUCKETS, LANES), 0)
        col = lax.broadcasted_iota(jnp.int32, (N_REL_BUCKETS, LANES), 1)
        acc = jnp.zeros((N_REL_BUCKETS, LANES), F32)
        for h in range(HEADS_PER_GROUP):
            d = db_ref[h]
            for b in range(N_REL_BUCKETS):
                s = jnp.sum(jnp.sum(jnp.where(bk == b, d, 0.0), axis=1, keepdims=True), axis=0, keepdims=True)
                acc = acc + jnp.where((row == b) & (col == g * HEADS_PER_GROUP + h), s, 0.0)
        o_ref[...] = acc

    return pl.pallas_call(
        body, in_specs=[pl.BlockSpec(memory_space=pltpu.VMEM), pl.BlockSpec(memory_space=pltpu.VMEM)],
        out_specs=pl.BlockSpec(memory_space=pltpu.VMEM),
        out_shape=jax.ShapeDtypeStruct((N_REL_BUCKETS, LANES), F32), name=f"rel_bias_bwd{g}")(dbias, buckets)


_NT = (((1,), (1,)), ((), ()))
_TN = (((0,), (0,)), ((), ()))
_QKV_BLOCKS = 3 * QKV_WIDTH // WIDTH_A


def _band_mask(n_is_first):
    i = lax.broadcasted_iota(jnp.int32, (ATT_BLOCK, 2 * ATT_BLOCK), 0)
    kk = lax.broadcasted_iota(jnp.int32, (ATT_BLOCK, 2 * ATT_BLOCK), 1)
    return (kk >= i) & (kk <= i + ATT_STEPS) & ((kk >= ATT_BLOCK) | jnp.logical_not(n_is_first))


def _head(ref, h):
    return ref[:, h * HEAD_DIM:(h + 1) * HEAD_DIM]


def _attn_specs(g, d):
    blk = (ATT_BLOCK, WIDTH_A)
    q = pl.BlockSpec(blk, lambda c, n: (n, c * _QKV_BLOCKS + g))
    kp = pl.BlockSpec(blk, lambda c, n: (jnp.maximum(n - 1, 0), c * _QKV_BLOCKS + 3 + g))
    kc = pl.BlockSpec(blk, lambda c, n: (n, c * _QKV_BLOCKS + 3 + g))
    vp = pl.BlockSpec(blk, lambda c, n: (jnp.maximum(n - 1, 0), c * _QKV_BLOCKS + 6 + g))
    vc = pl.BlockSpec(blk, lambda c, n: (n, c * _QKV_BLOCKS + 6 + g))
    return [q, kp, kc, vp, vc]


def _attn_fwd(qkv, bias, g, d):
    t = qkv.shape[0]
    lq = t // d
    nb = lq // ATT_BLOCK
    scale = HEAD_DIM ** -0.5

    def body(q_ref, kp_ref, kc_ref, vp_ref, vc_ref, b_ref, o_ref, l_ref):
        mask = _band_mask(pl.program_id(1) == 0)
        for h in range(HEADS_PER_GROUP):
            qh = _head(q_ref, h).astype(BF16)
            kh = jnp.concatenate([_head(kp_ref, h), _head(kc_ref, h)], axis=0).astype(BF16)
            vh = jnp.concatenate([_head(vp_ref, h), _head(vc_ref, h)], axis=0).astype(BF16)
            s = lax.dot_general(qh, kh, _NT, preferred_element_type=F32) * scale + b_ref[h]
            s = jnp.where(mask, s, NEG_INF)
            m = jnp.max(s, axis=1, keepdims=True)
            p = jnp.exp(s - m)
            den = jnp.sum(p, axis=1, keepdims=True)
            o = jnp.dot(p.astype(BF16), vh, preferred_element_type=F32) / den
            o_ref[:, h * HEAD_DIM:(h + 1) * HEAD_DIM] = o
            l_ref[:, h * HEAD_DIM:(h + 1) * HEAD_DIM] = jnp.broadcast_to(m + jnp.log(den), (ATT_BLOCK, HEAD_DIM))

    out_spec = pl.BlockSpec((ATT_BLOCK, WIDTH_A), lambda c, n: (n, c))
    o, lse = pl.pallas_call(
        body, grid=(d, nb),
        in_specs=_attn_specs(g, d) + [pl.BlockSpec(bias.shape, _zero_map(3))],
        out_specs=[out_spec, out_spec],
        out_shape=[jax.ShapeDtypeStruct((lq, d * WIDTH_A), F32)] * 2,
        compiler_params=_cparams("parallel", "parallel"), name=f"attn_fwd{g}",
    )(*([qkv.reshape(lq, d * 3 * QKV_WIDTH)] * 5), bias)
    return o.reshape(t, WIDTH_A), lse.reshape(t, WIDTH_A)


def _attn_dq(qkv, bias, do, lse, corr, g, d):
    t = qkv.shape[0]
    lq = t // d
    nb = lq // ATT_BLOCK
    scale = HEAD_DIM ** -0.5

    def body(q_ref, kp_ref, kc_ref, vp_ref, vc_ref, b_ref, do_ref, l_ref, c_ref, dq_ref):
        mask = _band_mask(pl.program_id(1) == 0)
        for h in range(HEADS_PER_GROUP):
            qh = _head(q_ref, h).astype(BF16)
            kh = jnp.concatenate([_head(kp_ref, h), _head(kc_ref, h)], axis=0).astype(BF16)
            vh = jnp.concatenate([_head(vp_ref, h), _head(vc_ref, h)], axis=0).astype(BF16)
            s = lax.dot_general(qh, kh, _NT, preferred_element_type=F32) * scale + b_ref[h]
            s = jnp.where(mask, s, NEG_INF)
            p = jnp.exp(s - l_ref[:, h * HEAD_DIM:h * HEAD_DIM + 1])
            dp = lax.dot_general(_head(do_ref, h).astype(BF16), vh, _NT, preferred_element_type=F32)
            ds = p * (dp + c_ref[:, h * HEAD_DIM:h * HEAD_DIM + 1])
            dq_ref[:, h * HEAD_DIM:(h + 1) * HEAD_DIM] = jnp.dot(ds.astype(BF16), kh, preferred_element_type=F32) * scale

    row_spec = pl.BlockSpec((ATT_BLOCK, WIDTH_A), lambda c, n: (n, c))
    view = lambda a: a.reshape(lq, d * WIDTH_A)
    dq = pl.pallas_call(
        body, grid=(d, nb),
        in_specs=_attn_specs(g, d) + [pl.BlockSpec(bias.shape, _zero_map(3)), row_spec, row_spec, row_spec],
        out_specs=row_spec, out_shape=jax.ShapeDtypeStruct((lq, d * WIDTH_A), F32),
        compiler_params=_cparams("parallel", "parallel"), name=f"attn_dq{g}",
    )(*([qkv.reshape(lq, d * 3 * QKV_WIDTH)] * 5), bias, view(do), view(lse), view(corr))
    return dq.reshape(t, WIDTH_A)


def _attn_dkv(qkv, bias, do, lse, corr, g, d):
    t = qkv.shape[0]
    lq = t // d
    nb = lq // ATT_BLOCK
    scale = HEAD_DIM ** -0.5

    def body(k_ref, v_ref, q0_ref, q1_ref, do0_ref, do1_ref, l0_ref, l1_ref, c0_ref, c1_ref, b_ref,
             dk_ref, dv_ref, db_ref):
        c, j = pl.program_id(0), pl.program_id(1)

        @pl.when((c == 0) & (j == 0))
        def _():
            db_ref[...] = jnp.zeros_like(db_ref)

        i = lax.broadcasted_iota(jnp.int32, (ATT_BLOCK, ATT_BLOCK), 0)
        kk = lax.broadcasted_iota(jnp.int32, (ATT_BLOCK, ATT_BLOCK), 1)
        mask0 = kk <= i
        mask1 = (kk >= i) & (j + 1 < nb)
        for h in range(HEADS_PER_GROUP):
            kh = _head(k_ref, h).astype(BF16)
            vh = _head(v_ref, h).astype(BF16)
            dk = jnp.zeros((ATT_BLOCK, HEAD_DIM), F32)
            dv = jnp.zeros((ATT_BLOCK, HEAD_DIM), F32)
            parts = ((q0_ref, do0_ref, l0_ref, c0_ref, mask0, ATT_BLOCK), (q1_ref, do1_ref, l1_ref, c1_ref, mask1, 0))
            for q_ref, do_ref, l_ref, c_ref, mask, off in parts:
                qh = _head(q_ref, h).astype(BF16)
                doh = _head(do_ref, h).astype(BF16)
                s = lax.dot_general(qh, kh, _NT, preferred_element_type=F32) * scale + b_ref[h, :, off:off + ATT_BLOCK]
                s = jnp.where(mask, s, NEG_INF)
                p = jnp.exp(s - l_ref[:, h * HEAD_DIM:h * HEAD_DIM + 1])
                dp = lax.dot_general(doh, vh, _NT, preferred_element_type=F32)
                ds = p * (dp + c_ref[:, h * HEAD_DIM:h * HEAD_DIM + 1])
                dv = dv + lax.dot_general(p.astype(BF16), doh, _TN, preferred_element_type=F32)
                dk = dk + lax.dot_general(ds.astype(BF16), qh, _TN, preferred_element_type=F32)
                db_ref[h, :, off:off + ATT_BLOCK] += ds
            dk_ref[:, h * HEAD_DIM:(h + 1) * HEAD_DIM] = dk * scale
            dv_ref[:, h * HEAD_DIM:(h + 1) * HEAD_DIM] = dv

    blk = (ATT_BLOCK, WIDTH_A)
    nxt = lambda n: jnp.minimum(n + 1, nb - 1)
    k_spec = pl.BlockSpec(blk, lambda c, n: (n, c * _QKV_BLOCKS + 3 + g))
    v_spec = pl.BlockSpec(blk, lambda c, n: (n, c * _QKV_BLOCKS + 6 + g))
    q0_spec = pl.BlockSpec(blk, lambda c, n: (n, c * _QKV_BLOCKS + g))
    q1_spec = pl.BlockSpec(blk, lambda c, n: (nxt(n), c * _QKV_BLOCKS + g))
    r0 = pl.BlockSpec(blk, lambda c, n: (n, c))
    r1 = pl.BlockSpec(blk, lambda c, n: (nxt(n), c))
    view = lambda a: a.reshape(lq, d * WIDTH_A)
    qv = qkv.reshape(lq, d * 3 * QKV_WIDTH)
    dk, dv, dbias = pl.pallas_call(
        body, grid=(d, nb),
        in_specs=[k_spec, v_spec, q0_spec, q1_spec, r0, r1, r0, r1, r0, r1, pl.BlockSpec(bias.shape, _zero_map(3))],
        out_specs=[r0, r0, pl.BlockSpec(bias.shape, _zero_map(3))],
        out_shape=[jax.ShapeDtypeStruct((lq, d * WIDTH_A), F32)] * 2 + [jax.ShapeDtypeStruct(bias.shape, F32)],
        compiler_params=_cparams("arbitrary", "arbitrary"), name=f"attn_dkv{g}",
    )(qv, qv, qv, qv, view(do), view(do), view(lse), view(lse), view(corr), view(corr), bias)
    return dk.reshape(t, WIDTH_A), dv.reshape(t, WIDTH_A), dbias


def _tril_mask():
    r = lax.broadcasted_iota(jnp.int32, (CHUNK, CHUNK), 0)
    c = lax.broadcasted_iota(jnp.int32, (CHUNK, CHUNK), 1)
    return c <= r


def _gmlp_fwd(zb, ln_g, ln_b, w_s, b_s_t):
    t = zb.shape[0]
    tr = _pick(t, 2 * CHUNK, CHUNK)

    def body(z_ref, g_ref, b_ref, ws_ref, bs_ref, o_ref):
        tri = _tril_mask()
        z = jax.nn.gelu(z_ref[...])
        u = z[:, :WIDTH_B]
        vn = _ln(z[:, WIDTH_B:], g_ref[...], b_ref[...])
        for ch in range(tr // CHUNK):
            rows = slice(ch * CHUNK, (ch + 1) * CHUNK)
            for gi in range(N_GROUPS_B):
                cols = slice(gi * CHUNK, (gi + 1) * CHUNK)
                w = jnp.where(tri, ws_ref[gi], 0.0).astype(BF16)
                mixed = jnp.dot(w, vn[rows, cols].astype(BF16), preferred_element_type=F32) + bs_ref[:, gi:gi + 1]
                o_ref[rows, cols] = (u[rows, cols] * mixed).astype(o_ref.dtype)

    return pl.pallas_call(
        body, grid=(t // tr,),
        in_specs=[pl.BlockSpec((tr, 2 * WIDTH_B), lambda i: (i, 0)), pl.BlockSpec(ln_g.shape, _zero_map(2)),
                  pl.BlockSpec(ln_b.shape, _zero_map(2)), pl.BlockSpec(w_s.shape, _zero_map(3)),
                  pl.BlockSpec(b_s_t.shape, _zero_map(2))],
        out_specs=pl.BlockSpec((tr, WIDTH_B), lambda i: (i, 0)),
        out_shape=jax.ShapeDtypeStruct((t, WIDTH_B), BF16),
        compiler_params=_cparams("parallel"), name="gmlp_fwd")(zb, ln_g, ln_b, w_s, b_s_t)


def _gmlp_bwd(zb, dyb, ln_g, ln_b, w_s, b_s_t, group_sel):
    t = zb.shape[0]
    tr = _pick(t, 2 * CHUNK, CHUNK)

    def body(z_ref, dy_ref, g_ref, b_ref, ws_ref, bs_ref, sel_ref, dz_ref, dzs_ref, dg_ref, db_ref, dws_ref, dbs_ref,
             du_s, dvn_s, dm_s):
        @pl.when(pl.program_id(0) == 0)
        def _():
            dzs_ref[...] = jnp.zeros_like(dzs_ref)
            dg_ref[...] = jnp.zeros_like(dg_ref)
            db_ref[...] = jnp.zeros_like(db_ref)
            dws_ref[...] = jnp.zeros_like(dws_ref)
            dbs_ref[...] = jnp.zeros_like(dbs_ref)

        tri = _tril_mask()
        z, gelu_vjp = jax.vjp(jax.nn.gelu, z_ref[...])
        u = z[:, :WIDTH_B]
        vn, ln_vjp = jax.vjp(_ln, z[:, WIDTH_B:], g_ref[...], b_ref[...])
        dy = dy_ref[...]
        for ch in range(tr // CHUNK):
            rows = slice(ch * CHUNK, (ch + 1) * CHUNK)
            for gi in range(N_GROUPS_B):
                cols = slice(gi * CHUNK, (gi + 1) * CHUNK)
                w = jnp.where(tri, ws_ref[gi], 0.0).astype(BF16)
                vg = vn[rows, cols].astype(BF16)
                mixed = jnp.dot(w, vg, preferred_element_type=F32) + bs_ref[:, gi:gi + 1]
                dyg = dy[rows, cols]
                dm = dyg * u[rows, cols]
                dmb = dm.astype(BF16)
                du_s[rows, cols] = dyg * mixed
                dm_s[rows, cols] = dm
                dvn_s[rows, cols] = lax.dot_general(w, dmb, _TN, preferred_element_type=F32)
                dws_ref[gi] += jnp.where(tri, lax.dot_general(dmb, vg, _NT, preferred_element_type=F32), 0.0)
            dbs_ref[...] += jnp.dot(dm_s[rows, :], sel_ref[...], precision=lax.Precision.HIGHEST,
                                    preferred_element_type=F32)
        dv, dg, db = ln_vjp(dvn_s[...])
        dg_ref[...] += dg
        db_ref[...] += db
        dz = gelu_vjp(jnp.concatenate([du_s[...], dv], axis=1))[0]
        dz_ref[...] = dz.astype(dz_ref.dtype)
        dzs_ref[...] += jnp.sum(dz, axis=0, keepdims=True)

    full = lambda a: pl.BlockSpec(a.shape, _zero_map(a.ndim))
    return pl.pallas_call(
        body, grid=(t // tr,),
        in_specs=[pl.BlockSpec((tr, 2 * WIDTH_B), lambda i: (i, 0)), pl.BlockSpec((tr, WIDTH_B), lambda i: (i, 0)),
                  full(ln_g), full(ln_b), full(w_s), full(b_s_t), full(group_sel)],
        out_specs=[pl.BlockSpec((tr, 2 * WIDTH_B), lambda i: (i, 0)), pl.BlockSpec((1, 2 * WIDTH_B), _zero_map(2)),
                   full(ln_g), full(ln_b), full(w_s), pl.BlockSpec((CHUNK, LANES), _zero_map(2))],
        out_shape=[jax.ShapeDtypeStruct((t, 2 * WIDTH_B), BF16), jax.ShapeDtypeStruct((1, 2 * WIDTH_B), F32),
                   jax.ShapeDtypeStruct(ln_g.shape, F32),
                   jax.ShapeDtypeStruct(ln_b.shape, F32), jax.ShapeDtypeStruct(w_s.shape, F32),
                   jax.ShapeDtypeStruct((CHUNK, LANES), F32)],
        scratch_shapes=[pltpu.VMEM((tr, WIDTH_B), F32)] * 3,
        compiler_params=_cparams("arbitrary"), name="gmlp_bwd")(zb, dyb, ln_g, ln_b, w_s, b_s_t, group_sel)


def _s5_disc(lr, li, ldt, br_t, bi_t):
    dt = jnp.exp(ldt)
    mag = jnp.exp(lr * dt)
    ab_re = mag * jnp.cos(li * dt)
    ab_im = mag * jnp.sin(li * dt)
    nrm = lr * lr + li * li
    cr = ((ab_re - 1.0) * lr + ab_im * li) / nrm
    ci = (ab_im * lr - (ab_re - 1.0) * li) / nrm
    return ab_re, ab_im, cr * br_t - ci * bi_t, cr * bi_t + ci * br_t


def _vmem_call(fn, args, out_shape, name):
    def body(*refs):
        res = fn(*[r[...] for r in refs[:len(args)]])
        for o, v in zip(refs[len(args):], res):
            o[...] = v

    vm = pl.BlockSpec(memory_space=pltpu.VMEM)
    return pl.pallas_call(body, in_specs=[vm] * len(args), out_specs=[vm] * len(out_shape),
                          out_shape=out_shape, name=name)(*args)


def _s5_disc_fwd(lr, li, ldt, br_t, bi_t):
    s1 = jax.ShapeDtypeStruct(lr.shape, F32)
    s2 = jax.ShapeDtypeStruct(br_t.shape, F32)
    return _vmem_call(_s5_disc, [lr, li, ldt, br_t, bi_t], [s1, s1, s2, s2], "s5_disc_fwd")


def _s5_disc_bwd(lr, li, ldt, br_t, bi_t, cts):
    def fn(lr, li, ldt, br_t, bi_t, d0, d1, d2, d3):
        _, vjp = jax.vjp(_s5_disc, lr, li, ldt, br_t, bi_t)
        return vjp((d0, d1, d2, d3))

    shp = [jax.ShapeDtypeStruct(a.shape, F32) for a in (lr, li, ldt, br_t, bi_t)]
    return _vmem_call(fn, [lr, li, ldt, br_t, bi_t, *cts], shp, "s5_disc_bwd")


_SCAN_ROWS = SSM_COLS // LANES
_SCAN_CHUNK = 256


def _scan_fwd(bre, bim, are, aim):
    t = bre.shape[0]
    tc = _pick(t, _SCAN_CHUNK, SUBLANES)

    def body(br_ref, bi_ref, ar_ref, ai_ref, xr_ref, xi_ref, st_ref):
        @pl.when(pl.program_id(0) == 0)
        def _():
            st_ref[...] = jnp.zeros_like(st_ref)

        ar, ai = ar_ref[...], ai_ref[...]

        def step(i, carry):
            xr, xi = carry
            nr = ar * xr - ai * xi + br_ref[i]
            ni = ar * xi + ai * xr + bi_ref[i]
            xr_ref[i] = nr
            xi_ref[i] = ni
            return nr, ni

        xr, xi = lax.fori_loop(0, tc, step, (st_ref[0], st_ref[1]), unroll=8)
        st_ref[0] = xr
        st_ref[1] = xi

    blk = pl.BlockSpec((tc, _SCAN_ROWS, LANES), lambda i: (i, 0, 0))
    par = pl.BlockSpec((_SCAN_ROWS, LANES), _zero_map(2))
    shp = jax.ShapeDtypeStruct(bre.shape, F32)
    return pl.pallas_call(
        body, grid=(t // tc,), in_specs=[blk, blk, par, par], out_specs=[blk, blk], out_shape=[shp, shp],
        scratch_shapes=[pltpu.VMEM((2, _SCAN_ROWS, LANES), F32)],
        compiler_params=_cparams("arbitrary"), name="s5_scan_fwd")(bre, bim, are, aim)


def _scan_bwd(dxr, dxi, xr, xi, are, aim):
    t = dxr.shape[0]
    tc = _pick(t, _SCAN_CHUNK, SUBLANES)
    nc = t // tc

    def body(dr_ref, di_ref, xr_ref, xi_ref, pr_ref, pi_ref, ar_ref, ai_ref, gr_ref, gi_ref, dar_ref, dai_ref, st_ref):
        step_id = pl.program_id(0)

        @pl.when(step_id == 0)
        def _():
            st_ref[...] = jnp.zeros_like(st_ref)
            dar_ref[...] = jnp.zeros_like(dar_ref)
            dai_ref[...] = jnp.zeros_like(dai_ref)

        ar, ai = ar_ref[...], ai_ref[...]

        def update(i, carry, pxr, pxi):
            gr, gi, dar, dai = carry
            ngr = dr_ref[i] + ar * gr + ai * gi
            ngi = di_ref[i] - ai * gr + ar * gi
            gr_ref[i] = ngr
            gi_ref[i] = ngi
            return ngr, ngi, dar + ngr * pxr + ngi * pxi, dai - ngr * pxi + ngi * pxr

        def step(s, carry):
            i = tc - 1 - s
            return update(i, carry, xr_ref[i - 1], xi_ref[i - 1])

        zero = jnp.zeros((_SCAN_ROWS, LANES), F32)
        carry = lax.fori_loop(0, tc - 1, step, (st_ref[0], st_ref[1], zero, zero), unroll=8)
        has_prev = (step_id < nc - 1).astype(F32)
        gr, gi, dar, dai = update(0, carry, pr_ref[0] * has_prev, pi_ref[0] * has_prev)
        st_ref[0] = gr
        st_ref[1] = gi
        dar_ref[...] += dar
        dai_ref[...] += dai

    blk = pl.BlockSpec((tc, _SCAN_ROWS, LANES), lambda i: (nc - 1 - i, 0, 0))
    prev = pl.BlockSpec((1, _SCAN_ROWS, LANES), lambda i: (jnp.maximum((nc - 1 - i) * tc - 1, 0), 0, 0))
    par = pl.BlockSpec((_SCAN_ROWS, LANES), _zero_map(2))
    shp = jax.ShapeDtypeStruct(dxr.shape, F32)
    psh = jax.ShapeDtypeStruct((_SCAN_ROWS, LANES), F32)
    return pl.pallas_call(
        body, grid=(nc,), in_specs=[blk, blk, blk, blk, prev, prev, par, par],
        out_specs=[blk, blk, par, par], out_shape=[shp, shp, psh, psh],
        scratch_shapes=[pltpu.VMEM((2, _SCAN_ROWS, LANES), F32)],
        compiler_params=_cparams("arbitrary"), name="s5_scan_bwd")(dxr, dxi, xr, xi, xr, xi, are, aim)


def _block_diag(m):
    _, a, b = m.shape
    m4 = m.reshape(N_SSM_BLOCKS, SSM_PACK, a, b)
    eye = jnp.eye(SSM_PACK, dtype=m.dtype)
    return (m4[:, :, :, None, :] * eye[None, :, None, :, None]).reshape(N_SSM_BLOCKS, SSM_PACK * a, SSM_PACK * b)


def _block_diag_extract(m, a, b):
    blocks = [m[:, i * a:(i + 1) * a, i * b:(i + 1) * b] for i in range(SSM_PACK)]
    return jnp.stack(blocks, axis=1).reshape(N_GROUPS_C, a, b)


def _exchange(src, *, gather, name):
    shape = src.shape if gather else src.shape[1:]

    def body(src_ref, out_ref, send_sems, recv_sems, local_sem):
        x, y, c = lax.axis_index("x"), lax.axis_index("y"), lax.axis_index("c")
        me = 4 * x + 2 * y + c
        copies = []
        for r in range(1, N_DEV):
            px = 1 - x if r & 4 else x
            py = 1 - y if r & 2 else y
            pc = 1 - c if r & 1 else c
            piece = src_ref if gather else src_ref.at[4 * px + 2 * py + pc]
            cp = pltpu.make_async_remote_copy(
                src_ref=piece, dst_ref=out_ref.at[me], send_sem=send_sems.at[r - 1], recv_sem=recv_sems.at[r - 1],
                device_id=(px, py, pc), device_id_type=pl.DeviceIdType.MESH)
            cp.start()
            copies.append(cp)
        mine = pltpu.make_async_copy(src_ref if gather else src_ref.at[me], out_ref.at[me], local_sem)
        mine.start()
        for cp in copies:
            cp.wait()
        mine.wait()

    hbm = pl.BlockSpec(memory_space=pl.ANY)
    return pl.pallas_call(
        body, in_specs=[hbm], out_specs=hbm, out_shape=jax.ShapeDtypeStruct((N_DEV,) + tuple(shape), src.dtype),
        scratch_shapes=[pltpu.SemaphoreType.DMA((N_DEV - 1,)), pltpu.SemaphoreType.DMA((N_DEV - 1,)),
                        pltpu.SemaphoreType.DMA(())],
        name=name)(src)


_HBM = pl.BlockSpec(memory_space=pl.ANY)
_MESH_ID = pl.DeviceIdType.MESH


def _mesh_place():
    x, y, c = lax.axis_index("x"), lax.axis_index("y"), lax.axis_index("c")
    other_chips = [(1 - x, y), (x, 1 - y), (1 - x, 1 - y)]
    return x, y, c, other_chips


def _gather_layer(srcs, layer, name):
    n = len(srcs)

    def body(*refs):
        src = [r.at[layer] for r in refs[:n]]
        out = refs[n:2 * n]
        send_sems, recv_sems, local_sems = refs[2 * n:]
        x, y, c, chips = _mesh_place()
        me, sibling = (x, y, c), (x, y, 1 - c)

        def copy(t, k, block, to, from_src=False):
            slot = 4 * block[0] + 2 * block[1] + block[2]
            return pltpu.make_async_remote_copy(
                src_ref=src[t] if from_src else out[t].at[slot], dst_ref=out[t].at[slot],
                send_sem=send_sems.at[t, k], recv_sem=recv_sems.at[t, k], device_id=to, device_id_type=_MESH_ID)

        mine = [pltpu.make_async_copy(src[t], out[t].at[4 * x + 2 * y + c], local_sems.at[t]) for t in range(n)]
        for cp in mine:
            cp.start()
        first = []
        for t in range(n):
            first.append(copy(t, 0, me, sibling, True))
            first += [copy(t, 1 + j, me, (*chip, c), True) for j, chip in enumerate(chips)]
        for cp in first:
            cp.start()
        passed = []
        for j, chip in enumerate(chips):
            for t in range(n):
                copy(t, 1 + j, (*chip, c), me).wait_recv()
                fwd = copy(t, 4 + j, (*chip, c), sibling)
                fwd.start()
                passed.append(fwd)
        for t in range(n):
            copy(t, 0, sibling, me).wait_recv()
            for j, chip in enumerate(chips):
                copy(t, 4 + j, (*chip, 1 - c), me).wait_recv()
        for cp in first + passed:
            cp.wait_send()
        for cp in mine:
            cp.wait()

    return pl.pallas_call(
        body, in_specs=[_HBM] * n, out_specs=[_HBM] * n,
        out_shape=[jax.ShapeDtypeStruct((N_DEV,) + s.shape[1:], s.dtype) for s in srcs],
        scratch_shapes=[pltpu.SemaphoreType.DMA((n, N_DEV - 1)), pltpu.SemaphoreType.DMA((n, N_DEV - 1)),
                        pltpu.SemaphoreType.DMA((n,))],
        name=name)(*srcs)


def _scatter_pair(srcs, name):
    n = len(srcs)

    def body(*refs):
        src, out = refs[:n], refs[n:2 * n]
        send_sems, recv_sems = refs[2 * n:]
        x, y, c, _ = _mesh_place()
        copies = [pltpu.make_async_remote_copy(
            src_ref=src[t].at[:, 1 - c], dst_ref=out[t], send_sem=send_sems.at[t], recv_sem=recv_sems.at[t],
            device_id=(x, y, 1 - c), device_id_type=_MESH_ID) for t in range(n)]
        for cp in copies:
            cp.start()
        for cp in copies:
            cp.wait()

    return pl.pallas_call(
        body, in_specs=[_HBM] * n, out_specs=[_HBM] * n,
        out_shape=[jax.ShapeDtypeStruct((s.shape[0],) + s.shape[2:], s.dtype) for s in srcs],
        scratch_shapes=[pltpu.SemaphoreType.DMA((n,)), pltpu.SemaphoreType.DMA((n,))], name=name)(*srcs)


def _pair_add(src, recv, name):
    nchip, _, r, cdim = src.shape
    tr = _pick(r, 256, 2 * SUBLANES)
    core = lax.axis_index("c").astype(jnp.int32).reshape(1)

    def body(core_ref, s_ref, r_ref, o_ref):
        o_ref[...] = (s_ref[...].astype(F32) + r_ref[...].astype(F32)).astype(o_ref.dtype)

    grid_spec = pltpu.PrefetchScalarGridSpec(
        num_scalar_prefetch=1, grid=(nchip, r // tr),
        in_specs=[pl.BlockSpec((None, None, tr, cdim), lambda k, i, core_ref: (k, core_ref[0], i, 0)),
                  pl.BlockSpec((None, tr, cdim), lambda k, i, core_ref: (k, i, 0))],
        out_specs=pl.BlockSpec((None, tr, cdim), lambda k, i, core_ref: (k, i, 0)))
    return pl.pallas_call(body, grid_spec=grid_spec, out_shape=jax.ShapeDtypeStruct(recv.shape, recv.dtype),
                          compiler_params=_cparams("parallel", "parallel"), name=name)(core, src, recv)


def _scatter_chips(srcs, name):
    n = len(srcs)

    def body(*refs):
        src, out = refs[:n], refs[n:2 * n]
        send_sems, recv_sems, local_sems = refs[2 * n:]
        x, y, c, chips = _mesh_place()
        my_chip = 2 * x + y
        mine = [pltpu.make_async_copy(src[t].at[my_chip], out[t].at[my_chip], local_sems.at[t]) for t in range(n)]
        copies = [pltpu.make_async_remote_copy(
            src_ref=src[t].at[2 * chip[0] + chip[1]], dst_ref=out[t].at[my_chip],
            send_sem=send_sems.at[t, j], recv_sem=recv_sems.at[t, j], device_id=(*chip, c), device_id_type=_MESH_ID)
            for t in range(n) for j, chip in enumerate(chips)]
        for cp in mine + copies:
            cp.start()
        for cp in copies + mine:
            cp.wait()

    return pl.pallas_call(
        body, in_specs=[_HBM] * n, out_specs=[_HBM] * n,
        out_shape=[jax.ShapeDtypeStruct(s.shape, s.dtype) for s in srcs],
        scratch_shapes=[pltpu.SemaphoreType.DMA((n, 3)), pltpu.SemaphoreType.DMA((n, 3)), pltpu.SemaphoreType.DMA((n,))],
        name=name)(*srcs)


def _reduce_scatter_layer(grads, tag):
    views = [g.reshape(N_DEV // 2, 2, g.shape[0] // N_DEV, g.shape[1]) for g in grads]
    from_sibling = _scatter_pair(views, name="scatter_pair_" + tag)
    chip_sums = [_pair_add(v, s, name="pair_add") for v, s in zip(views, from_sibling)]
    parts = _scatter_chips(chip_sums, name="scatter_chips_" + tag)
    return [_rowmap(_sum_parts_fn, [], stacks=[p], row_outs=[(p.shape[2], F32)], tr=128, name="sum_chips")[0]
            for p in parts]


def _small_sizes(shapes):
    return [int(np.prod(shapes[n])) for n in SMALL]


def _pack_small(vals):
    flat = jnp.concatenate([vals[n].reshape(-1).astype(F32) for n in SMALL])
    rows = -(-flat.shape[0] // (LANES * N_DEV * SUBLANES)) * (N_DEV * SUBLANES)
    return jnp.pad(flat, (0, rows * LANES - flat.shape[0])).reshape(rows, LANES)


def _unpack_small(packed, shapes):
    flat = packed.reshape(-1)
    out, off = {}, 0
    for n, size in zip(SMALL, _small_sizes(shapes)):
        out[n] = flat[off:off + size].reshape(shapes[n])
        off += size
    return out


def _row(v):
    return v.reshape(1, -1)


def _layer_params(l, full, small):
    o1, o2, o3 = 3 * QKV_WIDTH, 3 * QKV_WIDTH + 2 * WIDTH_B, 3 * QKV_WIDTH + 2 * WIDTH_B + WIDTH_C
    b_in = small["b_in"][l]
    p = {
        "in_pieces": (("qkv", 0, o1), ("zb", o1, o2 - o1), ("uc", o2, o3 - o2), ("gl", o3, b_in.shape[0] - o3)),
        "b_qkv": _row(b_in[:o1]), "b_zb": _row(b_in[o1:o2]), "b_uc": _row(b_in[o2:o3]), "b_gl": _row(b_in[o3:]),
        "sgu_ln_g": _row(small["sgu_ln_g"][l]), "sgu_ln_b": _row(small["sgu_ln_b"][l]),
        "w_s": small["w_s"][l], "b_s_t": small["b_s"][l].T,
        "lam_re": small["lam_re"][l][:, None, :], "lam_im": small["lam_im"][l][:, None, :],
        "log_dt": small["log_dt"][l][:, None, None],
        "b_re_t": small["b_re"][l].transpose(0, 2, 1), "b_im_t": small["b_im"][l].transpose(0, 2, 1),
        "c_re_t": small["c_re"][l].transpose(0, 2, 1), "c_im_t": small["c_im"][l].transpose(0, 2, 1),
        "d_skip": _row(small["d_skip"][l]), "b_glu": _row(small["b_glu"][l]),
        "ln1_g": _row(small["ln1_g"][l]), "ln1_b": _row(small["ln1_b"][l]),
        "ln2_g": _row(small["ln2_g"][l]), "ln2_b": _row(small["ln2_b"][l]),
    }
    for n in SHARDED:
        p[n] = full[n]
    return p


def _scan_view(a):
    return a.reshape(a.shape[0], _SCAN_ROWS, LANES)


def _twice(fn):
    def both(*args):
        y = fn(*args)
        return y, y
    return both


def _layer_fwd(x, xb, p, biases):
    t, d = x.shape
    r = {"x": x, "xb": xb}
    for piece, off, n in p["in_pieces"]:
        r[piece] = _mm(xb, p["w_in"], tb=True, b_off=off, n=n, bias=p["b_" + piece],
                       out_dtype=BF16 if piece == "qkv" else F32, name="mm_in_" + piece)
    ol = []
    for g, dil in enumerate(ATT_DILATIONS):
        ol += list(_attn_fwd(r["qkv"], biases[g], g, dil))
    r["ol"] = ol
    r["ya"], r["ya_b"] = _rowmap(_twice(_combine), ol, row_outs=[(WIDTH_A, F32), (WIDTH_A, BF16)], tr=512,
                                 name="attn_combine")
    r["yb"] = _gmlp_fwd(r["zb"], p["sgu_ln_g"], p["sgu_ln_b"], p["w_s"], p["b_s_t"])
    ab_re, ab_im, bb_re_t, bb_im_t = _s5_disc_fwd(p["lam_re"], p["lam_im"], p["log_dt"], p["b_re_t"], p["b_im_t"])
    r["a_re"], r["a_im"] = ab_re.reshape(_SCAN_ROWS, LANES), ab_im.reshape(_SCAN_ROWS, LANES)
    r["bmat_re"], r["bmat_im"] = _block_diag(bb_re_t), _block_diag(bb_im_t)
    r["cmat_re"], r["cmat_im"] = _block_diag(p["c_re_t"]), _block_diag(p["c_im_t"])
    bu_re = _bdmm(r["uc"], r["bmat_re"], name="s5_in_re")
    bu_im = _bdmm(r["uc"], r["bmat_im"], name="s5_in_im")
    xr, xi = _scan_fwd(_scan_view(bu_re), _scan_view(bu_im), r["a_re"], r["a_im"])
    r["xr"], r["xi"] = xr.reshape(t, SSM_COLS), xi.reshape(t, SSM_COLS)
    y_re = _bdmm(r["xr"], r["cmat_re"], name="s5_out_re")
    y_im = _bdmm(r["xi"], r["cmat_im"], scale=-1.0, name="s5_out_im")
    r["ys"], r["ycp"] = _rowmap(_s5_out, [y_re, y_im, r["uc"]], consts=[p["d_skip"]],
                                row_outs=[(WIDTH_C, F32)] * 2, tr=512, name="s5_out_act")
    r["glin"] = _mm(r["ycp"], p["w_glu"], bias=p["b_glu"], name="mm_glu")
    r["yc"] = _rowmap(_glu, [r["ycp"], r["glin"]], row_outs=[(WIDTH_C, BF16)], tr=512, name="glu")[0]
    r["pa"] = _mm(r["ya_b"], p["w_pa"], tb=True, name="mm_pa")
    r["pb"] = _mm(r["yb"], p["w_pb"], tb=True, name="mm_pb")
    r["pc"] = _mm(r["yc"], p["w_pc"], tb=True, name="mm_pc")
    r["merged"] = _rowmap(_merge, [r["gl"], r["pa"], r["pb"], r["pc"]], row_outs=[(d, BF16)], name="merge")[0]
    r["mo"] = _mm(r["merged"], p["w_o"], name="mm_o")
    r["xm"], r["xm_b"] = _rowmap(_twice(_post_norm), [x, r["mo"]], consts=[p["ln1_g"], p["ln1_b"]],
                                 row_outs=[(d, F32), (d, BF16)], name="norm1")
    r["gu"] = _mm(r["xm_b"], p["w_ffn_in"], tb=True, name="mm_ffn_in")
    r["act"] = _rowmap(_swiglu, [r["gu"]], row_outs=[(r["gu"].shape[1] // 2, BF16)], tr=128, name="swiglu")[0]
    r["f"] = _mm(r["act"], p["w_ffn_out"], name="mm_ffn_out")
    out, out_b = _rowmap(_twice(_post_norm), [r["xm"], r["f"]], consts=[p["ln2_g"], p["ln2_b"]],
                         row_outs=[(d, F32), (d, BF16)], name="norm2")
    return out, out_b, r


def _layer_bwd(dout, r, p, biases, consts):
    t, d = dout.shape
    gw, gs = {}, {}
    ffw = r["gu"].shape[1]
    dxm, df, gs["ln2_g"], gs["ln2_b"] = _rowmap(
        _post_norm_bwd, [r["xm"], r["f"], dout], consts=[p["ln2_g"], p["ln2_b"]],
        row_outs=[(d, F32), (d, BF16)], red_outs=[(1, d)] * 2, name="norm2_bwd")
    gw["w_ffn_out"] = _mm(r["act"], df, ta=True, out_dtype=BF16, name="mm_ffn_out_dw")
    dact = _mm(df, p["w_ffn_out"], tb=True, name="mm_ffn_out_dx")
    dgu = _rowmap(_swiglu_bwd, [r["gu"], dact], row_outs=[(ffw, BF16)], tr=128, name="swiglu_bwd")[0]
    gw["w_ffn_in"] = _mm(dgu, r["xm_b"], ta=True, out_dtype=BF16, name="mm_ffn_in_dw")
    dxm = _mm(dgu, p["w_ffn_in"], add=dxm, name="mm_ffn_in_dx")
    dx, dmo, gs["ln1_g"], gs["ln1_b"] = _rowmap(
        _post_norm_bwd, [r["x"], r["mo"], dxm], consts=[p["ln1_g"], p["ln1_b"]],
        row_outs=[(d, F32), (d, BF16)], red_outs=[(1, d)] * 2, name="norm1_bwd")
    gw["w_o"] = _mm(r["merged"], dmo, ta=True, out_dtype=BF16, name="mm_o_dw")
    dmerged = _mm(dmo, p["w_o"], tb=True, name="mm_o_dx")
    dgl, dpa, dpb, dpc, db_gl = _rowmap(
        _merge_bwd, [r["gl"], r["pa"], r["pb"], r["pc"], dmerged],
        row_outs=[(3 * d, BF16), (d, BF16), (d, BF16), (d, BF16)], red_outs=[(1, 3 * d)], tr=128, name="merge_bwd")
    gw["w_pa"] = _mm(dpa, r["ya_b"], ta=True, out_dtype=BF16, name="mm_pa_dw")
    gw["w_pb"] = _mm(dpb, r["yb"], ta=True, out_dtype=BF16, name="mm_pb_dw")
    gw["w_pc"] = _mm(dpc, r["yc"], ta=True, out_dtype=BF16, name="mm_pc_dw")
    dya = _mm(dpa, p["w_pa"], name="mm_pa_dx")
    dyb = _mm(dpb, p["w_pb"], name="mm_pb_dx")
    dyc = _mm(dpc, p["w_pc"], name="mm_pc_dx")
    dycp, dglin, gs["b_glu"] = _rowmap(_glu_bwd, [r["ycp"], r["glin"], dyc], row_outs=[(WIDTH_C, F32), (WIDTH_C, BF16)],
                                       red_outs=[(1, WIDTH_C)], tr=512, name="glu_bwd")
    gw["w_glu"] = _mm(r["ycp"], dglin, ta=True, out_dtype=BF16, name="mm_glu_dw")
    dycp = _mm(dglin, p["w_glu"], tb=True, add=dycp, name="mm_glu_dx")
    dys, duc, gs["d_skip"] = _rowmap(_s5_out_bwd, [r["ys"], r["uc"], dycp], consts=[p["d_skip"]],
                                     row_outs=[(WIDTH_C, F32)] * 2, red_outs=[(1, WIDTH_C)], tr=512, name="s5_out_act_bwd")
    dxr = _bdmm(dys, r["cmat_re"], tb=True, name="s5_out_re_dx")
    dxi = _bdmm(dys, r["cmat_im"], tb=True, scale=-1.0, name="s5_out_im_dx")
    d_cmat_re = _bdmm_tn(r["xr"], dys, N_SSM_BLOCKS, name="s5_out_re_dw")
    d_cmat_im = _bdmm_tn(r["xi"], dys, N_SSM_BLOCKS, scale=-1.0, name="s5_out_im_dw")
    g_re, g_im, da_re, da_im = _scan_bwd(_scan_view(dxr), _scan_view(dxi), _scan_view(r["xr"]), _scan_view(r["xi"]),
                                         r["a_re"], r["a_im"])
    g_re, g_im = g_re.reshape(t, SSM_COLS), g_im.reshape(t, SSM_COLS)
    duc = _bdmm(g_re, r["bmat_re"], tb=True, add=duc, name="s5_in_re_dx")
    duc = _bdmm(g_im, r["bmat_im"], tb=True, add=duc, name="s5_in_im_dx")
    d_bmat_re = _bdmm_tn(r["uc"], g_re, N_SSM_BLOCKS, name="s5_in_re_dw")
    d_bmat_im = _bdmm_tn(r["uc"], g_im, N_SSM_BLOCKS, name="s5_in_im_dw")
    cts = (da_re.reshape(N_GROUPS_C, 1, SSM_STATE), da_im.reshape(N_GROUPS_C, 1, SSM_STATE),
           _block_diag_extract(d_bmat_re, SSM_GROUP, SSM_STATE), _block_diag_extract(d_bmat_im, SSM_GROUP, SSM_STATE))
    d_lr, d_li, d_ldt, d_br_t, d_bi_t = _s5_disc_bwd(p["lam_re"], p["lam_im"], p["log_dt"], p["b_re_t"], p["b_im_t"], cts)
    gs["lam_re"], gs["lam_im"], gs["log_dt"] = d_lr[:, 0, :], d_li[:, 0, :], d_ldt[:, 0, 0]
    gs["b_re"], gs["b_im"] = d_br_t.transpose(0, 2, 1), d_bi_t.transpose(0, 2, 1)
    gs["c_re"] = _block_diag_extract(d_cmat_re, SSM_STATE, SSM_GROUP).transpose(0, 2, 1)
    gs["c_im"] = _block_diag_extract(d_cmat_im, SSM_STATE, SSM_GROUP).transpose(0, 2, 1)
    dzb, db_zb, gs["sgu_ln_g"], gs["sgu_ln_b"], gs["w_s"], dbs_t = _gmlp_bwd(
        r["zb"], dyb, p["sgu_ln_g"], p["sgu_ln_b"], p["w_s"], p["b_s_t"], consts["group_sel"])
    gs["b_s"] = dbs_t[:, :N_GROUPS_B].T
    do_corr = _rowmap(_combine_bwd, r["ol"] + [r["ya"], dya], consts=[consts["head_ones"]],
                      row_outs=[(WIDTH_A, F32)] * 6, tr=512, name="attn_combine_bwd")
    dq, dk, dv, dbias = [], [], [], []
    for g, dil in enumerate(ATT_DILATIONS):
        do_g, corr_g, lse_g = do_corr[g], do_corr[3 + g], r["ol"][2 * g + 1]
        dq.append(_attn_dq(r["qkv"], biases[g], do_g, lse_g, corr_g, g, dil))
        dk_g, dv_g, db_g = _attn_dkv(r["qkv"], biases[g], do_g, lse_g, corr_g, g, dil)
        dk.append(dk_g)
        dv.append(dv_g)
        dbias.append(db_g)
    cast_colsum = lambda a: (a, jnp.sum(a, axis=0, keepdims=True))
    dqkv, db_qkv = _rowmap(cast_colsum, [jnp.concatenate(dq + dk + dv, axis=1)], row_outs=[(3 * QKV_WIDTH, BF16)],
                           red_outs=[(1, 3 * QKV_WIDTH)], tr=512, name="cast_colsum_qkv")
    duc, db_uc = _rowmap(cast_colsum, [duc], row_outs=[(WIDTH_C, BF16)], red_outs=[(1, WIDTH_C)], tr=512,
                         name="cast_colsum_uc")
    dpieces = {"qkv": dqkv, "zb": dzb, "uc": duc, "gl": dgl}
    dws = []
    for piece, off, n in p["in_pieces"]:
        dws.append(_mm(dpieces[piece], r["xb"], ta=True, out_dtype=BF16, name="mm_in_dw_" + piece))
        dx = _mm(dpieces[piece], p["w_in"], b_off=off, add=dx, name="mm_in_dx_" + piece)
    gw["w_in"] = jnp.concatenate(dws, axis=0)
    gs["b_in"] = jnp.concatenate([db_qkv, db_zb, db_uc, db_gl], axis=1)[0]
    for n in ("sgu_ln_g", "sgu_ln_b", "d_skip", "b_glu", "ln1_g", "ln1_b", "ln2_g", "ln2_b"):
        gs[n] = gs[n][0]
    return dx, gw, gs, dbias


def _cast_bf16(w):
    w2 = w.reshape(-1, w.shape[-1])
    out = _rowmap(lambda a: a, [w2], row_outs=[(w2.shape[1], BF16)], tr=512, name="cast_bf16")[0]
    return out.reshape(w.shape)


def _static_consts():
    head_ones = np.kron(np.eye(HEADS_PER_GROUP, dtype=np.float32), np.ones((HEAD_DIM, HEAD_DIM), np.float32))
    group_sel = np.zeros((WIDTH_B, LANES), np.float32)
    group_sel[np.arange(WIDTH_B), np.arange(WIDTH_B) // CHUNK] = 1.0
    return {"head_ones": jnp.asarray(head_ones), "group_sel": jnp.asarray(group_sel)}


def _step(x, tgt, w, m, v):
    shapes = {n: w[n].shape for n in WEIGHTS}
    consts = _static_consts()
    mine_bf = {n: _cast_bf16(w[n].transpose(0, 2, 1) if n in TRANSPOSED else w[n]) for n in SHARDED}
    small = {n: w[n] for n in SMALL}
    buckets = [jnp.asarray(_bucket_table(dil)) for dil in ATT_DILATIONS]
    biases = [_bias_fwd(w["rel_bias"], buckets[g], g) for g in range(len(ATT_DILATIONS))]
    params, saved = [], []
    h, hb = _rowmap(_twice(lambda a: a), [x], row_outs=[(x.shape[1], F32), (x.shape[1], BF16)], name="cast_x")
    for l in range(DEPTH):
        gathered = _gather_layer([mine_bf[n] for n in SHARDED], l, name=f"gather_layer{l}")
        full = {n: g.reshape(-1, g.shape[2]) for n, g in zip(SHARDED, gathered)}
        p = _layer_params(l, full, small)
        h, hb, r = _layer_fwd(h, hb, p, biases)
        params.append(p)
        saved.append(r)
    dy, loss_part = _rowmap(_loss_fn, [h, tgt], row_outs=[(h.shape[1], F32)], red_outs=[(1, LANES)], name="loss")
    loss = lax.psum(loss_part[0, 0], MESH_AXES)
    g_mine, gs_layers = {n: [None] * DEPTH for n in SHARDED}, [None] * DEPTH
    dbias_sum = None
    for l in reversed(range(DEPTH)):
        dy, gw, gs_layers[l], dbias = _layer_bwd(dy, saved[l], params[l], biases, consts)
        saved[l] = None
        for n, g in zip(SHARDED, _reduce_scatter_layer([gw[n] for n in SHARDED], f"layer{l}")):
            g_mine[n][l] = g
        if dbias_sum is None:
            dbias_sum = dbias
        else:
            dbias_sum = [_rowmap(lambda a, b: a + b, [a.reshape(-1, 2 * ATT_BLOCK), b.reshape(-1, 2 * ATT_BLOCK)],
                                 row_outs=[(2 * ATT_BLOCK, F32)], name="dbias_add")[0].reshape(a.shape)
                         for a, b in zip(dbias_sum, dbias)]
    drel = [_bias_bwd(dbias_sum[g], buckets[g], g) for g in range(len(ATT_DILATIONS))]
    drel = _rowmap(lambda a, b, c: a + b + c, drel, row_outs=[(LANES, F32)], name="drel_add")[0]
    grad_small_local = {n: jnp.stack([gs_layers[l][n] for l in range(DEPTH)]) for n in SMALL if n != "rel_bias"}
    grad_small_local["rel_bias"] = drel[:, :shapes["rel_bias"][1]]
    out_g, out_d, out_m, out_v = {}, {}, {}, {}
    for n in SHARDED:
        g = jnp.stack(g_mine[n])
        out_g[n] = g.transpose(0, 2, 1) if n in TRANSPOSED else g
        cols = shapes[n][-1]
        res = _rowmap(_adamw, [a.reshape(-1, cols) for a in (w[n], out_g[n], m[n], v[n])],
                      row_outs=[(cols, F32)] * 3, tr=128, name="adamw_" + n)
        out_d[n], out_m[n], out_v[n] = [a.reshape(shapes[n]) for a in res]
    packed = _pack_small(grad_small_local)
    rows = packed.shape[0] // N_DEV
    parts = _exchange(packed.reshape(N_DEV, rows, LANES), gather=False, name="scatter_small")
    mine = _rowmap(_sum_parts_fn, [], stacks=[parts], row_outs=[(LANES, F32)], name="sum_small")[0]
    g_small = _exchange(mine, gather=True, name="gather_small").reshape(-1, LANES)
    res = _rowmap(lambda w_, g_, m_, v_: _adamw(w_, g_, m_, v_),
                  [_pack_small(w), g_small, _pack_small(m), _pack_small(v)],
                  row_outs=[(LANES, F32)] * 3, name="adamw_small")
    small_shapes = {n: shapes[n] for n in SMALL}
    out_g.update(_unpack_small(g_small, small_shapes))
    for dst, packed_res in zip((out_d, out_m, out_v), res):
        dst.update(_unpack_small(packed_res, small_shapes))
    return loss, dy, out_g, out_d, out_m, out_v


def kernel(x, w_in, b_in, rel_bias, sgu_ln_g, sgu_ln_b, w_s, b_s, lam_re, lam_im, log_dt, b_re, b_im, c_re, c_im, d_skip, w_glu, b_glu, w_pa, w_pb, w_pc, w_o, ln1_g, ln1_b, w_ffn_in, w_ffn_out, ln2_g, ln2_b, loss_target, m_w_in, m_b_in, m_rel_bias, m_sgu_ln_g, m_sgu_ln_b, m_w_s, m_b_s, m_lam_re, m_lam_im, m_log_dt, m_b_re, m_b_im, m_c_re, m_c_im, m_d_skip, m_w_glu, m_b_glu, m_w_pa, m_w_pb, m_w_pc, m_w_o, m_ln1_g, m_ln1_b, m_w_ffn_in, m_w_ffn_out, m_ln2_g, m_ln2_b, v_w_in, v_b_in, v_rel_bias, v_sgu_ln_g, v_sgu_ln_b, v_w_s, v_b_s, v_lam_re, v_lam_im, v_log_dt, v_b_re, v_b_im, v_c_re, v_c_im, v_d_skip, v_w_glu, v_b_glu, v_w_pa, v_w_pb, v_w_pc, v_w_o, v_ln1_g, v_ln1_b, v_w_ffn_in, v_w_ffn_out, v_ln2_g, v_ln2_b):
    args = dict(locals())
    w = {n: args[n] for n in WEIGHTS}
    m = {n: args["m_" + n] for n in WEIGHTS}
    v = {n: args["v_" + n] for n in WEIGHTS}
    loss, dx, g, d, nm, nv = _step(x[0], loss_target[0], w, m, v)
    return (loss, dx[None], *[g[n] for n in WEIGHTS], *[d[n] for n in WEIGHTS],
            *[nm[n] for n in WEIGHTS], *[nv[n] for n in WEIGHTS])
```

```python
import functools
import math
from typing import Callable, NamedTuple

import numpy as np
import jax
import jax.numpy as jnp
from jax import lax
from jax.experimental import pallas as pl
from jax.experimental.pallas import tpu as pltpu

F32 = jnp.float32
BF16 = jnp.bfloat16

MESH_AXES = ("x", "y", "c")
N_DEV = 8
DEPTH = 4

ATT_DILATIONS = (1, 4, 16)
ATT_STEPS = 128
HEADS_PER_GROUP = 8
HEAD_DIM = 64
QKV_WIDTH = 1536
WIDTH_A = HEADS_PER_GROUP * HEAD_DIM
ATT_BLOCK = 128
N_REL_BUCKETS = 32
REL_MAX_DIST = 2048
NEG_INF = -1e30
CHUNK = 128
WIDTH_B = 768
N_GROUPS_B = 6
WIDTH_C = 768
SSM_GROUP = 16
N_GROUPS_C = 48
SSM_STATE = 64
SSM_PACK = 8
N_SSM_BLOCKS = N_GROUPS_C // SSM_PACK
SSM_COLS = N_GROUPS_C * SSM_STATE
ALPHA = (2 * DEPTH) ** 0.25

ADAM_LR = 0.001
ADAM_B1 = 0.9
ADAM_B2 = 0.999
ADAM_EPS = 1e-08
ADAM_WD = 0.01
ADAM_STEP = 10

LANES = 128
SUBLANES = 8
VMEM_LIMIT = 48 * 1024 * 1024

SHARDED = ("w_in", "w_glu", "w_pa", "w_pb", "w_pc", "w_o", "w_ffn_in", "w_ffn_out")
TRANSPOSED = ("w_in", "w_pa", "w_pb", "w_pc", "w_ffn_in")
SMALL = ("b_in", "rel_bias", "sgu_ln_g", "sgu_ln_b", "w_s", "b_s", "lam_re", "lam_im", "log_dt",
         "b_re", "b_im", "c_re", "c_im", "d_skip", "b_glu", "ln1_g", "ln1_b", "ln2_g", "ln2_b")
WEIGHTS = ("w_in", "b_in", "rel_bias", "sgu_ln_g", "sgu_ln_b", "w_s", "b_s", "lam_re", "lam_im",
           "log_dt", "b_re", "b_im", "c_re", "c_im", "d_skip", "w_glu", "b_glu", "w_pa", "w_pb",
           "w_pc", "w_o", "ln1_g", "ln1_b", "w_ffn_in", "w_ffn_out", "ln2_g", "ln2_b")


def _pick(dim, target, mult):
    best = None
    for t in range(mult, min(dim, target) + 1, mult):
        if dim % t == 0:
            best = t
    return dim if best is None else best


def _cparams(*sem):
    return pltpu.CompilerParams(dimension_semantics=sem, vmem_limit_bytes=VMEM_LIMIT)


def _zero_map(ndim):
    return lambda *_: (0,) * ndim


_HBM = pl.BlockSpec(memory_space=pl.ANY)
_MESH_ID = pl.DeviceIdType.MESH


class _Ride(NamedTuple):
    inputs: tuple
    carried: tuple
    created: tuple
    n_sems: int
    start: Callable
    finish: Callable


def _pallas(body, *, grid, in_specs, out_specs, out_shape, args, scratch=(), semantics, rides=(), name):
    rides = [r for r in rides if r is not None]
    n_in, n_out, n_scr = len(args), len(out_shape), len(scratch)
    r_args, r_shapes, aliases, spans = [], [], {}, []
    for r in rides:
        i0, o0 = len(r_args), len(r_shapes)
        r_args += [*r.inputs, *r.carried]
        for k, a in enumerate(r.carried):
            aliases[n_in + i0 + len(r.inputs) + k] = n_out + o0 + k
        r_shapes += [jax.ShapeDtypeStruct(a.shape, a.dtype) for a in r.carried] + list(r.created)
        spans.append((i0, len(r.inputs), o0, len(r.carried) + len(r.created)))

    def full_body(*refs):
        host_in, ride_in = refs[:n_in], refs[n_in:n_in + len(r_args)]
        p = n_in + len(r_args)
        host_out, ride_out = refs[p:p + n_out], refs[p + n_out:p + n_out + len(r_shapes)]
        p += n_out + len(r_shapes)
        host_scr, ride_sems = refs[p:p + n_scr], refs[p + n_scr:]
        ids = [pl.program_id(k) for k in range(len(grid))]
        first = functools.reduce(jnp.logical_and, [i == 0 for i in ids])
        last = functools.reduce(jnp.logical_and, [i == g - 1 for i, g in zip(ids, grid)])

        def each(method):
            for r, (i0, ni, o0, no), sems in zip(rides, spans, ride_sems):
                getattr(r, method)(ride_in[i0:i0 + ni], ride_out[o0:o0 + no], sems)

        if rides:
            pl.when(first)(lambda: each("start"))
        body(*host_in, *host_out, *host_scr)
        if rides:
            pl.when(last)(lambda: each("finish"))

    if rides:
        semantics = ("arbitrary",) * len(grid)
    outs = pl.pallas_call(
        full_body, grid=grid, in_specs=list(in_specs) + [_HBM] * len(r_args),
        out_specs=list(out_specs) + [_HBM] * len(r_shapes), out_shape=list(out_shape) + r_shapes,
        scratch_shapes=list(scratch) + [pltpu.SemaphoreType.DMA((r.n_sems,)) for r in rides],
        input_output_aliases=aliases, compiler_params=_cparams(*semantics), name=name)(*args, *r_args)
    return outs[:n_out], [outs[n_out + o0:n_out + o0 + no] for _, _, o0, no in spans]


def _rowmap(fn, rows, consts=(), stacks=(), row_outs=(), red_outs=(), tr=256, name=None, rides=None):
    t = rows[0].shape[0] if rows else stacks[0].shape[1]
    dtypes = [a.dtype for a in (*rows, *stacks)] + [dt for _, dt in row_outs]
    packed = any(jnp.dtype(dt).itemsize < 4 for dt in dtypes)
    tr = _pick(t, tr, 2 * SUBLANES if packed else SUBLANES)
    n_r, n_c, n_s, n_o = len(rows), len(consts), len(stacks), len(row_outs)

    def body(*refs):
        ins = [r[...] for r in refs[:n_r + n_c + n_s]]
        outs = refs[n_r + n_c + n_s:n_r + n_c + n_s + n_o]
        reds = refs[n_r + n_c + n_s + n_o:]
        res = fn(*ins)
        if not isinstance(res, (tuple, list)):
            res = (res,)
        for o, v in zip(outs, res[:n_o]):
            o[...] = v.astype(o.dtype)
        if reds:
            @pl.when(pl.program_id(0) == 0)
            def _():
                for r in reds:
                    r[...] = jnp.zeros_like(r)
            for r, v in zip(reds, res[n_o:]):
                r[...] += v

    in_specs = [pl.BlockSpec((tr, r.shape[1]), lambda i: (i, 0)) for r in rows]
    in_specs += [pl.BlockSpec(c.shape, _zero_map(c.ndim)) for c in consts]
    in_specs += [pl.BlockSpec((s.shape[0], tr, s.shape[2]), lambda i: (0, i, 0)) for s in stacks]
    out_specs = [pl.BlockSpec((tr, w), lambda i: (i, 0)) for w, _ in row_outs]
    out_specs += [pl.BlockSpec(s, _zero_map(len(s))) for s in red_outs]
    out_shape = [jax.ShapeDtypeStruct((t, w), dt) for w, dt in row_outs]
    out_shape += [jax.ShapeDtypeStruct(s, F32) for s in red_outs]
    outs, ride_outs = _pallas(body, grid=(t // tr,), in_specs=in_specs, out_specs=out_specs, out_shape=out_shape,
                              args=[*rows, *consts, *stacks], semantics=("arbitrary",), rides=rides or (), name=name)
    return outs if rides is None else (outs, ride_outs)


MM_VMEM_BUDGET = 36 * 1024 * 1024


def _divisors(dim, mult, must_divide=0):
    out = [t for t in range(dim, 0, -mult) if t % mult == 0 and dim % t == 0 and must_divide % t == 0]
    return out or [dim]


def _mm_tiles(m, n, k, a_bytes, b_bytes, out_bytes, extra_bytes, ta, b_off_n, b_off_k, tm, tn):
    tms = _divisors(m, LANES if ta else SUBLANES)
    tm = next((t for t in tms if t <= tm), tms[-1])
    tns = [t for t in _divisors(n, LANES, b_off_n) if t <= tn] or [_divisors(n, LANES, b_off_n)[-1]]
    for tn_ in tns:
        for tk in _divisors(k, LANES, b_off_k):
            acc = 0 if tk == k else tm * tn_ * 4
            need = 2 * (tm * tk * a_bytes + tk * tn_ * b_bytes + tm * tn_ * (out_bytes + extra_bytes)) + acc
            if need <= MM_VMEM_BUDGET:
                return tm, tn_, tk
    return tm, tns[-1], _divisors(k, LANES, b_off_k)[-1]


def _mm(a, b, *, ta=False, tb=False, bias=None, add=None, out_dtype=F32, b_off=0, n=None, tm=1024, tn=1024, name=None,
        rides=None):
    k, m = a.shape if ta else a.shape[::-1]
    if tb:
        n = b.shape[0] if n is None else n
        assert b.shape[1] == k and b_off + n <= b.shape[0]
    else:
        n = b.shape[1]
        assert b_off + k <= b.shape[0]
    extra = 4 if add is not None else 0
    tm, tn, tk = _mm_tiles(m, n, k, a.dtype.itemsize, b.dtype.itemsize, jnp.dtype(out_dtype).itemsize, extra, ta,
                           b_off if tb else 0, 0 if tb else b_off, tm, tn)
    nk = k // tk
    off_n, off_k = (b_off // tn, 0) if tb else (0, b_off // tk)
    dims = (((0 if ta else 1,), (1 if tb else 0,)), ((), ()))

    def body(*refs):
        a_ref, b_ref = refs[0], refs[1]
        rest = list(refs[2:])
        bias_ref = rest.pop(0) if bias is not None else None
        add_ref = rest.pop(0) if add is not None else None
        o_ref = rest.pop(0)
        part = lax.dot_general(a_ref[...].astype(BF16), b_ref[...].astype(BF16), dims, preferred_element_type=F32)

        def finish(r):
            if bias_ref is not None:
                r = r + bias_ref[...]
            if add_ref is not None:
                r = r + add_ref[...]
            o_ref[...] = r.astype(o_ref.dtype)

        if nk == 1:
            finish(part)
        else:
            acc_ref = rest.pop(0)
            kk = pl.program_id(2)

            @pl.when(kk == 0)
            def _():
                acc_ref[...] = part

            @pl.when(kk > 0)
            def _():
                acc_ref[...] += part

            @pl.when(kk == nk - 1)
            def _():
                finish(acc_ref[...])

    a_spec = pl.BlockSpec((tk, tm), lambda i, j, q: (q, i)) if ta else pl.BlockSpec((tm, tk), lambda i, j, q: (i, q))
    if tb:
        b_spec = pl.BlockSpec((tn, tk), lambda i, j, q: (j + off_n, q))
    else:
        b_spec = pl.BlockSpec((tk, tn), lambda i, j, q: (q + off_k, j))
    in_specs, args = [a_spec, b_spec], [a, b]
    if bias is not None:
        in_specs.append(pl.BlockSpec((1, tn), lambda i, j, q: (0, j)))
        args.append(bias)
    if add is not None:
        in_specs.append(pl.BlockSpec((tm, tn), lambda i, j, q: (i, j)))
        args.append(add)
    outs, ride_outs = _pallas(
        body, grid=(m // tm, n // tn, nk), in_specs=in_specs,
        out_specs=[pl.BlockSpec((tm, tn), lambda i, j, q: (i, j))],
        out_shape=[jax.ShapeDtypeStruct((m, n), out_dtype)], args=args,
        scratch=[] if nk == 1 else [pltpu.VMEM((tm, tn), F32)],
        semantics=("parallel", "parallel", "arbitrary"), rides=rides or (), name=name)
    return outs[0] if rides is None else (outs[0], ride_outs)


def _bdmm(a, bm, *, tb=False, scale=1.0, add=None, name=None):
    t = a.shape[0]
    nj, ka, kb = bm.shape
    kin, kout = (kb, ka) if tb else (ka, kb)
    tm = _pick(t, 512, SUBLANES)
    dims = (((1,), (1 if tb else 0,)), ((), ()))

    def body(*refs):
        a_ref, b_ref = refs[0], refs[1]
        add_ref = refs[2] if add is not None else None
        o_ref = refs[-1]
        r = lax.dot_general(a_ref[...].astype(BF16), b_ref[0].astype(BF16), dims, preferred_element_type=F32)
        if scale != 1.0:
            r = r * scale
        if add_ref is not None:
            r = r + add_ref[...]
        o_ref[...] = r

    in_specs = [pl.BlockSpec((tm, kin), lambda i, j: (i, j)), pl.BlockSpec((1, ka, kb), lambda i, j: (j, 0, 0))]
    args = [a, bm]
    if add is not None:
        in_specs.append(pl.BlockSpec((tm, kout), lambda i, j: (i, j)))
        args.append(add)
    return pl.pallas_call(
        body, grid=(t // tm, nj), in_specs=in_specs, out_specs=pl.BlockSpec((tm, kout), lambda i, j: (i, j)),
        out_shape=jax.ShapeDtypeStruct((t, nj * kout), F32),
        compiler_params=_cparams("parallel", "parallel"), name=name)(*args)


def _bdmm_tn(a, b, nj, *, scale=1.0, name=None):
    t = a.shape[0]
    ka, kb = a.shape[1] // nj, b.shape[1] // nj
    tm = _pick(t, 512, LANES)
    nt = t // tm

    def body(a_ref, b_ref, o_ref):
        @pl.when(pl.program_id(1) == 0)
        def _():
            o_ref[...] = jnp.zeros_like(o_ref)

        r = lax.dot_general(a_ref[...].astype(BF16), b_ref[...].astype(BF16), (((0,), (0,)), ((), ())),
                            preferred_element_type=F32)
        o_ref[0] += r if scale == 1.0 else r * scale

    return pl.pallas_call(
        body, grid=(nj, nt),
        in_specs=[pl.BlockSpec((tm, ka), lambda j, i: (i, j)), pl.BlockSpec((tm, kb), lambda j, i: (i, j))],
        out_specs=pl.BlockSpec((1, ka, kb), lambda j, i: (j, 0, 0)),
        out_shape=jax.ShapeDtypeStruct((nj, ka, kb), F32),
        compiler_params=_cparams("parallel", "arbitrary"), name=name)(a, b)


def _ln(x, g, b, eps=1e-5):
    mu = jnp.mean(x, axis=-1, keepdims=True)
    var = jnp.mean(jnp.square(x - mu), axis=-1, keepdims=True)
    return (x - mu) * lax.rsqrt(var + eps) * g + b


def _post_norm(x, f, g, b):
    return _ln(ALPHA * x + f, g, b)


def _post_norm_bwd(x, f, dy, g, b):
    _, vjp = jax.vjp(_post_norm, x, f, g, b)
    return vjp(dy)


def _merge3(g0, g1, g2, pa, pb, pc):
    return jax.nn.sigmoid(g0) * pa + jax.nn.sigmoid(g1) * pb + jax.nn.sigmoid(g2) * pc


def _merge(gl, pa, pb, pc):
    d = pa.shape[1]
    return _merge3(gl[:, :d], gl[:, d:2 * d], gl[:, 2 * d:], pa, pb, pc)


def _merge_bwd(gl, pa, pb, pc, dm):
    d = pa.shape[1]
    _, vjp = jax.vjp(_merge3, gl[:, :d], gl[:, d:2 * d], gl[:, 2 * d:], pa, pb, pc)
    d0, d1, d2, dpa, dpb, dpc = vjp(dm)
    dgl = jnp.concatenate([d0, d1, d2], axis=1)
    return dgl, dpa, dpb, dpc, jnp.sum(dgl, axis=0, keepdims=True)


def _swiglu2(gate, up):
    return jax.nn.silu(gate) * up


def _swiglu(gu):
    h = gu.shape[1] // 2
    return _swiglu2(gu[:, :h], gu[:, h:])


def _swiglu_bwd(gu, dact):
    h = gu.shape[1] // 2
    _, vjp = jax.vjp(_swiglu2, gu[:, :h], gu[:, h:])
    dg, du = vjp(dact)
    return jnp.concatenate([dg, du], axis=1)


def _glu(ycp, lin):
    return ycp * jax.nn.sigmoid(lin)


def _glu_bwd(ycp, lin, dyc):
    _, vjp = jax.vjp(_glu, ycp, lin)
    dycp, dlin = vjp(dyc)
    return dycp, dlin, jnp.sum(dlin, axis=0, keepdims=True)


def _s5_out(yre, yim, uc, dskip):
    ys = yre + yim + dskip * uc
    return ys, jax.nn.gelu(ys)


def _s5_out_bwd(ys, uc, dycp, dskip):
    _, vjp = jax.vjp(jax.nn.gelu, ys)
    dys = vjp(dycp)[0]
    return dys, dys * dskip, jnp.sum(dys * uc, axis=0, keepdims=True)


def _combine(o0, l0, o1, l1, o2, l2):
    m = jnp.maximum(jnp.maximum(l0, l1), l2)
    e0, e1, e2 = jnp.exp(l0 - m), jnp.exp(l1 - m), jnp.exp(l2 - m)
    s = e0 + e1 + e2
    return (e0 / s) * o0 + (e1 / s) * o1 + (e2 / s) * o2


def _combine_bwd(o0, l0, o1, l1, o2, l2, ya, dya, head_ones):
    m = jnp.maximum(jnp.maximum(l0, l1), l2)
    e0, e1, e2 = jnp.exp(l0 - m), jnp.exp(l1 - m), jnp.exp(l2 - m)
    s = e0 + e1 + e2
    dot_ya = jnp.dot(dya * ya, head_ones, precision=lax.Precision.HIGHEST, preferred_element_type=F32)
    w0, w1, w2 = e0 / s, e1 / s, e2 / s
    return w0 * dya, w1 * dya, w2 * dya, -w0 * dot_ya, -w1 * dot_ya, -w2 * dot_ya


def _loss_fn(y, tgt):
    err = y - tgt
    part = jnp.sum(jnp.sum(jnp.square(err), axis=1, keepdims=True), axis=0, keepdims=True) * (0.5 / y.shape[1])
    return err * (1.0 / y.shape[1]), jnp.broadcast_to(part, (1, LANES))


def _adamw(w, g, m, v):
    m = ADAM_B1 * m + (1.0 - ADAM_B1) * g
    v = ADAM_B2 * v + (1.0 - ADAM_B2) * jnp.square(g)
    m_hat = m / (1.0 - ADAM_B1 ** ADAM_STEP)
    v_hat = v / (1.0 - ADAM_B2 ** ADAM_STEP)
    delta = -ADAM_LR * (m_hat / (jnp.sqrt(v_hat) + ADAM_EPS) + ADAM_WD * w)
    return delta, m, v


def _sum_parts_fn(parts):
    g = parts[0].astype(F32)
    for j in range(1, parts.shape[0]):
        g = g + parts[j].astype(F32)
    return g


def _t5_bucket(dist):
    max_exact = N_REL_BUCKETS // 2
    d = np.maximum(dist, 1).astype(np.float32)
    scale = (N_REL_BUCKETS - max_exact) / math.log(REL_MAX_DIST / max_exact)
    large = max_exact + (np.log(d / max_exact) * scale).astype(np.int32)
    large = np.minimum(large, N_REL_BUCKETS - 1)
    return np.where(dist < max_exact, dist, large).astype(np.int32)


def _bucket_table(dilation):
    i = np.arange(ATT_BLOCK)[:, None]
    kk = np.arange(2 * ATT_BLOCK)[None, :]
    steps = ATT_BLOCK + i - kk
    return _t5_bucket(np.maximum(steps, 0) * dilation)


def _bias_fwd(rel_bias, buckets, g):
    def body(rel_ref, bk_ref, o_ref):
        bk = bk_ref[...]
        for h in range(HEADS_PER_GROUP):
            acc = jnp.zeros(bk.shape, F32)
            for b in range(N_REL_BUCKETS):
                acc = jnp.where(bk == b, rel_ref[b, g * HEADS_PER_GROUP + h], acc)
            o_ref[h] = acc

    return pl.pallas_call(
        body, in_specs=[pl.BlockSpec(memory_space=pltpu.SMEM), pl.BlockSpec(memory_space=pltpu.VMEM)],
        out_specs=pl.BlockSpec(memory_space=pltpu.VMEM),
        out_shape=jax.ShapeDtypeStruct((HEADS_PER_GROUP, ATT_BLOCK, 2 * ATT_BLOCK), F32),
        name=f"rel_bias_fwd{g}")(rel_bias, buckets)


def _bias_bwd(dbias, buckets, g):
    def body(db_ref, bk_ref, o_ref):
        bk = bk_ref[...]
        row = lax.broadcasted_iota(jnp.int32, (N_REL_BUCKETS, LANES), 0)
        col = lax.broadcasted_iota(jnp.int32, (N_REL_BUCKETS, LANES), 1)
        acc = jnp.zeros((N_REL_BUCKETS, LANES), F32)
        for h in range(HEADS_PER_GROUP):
            d = db_ref[h]
            for b in range(N_REL_BUCKETS):
                s = jnp.sum(jnp.sum(jnp.where(bk == b, d, 0.0), axis=1, keepdims=True), axis=0, keepdims=True)
                acc = acc + jnp.where((row == b) & (col == g * HEADS_PER_GROUP + h), s, 0.0)
        o_ref[...] = acc

    return pl.pallas_call(
        body, in_specs=[pl.BlockSpec(memory_space=pltpu.VMEM), pl.BlockSpec(memory_space=pltpu.VMEM)],
        out_specs=pl.BlockSpec(memory_space=pltpu.VMEM),
        out_shape=jax.ShapeDtypeStruct((N_REL_BUCKETS, LANES), F32), name=f"rel_bias_bwd{g}")(dbias, buckets)


_NT = (((1,), (1,)), ((), ()))
_TN = (((0,), (0,)), ((), ()))
_QKV_BLOCKS = 3 * QKV_WIDTH // WIDTH_A


def _band_mask(n_is_first):
    i = lax.broadcasted_iota(jnp.int32, (ATT_BLOCK, 2 * ATT_BLOCK), 0)
    kk = lax.broadcasted_iota(jnp.int32, (ATT_BLOCK, 2 * ATT_BLOCK), 1)
    return (kk >= i) & (kk <= i + ATT_STEPS) & ((kk >= ATT_BLOCK) | jnp.logical_not(n_is_first))


def _head(ref, h):
    return ref[:, h * HEAD_DIM:(h + 1) * HEAD_DIM]


def _attn_specs(g, d):
    blk = (ATT_BLOCK, WIDTH_A)
    q = pl.BlockSpec(blk, lambda c, n: (n, c * _QKV_BLOCKS + g))
    kp = pl.BlockSpec(blk, lambda c, n: (jnp.maximum(n - 1, 0), c * _QKV_BLOCKS + 3 + g))
    kc = pl.BlockSpec(blk, lambda c, n: (n, c * _QKV_BLOCKS + 3 + g))
    vp = pl.BlockSpec(blk, lambda c, n: (jnp.maximum(n - 1, 0), c * _QKV_BLOCKS + 6 + g))
    vc = pl.BlockSpec(blk, lambda c, n: (n, c * _QKV_BLOCKS + 6 + g))
    return [q, kp, kc, vp, vc]


def _attn_fwd(qkv, bias, g, d):
    t = qkv.shape[0]
    lq = t // d
    nb = lq // ATT_BLOCK
    scale = HEAD_DIM ** -0.5

    def body(q_ref, kp_ref, kc_ref, vp_ref, vc_ref, b_ref, o_ref, l_ref):
        mask = _band_mask(pl.program_id(1) == 0)
        for h in range(HEADS_PER_GROUP):
            qh = _head(q_ref, h).astype(BF16)
            kh = jnp.concatenate([_head(kp_ref, h), _head(kc_ref, h)], axis=0).astype(BF16)
            vh = jnp.concatenate([_head(vp_ref, h), _head(vc_ref, h)], axis=0).astype(BF16)
            s = lax.dot_general(qh, kh, _NT, preferred_element_type=F32) * scale + b_ref[h]
            s = jnp.where(mask, s, NEG_INF)
            m = jnp.max(s, axis=1, keepdims=True)
            p = jnp.exp(s - m)
            den = jnp.sum(p, axis=1, keepdims=True)
            o = jnp.dot(p.astype(BF16), vh, preferred_element_type=F32) / den
            o_ref[:, h * HEAD_DIM:(h + 1) * HEAD_DIM] = o
            l_ref[:, h * HEAD_DIM:(h + 1) * HEAD_DIM] = jnp.broadcast_to(m + jnp.log(den), (ATT_BLOCK, HEAD_DIM))

    out_spec = pl.BlockSpec((ATT_BLOCK, WIDTH_A), lambda c, n: (n, c))
    o, lse = pl.pallas_call(
        body, grid=(d, nb),
        in_specs=_attn_specs(g, d) + [pl.BlockSpec(bias.shape, _zero_map(3))],
        out_specs=[out_spec, out_spec],
        out_shape=[jax.ShapeDtypeStruct((lq, d * WIDTH_A), F32)] * 2,
        compiler_params=_cparams("parallel", "parallel"), name=f"attn_fwd{g}",
    )(*([qkv.reshape(lq, d * 3 * QKV_WIDTH)] * 5), bias)
    return o.reshape(t, WIDTH_A), lse.reshape(t, WIDTH_A)


def _attn_dq(qkv, bias, do, lse, corr, g, d):
    t = qkv.shape[0]
    lq = t // d
    nb = lq // ATT_BLOCK
    scale = HEAD_DIM ** -0.5

    def body(q_ref, kp_ref, kc_ref, vp_ref, vc_ref, b_ref, do_ref, l_ref, c_ref, dq_ref):
        mask = _band_mask(pl.program_id(1) == 0)
        for h in range(HEADS_PER_GROUP):
            qh = _head(q_ref, h).astype(BF16)
            kh = jnp.concatenate([_head(kp_ref, h), _head(kc_ref, h)], axis=0).astype(BF16)
            vh = jnp.concatenate([_head(vp_ref, h), _head(vc_ref, h)], axis=0).astype(BF16)
            s = lax.dot_general(qh, kh, _NT, preferred_element_type=F32) * scale + b_ref[h]
            s = jnp.where(mask, s, NEG_INF)
            p = jnp.exp(s - l_ref[:, h * HEAD_DIM:h * HEAD_DIM + 1])
            dp = lax.dot_general(_head(do_ref, h).astype(BF16), vh, _NT, preferred_element_type=F32)
            ds = p * (dp + c_ref[:, h * HEAD_DIM:h * HEAD_DIM + 1])
            dq_ref[:, h * HEAD_DIM:(h + 1) * HEAD_DIM] = jnp.dot(ds.astype(BF16), kh, preferred_element_type=F32) * scale

    row_spec = pl.BlockSpec((ATT_BLOCK, WIDTH_A), lambda c, n: (n, c))
    view = lambda a: a.reshape(lq, d * WIDTH_A)
    dq = pl.pallas_call(
        body, grid=(d, nb),
        in_specs=_attn_specs(g, d) + [pl.BlockSpec(bias.shape, _zero_map(3)), row_spec, row_spec, row_spec],
        out_specs=row_spec, out_shape=jax.ShapeDtypeStruct((lq, d * WIDTH_A), F32),
        compiler_params=_cparams("parallel", "parallel"), name=f"attn_dq{g}",
    )(*([qkv.reshape(lq, d * 3 * QKV_WIDTH)] * 5), bias, view(do), view(lse), view(corr))
    return dq.reshape(t, WIDTH_A)


def _attn_dkv(qkv, bias, do, lse, corr, g, d):
    t = qkv.shape[0]
    lq = t // d
    nb = lq // ATT_BLOCK
    scale = HEAD_DIM ** -0.5

    def body(k_ref, v_ref, q0_ref, q1_ref, do0_ref, do1_ref, l0_ref, l1_ref, c0_ref, c1_ref, b_ref,
             dk_ref, dv_ref, db_ref):
        c, j = pl.program_id(0), pl.program_id(1)

        @pl.when((c == 0) & (j == 0))
        def _():
            db_ref[...] = jnp.zeros_like(db_ref)

        i = lax.broadcasted_iota(jnp.int32, (ATT_BLOCK, ATT_BLOCK), 0)
        kk = lax.broadcasted_iota(jnp.int32, (ATT_BLOCK, ATT_BLOCK), 1)
        mask0 = kk <= i
        mask1 = (kk >= i) & (j + 1 < nb)
        for h in range(HEADS_PER_GROUP):
            kh = _head(k_ref, h).astype(BF16)
            vh = _head(v_ref, h).astype(BF16)
            dk = jnp.zeros((ATT_BLOCK, HEAD_DIM), F32)
            dv = jnp.zeros((ATT_BLOCK, HEAD_DIM), F32)
            parts = ((q0_ref, do0_ref, l0_ref, c0_ref, mask0, ATT_BLOCK), (q1_ref, do1_ref, l1_ref, c1_ref, mask1, 0))
            for q_ref, do_ref, l_ref, c_ref, mask, off in parts:
                qh = _head(q_ref, h).astype(BF16)
                doh = _head(do_ref, h).astype(BF16)
                s = lax.dot_general(qh, kh, _NT, preferred_element_type=F32) * scale + b_ref[h, :, off:off + ATT_BLOCK]
                s = jnp.where(mask, s, NEG_INF)
                p = jnp.exp(s - l_ref[:, h * HEAD_DIM:h * HEAD_DIM + 1])
                dp = lax.dot_general(doh, vh, _NT, preferred_element_type=F32)
                ds = p * (dp + c_ref[:, h * HEAD_DIM:h * HEAD_DIM + 1])
                dv = dv + lax.dot_general(p.astype(BF16), doh, _TN, preferred_element_type=F32)
                dk = dk + lax.dot_general(ds.astype(BF16), qh, _TN, preferred_element_type=F32)
                db_ref[h, :, off:off + ATT_BLOCK] += ds
            dk_ref[:, h * HEAD_DIM:(h + 1) * HEAD_DIM] = dk * scale
            dv_ref[:, h * HEAD_DIM:(h + 1) * HEAD_DIM] = dv

    blk = (ATT_BLOCK, WIDTH_A)
    nxt = lambda n: jnp.minimum(n + 1, nb - 1)
    k_spec = pl.BlockSpec(blk, lambda c, n: (n, c * _QKV_BLOCKS + 3 + g))
    v_spec = pl.BlockSpec(blk, lambda c, n: (n, c * _QKV_BLOCKS + 6 + g))
    q0_spec = pl.BlockSpec(blk, lambda c, n: (n, c * _QKV_BLOCKS + g))
    q1_spec = pl.BlockSpec(blk, lambda c, n: (nxt(n), c * _QKV_BLOCKS + g))
    r0 = pl.BlockSpec(blk, lambda c, n: (n, c))
    r1 = pl.BlockSpec(blk, lambda c, n: (nxt(n), c))
    view = lambda a: a.reshape(lq, d * WIDTH_A)
    qv = qkv.reshape(lq, d * 3 * QKV_WIDTH)
    dk, dv, dbias = pl.pallas_call(
        body, grid=(d, nb),
        in_specs=[k_spec, v_spec, q0_spec, q1_spec, r0, r1, r0, r1, r0, r1, pl.BlockSpec(bias.shape, _zero_map(3))],
        out_specs=[r0, r0, pl.BlockSpec(bias.shape, _zero_map(3))],
        out_shape=[jax.ShapeDtypeStruct((lq, d * WIDTH_A), F32)] * 2 + [jax.ShapeDtypeStruct(bias.shape, F32)],
        compiler_params=_cparams("arbitrary", "arbitrary"), name=f"attn_dkv{g}",
    )(qv, qv, qv, qv, view(do), view(do), view(lse), view(lse), view(corr), view(corr), bias)
    return dk.reshape(t, WIDTH_A), dv.reshape(t, WIDTH_A), dbias


def _tril_mask():
    r = lax.broadcasted_iota(jnp.int32, (CHUNK, CHUNK), 0)
    c = lax.broadcasted_iota(jnp.int32, (CHUNK, CHUNK), 1)
    return c <= r


def _gmlp_fwd(zb, ln_g, ln_b, w_s, b_s_t):
    t = zb.shape[0]
    tr = _pick(t, 2 * CHUNK, CHUNK)

    def body(z_ref, g_ref, b_ref, ws_ref, bs_ref, o_ref):
        tri = _tril_mask()
        z = jax.nn.gelu(z_ref[...])
        u = z[:, :WIDTH_B]
        vn = _ln(z[:, WIDTH_B:], g_ref[...], b_ref[...])
        for ch in range(tr // CHUNK):
            rows = slice(ch * CHUNK, (ch + 1) * CHUNK)
            for gi in range(N_GROUPS_B):
                cols = slice(gi * CHUNK, (gi + 1) * CHUNK)
                w = jnp.where(tri, ws_ref[gi], 0.0).astype(BF16)
                mixed = jnp.dot(w, vn[rows, cols].astype(BF16), preferred_element_type=F32) + bs_ref[:, gi:gi + 1]
                o_ref[rows, cols] = (u[rows, cols] * mixed).astype(o_ref.dtype)

    return pl.pallas_call(
        body, grid=(t // tr,),
        in_specs=[pl.BlockSpec((tr, 2 * WIDTH_B), lambda i: (i, 0)), pl.BlockSpec(ln_g.shape, _zero_map(2)),
                  pl.BlockSpec(ln_b.shape, _zero_map(2)), pl.BlockSpec(w_s.shape, _zero_map(3)),
                  pl.BlockSpec(b_s_t.shape, _zero_map(2))],
        out_specs=pl.BlockSpec((tr, WIDTH_B), lambda i: (i, 0)),
        out_shape=jax.ShapeDtypeStruct((t, WIDTH_B), BF16),
        compiler_params=_cparams("parallel"), name="gmlp_fwd")(zb, ln_g, ln_b, w_s, b_s_t)


def _gmlp_bwd(zb, dyb, ln_g, ln_b, w_s, b_s_t, group_sel):
    t = zb.shape[0]
    tr = _pick(t, 2 * CHUNK, CHUNK)

    def body(z_ref, dy_ref, g_ref, b_ref, ws_ref, bs_ref, sel_ref, dz_ref, dzs_ref, dg_ref, db_ref, dws_ref, dbs_ref,
             du_s, dvn_s, dm_s):
        @pl.when(pl.program_id(0) == 0)
        def _():
            dzs_ref[...] = jnp.zeros_like(dzs_ref)
            dg_ref[...] = jnp.zeros_like(dg_ref)
            db_ref[...] = jnp.zeros_like(db_ref)
            dws_ref[...] = jnp.zeros_like(dws_ref)
            dbs_ref[...] = jnp.zeros_like(dbs_ref)

        tri = _tril_mask()
        z, gelu_vjp = jax.vjp(jax.nn.gelu, z_ref[...])
        u = z[:, :WIDTH_B]
        vn, ln_vjp = jax.vjp(_ln, z[:, WIDTH_B:], g_ref[...], b_ref[...])
        dy = dy_ref[...]
        for ch in range(tr // CHUNK):
            rows = slice(ch * CHUNK, (ch + 1) * CHUNK)
            for gi in range(N_GROUPS_B):
                cols = slice(gi * CHUNK, (gi + 1) * CHUNK)
                w = jnp.where(tri, ws_ref[gi], 0.0).astype(BF16)
                vg = vn[rows, cols].astype(BF16)
                mixed = jnp.dot(w, vg, preferred_element_type=F32) + bs_ref[:, gi:gi + 1]
                dyg = dy[rows, cols]
                dm = dyg * u[rows, cols]
                dmb = dm.astype(BF16)
                du_s[rows, cols] = dyg * mixed
                dm_s[rows, cols] = dm
                dvn_s[rows, cols] = lax.dot_general(w, dmb, _TN, preferred_element_type=F32)
                dws_ref[gi] += jnp.where(tri, lax.dot_general(dmb, vg, _NT, preferred_element_type=F32), 0.0)
            dbs_ref[...] += jnp.dot(dm_s[rows, :], sel_ref[...], precision=lax.Precision.HIGHEST,
                                    preferred_element_type=F32)
        dv, dg, db = ln_vjp(dvn_s[...])
        dg_ref[...] += dg
        db_ref[...] += db
        dz = gelu_vjp(jnp.concatenate([du_s[...], dv], axis=1))[0]
        dz_ref[...] = dz.astype(dz_ref.dtype)
        dzs_ref[...] += jnp.sum(dz, axis=0, keepdims=True)

    full = lambda a: pl.BlockSpec(a.shape, _zero_map(a.ndim))
    return pl.pallas_call(
        body, grid=(t // tr,),
        in_specs=[pl.BlockSpec((tr, 2 * WIDTH_B), lambda i: (i, 0)), pl.BlockSpec((tr, WIDTH_B), lambda i: (i, 0)),
                  full(ln_g), full(ln_b), full(w_s), full(b_s_t), full(group_sel)],
        out_specs=[pl.BlockSpec((tr, 2 * WIDTH_B), lambda i: (i, 0)), pl.BlockSpec((1, 2 * WIDTH_B), _zero_map(2)),
                   full(ln_g), full(ln_b), full(w_s), pl.BlockSpec((CHUNK, LANES), _zero_map(2))],
        out_shape=[jax.ShapeDtypeStruct((t, 2 * WIDTH_B), BF16), jax.ShapeDtypeStruct((1, 2 * WIDTH_B), F32),
                   jax.ShapeDtypeStruct(ln_g.shape, F32),
                   jax.ShapeDtypeStruct(ln_b.shape, F32), jax.ShapeDtypeStruct(w_s.shape, F32),
                   jax.ShapeDtypeStruct((CHUNK, LANES), F32)],
        scratch_shapes=[pltpu.VMEM((tr, WIDTH_B), F32)] * 3,
        compiler_params=_cparams("arbitrary"), name="gmlp_bwd")(zb, dyb, ln_g, ln_b, w_s, b_s_t, group_sel)


def _s5_disc(lr, li, ldt, br_t, bi_t):
    dt = jnp.exp(ldt)
    mag = jnp.exp(lr * dt)
    ab_re = mag * jnp.cos(li * dt)
    ab_im = mag * jnp.sin(li * dt)
    nrm = lr * lr + li * li
    cr = ((ab_re - 1.0) * lr + ab_im * li) / nrm
    ci = (ab_im * lr - (ab_re - 1.0) * li) / nrm
    return ab_re, ab_im, cr * br_t - ci * bi_t, cr * bi_t + ci * br_t


def _vmem_call(fn, args, out_shape, name):
    def body(*refs):
        res = fn(*[r[...] for r in refs[:len(args)]])
        for o, v in zip(refs[len(args):], res):
            o[...] = v

    vm = pl.BlockSpec(memory_space=pltpu.VMEM)
    return pl.pallas_call(body, in_specs=[vm] * len(args), out_specs=[vm] * len(out_shape),
                          out_shape=out_shape, name=name)(*args)


def _s5_disc_fwd(lr, li, ldt, br_t, bi_t):
    s1 = jax.ShapeDtypeStruct(lr.shape, F32)
    s2 = jax.ShapeDtypeStruct(br_t.shape, F32)
    return _vmem_call(_s5_disc, [lr, li, ldt, br_t, bi_t], [s1, s1, s2, s2], "s5_disc_fwd")


def _s5_disc_bwd(lr, li, ldt, br_t, bi_t, cts):
    def fn(lr, li, ldt, br_t, bi_t, d0, d1, d2, d3):
        _, vjp = jax.vjp(_s5_disc, lr, li, ldt, br_t, bi_t)
        return vjp((d0, d1, d2, d3))

    shp = [jax.ShapeDtypeStruct(a.shape, F32) for a in (lr, li, ldt, br_t, bi_t)]
    return _vmem_call(fn, [lr, li, ldt, br_t, bi_t, *cts], shp, "s5_disc_bwd")


_SCAN_ROWS = SSM_COLS // LANES
_SCAN_CHUNK = 256


def _scan_fwd(bre, bim, are, aim):
    t = bre.shape[0]
    tc = _pick(t, _SCAN_CHUNK, SUBLANES)

    def body(br_ref, bi_ref, ar_ref, ai_ref, xr_ref, xi_ref, st_ref):
        @pl.when(pl.program_id(0) == 0)
        def _():
            st_ref[...] = jnp.zeros_like(st_ref)

        ar, ai = ar_ref[...], ai_ref[...]

        def step(i, carry):
            xr, xi = carry
            nr = ar * xr - ai * xi + br_ref[i]
            ni = ar * xi + ai * xr + bi_ref[i]
            xr_ref[i] = nr
            xi_ref[i] = ni
            return nr, ni

        xr, xi = lax.fori_loop(0, tc, step, (st_ref[0], st_ref[1]), unroll=8)
        st_ref[0] = xr
        st_ref[1] = xi

    blk = pl.BlockSpec((tc, _SCAN_ROWS, LANES), lambda i: (i, 0, 0))
    par = pl.BlockSpec((_SCAN_ROWS, LANES), _zero_map(2))
    shp = jax.ShapeDtypeStruct(bre.shape, F32)
    return pl.pallas_call(
        body, grid=(t // tc,), in_specs=[blk, blk, par, par], out_specs=[blk, blk], out_shape=[shp, shp],
        scratch_shapes=[pltpu.VMEM((2, _SCAN_ROWS, LANES), F32)],
        compiler_params=_cparams("arbitrary"), name="s5_scan_fwd")(bre, bim, are, aim)


def _scan_bwd(dxr, dxi, xr, xi, are, aim):
    t = dxr.shape[0]
    tc = _pick(t, _SCAN_CHUNK, SUBLANES)
    nc = t // tc

    def body(dr_ref, di_ref, xr_ref, xi_ref, pr_ref, pi_ref, ar_ref, ai_ref, gr_ref, gi_ref, dar_ref, dai_ref, st_ref):
        step_id = pl.program_id(0)

        @pl.when(step_id == 0)
        def _():
            st_ref[...] = jnp.zeros_like(st_ref)
            dar_ref[...] = jnp.zeros_like(dar_ref)
            dai_ref[...] = jnp.zeros_like(dai_ref)

        ar, ai = ar_ref[...], ai_ref[...]

        def update(i, carry, pxr, pxi):
            gr, gi, dar, dai = carry
            ngr = dr_ref[i] + ar * gr + ai * gi
            ngi = di_ref[i] - ai * gr + ar * gi
            gr_ref[i] = ngr
            gi_ref[i] = ngi
            return ngr, ngi, dar + ngr * pxr + ngi * pxi, dai - ngr * pxi + ngi * pxr

        def step(s, carry):
            i = tc - 1 - s
            return update(i, carry, xr_ref[i - 1], xi_ref[i - 1])

        zero = jnp.zeros((_SCAN_ROWS, LANES), F32)
        carry = lax.fori_loop(0, tc - 1, step, (st_ref[0], st_ref[1], zero, zero), unroll=8)
        has_prev = (step_id < nc - 1).astype(F32)
        gr, gi, dar, dai = update(0, carry, pr_ref[0] * has_prev, pi_ref[0] * has_prev)
        st_ref[0] = gr
        st_ref[1] = gi
        dar_ref[...] += dar
        dai_ref[...] += dai

    blk = pl.BlockSpec((tc, _SCAN_ROWS, LANES), lambda i: (nc - 1 - i, 0, 0))
    prev = pl.BlockSpec((1, _SCAN_ROWS, LANES), lambda i: (jnp.maximum((nc - 1 - i) * tc - 1, 0), 0, 0))
    par = pl.BlockSpec((_SCAN_ROWS, LANES), _zero_map(2))
    shp = jax.ShapeDtypeStruct(dxr.shape, F32)
    psh = jax.ShapeDtypeStruct((_SCAN_ROWS, LANES), F32)
    return pl.pallas_call(
        body, grid=(nc,), in_specs=[blk, blk, blk, blk, prev, prev, par, par],
        out_specs=[blk, blk, par, par], out_shape=[shp, shp, psh, psh],
        scratch_shapes=[pltpu.VMEM((2, _SCAN_ROWS, LANES), F32)],
        compiler_params=_cparams("arbitrary"), name="s5_scan_bwd")(dxr, dxi, xr, xi, xr, xi, are, aim)


def _block_diag(m):
    _, a, b = m.shape
    m4 = m.reshape(N_SSM_BLOCKS, SSM_PACK, a, b)
    eye = jnp.eye(SSM_PACK, dtype=m.dtype)
    return (m4[:, :, :, None, :] * eye[None, :, None, :, None]).reshape(N_SSM_BLOCKS, SSM_PACK * a, SSM_PACK * b)


def _block_diag_extract(m, a, b):
    blocks = [m[:, i * a:(i + 1) * a, i * b:(i + 1) * b] for i in range(SSM_PACK)]
    return jnp.stack(blocks, axis=1).reshape(N_GROUPS_C, a, b)


def _exchange(src, *, gather, name):
    shape = src.shape if gather else src.shape[1:]

    def body(src_ref, out_ref, send_sems, recv_sems, local_sem):
        x, y, c = lax.axis_index("x"), lax.axis_index("y"), lax.axis_index("c")
        me = 4 * x + 2 * y + c
        copies = []
        for r in range(1, N_DEV):
            px = 1 - x if r & 4 else x
            py = 1 - y if r & 2 else y
            pc = 1 - c if r & 1 else c
            piece = src_ref if gather else src_ref.at[4 * px + 2 * py + pc]
            cp = pltpu.make_async_remote_copy(
                src_ref=piece, dst_ref=out_ref.at[me], send_sem=send_sems.at[r - 1], recv_sem=recv_sems.at[r - 1],
                device_id=(px, py, pc), device_id_type=pl.DeviceIdType.MESH)
            cp.start()
            copies.append(cp)
        mine = pltpu.make_async_copy(src_ref if gather else src_ref.at[me], out_ref.at[me], local_sem)
        mine.start()
        for cp in copies:
            cp.wait()
        mine.wait()

    hbm = pl.BlockSpec(memory_space=pl.ANY)
    return pl.pallas_call(
        body, in_specs=[hbm], out_specs=hbm, out_shape=jax.ShapeDtypeStruct((N_DEV,) + tuple(shape), src.dtype),
        scratch_shapes=[pltpu.SemaphoreType.DMA((N_DEV - 1,)), pltpu.SemaphoreType.DMA((N_DEV - 1,)),
                        pltpu.SemaphoreType.DMA(())],
        name=name)(src)


def _mesh_place():
    x, y, c = lax.axis_index("x"), lax.axis_index("y"), lax.axis_index("c")
    other_chips = [(1 - x, y), (x, 1 - y), (1 - x, 1 - y)]
    return x, y, c, other_chips


def _gather_layer(srcs, layer, name):
    n = len(srcs)

    def body(*refs):
        src = [r.at[layer] for r in refs[:n]]
        out = refs[n:2 * n]
        send_sems, recv_sems, local_sems = refs[2 * n:]
        x, y, c, chips = _mesh_place()
        me, sibling = (x, y, c), (x, y, 1 - c)

        def copy(t, k, block, to, from_src=False):
            slot = 4 * block[0] + 2 * block[1] + block[2]
            return pltpu.make_async_remote_copy(
                src_ref=src[t] if from_src else out[t].at[slot], dst_ref=out[t].at[slot],
                send_sem=send_sems.at[t, k], recv_sem=recv_sems.at[t, k], device_id=to, device_id_type=_MESH_ID)

        mine = [pltpu.make_async_copy(src[t], out[t].at[4 * x + 2 * y + c], local_sems.at[t]) for t in range(n)]
        for cp in mine:
            cp.start()
        first = []
        for t in range(n):
            first.append(copy(t, 0, me, sibling, True))
            first += [copy(t, 1 + j, me, (*chip, c), True) for j, chip in enumerate(chips)]
        for cp in first:
            cp.start()
        passed = []
        for j, chip in enumerate(chips):
            for t in range(n):
                copy(t, 1 + j, (*chip, c), me).wait_recv()
                fwd = copy(t, 4 + j, (*chip, c), sibling)
                fwd.start()
                passed.append(fwd)
        for t in range(n):
            copy(t, 0, sibling, me).wait_recv()
            for j, chip in enumerate(chips):
                copy(t, 4 + j, (*chip, 1 - c), me).wait_recv()
        for cp in first + passed:
            cp.wait_send()
        for cp in mine:
            cp.wait()

    return pl.pallas_call(
        body, in_specs=[_HBM] * n, out_specs=[_HBM] * n,
        out_shape=[jax.ShapeDtypeStruct((N_DEV,) + s.shape[1:], s.dtype) for s in srcs],
        scratch_shapes=[pltpu.SemaphoreType.DMA((n, N_DEV - 1)), pltpu.SemaphoreType.DMA((n, N_DEV - 1)),
                        pltpu.SemaphoreType.DMA((n,))],
        name=name)(*srcs)


def _scatter_pair(srcs, name):
    n = len(srcs)

    def body(*refs):
        src, out = refs[:n], refs[n:2 * n]
        send_sems, recv_sems = refs[2 * n:]
        x, y, c, _ = _mesh_place()
        copies = [pltpu.make_async_remote_copy(
            src_ref=src[t].at[:, 1 - c], dst_ref=out[t], send_sem=send_sems.at[t], recv_sem=recv_sems.at[t],
            device_id=(x, y, 1 - c), device_id_type=_MESH_ID) for t in range(n)]
        for cp in copies:
            cp.start()
        for cp in copies:
            cp.wait()

    return pl.pallas_call(
        body, in_specs=[_HBM] * n, out_specs=[_HBM] * n,
        out_shape=[jax.ShapeDtypeStruct((s.shape[0],) + s.shape[2:], s.dtype) for s in srcs],
        scratch_shapes=[pltpu.SemaphoreType.DMA((n,)), pltpu.SemaphoreType.DMA((n,))], name=name)(*srcs)


def _pair_add(src, recv, name):
    nchip, _, r, cdim = src.shape
    tr = _pick(r, 256, 2 * SUBLANES)
    core = lax.axis_index("c").astype(jnp.int32).reshape(1)

    def body(core_ref, s_ref, r_ref, o_ref):
        o_ref[...] = (s_ref[...].astype(F32) + r_ref[...].astype(F32)).astype(o_ref.dtype)

    grid_spec = pltpu.PrefetchScalarGridSpec(
        num_scalar_prefetch=1, grid=(nchip, r // tr),
        in_specs=[pl.BlockSpec((None, None, tr, cdim), lambda k, i, core_ref: (k, core_ref[0], i, 0)),
                  pl.BlockSpec((None, tr, cdim), lambda k, i, core_ref: (k, i, 0))],
        out_specs=pl.BlockSpec((None, tr, cdim), lambda k, i, core_ref: (k, i, 0)))
    return pl.pallas_call(body, grid_spec=grid_spec, out_shape=jax.ShapeDtypeStruct(recv.shape, recv.dtype),
                          compiler_params=_cparams("parallel", "parallel"), name=name)(core, src, recv)


def _scatter_chips(srcs, name):
    n = len(srcs)

    def body(*refs):
        src, out = refs[:n], refs[n:2 * n]
        send_sems, recv_sems, local_sems = refs[2 * n:]
        x, y, c, chips = _mesh_place()
        my_chip = 2 * x + y
        mine = [pltpu.make_async_copy(src[t].at[my_chip], out[t].at[my_chip], local_sems.at[t]) for t in range(n)]
        copies = [pltpu.make_async_remote_copy(
            src_ref=src[t].at[2 * chip[0] + chip[1]], dst_ref=out[t].at[my_chip],
            send_sem=send_sems.at[t, j], recv_sem=recv_sems.at[t, j], device_id=(*chip, c), device_id_type=_MESH_ID)
            for t in range(n) for j, chip in enumerate(chips)]
        for cp in mine + copies:
            cp.start()
        for cp in copies + mine:
            cp.wait()

    return pl.pallas_call(
        body, in_specs=[_HBM] * n, out_specs=[_HBM] * n,
        out_shape=[jax.ShapeDtypeStruct(s.shape, s.dtype) for s in srcs],
        scratch_shapes=[pltpu.SemaphoreType.DMA((n, 3)), pltpu.SemaphoreType.DMA((n, 3)), pltpu.SemaphoreType.DMA((n,))],
        name=name)(*srcs)


def _chip_sums(grads, tag):
    views = [g.reshape(N_DEV // 2, 2, g.shape[0] // N_DEV, g.shape[1]) for g in grads]
    from_sibling = _scatter_pair(views, name="scatter_pair_" + tag)
    return [_pair_add(v, s, name="pair_add") for v, s in zip(views, from_sibling)]


def _sum_chips(parts):
    return _rowmap(_sum_parts_fn, [], stacks=[parts], row_outs=[(parts.shape[2], F32)], tr=128, name="sum_chips")[0]


def _buffer_roles(kinds, bufs):
    carried = [k for k in kinds if k in bufs]
    return carried, [k for k in kinds if k not in bufs]


def _gather_ride(sends, forwards, bufs):
    carried, created = _buffer_roles(list(dict.fromkeys([s[0] for s in sends] + [f[0] for f in forwards])), bufs)
    shape_of = {s[0]: jax.ShapeDtypeStruct((N_DEV,) + s[1].shape, s[1].dtype) for s in sends}
    order = carried + created

    def copies(in_refs, buf_refs, sems):
        x, y, c, chips = _mesh_place()
        buf = dict(zip(order, buf_refs))
        out, s0 = [], 0
        for (kind, _, r0, nr), src in zip(sends, in_refs):
            mine, dst = src.at[pl.ds(r0, nr)], buf[kind].at[4 * x + 2 * y + c, pl.ds(r0, nr)]
            out.append(pltpu.make_async_copy(mine, dst, sems.at[s0 + 8]))
            for k, peer in enumerate([(x, y, 1 - c)] + [(*chip, c) for chip in chips]):
                out.append(pltpu.make_async_remote_copy(src_ref=mine, dst_ref=dst, send_sem=sems.at[s0 + k],
                                                        recv_sem=sems.at[s0 + 4 + k], device_id=peer, device_id_type=_MESH_ID))
            s0 += 9
        for kind, r0, nr in forwards:
            for j, chip in enumerate(chips):
                blk = buf[kind].at[4 * chip[0] + 2 * chip[1] + c, pl.ds(r0, nr)]
                out.append(pltpu.make_async_remote_copy(src_ref=blk, dst_ref=blk, send_sem=sems.at[s0 + j],
                                                        recv_sem=sems.at[s0 + 3 + j], device_id=(x, y, 1 - c),
                                                        device_id_type=_MESH_ID))
            s0 += 6
        return out

    def start(in_refs, buf_refs, sems):
        for cp in copies(in_refs, buf_refs, sems):
            cp.start()

    def finish(in_refs, buf_refs, sems):
        for cp in copies(in_refs, buf_refs, sems):
            cp.wait()

    ride = _Ride(inputs=tuple(s[1] for s in sends), carried=tuple(bufs[k] for k in carried),
                 created=tuple(shape_of[k] for k in created), n_sems=9 * len(sends) + 6 * len(forwards),
                 start=start, finish=finish)
    return ride, order


def _scatter_ride(pieces, bufs):
    carried, created = _buffer_roles(list(dict.fromkeys(p[0] for p in pieces)), bufs)
    shape_of = {p[0]: jax.ShapeDtypeStruct(p[1].shape, p[1].dtype) for p in pieces}
    order = carried + created

    def copies(in_refs, buf_refs, sems):
        x, y, c, chips = _mesh_place()
        buf = dict(zip(order, buf_refs))
        out, s0 = [], 0
        for (kind, _, r0, nr), src in zip(pieces, in_refs):
            dst = buf[kind].at[2 * x + y, pl.ds(r0, nr)]
            out.append(pltpu.make_async_copy(src.at[2 * x + y, pl.ds(r0, nr)], dst, sems.at[s0 + 6]))
            for j, chip in enumerate(chips):
                out.append(pltpu.make_async_remote_copy(
                    src_ref=src.at[2 * chip[0] + chip[1], pl.ds(r0, nr)], dst_ref=dst, send_sem=sems.at[s0 + j],
                    recv_sem=sems.at[s0 + 3 + j], device_id=(*chip, c), device_id_type=_MESH_ID))
            s0 += 7
        return out

    def start(in_refs, buf_refs, sems):
        for cp in copies(in_refs, buf_refs, sems):
            cp.start()

    def finish(in_refs, buf_refs, sems):
        for cp in copies(in_refs, buf_refs, sems):
            cp.wait()

    ride = _Ride(inputs=tuple(p[1] for p in pieces), carried=tuple(bufs[k] for k in carried),
                 created=tuple(shape_of[k] for k in created), n_sems=7 * len(pieces), start=start, finish=finish)
    return ride, order


GATHER_PLAN = (
    ("mm_in_qkv", (("w_in", 0, 3),)),
    ("mm_in_gl", (("w_in", 1, 3),)),
    ("mm_o", (("w_o", 0, 1),)),
    ("mm_ffn_in", (("w_in", 2, 3), ("w_ffn_in", 0, 2))),
    ("mm_ffn_out", (("w_ffn_in", 1, 2), ("w_ffn_out", 0, 1), ("w_pa", 0, 1), ("w_pb", 0, 1), ("w_pc", 0, 1),
                    ("w_glu", 0, 1))),
    ("norm2", ()),
)
SCATTER_PLAN = (
    ("mm_ffn_out_dw", (("w_ffn_out", 0, 1),)),
    ("mm_ffn_out_dx", (("w_in", 0, 3),)),
    ("mm_ffn_in_dw", (("w_in", 1, 3), ("w_in", 2, 3))),
    ("mm_ffn_in_dx", (("w_ffn_in", 0, 1),)),
    ("mm_o_dw", (("w_o", 0, 1),)),
    ("mm_o_dx", (("w_pa", 0, 1), ("w_pb", 0, 1), ("w_pc", 0, 1), ("w_glu", 0, 1))),
)


def _row_part(rows, part, parts):
    assert rows % (parts * 2 * SUBLANES) == 0
    return part * (rows // parts), rows // parts


class _Carried:
    def __init__(self, plan, blocks, make_ride, forwards_too):
        self.plan, self.blocks, self.make_ride, self.forwards_too = dict(plan), blocks, make_ride, forwards_too
        self.bufs, self.to_forward, self.order = {}, [], []

    def ride(self, host):
        if self.blocks is None or host not in self.plan:
            self.order = []
            return None
        sends = [(k, self.blocks[k], *_row_part(self.blocks[k].shape[-2], part, parts)) for k, part, parts in self.plan[host]]
        if self.forwards_too:
            ride, self.order = self.make_ride(sends, self.to_forward, self.bufs)
            self.to_forward = [(k, r0, nr) for k, _, r0, nr in sends]
        else:
            ride, self.order = self.make_ride(sends, self.bufs)
        return ride

    def took(self, ride_outs):
        for k, buf in zip(self.order, ride_outs[0] if ride_outs else []):
            self.bufs[k] = buf


def _hosted(comm, fn, *args, name, **kwargs):
    res, ride_outs = fn(*args, name=name, rides=[comm.ride(name)], **kwargs)
    comm.took(ride_outs)
    return res


def _small_sizes(shapes):
    return [int(np.prod(shapes[n])) for n in SMALL]


def _pack_small(vals):
    flat = jnp.concatenate([vals[n].reshape(-1).astype(F32) for n in SMALL])
    rows = -(-flat.shape[0] // (LANES * N_DEV * SUBLANES)) * (N_DEV * SUBLANES)
    return jnp.pad(flat, (0, rows * LANES - flat.shape[0])).reshape(rows, LANES)


def _unpack_small(packed, shapes):
    flat = packed.reshape(-1)
    out, off = {}, 0
    for n, size in zip(SMALL, _small_sizes(shapes)):
        out[n] = flat[off:off + size].reshape(shapes[n])
        off += size
    return out


def _row(v):
    return v.reshape(1, -1)


def _layer_params(l, full, small):
    o1, o2, o3 = 3 * QKV_WIDTH, 3 * QKV_WIDTH + 2 * WIDTH_B, 3 * QKV_WIDTH + 2 * WIDTH_B + WIDTH_C
    b_in = small["b_in"][l]
    p = {
        "in_pieces": (("qkv", 0, o1), ("zb", o1, o2 - o1), ("uc", o2, o3 - o2), ("gl", o3, b_in.shape[0] - o3)),
        "b_qkv": _row(b_in[:o1]), "b_zb": _row(b_in[o1:o2]), "b_uc": _row(b_in[o2:o3]), "b_gl": _row(b_in[o3:]),
        "sgu_ln_g": _row(small["sgu_ln_g"][l]), "sgu_ln_b": _row(small["sgu_ln_b"][l]),
        "w_s": small["w_s"][l], "b_s_t": small["b_s"][l].T,
        "lam_re": small["lam_re"][l][:, None, :], "lam_im": small["lam_im"][l][:, None, :],
        "log_dt": small["log_dt"][l][:, None, None],
        "b_re_t": small["b_re"][l].transpose(0, 2, 1), "b_im_t": small["b_im"][l].transpose(0, 2, 1),
        "c_re_t": small["c_re"][l].transpose(0, 2, 1), "c_im_t": small["c_im"][l].transpose(0, 2, 1),
        "d_skip": _row(small["d_skip"][l]), "b_glu": _row(small["b_glu"][l]),
        "ln1_g": _row(small["ln1_g"][l]), "ln1_b": _row(small["ln1_b"][l]),
        "ln2_g": _row(small["ln2_g"][l]), "ln2_b": _row(small["ln2_b"][l]),
    }
    for n in SHARDED:
        p[n] = full[n]
    return p


def _scan_view(a):
    return a.reshape(a.shape[0], _SCAN_ROWS, LANES)


def _twice(fn):
    def both(*args):
        y = fn(*args)
        return y, y
    return both


def _layer_fwd(x, xb, p, biases, comm):
    t, d = x.shape
    r = {"x": x, "xb": xb}
    for piece, off, n in p["in_pieces"]:
        r[piece] = _hosted(comm, _mm, xb, p["w_in"], tb=True, b_off=off, n=n, bias=p["b_" + piece],
                           out_dtype=BF16 if piece == "qkv" else F32, name="mm_in_" + piece)
    ol = []
    for g, dil in enumerate(ATT_DILATIONS):
        ol += list(_attn_fwd(r["qkv"], biases[g], g, dil))
    r["ol"] = ol
    r["ya"], r["ya_b"] = _rowmap(_twice(_combine), ol, row_outs=[(WIDTH_A, F32), (WIDTH_A, BF16)], tr=512,
                                 name="attn_combine")
    r["yb"] = _gmlp_fwd(r["zb"], p["sgu_ln_g"], p["sgu_ln_b"], p["w_s"], p["b_s_t"])
    ab_re, ab_im, bb_re_t, bb_im_t = _s5_disc_fwd(p["lam_re"], p["lam_im"], p["log_dt"], p["b_re_t"], p["b_im_t"])
    r["a_re"], r["a_im"] = ab_re.reshape(_SCAN_ROWS, LANES), ab_im.reshape(_SCAN_ROWS, LANES)
    r["bmat_re"], r["bmat_im"] = _block_diag(bb_re_t), _block_diag(bb_im_t)
    r["cmat_re"], r["cmat_im"] = _block_diag(p["c_re_t"]), _block_diag(p["c_im_t"])
    bu_re = _bdmm(r["uc"], r["bmat_re"], name="s5_in_re")
    bu_im = _bdmm(r["uc"], r["bmat_im"], name="s5_in_im")
    xr, xi = _scan_fwd(_scan_view(bu_re), _scan_view(bu_im), r["a_re"], r["a_im"])
    r["xr"], r["xi"] = xr.reshape(t, SSM_COLS), xi.reshape(t, SSM_COLS)
    y_re = _bdmm(r["xr"], r["cmat_re"], name="s5_out_re")
    y_im = _bdmm(r["xi"], r["cmat_im"], scale=-1.0, name="s5_out_im")
    r["ys"], r["ycp"] = _rowmap(_s5_out, [y_re, y_im, r["uc"]], consts=[p["d_skip"]],
                                row_outs=[(WIDTH_C, F32)] * 2, tr=512, name="s5_out_act")
    r["glin"] = _mm(r["ycp"], p["w_glu"], bias=p["b_glu"], name="mm_glu")
    r["yc"] = _rowmap(_glu, [r["ycp"], r["glin"]], row_outs=[(WIDTH_C, BF16)], tr=512, name="glu")[0]
    r["pa"] = _mm(r["ya_b"], p["w_pa"], tb=True, name="mm_pa")
    r["pb"] = _mm(r["yb"], p["w_pb"], tb=True, name="mm_pb")
    r["pc"] = _mm(r["yc"], p["w_pc"], tb=True, name="mm_pc")
    r["merged"] = _rowmap(_merge, [r["gl"], r["pa"], r["pb"], r["pc"]], row_outs=[(d, BF16)], name="merge")[0]
    r["mo"] = _hosted(comm, _mm, r["merged"], p["w_o"], name="mm_o")
    r["xm"], r["xm_b"] = _rowmap(_twice(_post_norm), [x, r["mo"]], consts=[p["ln1_g"], p["ln1_b"]],
                                 row_outs=[(d, F32), (d, BF16)], name="norm1")
    r["gu"] = _hosted(comm, _mm, r["xm_b"], p["w_ffn_in"], tb=True, name="mm_ffn_in")
    r["act"] = _rowmap(_swiglu, [r["gu"]], row_outs=[(r["gu"].shape[1] // 2, BF16)], tr=128, name="swiglu")[0]
    r["f"] = _hosted(comm, _mm, r["act"], p["w_ffn_out"], name="mm_ffn_out")
    out, out_b = _hosted(comm, _rowmap, _twice(_post_norm), [r["xm"], r["f"]], consts=[p["ln2_g"], p["ln2_b"]],
                         row_outs=[(d, F32), (d, BF16)], name="norm2")
    return out, out_b, r


def _layer_bwd(dout, r, p, biases, consts, comm):
    t, d = dout.shape
    gw, gs = {}, {}
    ffw = r["gu"].shape[1]
    dxm, df, gs["ln2_g"], gs["ln2_b"] = _rowmap(
        _post_norm_bwd, [r["xm"], r["f"], dout], consts=[p["ln2_g"], p["ln2_b"]],
        row_outs=[(d, F32), (d, BF16)], red_outs=[(1, d)] * 2, name="norm2_bwd")
    gw["w_ffn_out"] = _hosted(comm, _mm, r["act"], df, ta=True, out_dtype=BF16, name="mm_ffn_out_dw")
    dact = _hosted(comm, _mm, df, p["w_ffn_out"], tb=True, name="mm_ffn_out_dx")
    dgu = _rowmap(_swiglu_bwd, [r["gu"], dact], row_outs=[(ffw, BF16)], tr=128, name="swiglu_bwd")[0]
    gw["w_ffn_in"] = _hosted(comm, _mm, dgu, r["xm_b"], ta=True, out_dtype=BF16, name="mm_ffn_in_dw")
    dxm = _hosted(comm, _mm, dgu, p["w_ffn_in"], add=dxm, name="mm_ffn_in_dx")
    dx, dmo, gs["ln1_g"], gs["ln1_b"] = _rowmap(
        _post_norm_bwd, [r["x"], r["mo"], dxm], consts=[p["ln1_g"], p["ln1_b"]],
        row_outs=[(d, F32), (d, BF16)], red_outs=[(1, d)] * 2, name="norm1_bwd")
    gw["w_o"] = _hosted(comm, _mm, r["merged"], dmo, ta=True, out_dtype=BF16, name="mm_o_dw")
    dmerged = _hosted(comm, _mm, dmo, p["w_o"], tb=True, name="mm_o_dx")
    dgl, dpa, dpb, dpc, db_gl = _rowmap(
        _merge_bwd, [r["gl"], r["pa"], r["pb"], r["pc"], dmerged],
        row_outs=[(3 * d, BF16), (d, BF16), (d, BF16), (d, BF16)], red_outs=[(1, 3 * d)], tr=128, name="merge_bwd")
    gw["w_pa"] = _mm(dpa, r["ya_b"], ta=True, out_dtype=BF16, name="mm_pa_dw")
    gw["w_pb"] = _mm(dpb, r["yb"], ta=True, out_dtype=BF16, name="mm_pb_dw")
    gw["w_pc"] = _mm(dpc, r["yc"], ta=True, out_dtype=BF16, name="mm_pc_dw")
    dya = _mm(dpa, p["w_pa"], name="mm_pa_dx")
    dyb = _mm(dpb, p["w_pb"], name="mm_pb_dx")
    dyc = _mm(dpc, p["w_pc"], name="mm_pc_dx")
    dycp, dglin, gs["b_glu"] = _rowmap(_glu_bwd, [r["ycp"], r["glin"], dyc], row_outs=[(WIDTH_C, F32), (WIDTH_C, BF16)],
                                       red_outs=[(1, WIDTH_C)], tr=512, name="glu_bwd")
    gw["w_glu"] = _mm(r["ycp"], dglin, ta=True, out_dtype=BF16, name="mm_glu_dw")
    dycp = _mm(dglin, p["w_glu"], tb=True, add=dycp, name="mm_glu_dx")
    dys, duc, gs["d_skip"] = _rowmap(_s5_out_bwd, [r["ys"], r["uc"], dycp], consts=[p["d_skip"]],
                                     row_outs=[(WIDTH_C, F32)] * 2, red_outs=[(1, WIDTH_C)], tr=512, name="s5_out_act_bwd")
    dxr = _bdmm(dys, r["cmat_re"], tb=True, name="s5_out_re_dx")
    dxi = _bdmm(dys, r["cmat_im"], tb=True, scale=-1.0, name="s5_out_im_dx")
    d_cmat_re = _bdmm_tn(r["xr"], dys, N_SSM_BLOCKS, name="s5_out_re_dw")
    d_cmat_im = _bdmm_tn(r["xi"], dys, N_SSM_BLOCKS, scale=-1.0, name="s5_out_im_dw")
    g_re, g_im, da_re, da_im = _scan_bwd(_scan_view(dxr), _scan_view(dxi), _scan_view(r["xr"]), _scan_view(r["xi"]),
                                         r["a_re"], r["a_im"])
    g_re, g_im = g_re.reshape(t, SSM_COLS), g_im.reshape(t, SSM_COLS)
    duc = _bdmm(g_re, r["bmat_re"], tb=True, add=duc, name="s5_in_re_dx")
    duc = _bdmm(g_im, r["bmat_im"], tb=True, add=duc, name="s5_in_im_dx")
    d_bmat_re = _bdmm_tn(r["uc"], g_re, N_SSM_BLOCKS, name="s5_in_re_dw")
    d_bmat_im = _bdmm_tn(r["uc"], g_im, N_SSM_BLOCKS, name="s5_in_im_dw")
    cts = (da_re.reshape(N_GROUPS_C, 1, SSM_STATE), da_im.reshape(N_GROUPS_C, 1, SSM_STATE),
           _block_diag_extract(d_bmat_re, SSM_GROUP, SSM_STATE), _block_diag_extract(d_bmat_im, SSM_GROUP, SSM_STATE))
    d_lr, d_li, d_ldt, d_br_t, d_bi_t = _s5_disc_bwd(p["lam_re"], p["lam_im"], p["log_dt"], p["b_re_t"], p["b_im_t"], cts)
    gs["lam_re"], gs["lam_im"], gs["log_dt"] = d_lr[:, 0, :], d_li[:, 0, :], d_ldt[:, 0, 0]
    gs["b_re"], gs["b_im"] = d_br_t.transpose(0, 2, 1), d_bi_t.transpose(0, 2, 1)
    gs["c_re"] = _block_diag_extract(d_cmat_re, SSM_STATE, SSM_GROUP).transpose(0, 2, 1)
    gs["c_im"] = _block_diag_extract(d_cmat_im, SSM_STATE, SSM_GROUP).transpose(0, 2, 1)
    dzb, db_zb, gs["sgu_ln_g"], gs["sgu_ln_b"], gs["w_s"], dbs_t = _gmlp_bwd(
        r["zb"], dyb, p["sgu_ln_g"], p["sgu_ln_b"], p["w_s"], p["b_s_t"], consts["group_sel"])
    gs["b_s"] = dbs_t[:, :N_GROUPS_B].T
    do_corr = _rowmap(_combine_bwd, r["ol"] + [r["ya"], dya], consts=[consts["head_ones"]],
                      row_outs=[(WIDTH_A, F32)] * 6, tr=512, name="attn_combine_bwd")
    dq, dk, dv, dbias = [], [], [], []
    for g, dil in enumerate(ATT_DILATIONS):
        do_g, corr_g, lse_g = do_corr[g], do_corr[3 + g], r["ol"][2 * g + 1]
        dq.append(_attn_dq(r["qkv"], biases[g], do_g, lse_g, corr_g, g, dil))
        dk_g, dv_g, db_g = _attn_dkv(r["qkv"], biases[g], do_g, lse_g, corr_g, g, dil)
        dk.append(dk_g)
        dv.append(dv_g)
        dbias.append(db_g)
    cast_colsum = lambda a: (a, jnp.sum(a, axis=0, keepdims=True))
    dqkv, db_qkv = _rowmap(cast_colsum, [jnp.concatenate(dq + dk + dv, axis=1)], row_outs=[(3 * QKV_WIDTH, BF16)],
                           red_outs=[(1, 3 * QKV_WIDTH)], tr=512, name="cast_colsum_qkv")
    duc, db_uc = _rowmap(cast_colsum, [duc], row_outs=[(WIDTH_C, BF16)], red_outs=[(1, WIDTH_C)], tr=512,
                         name="cast_colsum_uc")
    dpieces = {"qkv": dqkv, "zb": dzb, "uc": duc, "gl": dgl}
    dws = []
    for piece, off, n in p["in_pieces"]:
        dws.append(_mm(dpieces[piece], r["xb"], ta=True, out_dtype=BF16, name="mm_in_dw_" + piece))
        dx = _mm(dpieces[piece], p["w_in"], b_off=off, add=dx, name="mm_in_dx_" + piece)
    gw["w_in"] = jnp.concatenate(dws, axis=0)
    gs["b_in"] = jnp.concatenate([db_qkv, db_zb, db_uc, db_gl], axis=1)[0]
    for n in ("sgu_ln_g", "sgu_ln_b", "d_skip", "b_glu", "ln1_g", "ln1_b", "ln2_g", "ln2_b"):
        gs[n] = gs[n][0]
    return dx, gw, gs, dbias


def _cast_bf16(w):
    w2 = w.reshape(-1, w.shape[-1])
    out = _rowmap(lambda a: a, [w2], row_outs=[(w2.shape[1], BF16)], tr=512, name="cast_bf16")[0]
    return out.reshape(w.shape)


def _static_consts():
    head_ones = np.kron(np.eye(HEADS_PER_GROUP, dtype=np.float32), np.ones((HEAD_DIM, HEAD_DIM), np.float32))
    group_sel = np.zeros((WIDTH_B, LANES), np.float32)
    group_sel[np.arange(WIDTH_B), np.arange(WIDTH_B) // CHUNK] = 1.0
    return {"head_ones": jnp.asarray(head_ones), "group_sel": jnp.asarray(group_sel)}


def _step(x, tgt, w, m, v):
    shapes = {n: w[n].shape for n in WEIGHTS}
    consts = _static_consts()
    mine_bf = {n: _cast_bf16(w[n].transpose(0, 2, 1) if n in TRANSPOSED else w[n]) for n in SHARDED}
    small = {n: w[n] for n in SMALL}
    buckets = [jnp.asarray(_bucket_table(dil)) for dil in ATT_DILATIONS]
    biases = [_bias_fwd(w["rel_bias"], buckets[g], g) for g in range(len(ATT_DILATIONS))]
    params, saved = [], []
    h, hb = _rowmap(_twice(lambda a: a), [x], row_outs=[(x.shape[1], F32), (x.shape[1], BF16)], name="cast_x")
    gathered = dict(zip(SHARDED, _gather_layer([mine_bf[n] for n in SHARDED], 0, name="gather_layer0")))
    for l in range(DEPTH):
        p = _layer_params(l, {n: g.reshape(-1, g.shape[2]) for n, g in gathered.items()}, small)
        ahead = _Carried(GATHER_PLAN, {n: mine_bf[n][l + 1] for n in SHARDED} if l + 1 < DEPTH else None,
                         _gather_ride, forwards_too=True)
        h, hb, r = _layer_fwd(h, hb, p, biases, ahead)
        gathered = ahead.bufs
        params.append(p)
        saved.append(r)
    dy, loss_part = _rowmap(_loss_fn, [h, tgt], row_outs=[(h.shape[1], F32)], red_outs=[(1, LANES)], name="loss")
    loss = lax.psum(loss_part[0, 0], MESH_AXES)
    g_mine, gs_layers = {n: [None] * DEPTH for n in SHARDED}, [None] * DEPTH
    dbias_sum = None
    behind = _Carried(SCATTER_PLAN, None, _scatter_ride, forwards_too=False)
    for l in reversed(range(DEPTH)):
        dy, gw, gs_layers[l], dbias = _layer_bwd(dy, saved[l], params[l], biases, consts, behind)
        saved[l] = None
        if behind.blocks is not None:
            for n in SHARDED:
                g_mine[n][l + 1] = _sum_chips(behind.bufs[n])
        sums = _chip_sums([gw[n] for n in SHARDED], f"layer{l}")
        behind = _Carried(SCATTER_PLAN, dict(zip(SHARDED, sums)), _scatter_ride, forwards_too=False)
        if l == 0:
            for n, parts in zip(SHARDED, _scatter_chips(sums, name="scatter_chips_layer0")):
                g_mine[n][0] = _sum_chips(parts)
        if dbias_sum is None:
            dbias_sum = dbias
        else:
            dbias_sum = [_rowmap(lambda a, b: a + b, [a.reshape(-1, 2 * ATT_BLOCK), b.reshape(-1, 2 * ATT_BLOCK)],
                                 row_outs=[(2 * ATT_BLOCK, F32)], name="dbias_add")[0].reshape(a.shape)
                         for a, b in zip(dbias_sum, dbias)]
    drel = [_bias_bwd(dbias_sum[g], buckets[g], g) for g in range(len(ATT_DILATIONS))]
    drel = _rowmap(lambda a, b, c: a + b + c, drel, row_outs=[(LANES, F32)], name="drel_add")[0]
    grad_small_local = {n: jnp.stack([gs_layers[l][n] for l in range(DEPTH)]) for n in SMALL if n != "rel_bias"}
    grad_small_local["rel_bias"] = drel[:, :shapes["rel_bias"][1]]
    out_g, out_d, out_m, out_v = {}, {}, {}, {}
    for n in SHARDED:
        g = jnp.stack(g_mine[n])
        out_g[n] = g.transpose(0, 2, 1) if n in TRANSPOSED else g
        cols = shapes[n][-1]
        res = _rowmap(_adamw, [a.reshape(-1, cols) for a in (w[n], out_g[n], m[n], v[n])],
                      row_outs=[(cols, F32)] * 3, tr=128, name="adamw_" + n)
        out_d[n], out_m[n], out_v[n] = [a.reshape(shapes[n]) for a in res]
    packed = _pack_small(grad_small_local)
    rows = packed.shape[0] // N_DEV
    parts = _exchange(packed.reshape(N_DEV, rows, LANES), gather=False, name="scatter_small")
    mine = _rowmap(_sum_parts_fn, [], stacks=[parts], row_outs=[(LANES, F32)], name="sum_small")[0]
    g_small = _exchange(mine, gather=True, name="gather_small").reshape(-1, LANES)
    res = _rowmap(lambda w_, g_, m_, v_: _adamw(w_, g_, m_, v_),
                  [_pack_small(w), g_small, _pack_small(m), _pack_small(v)],
                  row_outs=[(LANES, F32)] * 3, name="adamw_small")
    small_shapes = {n: shapes[n] for n in SMALL}
    out_g.update(_unpack_small(g_small, small_shapes))
    for dst, packed_res in zip((out_d, out_m, out_v), res):
        dst.update(_unpack_small(packed_res, small_shapes))
    return loss, dy, out_g, out_d, out_m, out_v


def kernel(x, w_in, b_in, rel_bias, sgu_ln_g, sgu_ln_b, w_s, b_s, lam_re, lam_im, log_dt, b_re, b_im, c_re, c_im, d_skip, w_glu, b_glu, w_pa, w_pb, w_pc, w_o, ln1_g, ln1_b, w_ffn_in, w_ffn_out, ln2_g, ln2_b, loss_target, m_w_in, m_b_in, m_rel_bias, m_sgu_ln_g, m_sgu_ln_b, m_w_s, m_b_s, m_lam_re, m_lam_im, m_log_dt, m_b_re, m_b_im, m_c_re, m_c_im, m_d_skip, m_w_glu, m_b_glu, m_w_pa, m_w_pb, m_w_pc, m_w_o, m_ln1_g, m_ln1_b, m_w_ffn_in, m_w_ffn_out, m_ln2_g, m_ln2_b, v_w_in, v_b_in, v_rel_bias, v_sgu_ln_g, v_sgu_ln_b, v_w_s, v_b_s, v_lam_re, v_lam_im, v_log_dt, v_b_re, v_b_im, v_c_re, v_c_im, v_d_skip, v_w_glu, v_b_glu, v_w_pa, v_w_pb, v_w_pc, v_w_o, v_ln1_g, v_ln1_b, v_w_ffn_in, v_w_ffn_out, v_ln2_g, v_ln2_b):
    args = dict(locals())
    w = {n: args[n] for n in WEIGHTS}
    m = {n: args["m_" + n] for n in WEIGHTS}
    v = {n: args["v_" + n] for n in WEIGHTS}
    loss, dx, g, d, nm, nv = _step(x[0], loss_target[0], w, m, v)
    return (loss, dx[None], *[g[n] for n in WEIGHTS], *[d[n] for n in WEIGHTS],
            *[nm[n] for n in WEIGHTS], *[nv[n] for n in WEIGHTS])
```

```python
import functools
import math
from typing import Callable, NamedTuple

import numpy as np
import jax
import jax.numpy as jnp
from jax import lax
from jax.experimental import pallas as pl
from jax.experimental.pallas import tpu as pltpu

F32 = jnp.float32
BF16 = jnp.bfloat16

MESH_AXES = ("x", "y", "c")
N_DEV = 8
DEPTH = 4

ATT_DILATIONS = (1, 4, 16)
ATT_STEPS = 128
HEADS_PER_GROUP = 8
HEAD_DIM = 64
QKV_WIDTH = 1536
WIDTH_A = HEADS_PER_GROUP * HEAD_DIM
ATT_BLOCK = 128
N_REL_BUCKETS = 32
REL_MAX_DIST = 2048
NEG_INF = -1e30
CHUNK = 128
WIDTH_B = 768
N_GROUPS_B = 6
WIDTH_C = 768
SSM_GROUP = 16
N_GROUPS_C = 48
SSM_STATE = 64
SSM_PACK = 8
N_SSM_BLOCKS = N_GROUPS_C // SSM_PACK
SSM_COLS = N_GROUPS_C * SSM_STATE
ALPHA = (2 * DEPTH) ** 0.25

ADAM_LR = 0.001
ADAM_B1 = 0.9
ADAM_B2 = 0.999
ADAM_EPS = 1e-08
ADAM_WD = 0.01
ADAM_STEP = 10

LANES = 128
SUBLANES = 8
VMEM_LIMIT = 48 * 1024 * 1024

SHARDED = ("w_in", "w_glu", "w_pa", "w_pb", "w_pc", "w_o", "w_ffn_in", "w_ffn_out")
TRANSPOSED = ("w_in", "w_pa", "w_pb", "w_pc", "w_ffn_in")
SMALL = ("b_in", "rel_bias", "sgu_ln_g", "sgu_ln_b", "w_s", "b_s", "lam_re", "lam_im", "log_dt",
         "b_re", "b_im", "c_re", "c_im", "d_skip", "b_glu", "ln1_g", "ln1_b", "ln2_g", "ln2_b")
WEIGHTS = ("w_in", "b_in", "rel_bias", "sgu_ln_g", "sgu_ln_b", "w_s", "b_s", "lam_re", "lam_im",
           "log_dt", "b_re", "b_im", "c_re", "c_im", "d_skip", "w_glu", "b_glu", "w_pa", "w_pb",
           "w_pc", "w_o", "ln1_g", "ln1_b", "w_ffn_in", "w_ffn_out", "ln2_g", "ln2_b")


def _pick(dim, target, mult):
    best = None
    for t in range(mult, min(dim, target) + 1, mult):
        if dim % t == 0:
            best = t
    return dim if best is None else best


def _cparams(*sem):
    return pltpu.CompilerParams(dimension_semantics=sem, vmem_limit_bytes=VMEM_LIMIT)


def _zero_map(ndim):
    return lambda *_: (0,) * ndim


_HBM = pl.BlockSpec(memory_space=pl.ANY)
_MESH_ID = pl.DeviceIdType.MESH


class _Ride(NamedTuple):
    inputs: tuple
    carried: tuple
    created: tuple
    n_sems: int
    start: Callable
    finish: Callable


def _pallas(body, *, grid, in_specs, out_specs, out_shape, args, scratch=(), semantics, rides=(), aliases=None, name):
    rides = [r for r in rides if r is not None]
    n_in, n_out, n_scr = len(args), len(out_shape), len(scratch)
    r_args, r_shapes, aliases, spans = [], [], dict(aliases or {}), []
    for r in rides:
        i0, o0 = len(r_args), len(r_shapes)
        r_args += [*r.inputs, *r.carried]
        for k, a in enumerate(r.carried):
            aliases[n_in + i0 + len(r.inputs) + k] = n_out + o0 + k
        r_shapes += [jax.ShapeDtypeStruct(a.shape, a.dtype) for a in r.carried] + list(r.created)
        spans.append((i0, len(r.inputs), o0, len(r.carried) + len(r.created)))

    def full_body(*refs):
        host_in, ride_in = refs[:n_in], refs[n_in:n_in + len(r_args)]
        p = n_in + len(r_args)
        host_out, ride_out = refs[p:p + n_out], refs[p + n_out:p + n_out + len(r_shapes)]
        p += n_out + len(r_shapes)
        host_scr, ride_sems = refs[p:p + n_scr], refs[p + n_scr:]
        ids = [pl.program_id(k) for k in range(len(grid))]
        first = functools.reduce(jnp.logical_and, [i == 0 for i in ids])
        last = functools.reduce(jnp.logical_and, [i == g - 1 for i, g in zip(ids, grid)])

        def each(method):
            for r, (i0, ni, o0, no), sems in zip(rides, spans, ride_sems):
                getattr(r, method)(ride_in[i0:i0 + ni], ride_out[o0:o0 + no], sems)

        if rides:
            pl.when(first)(lambda: each("start"))
        body(*host_in, *host_out, *host_scr)
        if rides:
            pl.when(last)(lambda: each("finish"))

    if rides:
        semantics = ("arbitrary",) * len(grid)
    outs = pl.pallas_call(
        full_body, grid=grid, in_specs=list(in_specs) + [_HBM] * len(r_args),
        out_specs=list(out_specs) + [_HBM] * len(r_shapes), out_shape=list(out_shape) + r_shapes,
        scratch_shapes=list(scratch) + [pltpu.SemaphoreType.DMA((r.n_sems,)) for r in rides],
        input_output_aliases=aliases, compiler_params=_cparams(*semantics), name=name)(*args, *r_args)
    return outs[:n_out], [outs[n_out + o0:n_out + o0 + no] for _, _, o0, no in spans]


def _rowmap(fn, rows, consts=(), stacks=(), row_outs=(), red_outs=(), tr=256, name=None, rides=None):
    t = rows[0].shape[0] if rows else stacks[0].shape[1]
    dtypes = [a.dtype for a in (*rows, *stacks)] + [dt for _, dt in row_outs]
    packed = any(jnp.dtype(dt).itemsize < 4 for dt in dtypes)
    tr = _pick(t, tr, 2 * SUBLANES if packed else SUBLANES)
    n_r, n_c, n_s, n_o = len(rows), len(consts), len(stacks), len(row_outs)

    def body(*refs):
        ins = [r[...] for r in refs[:n_r + n_c + n_s]]
        outs = refs[n_r + n_c + n_s:n_r + n_c + n_s + n_o]
        reds = refs[n_r + n_c + n_s + n_o:]
        res = fn(*ins)
        if not isinstance(res, (tuple, list)):
            res = (res,)
        for o, v in zip(outs, res[:n_o]):
            o[...] = v.astype(o.dtype)
        if reds:
            @pl.when(pl.program_id(0) == 0)
            def _():
                for r in reds:
                    r[...] = jnp.zeros_like(r)
            for r, v in zip(reds, res[n_o:]):
                r[...] += v

    in_specs = [pl.BlockSpec((tr, r.shape[1]), lambda i: (i, 0)) for r in rows]
    in_specs += [pl.BlockSpec(c.shape, _zero_map(c.ndim)) for c in consts]
    in_specs += [pl.BlockSpec((s.shape[0], tr, s.shape[2]), lambda i: (0, i, 0)) for s in stacks]
    out_specs = [pl.BlockSpec((tr, w), lambda i: (i, 0)) for w, _ in row_outs]
    out_specs += [pl.BlockSpec(s, _zero_map(len(s))) for s in red_outs]
    out_shape = [jax.ShapeDtypeStruct((t, w), dt) for w, dt in row_outs]
    out_shape += [jax.ShapeDtypeStruct(s, F32) for s in red_outs]
    outs, ride_outs = _pallas(body, grid=(t // tr,), in_specs=in_specs, out_specs=out_specs, out_shape=out_shape,
                              args=[*rows, *consts, *stacks], semantics=("arbitrary",), rides=rides or (), name=name)
    return outs if rides is None else (outs, ride_outs)


MM_VMEM_BUDGET = 36 * 1024 * 1024


def _divisors(dim, mult, must_divide=0):
    out = [t for t in range(dim, 0, -mult) if t % mult == 0 and dim % t == 0 and must_divide % t == 0]
    return out or [dim]


def _mm_tiles(m, n, k, a_bytes, b_bytes, out_bytes, extra_bytes, ta, b_off_n, b_off_k, out_off, tm, tn):
    tms = _divisors(m, LANES if ta else SUBLANES, out_off)
    tm = next((t for t in tms if t <= tm), tms[-1])
    tns = [t for t in _divisors(n, LANES, b_off_n) if t <= tn] or [_divisors(n, LANES, b_off_n)[-1]]
    for tn_ in tns:
        for tk in _divisors(k, LANES, b_off_k):
            acc = 0 if tk == k else tm * tn_ * 4
            need = 2 * (tm * tk * a_bytes + tk * tn_ * b_bytes + tm * tn_ * (out_bytes + extra_bytes)) + acc
            if need <= MM_VMEM_BUDGET:
                return tm, tn_, tk
    return tm, tns[-1], _divisors(k, LANES, b_off_k)[-1]


def _mm(a, b, *, ta=False, tb=False, bias=None, add=None, out_dtype=F32, b_off=0, n=None, tm=1024, tn=1024, name=None,
        rides=None, into=None):
    k, m = a.shape if ta else a.shape[::-1]
    if tb:
        n = b.shape[0] if n is None else n
        assert b.shape[1] == k and b_off + n <= b.shape[0]
    else:
        n = b.shape[1]
        assert b_off + k <= b.shape[0]
    out_rows, out_off, out_buf = (m, 0, None) if into is None else into
    extra = 4 if add is not None else 0
    tm, tn, tk = _mm_tiles(m, n, k, a.dtype.itemsize, b.dtype.itemsize, jnp.dtype(out_dtype).itemsize, extra, ta,
                           b_off if tb else 0, 0 if tb else b_off, out_off, tm, tn)
    nk = k // tk
    off_n, off_k = (b_off // tn, 0) if tb else (0, b_off // tk)
    off_m = out_off // tm
    dims = (((0 if ta else 1,), (1 if tb else 0,)), ((), ()))

    def body(*refs):
        a_ref, b_ref = refs[0], refs[1]
        rest = list(refs[2:])
        bias_ref = rest.pop(0) if bias is not None else None
        add_ref = rest.pop(0) if add is not None else None
        o_ref = rest.pop(0)
        part = lax.dot_general(a_ref[...].astype(BF16), b_ref[...].astype(BF16), dims, preferred_element_type=F32)

        def finish(r):
            if bias_ref is not None:
                r = r + bias_ref[...]
            if add_ref is not None:
                r = r + add_ref[...]
            o_ref[...] = r.astype(o_ref.dtype)

        if nk == 1:
            finish(part)
        else:
            acc_ref = rest.pop(0)
            kk = pl.program_id(2)

            @pl.when(kk == 0)
            def _():
                acc_ref[...] = part

            @pl.when(kk > 0)
            def _():
                acc_ref[...] += part

            @pl.when(kk == nk - 1)
            def _():
                finish(acc_ref[...])

    a_spec = pl.BlockSpec((tk, tm), lambda i, j, q: (q, i)) if ta else pl.BlockSpec((tm, tk), lambda i, j, q: (i, q))
    if tb:
        b_spec = pl.BlockSpec((tn, tk), lambda i, j, q: (j + off_n, q))
    else:
        b_spec = pl.BlockSpec((tk, tn), lambda i, j, q: (q + off_k, j))
    in_specs, args = [a_spec, b_spec], [a, b]
    if bias is not None:
        in_specs.append(pl.BlockSpec((1, tn), lambda i, j, q: (0, j)))
        args.append(bias)
    if add is not None:
        in_specs.append(pl.BlockSpec((tm, tn), lambda i, j, q: (i, j)))
        args.append(add)
    aliases = {}
    if out_buf is not None:
        assert out_buf.shape == (out_rows, n) and out_buf.dtype == jnp.dtype(out_dtype)
        in_specs.append(_HBM)
        args.append(out_buf)
        aliases = {len(args) - 1: 0}

    def body_in_place(*refs):
        body(*refs[:len(args) - 1], *refs[len(args):])

    outs, ride_outs = _pallas(
        body if out_buf is None else body_in_place, grid=(m // tm, n // tn, nk), in_specs=in_specs,
        out_specs=[pl.BlockSpec((tm, tn), lambda i, j, q: (i + off_m, j))],
        out_shape=[jax.ShapeDtypeStruct((out_rows, n), out_dtype)], args=args,
        scratch=[] if nk == 1 else [pltpu.VMEM((tm, tn), F32)],
        semantics=("parallel", "parallel", "arbitrary"), rides=rides or (), aliases=aliases, name=name)
    return outs[0] if rides is None else (outs[0], ride_outs)


def _bdmm(a, bm, *, tb=False, scale=1.0, add=None, name=None):
    t = a.shape[0]
    nj, ka, kb = bm.shape
    kin, kout = (kb, ka) if tb else (ka, kb)
    tm = _pick(t, 512, SUBLANES)
    dims = (((1,), (1 if tb else 0,)), ((), ()))

    def body(*refs):
        a_ref, b_ref = refs[0], refs[1]
        add_ref = refs[2] if add is not None else None
        o_ref = refs[-1]
        r = lax.dot_general(a_ref[...].astype(BF16), b_ref[0].astype(BF16), dims, preferred_element_type=F32)
        if scale != 1.0:
            r = r * scale
        if add_ref is not None:
            r = r + add_ref[...]
        o_ref[...] = r

    in_specs = [pl.BlockSpec((tm, kin), lambda i, j: (i, j)), pl.BlockSpec((1, ka, kb), lambda i, j: (j, 0, 0))]
    args = [a, bm]
    if add is not None:
        in_specs.append(pl.BlockSpec((tm, kout), lambda i, j: (i, j)))
        args.append(add)
    return pl.pallas_call(
        body, grid=(t // tm, nj), in_specs=in_specs, out_specs=pl.BlockSpec((tm, kout), lambda i, j: (i, j)),
        out_shape=jax.ShapeDtypeStruct((t, nj * kout), F32),
        compiler_params=_cparams("parallel", "parallel"), name=name)(*args)


def _bdmm_tn(a, b, nj, *, scale=1.0, name=None):
    t = a.shape[0]
    ka, kb = a.shape[1] // nj, b.shape[1] // nj
    tm = _pick(t, 512, LANES)
    nt = t // tm

    def body(a_ref, b_ref, o_ref):
        @pl.when(pl.program_id(1) == 0)
        def _():
            o_ref[...] = jnp.zeros_like(o_ref)

        r = lax.dot_general(a_ref[...].astype(BF16), b_ref[...].astype(BF16), (((0,), (0,)), ((), ())),
                            preferred_element_type=F32)
        o_ref[0] += r if scale == 1.0 else r * scale

    return pl.pallas_call(
        body, grid=(nj, nt),
        in_specs=[pl.BlockSpec((tm, ka), lambda j, i: (i, j)), pl.BlockSpec((tm, kb), lambda j, i: (i, j))],
        out_specs=pl.BlockSpec((1, ka, kb), lambda j, i: (j, 0, 0)),
        out_shape=jax.ShapeDtypeStruct((nj, ka, kb), F32),
        compiler_params=_cparams("parallel", "arbitrary"), name=name)(a, b)


def _ln(x, g, b, eps=1e-5):
    mu = jnp.mean(x, axis=-1, keepdims=True)
    var = jnp.mean(jnp.square(x - mu), axis=-1, keepdims=True)
    return (x - mu) * lax.rsqrt(var + eps) * g + b


def _post_norm(x, f, g, b):
    return _ln(ALPHA * x + f, g, b)


def _post_norm_bwd(x, f, dy, g, b):
    _, vjp = jax.vjp(_post_norm, x, f, g, b)
    return vjp(dy)


def _merge3(g0, g1, g2, pa, pb, pc):
    return jax.nn.sigmoid(g0) * pa + jax.nn.sigmoid(g1) * pb + jax.nn.sigmoid(g2) * pc


def _merge(gl, pa, pb, pc):
    d = pa.shape[1]
    return _merge3(gl[:, :d], gl[:, d:2 * d], gl[:, 2 * d:], pa, pb, pc)


def _merge_bwd(gl, pa, pb, pc, dm):
    d = pa.shape[1]
    _, vjp = jax.vjp(_merge3, gl[:, :d], gl[:, d:2 * d], gl[:, 2 * d:], pa, pb, pc)
    d0, d1, d2, dpa, dpb, dpc = vjp(dm)
    dgl = jnp.concatenate([d0, d1, d2], axis=1)
    return dgl, dpa, dpb, dpc, jnp.sum(dgl, axis=0, keepdims=True)


def _swiglu2(gate, up):
    return jax.nn.silu(gate) * up


def _swiglu(gu):
    h = gu.shape[1] // 2
    return _swiglu2(gu[:, :h], gu[:, h:])


def _swiglu_bwd(gu, dact):
    h = gu.shape[1] // 2
    _, vjp = jax.vjp(_swiglu2, gu[:, :h], gu[:, h:])
    dg, du = vjp(dact)
    return jnp.concatenate([dg, du], axis=1)


def _glu(ycp, lin):
    return ycp * jax.nn.sigmoid(lin)


def _glu_bwd(ycp, lin, dyc):
    _, vjp = jax.vjp(_glu, ycp, lin)
    dycp, dlin = vjp(dyc)
    return dycp, dlin, jnp.sum(dlin, axis=0, keepdims=True)


def _s5_out(yre, yim, uc, dskip):
    ys = yre + yim + dskip * uc
    return ys, jax.nn.gelu(ys)


def _s5_out_bwd(ys, uc, dycp, dskip):
    _, vjp = jax.vjp(jax.nn.gelu, ys)
    dys = vjp(dycp)[0]
    return dys, dys * dskip, jnp.sum(dys * uc, axis=0, keepdims=True)


def _combine(o0, l0, o1, l1, o2, l2):
    m = jnp.maximum(jnp.maximum(l0, l1), l2)
    e0, e1, e2 = jnp.exp(l0 - m), jnp.exp(l1 - m), jnp.exp(l2 - m)
    s = e0 + e1 + e2
    return (e0 / s) * o0 + (e1 / s) * o1 + (e2 / s) * o2


def _combine_bwd(o0, l0, o1, l1, o2, l2, ya, dya, head_ones):
    m = jnp.maximum(jnp.maximum(l0, l1), l2)
    e0, e1, e2 = jnp.exp(l0 - m), jnp.exp(l1 - m), jnp.exp(l2 - m)
    s = e0 + e1 + e2
    dot_ya = jnp.dot(dya * ya, head_ones, precision=lax.Precision.HIGHEST, preferred_element_type=F32)
    w0, w1, w2 = e0 / s, e1 / s, e2 / s
    return w0 * dya, w1 * dya, w2 * dya, -w0 * dot_ya, -w1 * dot_ya, -w2 * dot_ya


def _loss_fn(y, tgt):
    err = y - tgt
    part = jnp.sum(jnp.sum(jnp.square(err), axis=1, keepdims=True), axis=0, keepdims=True) * (0.5 / y.shape[1])
    return err * (1.0 / y.shape[1]), jnp.broadcast_to(part, (1, LANES))


def _adamw(w, g, m, v):
    m = ADAM_B1 * m + (1.0 - ADAM_B1) * g
    v = ADAM_B2 * v + (1.0 - ADAM_B2) * jnp.square(g)
    m_hat = m / (1.0 - ADAM_B1 ** ADAM_STEP)
    v_hat = v / (1.0 - ADAM_B2 ** ADAM_STEP)
    delta = -ADAM_LR * (m_hat / (jnp.sqrt(v_hat) + ADAM_EPS) + ADAM_WD * w)
    return delta, m, v


def _sum_parts_fn(parts):
    g = parts[0].astype(F32)
    for j in range(1, parts.shape[0]):
        g = g + parts[j].astype(F32)
    return g


def _t5_bucket(dist):
    max_exact = N_REL_BUCKETS // 2
    d = np.maximum(dist, 1).astype(np.float32)
    scale = (N_REL_BUCKETS - max_exact) / math.log(REL_MAX_DIST / max_exact)
    large = max_exact + (np.log(d / max_exact) * scale).astype(np.int32)
    large = np.minimum(large, N_REL_BUCKETS - 1)
    return np.where(dist < max_exact, dist, large).astype(np.int32)


def _bucket_table(dilation):
    i = np.arange(ATT_BLOCK)[:, None]
    kk = np.arange(2 * ATT_BLOCK)[None, :]
    steps = ATT_BLOCK + i - kk
    return _t5_bucket(np.maximum(steps, 0) * dilation)


def _bias_fwd(rel_bias, buckets, g):
    def body(rel_ref, bk_ref, o_ref):
        bk = bk_ref[...]
        for h in range(HEADS_PER_GROUP):
            acc = jnp.zeros(bk.shape, F32)
            for b in range(N_REL_BUCKETS):
                acc = jnp.where(bk == b, rel_ref[b, g * HEADS_PER_GROUP + h], acc)
            o_ref[h] = acc

    return pl.pallas_call(
        body, in_specs=[pl.BlockSpec(memory_space=pltpu.SMEM), pl.BlockSpec(memory_space=pltpu.VMEM)],
        out_specs=pl.BlockSpec(memory_space=pltpu.VMEM),
        out_shape=jax.ShapeDtypeStruct((HEADS_PER_GROUP, ATT_BLOCK, 2 * ATT_BLOCK), F32),
        name=f"rel_bias_fwd{g}")(rel_bias, buckets)


def _bias_bwd(dbias, buckets, g):
    def body(db_ref, bk_ref, o_ref):
        bk = bk_ref[...]
        row = lax.broadcasted_iota(jnp.int32, (N_REL_BUCKETS, LANES), 0)
        col = lax.broadcasted_iota(jnp.int32, (N_REL_BUCKETS, LANES), 1)
        acc = jnp.zeros((N_REL_BUCKETS, LANES), F32)
        for h in range(HEADS_PER_GROUP):
            d = db_ref[h]
            for b in range(N_REL_BUCKETS):
                s = jnp.sum(jnp.sum(jnp.where(bk == b, d, 0.0), axis=1, keepdims=True), axis=0, keepdims=True)
                acc = acc + jnp.where((row == b) & (col == g * HEADS_PER_GROUP + h), s, 0.0)
        o_ref[...] = acc

    return pl.pallas_call(
        body, in_specs=[pl.BlockSpec(memory_space=pltpu.VMEM), pl.BlockSpec(memory_space=pltpu.VMEM)],
        out_specs=pl.BlockSpec(memory_space=pltpu.VMEM),
        out_shape=jax.ShapeDtypeStruct((N_REL_BUCKETS, LANES), F32), name=f"rel_bias_bwd{g}")(dbias, buckets)


_NT = (((1,), (1,)), ((), ()))
_TN = (((0,), (0,)), ((), ()))
_QKV_BLOCKS = 3 * QKV_WIDTH // WIDTH_A


def _band_mask(n_is_first):
    i = lax.broadcasted_iota(jnp.int32, (ATT_BLOCK, 2 * ATT_BLOCK), 0)
    kk = lax.broadcasted_iota(jnp.int32, (ATT_BLOCK, 2 * ATT_BLOCK), 1)
    return (kk >= i) & (kk <= i + ATT_STEPS) & ((kk >= ATT_BLOCK) | jnp.logical_not(n_is_first))


def _head(ref, h):
    return ref[:, h * HEAD_DIM:(h + 1) * HEAD_DIM]


def _attn_specs(g, d):
    blk = (ATT_BLOCK, WIDTH_A)
    q = pl.BlockSpec(blk, lambda c, n: (n, c * _QKV_BLOCKS + g))
    kp = pl.BlockSpec(blk, lambda c, n: (jnp.maximum(n - 1, 0), c * _QKV_BLOCKS + 3 + g))
    kc = pl.BlockSpec(blk, lambda c, n: (n, c * _QKV_BLOCKS + 3 + g))
    vp = pl.BlockSpec(blk, lambda c, n: (jnp.maximum(n - 1, 0), c * _QKV_BLOCKS + 6 + g))
    vc = pl.BlockSpec(blk, lambda c, n: (n, c * _QKV_BLOCKS + 6 + g))
    return [q, kp, kc, vp, vc]


def _attn_fwd(qkv, bias, g, d, comm):
    t = qkv.shape[0]
    lq = t // d
    nb = lq // ATT_BLOCK
    scale = HEAD_DIM ** -0.5

    def body(q_ref, kp_ref, kc_ref, vp_ref, vc_ref, b_ref, o_ref, l_ref):
        mask = _band_mask(pl.program_id(1) == 0)
        for h in range(HEADS_PER_GROUP):
            qh = _head(q_ref, h).astype(BF16)
            kh = jnp.concatenate([_head(kp_ref, h), _head(kc_ref, h)], axis=0).astype(BF16)
            vh = jnp.concatenate([_head(vp_ref, h), _head(vc_ref, h)], axis=0).astype(BF16)
            s = lax.dot_general(qh, kh, _NT, preferred_element_type=F32) * scale + b_ref[h]
            s = jnp.where(mask, s, NEG_INF)
            m = jnp.max(s, axis=1, keepdims=True)
            p = jnp.exp(s - m)
            den = jnp.sum(p, axis=1, keepdims=True)
            o = jnp.dot(p.astype(BF16), vh, preferred_element_type=F32) / den
            o_ref[:, h * HEAD_DIM:(h + 1) * HEAD_DIM] = o
            l_ref[:, h * HEAD_DIM:(h + 1) * HEAD_DIM] = jnp.broadcast_to(m + jnp.log(den), (ATT_BLOCK, HEAD_DIM))

    out_spec = pl.BlockSpec((ATT_BLOCK, WIDTH_A), lambda c, n: (n, c))
    (o, lse), ride_outs = _pallas(
        body, grid=(d, nb),
        in_specs=_attn_specs(g, d) + [pl.BlockSpec(bias.shape, _zero_map(3))],
        out_specs=[out_spec, out_spec],
        out_shape=[jax.ShapeDtypeStruct((lq, d * WIDTH_A), F32)] * 2,
        args=[*([qkv.reshape(lq, d * 3 * QKV_WIDTH)] * 5), bias],
        semantics=("parallel", "parallel"), rides=[comm.ride(f"attn_fwd{g}")], name=f"attn_fwd{g}")
    comm.took(ride_outs)
    return o.reshape(t, WIDTH_A), lse.reshape(t, WIDTH_A)


def _attn_dq(qkv, bias, do, lse, corr, g, d):
    t = qkv.shape[0]
    lq = t // d
    nb = lq // ATT_BLOCK
    scale = HEAD_DIM ** -0.5

    def body(q_ref, kp_ref, kc_ref, vp_ref, vc_ref, b_ref, do_ref, l_ref, c_ref, dq_ref):
        mask = _band_mask(pl.program_id(1) == 0)
        for h in range(HEADS_PER_GROUP):
            qh = _head(q_ref, h).astype(BF16)
            kh = jnp.concatenate([_head(kp_ref, h), _head(kc_ref, h)], axis=0).astype(BF16)
            vh = jnp.concatenate([_head(vp_ref, h), _head(vc_ref, h)], axis=0).astype(BF16)
            s = lax.dot_general(qh, kh, _NT, preferred_element_type=F32) * scale + b_ref[h]
            s = jnp.where(mask, s, NEG_INF)
            p = jnp.exp(s - l_ref[:, h * HEAD_DIM:h * HEAD_DIM + 1])
            dp = lax.dot_general(_head(do_ref, h).astype(BF16), vh, _NT, preferred_element_type=F32)
            ds = p * (dp + c_ref[:, h * HEAD_DIM:h * HEAD_DIM + 1])
            dq_ref[:, h * HEAD_DIM:(h + 1) * HEAD_DIM] = jnp.dot(ds.astype(BF16), kh, preferred_element_type=F32) * scale

    row_spec = pl.BlockSpec((ATT_BLOCK, WIDTH_A), lambda c, n: (n, c))
    view = lambda a: a.reshape(lq, d * WIDTH_A)
    dq = pl.pallas_call(
        body, grid=(d, nb),
        in_specs=_attn_specs(g, d) + [pl.BlockSpec(bias.shape, _zero_map(3)), row_spec, row_spec, row_spec],
        out_specs=row_spec, out_shape=jax.ShapeDtypeStruct((lq, d * WIDTH_A), F32),
        compiler_params=_cparams("parallel", "parallel"), name=f"attn_dq{g}",
    )(*([qkv.reshape(lq, d * 3 * QKV_WIDTH)] * 5), bias, view(do), view(lse), view(corr))
    return dq.reshape(t, WIDTH_A)


def _attn_dkv(qkv, bias, do, lse, corr, g, d):
    t = qkv.shape[0]
    lq = t // d
    nb = lq // ATT_BLOCK
    scale = HEAD_DIM ** -0.5

    def body(k_ref, v_ref, q0_ref, q1_ref, do0_ref, do1_ref, l0_ref, l1_ref, c0_ref, c1_ref, b_ref,
             dk_ref, dv_ref, db_ref):
        c, j = pl.program_id(0), pl.program_id(1)

        @pl.when((c == 0) & (j == 0))
        def _():
            db_ref[...] = jnp.zeros_like(db_ref)

        i = lax.broadcasted_iota(jnp.int32, (ATT_BLOCK, ATT_BLOCK), 0)
        kk = lax.broadcasted_iota(jnp.int32, (ATT_BLOCK, ATT_BLOCK), 1)
        mask0 = kk <= i
        mask1 = (kk >= i) & (j + 1 < nb)
        for h in range(HEADS_PER_GROUP):
            kh = _head(k_ref, h).astype(BF16)
            vh = _head(v_ref, h).astype(BF16)
            dk = jnp.zeros((ATT_BLOCK, HEAD_DIM), F32)
            dv = jnp.zeros((ATT_BLOCK, HEAD_DIM), F32)
            parts = ((q0_ref, do0_ref, l0_ref, c0_ref, mask0, ATT_BLOCK), (q1_ref, do1_ref, l1_ref, c1_ref, mask1, 0))
            for q_ref, do_ref, l_ref, c_ref, mask, off in parts:
                qh = _head(q_ref, h).astype(BF16)
                doh = _head(do_ref, h).astype(BF16)
                s = lax.dot_general(qh, kh, _NT, preferred_element_type=F32) * scale + b_ref[h, :, off:off + ATT_BLOCK]
                s = jnp.where(mask, s, NEG_INF)
                p = jnp.exp(s - l_ref[:, h * HEAD_DIM:h * HEAD_DIM + 1])
                dp = lax.dot_general(doh, vh, _NT, preferred_element_type=F32)
                ds = p * (dp + c_ref[:, h * HEAD_DIM:h * HEAD_DIM + 1])
                dv = dv + lax.dot_general(p.astype(BF16), doh, _TN, preferred_element_type=F32)
                dk = dk + lax.dot_general(ds.astype(BF16), qh, _TN, preferred_element_type=F32)
                db_ref[h, :, off:off + ATT_BLOCK] += ds
            dk_ref[:, h * HEAD_DIM:(h + 1) * HEAD_DIM] = dk * scale
            dv_ref[:, h * HEAD_DIM:(h + 1) * HEAD_DIM] = dv

    blk = (ATT_BLOCK, WIDTH_A)
    nxt = lambda n: jnp.minimum(n + 1, nb - 1)
    k_spec = pl.BlockSpec(blk, lambda c, n: (n, c * _QKV_BLOCKS + 3 + g))
    v_spec = pl.BlockSpec(blk, lambda c, n: (n, c * _QKV_BLOCKS + 6 + g))
    q0_spec = pl.BlockSpec(blk, lambda c, n: (n, c * _QKV_BLOCKS + g))
    q1_spec = pl.BlockSpec(blk, lambda c, n: (nxt(n), c * _QKV_BLOCKS + g))
    r0 = pl.BlockSpec(blk, lambda c, n: (n, c))
    r1 = pl.BlockSpec(blk, lambda c, n: (nxt(n), c))
    view = lambda a: a.reshape(lq, d * WIDTH_A)
    qv = qkv.reshape(lq, d * 3 * QKV_WIDTH)
    dk, dv, dbias = pl.pallas_call(
        body, grid=(d, nb),
        in_specs=[k_spec, v_spec, q0_spec, q1_spec, r0, r1, r0, r1, r0, r1, pl.BlockSpec(bias.shape, _zero_map(3))],
        out_specs=[r0, r0, pl.BlockSpec(bias.shape, _zero_map(3))],
        out_shape=[jax.ShapeDtypeStruct((lq, d * WIDTH_A), F32)] * 2 + [jax.ShapeDtypeStruct(bias.shape, F32)],
        compiler_params=_cparams("arbitrary", "arbitrary"), name=f"attn_dkv{g}",
    )(qv, qv, qv, qv, view(do), view(do), view(lse), view(lse), view(corr), view(corr), bias)
    return dk.reshape(t, WIDTH_A), dv.reshape(t, WIDTH_A), dbias


def _tril_mask():
    r = lax.broadcasted_iota(jnp.int32, (CHUNK, CHUNK), 0)
    c = lax.broadcasted_iota(jnp.int32, (CHUNK, CHUNK), 1)
    return c <= r


def _gmlp_fwd(zb, ln_g, ln_b, w_s, b_s_t):
    t = zb.shape[0]
    tr = _pick(t, 2 * CHUNK, CHUNK)

    def body(z_ref, g_ref, b_ref, ws_ref, bs_ref, o_ref):
        tri = _tril_mask()
        z = jax.nn.gelu(z_ref[...])
        u = z[:, :WIDTH_B]
        vn = _ln(z[:, WIDTH_B:], g_ref[...], b_ref[...])
        for ch in range(tr // CHUNK):
            rows = slice(ch * CHUNK, (ch + 1) * CHUNK)
            for gi in range(N_GROUPS_B):
                cols = slice(gi * CHUNK, (gi + 1) * CHUNK)
                w = jnp.where(tri, ws_ref[gi], 0.0).astype(BF16)
                mixed = jnp.dot(w, vn[rows, cols].astype(BF16), preferred_element_type=F32) + bs_ref[:, gi:gi + 1]
                o_ref[rows, cols] = (u[rows, cols] * mixed).astype(o_ref.dtype)

    return pl.pallas_call(
        body, grid=(t // tr,),
        in_specs=[pl.BlockSpec((tr, 2 * WIDTH_B), lambda i: (i, 0)), pl.BlockSpec(ln_g.shape, _zero_map(2)),
                  pl.BlockSpec(ln_b.shape, _zero_map(2)), pl.BlockSpec(w_s.shape, _zero_map(3)),
                  pl.BlockSpec(b_s_t.shape, _zero_map(2))],
        out_specs=pl.BlockSpec((tr, WIDTH_B), lambda i: (i, 0)),
        out_shape=jax.ShapeDtypeStruct((t, WIDTH_B), BF16),
        compiler_params=_cparams("parallel"), name="gmlp_fwd")(zb, ln_g, ln_b, w_s, b_s_t)


def _gmlp_bwd(zb, dyb, ln_g, ln_b, w_s, b_s_t, group_sel):
    t = zb.shape[0]
    tr = _pick(t, 2 * CHUNK, CHUNK)

    def body(z_ref, dy_ref, g_ref, b_ref, ws_ref, bs_ref, sel_ref, dz_ref, dzs_ref, dg_ref, db_ref, dws_ref, dbs_ref,
             du_s, dvn_s, dm_s):
        @pl.when(pl.program_id(0) == 0)
        def _():
            dzs_ref[...] = jnp.zeros_like(dzs_ref)
            dg_ref[...] = jnp.zeros_like(dg_ref)
            db_ref[...] = jnp.zeros_like(db_ref)
            dws_ref[...] = jnp.zeros_like(dws_ref)
            dbs_ref[...] = jnp.zeros_like(dbs_ref)

        tri = _tril_mask()
        z, gelu_vjp = jax.vjp(jax.nn.gelu, z_ref[...])
        u = z[:, :WIDTH_B]
        vn, ln_vjp = jax.vjp(_ln, z[:, WIDTH_B:], g_ref[...], b_ref[...])
        dy = dy_ref[...]
        for ch in range(tr // CHUNK):
            rows = slice(ch * CHUNK, (ch + 1) * CHUNK)
            for gi in range(N_GROUPS_B):
                cols = slice(gi * CHUNK, (gi + 1) * CHUNK)
                w = jnp.where(tri, ws_ref[gi], 0.0).astype(BF16)
                vg = vn[rows, cols].astype(BF16)
                mixed = jnp.dot(w, vg, preferred_element_type=F32) + bs_ref[:, gi:gi + 1]
                dyg = dy[rows, cols]
                dm = dyg * u[rows, cols]
                dmb = dm.astype(BF16)
                du_s[rows, cols] = dyg * mixed
                dm_s[rows, cols] = dm
                dvn_s[rows, cols] = lax.dot_general(w, dmb, _TN, preferred_element_type=F32)
                dws_ref[gi] += jnp.where(tri, lax.dot_general(dmb, vg, _NT, preferred_element_type=F32), 0.0)
            dbs_ref[...] += jnp.dot(dm_s[rows, :], sel_ref[...], precision=lax.Precision.HIGHEST,
                                    preferred_element_type=F32)
        dv, dg, db = ln_vjp(dvn_s[...])
        dg_ref[...] += dg
        db_ref[...] += db
        dz = gelu_vjp(jnp.concatenate([du_s[...], dv], axis=1))[0]
        dz_ref[...] = dz.astype(dz_ref.dtype)
        dzs_ref[...] += jnp.sum(dz, axis=0, keepdims=True)

    full = lambda a: pl.BlockSpec(a.shape, _zero_map(a.ndim))
    return pl.pallas_call(
        body, grid=(t // tr,),
        in_specs=[pl.BlockSpec((tr, 2 * WIDTH_B), lambda i: (i, 0)), pl.BlockSpec((tr, WIDTH_B), lambda i: (i, 0)),
                  full(ln_g), full(ln_b), full(w_s), full(b_s_t), full(group_sel)],
        out_specs=[pl.BlockSpec((tr, 2 * WIDTH_B), lambda i: (i, 0)), pl.BlockSpec((1, 2 * WIDTH_B), _zero_map(2)),
                   full(ln_g), full(ln_b), full(w_s), pl.BlockSpec((CHUNK, LANES), _zero_map(2))],
        out_shape=[jax.ShapeDtypeStruct((t, 2 * WIDTH_B), BF16), jax.ShapeDtypeStruct((1, 2 * WIDTH_B), F32),
                   jax.ShapeDtypeStruct(ln_g.shape, F32),
                   jax.ShapeDtypeStruct(ln_b.shape, F32), jax.ShapeDtypeStruct(w_s.shape, F32),
                   jax.ShapeDtypeStruct((CHUNK, LANES), F32)],
        scratch_shapes=[pltpu.VMEM((tr, WIDTH_B), F32)] * 3,
        compiler_params=_cparams("arbitrary"), name="gmlp_bwd")(zb, dyb, ln_g, ln_b, w_s, b_s_t, group_sel)


def _s5_disc(lr, li, ldt, br_t, bi_t):
    dt = jnp.exp(ldt)
    mag = jnp.exp(lr * dt)
    ab_re = mag * jnp.cos(li * dt)
    ab_im = mag * jnp.sin(li * dt)
    nrm = lr * lr + li * li
    cr = ((ab_re - 1.0) * lr + ab_im * li) / nrm
    ci = (ab_im * lr - (ab_re - 1.0) * li) / nrm
    return ab_re, ab_im, cr * br_t - ci * bi_t, cr * bi_t + ci * br_t


def _vmem_call(fn, args, out_shape, name):
    def body(*refs):
        res = fn(*[r[...] for r in refs[:len(args)]])
        for o, v in zip(refs[len(args):], res):
            o[...] = v

    vm = pl.BlockSpec(memory_space=pltpu.VMEM)
    return pl.pallas_call(body, in_specs=[vm] * len(args), out_specs=[vm] * len(out_shape),
                          out_shape=out_shape, name=name)(*args)


def _s5_disc_fwd(lr, li, ldt, br_t, bi_t):
    s1 = jax.ShapeDtypeStruct(lr.shape, F32)
    s2 = jax.ShapeDtypeStruct(br_t.shape, F32)
    return _vmem_call(_s5_disc, [lr, li, ldt, br_t, bi_t], [s1, s1, s2, s2], "s5_disc_fwd")


def _s5_disc_bwd(lr, li, ldt, br_t, bi_t, cts):
    def fn(lr, li, ldt, br_t, bi_t, d0, d1, d2, d3):
        _, vjp = jax.vjp(_s5_disc, lr, li, ldt, br_t, bi_t)
        return vjp((d0, d1, d2, d3))

    shp = [jax.ShapeDtypeStruct(a.shape, F32) for a in (lr, li, ldt, br_t, bi_t)]
    return _vmem_call(fn, [lr, li, ldt, br_t, bi_t, *cts], shp, "s5_disc_bwd")


_SCAN_ROWS = SSM_COLS // LANES
_SCAN_CHUNK = 256


def _scan_fwd(bre, bim, are, aim):
    t = bre.shape[0]
    tc = _pick(t, _SCAN_CHUNK, SUBLANES)

    def body(br_ref, bi_ref, ar_ref, ai_ref, xr_ref, xi_ref, st_ref):
        @pl.when(pl.program_id(0) == 0)
        def _():
            st_ref[...] = jnp.zeros_like(st_ref)

        ar, ai = ar_ref[...], ai_ref[...]

        def step(i, carry):
            xr, xi = carry
            nr = ar * xr - ai * xi + br_ref[i]
            ni = ar * xi + ai * xr + bi_ref[i]
            xr_ref[i] = nr
            xi_ref[i] = ni
            return nr, ni

        xr, xi = lax.fori_loop(0, tc, step, (st_ref[0], st_ref[1]), unroll=8)
        st_ref[0] = xr
        st_ref[1] = xi

    blk = pl.BlockSpec((tc, _SCAN_ROWS, LANES), lambda i: (i, 0, 0))
    par = pl.BlockSpec((_SCAN_ROWS, LANES), _zero_map(2))
    shp = jax.ShapeDtypeStruct(bre.shape, F32)
    return pl.pallas_call(
        body, grid=(t // tc,), in_specs=[blk, blk, par, par], out_specs=[blk, blk], out_shape=[shp, shp],
        scratch_shapes=[pltpu.VMEM((2, _SCAN_ROWS, LANES), F32)],
        compiler_params=_cparams("arbitrary"), name="s5_scan_fwd")(bre, bim, are, aim)


def _scan_bwd(dxr, dxi, xr, xi, are, aim):
    t = dxr.shape[0]
    tc = _pick(t, _SCAN_CHUNK, SUBLANES)
    nc = t // tc

    def body(dr_ref, di_ref, xr_ref, xi_ref, pr_ref, pi_ref, ar_ref, ai_ref, gr_ref, gi_ref, dar_ref, dai_ref, st_ref):
        step_id = pl.program_id(0)

        @pl.when(step_id == 0)
        def _():
            st_ref[...] = jnp.zeros_like(st_ref)
            dar_ref[...] = jnp.zeros_like(dar_ref)
            dai_ref[...] = jnp.zeros_like(dai_ref)

        ar, ai = ar_ref[...], ai_ref[...]

        def update(i, carry, pxr, pxi):
            gr, gi, dar, dai = carry
            ngr = dr_ref[i] + ar * gr + ai * gi
            ngi = di_ref[i] - ai * gr + ar * gi
            gr_ref[i] = ngr
            gi_ref[i] = ngi
            return ngr, ngi, dar + ngr * pxr + ngi * pxi, dai - ngr * pxi + ngi * pxr

        def step(s, carry):
            i = tc - 1 - s
            return update(i, carry, xr_ref[i - 1], xi_ref[i - 1])

        zero = jnp.zeros((_SCAN_ROWS, LANES), F32)
        carry = lax.fori_loop(0, tc - 1, step, (st_ref[0], st_ref[1], zero, zero), unroll=8)
        has_prev = (step_id < nc - 1).astype(F32)
        gr, gi, dar, dai = update(0, carry, pr_ref[0] * has_prev, pi_ref[0] * has_prev)
        st_ref[0] = gr
        st_ref[1] = gi
        dar_ref[...] += dar
        dai_ref[...] += dai

    blk = pl.BlockSpec((tc, _SCAN_ROWS, LANES), lambda i: (nc - 1 - i, 0, 0))
    prev = pl.BlockSpec((1, _SCAN_ROWS, LANES), lambda i: (jnp.maximum((nc - 1 - i) * tc - 1, 0), 0, 0))
    par = pl.BlockSpec((_SCAN_ROWS, LANES), _zero_map(2))
    shp = jax.ShapeDtypeStruct(dxr.shape, F32)
    psh = jax.ShapeDtypeStruct((_SCAN_ROWS, LANES), F32)
    return pl.pallas_call(
        body, grid=(nc,), in_specs=[blk, blk, blk, blk, prev, prev, par, par],
        out_specs=[blk, blk, par, par], out_shape=[shp, shp, psh, psh],
        scratch_shapes=[pltpu.VMEM((2, _SCAN_ROWS, LANES), F32)],
        compiler_params=_cparams("arbitrary"), name="s5_scan_bwd")(dxr, dxi, xr, xi, xr, xi, are, aim)


def _block_diag(m):
    _, a, b = m.shape
    m4 = m.reshape(N_SSM_BLOCKS, SSM_PACK, a, b)
    eye = jnp.eye(SSM_PACK, dtype=m.dtype)
    return (m4[:, :, :, None, :] * eye[None, :, None, :, None]).reshape(N_SSM_BLOCKS, SSM_PACK * a, SSM_PACK * b)


def _block_diag_extract(m, a, b):
    blocks = [m[:, i * a:(i + 1) * a, i * b:(i + 1) * b] for i in range(SSM_PACK)]
    return jnp.stack(blocks, axis=1).reshape(N_GROUPS_C, a, b)


def _exchange(src, *, gather, name):
    shape = src.shape if gather else src.shape[1:]

    def body(src_ref, out_ref, send_sems, recv_sems, local_sem):
        x, y, c = lax.axis_index("x"), lax.axis_index("y"), lax.axis_index("c")
        me = 4 * x + 2 * y + c
        copies = []
        for r in range(1, N_DEV):
            px = 1 - x if r & 4 else x
            py = 1 - y if r & 2 else y
            pc = 1 - c if r & 1 else c
            piece = src_ref if gather else src_ref.at[4 * px + 2 * py + pc]
            cp = pltpu.make_async_remote_copy(
                src_ref=piece, dst_ref=out_ref.at[me], send_sem=send_sems.at[r - 1], recv_sem=recv_sems.at[r - 1],
                device_id=(px, py, pc), device_id_type=pl.DeviceIdType.MESH)
            cp.start()
            copies.append(cp)
        mine = pltpu.make_async_copy(src_ref if gather else src_ref.at[me], out_ref.at[me], local_sem)
        mine.start()
        for cp in copies:
            cp.wait()
        mine.wait()

    hbm = pl.BlockSpec(memory_space=pl.ANY)
    return pl.pallas_call(
        body, in_specs=[hbm], out_specs=hbm, out_shape=jax.ShapeDtypeStruct((N_DEV,) + tuple(shape), src.dtype),
        scratch_shapes=[pltpu.SemaphoreType.DMA((N_DEV - 1,)), pltpu.SemaphoreType.DMA((N_DEV - 1,)),
                        pltpu.SemaphoreType.DMA(())],
        name=name)(src)


def _mesh_place():
    x, y, c = lax.axis_index("x"), lax.axis_index("y"), lax.axis_index("c")
    other_chips = [(1 - x, y), (x, 1 - y), (1 - x, 1 - y)]
    return x, y, c, other_chips


def _gather_layer(srcs, layer, name):
    n = len(srcs)

    def body(*refs):
        src = [r.at[layer] for r in refs[:n]]
        out = refs[n:2 * n]
        send_sems, recv_sems, local_sems = refs[2 * n:]
        x, y, c, chips = _mesh_place()
        me, sibling = (x, y, c), (x, y, 1 - c)

        def copy(t, k, block, to, from_src=False):
            slot = 4 * block[0] + 2 * block[1] + block[2]
            return pltpu.make_async_remote_copy(
                src_ref=src[t] if from_src else out[t].at[slot], dst_ref=out[t].at[slot],
                send_sem=send_sems.at[t, k], recv_sem=recv_sems.at[t, k], device_id=to, device_id_type=_MESH_ID)

        mine = [pltpu.make_async_copy(src[t], out[t].at[4 * x + 2 * y + c], local_sems.at[t]) for t in range(n)]
        for cp in mine:
            cp.start()
        first = []
        for t in range(n):
            first.append(copy(t, 0, me, sibling, True))
            first += [copy(t, 1 + j, me, (*chip, c), True) for j, chip in enumerate(chips)]
        for cp in first:
            cp.start()
        passed = []
        for j, chip in enumerate(chips):
            for t in range(n):
                copy(t, 1 + j, (*chip, c), me).wait_recv()
                fwd = copy(t, 4 + j, (*chip, c), sibling)
                fwd.start()
                passed.append(fwd)
        for t in range(n):
            copy(t, 0, sibling, me).wait_recv()
            for j, chip in enumerate(chips):
                copy(t, 4 + j, (*chip, 1 - c), me).wait_recv()
        for cp in first + passed:
            cp.wait_send()
        for cp in mine:
            cp.wait()

    return pl.pallas_call(
        body, in_specs=[_HBM] * n, out_specs=[_HBM] * n,
        out_shape=[jax.ShapeDtypeStruct((N_DEV,) + s.shape[1:], s.dtype) for s in srcs],
        scratch_shapes=[pltpu.SemaphoreType.DMA((n, N_DEV - 1)), pltpu.SemaphoreType.DMA((n, N_DEV - 1)),
                        pltpu.SemaphoreType.DMA((n,))],
        name=name)(*srcs)


def _scatter_pair(srcs, name):
    n = len(srcs)

    def body(*refs):
        src, out = refs[:n], refs[n:2 * n]
        send_sems, recv_sems = refs[2 * n:]
        x, y, c, _ = _mesh_place()
        copies = [pltpu.make_async_remote_copy(
            src_ref=src[t].at[:, 1 - c], dst_ref=out[t], send_sem=send_sems.at[t], recv_sem=recv_sems.at[t],
            device_id=(x, y, 1 - c), device_id_type=_MESH_ID) for t in range(n)]
        for cp in copies:
            cp.start()
        for cp in copies:
            cp.wait()

    return pl.pallas_call(
        body, in_specs=[_HBM] * n, out_specs=[_HBM] * n,
        out_shape=[jax.ShapeDtypeStruct((s.shape[0],) + s.shape[2:], s.dtype) for s in srcs],
        scratch_shapes=[pltpu.SemaphoreType.DMA((n,)), pltpu.SemaphoreType.DMA((n,))], name=name)(*srcs)


def _pair_add(src, recv, name):
    nchip, _, r, cdim = src.shape
    tr = _pick(r, 256, 2 * SUBLANES)
    core = lax.axis_index("c").astype(jnp.int32).reshape(1)

    def body(core_ref, s_ref, r_ref, o_ref):
        o_ref[...] = (s_ref[...].astype(F32) + r_ref[...].astype(F32)).astype(o_ref.dtype)

    grid_spec = pltpu.PrefetchScalarGridSpec(
        num_scalar_prefetch=1, grid=(nchip, r // tr),
        in_specs=[pl.BlockSpec((None, None, tr, cdim), lambda k, i, core_ref: (k, core_ref[0], i, 0)),
                  pl.BlockSpec((None, tr, cdim), lambda k, i, core_ref: (k, i, 0))],
        out_specs=pl.BlockSpec((None, tr, cdim), lambda k, i, core_ref: (k, i, 0)))
    return pl.pallas_call(body, grid_spec=grid_spec, out_shape=jax.ShapeDtypeStruct(recv.shape, recv.dtype),
                          compiler_params=_cparams("parallel", "parallel"), name=name)(core, src, recv)


def _scatter_chips(srcs, name):
    n = len(srcs)

    def body(*refs):
        src, out = refs[:n], refs[n:2 * n]
        send_sems, recv_sems, local_sems = refs[2 * n:]
        x, y, c, chips = _mesh_place()
        my_chip = 2 * x + y
        mine = [pltpu.make_async_copy(src[t].at[my_chip], out[t].at[my_chip], local_sems.at[t]) for t in range(n)]
        copies = [pltpu.make_async_remote_copy(
            src_ref=src[t].at[2 * chip[0] + chip[1]], dst_ref=out[t].at[my_chip],
            send_sem=send_sems.at[t, j], recv_sem=recv_sems.at[t, j], device_id=(*chip, c), device_id_type=_MESH_ID)
            for t in range(n) for j, chip in enumerate(chips)]
        for cp in mine + copies:
            cp.start()
        for cp in copies + mine:
            cp.wait()

    return pl.pallas_call(
        body, in_specs=[_HBM] * n, out_specs=[_HBM] * n,
        out_shape=[jax.ShapeDtypeStruct(s.shape, s.dtype) for s in srcs],
        scratch_shapes=[pltpu.SemaphoreType.DMA((n, 3)), pltpu.SemaphoreType.DMA((n, 3)), pltpu.SemaphoreType.DMA((n,))],
        name=name)(*srcs)


def _chip_sums(grads, tag):
    views = [g.reshape(N_DEV // 2, 2, g.shape[0] // N_DEV, g.shape[1]) for g in grads]
    from_sibling = _scatter_pair(views, name="scatter_pair_" + tag)
    return [_pair_add(v, s, name="pair_add") for v, s in zip(views, from_sibling)]


def _sum_chips(parts):
    return _rowmap(_sum_parts_fn, [], stacks=[parts], row_outs=[(parts.shape[2], F32)], tr=128, name="sum_chips")[0]


def _buffer_roles(kinds, bufs):
    carried = [k for k in kinds if k in bufs]
    return carried, [k for k in kinds if k not in bufs]


def _gather_ride(sends, forwards, bufs):
    carried, created = _buffer_roles(list(dict.fromkeys([s[0] for s in sends] + [f[0] for f in forwards])), bufs)
    shape_of = {s[0]: jax.ShapeDtypeStruct((N_DEV,) + s[1].shape, s[1].dtype) for s in sends}
    order = carried + created

    def copies(in_refs, buf_refs, sems):
        x, y, c, chips = _mesh_place()
        buf = dict(zip(order, buf_refs))
        out, s0 = [], 0
        for (kind, _, r0, nr), src in zip(sends, in_refs):
            mine, dst = src.at[pl.ds(r0, nr)], buf[kind].at[4 * x + 2 * y + c, pl.ds(r0, nr)]
            out.append(pltpu.make_async_copy(mine, dst, sems.at[s0 + 8]))
            for k, peer in enumerate([(x, y, 1 - c)] + [(*chip, c) for chip in chips]):
                out.append(pltpu.make_async_remote_copy(src_ref=mine, dst_ref=dst, send_sem=sems.at[s0 + k],
                                                        recv_sem=sems.at[s0 + 4 + k], device_id=peer, device_id_type=_MESH_ID))
            s0 += 9
        for kind, r0, nr in forwards:
            for j, chip in enumerate(chips):
                blk = buf[kind].at[4 * chip[0] + 2 * chip[1] + c, pl.ds(r0, nr)]
                out.append(pltpu.make_async_remote_copy(src_ref=blk, dst_ref=blk, send_sem=sems.at[s0 + j],
                                                        recv_sem=sems.at[s0 + 3 + j], device_id=(x, y, 1 - c),
                                                        device_id_type=_MESH_ID))
            s0 += 6
        return out

    def start(in_refs, buf_refs, sems):
        for cp in copies(in_refs, buf_refs, sems):
            cp.start()

    def finish(in_refs, buf_refs, sems):
        for cp in copies(in_refs, buf_refs, sems):
            cp.wait()

    ride = _Ride(inputs=tuple(s[1] for s in sends), carried=tuple(bufs[k] for k in carried),
                 created=tuple(shape_of[k] for k in created), n_sems=9 * len(sends) + 6 * len(forwards),
                 start=start, finish=finish)
    return ride, order


def _scatter_ride(pieces, bufs):
    carried, created = _buffer_roles(list(dict.fromkeys(p[0] for p in pieces)), bufs)
    shape_of = {p[0]: jax.ShapeDtypeStruct(p[1].shape, p[1].dtype) for p in pieces}
    order = carried + created

    def copies(in_refs, buf_refs, sems):
        x, y, c, chips = _mesh_place()
        buf = dict(zip(order, buf_refs))
        out, s0 = [], 0
        for (kind, _, r0, nr), src in zip(pieces, in_refs):
            dst = buf[kind].at[2 * x + y, pl.ds(r0, nr)]
            out.append(pltpu.make_async_copy(src.at[2 * x + y, pl.ds(r0, nr)], dst, sems.at[s0 + 6]))
            for j, chip in enumerate(chips):
                out.append(pltpu.make_async_remote_copy(
                    src_ref=src.at[2 * chip[0] + chip[1], pl.ds(r0, nr)], dst_ref=dst, send_sem=sems.at[s0 + j],
                    recv_sem=sems.at[s0 + 3 + j], device_id=(*chip, c), device_id_type=_MESH_ID))
            s0 += 7
        return out

    def start(in_refs, buf_refs, sems):
        for cp in copies(in_refs, buf_refs, sems):
            cp.start()

    def finish(in_refs, buf_refs, sems):
        for cp in copies(in_refs, buf_refs, sems):
            cp.wait()

    ride = _Ride(inputs=tuple(p[1] for p in pieces), carried=tuple(bufs[k] for k in carried),
                 created=tuple(shape_of[k] for k in created), n_sems=7 * len(pieces), start=start, finish=finish)
    return ride, order


GATHER_PLAN = (
    ("mm_in_qkv", (("w_in", 0, 3),)),
    ("mm_in_zb", (("w_pa", 0, 1), ("w_pb", 0, 1), ("w_pc", 0, 1), ("w_glu", 0, 1))),
    ("mm_in_gl", (("w_in", 1, 3),)),
    ("attn_fwd0", (("w_ffn_out", 0, 2),)),
    ("attn_fwd1", (("w_ffn_out", 1, 2),)),
    ("mm_o", (("w_o", 0, 1),)),
    ("mm_ffn_in", (("w_in", 2, 3), ("w_ffn_in", 0, 2))),
    ("mm_ffn_out", (("w_ffn_in", 1, 2),)),
    ("norm2", ()),
)
EARLY_KINDS = ("w_ffn_out", "w_ffn_in")
LATE_KINDS = tuple(n for n in SHARDED if n not in EARLY_KINDS)
SCATTER_EARLY_PLAN = (
    ("mm_in_dw_qkv", (("w_ffn_out", 0, 2),)),
    ("mm_in_dx_qkv", (("w_ffn_out", 1, 2),)),
    ("mm_in_dx_zb", (("w_ffn_in", 3, 4),)),
    ("mm_in_dw_gl", (("w_ffn_in", 0, 4),)),
    ("mm_in_dx_gl", (("w_ffn_in", 1, 4), ("w_ffn_in", 2, 4))),
)
SCATTER_PLAN = (
    ("mm_ffn_out_dw", (("w_in", 0, 3),)),
    ("mm_ffn_out_dx", (("w_in", 1, 3),)),
    ("mm_ffn_in_dw", (("w_in", 2, 3), ("w_o", 0, 1), ("w_pa", 0, 1), ("w_pb", 0, 1), ("w_pc", 0, 1), ("w_glu", 0, 1))),
)


def _row_part(rows, part, parts):
    assert rows % (parts * 2 * SUBLANES) == 0
    return part * (rows // parts), rows // parts


class _Carried:
    def __init__(self, plan, blocks, make_ride, forwards_too):
        self.plan, self.blocks, self.make_ride, self.forwards_too = dict(plan), blocks, make_ride, forwards_too
        self.bufs, self.to_forward, self.order = {}, [], []

    def ride(self, host):
        if self.blocks is None or host not in self.plan:
            self.order = []
            return None
        sends = [(k, self.blocks[k], *_row_part(self.blocks[k].shape[-2], part, parts)) for k, part, parts in self.plan[host]]
        if self.forwards_too:
            ride, self.order = self.make_ride(sends, self.to_forward, self.bufs)
            self.to_forward = [(k, r0, nr) for k, _, r0, nr in sends]
        else:
            ride, self.order = self.make_ride(sends, self.bufs)
        return ride

    def took(self, ride_outs):
        for k, buf in zip(self.order, ride_outs[0] if ride_outs else []):
            self.bufs[k] = buf


def _hosted(comm, fn, *args, name, **kwargs):
    res, ride_outs = fn(*args, name=name, rides=[comm.ride(name)], **kwargs)
    comm.took(ride_outs)
    return res


def _small_sizes(shapes):
    return [int(np.prod(shapes[n])) for n in SMALL]


def _pack_small(vals):
    flat = jnp.concatenate([vals[n].reshape(-1).astype(F32) for n in SMALL])
    rows = -(-flat.shape[0] // (LANES * N_DEV * SUBLANES)) * (N_DEV * SUBLANES)
    return jnp.pad(flat, (0, rows * LANES - flat.shape[0])).reshape(rows, LANES)


def _unpack_small(packed, shapes):
    flat = packed.reshape(-1)
    out, off = {}, 0
    for n, size in zip(SMALL, _small_sizes(shapes)):
        out[n] = flat[off:off + size].reshape(shapes[n])
        off += size
    return out


def _row(v):
    return v.reshape(1, -1)


def _layer_params(l, full, small):
    o1, o2, o3 = 3 * QKV_WIDTH, 3 * QKV_WIDTH + 2 * WIDTH_B, 3 * QKV_WIDTH + 2 * WIDTH_B + WIDTH_C
    b_in = small["b_in"][l]
    p = {
        "in_pieces": (("qkv", 0, o1), ("zb", o1, o2 - o1), ("uc", o2, o3 - o2), ("gl", o3, b_in.shape[0] - o3)),
        "b_qkv": _row(b_in[:o1]), "b_zb": _row(b_in[o1:o2]), "b_uc": _row(b_in[o2:o3]), "b_gl": _row(b_in[o3:]),
        "sgu_ln_g": _row(small["sgu_ln_g"][l]), "sgu_ln_b": _row(small["sgu_ln_b"][l]),
        "w_s": small["w_s"][l], "b_s_t": small["b_s"][l].T,
        "lam_re": small["lam_re"][l][:, None, :], "lam_im": small["lam_im"][l][:, None, :],
        "log_dt": small["log_dt"][l][:, None, None],
        "b_re_t": small["b_re"][l].transpose(0, 2, 1), "b_im_t": small["b_im"][l].transpose(0, 2, 1),
        "c_re_t": small["c_re"][l].transpose(0, 2, 1), "c_im_t": small["c_im"][l].transpose(0, 2, 1),
        "d_skip": _row(small["d_skip"][l]), "b_glu": _row(small["b_glu"][l]),
        "ln1_g": _row(small["ln1_g"][l]), "ln1_b": _row(small["ln1_b"][l]),
        "ln2_g": _row(small["ln2_g"][l]), "ln2_b": _row(small["ln2_b"][l]),
    }
    for n in SHARDED:
        p[n] = full[n]
    return p


def _scan_view(a):
    return a.reshape(a.shape[0], _SCAN_ROWS, LANES)


def _twice(fn):
    def both(*args):
        y = fn(*args)
        return y, y
    return both


def _layer_fwd(x, xb, p, biases, comm):
    t, d = x.shape
    r = {"x": x, "xb": xb}
    for piece, off, n in p["in_pieces"]:
        r[piece] = _hosted(comm, _mm, xb, p["w_in"], tb=True, b_off=off, n=n, bias=p["b_" + piece],
                           out_dtype=BF16 if piece == "qkv" else F32, name="mm_in_" + piece)
    ol = []
    for g, dil in enumerate(ATT_DILATIONS):
        ol += list(_attn_fwd(r["qkv"], biases[g], g, dil, comm))
    r["ol"] = ol
    r["ya"], r["ya_b"] = _rowmap(_twice(_combine), ol, row_outs=[(WIDTH_A, F32), (WIDTH_A, BF16)], tr=512,
                                 name="attn_combine")
    r["yb"] = _gmlp_fwd(r["zb"], p["sgu_ln_g"], p["sgu_ln_b"], p["w_s"], p["b_s_t"])
    ab_re, ab_im, bb_re_t, bb_im_t = _s5_disc_fwd(p["lam_re"], p["lam_im"], p["log_dt"], p["b_re_t"], p["b_im_t"])
    r["a_re"], r["a_im"] = ab_re.reshape(_SCAN_ROWS, LANES), ab_im.reshape(_SCAN_ROWS, LANES)
    r["bmat_re"], r["bmat_im"] = _block_diag(bb_re_t), _block_diag(bb_im_t)
    r["cmat_re"], r["cmat_im"] = _block_diag(p["c_re_t"]), _block_diag(p["c_im_t"])
    bu_re = _bdmm(r["uc"], r["bmat_re"], name="s5_in_re")
    bu_im = _bdmm(r["uc"], r["bmat_im"], name="s5_in_im")
    xr, xi = _scan_fwd(_scan_view(bu_re), _scan_view(bu_im), r["a_re"], r["a_im"])
    r["xr"], r["xi"] = xr.reshape(t, SSM_COLS), xi.reshape(t, SSM_COLS)
    y_re = _bdmm(r["xr"], r["cmat_re"], name="s5_out_re")
    y_im = _bdmm(r["xi"], r["cmat_im"], scale=-1.0, name="s5_out_im")
    r["ys"], r["ycp"] = _rowmap(_s5_out, [y_re, y_im, r["uc"]], consts=[p["d_skip"]],
                                row_outs=[(WIDTH_C, F32)] * 2, tr=512, name="s5_out_act")
    r["glin"] = _mm(r["ycp"], p["w_glu"], bias=p["b_glu"], name="mm_glu")
    r["yc"] = _rowmap(_glu, [r["ycp"], r["glin"]], row_outs=[(WIDTH_C, BF16)], tr=512, name="glu")[0]
    r["pa"] = _mm(r["ya_b"], p["w_pa"], tb=True, name="mm_pa")
    r["pb"] = _mm(r["yb"], p["w_pb"], tb=True, name="mm_pb")
    r["pc"] = _mm(r["yc"], p["w_pc"], tb=True, name="mm_pc")
    r["merged"] = _rowmap(_merge, [r["gl"], r["pa"], r["pb"], r["pc"]], row_outs=[(d, BF16)], name="merge")[0]
    r["mo"] = _hosted(comm, _mm, r["merged"], p["w_o"], name="mm_o")
    r["xm"], r["xm_b"] = _rowmap(_twice(_post_norm), [x, r["mo"]], consts=[p["ln1_g"], p["ln1_b"]],
                                 row_outs=[(d, F32), (d, BF16)], name="norm1")
    r["gu"] = _hosted(comm, _mm, r["xm_b"], p["w_ffn_in"], tb=True, name="mm_ffn_in")
    r["act"] = _rowmap(_swiglu, [r["gu"]], row_outs=[(r["gu"].shape[1] // 2, BF16)], tr=128, name="swiglu")[0]
    r["f"] = _hosted(comm, _mm, r["act"], p["w_ffn_out"], name="mm_ffn_out")
    out, out_b = _hosted(comm, _rowmap, _twice(_post_norm), [r["xm"], r["f"]], consts=[p["ln2_g"], p["ln2_b"]],
                         row_outs=[(d, F32), (d, BF16)], name="norm2")
    return out, out_b, r


def _layer_bwd(dout, r, p, biases, consts, comm, make_early):
    t, d = dout.shape
    gw, gs = {}, {}
    ffw = r["gu"].shape[1]
    dxm, df, gs["ln2_g"], gs["ln2_b"] = _rowmap(
        _post_norm_bwd, [r["xm"], r["f"], dout], consts=[p["ln2_g"], p["ln2_b"]],
        row_outs=[(d, F32), (d, BF16)], red_outs=[(1, d)] * 2, name="norm2_bwd")
    gw["w_ffn_out"] = _hosted(comm, _mm, r["act"], df, ta=True, out_dtype=BF16, name="mm_ffn_out_dw")
    dact = _hosted(comm, _mm, df, p["w_ffn_out"], tb=True, name="mm_ffn_out_dx")
    dgu = _rowmap(_swiglu_bwd, [r["gu"], dact], row_outs=[(ffw, BF16)], tr=128, name="swiglu_bwd")[0]
    gw["w_ffn_in"] = _hosted(comm, _mm, dgu, r["xm_b"], ta=True, out_dtype=BF16, name="mm_ffn_in_dw")
    dxm = _hosted(comm, _mm, dgu, p["w_ffn_in"], add=dxm, name="mm_ffn_in_dx")
    early = make_early({n: gw[n] for n in EARLY_KINDS})
    dx, dmo, gs["ln1_g"], gs["ln1_b"] = _rowmap(
        _post_norm_bwd, [r["x"], r["mo"], dxm], consts=[p["ln1_g"], p["ln1_b"]],
        row_outs=[(d, F32), (d, BF16)], red_outs=[(1, d)] * 2, name="norm1_bwd")
    gw["w_o"] = _hosted(comm, _mm, r["merged"], dmo, ta=True, out_dtype=BF16, name="mm_o_dw")
    dmerged = _hosted(comm, _mm, dmo, p["w_o"], tb=True, name="mm_o_dx")
    dgl, dpa, dpb, dpc, db_gl = _rowmap(
        _merge_bwd, [r["gl"], r["pa"], r["pb"], r["pc"], dmerged],
        row_outs=[(3 * d, BF16), (d, BF16), (d, BF16), (d, BF16)], red_outs=[(1, 3 * d)], tr=128, name="merge_bwd")
    gw["w_pa"] = _mm(dpa, r["ya_b"], ta=True, out_dtype=BF16, name="mm_pa_dw")
    gw["w_pb"] = _mm(dpb, r["yb"], ta=True, out_dtype=BF16, name="mm_pb_dw")
    gw["w_pc"] = _mm(dpc, r["yc"], ta=True, out_dtype=BF16, name="mm_pc_dw")
    dya = _mm(dpa, p["w_pa"], name="mm_pa_dx")
    dyb = _mm(dpb, p["w_pb"], name="mm_pb_dx")
    dyc = _mm(dpc, p["w_pc"], name="mm_pc_dx")
    dycp, dglin, gs["b_glu"] = _rowmap(_glu_bwd, [r["ycp"], r["glin"], dyc], row_outs=[(WIDTH_C, F32), (WIDTH_C, BF16)],
                                       red_outs=[(1, WIDTH_C)], tr=512, name="glu_bwd")
    gw["w_glu"] = _mm(r["ycp"], dglin, ta=True, out_dtype=BF16, name="mm_glu_dw")
    dycp = _mm(dglin, p["w_glu"], tb=True, add=dycp, name="mm_glu_dx")
    dys, duc, gs["d_skip"] = _rowmap(_s5_out_bwd, [r["ys"], r["uc"], dycp], consts=[p["d_skip"]],
                                     row_outs=[(WIDTH_C, F32)] * 2, red_outs=[(1, WIDTH_C)], tr=512, name="s5_out_act_bwd")
    dxr = _bdmm(dys, r["cmat_re"], tb=True, name="s5_out_re_dx")
    dxi = _bdmm(dys, r["cmat_im"], tb=True, scale=-1.0, name="s5_out_im_dx")
    d_cmat_re = _bdmm_tn(r["xr"], dys, N_SSM_BLOCKS, name="s5_out_re_dw")
    d_cmat_im = _bdmm_tn(r["xi"], dys, N_SSM_BLOCKS, scale=-1.0, name="s5_out_im_dw")
    g_re, g_im, da_re, da_im = _scan_bwd(_scan_view(dxr), _scan_view(dxi), _scan_view(r["xr"]), _scan_view(r["xi"]),
                                         r["a_re"], r["a_im"])
    g_re, g_im = g_re.reshape(t, SSM_COLS), g_im.reshape(t, SSM_COLS)
    duc = _bdmm(g_re, r["bmat_re"], tb=True, add=duc, name="s5_in_re_dx")
    duc = _bdmm(g_im, r["bmat_im"], tb=True, add=duc, name="s5_in_im_dx")
    d_bmat_re = _bdmm_tn(r["uc"], g_re, N_SSM_BLOCKS, name="s5_in_re_dw")
    d_bmat_im = _bdmm_tn(r["uc"], g_im, N_SSM_BLOCKS, name="s5_in_im_dw")
    cts = (da_re.reshape(N_GROUPS_C, 1, SSM_STATE), da_im.reshape(N_GROUPS_C, 1, SSM_STATE),
           _block_diag_extract(d_bmat_re, SSM_GROUP, SSM_STATE), _block_diag_extract(d_bmat_im, SSM_GROUP, SSM_STATE))
    d_lr, d_li, d_ldt, d_br_t, d_bi_t = _s5_disc_bwd(p["lam_re"], p["lam_im"], p["log_dt"], p["b_re_t"], p["b_im_t"], cts)
    gs["lam_re"], gs["lam_im"], gs["log_dt"] = d_lr[:, 0, :], d_li[:, 0, :], d_ldt[:, 0, 0]
    gs["b_re"], gs["b_im"] = d_br_t.transpose(0, 2, 1), d_bi_t.transpose(0, 2, 1)
    gs["c_re"] = _block_diag_extract(d_cmat_re, SSM_STATE, SSM_GROUP).transpose(0, 2, 1)
    gs["c_im"] = _block_diag_extract(d_cmat_im, SSM_STATE, SSM_GROUP).transpose(0, 2, 1)
    dzb, db_zb, gs["sgu_ln_g"], gs["sgu_ln_b"], gs["w_s"], dbs_t = _gmlp_bwd(
        r["zb"], dyb, p["sgu_ln_g"], p["sgu_ln_b"], p["w_s"], p["b_s_t"], consts["group_sel"])
    gs["b_s"] = dbs_t[:, :N_GROUPS_B].T
    do_corr = _rowmap(_combine_bwd, r["ol"] + [r["ya"], dya], consts=[consts["head_ones"]],
                      row_outs=[(WIDTH_A, F32)] * 6, tr=512, name="attn_combine_bwd")
    dq, dk, dv, dbias = [], [], [], []
    for g, dil in enumerate(ATT_DILATIONS):
        do_g, corr_g, lse_g = do_corr[g], do_corr[3 + g], r["ol"][2 * g + 1]
        dq.append(_attn_dq(r["qkv"], biases[g], do_g, lse_g, corr_g, g, dil))
        dk_g, dv_g, db_g = _attn_dkv(r["qkv"], biases[g], do_g, lse_g, corr_g, g, dil)
        dk.append(dk_g)
        dv.append(dv_g)
        dbias.append(db_g)
    cast_colsum = lambda a: (a, jnp.sum(a, axis=0, keepdims=True))
    dqkv, db_qkv = _rowmap(cast_colsum, [jnp.concatenate(dq + dk + dv, axis=1)], row_outs=[(3 * QKV_WIDTH, BF16)],
                           red_outs=[(1, 3 * QKV_WIDTH)], tr=512, name="cast_colsum_qkv")
    duc, db_uc = _rowmap(cast_colsum, [duc], row_outs=[(WIDTH_C, BF16)], red_outs=[(1, WIDTH_C)], tr=512,
                         name="cast_colsum_uc")
    dpieces = {"qkv": dqkv, "zb": dzb, "uc": duc, "gl": dgl}
    rows_in = p["w_in"].shape[0]
    dw_in = None
    for piece, off, n in p["in_pieces"]:
        dw_in = _hosted(early, _mm, dpieces[piece], r["xb"], ta=True, out_dtype=BF16, into=(rows_in, off, dw_in),
                        name="mm_in_dw_" + piece)
        dx = _hosted(early, _mm, dpieces[piece], p["w_in"], b_off=off, add=dx, name="mm_in_dx_" + piece)
    gw["w_in"] = dw_in
    gs["b_in"] = jnp.concatenate([db_qkv, db_zb, db_uc, db_gl], axis=1)[0]
    for n in ("sgu_ln_g", "sgu_ln_b", "d_skip", "b_glu", "ln1_g", "ln1_b", "ln2_g", "ln2_b"):
        gs[n] = gs[n][0]
    return dx, gw, gs, dbias, early.bufs


def _cast_bf16(w):
    w2 = w.reshape(-1, w.shape[-1])
    out = _rowmap(lambda a: a, [w2], row_outs=[(w2.shape[1], BF16)], tr=512, name="cast_bf16")[0]
    return out.reshape(w.shape)


def _static_consts():
    head_ones = np.kron(np.eye(HEADS_PER_GROUP, dtype=np.float32), np.ones((HEAD_DIM, HEAD_DIM), np.float32))
    group_sel = np.zeros((WIDTH_B, LANES), np.float32)
    group_sel[np.arange(WIDTH_B), np.arange(WIDTH_B) // CHUNK] = 1.0
    return {"head_ones": jnp.asarray(head_ones), "group_sel": jnp.asarray(group_sel)}


def _step(x, tgt, w, m, v):
    shapes = {n: w[n].shape for n in WEIGHTS}
    consts = _static_consts()
    mine_bf = {n: _cast_bf16(w[n].transpose(0, 2, 1) if n in TRANSPOSED else w[n]) for n in SHARDED}
    small = {n: w[n] for n in SMALL}
    buckets = [jnp.asarray(_bucket_table(dil)) for dil in ATT_DILATIONS]
    biases = [_bias_fwd(w["rel_bias"], buckets[g], g) for g in range(len(ATT_DILATIONS))]
    params, saved = [], []
    h, hb = _rowmap(_twice(lambda a: a), [x], row_outs=[(x.shape[1], F32), (x.shape[1], BF16)], name="cast_x")
    gathered = dict(zip(SHARDED, _gather_layer([mine_bf[n] for n in SHARDED], 0, name="gather_layer0")))
    for l in range(DEPTH):
        p = _layer_params(l, {n: g.reshape(-1, g.shape[2]) for n, g in gathered.items()}, small)
        ahead = _Carried(GATHER_PLAN, {n: mine_bf[n][l + 1] for n in SHARDED} if l + 1 < DEPTH else None,
                         _gather_ride, forwards_too=True)
        h, hb, r = _layer_fwd(h, hb, p, biases, ahead)
        gathered = ahead.bufs
        params.append(p)
        saved.append(r)
    dy, loss_part = _rowmap(_loss_fn, [h, tgt], row_outs=[(h.shape[1], F32)], red_outs=[(1, LANES)], name="loss")
    loss = lax.psum(loss_part[0, 0], MESH_AXES)
    g_mine, gs_layers = {n: [None] * DEPTH for n in SHARDED}, [None] * DEPTH
    dbias_sum = None
    def scatter_steps(plan, grads, tag):
        sums = _chip_sums(list(grads.values()), tag)
        return _Carried(plan, dict(zip(grads, sums)), _scatter_ride, forwards_too=False)

    behind = _Carried(SCATTER_PLAN, None, _scatter_ride, forwards_too=False)
    for l in reversed(range(DEPTH)):
        dy, gw, gs_layers[l], dbias, early_parts = _layer_bwd(
            dy, saved[l], params[l], biases, consts, behind,
            functools.partial(scatter_steps, SCATTER_EARLY_PLAN, tag="early"))
        saved[l] = None
        for n in EARLY_KINDS:
            g_mine[n][l] = _sum_chips(early_parts[n])
        if behind.blocks is not None:
            for n in LATE_KINDS:
                g_mine[n][l + 1] = _sum_chips(behind.bufs[n])
        behind = scatter_steps(SCATTER_PLAN, {n: gw[n] for n in LATE_KINDS}, "late")
        if l == 0:
            last = _scatter_chips([behind.blocks[n] for n in LATE_KINDS], name="scatter_chips_layer0")
            for n, parts in zip(LATE_KINDS, last):
                g_mine[n][0] = _sum_chips(parts)
        if dbias_sum is None:
            dbias_sum = dbias
        else:
            dbias_sum = [_rowmap(lambda a, b: a + b, [a.reshape(-1, 2 * ATT_BLOCK), b.reshape(-1, 2 * ATT_BLOCK)],
                                 row_outs=[(2 * ATT_BLOCK, F32)], name="dbias_add")[0].reshape(a.shape)
                         for a, b in zip(dbias_sum, dbias)]
    drel = [_bias_bwd(dbias_sum[g], buckets[g], g) for g in range(len(ATT_DILATIONS))]
    drel = _rowmap(lambda a, b, c: a + b + c, drel, row_outs=[(LANES, F32)], name="drel_add")[0]
    grad_small_local = {n: jnp.stack([gs_layers[l][n] for l in range(DEPTH)]) for n in SMALL if n != "rel_bias"}
    grad_small_local["rel_bias"] = drel[:, :shapes["rel_bias"][1]]
    out_g, out_d, out_m, out_v = {}, {}, {}, {}
    for n in SHARDED:
        g = jnp.stack(g_mine[n])
        out_g[n] = g.transpose(0, 2, 1) if n in TRANSPOSED else g
        cols = shapes[n][-1]
        res = _rowmap(_adamw, [a.reshape(-1, cols) for a in (w[n], out_g[n], m[n], v[n])],
                      row_outs=[(cols, F32)] * 3, tr=128, name="adamw_" + n)
        out_d[n], out_m[n], out_v[n] = [a.reshape(shapes[n]) for a in res]
    packed = _pack_small(grad_small_local)
    rows = packed.shape[0] // N_DEV
    parts = _exchange(packed.reshape(N_DEV, rows, LANES), gather=False, name="scatter_small")
    mine = _rowmap(_sum_parts_fn, [], stacks=[parts], row_outs=[(LANES, F32)], name="sum_small")[0]
    g_small = _exchange(mine, gather=True, name="gather_small").reshape(-1, LANES)
    res = _rowmap(lambda w_, g_, m_, v_: _adamw(w_, g_, m_, v_),
                  [_pack_small(w), g_small, _pack_small(m), _pack_small(v)],
                  row_outs=[(LANES, F32)] * 3, name="adamw_small")
    small_shapes = {n: shapes[n] for n in SMALL}
    out_g.update(_unpack_small(g_small, small_shapes))
    for dst, packed_res in zip((out_d, out_m, out_v), res):
        dst.update(_unpack_small(packed_res, small_shapes))
    return loss, dy, out_g, out_d, out_m, out_v


def kernel(x, w_in, b_in, rel_bias, sgu_ln_g, sgu_ln_b, w_s, b_s, lam_re, lam_im, log_dt, b_re, b_im, c_re, c_im, d_skip, w_glu, b_glu, w_pa, w_pb, w_pc, w_o, ln1_g, ln1_b, w_ffn_in, w_ffn_out, ln2_g, ln2_b, loss_target, m_w_in, m_b_in, m_rel_bias, m_sgu_ln_g, m_sgu_ln_b, m_w_s, m_b_s, m_lam_re, m_lam_im, m_log_dt, m_b_re, m_b_im, m_c_re, m_c_im, m_d_skip, m_w_glu, m_b_glu, m_w_pa, m_w_pb, m_w_pc, m_w_o, m_ln1_g, m_ln1_b, m_w_ffn_in, m_w_ffn_out, m_ln2_g, m_ln2_b, v_w_in, v_b_in, v_rel_bias, v_sgu_ln_g, v_sgu_ln_b, v_w_s, v_b_s, v_lam_re, v_lam_im, v_log_dt, v_b_re, v_b_im, v_c_re, v_c_im, v_d_skip, v_w_glu, v_b_glu, v_w_pa, v_w_pb, v_w_pc, v_w_o, v_ln1_g, v_ln1_b, v_w_ffn_in, v_w_ffn_out, v_ln2_g, v_ln2_b):
    args = dict(locals())
    w = {n: args[n] for n in WEIGHTS}
    m = {n: args["m_" + n] for n in WEIGHTS}
    v = {n: args["v_" + n] for n in WEIGHTS}
    loss, dx, g, d, nm, nv = _step(x[0], loss_target[0], w, m, v)
    return (loss, dx[None], *[g[n] for n in WEIGHTS], *[d[n] for n in WEIGHTS],
            *[nm[n] for n in WEIGHTS], *[nv[n] for n in WEIGHTS])
```

```python
import functools
import math
from typing import Callable, NamedTuple

import numpy as np
import jax
import jax.numpy as jnp
from jax import lax
from jax.experimental import pallas as pl
from jax.experimental.pallas import tpu as pltpu

F32 = jnp.float32
BF16 = jnp.bfloat16

MESH_AXES = ("x", "y", "c")
N_DEV = 8
DEPTH = 4

ATT_DILATIONS = (1, 4, 16)
ATT_STEPS = 128
HEADS_PER_GROUP = 8
HEAD_DIM = 64
QKV_WIDTH = 1536
WIDTH_A = HEADS_PER_GROUP * HEAD_DIM
ATT_BLOCK = 128
N_REL_BUCKETS = 32
REL_MAX_DIST = 2048
NEG_INF = -1e30
CHUNK = 128
WIDTH_B = 768
N_GROUPS_B = 6
WIDTH_C = 768
SSM_GROUP = 16
N_GROUPS_C = 48
SSM_STATE = 64
SSM_PACK = 8
N_SSM_BLOCKS = N_GROUPS_C // SSM_PACK
SSM_COLS = N_GROUPS_C * SSM_STATE
ALPHA = (2 * DEPTH) ** 0.25

ADAM_LR = 0.001
ADAM_B1 = 0.9
ADAM_B2 = 0.999
ADAM_EPS = 1e-08
ADAM_WD = 0.01
ADAM_STEP = 10

LANES = 128
SUBLANES = 8
VMEM_LIMIT = 48 * 1024 * 1024

SHARDED = ("w_in", "w_glu", "w_pa", "w_pb", "w_pc", "w_o", "w_ffn_in", "w_ffn_out")
TRANSPOSED = ("w_in", "w_pa", "w_pb", "w_pc", "w_ffn_in")
SMALL = ("b_in", "rel_bias", "sgu_ln_g", "sgu_ln_b", "w_s", "b_s", "lam_re", "lam_im", "log_dt",
         "b_re", "b_im", "c_re", "c_im", "d_skip", "b_glu", "ln1_g", "ln1_b", "ln2_g", "ln2_b")
WEIGHTS = ("w_in", "b_in", "rel_bias", "sgu_ln_g", "sgu_ln_b", "w_s", "b_s", "lam_re", "lam_im",
           "log_dt", "b_re", "b_im", "c_re", "c_im", "d_skip", "w_glu", "b_glu", "w_pa", "w_pb",
           "w_pc", "w_o", "ln1_g", "ln1_b", "w_ffn_in", "w_ffn_out", "ln2_g", "ln2_b")


def _pick(dim, target, mult):
    best = None
    for t in range(mult, min(dim, target) + 1, mult):
        if dim % t == 0:
            best = t
    return dim if best is None else best


def _cparams(*sem):
    return pltpu.CompilerParams(dimension_semantics=sem, vmem_limit_bytes=VMEM_LIMIT)


def _zero_map(ndim):
    return lambda *_: (0,) * ndim


_HBM = pl.BlockSpec(memory_space=pl.ANY)
_MESH_ID = pl.DeviceIdType.MESH


class _Ride(NamedTuple):
    inputs: tuple
    carried: tuple
    created: tuple
    n_sems: int
    start: Callable
    finish: Callable


def _pallas(body, *, grid, in_specs, out_specs, out_shape, args, scratch=(), semantics, rides=(), aliases=None, name):
    rides = [r for r in rides if r is not None]
    n_in, n_out, n_scr = len(args), len(out_shape), len(scratch)
    r_args, r_shapes, aliases, spans = [], [], dict(aliases or {}), []
    for r in rides:
        i0, o0 = len(r_args), len(r_shapes)
        r_args += [*r.inputs, *r.carried]
        for k, a in enumerate(r.carried):
            aliases[n_in + i0 + len(r.inputs) + k] = n_out + o0 + k
        r_shapes += [jax.ShapeDtypeStruct(a.shape, a.dtype) for a in r.carried] + list(r.created)
        spans.append((i0, len(r.inputs), o0, len(r.carried) + len(r.created)))

    def full_body(*refs):
        host_in, ride_in = refs[:n_in], refs[n_in:n_in + len(r_args)]
        p = n_in + len(r_args)
        host_out, ride_out = refs[p:p + n_out], refs[p + n_out:p + n_out + len(r_shapes)]
        p += n_out + len(r_shapes)
        host_scr, ride_sems = refs[p:p + n_scr], refs[p + n_scr:]
        ids = [pl.program_id(k) for k in range(len(grid))]
        first = functools.reduce(jnp.logical_and, [i == 0 for i in ids])
        last = functools.reduce(jnp.logical_and, [i == g - 1 for i, g in zip(ids, grid)])

        def each(method):
            for r, (i0, ni, o0, no), sems in zip(rides, spans, ride_sems):
                getattr(r, method)(ride_in[i0:i0 + ni], ride_out[o0:o0 + no], sems)

        if rides:
            pl.when(first)(lambda: each("start"))
        body(*host_in, *host_out, *host_scr)
        if rides:
            pl.when(last)(lambda: each("finish"))

    if rides:
        semantics = ("arbitrary",) * len(grid)
    outs = pl.pallas_call(
        full_body, grid=grid, in_specs=list(in_specs) + [_HBM] * len(r_args),
        out_specs=list(out_specs) + [_HBM] * len(r_shapes), out_shape=list(out_shape) + r_shapes,
        scratch_shapes=list(scratch) + [pltpu.SemaphoreType.DMA((r.n_sems,)) for r in rides],
        input_output_aliases=aliases, compiler_params=_cparams(*semantics), name=name)(*args, *r_args)
    return outs[:n_out], [outs[n_out + o0:n_out + o0 + no] for _, _, o0, no in spans]


def _rowmap(fn, rows, consts=(), stacks=(), row_outs=(), red_outs=(), tr=256, name=None, rides=None):
    t = rows[0].shape[0] if rows else stacks[0].shape[1]
    dtypes = [a.dtype for a in (*rows, *stacks)] + [dt for _, dt in row_outs]
    packed = any(jnp.dtype(dt).itemsize < 4 for dt in dtypes)
    tr = _pick(t, tr, 2 * SUBLANES if packed else SUBLANES)
    n_r, n_c, n_s, n_o = len(rows), len(consts), len(stacks), len(row_outs)

    def body(*refs):
        ins = [r[...] for r in refs[:n_r + n_c + n_s]]
        outs = refs[n_r + n_c + n_s:n_r + n_c + n_s + n_o]
        reds = refs[n_r + n_c + n_s + n_o:]
        res = fn(*ins)
        if not isinstance(res, (tuple, list)):
            res = (res,)
        for o, v in zip(outs, res[:n_o]):
            o[...] = v.astype(o.dtype)
        if reds:
            @pl.when(pl.program_id(0) == 0)
            def _():
                for r in reds:
                    r[...] = jnp.zeros_like(r)
            for r, v in zip(reds, res[n_o:]):
                r[...] += v

    in_specs = [pl.BlockSpec((tr, r.shape[1]), lambda i: (i, 0)) for r in rows]
    in_specs += [pl.BlockSpec(c.shape, _zero_map(c.ndim)) for c in consts]
    in_specs += [pl.BlockSpec((s.shape[0], tr, s.shape[2]), lambda i: (0, i, 0)) for s in stacks]
    out_specs = [pl.BlockSpec((tr, w), lambda i: (i, 0)) for w, _ in row_outs]
    out_specs += [pl.BlockSpec(s, _zero_map(len(s))) for s in red_outs]
    out_shape = [jax.ShapeDtypeStruct((t, w), dt) for w, dt in row_outs]
    out_shape += [jax.ShapeDtypeStruct(s, F32) for s in red_outs]
    outs, ride_outs = _pallas(body, grid=(t // tr,), in_specs=in_specs, out_specs=out_specs, out_shape=out_shape,
                              args=[*rows, *consts, *stacks], semantics=("arbitrary",), rides=rides or (), name=name)
    return outs if rides is None else (outs, ride_outs)


MM_VMEM_BUDGET = 36 * 1024 * 1024


def _divisors(dim, mult, must_divide=0):
    out = [t for t in range(dim, 0, -mult) if t % mult == 0 and dim % t == 0 and must_divide % t == 0]
    return out or [dim]


def _mm_tiles(m, n, k, a_bytes, b_bytes, out_bytes, extra_bytes, ta, b_off_n, b_off_k, out_off, tm, tn):
    tms = _divisors(m, LANES if ta else SUBLANES, out_off)
    tm = next((t for t in tms if t <= tm), tms[-1])
    tns = [t for t in _divisors(n, LANES, b_off_n) if t <= tn] or [_divisors(n, LANES, b_off_n)[-1]]
    for tn_ in tns:
        for tk in _divisors(k, LANES, b_off_k):
            acc = 0 if tk == k else tm * tn_ * 4
            need = 2 * (tm * tk * a_bytes + tk * tn_ * b_bytes + tm * tn_ * (out_bytes + extra_bytes)) + acc
            if need <= MM_VMEM_BUDGET:
                return tm, tn_, tk
    return tm, tns[-1], _divisors(k, LANES, b_off_k)[-1]


def _mm(a, b, *, ta=False, tb=False, bias=None, add=None, out_dtype=F32, b_off=0, n=None, tm=1024, tn=1024, name=None,
        rides=None, into=None):
    k, m = a.shape if ta else a.shape[::-1]
    if tb:
        n = b.shape[0] if n is None else n
        assert b.shape[1] == k and b_off + n <= b.shape[0]
    else:
        n = b.shape[1]
        assert b_off + k <= b.shape[0]
    out_rows, out_off, out_buf = (m, 0, None) if into is None else into
    extra = 4 if add is not None else 0
    tm, tn, tk = _mm_tiles(m, n, k, a.dtype.itemsize, b.dtype.itemsize, jnp.dtype(out_dtype).itemsize, extra, ta,
                           b_off if tb else 0, 0 if tb else b_off, out_off, tm, tn)
    nk = k // tk
    off_n, off_k = (b_off // tn, 0) if tb else (0, b_off // tk)
    off_m = out_off // tm
    dims = (((0 if ta else 1,), (1 if tb else 0,)), ((), ()))

    def body(*refs):
        a_ref, b_ref = refs[0], refs[1]
        rest = list(refs[2:])
        bias_ref = rest.pop(0) if bias is not None else None
        add_ref = rest.pop(0) if add is not None else None
        o_ref = rest.pop(0)
        part = lax.dot_general(a_ref[...].astype(BF16), b_ref[...].astype(BF16), dims, preferred_element_type=F32)

        def finish(r):
            if bias_ref is not None:
                r = r + bias_ref[...]
            if add_ref is not None:
                r = r + add_ref[...]
            o_ref[...] = r.astype(o_ref.dtype)

        if nk == 1:
            finish(part)
        else:
            acc_ref = rest.pop(0)
            kk = pl.program_id(2)

            @pl.when(kk == 0)
            def _():
                acc_ref[...] = part

            @pl.when(kk > 0)
            def _():
                acc_ref[...] += part

            @pl.when(kk == nk - 1)
            def _():
                finish(acc_ref[...])

    a_spec = pl.BlockSpec((tk, tm), lambda i, j, q: (q, i)) if ta else pl.BlockSpec((tm, tk), lambda i, j, q: (i, q))
    if tb:
        b_spec = pl.BlockSpec((tn, tk), lambda i, j, q: (j + off_n, q))
    else:
        b_spec = pl.BlockSpec((tk, tn), lambda i, j, q: (q + off_k, j))
    in_specs, args = [a_spec, b_spec], [a, b]
    if bias is not None:
        in_specs.append(pl.BlockSpec((1, tn), lambda i, j, q: (0, j)))
        args.append(bias)
    if add is not None:
        in_specs.append(pl.BlockSpec((tm, tn), lambda i, j, q: (i, j)))
        args.append(add)
    aliases = {}
    if out_buf is not None:
        assert out_buf.shape == (out_rows, n) and out_buf.dtype == jnp.dtype(out_dtype)
        in_specs.append(_HBM)
        args.append(out_buf)
        aliases = {len(args) - 1: 0}

    def body_in_place(*refs):
        body(*refs[:len(args) - 1], *refs[len(args):])

    outs, ride_outs = _pallas(
        body if out_buf is None else body_in_place, grid=(m // tm, n // tn, nk), in_specs=in_specs,
        out_specs=[pl.BlockSpec((tm, tn), lambda i, j, q: (i + off_m, j))],
        out_shape=[jax.ShapeDtypeStruct((out_rows, n), out_dtype)], args=args,
        scratch=[] if nk == 1 else [pltpu.VMEM((tm, tn), F32)],
        semantics=("parallel", "parallel", "arbitrary"), rides=rides or (), aliases=aliases, name=name)
    return outs[0] if rides is None else (outs[0], ride_outs)


def _mm_swiglu(a, w_t, *, name, rides=None):
    m, k = a.shape
    f = w_t.shape[0] // 2
    tm, tn, tk = _mm_tiles(m, f, k, a.dtype.itemsize, 2 * w_t.dtype.itemsize, 3 * 2, 0, False, 0, 0, 0, 1024, 512)
    assert tk == k, "the fused activation needs the whole contraction in one block"

    def body(a_ref, g_ref, u_ref, gate_ref, up_ref, act_ref):
        av = a_ref[...].astype(BF16)
        gate = lax.dot_general(av, g_ref[...].astype(BF16), _NT, preferred_element_type=F32)
        up = lax.dot_general(av, u_ref[...].astype(BF16), _NT, preferred_element_type=F32)
        gate_ref[...] = gate.astype(gate_ref.dtype)
        up_ref[...] = up.astype(up_ref.dtype)
        act_ref[...] = _swiglu2(gate, up).astype(act_ref.dtype)

    out_spec = pl.BlockSpec((tm, tn), lambda i, j: (i, j))
    outs, ride_outs = _pallas(
        body, grid=(m // tm, f // tn),
        in_specs=[pl.BlockSpec((tm, k), lambda i, j: (i, 0)), pl.BlockSpec((tn, k), lambda i, j: (j, 0)),
                  pl.BlockSpec((tn, k), lambda i, j: (j + f // tn, 0))],
        out_specs=[out_spec] * 3, out_shape=[jax.ShapeDtypeStruct((m, f), BF16)] * 3, args=[a, w_t, w_t],
        semantics=("parallel", "parallel"), rides=rides or (), name=name)
    return outs if rides is None else (outs, ride_outs)


def _bdmm(a, bm, *, tb=False, scale=1.0, add=None, name=None):
    t = a.shape[0]
    nj, ka, kb = bm.shape
    kin, kout = (kb, ka) if tb else (ka, kb)
    tm = _pick(t, 512, SUBLANES)
    dims = (((1,), (1 if tb else 0,)), ((), ()))

    def body(*refs):
        a_ref, b_ref = refs[0], refs[1]
        add_ref = refs[2] if add is not None else None
        o_ref = refs[-1]
        r = lax.dot_general(a_ref[...].astype(BF16), b_ref[0].astype(BF16), dims, preferred_element_type=F32)
        if scale != 1.0:
            r = r * scale
        if add_ref is not None:
            r = r + add_ref[...]
        o_ref[...] = r

    in_specs = [pl.BlockSpec((tm, kin), lambda i, j: (i, j)), pl.BlockSpec((1, ka, kb), lambda i, j: (j, 0, 0))]
    args = [a, bm]
    if add is not None:
        in_specs.append(pl.BlockSpec((tm, kout), lambda i, j: (i, j)))
        args.append(add)
    return pl.pallas_call(
        body, grid=(t // tm, nj), in_specs=in_specs, out_specs=pl.BlockSpec((tm, kout), lambda i, j: (i, j)),
        out_shape=jax.ShapeDtypeStruct((t, nj * kout), F32),
        compiler_params=_cparams("parallel", "parallel"), name=name)(*args)


def _bdmm_tn(a, b, nj, *, scale=1.0, name=None):
    t = a.shape[0]
    ka, kb = a.shape[1] // nj, b.shape[1] // nj
    tm = _pick(t, 512, LANES)
    nt = t // tm

    def body(a_ref, b_ref, o_ref):
        @pl.when(pl.program_id(1) == 0)
        def _():
            o_ref[...] = jnp.zeros_like(o_ref)

        r = lax.dot_general(a_ref[...].astype(BF16), b_ref[...].astype(BF16), (((0,), (0,)), ((), ())),
                            preferred_element_type=F32)
        o_ref[0] += r if scale == 1.0 else r * scale

    return pl.pallas_call(
        body, grid=(nj, nt),
        in_specs=[pl.BlockSpec((tm, ka), lambda j, i: (i, j)), pl.BlockSpec((tm, kb), lambda j, i: (i, j))],
        out_specs=pl.BlockSpec((1, ka, kb), lambda j, i: (j, 0, 0)),
        out_shape=jax.ShapeDtypeStruct((nj, ka, kb), F32),
        compiler_params=_cparams("parallel", "arbitrary"), name=name)(a, b)


def _ln(x, g, b, eps=1e-5):
    mu = jnp.mean(x, axis=-1, keepdims=True)
    var = jnp.mean(jnp.square(x - mu), axis=-1, keepdims=True)
    return (x - mu) * lax.rsqrt(var + eps) * g + b


def _post_norm(x, f, g, b):
    return _ln(ALPHA * x + f, g, b)


def _post_norm_bwd(x, f, dy, g, b):
    _, vjp = jax.vjp(_post_norm, x, f, g, b)
    return vjp(dy)


def _merge3(g0, g1, g2, pa, pb, pc):
    return jax.nn.sigmoid(g0) * pa + jax.nn.sigmoid(g1) * pb + jax.nn.sigmoid(g2) * pc


def _merge(gl, pa, pb, pc):
    d = pa.shape[1]
    return _merge3(gl[:, :d], gl[:, d:2 * d], gl[:, 2 * d:], pa, pb, pc)


def _merge_bwd(gl, pa, pb, pc, dm):
    d = pa.shape[1]
    _, vjp = jax.vjp(_merge3, gl[:, :d], gl[:, d:2 * d], gl[:, 2 * d:], pa, pb, pc)
    d0, d1, d2, dpa, dpb, dpc = vjp(dm)
    dgl = jnp.concatenate([d0, d1, d2], axis=1)
    return dgl, dpa, dpb, dpc, jnp.sum(dgl, axis=0, keepdims=True)


def _swiglu2(gate, up):
    return jax.nn.silu(gate) * up


def _swiglu_bwd(gate, up, dact):
    _, vjp = jax.vjp(_swiglu2, gate.astype(F32), up.astype(F32))
    dg, du = vjp(dact)
    return jnp.concatenate([dg, du], axis=1)


def _glu(ycp, lin):
    return ycp * jax.nn.sigmoid(lin)


def _glu_bwd(ycp, lin, dyc):
    _, vjp = jax.vjp(_glu, ycp, lin)
    dycp, dlin = vjp(dyc)
    return dycp, dlin, jnp.sum(dlin, axis=0, keepdims=True)


def _s5_out(yre, yim, uc, dskip):
    ys = yre + yim + dskip * uc
    return ys, jax.nn.gelu(ys)


def _s5_out_bwd(ys, uc, dycp, dskip):
    _, vjp = jax.vjp(jax.nn.gelu, ys)
    dys = vjp(dycp)[0]
    return dys, dys * dskip, jnp.sum(dys * uc, axis=0, keepdims=True)


def _combine(o0, l0, o1, l1, o2, l2):
    m = jnp.maximum(jnp.maximum(l0, l1), l2)
    e0, e1, e2 = jnp.exp(l0 - m), jnp.exp(l1 - m), jnp.exp(l2 - m)
    s = e0 + e1 + e2
    return (e0 / s) * o0 + (e1 / s) * o1 + (e2 / s) * o2


def _combine_bwd(o0, l0, o1, l1, o2, l2, ya, dya, head_ones):
    m = jnp.maximum(jnp.maximum(l0, l1), l2)
    e0, e1, e2 = jnp.exp(l0 - m), jnp.exp(l1 - m), jnp.exp(l2 - m)
    s = e0 + e1 + e2
    dot_ya = jnp.dot(dya * ya, head_ones, precision=lax.Precision.HIGHEST, preferred_element_type=F32)
    w0, w1, w2 = e0 / s, e1 / s, e2 / s
    return w0 * dya, w1 * dya, w2 * dya, -w0 * dot_ya, -w1 * dot_ya, -w2 * dot_ya


def _loss_fn(y, tgt):
    err = y - tgt
    part = jnp.sum(jnp.sum(jnp.square(err), axis=1, keepdims=True), axis=0, keepdims=True) * (0.5 / y.shape[1])
    return err * (1.0 / y.shape[1]), jnp.broadcast_to(part, (1, LANES))


def _adamw(w, g, m, v):
    m = ADAM_B1 * m + (1.0 - ADAM_B1) * g
    v = ADAM_B2 * v + (1.0 - ADAM_B2) * jnp.square(g)
    m_hat = m / (1.0 - ADAM_B1 ** ADAM_STEP)
    v_hat = v / (1.0 - ADAM_B2 ** ADAM_STEP)
    delta = -ADAM_LR * (m_hat / (jnp.sqrt(v_hat) + ADAM_EPS) + ADAM_WD * w)
    return delta, m, v


def _sum_parts_fn(parts):
    g = parts[0].astype(F32)
    for j in range(1, parts.shape[0]):
        g = g + parts[j].astype(F32)
    return g


def _t5_bucket(dist):
    max_exact = N_REL_BUCKETS // 2
    d = np.maximum(dist, 1).astype(np.float32)
    scale = (N_REL_BUCKETS - max_exact) / math.log(REL_MAX_DIST / max_exact)
    large = max_exact + (np.log(d / max_exact) * scale).astype(np.int32)
    large = np.minimum(large, N_REL_BUCKETS - 1)
    return np.where(dist < max_exact, dist, large).astype(np.int32)


def _bucket_table(dilation):
    i = np.arange(ATT_BLOCK)[:, None]
    kk = np.arange(2 * ATT_BLOCK)[None, :]
    steps = ATT_BLOCK + i - kk
    return _t5_bucket(np.maximum(steps, 0) * dilation)


def _bias_fwd(rel_bias, buckets, g):
    def body(rel_ref, bk_ref, o_ref):
        bk = bk_ref[...]
        for h in range(HEADS_PER_GROUP):
            acc = jnp.zeros(bk.shape, F32)
            for b in range(N_REL_BUCKETS):
                acc = jnp.where(bk == b, rel_ref[b, g * HEADS_PER_GROUP + h], acc)
            o_ref[h] = acc

    return pl.pallas_call(
        body, in_specs=[pl.BlockSpec(memory_space=pltpu.SMEM), pl.BlockSpec(memory_space=pltpu.VMEM)],
        out_specs=pl.BlockSpec(memory_space=pltpu.VMEM),
        out_shape=jax.ShapeDtypeStruct((HEADS_PER_GROUP, ATT_BLOCK, 2 * ATT_BLOCK), F32),
        name=f"rel_bias_fwd{g}")(rel_bias, buckets)


def _bias_bwd(dbias, buckets, g):
    def body(db_ref, bk_ref, o_ref):
        bk = bk_ref[...]
        row = lax.broadcasted_iota(jnp.int32, (N_REL_BUCKETS, LANES), 0)
        col = lax.broadcasted_iota(jnp.int32, (N_REL_BUCKETS, LANES), 1)
        acc = jnp.zeros((N_REL_BUCKETS, LANES), F32)
        for h in range(HEADS_PER_GROUP):
            d = db_ref[h]
            for b in range(N_REL_BUCKETS):
                s = jnp.sum(jnp.sum(jnp.where(bk == b, d, 0.0), axis=1, keepdims=True), axis=0, keepdims=True)
                acc = acc + jnp.where((row == b) & (col == g * HEADS_PER_GROUP + h), s, 0.0)
        o_ref[...] = acc

    return pl.pallas_call(
        body, in_specs=[pl.BlockSpec(memory_space=pltpu.VMEM), pl.BlockSpec(memory_space=pltpu.VMEM)],
        out_specs=pl.BlockSpec(memory_space=pltpu.VMEM),
        out_shape=jax.ShapeDtypeStruct((N_REL_BUCKETS, LANES), F32), name=f"rel_bias_bwd{g}")(dbias, buckets)


_NT = (((1,), (1,)), ((), ()))
_TN = (((0,), (0,)), ((), ()))
_QKV_BLOCKS = 3 * QKV_WIDTH // WIDTH_A


def _band_mask(n_is_first):
    i = lax.broadcasted_iota(jnp.int32, (ATT_BLOCK, 2 * ATT_BLOCK), 0)
    kk = lax.broadcasted_iota(jnp.int32, (ATT_BLOCK, 2 * ATT_BLOCK), 1)
    return (kk >= i) & (kk <= i + ATT_STEPS) & ((kk >= ATT_BLOCK) | jnp.logical_not(n_is_first))


def _head(ref, h):
    return ref[:, h * HEAD_DIM:(h + 1) * HEAD_DIM]


def _attn_specs(g, d):
    blk = (ATT_BLOCK, WIDTH_A)
    q = pl.BlockSpec(blk, lambda c, n: (n, c * _QKV_BLOCKS + g))
    kp = pl.BlockSpec(blk, lambda c, n: (jnp.maximum(n - 1, 0), c * _QKV_BLOCKS + 3 + g))
    kc = pl.BlockSpec(blk, lambda c, n: (n, c * _QKV_BLOCKS + 3 + g))
    vp = pl.BlockSpec(blk, lambda c, n: (jnp.maximum(n - 1, 0), c * _QKV_BLOCKS + 6 + g))
    vc = pl.BlockSpec(blk, lambda c, n: (n, c * _QKV_BLOCKS + 6 + g))
    return [q, kp, kc, vp, vc]


def _attn_fwd(qkv, bias, g, d, comm):
    t = qkv.shape[0]
    lq = t // d
    nb = lq // ATT_BLOCK
    scale = HEAD_DIM ** -0.5

    def body(q_ref, kp_ref, kc_ref, vp_ref, vc_ref, b_ref, o_ref, l_ref):
        mask = _band_mask(pl.program_id(1) == 0)
        for h in range(HEADS_PER_GROUP):
            qh = _head(q_ref, h).astype(BF16)
            kh = jnp.concatenate([_head(kp_ref, h), _head(kc_ref, h)], axis=0).astype(BF16)
            vh = jnp.concatenate([_head(vp_ref, h), _head(vc_ref, h)], axis=0).astype(BF16)
            s = lax.dot_general(qh, kh, _NT, preferred_element_type=F32) * scale + b_ref[h]
            s = jnp.where(mask, s, NEG_INF)
            m = jnp.max(s, axis=1, keepdims=True)
            p = jnp.exp(s - m)
            den = jnp.sum(p, axis=1, keepdims=True)
            o = jnp.dot(p.astype(BF16), vh, preferred_element_type=F32) / den
            o_ref[:, h * HEAD_DIM:(h + 1) * HEAD_DIM] = o
            l_ref[:, h * HEAD_DIM:(h + 1) * HEAD_DIM] = jnp.broadcast_to(m + jnp.log(den), (ATT_BLOCK, HEAD_DIM))

    out_spec = pl.BlockSpec((ATT_BLOCK, WIDTH_A), lambda c, n: (n, c))
    (o, lse), ride_outs = _pallas(
        body, grid=(d, nb),
        in_specs=_attn_specs(g, d) + [pl.BlockSpec(bias.shape, _zero_map(3))],
        out_specs=[out_spec, out_spec],
        out_shape=[jax.ShapeDtypeStruct((lq, d * WIDTH_A), F32)] * 2,
        args=[*([qkv.reshape(lq, d * 3 * QKV_WIDTH)] * 5), bias],
        semantics=("parallel", "parallel"), rides=[comm.ride(f"attn_fwd{g}")], name=f"attn_fwd{g}")
    comm.took(ride_outs)
    return o.reshape(t, WIDTH_A), lse.reshape(t, WIDTH_A)


def _attn_bwd(qkv, bias, do, lse, corr, g, d):
    t = qkv.shape[0]
    lq = t // d
    nb = lq // ATT_BLOCK
    scale = HEAD_DIM ** -0.5

    def body(k_ref, v_ref, q0_ref, q1_ref, do0_ref, do1_ref, l0_ref, l1_ref, c0_ref, c1_ref, b_ref,
             dq_ref, dk_ref, dv_ref, db_ref, dq_prev):
        c, j = pl.program_id(0), pl.program_id(1)

        @pl.when((c == 0) & (j == 0))
        def _():
            db_ref[...] = jnp.zeros_like(db_ref)

        @pl.when(j == 0)
        def _():
            dq_prev[...] = jnp.zeros_like(dq_prev)

        i = lax.broadcasted_iota(jnp.int32, (ATT_BLOCK, ATT_BLOCK), 0)
        kk = lax.broadcasted_iota(jnp.int32, (ATT_BLOCK, ATT_BLOCK), 1)
        mask0 = kk <= i
        mask1 = (kk >= i) & (j + 1 < nb)
        for h in range(HEADS_PER_GROUP):
            kh = _head(k_ref, h).astype(BF16)
            vh = _head(v_ref, h).astype(BF16)
            cols = slice(h * HEAD_DIM, (h + 1) * HEAD_DIM)
            dk = jnp.zeros((ATT_BLOCK, HEAD_DIM), F32)
            dv = jnp.zeros((ATT_BLOCK, HEAD_DIM), F32)
            dq_parts = []
            parts = ((q0_ref, do0_ref, l0_ref, c0_ref, mask0, ATT_BLOCK), (q1_ref, do1_ref, l1_ref, c1_ref, mask1, 0))
            for q_ref, do_ref, l_ref, c_ref, mask, off in parts:
                qh = _head(q_ref, h).astype(BF16)
                doh = _head(do_ref, h).astype(BF16)
                s = lax.dot_general(qh, kh, _NT, preferred_element_type=F32) * scale + b_ref[h, :, off:off + ATT_BLOCK]
                s = jnp.where(mask, s, NEG_INF)
                p = jnp.exp(s - l_ref[:, h * HEAD_DIM:h * HEAD_DIM + 1])
                dp = lax.dot_general(doh, vh, _NT, preferred_element_type=F32)
                ds = p * (dp + c_ref[:, h * HEAD_DIM:h * HEAD_DIM + 1])
                dsb = ds.astype(BF16)
                dv = dv + lax.dot_general(p.astype(BF16), doh, _TN, preferred_element_type=F32)
                dk = dk + lax.dot_general(dsb, qh, _TN, preferred_element_type=F32)
                dq_parts.append(jnp.dot(dsb, kh, preferred_element_type=F32))
                db_ref[h, :, off:off + ATT_BLOCK] += ds
            dk_ref[:, cols] = dk * scale
            dv_ref[:, cols] = dv
            dq_ref[:, cols] = (dq_prev[:, cols] + dq_parts[0]) * scale
            dq_prev[:, cols] = dq_parts[1]

    blk = (ATT_BLOCK, WIDTH_A)
    nxt = lambda n: jnp.minimum(n + 1, nb - 1)
    k_spec = pl.BlockSpec(blk, lambda c, n: (n, c * _QKV_BLOCKS + 3 + g))
    v_spec = pl.BlockSpec(blk, lambda c, n: (n, c * _QKV_BLOCKS + 6 + g))
    q0_spec = pl.BlockSpec(blk, lambda c, n: (n, c * _QKV_BLOCKS + g))
    q1_spec = pl.BlockSpec(blk, lambda c, n: (nxt(n), c * _QKV_BLOCKS + g))
    r0 = pl.BlockSpec(blk, lambda c, n: (n, c))
    r1 = pl.BlockSpec(blk, lambda c, n: (nxt(n), c))
    view = lambda a: a.reshape(lq, d * WIDTH_A)
    qv = qkv.reshape(lq, d * 3 * QKV_WIDTH)
    dq, dk, dv, dbias = pl.pallas_call(
        body, grid=(d, nb),
        in_specs=[k_spec, v_spec, q0_spec, q1_spec, r0, r1, r0, r1, r0, r1, pl.BlockSpec(bias.shape, _zero_map(3))],
        out_specs=[r0, r0, r0, pl.BlockSpec(bias.shape, _zero_map(3))],
        out_shape=[jax.ShapeDtypeStruct((lq, d * WIDTH_A), F32)] * 3 + [jax.ShapeDtypeStruct(bias.shape, F32)],
        scratch_shapes=[pltpu.VMEM(blk, F32)],
        compiler_params=_cparams("arbitrary", "arbitrary"), name=f"attn_bwd{g}",
    )(qv, qv, qv, qv, view(do), view(do), view(lse), view(lse), view(corr), view(corr), bias)
    return dq.reshape(t, WIDTH_A), dk.reshape(t, WIDTH_A), dv.reshape(t, WIDTH_A), dbias


def _tril_mask():
    r = lax.broadcasted_iota(jnp.int32, (CHUNK, CHUNK), 0)
    c = lax.broadcasted_iota(jnp.int32, (CHUNK, CHUNK), 1)
    return c <= r


def _gmlp_fwd(zb, ln_g, ln_b, w_s, b_s_t):
    t = zb.shape[0]
    tr = _pick(t, 2 * CHUNK, CHUNK)

    def body(z_ref, g_ref, b_ref, ws_ref, bs_ref, o_ref):
        tri = _tril_mask()
        z = jax.nn.gelu(z_ref[...])
        u = z[:, :WIDTH_B]
        vn = _ln(z[:, WIDTH_B:], g_ref[...], b_ref[...])
        for ch in range(tr // CHUNK):
            rows = slice(ch * CHUNK, (ch + 1) * CHUNK)
            for gi in range(N_GROUPS_B):
                cols = slice(gi * CHUNK, (gi + 1) * CHUNK)
                w = jnp.where(tri, ws_ref[gi], 0.0).astype(BF16)
                mixed = jnp.dot(w, vn[rows, cols].astype(BF16), preferred_element_type=F32) + bs_ref[:, gi:gi + 1]
                o_ref[rows, cols] = (u[rows, cols] * mixed).astype(o_ref.dtype)

    return pl.pallas_call(
        body, grid=(t // tr,),
        in_specs=[pl.BlockSpec((tr, 2 * WIDTH_B), lambda i: (i, 0)), pl.BlockSpec(ln_g.shape, _zero_map(2)),
                  pl.BlockSpec(ln_b.shape, _zero_map(2)), pl.BlockSpec(w_s.shape, _zero_map(3)),
                  pl.BlockSpec(b_s_t.shape, _zero_map(2))],
        out_specs=pl.BlockSpec((tr, WIDTH_B), lambda i: (i, 0)),
        out_shape=jax.ShapeDtypeStruct((t, WIDTH_B), BF16),
        compiler_params=_cparams("parallel"), name="gmlp_fwd")(zb, ln_g, ln_b, w_s, b_s_t)


def _gmlp_bwd(zb, dyb, ln_g, ln_b, w_s, b_s_t, group_sel):
    t = zb.shape[0]
    tr = _pick(t, 2 * CHUNK, CHUNK)

    def body(z_ref, dy_ref, g_ref, b_ref, ws_ref, bs_ref, sel_ref, dz_ref, dzs_ref, dg_ref, db_ref, dws_ref, dbs_ref,
             du_s, dvn_s, dm_s):
        @pl.when(pl.program_id(0) == 0)
        def _():
            dzs_ref[...] = jnp.zeros_like(dzs_ref)
            dg_ref[...] = jnp.zeros_like(dg_ref)
            db_ref[...] = jnp.zeros_like(db_ref)
            dws_ref[...] = jnp.zeros_like(dws_ref)
            dbs_ref[...] = jnp.zeros_like(dbs_ref)

        tri = _tril_mask()
        z, gelu_vjp = jax.vjp(jax.nn.gelu, z_ref[...])
        u = z[:, :WIDTH_B]
        vn, ln_vjp = jax.vjp(_ln, z[:, WIDTH_B:], g_ref[...], b_ref[...])
        dy = dy_ref[...]
        for ch in range(tr // CHUNK):
            rows = slice(ch * CHUNK, (ch + 1) * CHUNK)
            for gi in range(N_GROUPS_B):
                cols = slice(gi * CHUNK, (gi + 1) * CHUNK)
                w = jnp.where(tri, ws_ref[gi], 0.0).astype(BF16)
                vg = vn[rows, cols].astype(BF16)
                mixed = jnp.dot(w, vg, preferred_element_type=F32) + bs_ref[:, gi:gi + 1]
                dyg = dy[rows, cols]
                dm = dyg * u[rows, cols]
                dmb = dm.astype(BF16)
                du_s[rows, cols] = dyg * mixed
                dm_s[rows, cols] = dm
                dvn_s[rows, cols] = lax.dot_general(w, dmb, _TN, preferred_element_type=F32)
                dws_ref[gi] += jnp.where(tri, lax.dot_general(dmb, vg, _NT, preferred_element_type=F32), 0.0)
            dbs_ref[...] += jnp.dot(dm_s[rows, :], sel_ref[...], precision=lax.Precision.HIGHEST,
                                    preferred_element_type=F32)
        dv, dg, db = ln_vjp(dvn_s[...])
        dg_ref[...] += dg
        db_ref[...] += db
        dz = gelu_vjp(jnp.concatenate([du_s[...], dv], axis=1))[0]
        dz_ref[...] = dz.astype(dz_ref.dtype)
        dzs_ref[...] += jnp.sum(dz, axis=0, keepdims=True)

    full = lambda a: pl.BlockSpec(a.shape, _zero_map(a.ndim))
    return pl.pallas_call(
        body, grid=(t // tr,),
        in_specs=[pl.BlockSpec((tr, 2 * WIDTH_B), lambda i: (i, 0)), pl.BlockSpec((tr, WIDTH_B), lambda i: (i, 0)),
                  full(ln_g), full(ln_b), full(w_s), full(b_s_t), full(group_sel)],
        out_specs=[pl.BlockSpec((tr, 2 * WIDTH_B), lambda i: (i, 0)), pl.BlockSpec((1, 2 * WIDTH_B), _zero_map(2)),
                   full(ln_g), full(ln_b), full(w_s), pl.BlockSpec((CHUNK, LANES), _zero_map(2))],
        out_shape=[jax.ShapeDtypeStruct((t, 2 * WIDTH_B), BF16), jax.ShapeDtypeStruct((1, 2 * WIDTH_B), F32),
                   jax.ShapeDtypeStruct(ln_g.shape, F32),
                   jax.ShapeDtypeStruct(ln_b.shape, F32), jax.ShapeDtypeStruct(w_s.shape, F32),
                   jax.ShapeDtypeStruct((CHUNK, LANES), F32)],
        scratch_shapes=[pltpu.VMEM((tr, WIDTH_B), F32)] * 3,
        compiler_params=_cparams("arbitrary"), name="gmlp_bwd")(zb, dyb, ln_g, ln_b, w_s, b_s_t, group_sel)


def _s5_disc(lr, li, ldt, br_t, bi_t):
    dt = jnp.exp(ldt)
    mag = jnp.exp(lr * dt)
    ab_re = mag * jnp.cos(li * dt)
    ab_im = mag * jnp.sin(li * dt)
    nrm = lr * lr + li * li
    cr = ((ab_re - 1.0) * lr + ab_im * li) / nrm
    ci = (ab_im * lr - (ab_re - 1.0) * li) / nrm
    return ab_re, ab_im, cr * br_t - ci * bi_t, cr * bi_t + ci * br_t


def _vmem_call(fn, args, out_shape, name):
    def body(*refs):
        res = fn(*[r[...] for r in refs[:len(args)]])
        for o, v in zip(refs[len(args):], res):
            o[...] = v

    vm = pl.BlockSpec(memory_space=pltpu.VMEM)
    return pl.pallas_call(body, in_specs=[vm] * len(args), out_specs=[vm] * len(out_shape),
                          out_shape=out_shape, name=name)(*args)


def _s5_disc_fwd(lr, li, ldt, br_t, bi_t):
    s1 = jax.ShapeDtypeStruct(lr.shape, F32)
    s2 = jax.ShapeDtypeStruct(br_t.shape, F32)
    return _vmem_call(_s5_disc, [lr, li, ldt, br_t, bi_t], [s1, s1, s2, s2], "s5_disc_fwd")


def _s5_disc_bwd(lr, li, ldt, br_t, bi_t, cts):
    def fn(lr, li, ldt, br_t, bi_t, d0, d1, d2, d3):
        _, vjp = jax.vjp(_s5_disc, lr, li, ldt, br_t, bi_t)
        return vjp((d0, d1, d2, d3))

    shp = [jax.ShapeDtypeStruct(a.shape, F32) for a in (lr, li, ldt, br_t, bi_t)]
    return _vmem_call(fn, [lr, li, ldt, br_t, bi_t, *cts], shp, "s5_disc_bwd")


_SCAN_ROWS = SSM_COLS // LANES
_SCAN_CHUNK = 256


def _scan_fwd(bre, bim, are, aim):
    t = bre.shape[0]
    tc = _pick(t, _SCAN_CHUNK, SUBLANES)

    def body(br_ref, bi_ref, ar_ref, ai_ref, xr_ref, xi_ref, st_ref):
        @pl.when(pl.program_id(0) == 0)
        def _():
            st_ref[...] = jnp.zeros_like(st_ref)

        ar, ai = ar_ref[...], ai_ref[...]

        def step(i, carry):
            xr, xi = carry
            nr = ar * xr - ai * xi + br_ref[i]
            ni = ar * xi + ai * xr + bi_ref[i]
            xr_ref[i] = nr
            xi_ref[i] = ni
            return nr, ni

        xr, xi = lax.fori_loop(0, tc, step, (st_ref[0], st_ref[1]), unroll=8)
        st_ref[0] = xr
        st_ref[1] = xi

    blk = pl.BlockSpec((tc, _SCAN_ROWS, LANES), lambda i: (i, 0, 0))
    par = pl.BlockSpec((_SCAN_ROWS, LANES), _zero_map(2))
    shp = jax.ShapeDtypeStruct(bre.shape, F32)
    return pl.pallas_call(
        body, grid=(t // tc,), in_specs=[blk, blk, par, par], out_specs=[blk, blk], out_shape=[shp, shp],
        scratch_shapes=[pltpu.VMEM((2, _SCAN_ROWS, LANES), F32)],
        compiler_params=_cparams("arbitrary"), name="s5_scan_fwd")(bre, bim, are, aim)


def _scan_bwd(dxr, dxi, xr, xi, are, aim):
    t = dxr.shape[0]
    tc = _pick(t, _SCAN_CHUNK, SUBLANES)
    nc = t // tc

    def body(dr_ref, di_ref, xr_ref, xi_ref, pr_ref, pi_ref, ar_ref, ai_ref, gr_ref, gi_ref, dar_ref, dai_ref, st_ref):
        step_id = pl.program_id(0)

        @pl.when(step_id == 0)
        def _():
            st_ref[...] = jnp.zeros_like(st_ref)
            dar_ref[...] = jnp.zeros_like(dar_ref)
            dai_ref[...] = jnp.zeros_like(dai_ref)

        ar, ai = ar_ref[...], ai_ref[...]

        def update(i, carry, pxr, pxi):
            gr, gi, dar, dai = carry
            ngr = dr_ref[i] + ar * gr + ai * gi
            ngi = di_ref[i] - ai * gr + ar * gi
            gr_ref[i] = ngr
            gi_ref[i] = ngi
            return ngr, ngi, dar + ngr * pxr + ngi * pxi, dai - ngr * pxi + ngi * pxr

        def step(s, carry):
            i = tc - 1 - s
            return update(i, carry, xr_ref[i - 1], xi_ref[i - 1])

        zero = jnp.zeros((_SCAN_ROWS, LANES), F32)
        carry = lax.fori_loop(0, tc - 1, step, (st_ref[0], st_ref[1], zero, zero), unroll=8)
        has_prev = (step_id < nc - 1).astype(F32)
        gr, gi, dar, dai = update(0, carry, pr_ref[0] * has_prev, pi_ref[0] * has_prev)
        st_ref[0] = gr
        st_ref[1] = gi
        dar_ref[...] += dar
        dai_ref[...] += dai

    blk = pl.BlockSpec((tc, _SCAN_ROWS, LANES), lambda i: (nc - 1 - i, 0, 0))
    prev = pl.BlockSpec((1, _SCAN_ROWS, LANES), lambda i: (jnp.maximum((nc - 1 - i) * tc - 1, 0), 0, 0))
    par = pl.BlockSpec((_SCAN_ROWS, LANES), _zero_map(2))
    shp = jax.ShapeDtypeStruct(dxr.shape, F32)
    psh = jax.ShapeDtypeStruct((_SCAN_ROWS, LANES), F32)
    return pl.pallas_call(
        body, grid=(nc,), in_specs=[blk, blk, blk, blk, prev, prev, par, par],
        out_specs=[blk, blk, par, par], out_shape=[shp, shp, psh, psh],
        scratch_shapes=[pltpu.VMEM((2, _SCAN_ROWS, LANES), F32)],
        compiler_params=_cparams("arbitrary"), name="s5_scan_bwd")(dxr, dxi, xr, xi, xr, xi, are, aim)


def _diag_blocks(a, b):
    return [(j, i, slice(i * a, (i + 1) * a), slice(i * b, (i + 1) * b))
            for j in range(N_SSM_BLOCKS) for i in range(SSM_PACK)]


def _block_diag(ms):
    _, a, b = ms[0].shape

    def body(*refs):
        for m_ref, o_ref in zip(refs[:len(ms)], refs[len(ms):]):
            o_ref[...] = jnp.zeros_like(o_ref)
            for j, i, rows, cols in _diag_blocks(a, b):
                o_ref[j, rows, cols] = m_ref[j * SSM_PACK + i]

    vm = pl.BlockSpec(memory_space=pltpu.VMEM)
    shape = jax.ShapeDtypeStruct((N_SSM_BLOCKS, SSM_PACK * a, SSM_PACK * b), F32)
    return pl.pallas_call(body, in_specs=[vm] * len(ms), out_specs=[vm] * len(ms), out_shape=[shape] * len(ms),
                          name="s5_block_diag")(*ms)


def _block_diag_extract(ms, a, b):
    def body(*refs):
        for m_ref, o_ref in zip(refs[:len(ms)], refs[len(ms):]):
            for j, i, rows, cols in _diag_blocks(a, b):
                o_ref[j * SSM_PACK + i] = m_ref[j, rows, cols]

    vm = pl.BlockSpec(memory_space=pltpu.VMEM)
    shape = jax.ShapeDtypeStruct((N_GROUPS_C, a, b), F32)
    return pl.pallas_call(body, in_specs=[vm] * len(ms), out_specs=[vm] * len(ms), out_shape=[shape] * len(ms),
                          name="s5_block_diag_extract")(*ms)


def _exchange(src, *, gather, name):
    shape = src.shape if gather else src.shape[1:]

    def body(src_ref, out_ref, send_sems, recv_sems, local_sem):
        x, y, c = lax.axis_index("x"), lax.axis_index("y"), lax.axis_index("c")
        me = 4 * x + 2 * y + c
        copies = []
        for r in range(1, N_DEV):
            px = 1 - x if r & 4 else x
            py = 1 - y if r & 2 else y
            pc = 1 - c if r & 1 else c
            piece = src_ref if gather else src_ref.at[4 * px + 2 * py + pc]
            cp = pltpu.make_async_remote_copy(
                src_ref=piece, dst_ref=out_ref.at[me], send_sem=send_sems.at[r - 1], recv_sem=recv_sems.at[r - 1],
                device_id=(px, py, pc), device_id_type=pl.DeviceIdType.MESH)
            cp.start()
            copies.append(cp)
        mine = pltpu.make_async_copy(src_ref if gather else src_ref.at[me], out_ref.at[me], local_sem)
        mine.start()
        for cp in copies:
            cp.wait()
        mine.wait()

    hbm = pl.BlockSpec(memory_space=pl.ANY)
    return pl.pallas_call(
        body, in_specs=[hbm], out_specs=hbm, out_shape=jax.ShapeDtypeStruct((N_DEV,) + tuple(shape), src.dtype),
        scratch_shapes=[pltpu.SemaphoreType.DMA((N_DEV - 1,)), pltpu.SemaphoreType.DMA((N_DEV - 1,)),
                        pltpu.SemaphoreType.DMA(())],
        name=name)(src)


def _mesh_place():
    x, y, c = lax.axis_index("x"), lax.axis_index("y"), lax.axis_index("c")
    other_chips = [(1 - x, y), (x, 1 - y), (1 - x, 1 - y)]
    return x, y, c, other_chips


def _gather_layer(srcs, layer, name):
    n = len(srcs)

    def body(*refs):
        src = [r.at[layer] for r in refs[:n]]
        out = refs[n:2 * n]
        send_sems, recv_sems, local_sems = refs[2 * n:]
        x, y, c, chips = _mesh_place()
        me, sibling = (x, y, c), (x, y, 1 - c)

        def copy(t, k, block, to, from_src=False):
            slot = 4 * block[0] + 2 * block[1] + block[2]
            return pltpu.make_async_remote_copy(
                src_ref=src[t] if from_src else out[t].at[slot], dst_ref=out[t].at[slot],
                send_sem=send_sems.at[t, k], recv_sem=recv_sems.at[t, k], device_id=to, device_id_type=_MESH_ID)

        mine = [pltpu.make_async_copy(src[t], out[t].at[4 * x + 2 * y + c], local_sems.at[t]) for t in range(n)]
        for cp in mine:
            cp.start()
        first = []
        for t in range(n):
            first.append(copy(t, 0, me, sibling, True))
            first += [copy(t, 1 + j, me, (*chip, c), True) for j, chip in enumerate(chips)]
        for cp in first:
            cp.start()
        passed = []
        for j, chip in enumerate(chips):
            for t in range(n):
                copy(t, 1 + j, (*chip, c), me).wait_recv()
                fwd = copy(t, 4 + j, (*chip, c), sibling)
                fwd.start()
                passed.append(fwd)
        for t in range(n):
            copy(t, 0, sibling, me).wait_recv()
            for j, chip in enumerate(chips):
                copy(t, 4 + j, (*chip, 1 - c), me).wait_recv()
        for cp in first + passed:
            cp.wait_send()
        for cp in mine:
            cp.wait()

    return pl.pallas_call(
        body, in_specs=[_HBM] * n, out_specs=[_HBM] * n,
        out_shape=[jax.ShapeDtypeStruct((N_DEV,) + s.shape[1:], s.dtype) for s in srcs],
        scratch_shapes=[pltpu.SemaphoreType.DMA((n, N_DEV - 1)), pltpu.SemaphoreType.DMA((n, N_DEV - 1)),
                        pltpu.SemaphoreType.DMA((n,))],
        name=name)(*srcs)


def _scatter_pair(srcs, name):
    n = len(srcs)

    def body(*refs):
        src, out = refs[:n], refs[n:2 * n]
        send_sems, recv_sems = refs[2 * n:]
        x, y, c, _ = _mesh_place()
        copies = [pltpu.make_async_remote_copy(
            src_ref=src[t].at[:, 1 - c], dst_ref=out[t], send_sem=send_sems.at[t], recv_sem=recv_sems.at[t],
            device_id=(x, y, 1 - c), device_id_type=_MESH_ID) for t in range(n)]
        for cp in copies:
            cp.start()
        for cp in copies:
            cp.wait()

    return pl.pallas_call(
        body, in_specs=[_HBM] * n, out_specs=[_HBM] * n,
        out_shape=[jax.ShapeDtypeStruct((s.shape[0],) + s.shape[2:], s.dtype) for s in srcs],
        scratch_shapes=[pltpu.SemaphoreType.DMA((n,)), pltpu.SemaphoreType.DMA((n,))], name=name)(*srcs)


def _pair_add(src, recv, name):
    nchip, _, r, cdim = src.shape
    tr = _pick(r, 256, 2 * SUBLANES)
    core = lax.axis_index("c").astype(jnp.int32).reshape(1)

    def body(core_ref, s_ref, r_ref, o_ref):
        o_ref[...] = (s_ref[...].astype(F32) + r_ref[...].astype(F32)).astype(o_ref.dtype)

    grid_spec = pltpu.PrefetchScalarGridSpec(
        num_scalar_prefetch=1, grid=(nchip, r // tr),
        in_specs=[pl.BlockSpec((None, None, tr, cdim), lambda k, i, core_ref: (k, core_ref[0], i, 0)),
                  pl.BlockSpec((None, tr, cdim), lambda k, i, core_ref: (k, i, 0))],
        out_specs=pl.BlockSpec((None, tr, cdim), lambda k, i, core_ref: (k, i, 0)))
    return pl.pallas_call(body, grid_spec=grid_spec, out_shape=jax.ShapeDtypeStruct(recv.shape, recv.dtype),
                          compiler_params=_cparams("parallel", "parallel"), name=name)(core, src, recv)


def _scatter_chips(srcs, name):
    n = len(srcs)

    def body(*refs):
        src, out = refs[:n], refs[n:2 * n]
        send_sems, recv_sems, local_sems = refs[2 * n:]
        x, y, c, chips = _mesh_place()
        my_chip = 2 * x + y
        mine = [pltpu.make_async_copy(src[t].at[my_chip], out[t].at[my_chip], local_sems.at[t]) for t in range(n)]
        copies = [pltpu.make_async_remote_copy(
            src_ref=src[t].at[2 * chip[0] + chip[1]], dst_ref=out[t].at[my_chip],
            send_sem=send_sems.at[t, j], recv_sem=recv_sems.at[t, j], device_id=(*chip, c), device_id_type=_MESH_ID)
            for t in range(n) for j, chip in enumerate(chips)]
        for cp in mine + copies:
            cp.start()
        for cp in copies + mine:
            cp.wait()

    return pl.pallas_call(
        body, in_specs=[_HBM] * n, out_specs=[_HBM] * n,
        out_shape=[jax.ShapeDtypeStruct(s.shape, s.dtype) for s in srcs],
        scratch_shapes=[pltpu.SemaphoreType.DMA((n, 3)), pltpu.SemaphoreType.DMA((n, 3)), pltpu.SemaphoreType.DMA((n,))],
        name=name)(*srcs)


def _chip_sums(grads, tag):
    views = [g.reshape(N_DEV // 2, 2, g.shape[0] // N_DEV, g.shape[1]) for g in grads]
    from_sibling = _scatter_pair(views, name="scatter_pair_" + tag)
    return [_pair_add(v, s, name="pair_add") for v, s in zip(views, from_sibling)]


def _sum_chips(parts):
    return _rowmap(_sum_parts_fn, [], stacks=[parts], row_outs=[(parts.shape[2], F32)], tr=128, name="sum_chips")[0]


def _buffer_roles(kinds, bufs):
    carried = [k for k in kinds if k in bufs]
    return carried, [k for k in kinds if k not in bufs]


def _gather_ride(sends, forwards, bufs):
    carried, created = _buffer_roles(list(dict.fromkeys([s[0] for s in sends] + [f[0] for f in forwards])), bufs)
    shape_of = {s[0]: jax.ShapeDtypeStruct((N_DEV,) + s[1].shape, s[1].dtype) for s in sends}
    order = carried + created

    def copies(in_refs, buf_refs, sems):
        x, y, c, chips = _mesh_place()
        buf = dict(zip(order, buf_refs))
        out, s0 = [], 0
        for (kind, _, r0, nr), src in zip(sends, in_refs):
            mine, dst = src.at[pl.ds(r0, nr)], buf[kind].at[4 * x + 2 * y + c, pl.ds(r0, nr)]
            out.append(pltpu.make_async_copy(mine, dst, sems.at[s0 + 8]))
            for k, peer in enumerate([(x, y, 1 - c)] + [(*chip, c) for chip in chips]):
                out.append(pltpu.make_async_remote_copy(src_ref=mine, dst_ref=dst, send_sem=sems.at[s0 + k],
                                                        recv_sem=sems.at[s0 + 4 + k], device_id=peer, device_id_type=_MESH_ID))
            s0 += 9
        for kind, r0, nr in forwards:
            for j, chip in enumerate(chips):
                blk = buf[kind].at[4 * chip[0] + 2 * chip[1] + c, pl.ds(r0, nr)]
                out.append(pltpu.make_async_remote_copy(src_ref=blk, dst_ref=blk, send_sem=sems.at[s0 + j],
                                                        recv_sem=sems.at[s0 + 3 + j], device_id=(x, y, 1 - c),
                                                        device_id_type=_MESH_ID))
            s0 += 6
        return out

    def start(in_refs, buf_refs, sems):
        for cp in copies(in_refs, buf_refs, sems):
            cp.start()

    def finish(in_refs, buf_refs, sems):
        for cp in copies(in_refs, buf_refs, sems):
            cp.wait()

    ride = _Ride(inputs=tuple(s[1] for s in sends), carried=tuple(bufs[k] for k in carried),
                 created=tuple(shape_of[k] for k in created), n_sems=9 * len(sends) + 6 * len(forwards),
                 start=start, finish=finish)
    return ride, order


def _scatter_ride(pieces, bufs):
    carried, created = _buffer_roles(list(dict.fromkeys(p[0] for p in pieces)), bufs)
    shape_of = {p[0]: jax.ShapeDtypeStruct(p[1].shape, p[1].dtype) for p in pieces}
    order = carried + created

    def copies(in_refs, buf_refs, sems):
        x, y, c, chips = _mesh_place()
        buf = dict(zip(order, buf_refs))
        out, s0 = [], 0
        for (kind, _, r0, nr), src in zip(pieces, in_refs):
            dst = buf[kind].at[2 * x + y, pl.ds(r0, nr)]
            out.append(pltpu.make_async_copy(src.at[2 * x + y, pl.ds(r0, nr)], dst, sems.at[s0 + 6]))
            for j, chip in enumerate(chips):
                out.append(pltpu.make_async_remote_copy(
                    src_ref=src.at[2 * chip[0] + chip[1], pl.ds(r0, nr)], dst_ref=dst, send_sem=sems.at[s0 + j],
                    recv_sem=sems.at[s0 + 3 + j], device_id=(*chip, c), device_id_type=_MESH_ID))
            s0 += 7
        return out

    def start(in_refs, buf_refs, sems):
        for cp in copies(in_refs, buf_refs, sems):
            cp.start()

    def finish(in_refs, buf_refs, sems):
        for cp in copies(in_refs, buf_refs, sems):
            cp.wait()

    ride = _Ride(inputs=tuple(p[1] for p in pieces), carried=tuple(bufs[k] for k in carried),
                 created=tuple(shape_of[k] for k in created), n_sems=7 * len(pieces), start=start, finish=finish)
    return ride, order


GATHER_PLAN = (
    ("mm_in_qkv", (("w_in", 0, 3),)),
    ("mm_in_zb", (("w_pa", 0, 1), ("w_pb", 0, 1), ("w_pc", 0, 1), ("w_glu", 0, 1))),
    ("mm_in_gl", (("w_in", 1, 3),)),
    ("attn_fwd0", (("w_ffn_out", 0, 2),)),
    ("attn_fwd1", (("w_ffn_out", 1, 2),)),
    ("mm_o", (("w_o", 0, 1),)),
    ("mm_ffn_in", (("w_in", 2, 3), ("w_ffn_in", 0, 2))),
    ("mm_ffn_out", (("w_ffn_in", 1, 2),)),
    ("norm2", ()),
)
EARLY_KINDS = ("w_ffn_out", "w_ffn_in")
LATE_KINDS = tuple(n for n in SHARDED if n not in EARLY_KINDS)
SCATTER_EARLY_PLAN = (
    ("mm_in_dw_qkv", (("w_ffn_out", 0, 2),)),
    ("mm_in_dx_qkv", (("w_ffn_out", 1, 2),)),
    ("mm_in_dx_zb", (("w_ffn_in", 3, 4),)),
    ("mm_in_dw_gl", (("w_ffn_in", 0, 4),)),
    ("mm_in_dx_gl", (("w_ffn_in", 1, 4), ("w_ffn_in", 2, 4))),
)
SCATTER_PLAN = (
    ("mm_ffn_out_dw", (("w_in", 0, 3),)),
    ("mm_ffn_out_dx", (("w_in", 1, 3),)),
    ("mm_ffn_in_dw", (("w_in", 2, 3), ("w_o", 0, 1), ("w_pa", 0, 1), ("w_pb", 0, 1), ("w_pc", 0, 1), ("w_glu", 0, 1))),
)


def _row_part(rows, part, parts):
    assert rows % (parts * 2 * SUBLANES) == 0
    return part * (rows // parts), rows // parts


class _Carried:
    def __init__(self, plan, blocks, make_ride, forwards_too):
        self.plan, self.blocks, self.make_ride, self.forwards_too = dict(plan), blocks, make_ride, forwards_too
        self.bufs, self.to_forward, self.order = {}, [], []

    def ride(self, host):
        if self.blocks is None or host not in self.plan:
            self.order = []
            return None
        sends = [(k, self.blocks[k], *_row_part(self.blocks[k].shape[-2], part, parts)) for k, part, parts in self.plan[host]]
        if self.forwards_too:
            ride, self.order = self.make_ride(sends, self.to_forward, self.bufs)
            self.to_forward = [(k, r0, nr) for k, _, r0, nr in sends]
        else:
            ride, self.order = self.make_ride(sends, self.bufs)
        return ride

    def took(self, ride_outs):
        for k, buf in zip(self.order, ride_outs[0] if ride_outs else []):
            self.bufs[k] = buf


def _hosted(comm, fn, *args, name, **kwargs):
    res, ride_outs = fn(*args, name=name, rides=[comm.ride(name)], **kwargs)
    comm.took(ride_outs)
    return res


def _small_sizes(shapes):
    return [int(np.prod(shapes[n])) for n in SMALL]


def _pack_small(vals):
    flat = jnp.concatenate([vals[n].reshape(-1).astype(F32) for n in SMALL])
    rows = -(-flat.shape[0] // (LANES * N_DEV * SUBLANES)) * (N_DEV * SUBLANES)
    return jnp.pad(flat, (0, rows * LANES - flat.shape[0])).reshape(rows, LANES)


def _unpack_small(packed, shapes):
    flat = packed.reshape(-1)
    out, off = {}, 0
    for n, size in zip(SMALL, _small_sizes(shapes)):
        out[n] = flat[off:off + size].reshape(shapes[n])
        off += size
    return out


def _row(v):
    return v.reshape(1, -1)


def _layer_params(l, full, small):
    o1, o2, o3 = 3 * QKV_WIDTH, 3 * QKV_WIDTH + 2 * WIDTH_B, 3 * QKV_WIDTH + 2 * WIDTH_B + WIDTH_C
    b_in = small["b_in"][l]
    p = {
        "in_pieces": (("qkv", 0, o1), ("zb", o1, o2 - o1), ("uc", o2, o3 - o2), ("gl", o3, b_in.shape[0] - o3)),
        "b_qkv": _row(b_in[:o1]), "b_zb": _row(b_in[o1:o2]), "b_uc": _row(b_in[o2:o3]), "b_gl": _row(b_in[o3:]),
        "sgu_ln_g": _row(small["sgu_ln_g"][l]), "sgu_ln_b": _row(small["sgu_ln_b"][l]),
        "w_s": small["w_s"][l], "b_s_t": small["b_s"][l].T,
        "lam_re": small["lam_re"][l][:, None, :], "lam_im": small["lam_im"][l][:, None, :],
        "log_dt": small["log_dt"][l][:, None, None],
        "b_re_t": small["b_re"][l].transpose(0, 2, 1), "b_im_t": small["b_im"][l].transpose(0, 2, 1),
        "c_re": small["c_re"][l], "c_im": small["c_im"][l],
        "d_skip": _row(small["d_skip"][l]), "b_glu": _row(small["b_glu"][l]),
        "ln1_g": _row(small["ln1_g"][l]), "ln1_b": _row(small["ln1_b"][l]),
        "ln2_g": _row(small["ln2_g"][l]), "ln2_b": _row(small["ln2_b"][l]),
    }
    for n in SHARDED:
        p[n] = full[n]
    return p


def _scan_view(a):
    return a.reshape(a.shape[0], _SCAN_ROWS, LANES)


def _twice(fn):
    def both(*args):
        y = fn(*args)
        return y, y
    return both


def _layer_fwd(x, xb, p, biases, comm):
    t, d = x.shape
    r = {"x": x, "xb": xb}
    for piece, off, n in p["in_pieces"]:
        r[piece] = _hosted(comm, _mm, xb, p["w_in"], tb=True, b_off=off, n=n, bias=p["b_" + piece],
                           out_dtype=BF16 if piece == "qkv" else F32, name="mm_in_" + piece)
    ol = []
    for g, dil in enumerate(ATT_DILATIONS):
        ol += list(_attn_fwd(r["qkv"], biases[g], g, dil, comm))
    r["ol"] = ol
    r["ya"], r["ya_b"] = _rowmap(_twice(_combine), ol, row_outs=[(WIDTH_A, F32), (WIDTH_A, BF16)], tr=512,
                                 name="attn_combine")
    r["yb"] = _gmlp_fwd(r["zb"], p["sgu_ln_g"], p["sgu_ln_b"], p["w_s"], p["b_s_t"])
    ab_re, ab_im, bb_re_t, bb_im_t = _s5_disc_fwd(p["lam_re"], p["lam_im"], p["log_dt"], p["b_re_t"], p["b_im_t"])
    r["a_re"], r["a_im"] = ab_re.reshape(_SCAN_ROWS, LANES), ab_im.reshape(_SCAN_ROWS, LANES)
    r["bmat_re"], r["bmat_im"], r["cmat_re"], r["cmat_im"] = _block_diag([bb_re_t, bb_im_t, p["c_re"], p["c_im"]])
    bu_re = _bdmm(r["uc"], r["bmat_re"], name="s5_in_re")
    bu_im = _bdmm(r["uc"], r["bmat_im"], name="s5_in_im")
    xr, xi = _scan_fwd(_scan_view(bu_re), _scan_view(bu_im), r["a_re"], r["a_im"])
    r["xr"], r["xi"] = xr.reshape(t, SSM_COLS), xi.reshape(t, SSM_COLS)
    y_re = _bdmm(r["xr"], r["cmat_re"], tb=True, name="s5_out_re")
    y_im = _bdmm(r["xi"], r["cmat_im"], tb=True, scale=-1.0, name="s5_out_im")
    r["ys"], r["ycp"] = _rowmap(_s5_out, [y_re, y_im, r["uc"]], consts=[p["d_skip"]],
                                row_outs=[(WIDTH_C, F32)] * 2, tr=512, name="s5_out_act")
    r["glin"] = _mm(r["ycp"], p["w_glu"], bias=p["b_glu"], name="mm_glu")
    r["yc"] = _rowmap(_glu, [r["ycp"], r["glin"]], row_outs=[(WIDTH_C, BF16)], tr=512, name="glu")[0]
    r["pa"] = _mm(r["ya_b"], p["w_pa"], tb=True, name="mm_pa")
    r["pb"] = _mm(r["yb"], p["w_pb"], tb=True, name="mm_pb")
    r["pc"] = _mm(r["yc"], p["w_pc"], tb=True, name="mm_pc")
    r["merged"] = _rowmap(_merge, [r["gl"], r["pa"], r["pb"], r["pc"]], row_outs=[(d, BF16)], name="merge")[0]
    r["mo"] = _hosted(comm, _mm, r["merged"], p["w_o"], name="mm_o")
    r["xm"], r["xm_b"] = _rowmap(_twice(_post_norm), [x, r["mo"]], consts=[p["ln1_g"], p["ln1_b"]],
                                 row_outs=[(d, F32), (d, BF16)], name="norm1")
    r["gate"], r["up"], r["act"] = _hosted(comm, _mm_swiglu, r["xm_b"], p["w_ffn_in"], name="mm_ffn_in")
    r["f"] = _hosted(comm, _mm, r["act"], p["w_ffn_out"], name="mm_ffn_out")
    out, out_b = _hosted(comm, _rowmap, _twice(_post_norm), [r["xm"], r["f"]], consts=[p["ln2_g"], p["ln2_b"]],
                         row_outs=[(d, F32), (d, BF16)], name="norm2")
    return out, out_b, r


def _layer_bwd(dout, r, p, biases, consts, comm, make_early):
    t, d = dout.shape
    gw, gs = {}, {}
    ffw = 2 * r["gate"].shape[1]
    dxm, df, gs["ln2_g"], gs["ln2_b"] = _rowmap(
        _post_norm_bwd, [r["xm"], r["f"], dout], consts=[p["ln2_g"], p["ln2_b"]],
        row_outs=[(d, F32), (d, BF16)], red_outs=[(1, d)] * 2, name="norm2_bwd")
    gw["w_ffn_out"] = _hosted(comm, _mm, r["act"], df, ta=True, out_dtype=BF16, name="mm_ffn_out_dw")
    dact = _hosted(comm, _mm, df, p["w_ffn_out"], tb=True, name="mm_ffn_out_dx")
    dgu = _rowmap(_swiglu_bwd, [r["gate"], r["up"], dact], row_outs=[(ffw, BF16)], tr=128, name="swiglu_bwd")[0]
    gw["w_ffn_in"] = _hosted(comm, _mm, dgu, r["xm_b"], ta=True, out_dtype=BF16, name="mm_ffn_in_dw")
    dxm = _hosted(comm, _mm, dgu, p["w_ffn_in"], add=dxm, name="mm_ffn_in_dx")
    early = make_early({n: gw[n] for n in EARLY_KINDS})
    dx, dmo, gs["ln1_g"], gs["ln1_b"] = _rowmap(
        _post_norm_bwd, [r["x"], r["mo"], dxm], consts=[p["ln1_g"], p["ln1_b"]],
        row_outs=[(d, F32), (d, BF16)], red_outs=[(1, d)] * 2, name="norm1_bwd")
    gw["w_o"] = _hosted(comm, _mm, r["merged"], dmo, ta=True, out_dtype=BF16, name="mm_o_dw")
    dmerged = _hosted(comm, _mm, dmo, p["w_o"], tb=True, name="mm_o_dx")
    dgl, dpa, dpb, dpc, db_gl = _rowmap(
        _merge_bwd, [r["gl"], r["pa"], r["pb"], r["pc"], dmerged],
        row_outs=[(3 * d, BF16), (d, BF16), (d, BF16), (d, BF16)], red_outs=[(1, 3 * d)], tr=128, name="merge_bwd")
    gw["w_pa"] = _mm(dpa, r["ya_b"], ta=True, out_dtype=BF16, name="mm_pa_dw")
    gw["w_pb"] = _mm(dpb, r["yb"], ta=True, out_dtype=BF16, name="mm_pb_dw")
    gw["w_pc"] = _mm(dpc, r["yc"], ta=True, out_dtype=BF16, name="mm_pc_dw")
    dya = _mm(dpa, p["w_pa"], name="mm_pa_dx")
    dyb = _mm(dpb, p["w_pb"], name="mm_pb_dx")
    dyc = _mm(dpc, p["w_pc"], name="mm_pc_dx")
    dycp, dglin, gs["b_glu"] = _rowmap(_glu_bwd, [r["ycp"], r["glin"], dyc], row_outs=[(WIDTH_C, F32), (WIDTH_C, BF16)],
                                       red_outs=[(1, WIDTH_C)], tr=512, name="glu_bwd")
    gw["w_glu"] = _mm(r["ycp"], dglin, ta=True, out_dtype=BF16, name="mm_glu_dw")
    dycp = _mm(dglin, p["w_glu"], tb=True, add=dycp, name="mm_glu_dx")
    dys, duc, gs["d_skip"] = _rowmap(_s5_out_bwd, [r["ys"], r["uc"], dycp], consts=[p["d_skip"]],
                                     row_outs=[(WIDTH_C, F32)] * 2, red_outs=[(1, WIDTH_C)], tr=512, name="s5_out_act_bwd")
    dxr = _bdmm(dys, r["cmat_re"], name="s5_out_re_dx")
    dxi = _bdmm(dys, r["cmat_im"], scale=-1.0, name="s5_out_im_dx")
    d_cmat_re = _bdmm_tn(dys, r["xr"], N_SSM_BLOCKS, name="s5_out_re_dw")
    d_cmat_im = _bdmm_tn(dys, r["xi"], N_SSM_BLOCKS, scale=-1.0, name="s5_out_im_dw")
    g_re, g_im, da_re, da_im = _scan_bwd(_scan_view(dxr), _scan_view(dxi), _scan_view(r["xr"]), _scan_view(r["xi"]),
                                         r["a_re"], r["a_im"])
    g_re, g_im = g_re.reshape(t, SSM_COLS), g_im.reshape(t, SSM_COLS)
    duc = _bdmm(g_re, r["bmat_re"], tb=True, add=duc, name="s5_in_re_dx")
    duc = _bdmm(g_im, r["bmat_im"], tb=True, add=duc, name="s5_in_im_dx")
    d_bmat_re = _bdmm_tn(r["uc"], g_re, N_SSM_BLOCKS, name="s5_in_re_dw")
    d_bmat_im = _bdmm_tn(r["uc"], g_im, N_SSM_BLOCKS, name="s5_in_im_dw")
    d_bb_re_t, d_bb_im_t, gs["c_re"], gs["c_im"] = _block_diag_extract(
        [d_bmat_re, d_bmat_im, d_cmat_re, d_cmat_im], SSM_GROUP, SSM_STATE)
    cts = (da_re.reshape(N_GROUPS_C, 1, SSM_STATE), da_im.reshape(N_GROUPS_C, 1, SSM_STATE), d_bb_re_t, d_bb_im_t)
    d_lr, d_li, d_ldt, d_br_t, d_bi_t = _s5_disc_bwd(p["lam_re"], p["lam_im"], p["log_dt"], p["b_re_t"], p["b_im_t"], cts)
    gs["lam_re"], gs["lam_im"], gs["log_dt"] = d_lr[:, 0, :], d_li[:, 0, :], d_ldt[:, 0, 0]
    gs["b_re"], gs["b_im"] = d_br_t.transpose(0, 2, 1), d_bi_t.transpose(0, 2, 1)
    dzb, db_zb, gs["sgu_ln_g"], gs["sgu_ln_b"], gs["w_s"], dbs_t = _gmlp_bwd(
        r["zb"], dyb, p["sgu_ln_g"], p["sgu_ln_b"], p["w_s"], p["b_s_t"], consts["group_sel"])
    gs["b_s"] = dbs_t[:, :N_GROUPS_B].T
    do_corr = _rowmap(_combine_bwd, r["ol"] + [r["ya"], dya], consts=[consts["head_ones"]],
                      row_outs=[(WIDTH_A, F32)] * 6, tr=512, name="attn_combine_bwd")
    dq, dk, dv, dbias = [], [], [], []
    for g, dil in enumerate(ATT_DILATIONS):
        do_g, corr_g, lse_g = do_corr[g], do_corr[3 + g], r["ol"][2 * g + 1]
        dq_g, dk_g, dv_g, db_g = _attn_bwd(r["qkv"], biases[g], do_g, lse_g, corr_g, g, dil)
        dq.append(dq_g)
        dk.append(dk_g)
        dv.append(dv_g)
        dbias.append(db_g)
    cast_colsum = lambda a: (a, jnp.sum(a, axis=0, keepdims=True))
    dqkv, db_qkv = _rowmap(cast_colsum, [jnp.concatenate(dq + dk + dv, axis=1)], row_outs=[(3 * QKV_WIDTH, BF16)],
                           red_outs=[(1, 3 * QKV_WIDTH)], tr=512, name="cast_colsum_qkv")
    duc, db_uc = _rowmap(cast_colsum, [duc], row_outs=[(WIDTH_C, BF16)], red_outs=[(1, WIDTH_C)], tr=512,
                         name="cast_colsum_uc")
    dpieces = {"qkv": dqkv, "zb": dzb, "uc": duc, "gl": dgl}
    rows_in = p["w_in"].shape[0]
    dw_in = None
    for piece, off, n in p["in_pieces"]:
        dw_in = _hosted(early, _mm, dpieces[piece], r["xb"], ta=True, out_dtype=BF16, into=(rows_in, off, dw_in),
                        name="mm_in_dw_" + piece)
        dx = _hosted(early, _mm, dpieces[piece], p["w_in"], b_off=off, add=dx, name="mm_in_dx_" + piece)
    gw["w_in"] = dw_in
    gs["b_in"] = jnp.concatenate([db_qkv, db_zb, db_uc, db_gl], axis=1)[0]
    for n in ("sgu_ln_g", "sgu_ln_b", "d_skip", "b_glu", "ln1_g", "ln1_b", "ln2_g", "ln2_b"):
        gs[n] = gs[n][0]
    return dx, gw, gs, dbias, early.bufs


def _cast_bf16(w):
    w2 = w.reshape(-1, w.shape[-1])
    out = _rowmap(lambda a: a, [w2], row_outs=[(w2.shape[1], BF16)], tr=512, name="cast_bf16")[0]
    return out.reshape(w.shape)


def _static_consts():
    head_ones = np.kron(np.eye(HEADS_PER_GROUP, dtype=np.float32), np.ones((HEAD_DIM, HEAD_DIM), np.float32))
    group_sel = np.zeros((WIDTH_B, LANES), np.float32)
    group_sel[np.arange(WIDTH_B), np.arange(WIDTH_B) // CHUNK] = 1.0
    return {"head_ones": jnp.asarray(head_ones), "group_sel": jnp.asarray(group_sel)}


def _step(x, tgt, w, m, v):
    shapes = {n: w[n].shape for n in WEIGHTS}
    consts = _static_consts()
    mine_bf = {n: _cast_bf16(w[n].transpose(0, 2, 1) if n in TRANSPOSED else w[n]) for n in SHARDED}
    small = {n: w[n] for n in SMALL}
    buckets = [jnp.asarray(_bucket_table(dil)) for dil in ATT_DILATIONS]
    biases = [_bias_fwd(w["rel_bias"], buckets[g], g) for g in range(len(ATT_DILATIONS))]
    params, saved = [], []
    h, hb = _rowmap(_twice(lambda a: a), [x], row_outs=[(x.shape[1], F32), (x.shape[1], BF16)], name="cast_x")
    gathered = dict(zip(SHARDED, _gather_layer([mine_bf[n] for n in SHARDED], 0, name="gather_layer0")))
    for l in range(DEPTH):
        p = _layer_params(l, {n: g.reshape(-1, g.shape[2]) for n, g in gathered.items()}, small)
        ahead = _Carried(GATHER_PLAN, {n: mine_bf[n][l + 1] for n in SHARDED} if l + 1 < DEPTH else None,
                         _gather_ride, forwards_too=True)
        h, hb, r = _layer_fwd(h, hb, p, biases, ahead)
        gathered = ahead.bufs
        params.append(p)
        saved.append(r)
    dy, loss_part = _rowmap(_loss_fn, [h, tgt], row_outs=[(h.shape[1], F32)], red_outs=[(1, LANES)], name="loss")
    loss = lax.psum(loss_part[0, 0], MESH_AXES)
    g_mine, gs_layers = {n: [None] * DEPTH for n in SHARDED}, [None] * DEPTH
    dbias_sum = None
    def scatter_steps(plan, grads, tag):
        sums = _chip_sums(list(grads.values()), tag)
        return _Carried(plan, dict(zip(grads, sums)), _scatter_ride, forwards_too=False)

    behind = _Carried(SCATTER_PLAN, None, _scatter_ride, forwards_too=False)
    for l in reversed(range(DEPTH)):
        dy, gw, gs_layers[l], dbias, early_parts = _layer_bwd(
            dy, saved[l], params[l], biases, consts, behind,
            functools.partial(scatter_steps, SCATTER_EARLY_PLAN, tag="early"))
        saved[l] = None
        for n in EARLY_KINDS:
            g_mine[n][l] = _sum_chips(early_parts[n])
        if behind.blocks is not None:
            for n in LATE_KINDS:
                g_mine[n][l + 1] = _sum_chips(behind.bufs[n])
        behind = scatter_steps(SCATTER_PLAN, {n: gw[n] for n in LATE_KINDS}, "late")
        if l == 0:
            last = _scatter_chips([behind.blocks[n] for n in LATE_KINDS], name="scatter_chips_layer0")
            for n, parts in zip(LATE_KINDS, last):
                g_mine[n][0] = _sum_chips(parts)
        if dbias_sum is None:
            dbias_sum = dbias
        else:
            dbias_sum = [_rowmap(lambda a, b: a + b, [a.reshape(-1, 2 * ATT_BLOCK), b.reshape(-1, 2 * ATT_BLOCK)],
                                 row_outs=[(2 * ATT_BLOCK, F32)], name="dbias_add")[0].reshape(a.shape)
                         for a, b in zip(dbias_sum, dbias)]
    drel = [_bias_bwd(dbias_sum[g], buckets[g], g) for g in range(len(ATT_DILATIONS))]
    drel = _rowmap(lambda a, b, c: a + b + c, drel, row_outs=[(LANES, F32)], name="drel_add")[0]
    grad_small_local = {n: jnp.stack([gs_layers[l][n] for l in range(DEPTH)]) for n in SMALL if n != "rel_bias"}
    grad_small_local["rel_bias"] = drel[:, :shapes["rel_bias"][1]]
    out_g, out_d, out_m, out_v = {}, {}, {}, {}
    for n in SHARDED:
        g = jnp.stack(g_mine[n])
        out_g[n] = g.transpose(0, 2, 1) if n in TRANSPOSED else g
        cols = shapes[n][-1]
        res = _rowmap(_adamw, [a.reshape(-1, cols) for a in (w[n], out_g[n], m[n], v[n])],
                      row_outs=[(cols, F32)] * 3, tr=128, name="adamw_" + n)
        out_d[n], out_m[n], out_v[n] = [a.reshape(shapes[n]) for a in res]
    packed = _pack_small(grad_small_local)
    rows = packed.shape[0] // N_DEV
    parts = _exchange(packed.reshape(N_DEV, rows, LANES), gather=False, name="scatter_small")
    mine = _rowmap(_sum_parts_fn, [], stacks=[parts], row_outs=[(LANES, F32)], name="sum_small")[0]
    g_small = _exchange(mine, gather=True, name="gather_small").reshape(-1, LANES)
    res = _rowmap(lambda w_, g_, m_, v_: _adamw(w_, g_, m_, v_),
                  [_pack_small(w), g_small, _pack_small(m), _pack_small(v)],
                  row_outs=[(LANES, F32)] * 3, name="adamw_small")
    small_shapes = {n: shapes[n] for n in SMALL}
    out_g.update(_unpack_small(g_small, small_shapes))
    for dst, packed_res in zip((out_d, out_m, out_v), res):
        dst.update(_unpack_small(packed_res, small_shapes))
    return loss, dy, out_g, out_d, out_m, out_v


def kernel(x, w_in, b_in, rel_bias, sgu_ln_g, sgu_ln_b, w_s, b_s, lam_re, lam_im, log_dt, b_re, b_im, c_re, c_im, d_skip, w_glu, b_glu, w_pa, w_pb, w_pc, w_o, ln1_g, ln1_b, w_ffn_in, w_ffn_out, ln2_g, ln2_b, loss_target, m_w_in, m_b_in, m_rel_bias, m_sgu_ln_g, m_sgu_ln_b, m_w_s, m_b_s, m_lam_re, m_lam_im, m_log_dt, m_b_re, m_b_im, m_c_re, m_c_im, m_d_skip, m_w_glu, m_b_glu, m_w_pa, m_w_pb, m_w_pc, m_w_o, m_ln1_g, m_ln1_b, m_w_ffn_in, m_w_ffn_out, m_ln2_g, m_ln2_b, v_w_in, v_b_in, v_rel_bias, v_sgu_ln_g, v_sgu_ln_b, v_w_s, v_b_s, v_lam_re, v_lam_im, v_log_dt, v_b_re, v_b_im, v_c_re, v_c_im, v_d_skip, v_w_glu, v_b_glu, v_w_pa, v_w_pb, v_w_pc, v_w_o, v_ln1_g, v_ln1_b, v_w_ffn_in, v_w_ffn_out, v_ln2_g, v_ln2_b):
    args = dict(locals())
    w = {n: args[n] for n in WEIGHTS}
    m = {n: args["m_" + n] for n in WEIGHTS}
    v = {n: args["v_" + n] for n in WEIGHTS}
    loss, dx, g, d, nm, nv = _step(x[0], loss_target[0], w, m, v)
    return (loss, dx[None], *[g[n] for n in WEIGHTS], *[d[n] for n in WEIGHTS],
            *[nm[n] for n in WEIGHTS], *[nv[n] for n in WEIGHTS])
```

```python
import functools
import math
from typing import Callable, NamedTuple

import numpy as np
import jax
import jax.numpy as jnp
from jax import lax
from jax.experimental import pallas as pl
from jax.experimental.pallas import tpu as pltpu

F32 = jnp.float32
BF16 = jnp.bfloat16

MESH_AXES = ("x", "y", "c")
N_DEV = 8
DEPTH = 4

ATT_DILATIONS = (1, 4, 16)
ATT_STEPS = 128
HEADS_PER_GROUP = 8
HEAD_DIM = 64
QKV_WIDTH = 1536
WIDTH_A = HEADS_PER_GROUP * HEAD_DIM
ATT_BLOCK = 128
N_REL_BUCKETS = 32
REL_MAX_DIST = 2048
NEG_INF = -1e30
CHUNK = 128
WIDTH_B = 768
N_GROUPS_B = 6
WIDTH_C = 768
SSM_GROUP = 16
N_GROUPS_C = 48
SSM_STATE = 64
SSM_PACK = 8
N_SSM_BLOCKS = N_GROUPS_C // SSM_PACK
SSM_COLS = N_GROUPS_C * SSM_STATE
ALPHA = (2 * DEPTH) ** 0.25

ADAM_LR = 0.001
ADAM_B1 = 0.9
ADAM_B2 = 0.999
ADAM_EPS = 1e-08
ADAM_WD = 0.01
ADAM_STEP = 10

LANES = 128
SUBLANES = 8
VMEM_LIMIT = 48 * 1024 * 1024

SHARDED = ("w_in", "w_glu", "w_pa", "w_pb", "w_pc", "w_o", "w_ffn_in", "w_ffn_out")
TRANSPOSED = ("w_in", "w_pa", "w_pb", "w_pc", "w_ffn_in")
SMALL = ("b_in", "rel_bias", "sgu_ln_g", "sgu_ln_b", "w_s", "b_s", "lam_re", "lam_im", "log_dt",
         "b_re", "b_im", "c_re", "c_im", "d_skip", "b_glu", "ln1_g", "ln1_b", "ln2_g", "ln2_b")
WEIGHTS = ("w_in", "b_in", "rel_bias", "sgu_ln_g", "sgu_ln_b", "w_s", "b_s", "lam_re", "lam_im",
           "log_dt", "b_re", "b_im", "c_re", "c_im", "d_skip", "w_glu", "b_glu", "w_pa", "w_pb",
           "w_pc", "w_o", "ln1_g", "ln1_b", "w_ffn_in", "w_ffn_out", "ln2_g", "ln2_b")


def _pick(dim, target, mult):
    best = None
    for t in range(mult, min(dim, target) + 1, mult):
        if dim % t == 0:
            best = t
    return dim if best is None else best


def _cparams(*sem):
    return pltpu.CompilerParams(dimension_semantics=sem, vmem_limit_bytes=VMEM_LIMIT)


def _zero_map(ndim):
    return lambda *_: (0,) * ndim


_HBM = pl.BlockSpec(memory_space=pl.ANY)
_MESH_ID = pl.DeviceIdType.MESH


class _Ride(NamedTuple):
    inputs: tuple
    carried: tuple
    created: tuple
    n_sems: int
    start: Callable
    finish: Callable


def _pallas(body, *, grid, in_specs, out_specs, out_shape, args, scratch=(), semantics, rides=(), aliases=None, name):
    rides = [r for r in rides if r is not None]
    n_in, n_out, n_scr = len(args), len(out_shape), len(scratch)
    r_args, r_shapes, aliases, spans = [], [], dict(aliases or {}), []
    for r in rides:
        i0, o0 = len(r_args), len(r_shapes)
        r_args += [*r.inputs, *r.carried]
        for k, a in enumerate(r.carried):
            aliases[n_in + i0 + len(r.inputs) + k] = n_out + o0 + k
        r_shapes += [jax.ShapeDtypeStruct(a.shape, a.dtype) for a in r.carried] + list(r.created)
        spans.append((i0, len(r.inputs), o0, len(r.carried) + len(r.created)))

    def full_body(*refs):
        host_in, ride_in = refs[:n_in], refs[n_in:n_in + len(r_args)]
        p = n_in + len(r_args)
        host_out, ride_out = refs[p:p + n_out], refs[p + n_out:p + n_out + len(r_shapes)]
        p += n_out + len(r_shapes)
        host_scr, ride_sems = refs[p:p + n_scr], refs[p + n_scr:]
        ids = [pl.program_id(k) for k in range(len(grid))]
        first = functools.reduce(jnp.logical_and, [i == 0 for i in ids])
        last = functools.reduce(jnp.logical_and, [i == g - 1 for i, g in zip(ids, grid)])

        def each(method):
            for r, (i0, ni, o0, no), sems in zip(rides, spans, ride_sems):
                getattr(r, method)(ride_in[i0:i0 + ni], ride_out[o0:o0 + no], sems)

        if rides:
            pl.when(first)(lambda: each("start"))
        body(*host_in, *host_out, *host_scr)
        if rides:
            pl.when(last)(lambda: each("finish"))

    if rides:
        semantics = ("arbitrary",) * len(grid)
    outs = pl.pallas_call(
        full_body, grid=grid, in_specs=list(in_specs) + [_HBM] * len(r_args),
        out_specs=list(out_specs) + [_HBM] * len(r_shapes), out_shape=list(out_shape) + r_shapes,
        scratch_shapes=list(scratch) + [pltpu.SemaphoreType.DMA((r.n_sems,)) for r in rides],
        input_output_aliases=aliases, compiler_params=_cparams(*semantics), name=name)(*args, *r_args)
    return outs[:n_out], [outs[n_out + o0:n_out + o0 + no] for _, _, o0, no in spans]


def _rowmap(fn, rows, consts=(), stacks=(), row_outs=(), red_outs=(), tr=256, name=None, rides=None):
    t = rows[0].shape[0] if rows else stacks[0].shape[1]
    dtypes = [a.dtype for a in (*rows, *stacks)] + [dt for _, dt in row_outs]
    packed = any(jnp.dtype(dt).itemsize < 4 for dt in dtypes)
    tr = _pick(t, tr, 2 * SUBLANES if packed else SUBLANES)
    n_r, n_c, n_s, n_o = len(rows), len(consts), len(stacks), len(row_outs)

    def body(*refs):
        ins = [r[...] for r in refs[:n_r + n_c + n_s]]
        outs = refs[n_r + n_c + n_s:n_r + n_c + n_s + n_o]
        reds = refs[n_r + n_c + n_s + n_o:]
        res = fn(*ins)
        if not isinstance(res, (tuple, list)):
            res = (res,)
        for o, v in zip(outs, res[:n_o]):
            o[...] = v.astype(o.dtype)
        if reds:
            @pl.when(pl.program_id(0) == 0)
            def _():
                for r in reds:
                    r[...] = jnp.zeros_like(r)
            for r, v in zip(reds, res[n_o:]):
                r[...] += v

    in_specs = [pl.BlockSpec((tr, r.shape[1]), lambda i: (i, 0)) for r in rows]
    in_specs += [pl.BlockSpec(c.shape, _zero_map(c.ndim)) for c in consts]
    in_specs += [pl.BlockSpec((s.shape[0], tr, s.shape[2]), lambda i: (0, i, 0)) for s in stacks]
    out_specs = [pl.BlockSpec((tr, w), lambda i: (i, 0)) for w, _ in row_outs]
    out_specs += [pl.BlockSpec(s, _zero_map(len(s))) for s in red_outs]
    out_shape = [jax.ShapeDtypeStruct((t, w), dt) for w, dt in row_outs]
    out_shape += [jax.ShapeDtypeStruct(s, F32) for s in red_outs]
    outs, ride_outs = _pallas(body, grid=(t // tr,), in_specs=in_specs, out_specs=out_specs, out_shape=out_shape,
                              args=[*rows, *consts, *stacks], semantics=("arbitrary",), rides=rides or (), name=name)
    return outs if rides is None else (outs, ride_outs)


MM_VMEM_BUDGET = 36 * 1024 * 1024


def _divisors(dim, mult, must_divide=0):
    out = [t for t in range(dim, 0, -mult) if t % mult == 0 and dim % t == 0 and must_divide % t == 0]
    return out or [dim]


def _mm_tiles(m, n, k, a_bytes, b_bytes, out_bytes, extra_bytes, ta, b_off_n, b_off_k, out_off, tm, tn):
    tms = _divisors(m, LANES if ta else SUBLANES, out_off)
    tm = next((t for t in tms if t <= tm), tms[-1])
    tns = [t for t in _divisors(n, LANES, b_off_n) if t <= tn] or [_divisors(n, LANES, b_off_n)[-1]]
    for tn_ in tns:
        for tk in _divisors(k, LANES, b_off_k):
            acc = 0 if tk == k else tm * tn_ * 4
            need = 2 * (tm * tk * a_bytes + tk * tn_ * b_bytes + tm * tn_ * (out_bytes + extra_bytes)) + acc
            if need <= MM_VMEM_BUDGET:
                return tm, tn_, tk
    return tm, tns[-1], _divisors(k, LANES, b_off_k)[-1]


def _mm(a, b, *, ta=False, tb=False, bias=None, add=None, out_dtype=F32, b_off=0, n=None, tm=1024, tn=1024, name=None,
        rides=None, into=None):
    k, m = a.shape if ta else a.shape[::-1]
    if tb:
        n = b.shape[0] if n is None else n
        assert b.shape[1] == k and b_off + n <= b.shape[0]
    else:
        n = b.shape[1]
        assert b_off + k <= b.shape[0]
    out_rows, out_off, out_buf = (m, 0, None) if into is None else into
    extra = 4 if add is not None else 0
    tm, tn, tk = _mm_tiles(m, n, k, a.dtype.itemsize, b.dtype.itemsize, jnp.dtype(out_dtype).itemsize, extra, ta,
                           b_off if tb else 0, 0 if tb else b_off, out_off, tm, tn)
    nk = k // tk
    off_n, off_k = (b_off // tn, 0) if tb else (0, b_off // tk)
    off_m = out_off // tm
    dims = (((0 if ta else 1,), (1 if tb else 0,)), ((), ()))

    def body(*refs):
        a_ref, b_ref = refs[0], refs[1]
        rest = list(refs[2:])
        bias_ref = rest.pop(0) if bias is not None else None
        add_ref = rest.pop(0) if add is not None else None
        o_ref = rest.pop(0)
        part = lax.dot_general(a_ref[...].astype(BF16), b_ref[...].astype(BF16), dims, preferred_element_type=F32)

        def finish(r):
            if bias_ref is not None:
                r = r + bias_ref[...]
            if add_ref is not None:
                r = r + add_ref[...]
            o_ref[...] = r.astype(o_ref.dtype)

        if nk == 1:
            finish(part)
        else:
            acc_ref = rest.pop(0)
            kk = pl.program_id(2)

            @pl.when(kk == 0)
            def _():
                acc_ref[...] = part

            @pl.when(kk > 0)
            def _():
                acc_ref[...] += part

            @pl.when(kk == nk - 1)
            def _():
                finish(acc_ref[...])

    a_spec = pl.BlockSpec((tk, tm), lambda i, j, q: (q, i)) if ta else pl.BlockSpec((tm, tk), lambda i, j, q: (i, q))
    if tb:
        b_spec = pl.BlockSpec((tn, tk), lambda i, j, q: (j + off_n, q))
    else:
        b_spec = pl.BlockSpec((tk, tn), lambda i, j, q: (q + off_k, j))
    in_specs, args = [a_spec, b_spec], [a, b]
    if bias is not None:
        in_specs.append(pl.BlockSpec((1, tn), lambda i, j, q: (0, j)))
        args.append(bias)
    if add is not None:
        in_specs.append(pl.BlockSpec((tm, tn), lambda i, j, q: (i, j)))
        args.append(add)
    aliases = {}
    if out_buf is not None:
        assert out_buf.shape == (out_rows, n) and out_buf.dtype == jnp.dtype(out_dtype)
        in_specs.append(_HBM)
        args.append(out_buf)
        aliases = {len(args) - 1: 0}

    def body_in_place(*refs):
        body(*refs[:len(args) - 1], *refs[len(args):])

    outs, ride_outs = _pallas(
        body if out_buf is None else body_in_place, grid=(m // tm, n // tn, nk), in_specs=in_specs,
        out_specs=[pl.BlockSpec((tm, tn), lambda i, j, q: (i + off_m, j))],
        out_shape=[jax.ShapeDtypeStruct((out_rows, n), out_dtype)], args=args,
        scratch=[] if nk == 1 else [pltpu.VMEM((tm, tn), F32)],
        semantics=("parallel", "parallel", "arbitrary"), rides=rides or (), aliases=aliases, name=name)
    return outs[0] if rides is None else (outs[0], ride_outs)


def _mm_swiglu(a, w_t, *, name, rides=None):
    m, k = a.shape
    f = w_t.shape[0] // 2
    tm, tn, tk = _mm_tiles(m, f, k, a.dtype.itemsize, 2 * w_t.dtype.itemsize, 3 * 2, 0, False, 0, 0, 0, 1024, 512)
    assert tk == k, "the fused activation needs the whole contraction in one block"

    def body(a_ref, g_ref, u_ref, gate_ref, up_ref, act_ref):
        av = a_ref[...].astype(BF16)
        gate = lax.dot_general(av, g_ref[...].astype(BF16), _NT, preferred_element_type=F32)
        up = lax.dot_general(av, u_ref[...].astype(BF16), _NT, preferred_element_type=F32)
        gate_ref[...] = gate.astype(gate_ref.dtype)
        up_ref[...] = up.astype(up_ref.dtype)
        act_ref[...] = _swiglu2(gate, up).astype(act_ref.dtype)

    out_spec = pl.BlockSpec((tm, tn), lambda i, j: (i, j))
    outs, ride_outs = _pallas(
        body, grid=(m // tm, f // tn),
        in_specs=[pl.BlockSpec((tm, k), lambda i, j: (i, 0)), pl.BlockSpec((tn, k), lambda i, j: (j, 0)),
                  pl.BlockSpec((tn, k), lambda i, j: (j + f // tn, 0))],
        out_specs=[out_spec] * 3, out_shape=[jax.ShapeDtypeStruct((m, f), BF16)] * 3, args=[a, w_t, w_t],
        semantics=("parallel", "parallel"), rides=rides or (), name=name)
    return outs if rides is None else (outs, ride_outs)


def _ln(x, g, b, eps=1e-5):
    mu = jnp.mean(x, axis=-1, keepdims=True)
    var = jnp.mean(jnp.square(x - mu), axis=-1, keepdims=True)
    return (x - mu) * lax.rsqrt(var + eps) * g + b


def _post_norm(x, f, g, b):
    return _ln(ALPHA * x + f, g, b)


def _post_norm_bwd(x, f, dy, g, b):
    _, vjp = jax.vjp(_post_norm, x, f, g, b)
    return vjp(dy)


def _merge3(g0, g1, g2, pa, pb, pc):
    return jax.nn.sigmoid(g0) * pa + jax.nn.sigmoid(g1) * pb + jax.nn.sigmoid(g2) * pc


def _merge(gl, pa, pb, pc):
    d = pa.shape[1]
    return _merge3(gl[:, :d], gl[:, d:2 * d], gl[:, 2 * d:], pa, pb, pc)


def _merge_bwd(gl, pa, pb, pc, dm):
    d = pa.shape[1]
    _, vjp = jax.vjp(_merge3, gl[:, :d], gl[:, d:2 * d], gl[:, 2 * d:], pa, pb, pc)
    d0, d1, d2, dpa, dpb, dpc = vjp(dm)
    dgl = jnp.concatenate([d0, d1, d2], axis=1)
    return dgl, dpa, dpb, dpc, jnp.sum(dgl, axis=0, keepdims=True)


def _swiglu2(gate, up):
    return jax.nn.silu(gate) * up


def _swiglu_bwd(gate, up, dact):
    _, vjp = jax.vjp(_swiglu2, gate.astype(F32), up.astype(F32))
    dg, du = vjp(dact)
    return jnp.concatenate([dg, du], axis=1)


def _glu(ycp, lin):
    return ycp * jax.nn.sigmoid(lin)


def _glu_bwd(ycp, lin, dyc):
    _, vjp = jax.vjp(_glu, ycp, lin)
    dycp, dlin = vjp(dyc)
    return dycp, dlin, jnp.sum(dlin, axis=0, keepdims=True)


def _s5_out_bwd(ys, uc, dycp, dskip):
    _, vjp = jax.vjp(jax.nn.gelu, ys)
    dys = vjp(dycp)[0]
    return dys, dys * dskip, jnp.sum(dys * uc, axis=0, keepdims=True)


def _combine(o0, l0, o1, l1, o2, l2):
    m = jnp.maximum(jnp.maximum(l0, l1), l2)
    e0, e1, e2 = jnp.exp(l0 - m), jnp.exp(l1 - m), jnp.exp(l2 - m)
    s = e0 + e1 + e2
    return (e0 / s) * o0 + (e1 / s) * o1 + (e2 / s) * o2


def _combine_bwd(o0, l0, o1, l1, o2, l2, ya, dya, head_ones):
    m = jnp.maximum(jnp.maximum(l0, l1), l2)
    e0, e1, e2 = jnp.exp(l0 - m), jnp.exp(l1 - m), jnp.exp(l2 - m)
    s = e0 + e1 + e2
    dot_ya = jnp.dot(dya * ya, head_ones, precision=lax.Precision.HIGHEST, preferred_element_type=F32)
    w0, w1, w2 = e0 / s, e1 / s, e2 / s
    return w0 * dya, w1 * dya, w2 * dya, -w0 * dot_ya, -w1 * dot_ya, -w2 * dot_ya


def _loss_fn(y, tgt):
    err = y - tgt
    part = jnp.sum(jnp.sum(jnp.square(err), axis=1, keepdims=True), axis=0, keepdims=True) * (0.5 / y.shape[1])
    return err * (1.0 / y.shape[1]), jnp.broadcast_to(part, (1, LANES))


def _adamw(w, g, m, v):
    m = ADAM_B1 * m + (1.0 - ADAM_B1) * g
    v = ADAM_B2 * v + (1.0 - ADAM_B2) * jnp.square(g)
    m_hat = m / (1.0 - ADAM_B1 ** ADAM_STEP)
    v_hat = v / (1.0 - ADAM_B2 ** ADAM_STEP)
    delta = -ADAM_LR * (m_hat / (jnp.sqrt(v_hat) + ADAM_EPS) + ADAM_WD * w)
    return delta, m, v


def _sum_parts_fn(parts):
    g = parts[0].astype(F32)
    for j in range(1, parts.shape[0]):
        g = g + parts[j].astype(F32)
    return g


def _t5_bucket(dist):
    max_exact = N_REL_BUCKETS // 2
    d = np.maximum(dist, 1).astype(np.float32)
    scale = (N_REL_BUCKETS - max_exact) / math.log(REL_MAX_DIST / max_exact)
    large = max_exact + (np.log(d / max_exact) * scale).astype(np.int32)
    large = np.minimum(large, N_REL_BUCKETS - 1)
    return np.where(dist < max_exact, dist, large).astype(np.int32)


def _bucket_table(dilation):
    i = np.arange(ATT_BLOCK)[:, None]
    kk = np.arange(2 * ATT_BLOCK)[None, :]
    steps = ATT_BLOCK + i - kk
    return _t5_bucket(np.maximum(steps, 0) * dilation)


def _bias_fwd(rel_bias, buckets, g):
    def body(rel_ref, bk_ref, o_ref):
        bk = bk_ref[...]
        for h in range(HEADS_PER_GROUP):
            acc = jnp.zeros(bk.shape, F32)
            for b in range(N_REL_BUCKETS):
                acc = jnp.where(bk == b, rel_ref[b, g * HEADS_PER_GROUP + h], acc)
            o_ref[h] = acc

    return pl.pallas_call(
        body, in_specs=[pl.BlockSpec(memory_space=pltpu.SMEM), pl.BlockSpec(memory_space=pltpu.VMEM)],
        out_specs=pl.BlockSpec(memory_space=pltpu.VMEM),
        out_shape=jax.ShapeDtypeStruct((HEADS_PER_GROUP, ATT_BLOCK, 2 * ATT_BLOCK), F32),
        name=f"rel_bias_fwd{g}")(rel_bias, buckets)


def _bias_bwd(dbias, buckets, g):
    def body(db_ref, bk_ref, o_ref):
        bk = bk_ref[...]
        row = lax.broadcasted_iota(jnp.int32, (N_REL_BUCKETS, LANES), 0)
        col = lax.broadcasted_iota(jnp.int32, (N_REL_BUCKETS, LANES), 1)
        acc = jnp.zeros((N_REL_BUCKETS, LANES), F32)
        for h in range(HEADS_PER_GROUP):
            d = db_ref[h]
            for b in range(N_REL_BUCKETS):
                s = jnp.sum(jnp.sum(jnp.where(bk == b, d, 0.0), axis=1, keepdims=True), axis=0, keepdims=True)
                acc = acc + jnp.where((row == b) & (col == g * HEADS_PER_GROUP + h), s, 0.0)
        o_ref[...] = acc

    return pl.pallas_call(
        body, in_specs=[pl.BlockSpec(memory_space=pltpu.VMEM), pl.BlockSpec(memory_space=pltpu.VMEM)],
        out_specs=pl.BlockSpec(memory_space=pltpu.VMEM),
        out_shape=jax.ShapeDtypeStruct((N_REL_BUCKETS, LANES), F32), name=f"rel_bias_bwd{g}")(dbias, buckets)


_NT = (((1,), (1,)), ((), ()))
_TN = (((0,), (0,)), ((), ()))
_QKV_BLOCKS = 3 * QKV_WIDTH // WIDTH_A


def _band_mask(n_is_first):
    i = lax.broadcasted_iota(jnp.int32, (ATT_BLOCK, 2 * ATT_BLOCK), 0)
    kk = lax.broadcasted_iota(jnp.int32, (ATT_BLOCK, 2 * ATT_BLOCK), 1)
    return (kk >= i) & (kk <= i + ATT_STEPS) & ((kk >= ATT_BLOCK) | jnp.logical_not(n_is_first))


def _head(ref, h):
    return ref[:, h * HEAD_DIM:(h + 1) * HEAD_DIM]


def _attn_specs(g, d):
    blk = (ATT_BLOCK, WIDTH_A)
    q = pl.BlockSpec(blk, lambda c, n: (n, c * _QKV_BLOCKS + g))
    kp = pl.BlockSpec(blk, lambda c, n: (jnp.maximum(n - 1, 0), c * _QKV_BLOCKS + 3 + g))
    kc = pl.BlockSpec(blk, lambda c, n: (n, c * _QKV_BLOCKS + 3 + g))
    vp = pl.BlockSpec(blk, lambda c, n: (jnp.maximum(n - 1, 0), c * _QKV_BLOCKS + 6 + g))
    vc = pl.BlockSpec(blk, lambda c, n: (n, c * _QKV_BLOCKS + 6 + g))
    return [q, kp, kc, vp, vc]


def _attn_fwd(qkv, bias, g, d, comm):
    t = qkv.shape[0]
    lq = t // d
    nb = lq // ATT_BLOCK
    scale = HEAD_DIM ** -0.5

    def body(q_ref, kp_ref, kc_ref, vp_ref, vc_ref, b_ref, o_ref, l_ref):
        mask = _band_mask(pl.program_id(1) == 0)
        for h in range(HEADS_PER_GROUP):
            qh = _head(q_ref, h).astype(BF16)
            kh = jnp.concatenate([_head(kp_ref, h), _head(kc_ref, h)], axis=0).astype(BF16)
            vh = jnp.concatenate([_head(vp_ref, h), _head(vc_ref, h)], axis=0).astype(BF16)
            s = lax.dot_general(qh, kh, _NT, preferred_element_type=F32) * scale + b_ref[h]
            s = jnp.where(mask, s, NEG_INF)
            m = jnp.max(s, axis=1, keepdims=True)
            p = jnp.exp(s - m)
            den = jnp.sum(p, axis=1, keepdims=True)
            o = jnp.dot(p.astype(BF16), vh, preferred_element_type=F32) / den
            o_ref[:, h * HEAD_DIM:(h + 1) * HEAD_DIM] = o
            l_ref[:, h * HEAD_DIM:(h + 1) * HEAD_DIM] = jnp.broadcast_to(m + jnp.log(den), (ATT_BLOCK, HEAD_DIM))

    out_spec = pl.BlockSpec((ATT_BLOCK, WIDTH_A), lambda c, n: (n, c))
    (o, lse), ride_outs = _pallas(
        body, grid=(d, nb),
        in_specs=_attn_specs(g, d) + [pl.BlockSpec(bias.shape, _zero_map(3))],
        out_specs=[out_spec, out_spec],
        out_shape=[jax.ShapeDtypeStruct((lq, d * WIDTH_A), F32)] * 2,
        args=[*([qkv.reshape(lq, d * 3 * QKV_WIDTH)] * 5), bias],
        semantics=("parallel", "parallel"), rides=[comm.ride(f"attn_fwd{g}")], name=f"attn_fwd{g}")
    comm.took(ride_outs)
    return o.reshape(t, WIDTH_A), lse.reshape(t, WIDTH_A)


def _attn_bwd(qkv, bias, do, lse, corr, g, d):
    t = qkv.shape[0]
    lq = t // d
    nb = lq // ATT_BLOCK
    scale = HEAD_DIM ** -0.5

    def body(k_ref, v_ref, q0_ref, q1_ref, do0_ref, do1_ref, l0_ref, l1_ref, c0_ref, c1_ref, b_ref,
             dq_ref, dk_ref, dv_ref, db_ref, dq_prev):
        c, j = pl.program_id(0), pl.program_id(1)

        @pl.when((c == 0) & (j == 0))
        def _():
            db_ref[...] = jnp.zeros_like(db_ref)

        @pl.when(j == 0)
        def _():
            dq_prev[...] = jnp.zeros_like(dq_prev)

        i = lax.broadcasted_iota(jnp.int32, (ATT_BLOCK, ATT_BLOCK), 0)
        kk = lax.broadcasted_iota(jnp.int32, (ATT_BLOCK, ATT_BLOCK), 1)
        mask0 = kk <= i
        mask1 = (kk >= i) & (j + 1 < nb)
        for h in range(HEADS_PER_GROUP):
            kh = _head(k_ref, h).astype(BF16)
            vh = _head(v_ref, h).astype(BF16)
            cols = slice(h * HEAD_DIM, (h + 1) * HEAD_DIM)
            dk = jnp.zeros((ATT_BLOCK, HEAD_DIM), F32)
            dv = jnp.zeros((ATT_BLOCK, HEAD_DIM), F32)
            dq_parts = []
            parts = ((q0_ref, do0_ref, l0_ref, c0_ref, mask0, ATT_BLOCK), (q1_ref, do1_ref, l1_ref, c1_ref, mask1, 0))
            for q_ref, do_ref, l_ref, c_ref, mask, off in parts:
                qh = _head(q_ref, h).astype(BF16)
                doh = _head(do_ref, h).astype(BF16)
                s = lax.dot_general(qh, kh, _NT, preferred_element_type=F32) * scale + b_ref[h, :, off:off + ATT_BLOCK]
                s = jnp.where(mask, s, NEG_INF)
                p = jnp.exp(s - l_ref[:, h * HEAD_DIM:h * HEAD_DIM + 1])
                dp = lax.dot_general(doh, vh, _NT, preferred_element_type=F32)
                ds = p * (dp + c_ref[:, h * HEAD_DIM:h * HEAD_DIM + 1])
                dsb = ds.astype(BF16)
                dv = dv + lax.dot_general(p.astype(BF16), doh, _TN, preferred_element_type=F32)
                dk = dk + lax.dot_general(dsb, qh, _TN, preferred_element_type=F32)
                dq_parts.append(jnp.dot(dsb, kh, preferred_element_type=F32))
                db_ref[h, :, off:off + ATT_BLOCK] += ds
            dk_ref[:, cols] = dk * scale
            dv_ref[:, cols] = dv
            dq_ref[:, cols] = (dq_prev[:, cols] + dq_parts[0]) * scale
            dq_prev[:, cols] = dq_parts[1]

    blk = (ATT_BLOCK, WIDTH_A)
    nxt = lambda n: jnp.minimum(n + 1, nb - 1)
    k_spec = pl.BlockSpec(blk, lambda c, n: (n, c * _QKV_BLOCKS + 3 + g))
    v_spec = pl.BlockSpec(blk, lambda c, n: (n, c * _QKV_BLOCKS + 6 + g))
    q0_spec = pl.BlockSpec(blk, lambda c, n: (n, c * _QKV_BLOCKS + g))
    q1_spec = pl.BlockSpec(blk, lambda c, n: (nxt(n), c * _QKV_BLOCKS + g))
    r0 = pl.BlockSpec(blk, lambda c, n: (n, c))
    r1 = pl.BlockSpec(blk, lambda c, n: (nxt(n), c))
    view = lambda a: a.reshape(lq, d * WIDTH_A)
    qv = qkv.reshape(lq, d * 3 * QKV_WIDTH)
    dq, dk, dv, dbias = pl.pallas_call(
        body, grid=(d, nb),
        in_specs=[k_spec, v_spec, q0_spec, q1_spec, r0, r1, r0, r1, r0, r1, pl.BlockSpec(bias.shape, _zero_map(3))],
        out_specs=[r0, r0, r0, pl.BlockSpec(bias.shape, _zero_map(3))],
        out_shape=[jax.ShapeDtypeStruct((lq, d * WIDTH_A), F32)] * 3 + [jax.ShapeDtypeStruct(bias.shape, F32)],
        scratch_shapes=[pltpu.VMEM(blk, F32)],
        compiler_params=_cparams("arbitrary", "arbitrary"), name=f"attn_bwd{g}",
    )(qv, qv, qv, qv, view(do), view(do), view(lse), view(lse), view(corr), view(corr), bias)
    return dq.reshape(t, WIDTH_A), dk.reshape(t, WIDTH_A), dv.reshape(t, WIDTH_A), dbias


def _tril_mask():
    r = lax.broadcasted_iota(jnp.int32, (CHUNK, CHUNK), 0)
    c = lax.broadcasted_iota(jnp.int32, (CHUNK, CHUNK), 1)
    return c <= r


def _gmlp_fwd(zb, ln_g, ln_b, w_s, b_s_t):
    t = zb.shape[0]
    tr = _pick(t, 2 * CHUNK, CHUNK)

    def body(z_ref, g_ref, b_ref, ws_ref, bs_ref, o_ref):
        tri = _tril_mask()
        z = jax.nn.gelu(z_ref[...])
        u = z[:, :WIDTH_B]
        vn = _ln(z[:, WIDTH_B:], g_ref[...], b_ref[...])
        for ch in range(tr // CHUNK):
            rows = slice(ch * CHUNK, (ch + 1) * CHUNK)
            for gi in range(N_GROUPS_B):
                cols = slice(gi * CHUNK, (gi + 1) * CHUNK)
                w = jnp.where(tri, ws_ref[gi], 0.0).astype(BF16)
                mixed = jnp.dot(w, vn[rows, cols].astype(BF16), preferred_element_type=F32) + bs_ref[:, gi:gi + 1]
                o_ref[rows, cols] = (u[rows, cols] * mixed).astype(o_ref.dtype)

    return pl.pallas_call(
        body, grid=(t // tr,),
        in_specs=[pl.BlockSpec((tr, 2 * WIDTH_B), lambda i: (i, 0)), pl.BlockSpec(ln_g.shape, _zero_map(2)),
                  pl.BlockSpec(ln_b.shape, _zero_map(2)), pl.BlockSpec(w_s.shape, _zero_map(3)),
                  pl.BlockSpec(b_s_t.shape, _zero_map(2))],
        out_specs=pl.BlockSpec((tr, WIDTH_B), lambda i: (i, 0)),
        out_shape=jax.ShapeDtypeStruct((t, WIDTH_B), BF16),
        compiler_params=_cparams("parallel"), name="gmlp_fwd")(zb, ln_g, ln_b, w_s, b_s_t)


def _gmlp_bwd(zb, dyb, ln_g, ln_b, w_s, b_s_t, group_sel):
    t = zb.shape[0]
    tr = _pick(t, 2 * CHUNK, CHUNK)

    def body(z_ref, dy_ref, g_ref, b_ref, ws_ref, bs_ref, sel_ref, dz_ref, dzs_ref, dg_ref, db_ref, dws_ref, dbs_ref,
             du_s, dvn_s, dm_s):
        @pl.when(pl.program_id(0) == 0)
        def _():
            dzs_ref[...] = jnp.zeros_like(dzs_ref)
            dg_ref[...] = jnp.zeros_like(dg_ref)
            db_ref[...] = jnp.zeros_like(db_ref)
            dws_ref[...] = jnp.zeros_like(dws_ref)
            dbs_ref[...] = jnp.zeros_like(dbs_ref)

        tri = _tril_mask()
        z, gelu_vjp = jax.vjp(jax.nn.gelu, z_ref[...])
        u = z[:, :WIDTH_B]
        vn, ln_vjp = jax.vjp(_ln, z[:, WIDTH_B:], g_ref[...], b_ref[...])
        dy = dy_ref[...]
        for ch in range(tr // CHUNK):
            rows = slice(ch * CHUNK, (ch + 1) * CHUNK)
            for gi in range(N_GROUPS_B):
                cols = slice(gi * CHUNK, (gi + 1) * CHUNK)
                w = jnp.where(tri, ws_ref[gi], 0.0).astype(BF16)
                vg = vn[rows, cols].astype(BF16)
                mixed = jnp.dot(w, vg, preferred_element_type=F32) + bs_ref[:, gi:gi + 1]
                dyg = dy[rows, cols]
                dm = dyg * u[rows, cols]
                dmb = dm.astype(BF16)
                du_s[rows, cols] = dyg * mixed
                dm_s[rows, cols] = dm
                dvn_s[rows, cols] = lax.dot_general(w, dmb, _TN, preferred_element_type=F32)
                dws_ref[gi] += jnp.where(tri, lax.dot_general(dmb, vg, _NT, preferred_element_type=F32), 0.0)
            dbs_ref[...] += jnp.dot(dm_s[rows, :], sel_ref[...], precision=lax.Precision.HIGHEST,
                                    preferred_element_type=F32)
        dv, dg, db = ln_vjp(dvn_s[...])
        dg_ref[...] += dg
        db_ref[...] += db
        dz = gelu_vjp(jnp.concatenate([du_s[...], dv], axis=1))[0]
        dz_ref[...] = dz.astype(dz_ref.dtype)
        dzs_ref[...] += jnp.sum(dz, axis=0, keepdims=True)

    full = lambda a: pl.BlockSpec(a.shape, _zero_map(a.ndim))
    return pl.pallas_call(
        body, grid=(t // tr,),
        in_specs=[pl.BlockSpec((tr, 2 * WIDTH_B), lambda i: (i, 0)), pl.BlockSpec((tr, WIDTH_B), lambda i: (i, 0)),
                  full(ln_g), full(ln_b), full(w_s), full(b_s_t), full(group_sel)],
        out_specs=[pl.BlockSpec((tr, 2 * WIDTH_B), lambda i: (i, 0)), pl.BlockSpec((1, 2 * WIDTH_B), _zero_map(2)),
                   full(ln_g), full(ln_b), full(w_s), pl.BlockSpec((CHUNK, LANES), _zero_map(2))],
        out_shape=[jax.ShapeDtypeStruct((t, 2 * WIDTH_B), BF16), jax.ShapeDtypeStruct((1, 2 * WIDTH_B), F32),
                   jax.ShapeDtypeStruct(ln_g.shape, F32),
                   jax.ShapeDtypeStruct(ln_b.shape, F32), jax.ShapeDtypeStruct(w_s.shape, F32),
                   jax.ShapeDtypeStruct((CHUNK, LANES), F32)],
        scratch_shapes=[pltpu.VMEM((tr, WIDTH_B), F32)] * 3,
        compiler_params=_cparams("arbitrary"), name="gmlp_bwd")(zb, dyb, ln_g, ln_b, w_s, b_s_t, group_sel)


def _s5_disc(lr, li, ldt, br_t, bi_t):
    dt = jnp.exp(ldt)
    mag = jnp.exp(lr * dt)
    ab_re = mag * jnp.cos(li * dt)
    ab_im = mag * jnp.sin(li * dt)
    nrm = lr * lr + li * li
    cr = ((ab_re - 1.0) * lr + ab_im * li) / nrm
    ci = (ab_im * lr - (ab_re - 1.0) * li) / nrm
    return ab_re, ab_im, cr * br_t - ci * bi_t, cr * bi_t + ci * br_t


def _vmem_call(fn, args, out_shape, name):
    def body(*refs):
        res = fn(*[r[...] for r in refs[:len(args)]])
        for o, v in zip(refs[len(args):], res):
            o[...] = v

    vm = pl.BlockSpec(memory_space=pltpu.VMEM)
    return pl.pallas_call(body, in_specs=[vm] * len(args), out_specs=[vm] * len(out_shape),
                          out_shape=out_shape, name=name)(*args)


def _s5_disc_fwd(lr, li, ldt, br_t, bi_t):
    s1 = jax.ShapeDtypeStruct(lr.shape, F32)
    s2 = jax.ShapeDtypeStruct(br_t.shape, F32)
    return _vmem_call(_s5_disc, [lr, li, ldt, br_t, bi_t], [s1, s1, s2, s2], "s5_disc_fwd")


def _s5_disc_bwd(lr, li, ldt, br_t, bi_t, cts):
    def fn(lr, li, ldt, br_t, bi_t, d0, d1, d2, d3):
        _, vjp = jax.vjp(_s5_disc, lr, li, ldt, br_t, bi_t)
        return vjp((d0, d1, d2, d3))

    shp = [jax.ShapeDtypeStruct(a.shape, F32) for a in (lr, li, ldt, br_t, bi_t)]
    return _vmem_call(fn, [lr, li, ldt, br_t, bi_t, *cts], shp, "s5_disc_bwd")


_SCAN_ROWS = SSM_COLS // LANES
_SCAN_CHUNK = 128
_SSM_IN = SSM_PACK * SSM_GROUP
_SSM_ST = SSM_PACK * SSM_STATE


def _packed_in(xb, m_ref):
    return jnp.concatenate([jnp.dot(xb[:, j * _SSM_IN:(j + 1) * _SSM_IN], m_ref[j], preferred_element_type=F32)
                            for j in range(N_SSM_BLOCKS)], axis=1)


def _packed_out(xb, m_ref):
    return jnp.concatenate([lax.dot_general(xb[:, j * _SSM_ST:(j + 1) * _SSM_ST], m_ref[j], _NT, preferred_element_type=F32)
                            for j in range(N_SSM_BLOCKS)], axis=1)


def _s5_fwd(uc, mats, are, aim, d_skip):
    t = uc.shape[0]
    tc = _pick(t, _SCAN_CHUNK, SUBLANES)

    def body(u_ref, br_ref, bi_ref, cr_ref, ci_ref, ar_ref, ai_ref, d_ref, xr_ref, xi_ref, ys_ref, ycp_ref,
             sr, si, st_ref):
        @pl.when(pl.program_id(0) == 0)
        def _():
            st_ref[...] = jnp.zeros_like(st_ref)

        u = u_ref[...]
        ub = u.astype(BF16)
        sr[...] = _packed_in(ub, br_ref).reshape(tc, _SCAN_ROWS, LANES)
        si[...] = _packed_in(ub, bi_ref).reshape(tc, _SCAN_ROWS, LANES)
        ar, ai = ar_ref[...], ai_ref[...]

        def step(i, carry):
            xr, xi = carry
            nr = ar * xr - ai * xi + sr[i]
            ni = ar * xi + ai * xr + si[i]
            sr[i] = nr
            si[i] = ni
            return nr, ni

        xr, xi = lax.fori_loop(0, tc, step, (st_ref[0], st_ref[1]), unroll=8)
        st_ref[0] = xr
        st_ref[1] = xi
        x_re = sr[...].reshape(tc, SSM_COLS)
        x_im = si[...].reshape(tc, SSM_COLS)
        xr_ref[...] = x_re
        xi_ref[...] = x_im
        ys = _packed_out(x_re.astype(BF16), cr_ref) - _packed_out(x_im.astype(BF16), ci_ref) + d_ref[...] * u
        ys_ref[...] = ys
        ycp_ref[...] = jax.nn.gelu(ys)

    row = lambda w: pl.BlockSpec((tc, w), lambda i: (i, 0))
    mat = pl.BlockSpec(mats[0].shape, _zero_map(3))
    par = pl.BlockSpec((_SCAN_ROWS, LANES), _zero_map(2))
    wide, narrow = jax.ShapeDtypeStruct((t, SSM_COLS), F32), jax.ShapeDtypeStruct((t, WIDTH_C), F32)
    return pl.pallas_call(
        body, grid=(t // tc,), in_specs=[row(WIDTH_C), mat, mat, mat, mat, par, par, pl.BlockSpec(d_skip.shape, _zero_map(2))],
        out_specs=[row(SSM_COLS), row(SSM_COLS), row(WIDTH_C), row(WIDTH_C)], out_shape=[wide, wide, narrow, narrow],
        scratch_shapes=[pltpu.VMEM((tc, _SCAN_ROWS, LANES), F32)] * 2 + [pltpu.VMEM((2, _SCAN_ROWS, LANES), F32)],
        compiler_params=_cparams("arbitrary"), name="s5_fwd")(uc, *mats, are, aim, d_skip)


def _s5_bwd(dys, duc_skip, uc, xr, xi, mats, are, aim):
    t = dys.shape[0]
    tc = _pick(t, _SCAN_CHUNK, SUBLANES)
    nc = t // tc

    def body(dy_ref, ds_ref, u_ref, xr_ref, xi_ref, pr_ref, pi_ref, br_ref, bi_ref, cr_ref, ci_ref, ar_ref, ai_ref,
             du_ref, dbr_ref, dbi_ref, dcr_ref, dci_ref, dar_ref, dai_ref, gr, gi, x3r, x3i, st_ref):
        step_id = pl.program_id(0)

        @pl.when(step_id == 0)
        def _():
            st_ref[...] = jnp.zeros_like(st_ref)
            for ref in (dbr_ref, dbi_ref, dcr_ref, dci_ref, dar_ref, dai_ref):
                ref[...] = jnp.zeros_like(ref)

        dyb = dy_ref[...].astype(BF16)
        x_re, x_im = xr_ref[...], xi_ref[...]
        gr[...] = _packed_in(dyb, cr_ref).reshape(tc, _SCAN_ROWS, LANES)
        gi[...] = (-_packed_in(dyb, ci_ref)).reshape(tc, _SCAN_ROWS, LANES)
        x3r[...] = x_re.reshape(tc, _SCAN_ROWS, LANES)
        x3i[...] = x_im.reshape(tc, _SCAN_ROWS, LANES)
        ar, ai = ar_ref[...], ai_ref[...]

        def update(i, carry, pxr, pxi):
            g_r, g_i, dar, dai = carry
            ngr = gr[i] + ar * g_r + ai * g_i
            ngi = gi[i] - ai * g_r + ar * g_i
            gr[i] = ngr
            gi[i] = ngi
            return ngr, ngi, dar + ngr * pxr + ngi * pxi, dai - ngr * pxi + ngi * pxr

        def step(s, carry):
            i = tc - 1 - s
            return update(i, carry, x3r[i - 1], x3i[i - 1])

        zero = jnp.zeros((_SCAN_ROWS, LANES), F32)
        carry = lax.fori_loop(0, tc - 1, step, (st_ref[0], st_ref[1], zero, zero), unroll=8)
        has_prev = (step_id < nc - 1).astype(F32)
        last = SUBLANES - 1
        p_re = pr_ref[last:, :].reshape(1, _SCAN_ROWS, LANES)[0] * has_prev
        p_im = pi_ref[last:, :].reshape(1, _SCAN_ROWS, LANES)[0] * has_prev
        g_r, g_i, dar, dai = update(0, carry, p_re, p_im)
        st_ref[0] = g_r
        st_ref[1] = g_i
        dar_ref[...] += dar
        dai_ref[...] += dai

        g_re = gr[...].reshape(tc, SSM_COLS).astype(BF16)
        g_im = gi[...].reshape(tc, SSM_COLS).astype(BF16)
        du_ref[...] = ds_ref[...] + _packed_out(g_re, br_ref) + _packed_out(g_im, bi_ref)
        ub, xrb, xib = u_ref[...].astype(BF16), x_re.astype(BF16), x_im.astype(BF16)
        for j in range(N_SSM_BLOCKS):
            narrow, wide = slice(j * _SSM_IN, (j + 1) * _SSM_IN), slice(j * _SSM_ST, (j + 1) * _SSM_ST)
            dbr_ref[j] += lax.dot_general(ub[:, narrow], g_re[:, wide], _TN, preferred_element_type=F32)
            dbi_ref[j] += lax.dot_general(ub[:, narrow], g_im[:, wide], _TN, preferred_element_type=F32)
            dcr_ref[j] += lax.dot_general(dyb[:, narrow], xrb[:, wide], _TN, preferred_element_type=F32)
            dci_ref[j] -= lax.dot_general(dyb[:, narrow], xib[:, wide], _TN, preferred_element_type=F32)

    rev = lambda w: pl.BlockSpec((tc, w), lambda i: (nc - 1 - i, 0))
    prev = pl.BlockSpec((SUBLANES, SSM_COLS), lambda i: (jnp.maximum((nc - 1 - i) * (tc // SUBLANES) - 1, 0), 0))
    mat = pl.BlockSpec(mats[0].shape, _zero_map(3))
    par = pl.BlockSpec((_SCAN_ROWS, LANES), _zero_map(2))
    msh = jax.ShapeDtypeStruct(mats[0].shape, F32)
    psh = jax.ShapeDtypeStruct((_SCAN_ROWS, LANES), F32)
    return pl.pallas_call(
        body, grid=(nc,),
        in_specs=[rev(WIDTH_C), rev(WIDTH_C), rev(WIDTH_C), rev(SSM_COLS), rev(SSM_COLS), prev, prev,
                  mat, mat, mat, mat, par, par],
        out_specs=[rev(WIDTH_C), mat, mat, mat, mat, par, par],
        out_shape=[jax.ShapeDtypeStruct((t, WIDTH_C), F32), msh, msh, msh, msh, psh, psh],
        scratch_shapes=[pltpu.VMEM((tc, _SCAN_ROWS, LANES), F32)] * 4 + [pltpu.VMEM((2, _SCAN_ROWS, LANES), F32)],
        compiler_params=_cparams("arbitrary"), name="s5_bwd")(dys, duc_skip, uc, xr, xi, xr, xi, *mats, are, aim)


def _diag_blocks(a, b):
    return [(j, i, slice(i * a, (i + 1) * a), slice(i * b, (i + 1) * b))
            for j in range(N_SSM_BLOCKS) for i in range(SSM_PACK)]


def _block_diag(ms):
    _, a, b = ms[0].shape

    def body(*refs):
        for m_ref, o_ref in zip(refs[:len(ms)], refs[len(ms):]):
            o_ref[...] = jnp.zeros_like(o_ref)
            for j, i, rows, cols in _diag_blocks(a, b):
                o_ref[j, rows, cols] = m_ref[j * SSM_PACK + i].astype(o_ref.dtype)

    vm = pl.BlockSpec(memory_space=pltpu.VMEM)
    shape = jax.ShapeDtypeStruct((N_SSM_BLOCKS, SSM_PACK * a, SSM_PACK * b), BF16)
    return pl.pallas_call(body, in_specs=[vm] * len(ms), out_specs=[vm] * len(ms), out_shape=[shape] * len(ms),
                          name="s5_block_diag")(*ms)


def _block_diag_extract(ms, a, b):
    def body(*refs):
        for m_ref, o_ref in zip(refs[:len(ms)], refs[len(ms):]):
            for j, i, rows, cols in _diag_blocks(a, b):
                o_ref[j * SSM_PACK + i] = m_ref[j, rows, cols]

    vm = pl.BlockSpec(memory_space=pltpu.VMEM)
    shape = jax.ShapeDtypeStruct((N_GROUPS_C, a, b), F32)
    return pl.pallas_call(body, in_specs=[vm] * len(ms), out_specs=[vm] * len(ms), out_shape=[shape] * len(ms),
                          name="s5_block_diag_extract")(*ms)


def _exchange(src, *, gather, name):
    shape = src.shape if gather else src.shape[1:]

    def body(src_ref, out_ref, send_sems, recv_sems, local_sem):
        x, y, c = lax.axis_index("x"), lax.axis_index("y"), lax.axis_index("c")
        me = 4 * x + 2 * y + c
        copies = []
        for r in range(1, N_DEV):
            px = 1 - x if r & 4 else x
            py = 1 - y if r & 2 else y
            pc = 1 - c if r & 1 else c
            piece = src_ref if gather else src_ref.at[4 * px + 2 * py + pc]
            cp = pltpu.make_async_remote_copy(
                src_ref=piece, dst_ref=out_ref.at[me], send_sem=send_sems.at[r - 1], recv_sem=recv_sems.at[r - 1],
                device_id=(px, py, pc), device_id_type=pl.DeviceIdType.MESH)
            cp.start()
            copies.append(cp)
        mine = pltpu.make_async_copy(src_ref if gather else src_ref.at[me], out_ref.at[me], local_sem)
        mine.start()
        for cp in copies:
            cp.wait()
        mine.wait()

    hbm = pl.BlockSpec(memory_space=pl.ANY)
    return pl.pallas_call(
        body, in_specs=[hbm], out_specs=hbm, out_shape=jax.ShapeDtypeStruct((N_DEV,) + tuple(shape), src.dtype),
        scratch_shapes=[pltpu.SemaphoreType.DMA((N_DEV - 1,)), pltpu.SemaphoreType.DMA((N_DEV - 1,)),
                        pltpu.SemaphoreType.DMA(())],
        name=name)(src)


def _mesh_place():
    x, y, c = lax.axis_index("x"), lax.axis_index("y"), lax.axis_index("c")
    other_chips = [(1 - x, y), (x, 1 - y), (1 - x, 1 - y)]
    return x, y, c, other_chips


def _gather_layer(srcs, layer, name):
    n = len(srcs)

    def body(*refs):
        src = [r.at[layer] for r in refs[:n]]
        out = refs[n:2 * n]
        send_sems, recv_sems, local_sems = refs[2 * n:]
        x, y, c, chips = _mesh_place()
        me, sibling = (x, y, c), (x, y, 1 - c)

        def copy(t, k, block, to, from_src=False):
            slot = 4 * block[0] + 2 * block[1] + block[2]
            return pltpu.make_async_remote_copy(
                src_ref=src[t] if from_src else out[t].at[slot], dst_ref=out[t].at[slot],
                send_sem=send_sems.at[t, k], recv_sem=recv_sems.at[t, k], device_id=to, device_id_type=_MESH_ID)

        mine = [pltpu.make_async_copy(src[t], out[t].at[4 * x + 2 * y + c], local_sems.at[t]) for t in range(n)]
        for cp in mine:
            cp.start()
        first = []
        for t in range(n):
            first.append(copy(t, 0, me, sibling, True))
            first += [copy(t, 1 + j, me, (*chip, c), True) for j, chip in enumerate(chips)]
        for cp in first:
            cp.start()
        passed = []
        for j, chip in enumerate(chips):
            for t in range(n):
                copy(t, 1 + j, (*chip, c), me).wait_recv()
                fwd = copy(t, 4 + j, (*chip, c), sibling)
                fwd.start()
                passed.append(fwd)
        for t in range(n):
            copy(t, 0, sibling, me).wait_recv()
            for j, chip in enumerate(chips):
                copy(t, 4 + j, (*chip, 1 - c), me).wait_recv()
        for cp in first + passed:
            cp.wait_send()
        for cp in mine:
            cp.wait()

    return pl.pallas_call(
        body, in_specs=[_HBM] * n, out_specs=[_HBM] * n,
        out_shape=[jax.ShapeDtypeStruct((N_DEV,) + s.shape[1:], s.dtype) for s in srcs],
        scratch_shapes=[pltpu.SemaphoreType.DMA((n, N_DEV - 1)), pltpu.SemaphoreType.DMA((n, N_DEV - 1)),
                        pltpu.SemaphoreType.DMA((n,))],
        name=name)(*srcs)


def _scatter_pair(srcs, name):
    n = len(srcs)

    def body(*refs):
        src, out = refs[:n], refs[n:2 * n]
        send_sems, recv_sems = refs[2 * n:]
        x, y, c, _ = _mesh_place()
        copies = [pltpu.make_async_remote_copy(
            src_ref=src[t].at[:, 1 - c], dst_ref=out[t], send_sem=send_sems.at[t], recv_sem=recv_sems.at[t],
            device_id=(x, y, 1 - c), device_id_type=_MESH_ID) for t in range(n)]
        for cp in copies:
            cp.start()
        for cp in copies:
            cp.wait()

    return pl.pallas_call(
        body, in_specs=[_HBM] * n, out_specs=[_HBM] * n,
        out_shape=[jax.ShapeDtypeStruct((s.shape[0],) + s.shape[2:], s.dtype) for s in srcs],
        scratch_shapes=[pltpu.SemaphoreType.DMA((n,)), pltpu.SemaphoreType.DMA((n,))], name=name)(*srcs)


def _pair_add(src, recv, name):
    nchip, _, r, cdim = src.shape
    tr = _pick(r, 256, 2 * SUBLANES)
    core = lax.axis_index("c").astype(jnp.int32).reshape(1)

    def body(core_ref, s_ref, r_ref, o_ref):
        o_ref[...] = (s_ref[...].astype(F32) + r_ref[...].astype(F32)).astype(o_ref.dtype)

    grid_spec = pltpu.PrefetchScalarGridSpec(
        num_scalar_prefetch=1, grid=(nchip, r // tr),
        in_specs=[pl.BlockSpec((None, None, tr, cdim), lambda k, i, core_ref: (k, core_ref[0], i, 0)),
                  pl.BlockSpec((None, tr, cdim), lambda k, i, core_ref: (k, i, 0))],
        out_specs=pl.BlockSpec((None, tr, cdim), lambda k, i, core_ref: (k, i, 0)))
    return pl.pallas_call(body, grid_spec=grid_spec, out_shape=jax.ShapeDtypeStruct(recv.shape, recv.dtype),
                          compiler_params=_cparams("parallel", "parallel"), name=name)(core, src, recv)


def _scatter_chips(srcs, name):
    n = len(srcs)

    def body(*refs):
        src, out = refs[:n], refs[n:2 * n]
        send_sems, recv_sems, local_sems = refs[2 * n:]
        x, y, c, chips = _mesh_place()
        my_chip = 2 * x + y
        mine = [pltpu.make_async_copy(src[t].at[my_chip], out[t].at[my_chip], local_sems.at[t]) for t in range(n)]
        copies = [pltpu.make_async_remote_copy(
            src_ref=src[t].at[2 * chip[0] + chip[1]], dst_ref=out[t].at[my_chip],
            send_sem=send_sems.at[t, j], recv_sem=recv_sems.at[t, j], device_id=(*chip, c), device_id_type=_MESH_ID)
            for t in range(n) for j, chip in enumerate(chips)]
        for cp in mine + copies:
            cp.start()
        for cp in copies + mine:
            cp.wait()

    return pl.pallas_call(
        body, in_specs=[_HBM] * n, out_specs=[_HBM] * n,
        out_shape=[jax.ShapeDtypeStruct(s.shape, s.dtype) for s in srcs],
        scratch_shapes=[pltpu.SemaphoreType.DMA((n, 3)), pltpu.SemaphoreType.DMA((n, 3)), pltpu.SemaphoreType.DMA((n,))],
        name=name)(*srcs)


def _chip_sums(grads, tag):
    views = [g.reshape(N_DEV // 2, 2, g.shape[0] // N_DEV, g.shape[1]) for g in grads]
    from_sibling = _scatter_pair(views, name="scatter_pair_" + tag)
    return [_pair_add(v, s, name="pair_add") for v, s in zip(views, from_sibling)]


def _sum_chips(parts):
    return _rowmap(_sum_parts_fn, [], stacks=[parts], row_outs=[(parts.shape[2], F32)], tr=128, name="sum_chips")[0]


def _buffer_roles(kinds, bufs):
    carried = [k for k in kinds if k in bufs]
    return carried, [k for k in kinds if k not in bufs]


def _gather_ride(sends, forwards, bufs):
    carried, created = _buffer_roles(list(dict.fromkeys([s[0] for s in sends] + [f[0] for f in forwards])), bufs)
    shape_of = {s[0]: jax.ShapeDtypeStruct((N_DEV,) + s[1].shape, s[1].dtype) for s in sends}
    order = carried + created

    def copies(in_refs, buf_refs, sems):
        x, y, c, chips = _mesh_place()
        buf = dict(zip(order, buf_refs))
        out, s0 = [], 0
        for (kind, _, r0, nr), src in zip(sends, in_refs):
            mine, dst = src.at[pl.ds(r0, nr)], buf[kind].at[4 * x + 2 * y + c, pl.ds(r0, nr)]
            out.append(pltpu.make_async_copy(mine, dst, sems.at[s0 + 8]))
            for k, peer in enumerate([(x, y, 1 - c)] + [(*chip, c) for chip in chips]):
                out.append(pltpu.make_async_remote_copy(src_ref=mine, dst_ref=dst, send_sem=sems.at[s0 + k],
                                                        recv_sem=sems.at[s0 + 4 + k], device_id=peer, device_id_type=_MESH_ID))
            s0 += 9
        for kind, r0, nr in forwards:
            for j, chip in enumerate(chips):
                blk = buf[kind].at[4 * chip[0] + 2 * chip[1] + c, pl.ds(r0, nr)]
                out.append(pltpu.make_async_remote_copy(src_ref=blk, dst_ref=blk, send_sem=sems.at[s0 + j],
                                                        recv_sem=sems.at[s0 + 3 + j], device_id=(x, y, 1 - c),
                                                        device_id_type=_MESH_ID))
            s0 += 6
        return out

    def start(in_refs, buf_refs, sems):
        for cp in copies(in_refs, buf_refs, sems):
            cp.start()

    def finish(in_refs, buf_refs, sems):
        for cp in copies(in_refs, buf_refs, sems):
            cp.wait()

    ride = _Ride(inputs=tuple(s[1] for s in sends), carried=tuple(bufs[k] for k in carried),
                 created=tuple(shape_of[k] for k in created), n_sems=9 * len(sends) + 6 * len(forwards),
                 start=start, finish=finish)
    return ride, order


def _scatter_ride(pieces, bufs):
    carried, created = _buffer_roles(list(dict.fromkeys(p[0] for p in pieces)), bufs)
    shape_of = {p[0]: jax.ShapeDtypeStruct(p[1].shape, p[1].dtype) for p in pieces}
    order = carried + created

    def copies(in_refs, buf_refs, sems):
        x, y, c, chips = _mesh_place()
        buf = dict(zip(order, buf_refs))
        out, s0 = [], 0
        for (kind, _, r0, nr), src in zip(pieces, in_refs):
            dst = buf[kind].at[2 * x + y, pl.ds(r0, nr)]
            out.append(pltpu.make_async_copy(src.at[2 * x + y, pl.ds(r0, nr)], dst, sems.at[s0 + 6]))
            for j, chip in enumerate(chips):
                out.append(pltpu.make_async_remote_copy(
                    src_ref=src.at[2 * chip[0] + chip[1], pl.ds(r0, nr)], dst_ref=dst, send_sem=sems.at[s0 + j],
                    recv_sem=sems.at[s0 + 3 + j], device_id=(*chip, c), device_id_type=_MESH_ID))
            s0 += 7
        return out

    def start(in_refs, buf_refs, sems):
        for cp in copies(in_refs, buf_refs, sems):
            cp.start()

    def finish(in_refs, buf_refs, sems):
        for cp in copies(in_refs, buf_refs, sems):
            cp.wait()

    ride = _Ride(inputs=tuple(p[1] for p in pieces), carried=tuple(bufs[k] for k in carried),
                 created=tuple(shape_of[k] for k in created), n_sems=7 * len(pieces), start=start, finish=finish)
    return ride, order


GATHER_PLAN = (
    ("mm_in_qkv", (("w_in", 0, 3),)),
    ("mm_in_zb", (("w_pa", 0, 1), ("w_pb", 0, 1), ("w_pc", 0, 1), ("w_glu", 0, 1))),
    ("mm_in_gl", (("w_in", 1, 3),)),
    ("attn_fwd0", (("w_ffn_out", 0, 2),)),
    ("attn_fwd1", (("w_ffn_out", 1, 2),)),
    ("mm_o", (("w_o", 0, 1),)),
    ("mm_ffn_in", (("w_in", 2, 3), ("w_ffn_in", 0, 2))),
    ("mm_ffn_out", (("w_ffn_in", 1, 2),)),
    ("norm2", ()),
)
EARLY_KINDS = ("w_ffn_out", "w_ffn_in")
LATE_KINDS = tuple(n for n in SHARDED if n not in EARLY_KINDS)
SCATTER_EARLY_PLAN = (
    ("mm_in_dw_qkv", (("w_ffn_out", 0, 2),)),
    ("mm_in_dx_qkv", (("w_ffn_out", 1, 2),)),
    ("mm_in_dx_zb", (("w_ffn_in", 3, 4),)),
    ("mm_in_dw_gl", (("w_ffn_in", 0, 4),)),
    ("mm_in_dx_gl", (("w_ffn_in", 1, 4), ("w_ffn_in", 2, 4))),
)
SCATTER_PLAN = (
    ("mm_ffn_out_dw", (("w_in", 0, 3),)),
    ("mm_ffn_out_dx", (("w_in", 1, 3),)),
    ("mm_ffn_in_dw", (("w_in", 2, 3), ("w_o", 0, 1), ("w_pa", 0, 1), ("w_pb", 0, 1), ("w_pc", 0, 1), ("w_glu", 0, 1))),
)


def _row_part(rows, part, parts):
    assert rows % (parts * 2 * SUBLANES) == 0
    return part * (rows // parts), rows // parts


class _Carried:
    def __init__(self, plan, blocks, make_ride, forwards_too):
        self.plan, self.blocks, self.make_ride, self.forwards_too = dict(plan), blocks, make_ride, forwards_too
        self.bufs, self.to_forward, self.order = {}, [], []

    def ride(self, host):
        if self.blocks is None or host not in self.plan:
            self.order = []
            return None
        sends = [(k, self.blocks[k], *_row_part(self.blocks[k].shape[-2], part, parts)) for k, part, parts in self.plan[host]]
        if self.forwards_too:
            ride, self.order = self.make_ride(sends, self.to_forward, self.bufs)
            self.to_forward = [(k, r0, nr) for k, _, r0, nr in sends]
        else:
            ride, self.order = self.make_ride(sends, self.bufs)
        return ride

    def took(self, ride_outs):
        for k, buf in zip(self.order, ride_outs[0] if ride_outs else []):
            self.bufs[k] = buf


def _hosted(comm, fn, *args, name, **kwargs):
    res, ride_outs = fn(*args, name=name, rides=[comm.ride(name)], **kwargs)
    comm.took(ride_outs)
    return res


def _small_sizes(shapes):
    return [int(np.prod(shapes[n])) for n in SMALL]


def _pack_small(vals):
    flat = jnp.concatenate([vals[n].reshape(-1).astype(F32) for n in SMALL])
    rows = -(-flat.shape[0] // (LANES * N_DEV * SUBLANES)) * (N_DEV * SUBLANES)
    return jnp.pad(flat, (0, rows * LANES - flat.shape[0])).reshape(rows, LANES)


def _unpack_small(packed, shapes):
    flat = packed.reshape(-1)
    out, off = {}, 0
    for n, size in zip(SMALL, _small_sizes(shapes)):
        out[n] = flat[off:off + size].reshape(shapes[n])
        off += size
    return out


def _row(v):
    return v.reshape(1, -1)


def _layer_params(l, full, small):
    o1, o2, o3 = 3 * QKV_WIDTH, 3 * QKV_WIDTH + 2 * WIDTH_B, 3 * QKV_WIDTH + 2 * WIDTH_B + WIDTH_C
    b_in = small["b_in"][l]
    p = {
        "in_pieces": (("qkv", 0, o1), ("zb", o1, o2 - o1), ("uc", o2, o3 - o2), ("gl", o3, b_in.shape[0] - o3)),
        "b_qkv": _row(b_in[:o1]), "b_zb": _row(b_in[o1:o2]), "b_uc": _row(b_in[o2:o3]), "b_gl": _row(b_in[o3:]),
        "sgu_ln_g": _row(small["sgu_ln_g"][l]), "sgu_ln_b": _row(small["sgu_ln_b"][l]),
        "w_s": small["w_s"][l], "b_s_t": small["b_s"][l].T,
        "lam_re": small["lam_re"][l][:, None, :], "lam_im": small["lam_im"][l][:, None, :],
        "log_dt": small["log_dt"][l][:, None, None],
        "b_re_t": small["b_re"][l].transpose(0, 2, 1), "b_im_t": small["b_im"][l].transpose(0, 2, 1),
        "c_re": small["c_re"][l], "c_im": small["c_im"][l],
        "d_skip": _row(small["d_skip"][l]), "b_glu": _row(small["b_glu"][l]),
        "ln1_g": _row(small["ln1_g"][l]), "ln1_b": _row(small["ln1_b"][l]),
        "ln2_g": _row(small["ln2_g"][l]), "ln2_b": _row(small["ln2_b"][l]),
    }
    for n in SHARDED:
        p[n] = full[n]
    return p


def _twice(fn):
    def both(*args):
        y = fn(*args)
        return y, y
    return both


def _layer_fwd(x, xb, p, biases, comm):
    t, d = x.shape
    r = {"x": x, "xb": xb}
    for piece, off, n in p["in_pieces"]:
        r[piece] = _hosted(comm, _mm, xb, p["w_in"], tb=True, b_off=off, n=n, bias=p["b_" + piece],
                           out_dtype=BF16 if piece == "qkv" else F32, name="mm_in_" + piece)
    ol = []
    for g, dil in enumerate(ATT_DILATIONS):
        ol += list(_attn_fwd(r["qkv"], biases[g], g, dil, comm))
    r["ol"] = ol
    r["ya"], r["ya_b"] = _rowmap(_twice(_combine), ol, row_outs=[(WIDTH_A, F32), (WIDTH_A, BF16)], tr=512,
                                 name="attn_combine")
    r["yb"] = _gmlp_fwd(r["zb"], p["sgu_ln_g"], p["sgu_ln_b"], p["w_s"], p["b_s_t"])
    ab_re, ab_im, bb_re_t, bb_im_t = _s5_disc_fwd(p["lam_re"], p["lam_im"], p["log_dt"], p["b_re_t"], p["b_im_t"])
    r["a_re"], r["a_im"] = ab_re.reshape(_SCAN_ROWS, LANES), ab_im.reshape(_SCAN_ROWS, LANES)
    r["s5_mats"] = _block_diag([bb_re_t, bb_im_t, p["c_re"], p["c_im"]])
    r["xr"], r["xi"], r["ys"], r["ycp"] = _s5_fwd(r["uc"], r["s5_mats"], r["a_re"], r["a_im"], p["d_skip"])
    r["glin"] = _mm(r["ycp"], p["w_glu"], bias=p["b_glu"], name="mm_glu")
    r["yc"] = _rowmap(_glu, [r["ycp"], r["glin"]], row_outs=[(WIDTH_C, BF16)], tr=512, name="glu")[0]
    r["pa"] = _mm(r["ya_b"], p["w_pa"], tb=True, name="mm_pa")
    r["pb"] = _mm(r["yb"], p["w_pb"], tb=True, name="mm_pb")
    r["pc"] = _mm(r["yc"], p["w_pc"], tb=True, name="mm_pc")
    r["merged"] = _rowmap(_merge, [r["gl"], r["pa"], r["pb"], r["pc"]], row_outs=[(d, BF16)], name="merge")[0]
    r["mo"] = _hosted(comm, _mm, r["merged"], p["w_o"], name="mm_o")
    r["xm"], r["xm_b"] = _rowmap(_twice(_post_norm), [x, r["mo"]], consts=[p["ln1_g"], p["ln1_b"]],
                                 row_outs=[(d, F32), (d, BF16)], name="norm1")
    r["gate"], r["up"], r["act"] = _hosted(comm, _mm_swiglu, r["xm_b"], p["w_ffn_in"], name="mm_ffn_in")
    r["f"] = _hosted(comm, _mm, r["act"], p["w_ffn_out"], name="mm_ffn_out")
    out, out_b = _hosted(comm, _rowmap, _twice(_post_norm), [r["xm"], r["f"]], consts=[p["ln2_g"], p["ln2_b"]],
                         row_outs=[(d, F32), (d, BF16)], name="norm2")
    return out, out_b, r


def _layer_bwd(dout, r, p, biases, consts, comm, make_early):
    t, d = dout.shape
    gw, gs = {}, {}
    ffw = 2 * r["gate"].shape[1]
    dxm, df, gs["ln2_g"], gs["ln2_b"] = _rowmap(
        _post_norm_bwd, [r["xm"], r["f"], dout], consts=[p["ln2_g"], p["ln2_b"]],
        row_outs=[(d, F32), (d, BF16)], red_outs=[(1, d)] * 2, name="norm2_bwd")
    gw["w_ffn_out"] = _hosted(comm, _mm, r["act"], df, ta=True, out_dtype=BF16, name="mm_ffn_out_dw")
    dact = _hosted(comm, _mm, df, p["w_ffn_out"], tb=True, name="mm_ffn_out_dx")
    dgu = _rowmap(_swiglu_bwd, [r["gate"], r["up"], dact], row_outs=[(ffw, BF16)], tr=128, name="swiglu_bwd")[0]
    gw["w_ffn_in"] = _hosted(comm, _mm, dgu, r["xm_b"], ta=True, out_dtype=BF16, name="mm_ffn_in_dw")
    dxm = _hosted(comm, _mm, dgu, p["w_ffn_in"], add=dxm, name="mm_ffn_in_dx")
    early = make_early({n: gw[n] for n in EARLY_KINDS})
    dx, dmo, gs["ln1_g"], gs["ln1_b"] = _rowmap(
        _post_norm_bwd, [r["x"], r["mo"], dxm], consts=[p["ln1_g"], p["ln1_b"]],
        row_outs=[(d, F32), (d, BF16)], red_outs=[(1, d)] * 2, name="norm1_bwd")
    gw["w_o"] = _hosted(comm, _mm, r["merged"], dmo, ta=True, out_dtype=BF16, name="mm_o_dw")
    dmerged = _hosted(comm, _mm, dmo, p["w_o"], tb=True, name="mm_o_dx")
    dgl, dpa, dpb, dpc, db_gl = _rowmap(
        _merge_bwd, [r["gl"], r["pa"], r["pb"], r["pc"], dmerged],
        row_outs=[(3 * d, BF16), (d, BF16), (d, BF16), (d, BF16)], red_outs=[(1, 3 * d)], tr=128, name="merge_bwd")
    gw["w_pa"] = _mm(dpa, r["ya_b"], ta=True, out_dtype=BF16, name="mm_pa_dw")
    gw["w_pb"] = _mm(dpb, r["yb"], ta=True, out_dtype=BF16, name="mm_pb_dw")
    gw["w_pc"] = _mm(dpc, r["yc"], ta=True, out_dtype=BF16, name="mm_pc_dw")
    dya = _mm(dpa, p["w_pa"], name="mm_pa_dx")
    dyb = _mm(dpb, p["w_pb"], name="mm_pb_dx")
    dyc = _mm(dpc, p["w_pc"], name="mm_pc_dx")
    dycp, dglin, gs["b_glu"] = _rowmap(_glu_bwd, [r["ycp"], r["glin"], dyc], row_outs=[(WIDTH_C, F32), (WIDTH_C, BF16)],
                                       red_outs=[(1, WIDTH_C)], tr=512, name="glu_bwd")
    gw["w_glu"] = _mm(r["ycp"], dglin, ta=True, out_dtype=BF16, name="mm_glu_dw")
    dycp = _mm(dglin, p["w_glu"], tb=True, add=dycp, name="mm_glu_dx")
    dys, duc, gs["d_skip"] = _rowmap(_s5_out_bwd, [r["ys"], r["uc"], dycp], consts=[p["d_skip"]],
                                     row_outs=[(WIDTH_C, F32)] * 2, red_outs=[(1, WIDTH_C)], tr=512, name="s5_out_act_bwd")
    duc, d_bmat_re, d_bmat_im, d_cmat_re, d_cmat_im, da_re, da_im = _s5_bwd(
        dys, duc, r["uc"], r["xr"], r["xi"], r["s5_mats"], r["a_re"], r["a_im"])
    d_bb_re_t, d_bb_im_t, gs["c_re"], gs["c_im"] = _block_diag_extract(
        [d_bmat_re, d_bmat_im, d_cmat_re, d_cmat_im], SSM_GROUP, SSM_STATE)
    cts = (da_re.reshape(N_GROUPS_C, 1, SSM_STATE), da_im.reshape(N_GROUPS_C, 1, SSM_STATE), d_bb_re_t, d_bb_im_t)
    d_lr, d_li, d_ldt, d_br_t, d_bi_t = _s5_disc_bwd(p["lam_re"], p["lam_im"], p["log_dt"], p["b_re_t"], p["b_im_t"], cts)
    gs["lam_re"], gs["lam_im"], gs["log_dt"] = d_lr[:, 0, :], d_li[:, 0, :], d_ldt[:, 0, 0]
    gs["b_re"], gs["b_im"] = d_br_t.transpose(0, 2, 1), d_bi_t.transpose(0, 2, 1)
    dzb, db_zb, gs["sgu_ln_g"], gs["sgu_ln_b"], gs["w_s"], dbs_t = _gmlp_bwd(
        r["zb"], dyb, p["sgu_ln_g"], p["sgu_ln_b"], p["w_s"], p["b_s_t"], consts["group_sel"])
    gs["b_s"] = dbs_t[:, :N_GROUPS_B].T
    do_corr = _rowmap(_combine_bwd, r["ol"] + [r["ya"], dya], consts=[consts["head_ones"]],
                      row_outs=[(WIDTH_A, F32)] * 6, tr=512, name="attn_combine_bwd")
    dq, dk, dv, dbias = [], [], [], []
    for g, dil in enumerate(ATT_DILATIONS):
        do_g, corr_g, lse_g = do_corr[g], do_corr[3 + g], r["ol"][2 * g + 1]
        dq_g, dk_g, dv_g, db_g = _attn_bwd(r["qkv"], biases[g], do_g, lse_g, corr_g, g, dil)
        dq.append(dq_g)
        dk.append(dk_g)
        dv.append(dv_g)
        dbias.append(db_g)
    cast_colsum = lambda a: (a, jnp.sum(a, axis=0, keepdims=True))
    dqkv, db_qkv = _rowmap(cast_colsum, [jnp.concatenate(dq + dk + dv, axis=1)], row_outs=[(3 * QKV_WIDTH, BF16)],
                           red_outs=[(1, 3 * QKV_WIDTH)], tr=512, name="cast_colsum_qkv")
    duc, db_uc = _rowmap(cast_colsum, [duc], row_outs=[(WIDTH_C, BF16)], red_outs=[(1, WIDTH_C)], tr=512,
                         name="cast_colsum_uc")
    dpieces = {"qkv": dqkv, "zb": dzb, "uc": duc, "gl": dgl}
    rows_in = p["w_in"].shape[0]
    dw_in = None
    for piece, off, n in p["in_pieces"]:
        dw_in = _hosted(early, _mm, dpieces[piece], r["xb"], ta=True, out_dtype=BF16, into=(rows_in, off, dw_in),
                        name="mm_in_dw_" + piece)
        dx = _hosted(early, _mm, dpieces[piece], p["w_in"], b_off=off, add=dx, name="mm_in_dx_" + piece)
    gw["w_in"] = dw_in
    gs["b_in"] = jnp.concatenate([db_qkv, db_zb, db_uc, db_gl], axis=1)[0]
    for n in ("sgu_ln_g", "sgu_ln_b", "d_skip", "b_glu", "ln1_g", "ln1_b", "ln2_g", "ln2_b"):
        gs[n] = gs[n][0]
    return dx, gw, gs, dbias, early.bufs


def _cast_bf16(w):
    w2 = w.reshape(-1, w.shape[-1])
    out = _rowmap(lambda a: a, [w2], row_outs=[(w2.shape[1], BF16)], tr=512, name="cast_bf16")[0]
    return out.reshape(w.shape)


def _static_consts():
    head_ones = np.kron(np.eye(HEADS_PER_GROUP, dtype=np.float32), np.ones((HEAD_DIM, HEAD_DIM), np.float32))
    group_sel = np.zeros((WIDTH_B, LANES), np.float32)
    group_sel[np.arange(WIDTH_B), np.arange(WIDTH_B) // CHUNK] = 1.0
    return {"head_ones": jnp.asarray(head_ones), "group_sel": jnp.asarray(group_sel)}


def _step(x, tgt, w, m, v):
    shapes = {n: w[n].shape for n in WEIGHTS}
    consts = _static_consts()
    mine_bf = {n: _cast_bf16(w[n].transpose(0, 2, 1) if n in TRANSPOSED else w[n]) for n in SHARDED}
    small = {n: w[n] for n in SMALL}
    buckets = [jnp.asarray(_bucket_table(dil)) for dil in ATT_DILATIONS]
    biases = [_bias_fwd(w["rel_bias"], buckets[g], g) for g in range(len(ATT_DILATIONS))]
    params, saved = [], []
    h, hb = _rowmap(_twice(lambda a: a), [x], row_outs=[(x.shape[1], F32), (x.shape[1], BF16)], name="cast_x")
    gathered = dict(zip(SHARDED, _gather_layer([mine_bf[n] for n in SHARDED], 0, name="gather_layer0")))
    for l in range(DEPTH):
        p = _layer_params(l, {n: g.reshape(-1, g.shape[2]) for n, g in gathered.items()}, small)
        ahead = _Carried(GATHER_PLAN, {n: mine_bf[n][l + 1] for n in SHARDED} if l + 1 < DEPTH else None,
                         _gather_ride, forwards_too=True)
        h, hb, r = _layer_fwd(h, hb, p, biases, ahead)
        gathered = ahead.bufs
        params.append(p)
        saved.append(r)
    dy, loss_part = _rowmap(_loss_fn, [h, tgt], row_outs=[(h.shape[1], F32)], red_outs=[(1, LANES)], name="loss")
    loss = lax.psum(loss_part[0, 0], MESH_AXES)
    g_mine, gs_layers = {n: [None] * DEPTH for n in SHARDED}, [None] * DEPTH
    dbias_sum = None
    def scatter_steps(plan, grads, tag):
        sums = _chip_sums(list(grads.values()), tag)
        return _Carried(plan, dict(zip(grads, sums)), _scatter_ride, forwards_too=False)

    behind = _Carried(SCATTER_PLAN, None, _scatter_ride, forwards_too=False)
    for l in reversed(range(DEPTH)):
        dy, gw, gs_layers[l], dbias, early_parts = _layer_bwd(
            dy, saved[l], params[l], biases, consts, behind,
            functools.partial(scatter_steps, SCATTER_EARLY_PLAN, tag="early"))
        saved[l] = None
        for n in EARLY_KINDS:
            g_mine[n][l] = _sum_chips(early_parts[n])
        if behind.blocks is not None:
            for n in LATE_KINDS:
                g_mine[n][l + 1] = _sum_chips(behind.bufs[n])
        behind = scatter_steps(SCATTER_PLAN, {n: gw[n] for n in LATE_KINDS}, "late")
        if l == 0:
            last = _scatter_chips([behind.blocks[n] for n in LATE_KINDS], name="scatter_chips_layer0")
            for n, parts in zip(LATE_KINDS, last):
                g_mine[n][0] = _sum_chips(parts)
        if dbias_sum is None:
            dbias_sum = dbias
        else:
            dbias_sum = [_rowmap(lambda a, b: a + b, [a.reshape(-1, 2 * ATT_BLOCK), b.reshape(-1, 2 * ATT_BLOCK)],
                                 row_outs=[(2 * ATT_BLOCK, F32)], name="dbias_add")[0].reshape(a.shape)
                         for a, b in zip(dbias_sum, dbias)]
    drel = [_bias_bwd(dbias_sum[g], buckets[g], g) for g in range(len(ATT_DILATIONS))]
    drel = _rowmap(lambda a, b, c: a + b + c, drel, row_outs=[(LANES, F32)], name="drel_add")[0]
    grad_small_local = {n: jnp.stack([gs_layers[l][n] for l in range(DEPTH)]) for n in SMALL if n != "rel_bias"}
    grad_small_local["rel_bias"] = drel[:, :shapes["rel_bias"][1]]
    out_g, out_d, out_m, out_v = {}, {}, {}, {}
    for n in SHARDED:
        g = jnp.stack(g_mine[n])
        out_g[n] = g.transpose(0, 2, 1) if n in TRANSPOSED else g
        cols = shapes[n][-1]
        res = _rowmap(_adamw, [a.reshape(-1, cols) for a in (w[n], out_g[n], m[n], v[n])],
                      row_outs=[(cols, F32)] * 3, tr=128, name="adamw_" + n)
        out_d[n], out_m[n], out_v[n] = [a.reshape(shapes[n]) for a in res]
    packed = _pack_small(grad_small_local)
    rows = packed.shape[0] // N_DEV
    parts = _exchange(packed.reshape(N_DEV, rows, LANES), gather=False, name="scatter_small")
    mine = _rowmap(_sum_parts_fn, [], stacks=[parts], row_outs=[(LANES, F32)], name="sum_small")[0]
    g_small = _exchange(mine, gather=True, name="gather_small").reshape(-1, LANES)
    res = _rowmap(lambda w_, g_, m_, v_: _adamw(w_, g_, m_, v_),
                  [_pack_small(w), g_small, _pack_small(m), _pack_small(v)],
                  row_outs=[(LANES, F32)] * 3, name="adamw_small")
    small_shapes = {n: shapes[n] for n in SMALL}
    out_g.update(_unpack_small(g_small, small_shapes))
    for dst, packed_res in zip((out_d, out_m, out_v), res):
        dst.update(_unpack_small(packed_res, small_shapes))
    return loss, dy, out_g, out_d, out_m, out_v


def kernel(x, w_in, b_in, rel_bias, sgu_ln_g, sgu_ln_b, w_s, b_s, lam_re, lam_im, log_dt, b_re, b_im, c_re, c_im, d_skip, w_glu, b_glu, w_pa, w_pb, w_pc, w_o, ln1_g, ln1_b, w_ffn_in, w_ffn_out, ln2_g, ln2_b, loss_target, m_w_in, m_b_in, m_rel_bias, m_sgu_ln_g, m_sgu_ln_b, m_w_s, m_b_s, m_lam_re, m_lam_im, m_log_dt, m_b_re, m_b_im, m_c_re, m_c_im, m_d_skip, m_w_glu, m_b_glu, m_w_pa, m_w_pb, m_w_pc, m_w_o, m_ln1_g, m_ln1_b, m_w_ffn_in, m_w_ffn_out, m_ln2_g, m_ln2_b, v_w_in, v_b_in, v_rel_bias, v_sgu_ln_g, v_sgu_ln_b, v_w_s, v_b_s, v_lam_re, v_lam_im, v_log_dt, v_b_re, v_b_im, v_c_re, v_c_im, v_d_skip, v_w_glu, v_b_glu, v_w_pa, v_w_pb, v_w_pc, v_w_o, v_ln1_g, v_ln1_b, v_w_ffn_in, v_w_ffn_out, v_ln2_g, v_ln2_b):
    args = dict(locals())
    w = {n: args[n] for n in WEIGHTS}
    m = {n: args["m_" + n] for n in WEIGHTS}
    v = {n: args["v_" + n] for n in WEIGHTS}
    loss, dx, g, d, nm, nv = _step(x[0], loss_target[0], w, m, v)
    return (loss, dx[None], *[g[n] for n in WEIGHTS], *[d[n] for n in WEIGHTS],
            *[nm[n] for n in WEIGHTS], *[nv[n] for n in WEIGHTS])
```

```python
import functools
import math
from typing import Callable, NamedTuple

import numpy as np
import jax
import jax.numpy as jnp
from jax import lax
from jax.experimental import pallas as pl
from jax.experimental.pallas import tpu as pltpu

F32 = jnp.float32
BF16 = jnp.bfloat16

MESH_AXES = ("x", "y", "c")
N_DEV = 8
DEPTH = 4

ATT_DILATIONS = (1, 4, 16)
ATT_STEPS = 128
HEADS_PER_GROUP = 8
HEAD_DIM = 64
QKV_WIDTH = 1536
WIDTH_A = HEADS_PER_GROUP * HEAD_DIM
ATT_BLOCK = 128
N_REL_BUCKETS = 32
REL_MAX_DIST = 2048
NEG_INF = -1e30
CHUNK = 128
WIDTH_B = 768
N_GROUPS_B = 6
WIDTH_C = 768
SSM_GROUP = 16
N_GROUPS_C = 48
SSM_STATE = 64
SSM_PACK = 8
N_SSM_BLOCKS = N_GROUPS_C // SSM_PACK
SSM_COLS = N_GROUPS_C * SSM_STATE
ALPHA = (2 * DEPTH) ** 0.25

ADAM_LR = 0.001
ADAM_B1 = 0.9
ADAM_B2 = 0.999
ADAM_EPS = 1e-08
ADAM_WD = 0.01
ADAM_STEP = 10

LANES = 128
SUBLANES = 8
VMEM_LIMIT = 48 * 1024 * 1024

SHARDED = ("w_in", "w_glu", "w_pa", "w_pb", "w_pc", "w_o", "w_ffn_in", "w_ffn_out")
TRANSPOSED = ("w_in", "w_pa", "w_pb", "w_pc", "w_ffn_in")
SMALL = ("b_in", "rel_bias", "sgu_ln_g", "sgu_ln_b", "w_s", "b_s", "lam_re", "lam_im", "log_dt",
         "b_re", "b_im", "c_re", "c_im", "d_skip", "b_glu", "ln1_g", "ln1_b", "ln2_g", "ln2_b")
WEIGHTS = ("w_in", "b_in", "rel_bias", "sgu_ln_g", "sgu_ln_b", "w_s", "b_s", "lam_re", "lam_im",
           "log_dt", "b_re", "b_im", "c_re", "c_im", "d_skip", "w_glu", "b_glu", "w_pa", "w_pb",
           "w_pc", "w_o", "ln1_g", "ln1_b", "w_ffn_in", "w_ffn_out", "ln2_g", "ln2_b")


def _pick(dim, target, mult):
    best = None
    for t in range(mult, min(dim, target) + 1, mult):
        if dim % t == 0:
            best = t
    return dim if best is None else best


def _cparams(*sem):
    return pltpu.CompilerParams(dimension_semantics=sem, vmem_limit_bytes=VMEM_LIMIT)


def _zero_map(ndim):
    return lambda *_: (0,) * ndim


_HBM = pl.BlockSpec(memory_space=pl.ANY)
_MESH_ID = pl.DeviceIdType.MESH


class _Ride(NamedTuple):
    inputs: tuple
    carried: tuple
    created: tuple
    n_sems: int
    start: Callable
    finish: Callable


def _pallas(body, *, grid, in_specs, out_specs, out_shape, args, scratch=(), semantics, rides=(), aliases=None, name):
    rides = [r for r in rides if r is not None]
    n_in, n_out, n_scr = len(args), len(out_shape), len(scratch)
    r_args, r_shapes, aliases, spans = [], [], dict(aliases or {}), []
    for r in rides:
        i0, o0 = len(r_args), len(r_shapes)
        r_args += [*r.inputs, *r.carried]
        for k, a in enumerate(r.carried):
            aliases[n_in + i0 + len(r.inputs) + k] = n_out + o0 + k
        r_shapes += [jax.ShapeDtypeStruct(a.shape, a.dtype) for a in r.carried] + list(r.created)
        spans.append((i0, len(r.inputs), o0, len(r.carried) + len(r.created)))

    def full_body(*refs):
        host_in, ride_in = refs[:n_in], refs[n_in:n_in + len(r_args)]
        p = n_in + len(r_args)
        host_out, ride_out = refs[p:p + n_out], refs[p + n_out:p + n_out + len(r_shapes)]
        p += n_out + len(r_shapes)
        host_scr, ride_sems = refs[p:p + n_scr], refs[p + n_scr:]
        ids = [pl.program_id(k) for k in range(len(grid))]
        first = functools.reduce(jnp.logical_and, [i == 0 for i in ids])
        last = functools.reduce(jnp.logical_and, [i == g - 1 for i, g in zip(ids, grid)])

        def each(method):
            for r, (i0, ni, o0, no), sems in zip(rides, spans, ride_sems):
                getattr(r, method)(ride_in[i0:i0 + ni], ride_out[o0:o0 + no], sems)

        if rides:
            pl.when(first)(lambda: each("start"))
        body(*host_in, *host_out, *host_scr)
        if rides:
            pl.when(last)(lambda: each("finish"))

    if rides:
        semantics = ("arbitrary",) * len(grid)
    outs = pl.pallas_call(
        full_body, grid=grid, in_specs=list(in_specs) + [_HBM] * len(r_args),
        out_specs=list(out_specs) + [_HBM] * len(r_shapes), out_shape=list(out_shape) + r_shapes,
        scratch_shapes=list(scratch) + [pltpu.SemaphoreType.DMA((r.n_sems,)) for r in rides],
        input_output_aliases=aliases, compiler_params=_cparams(*semantics), name=name)(*args, *r_args)
    return outs[:n_out], [outs[n_out + o0:n_out + o0 + no] for _, _, o0, no in spans]


def _rowmap(fn, rows, consts=(), stacks=(), row_outs=(), red_outs=(), tr=256, name=None, rides=None):
    t = rows[0].shape[0] if rows else stacks[0].shape[1]
    dtypes = [a.dtype for a in (*rows, *stacks)] + [dt for _, dt in row_outs]
    packed = any(jnp.dtype(dt).itemsize < 4 for dt in dtypes)
    tr = _pick(t, tr, 2 * SUBLANES if packed else SUBLANES)
    n_r, n_c, n_s, n_o = len(rows), len(consts), len(stacks), len(row_outs)

    def body(*refs):
        ins = [r[...] for r in refs[:n_r + n_c + n_s]]
        outs = refs[n_r + n_c + n_s:n_r + n_c + n_s + n_o]
        reds = refs[n_r + n_c + n_s + n_o:]
        res = fn(*ins)
        if not isinstance(res, (tuple, list)):
            res = (res,)
        for o, v in zip(outs, res[:n_o]):
            o[...] = v.astype(o.dtype)
        if reds:
            @pl.when(pl.program_id(0) == 0)
            def _():
                for r in reds:
                    r[...] = jnp.zeros_like(r)
            for r, v in zip(reds, res[n_o:]):
                r[...] += v

    in_specs = [pl.BlockSpec((tr, r.shape[1]), lambda i: (i, 0)) for r in rows]
    in_specs += [pl.BlockSpec(c.shape, _zero_map(c.ndim)) for c in consts]
    in_specs += [pl.BlockSpec((s.shape[0], tr, s.shape[2]), lambda i: (0, i, 0)) for s in stacks]
    out_specs = [pl.BlockSpec((tr, w), lambda i: (i, 0)) for w, _ in row_outs]
    out_specs += [pl.BlockSpec(s, _zero_map(len(s))) for s in red_outs]
    out_shape = [jax.ShapeDtypeStruct((t, w), dt) for w, dt in row_outs]
    out_shape += [jax.ShapeDtypeStruct(s, F32) for s in red_outs]
    outs, ride_outs = _pallas(body, grid=(t // tr,), in_specs=in_specs, out_specs=out_specs, out_shape=out_shape,
                              args=[*rows, *consts, *stacks], semantics=("arbitrary",), rides=rides or (), name=name)
    return outs if rides is None else (outs, ride_outs)


MM_VMEM_BUDGET = 36 * 1024 * 1024


def _divisors(dim, mult, must_divide=0):
    out = [t for t in range(dim, 0, -mult) if t % mult == 0 and dim % t == 0 and must_divide % t == 0]
    return out or [dim]


def _mm_tiles(m, n, k, a_bytes, b_bytes, out_bytes, extra_bytes, ta, b_off_n, b_off_k, out_off, tm, tn):
    tms = _divisors(m, LANES if ta else SUBLANES, out_off)
    tm = next((t for t in tms if t <= tm), tms[-1])
    tns = [t for t in _divisors(n, LANES, b_off_n) if t <= tn] or [_divisors(n, LANES, b_off_n)[-1]]
    for tn_ in tns:
        for tk in _divisors(k, LANES, b_off_k):
            acc = 0 if tk == k else tm * tn_ * 4
            need = 2 * (tm * tk * a_bytes + tk * tn_ * b_bytes + tm * tn_ * (out_bytes + extra_bytes)) + acc
            if need <= MM_VMEM_BUDGET:
                return tm, tn_, tk
    return tm, tns[-1], _divisors(k, LANES, b_off_k)[-1]


def _mm(a, b, *, ta=False, tb=False, bias=None, add=None, out_dtype=F32, b_off=0, n=None, tm=1024, tn=1024, name=None,
        rides=None, into=None):
    k, m = a.shape if ta else a.shape[::-1]
    if tb:
        n = b.shape[0] if n is None else n
        assert b.shape[1] == k and b_off + n <= b.shape[0]
    else:
        n = b.shape[1]
        assert b_off + k <= b.shape[0]
    out_rows, out_off, out_buf = (m, 0, None) if into is None else into
    extra = 4 if add is not None else 0
    tm, tn, tk = _mm_tiles(m, n, k, a.dtype.itemsize, b.dtype.itemsize, jnp.dtype(out_dtype).itemsize, extra, ta,
                           b_off if tb else 0, 0 if tb else b_off, out_off, tm, tn)
    nk = k // tk
    off_n, off_k = (b_off // tn, 0) if tb else (0, b_off // tk)
    off_m = out_off // tm
    dims = (((0 if ta else 1,), (1 if tb else 0,)), ((), ()))

    def body(*refs):
        a_ref, b_ref = refs[0], refs[1]
        rest = list(refs[2:])
        bias_ref = rest.pop(0) if bias is not None else None
        add_ref = rest.pop(0) if add is not None else None
        o_ref = rest.pop(0)
        part = lax.dot_general(a_ref[...].astype(BF16), b_ref[...].astype(BF16), dims, preferred_element_type=F32)

        def finish(r):
            if bias_ref is not None:
                r = r + bias_ref[...]
            if add_ref is not None:
                r = r + add_ref[...]
            o_ref[...] = r.astype(o_ref.dtype)

        if nk == 1:
            finish(part)
        else:
            acc_ref = rest.pop(0)
            kk = pl.program_id(2)

            @pl.when(kk == 0)
            def _():
                acc_ref[...] = part

            @pl.when(kk > 0)
            def _():
                acc_ref[...] += part

            @pl.when(kk == nk - 1)
            def _():
                finish(acc_ref[...])

    a_spec = pl.BlockSpec((tk, tm), lambda i, j, q: (q, i)) if ta else pl.BlockSpec((tm, tk), lambda i, j, q: (i, q))
    if tb:
        b_spec = pl.BlockSpec((tn, tk), lambda i, j, q: (j + off_n, q))
    else:
        b_spec = pl.BlockSpec((tk, tn), lambda i, j, q: (q + off_k, j))
    in_specs, args = [a_spec, b_spec], [a, b]
    if bias is not None:
        in_specs.append(pl.BlockSpec((1, tn), lambda i, j, q: (0, j)))
        args.append(bias)
    if add is not None:
        in_specs.append(pl.BlockSpec((tm, tn), lambda i, j, q: (i, j)))
        args.append(add)
    aliases = {}
    if out_buf is not None:
        assert out_buf.shape == (out_rows, n) and out_buf.dtype == jnp.dtype(out_dtype)
        in_specs.append(_HBM)
        args.append(out_buf)
        aliases = {len(args) - 1: 0}

    def body_in_place(*refs):
        body(*refs[:len(args) - 1], *refs[len(args):])

    outs, ride_outs = _pallas(
        body if out_buf is None else body_in_place, grid=(m // tm, n // tn, nk), in_specs=in_specs,
        out_specs=[pl.BlockSpec((tm, tn), lambda i, j, q: (i + off_m, j))],
        out_shape=[jax.ShapeDtypeStruct((out_rows, n), out_dtype)], args=args,
        scratch=[] if nk == 1 else [pltpu.VMEM((tm, tn), F32)],
        semantics=("parallel", "parallel", "arbitrary"), rides=rides or (), aliases=aliases, name=name)
    return outs[0] if rides is None else (outs[0], ride_outs)


def _mm_swiglu(a, w_t, *, name, rides=None):
    m, k = a.shape
    f = w_t.shape[0] // 2
    tm, tn, tk = _mm_tiles(m, f, k, a.dtype.itemsize, 2 * w_t.dtype.itemsize, 3 * 2, 0, False, 0, 0, 0, 1024, 512)
    assert tk == k, "the fused activation needs the whole contraction in one block"

    def body(a_ref, g_ref, u_ref, gate_ref, up_ref, act_ref):
        av = a_ref[...].astype(BF16)
        gate = lax.dot_general(av, g_ref[...].astype(BF16), _NT, preferred_element_type=F32)
        up = lax.dot_general(av, u_ref[...].astype(BF16), _NT, preferred_element_type=F32)
        gate_ref[...] = gate.astype(gate_ref.dtype)
        up_ref[...] = up.astype(up_ref.dtype)
        act_ref[...] = _swiglu2(gate, up).astype(act_ref.dtype)

    out_spec = pl.BlockSpec((tm, tn), lambda i, j: (i, j))
    outs, ride_outs = _pallas(
        body, grid=(m // tm, f // tn),
        in_specs=[pl.BlockSpec((tm, k), lambda i, j: (i, 0)), pl.BlockSpec((tn, k), lambda i, j: (j, 0)),
                  pl.BlockSpec((tn, k), lambda i, j: (j + f // tn, 0))],
        out_specs=[out_spec] * 3, out_shape=[jax.ShapeDtypeStruct((m, f), BF16)] * 3, args=[a, w_t, w_t],
        semantics=("parallel", "parallel"), rides=rides or (), name=name)
    return outs if rides is None else (outs, ride_outs)


def _ln(x, g, b, eps=1e-5):
    mu = jnp.mean(x, axis=-1, keepdims=True)
    var = jnp.mean(jnp.square(x - mu), axis=-1, keepdims=True)
    return (x - mu) * lax.rsqrt(var + eps) * g + b


def _post_norm(x, f, g, b):
    return _ln(ALPHA * x + f, g, b)


def _post_norm_bwd(x, f, dy, g, b):
    _, vjp = jax.vjp(_post_norm, x, f, g, b)
    return vjp(dy)


def _merge3(g0, g1, g2, pa, pb, pc):
    return jax.nn.sigmoid(g0) * pa + jax.nn.sigmoid(g1) * pb + jax.nn.sigmoid(g2) * pc


def _merge_args(gl, pa, pb, pc):
    d = pa.shape[1]
    return [a.astype(F32) for a in (gl[:, :d], gl[:, d:2 * d], gl[:, 2 * d:], pa, pb, pc)]


def _merge(gl, pa, pb, pc):
    return _merge3(*_merge_args(gl, pa, pb, pc))


def _merge_bwd(gl, pa, pb, pc, dm):
    _, vjp = jax.vjp(_merge3, *_merge_args(gl, pa, pb, pc))
    d0, d1, d2, dpa, dpb, dpc = vjp(dm)
    dgl = jnp.concatenate([d0, d1, d2], axis=1)
    return dgl, dpa, dpb, dpc, jnp.sum(dgl, axis=0, keepdims=True)


def _swiglu2(gate, up):
    return jax.nn.silu(gate) * up


def _swiglu_bwd(gate, up, dact):
    _, vjp = jax.vjp(_swiglu2, gate.astype(F32), up.astype(F32))
    dg, du = vjp(dact.astype(F32))
    return jnp.concatenate([dg, du], axis=1)


def _glu(ycp, lin):
    return ycp * jax.nn.sigmoid(lin)


def _glu_bwd(ycp, lin, dyc):
    _, vjp = jax.vjp(_glu, ycp, lin)
    dycp, dlin = vjp(dyc)
    return dycp, dlin, jnp.sum(dlin, axis=0, keepdims=True)


def _s5_out_bwd(ys, uc, dycp, dskip):
    _, vjp = jax.vjp(jax.nn.gelu, ys)
    dys = vjp(dycp)[0]
    return dys, dys * dskip, jnp.sum(dys * uc, axis=0, keepdims=True)


def _combine(o0, l0, o1, l1, o2, l2):
    m = jnp.maximum(jnp.maximum(l0, l1), l2)
    e0, e1, e2 = jnp.exp(l0 - m), jnp.exp(l1 - m), jnp.exp(l2 - m)
    s = e0 + e1 + e2
    return (e0 / s) * o0 + (e1 / s) * o1 + (e2 / s) * o2


def _combine_bwd(o0, l0, o1, l1, o2, l2, ya, dya, head_ones):
    m = jnp.maximum(jnp.maximum(l0, l1), l2)
    e0, e1, e2 = jnp.exp(l0 - m), jnp.exp(l1 - m), jnp.exp(l2 - m)
    s = e0 + e1 + e2
    dot_ya = jnp.dot(dya * ya, head_ones, precision=lax.Precision.HIGHEST, preferred_element_type=F32)
    w0, w1, w2 = e0 / s, e1 / s, e2 / s
    return w0 * dya, w1 * dya, w2 * dya, -w0 * dot_ya, -w1 * dot_ya, -w2 * dot_ya


def _loss_fn(y, tgt):
    err = y - tgt
    part = jnp.sum(jnp.sum(jnp.square(err), axis=1, keepdims=True), axis=0, keepdims=True) * (0.5 / y.shape[1])
    return err * (1.0 / y.shape[1]), jnp.broadcast_to(part, (1, LANES))


def _adamw(w, g, m, v):
    m = ADAM_B1 * m + (1.0 - ADAM_B1) * g
    v = ADAM_B2 * v + (1.0 - ADAM_B2) * jnp.square(g)
    m_hat = m / (1.0 - ADAM_B1 ** ADAM_STEP)
    v_hat = v / (1.0 - ADAM_B2 ** ADAM_STEP)
    delta = -ADAM_LR * (m_hat / (jnp.sqrt(v_hat) + ADAM_EPS) + ADAM_WD * w)
    return delta, m, v


def _sum_parts_fn(parts):
    g = parts[0].astype(F32)
    for j in range(1, parts.shape[0]):
        g = g + parts[j].astype(F32)
    return g


def _t5_bucket(dist):
    max_exact = N_REL_BUCKETS // 2
    d = np.maximum(dist, 1).astype(np.float32)
    scale = (N_REL_BUCKETS - max_exact) / math.log(REL_MAX_DIST / max_exact)
    large = max_exact + (np.log(d / max_exact) * scale).astype(np.int32)
    large = np.minimum(large, N_REL_BUCKETS - 1)
    return np.where(dist < max_exact, dist, large).astype(np.int32)


def _bucket_table(dilation):
    i = np.arange(ATT_BLOCK)[:, None]
    kk = np.arange(2 * ATT_BLOCK)[None, :]
    steps = ATT_BLOCK + i - kk
    return _t5_bucket(np.maximum(steps, 0) * dilation)


def _bias_fwd(rel_bias, buckets, g):
    def body(rel_ref, bk_ref, o_ref):
        bk = bk_ref[...]
        for h in range(HEADS_PER_GROUP):
            acc = jnp.zeros(bk.shape, F32)
            for b in range(N_REL_BUCKETS):
                acc = jnp.where(bk == b, rel_ref[b, g * HEADS_PER_GROUP + h], acc)
            o_ref[h] = acc

    return pl.pallas_call(
        body, in_specs=[pl.BlockSpec(memory_space=pltpu.SMEM), pl.BlockSpec(memory_space=pltpu.VMEM)],
        out_specs=pl.BlockSpec(memory_space=pltpu.VMEM),
        out_shape=jax.ShapeDtypeStruct((HEADS_PER_GROUP, ATT_BLOCK, 2 * ATT_BLOCK), F32),
        name=f"rel_bias_fwd{g}")(rel_bias, buckets)


def _bias_bwd(dbias, buckets, g):
    def body(db_ref, bk_ref, o_ref):
        bk = bk_ref[...]
        row = lax.broadcasted_iota(jnp.int32, (N_REL_BUCKETS, LANES), 0)
        col = lax.broadcasted_iota(jnp.int32, (N_REL_BUCKETS, LANES), 1)
        acc = jnp.zeros((N_REL_BUCKETS, LANES), F32)
        for h in range(HEADS_PER_GROUP):
            d = db_ref[h]
            for b in range(N_REL_BUCKETS):
                s = jnp.sum(jnp.sum(jnp.where(bk == b, d, 0.0), axis=1, keepdims=True), axis=0, keepdims=True)
                acc = acc + jnp.where((row == b) & (col == g * HEADS_PER_GROUP + h), s, 0.0)
        o_ref[...] = acc

    return pl.pallas_call(
        body, in_specs=[pl.BlockSpec(memory_space=pltpu.VMEM), pl.BlockSpec(memory_space=pltpu.VMEM)],
        out_specs=pl.BlockSpec(memory_space=pltpu.VMEM),
        out_shape=jax.ShapeDtypeStruct((N_REL_BUCKETS, LANES), F32), name=f"rel_bias_bwd{g}")(dbias, buckets)


_NT = (((1,), (1,)), ((), ()))
_TN = (((0,), (0,)), ((), ()))
_QKV_BLOCKS = 3 * QKV_WIDTH // WIDTH_A


def _band_mask(n_is_first):
    i = lax.broadcasted_iota(jnp.int32, (ATT_BLOCK, 2 * ATT_BLOCK), 0)
    kk = lax.broadcasted_iota(jnp.int32, (ATT_BLOCK, 2 * ATT_BLOCK), 1)
    return (kk >= i) & (kk <= i + ATT_STEPS) & ((kk >= ATT_BLOCK) | jnp.logical_not(n_is_first))


def _head(ref, h):
    return ref[:, h * HEAD_DIM:(h + 1) * HEAD_DIM]


def _attn_specs(g, d):
    blk = (ATT_BLOCK, WIDTH_A)
    q = pl.BlockSpec(blk, lambda c, n: (n, c * _QKV_BLOCKS + g))
    kp = pl.BlockSpec(blk, lambda c, n: (jnp.maximum(n - 1, 0), c * _QKV_BLOCKS + 3 + g))
    kc = pl.BlockSpec(blk, lambda c, n: (n, c * _QKV_BLOCKS + 3 + g))
    vp = pl.BlockSpec(blk, lambda c, n: (jnp.maximum(n - 1, 0), c * _QKV_BLOCKS + 6 + g))
    vc = pl.BlockSpec(blk, lambda c, n: (n, c * _QKV_BLOCKS + 6 + g))
    return [q, kp, kc, vp, vc]


def _attn_fwd(qkv, bias, g, d, comm):
    t = qkv.shape[0]
    lq = t // d
    nb = lq // ATT_BLOCK
    scale = HEAD_DIM ** -0.5

    def body(q_ref, kp_ref, kc_ref, vp_ref, vc_ref, b_ref, o_ref, l_ref):
        mask = _band_mask(pl.program_id(1) == 0)
        for h in range(HEADS_PER_GROUP):
            qh = _head(q_ref, h).astype(BF16)
            kh = jnp.concatenate([_head(kp_ref, h), _head(kc_ref, h)], axis=0).astype(BF16)
            vh = jnp.concatenate([_head(vp_ref, h), _head(vc_ref, h)], axis=0).astype(BF16)
            s = lax.dot_general(qh, kh, _NT, preferred_element_type=F32) * scale + b_ref[h]
            s = jnp.where(mask, s, NEG_INF)
            m = jnp.max(s, axis=1, keepdims=True)
            p = jnp.exp(s - m)
            den = jnp.sum(p, axis=1, keepdims=True)
            o = jnp.dot(p.astype(BF16), vh, preferred_element_type=F32) / den
            o_ref[:, h * HEAD_DIM:(h + 1) * HEAD_DIM] = o
            l_ref[:, h * HEAD_DIM:(h + 1) * HEAD_DIM] = jnp.broadcast_to(m + jnp.log(den), (ATT_BLOCK, HEAD_DIM))

    out_spec = pl.BlockSpec((ATT_BLOCK, WIDTH_A), lambda c, n: (n, c))
    (o, lse), ride_outs = _pallas(
        body, grid=(d, nb),
        in_specs=_attn_specs(g, d) + [pl.BlockSpec(bias.shape, _zero_map(3))],
        out_specs=[out_spec, out_spec],
        out_shape=[jax.ShapeDtypeStruct((lq, d * WIDTH_A), F32)] * 2,
        args=[*([qkv.reshape(lq, d * 3 * QKV_WIDTH)] * 5), bias],
        semantics=("parallel", "parallel"), rides=[comm.ride(f"attn_fwd{g}")], name=f"attn_fwd{g}")
    comm.took(ride_outs)
    return o.reshape(t, WIDTH_A), lse.reshape(t, WIDTH_A)


def _attn_bwd(qkv, bias, do, lse, corr, g, d):
    t = qkv.shape[0]
    lq = t // d
    nb = lq // ATT_BLOCK
    scale = HEAD_DIM ** -0.5

    def body(k_ref, v_ref, q0_ref, q1_ref, do0_ref, do1_ref, l0_ref, l1_ref, c0_ref, c1_ref, b_ref,
             dq_ref, dk_ref, dv_ref, db_ref, dq_prev):
        c, j = pl.program_id(0), pl.program_id(1)

        @pl.when((c == 0) & (j == 0))
        def _():
            db_ref[...] = jnp.zeros_like(db_ref)

        @pl.when(j == 0)
        def _():
            dq_prev[...] = jnp.zeros_like(dq_prev)

        i = lax.broadcasted_iota(jnp.int32, (ATT_BLOCK, ATT_BLOCK), 0)
        kk = lax.broadcasted_iota(jnp.int32, (ATT_BLOCK, ATT_BLOCK), 1)
        mask0 = kk <= i
        mask1 = (kk >= i) & (j + 1 < nb)
        for h in range(HEADS_PER_GROUP):
            kh = _head(k_ref, h).astype(BF16)
            vh = _head(v_ref, h).astype(BF16)
            cols = slice(h * HEAD_DIM, (h + 1) * HEAD_DIM)
            dk = jnp.zeros((ATT_BLOCK, HEAD_DIM), F32)
            dv = jnp.zeros((ATT_BLOCK, HEAD_DIM), F32)
            dq_parts = []
            parts = ((q0_ref, do0_ref, l0_ref, c0_ref, mask0, ATT_BLOCK), (q1_ref, do1_ref, l1_ref, c1_ref, mask1, 0))
            for q_ref, do_ref, l_ref, c_ref, mask, off in parts:
                qh = _head(q_ref, h).astype(BF16)
                doh = _head(do_ref, h).astype(BF16)
                s = lax.dot_general(qh, kh, _NT, preferred_element_type=F32) * scale + b_ref[h, :, off:off + ATT_BLOCK]
                s = jnp.where(mask, s, NEG_INF)
                p = jnp.exp(s - l_ref[:, h * HEAD_DIM:h * HEAD_DIM + 1])
                dp = lax.dot_general(doh, vh, _NT, preferred_element_type=F32)
                ds = p * (dp + c_ref[:, h * HEAD_DIM:h * HEAD_DIM + 1])
                dsb = ds.astype(BF16)
                dv = dv + lax.dot_general(p.astype(BF16), doh, _TN, preferred_element_type=F32)
                dk = dk + lax.dot_general(dsb, qh, _TN, preferred_element_type=F32)
                dq_parts.append(jnp.dot(dsb, kh, preferred_element_type=F32))
                db_ref[h, :, off:off + ATT_BLOCK] += ds
            dk_ref[:, cols] = dk * scale
            dv_ref[:, cols] = dv
            dq_ref[:, cols] = (dq_prev[:, cols] + dq_parts[0]) * scale
            dq_prev[:, cols] = dq_parts[1]

    blk = (ATT_BLOCK, WIDTH_A)
    nxt = lambda n: jnp.minimum(n + 1, nb - 1)
    k_spec = pl.BlockSpec(blk, lambda c, n: (n, c * _QKV_BLOCKS + 3 + g))
    v_spec = pl.BlockSpec(blk, lambda c, n: (n, c * _QKV_BLOCKS + 6 + g))
    q0_spec = pl.BlockSpec(blk, lambda c, n: (n, c * _QKV_BLOCKS + g))
    q1_spec = pl.BlockSpec(blk, lambda c, n: (nxt(n), c * _QKV_BLOCKS + g))
    r0 = pl.BlockSpec(blk, lambda c, n: (n, c))
    r1 = pl.BlockSpec(blk, lambda c, n: (nxt(n), c))
    view = lambda a: a.reshape(lq, d * WIDTH_A)
    qv = qkv.reshape(lq, d * 3 * QKV_WIDTH)
    dq, dk, dv, dbias = pl.pallas_call(
        body, grid=(d, nb),
        in_specs=[k_spec, v_spec, q0_spec, q1_spec, r0, r1, r0, r1, r0, r1, pl.BlockSpec(bias.shape, _zero_map(3))],
        out_specs=[r0, r0, r0, pl.BlockSpec(bias.shape, _zero_map(3))],
        out_shape=[jax.ShapeDtypeStruct((lq, d * WIDTH_A), F32)] * 3 + [jax.ShapeDtypeStruct(bias.shape, F32)],
        scratch_shapes=[pltpu.VMEM(blk, F32)],
        compiler_params=_cparams("arbitrary", "arbitrary"), name=f"attn_bwd{g}",
    )(qv, qv, qv, qv, view(do), view(do), view(lse), view(lse), view(corr), view(corr), bias)
    return dq.reshape(t, WIDTH_A), dk.reshape(t, WIDTH_A), dv.reshape(t, WIDTH_A), dbias


def _tril_mask():
    r = lax.broadcasted_iota(jnp.int32, (CHUNK, CHUNK), 0)
    c = lax.broadcasted_iota(jnp.int32, (CHUNK, CHUNK), 1)
    return c <= r


def _gmlp_fwd(zb, ln_g, ln_b, w_s, b_s_t):
    t = zb.shape[0]
    tr = _pick(t, 2 * CHUNK, CHUNK)

    def body(z_ref, g_ref, b_ref, ws_ref, bs_ref, o_ref):
        tri = _tril_mask()
        z = jax.nn.gelu(z_ref[...])
        u = z[:, :WIDTH_B]
        vn = _ln(z[:, WIDTH_B:], g_ref[...], b_ref[...])
        for ch in range(tr // CHUNK):
            rows = slice(ch * CHUNK, (ch + 1) * CHUNK)
            for gi in range(N_GROUPS_B):
                cols = slice(gi * CHUNK, (gi + 1) * CHUNK)
                w = jnp.where(tri, ws_ref[gi], 0.0).astype(BF16)
                mixed = jnp.dot(w, vn[rows, cols].astype(BF16), preferred_element_type=F32) + bs_ref[:, gi:gi + 1]
                o_ref[rows, cols] = (u[rows, cols] * mixed).astype(o_ref.dtype)

    return pl.pallas_call(
        body, grid=(t // tr,),
        in_specs=[pl.BlockSpec((tr, 2 * WIDTH_B), lambda i: (i, 0)), pl.BlockSpec(ln_g.shape, _zero_map(2)),
                  pl.BlockSpec(ln_b.shape, _zero_map(2)), pl.BlockSpec(w_s.shape, _zero_map(3)),
                  pl.BlockSpec(b_s_t.shape, _zero_map(2))],
        out_specs=pl.BlockSpec((tr, WIDTH_B), lambda i: (i, 0)),
        out_shape=jax.ShapeDtypeStruct((t, WIDTH_B), BF16),
        compiler_params=_cparams("parallel"), name="gmlp_fwd")(zb, ln_g, ln_b, w_s, b_s_t)


def _gmlp_bwd(zb, dyb, ln_g, ln_b, w_s, b_s_t, group_sel):
    t = zb.shape[0]
    tr = _pick(t, 2 * CHUNK, CHUNK)

    def body(z_ref, dy_ref, g_ref, b_ref, ws_ref, bs_ref, sel_ref, dz_ref, dzs_ref, dg_ref, db_ref, dws_ref, dbs_ref,
             du_s, dvn_s, dm_s):
        @pl.when(pl.program_id(0) == 0)
        def _():
            dzs_ref[...] = jnp.zeros_like(dzs_ref)
            dg_ref[...] = jnp.zeros_like(dg_ref)
            db_ref[...] = jnp.zeros_like(db_ref)
            dws_ref[...] = jnp.zeros_like(dws_ref)
            dbs_ref[...] = jnp.zeros_like(dbs_ref)

        tri = _tril_mask()
        z, gelu_vjp = jax.vjp(jax.nn.gelu, z_ref[...])
        u = z[:, :WIDTH_B]
        vn, ln_vjp = jax.vjp(_ln, z[:, WIDTH_B:], g_ref[...], b_ref[...])
        dy = dy_ref[...]
        for ch in range(tr // CHUNK):
            rows = slice(ch * CHUNK, (ch + 1) * CHUNK)
            for gi in range(N_GROUPS_B):
                cols = slice(gi * CHUNK, (gi + 1) * CHUNK)
                w = jnp.where(tri, ws_ref[gi], 0.0).astype(BF16)
                vg = vn[rows, cols].astype(BF16)
                mixed = jnp.dot(w, vg, preferred_element_type=F32) + bs_ref[:, gi:gi + 1]
                dyg = dy[rows, cols]
                dm = dyg * u[rows, cols]
                dmb = dm.astype(BF16)
                du_s[rows, cols] = dyg * mixed
                dm_s[rows, cols] = dm
                dvn_s[rows, cols] = lax.dot_general(w, dmb, _TN, preferred_element_type=F32)
                dws_ref[gi] += jnp.where(tri, lax.dot_general(dmb, vg, _NT, preferred_element_type=F32), 0.0)
            dbs_ref[...] += jnp.dot(dm_s[rows, :], sel_ref[...], precision=lax.Precision.HIGHEST,
                                    preferred_element_type=F32)
        dv, dg, db = ln_vjp(dvn_s[...])
        dg_ref[...] += dg
        db_ref[...] += db
        dz = gelu_vjp(jnp.concatenate([du_s[...], dv], axis=1))[0]
        dz_ref[...] = dz.astype(dz_ref.dtype)
        dzs_ref[...] += jnp.sum(dz, axis=0, keepdims=True)

    full = lambda a: pl.BlockSpec(a.shape, _zero_map(a.ndim))
    return pl.pallas_call(
        body, grid=(t // tr,),
        in_specs=[pl.BlockSpec((tr, 2 * WIDTH_B), lambda i: (i, 0)), pl.BlockSpec((tr, WIDTH_B), lambda i: (i, 0)),
                  full(ln_g), full(ln_b), full(w_s), full(b_s_t), full(group_sel)],
        out_specs=[pl.BlockSpec((tr, 2 * WIDTH_B), lambda i: (i, 0)), pl.BlockSpec((1, 2 * WIDTH_B), _zero_map(2)),
                   full(ln_g), full(ln_b), full(w_s), pl.BlockSpec((CHUNK, LANES), _zero_map(2))],
        out_shape=[jax.ShapeDtypeStruct((t, 2 * WIDTH_B), BF16), jax.ShapeDtypeStruct((1, 2 * WIDTH_B), F32),
                   jax.ShapeDtypeStruct(ln_g.shape, F32),
                   jax.ShapeDtypeStruct(ln_b.shape, F32), jax.ShapeDtypeStruct(w_s.shape, F32),
                   jax.ShapeDtypeStruct((CHUNK, LANES), F32)],
        scratch_shapes=[pltpu.VMEM((tr, WIDTH_B), F32)] * 3,
        compiler_params=_cparams("arbitrary"), name="gmlp_bwd")(zb, dyb, ln_g, ln_b, w_s, b_s_t, group_sel)


def _s5_disc(lr, li, ldt, br_t, bi_t):
    dt = jnp.exp(ldt)
    mag = jnp.exp(lr * dt)
    ab_re = mag * jnp.cos(li * dt)
    ab_im = mag * jnp.sin(li * dt)
    nrm = lr * lr + li * li
    cr = ((ab_re - 1.0) * lr + ab_im * li) / nrm
    ci = (ab_im * lr - (ab_re - 1.0) * li) / nrm
    return ab_re, ab_im, cr * br_t - ci * bi_t, cr * bi_t + ci * br_t


def _vmem_call(fn, args, out_shape, name):
    def body(*refs):
        res = fn(*[r[...] for r in refs[:len(args)]])
        for o, v in zip(refs[len(args):], res):
            o[...] = v

    vm = pl.BlockSpec(memory_space=pltpu.VMEM)
    return pl.pallas_call(body, in_specs=[vm] * len(args), out_specs=[vm] * len(out_shape),
                          out_shape=out_shape, name=name)(*args)


def _s5_disc_fwd(lr, li, ldt, br_t, bi_t):
    s1 = jax.ShapeDtypeStruct(lr.shape, F32)
    s2 = jax.ShapeDtypeStruct(br_t.shape, F32)
    return _vmem_call(_s5_disc, [lr, li, ldt, br_t, bi_t], [s1, s1, s2, s2], "s5_disc_fwd")


def _s5_disc_bwd(lr, li, ldt, br_t, bi_t, cts):
    def fn(lr, li, ldt, br_t, bi_t, d0, d1, d2, d3):
        _, vjp = jax.vjp(_s5_disc, lr, li, ldt, br_t, bi_t)
        return vjp((d0, d1, d2, d3))

    shp = [jax.ShapeDtypeStruct(a.shape, F32) for a in (lr, li, ldt, br_t, bi_t)]
    return _vmem_call(fn, [lr, li, ldt, br_t, bi_t, *cts], shp, "s5_disc_bwd")


_SCAN_ROWS = SSM_COLS // LANES
_SCAN_CHUNK = 128
_SSM_IN = SSM_PACK * SSM_GROUP
_SSM_ST = SSM_PACK * SSM_STATE


def _packed_in(xb, m_ref):
    return jnp.concatenate([jnp.dot(xb[:, j * _SSM_IN:(j + 1) * _SSM_IN], m_ref[j], preferred_element_type=F32)
                            for j in range(N_SSM_BLOCKS)], axis=1)


def _packed_out(xb, m_ref):
    return jnp.concatenate([lax.dot_general(xb[:, j * _SSM_ST:(j + 1) * _SSM_ST], m_ref[j], _NT, preferred_element_type=F32)
                            for j in range(N_SSM_BLOCKS)], axis=1)


def _s5_fwd(uc, mats, are, aim, d_skip, comm):
    t = uc.shape[0]
    tc = _pick(t, _SCAN_CHUNK, SUBLANES)

    def body(u_ref, br_ref, bi_ref, cr_ref, ci_ref, ar_ref, ai_ref, d_ref, xr_ref, xi_ref, ys_ref, ycp_ref,
             sr, si, st_ref):
        @pl.when(pl.program_id(0) == 0)
        def _():
            st_ref[...] = jnp.zeros_like(st_ref)

        u = u_ref[...]
        ub = u.astype(BF16)
        sr[...] = _packed_in(ub, br_ref).reshape(tc, _SCAN_ROWS, LANES)
        si[...] = _packed_in(ub, bi_ref).reshape(tc, _SCAN_ROWS, LANES)
        ar, ai = ar_ref[...], ai_ref[...]

        def step(i, carry):
            xr, xi = carry
            nr = ar * xr - ai * xi + sr[i]
            ni = ar * xi + ai * xr + si[i]
            sr[i] = nr
            si[i] = ni
            return nr, ni

        xr, xi = lax.fori_loop(0, tc, step, (st_ref[0], st_ref[1]), unroll=8)
        st_ref[0] = xr
        st_ref[1] = xi
        x_re = sr[...].reshape(tc, SSM_COLS)
        x_im = si[...].reshape(tc, SSM_COLS)
        xr_ref[...] = x_re
        xi_ref[...] = x_im
        ys = _packed_out(x_re.astype(BF16), cr_ref) - _packed_out(x_im.astype(BF16), ci_ref) + d_ref[...] * u
        ys_ref[...] = ys
        ycp_ref[...] = jax.nn.gelu(ys)

    row = lambda w: pl.BlockSpec((tc, w), lambda i: (i, 0))
    mat = pl.BlockSpec(mats[0].shape, _zero_map(3))
    par = pl.BlockSpec((_SCAN_ROWS, LANES), _zero_map(2))
    wide, narrow = jax.ShapeDtypeStruct((t, SSM_COLS), F32), jax.ShapeDtypeStruct((t, WIDTH_C), F32)
    outs, ride_outs = _pallas(
        body, grid=(t // tc,), in_specs=[row(WIDTH_C), mat, mat, mat, mat, par, par, pl.BlockSpec(d_skip.shape, _zero_map(2))],
        out_specs=[row(SSM_COLS), row(SSM_COLS), row(WIDTH_C), row(WIDTH_C)], out_shape=[wide, wide, narrow, narrow],
        args=[uc, *mats, are, aim, d_skip],
        scratch=[pltpu.VMEM((tc, _SCAN_ROWS, LANES), F32)] * 2 + [pltpu.VMEM((2, _SCAN_ROWS, LANES), F32)],
        semantics=("arbitrary",), rides=[comm.ride("s5_fwd")], name="s5_fwd")
    comm.took(ride_outs)
    return outs


def _s5_bwd(dys, duc_skip, uc, xr, xi, mats, are, aim):
    t = dys.shape[0]
    tc = _pick(t, _SCAN_CHUNK, SUBLANES)
    nc = t // tc

    def body(dy_ref, ds_ref, u_ref, xr_ref, xi_ref, pr_ref, pi_ref, br_ref, bi_ref, cr_ref, ci_ref, ar_ref, ai_ref,
             du_ref, dbr_ref, dbi_ref, dcr_ref, dci_ref, dar_ref, dai_ref, gr, gi, x3r, x3i, st_ref):
        step_id = pl.program_id(0)

        @pl.when(step_id == 0)
        def _():
            st_ref[...] = jnp.zeros_like(st_ref)
            for ref in (dbr_ref, dbi_ref, dcr_ref, dci_ref, dar_ref, dai_ref):
                ref[...] = jnp.zeros_like(ref)

        dyb = dy_ref[...].astype(BF16)
        x_re, x_im = xr_ref[...], xi_ref[...]
        gr[...] = _packed_in(dyb, cr_ref).reshape(tc, _SCAN_ROWS, LANES)
        gi[...] = (-_packed_in(dyb, ci_ref)).reshape(tc, _SCAN_ROWS, LANES)
        x3r[...] = x_re.reshape(tc, _SCAN_ROWS, LANES)
        x3i[...] = x_im.reshape(tc, _SCAN_ROWS, LANES)
        ar, ai = ar_ref[...], ai_ref[...]

        def update(i, carry, pxr, pxi):
            g_r, g_i, dar, dai = carry
            ngr = gr[i] + ar * g_r + ai * g_i
            ngi = gi[i] - ai * g_r + ar * g_i
            gr[i] = ngr
            gi[i] = ngi
            return ngr, ngi, dar + ngr * pxr + ngi * pxi, dai - ngr * pxi + ngi * pxr

        def step(s, carry):
            i = tc - 1 - s
            return update(i, carry, x3r[i - 1], x3i[i - 1])

        zero = jnp.zeros((_SCAN_ROWS, LANES), F32)
        carry = lax.fori_loop(0, tc - 1, step, (st_ref[0], st_ref[1], zero, zero), unroll=8)
        has_prev = (step_id < nc - 1).astype(F32)
        last = SUBLANES - 1
        p_re = pr_ref[last:, :].reshape(1, _SCAN_ROWS, LANES)[0] * has_prev
        p_im = pi_ref[last:, :].reshape(1, _SCAN_ROWS, LANES)[0] * has_prev
        g_r, g_i, dar, dai = update(0, carry, p_re, p_im)
        st_ref[0] = g_r
        st_ref[1] = g_i
        dar_ref[...] += dar
        dai_ref[...] += dai

        g_re = gr[...].reshape(tc, SSM_COLS).astype(BF16)
        g_im = gi[...].reshape(tc, SSM_COLS).astype(BF16)
        du_ref[...] = ds_ref[...] + _packed_out(g_re, br_ref) + _packed_out(g_im, bi_ref)
        ub, xrb, xib = u_ref[...].astype(BF16), x_re.astype(BF16), x_im.astype(BF16)
        for j in range(N_SSM_BLOCKS):
            narrow, wide = slice(j * _SSM_IN, (j + 1) * _SSM_IN), slice(j * _SSM_ST, (j + 1) * _SSM_ST)
            dbr_ref[j] += lax.dot_general(ub[:, narrow], g_re[:, wide], _TN, preferred_element_type=F32)
            dbi_ref[j] += lax.dot_general(ub[:, narrow], g_im[:, wide], _TN, preferred_element_type=F32)
            dcr_ref[j] += lax.dot_general(dyb[:, narrow], xrb[:, wide], _TN, preferred_element_type=F32)
            dci_ref[j] -= lax.dot_general(dyb[:, narrow], xib[:, wide], _TN, preferred_element_type=F32)

    rev = lambda w: pl.BlockSpec((tc, w), lambda i: (nc - 1 - i, 0))
    prev = pl.BlockSpec((SUBLANES, SSM_COLS), lambda i: (jnp.maximum((nc - 1 - i) * (tc // SUBLANES) - 1, 0), 0))
    mat = pl.BlockSpec(mats[0].shape, _zero_map(3))
    par = pl.BlockSpec((_SCAN_ROWS, LANES), _zero_map(2))
    msh = jax.ShapeDtypeStruct(mats[0].shape, F32)
    psh = jax.ShapeDtypeStruct((_SCAN_ROWS, LANES), F32)
    return pl.pallas_call(
        body, grid=(nc,),
        in_specs=[rev(WIDTH_C), rev(WIDTH_C), rev(WIDTH_C), rev(SSM_COLS), rev(SSM_COLS), prev, prev,
                  mat, mat, mat, mat, par, par],
        out_specs=[rev(WIDTH_C), mat, mat, mat, mat, par, par],
        out_shape=[jax.ShapeDtypeStruct((t, WIDTH_C), F32), msh, msh, msh, msh, psh, psh],
        scratch_shapes=[pltpu.VMEM((tc, _SCAN_ROWS, LANES), F32)] * 4 + [pltpu.VMEM((2, _SCAN_ROWS, LANES), F32)],
        compiler_params=_cparams("arbitrary"), name="s5_bwd")(dys, duc_skip, uc, xr, xi, xr, xi, *mats, are, aim)


def _diag_blocks(a, b):
    return [(j, i, slice(i * a, (i + 1) * a), slice(i * b, (i + 1) * b))
            for j in range(N_SSM_BLOCKS) for i in range(SSM_PACK)]


def _block_diag(ms):
    _, a, b = ms[0].shape

    def body(*refs):
        for m_ref, o_ref in zip(refs[:len(ms)], refs[len(ms):]):
            o_ref[...] = jnp.zeros_like(o_ref)
            for j, i, rows, cols in _diag_blocks(a, b):
                o_ref[j, rows, cols] = m_ref[j * SSM_PACK + i].astype(o_ref.dtype)

    vm = pl.BlockSpec(memory_space=pltpu.VMEM)
    shape = jax.ShapeDtypeStruct((N_SSM_BLOCKS, SSM_PACK * a, SSM_PACK * b), BF16)
    return pl.pallas_call(body, in_specs=[vm] * len(ms), out_specs=[vm] * len(ms), out_shape=[shape] * len(ms),
                          name="s5_block_diag")(*ms)


def _block_diag_extract(ms, a, b):
    def body(*refs):
        for m_ref, o_ref in zip(refs[:len(ms)], refs[len(ms):]):
            for j, i, rows, cols in _diag_blocks(a, b):
                o_ref[j * SSM_PACK + i] = m_ref[j, rows, cols]

    vm = pl.BlockSpec(memory_space=pltpu.VMEM)
    shape = jax.ShapeDtypeStruct((N_GROUPS_C, a, b), F32)
    return pl.pallas_call(body, in_specs=[vm] * len(ms), out_specs=[vm] * len(ms), out_shape=[shape] * len(ms),
                          name="s5_block_diag_extract")(*ms)


def _exchange(src, *, gather, name):
    shape = src.shape if gather else src.shape[1:]

    def body(src_ref, out_ref, send_sems, recv_sems, local_sem):
        x, y, c = lax.axis_index("x"), lax.axis_index("y"), lax.axis_index("c")
        me = 4 * x + 2 * y + c
        copies = []
        for r in range(1, N_DEV):
            px = 1 - x if r & 4 else x
            py = 1 - y if r & 2 else y
            pc = 1 - c if r & 1 else c
            piece = src_ref if gather else src_ref.at[4 * px + 2 * py + pc]
            cp = pltpu.make_async_remote_copy(
                src_ref=piece, dst_ref=out_ref.at[me], send_sem=send_sems.at[r - 1], recv_sem=recv_sems.at[r - 1],
                device_id=(px, py, pc), device_id_type=pl.DeviceIdType.MESH)
            cp.start()
            copies.append(cp)
        mine = pltpu.make_async_copy(src_ref if gather else src_ref.at[me], out_ref.at[me], local_sem)
        mine.start()
        for cp in copies:
            cp.wait()
        mine.wait()

    hbm = pl.BlockSpec(memory_space=pl.ANY)
    return pl.pallas_call(
        body, in_specs=[hbm], out_specs=hbm, out_shape=jax.ShapeDtypeStruct((N_DEV,) + tuple(shape), src.dtype),
        scratch_shapes=[pltpu.SemaphoreType.DMA((N_DEV - 1,)), pltpu.SemaphoreType.DMA((N_DEV - 1,)),
                        pltpu.SemaphoreType.DMA(())],
        name=name)(src)


def _mesh_place():
    x, y, c = lax.axis_index("x"), lax.axis_index("y"), lax.axis_index("c")
    other_chips = [(1 - x, y), (x, 1 - y), (1 - x, 1 - y)]
    return x, y, c, other_chips


def _gather_layer(srcs, layer, name):
    n = len(srcs)

    def body(*refs):
        src = [r.at[layer] for r in refs[:n]]
        out = refs[n:2 * n]
        send_sems, recv_sems, local_sems = refs[2 * n:]
        x, y, c, chips = _mesh_place()
        me, sibling = (x, y, c), (x, y, 1 - c)

        def copy(t, k, block, to, from_src=False):
            slot = 4 * block[0] + 2 * block[1] + block[2]
            return pltpu.make_async_remote_copy(
                src_ref=src[t] if from_src else out[t].at[slot], dst_ref=out[t].at[slot],
                send_sem=send_sems.at[t, k], recv_sem=recv_sems.at[t, k], device_id=to, device_id_type=_MESH_ID)

        mine = [pltpu.make_async_copy(src[t], out[t].at[4 * x + 2 * y + c], local_sems.at[t]) for t in range(n)]
        for cp in mine:
            cp.start()
        first = []
        for t in range(n):
            first.append(copy(t, 0, me, sibling, True))
            first += [copy(t, 1 + j, me, (*chip, c), True) for j, chip in enumerate(chips)]
        for cp in first:
            cp.start()
        passed = []
        for j, chip in enumerate(chips):
            for t in range(n):
                copy(t, 1 + j, (*chip, c), me).wait_recv()
                fwd = copy(t, 4 + j, (*chip, c), sibling)
                fwd.start()
                passed.append(fwd)
        for t in range(n):
            copy(t, 0, sibling, me).wait_recv()
            for j, chip in enumerate(chips):
                copy(t, 4 + j, (*chip, 1 - c), me).wait_recv()
        for cp in first + passed:
            cp.wait_send()
        for cp in mine:
            cp.wait()

    return pl.pallas_call(
        body, in_specs=[_HBM] * n, out_specs=[_HBM] * n,
        out_shape=[jax.ShapeDtypeStruct((N_DEV,) + s.shape[1:], s.dtype) for s in srcs],
        scratch_shapes=[pltpu.SemaphoreType.DMA((n, N_DEV - 1)), pltpu.SemaphoreType.DMA((n, N_DEV - 1)),
                        pltpu.SemaphoreType.DMA((n,))],
        name=name)(*srcs)


def _scatter_pair(srcs, name):
    n = len(srcs)

    def body(*refs):
        src, out = refs[:n], refs[n:2 * n]
        send_sems, recv_sems = refs[2 * n:]
        x, y, c, _ = _mesh_place()
        copies = [pltpu.make_async_remote_copy(
            src_ref=src[t].at[:, 1 - c], dst_ref=out[t], send_sem=send_sems.at[t], recv_sem=recv_sems.at[t],
            device_id=(x, y, 1 - c), device_id_type=_MESH_ID) for t in range(n)]
        for cp in copies:
            cp.start()
        for cp in copies:
            cp.wait()

    return pl.pallas_call(
        body, in_specs=[_HBM] * n, out_specs=[_HBM] * n,
        out_shape=[jax.ShapeDtypeStruct((s.shape[0],) + s.shape[2:], s.dtype) for s in srcs],
        scratch_shapes=[pltpu.SemaphoreType.DMA((n,)), pltpu.SemaphoreType.DMA((n,))], name=name)(*srcs)


def _pair_add(src, recv, name):
    nchip, _, r, cdim = src.shape
    tr = _pick(r, 256, 2 * SUBLANES)
    core = lax.axis_index("c").astype(jnp.int32).reshape(1)

    def body(core_ref, s_ref, r_ref, o_ref):
        o_ref[...] = (s_ref[...].astype(F32) + r_ref[...].astype(F32)).astype(o_ref.dtype)

    grid_spec = pltpu.PrefetchScalarGridSpec(
        num_scalar_prefetch=1, grid=(nchip, r // tr),
        in_specs=[pl.BlockSpec((None, None, tr, cdim), lambda k, i, core_ref: (k, core_ref[0], i, 0)),
                  pl.BlockSpec((None, tr, cdim), lambda k, i, core_ref: (k, i, 0))],
        out_specs=pl.BlockSpec((None, tr, cdim), lambda k, i, core_ref: (k, i, 0)))
    return pl.pallas_call(body, grid_spec=grid_spec, out_shape=jax.ShapeDtypeStruct(recv.shape, recv.dtype),
                          compiler_params=_cparams("parallel", "parallel"), name=name)(core, src, recv)


def _scatter_chips(srcs, name):
    n = len(srcs)

    def body(*refs):
        src, out = refs[:n], refs[n:2 * n]
        send_sems, recv_sems, local_sems = refs[2 * n:]
        x, y, c, chips = _mesh_place()
        my_chip = 2 * x + y
        mine = [pltpu.make_async_copy(src[t].at[my_chip], out[t].at[my_chip], local_sems.at[t]) for t in range(n)]
        copies = [pltpu.make_async_remote_copy(
            src_ref=src[t].at[2 * chip[0] + chip[1]], dst_ref=out[t].at[my_chip],
            send_sem=send_sems.at[t, j], recv_sem=recv_sems.at[t, j], device_id=(*chip, c), device_id_type=_MESH_ID)
            for t in range(n) for j, chip in enumerate(chips)]
        for cp in mine + copies:
            cp.start()
        for cp in copies + mine:
            cp.wait()

    return pl.pallas_call(
        body, in_specs=[_HBM] * n, out_specs=[_HBM] * n,
        out_shape=[jax.ShapeDtypeStruct(s.shape, s.dtype) for s in srcs],
        scratch_shapes=[pltpu.SemaphoreType.DMA((n, 3)), pltpu.SemaphoreType.DMA((n, 3)), pltpu.SemaphoreType.DMA((n,))],
        name=name)(*srcs)


def _chip_sums(grads, tag):
    views = [g.reshape(N_DEV // 2, 2, g.shape[0] // N_DEV, g.shape[1]) for g in grads]
    from_sibling = _scatter_pair(views, name="scatter_pair_" + tag)
    return [_pair_add(v, s, name="pair_add") for v, s in zip(views, from_sibling)]


def _sum_chips(parts):
    return _rowmap(_sum_parts_fn, [], stacks=[parts], row_outs=[(parts.shape[2], F32)], tr=128, name="sum_chips")[0]


def _buffer_roles(kinds, bufs):
    carried = [k for k in kinds if k in bufs]
    return carried, [k for k in kinds if k not in bufs]


def _gather_ride(sends, forwards, bufs):
    carried, created = _buffer_roles(list(dict.fromkeys([s[0] for s in sends] + [f[0] for f in forwards])), bufs)
    shape_of = {s[0]: jax.ShapeDtypeStruct((N_DEV,) + s[1].shape, s[1].dtype) for s in sends}
    order = carried + created

    def copies(in_refs, buf_refs, sems):
        x, y, c, chips = _mesh_place()
        buf = dict(zip(order, buf_refs))
        out, s0 = [], 0
        for (kind, _, r0, nr), src in zip(sends, in_refs):
            mine, dst = src.at[pl.ds(r0, nr)], buf[kind].at[4 * x + 2 * y + c, pl.ds(r0, nr)]
            out.append(pltpu.make_async_copy(mine, dst, sems.at[s0 + 8]))
            for k, peer in enumerate([(x, y, 1 - c)] + [(*chip, c) for chip in chips]):
                out.append(pltpu.make_async_remote_copy(src_ref=mine, dst_ref=dst, send_sem=sems.at[s0 + k],
                                                        recv_sem=sems.at[s0 + 4 + k], device_id=peer, device_id_type=_MESH_ID))
            s0 += 9
        for kind, r0, nr in forwards:
            for j, chip in enumerate(chips):
                blk = buf[kind].at[4 * chip[0] + 2 * chip[1] + c, pl.ds(r0, nr)]
                out.append(pltpu.make_async_remote_copy(src_ref=blk, dst_ref=blk, send_sem=sems.at[s0 + j],
                                                        recv_sem=sems.at[s0 + 3 + j], device_id=(x, y, 1 - c),
                                                        device_id_type=_MESH_ID))
            s0 += 6
        return out

    def start(in_refs, buf_refs, sems):
        for cp in copies(in_refs, buf_refs, sems):
            cp.start()

    def finish(in_refs, buf_refs, sems):
        for cp in copies(in_refs, buf_refs, sems):
            cp.wait()

    ride = _Ride(inputs=tuple(s[1] for s in sends), carried=tuple(bufs[k] for k in carried),
                 created=tuple(shape_of[k] for k in created), n_sems=9 * len(sends) + 6 * len(forwards),
                 start=start, finish=finish)
    return ride, order


def _scatter_ride(pieces, bufs):
    carried, created = _buffer_roles(list(dict.fromkeys(p[0] for p in pieces)), bufs)
    shape_of = {p[0]: jax.ShapeDtypeStruct(p[1].shape, p[1].dtype) for p in pieces}
    order = carried + created

    def copies(in_refs, buf_refs, sems):
        x, y, c, chips = _mesh_place()
        buf = dict(zip(order, buf_refs))
        out, s0 = [], 0
        for (kind, _, r0, nr), src in zip(pieces, in_refs):
            dst = buf[kind].at[2 * x + y, pl.ds(r0, nr)]
            out.append(pltpu.make_async_copy(src.at[2 * x + y, pl.ds(r0, nr)], dst, sems.at[s0 + 6]))
            for j, chip in enumerate(chips):
                out.append(pltpu.make_async_remote_copy(
                    src_ref=src.at[2 * chip[0] + chip[1], pl.ds(r0, nr)], dst_ref=dst, send_sem=sems.at[s0 + j],
                    recv_sem=sems.at[s0 + 3 + j], device_id=(*chip, c), device_id_type=_MESH_ID))
            s0 += 7
        return out

    def start(in_refs, buf_refs, sems):
        for cp in copies(in_refs, buf_refs, sems):
            cp.start()

    def finish(in_refs, buf_refs, sems):
        for cp in copies(in_refs, buf_refs, sems):
            cp.wait()

    ride = _Ride(inputs=tuple(p[1] for p in pieces), carried=tuple(bufs[k] for k in carried),
                 created=tuple(shape_of[k] for k in created), n_sems=7 * len(pieces), start=start, finish=finish)
    return ride, order


GATHER_PLAN = (
    ("mm_in_qkv", (("w_in", 0, 3),)),
    ("mm_in_gl", (("w_in", 1, 3),)),
    ("attn_fwd0", (("w_pa", 0, 1), ("w_pb", 0, 1))),
    ("attn_fwd1", (("w_pc", 0, 1), ("w_glu", 0, 1))),
    ("attn_fwd2", (("w_o", 0, 1),)),
    ("s5_fwd", (("w_ffn_out", 0, 2),)),
    ("mm_o", (("w_ffn_out", 1, 2),)),
    ("mm_ffn_in", (("w_in", 2, 3), ("w_ffn_in", 0, 2))),
    ("mm_ffn_out", (("w_ffn_in", 1, 2),)),
    ("norm2", ()),
)
EARLY_KINDS = ("w_ffn_out", "w_ffn_in")
LATE_KINDS = tuple(n for n in SHARDED if n not in EARLY_KINDS)
SCATTER_EARLY_PLAN = (
    ("mm_in_dw_qkv", (("w_ffn_out", 0, 2),)),
    ("mm_in_dx_qkv", (("w_ffn_out", 1, 2),)),
    ("mm_in_dw_gl", (("w_ffn_in", 0, 4), ("w_ffn_in", 3, 4))),
    ("mm_in_dx_gl", (("w_ffn_in", 1, 4), ("w_ffn_in", 2, 4))),
)
SCATTER_PLAN = (
    ("mm_ffn_out_dw", (("w_in", 0, 3),)),
    ("mm_ffn_out_dx", (("w_in", 1, 3),)),
    ("mm_ffn_in_dw", (("w_in", 2, 3), ("w_o", 0, 1), ("w_pa", 0, 1), ("w_pb", 0, 1), ("w_pc", 0, 1), ("w_glu", 0, 1))),
)


def _row_part(rows, part, parts):
    assert rows % (parts * 2 * SUBLANES) == 0
    return part * (rows // parts), rows // parts


class _Carried:
    def __init__(self, plan, blocks, make_ride, forwards_too):
        self.plan, self.blocks, self.make_ride, self.forwards_too = dict(plan), blocks, make_ride, forwards_too
        self.bufs, self.to_forward, self.order = {}, [], []

    def ride(self, host):
        if self.blocks is None or host not in self.plan:
            self.order = []
            return None
        sends = [(k, self.blocks[k], *_row_part(self.blocks[k].shape[-2], part, parts)) for k, part, parts in self.plan[host]]
        if self.forwards_too:
            ride, self.order = self.make_ride(sends, self.to_forward, self.bufs)
            self.to_forward = [(k, r0, nr) for k, _, r0, nr in sends]
        else:
            ride, self.order = self.make_ride(sends, self.bufs)
        return ride

    def took(self, ride_outs):
        for k, buf in zip(self.order, ride_outs[0] if ride_outs else []):
            self.bufs[k] = buf


def _hosted(comm, fn, *args, name, **kwargs):
    res, ride_outs = fn(*args, name=name, rides=[comm.ride(name)], **kwargs)
    comm.took(ride_outs)
    return res


def _small_sizes(shapes):
    return [int(np.prod(shapes[n])) for n in SMALL]


def _pack_small(vals):
    flat = jnp.concatenate([vals[n].reshape(-1).astype(F32) for n in SMALL])
    rows = -(-flat.shape[0] // (LANES * N_DEV * SUBLANES)) * (N_DEV * SUBLANES)
    return jnp.pad(flat, (0, rows * LANES - flat.shape[0])).reshape(rows, LANES)


def _unpack_small(packed, shapes):
    flat = packed.reshape(-1)
    out, off = {}, 0
    for n, size in zip(SMALL, _small_sizes(shapes)):
        out[n] = flat[off:off + size].reshape(shapes[n])
        off += size
    return out


def _row(v):
    return v.reshape(1, -1)


def _layer_params(l, full, small):
    o1, o2, o3 = 3 * QKV_WIDTH, 3 * QKV_WIDTH + 2 * WIDTH_B, 3 * QKV_WIDTH + 2 * WIDTH_B + WIDTH_C
    b_in = small["b_in"][l]
    p = {
        "in_pieces": (("qkv", 0, o1), ("zb", o1, o2 - o1), ("uc", o2, o3 - o2), ("gl", o3, b_in.shape[0] - o3)),
        "b_qkv": _row(b_in[:o1]), "b_zb": _row(b_in[o1:o2]), "b_uc": _row(b_in[o2:o3]), "b_gl": _row(b_in[o3:]),
        "sgu_ln_g": _row(small["sgu_ln_g"][l]), "sgu_ln_b": _row(small["sgu_ln_b"][l]),
        "w_s": small["w_s"][l], "b_s_t": small["b_s"][l].T,
        "lam_re": small["lam_re"][l][:, None, :], "lam_im": small["lam_im"][l][:, None, :],
        "log_dt": small["log_dt"][l][:, None, None],
        "b_re_t": small["b_re"][l].transpose(0, 2, 1), "b_im_t": small["b_im"][l].transpose(0, 2, 1),
        "c_re": small["c_re"][l], "c_im": small["c_im"][l],
        "d_skip": _row(small["d_skip"][l]), "b_glu": _row(small["b_glu"][l]),
        "ln1_g": _row(small["ln1_g"][l]), "ln1_b": _row(small["ln1_b"][l]),
        "ln2_g": _row(small["ln2_g"][l]), "ln2_b": _row(small["ln2_b"][l]),
    }
    for n in SHARDED:
        p[n] = full[n]
    return p


def _twice(fn):
    def both(*args):
        y = fn(*args)
        return y, y
    return both


def _layer_fwd(x, xb, p, biases, comm):
    t, d = x.shape
    r = {"x": x, "xb": xb}
    for piece, off, n in p["in_pieces"]:
        r[piece] = _hosted(comm, _mm, xb, p["w_in"], tb=True, b_off=off, n=n, bias=p["b_" + piece],
                           out_dtype=BF16 if piece in ("qkv", "gl") else F32, name="mm_in_" + piece)
    ol = []
    for g, dil in enumerate(ATT_DILATIONS):
        ol += list(_attn_fwd(r["qkv"], biases[g], g, dil, comm))
    r["ol"] = ol
    r["ya"], r["ya_b"] = _rowmap(_twice(_combine), ol, row_outs=[(WIDTH_A, F32), (WIDTH_A, BF16)], tr=512,
                                 name="attn_combine")
    r["yb"] = _gmlp_fwd(r["zb"], p["sgu_ln_g"], p["sgu_ln_b"], p["w_s"], p["b_s_t"])
    ab_re, ab_im, bb_re_t, bb_im_t = _s5_disc_fwd(p["lam_re"], p["lam_im"], p["log_dt"], p["b_re_t"], p["b_im_t"])
    r["a_re"], r["a_im"] = ab_re.reshape(_SCAN_ROWS, LANES), ab_im.reshape(_SCAN_ROWS, LANES)
    r["s5_mats"] = _block_diag([bb_re_t, bb_im_t, p["c_re"], p["c_im"]])
    r["xr"], r["xi"], r["ys"], r["ycp"] = _s5_fwd(r["uc"], r["s5_mats"], r["a_re"], r["a_im"], p["d_skip"], comm)
    r["glin"] = _mm(r["ycp"], p["w_glu"], bias=p["b_glu"], name="mm_glu")
    r["yc"] = _rowmap(_glu, [r["ycp"], r["glin"]], row_outs=[(WIDTH_C, BF16)], tr=512, name="glu")[0]
    r["pa"] = _mm(r["ya_b"], p["w_pa"], tb=True, out_dtype=BF16, name="mm_pa")
    r["pb"] = _mm(r["yb"], p["w_pb"], tb=True, out_dtype=BF16, name="mm_pb")
    r["pc"] = _mm(r["yc"], p["w_pc"], tb=True, out_dtype=BF16, name="mm_pc")
    r["merged"] = _rowmap(_merge, [r["gl"], r["pa"], r["pb"], r["pc"]], row_outs=[(d, BF16)], name="merge")[0]
    r["mo"] = _hosted(comm, _mm, r["merged"], p["w_o"], name="mm_o")
    r["xm"], r["xm_b"] = _rowmap(_twice(_post_norm), [x, r["mo"]], consts=[p["ln1_g"], p["ln1_b"]],
                                 row_outs=[(d, F32), (d, BF16)], name="norm1")
    r["gate"], r["up"], r["act"] = _hosted(comm, _mm_swiglu, r["xm_b"], p["w_ffn_in"], name="mm_ffn_in")
    r["f"] = _hosted(comm, _mm, r["act"], p["w_ffn_out"], name="mm_ffn_out")
    out, out_b = _hosted(comm, _rowmap, _twice(_post_norm), [r["xm"], r["f"]], consts=[p["ln2_g"], p["ln2_b"]],
                         row_outs=[(d, F32), (d, BF16)], name="norm2")
    return out, out_b, r


def _layer_bwd(dout, r, p, biases, consts, comm, make_early):
    t, d = dout.shape
    gw, gs = {}, {}
    ffw = 2 * r["gate"].shape[1]
    dxm, df, gs["ln2_g"], gs["ln2_b"] = _rowmap(
        _post_norm_bwd, [r["xm"], r["f"], dout], consts=[p["ln2_g"], p["ln2_b"]],
        row_outs=[(d, F32), (d, BF16)], red_outs=[(1, d)] * 2, name="norm2_bwd")
    gw["w_ffn_out"] = _hosted(comm, _mm, r["act"], df, ta=True, out_dtype=BF16, name="mm_ffn_out_dw")
    dact = _hosted(comm, _mm, df, p["w_ffn_out"], tb=True, out_dtype=BF16, name="mm_ffn_out_dx")
    dgu = _rowmap(_swiglu_bwd, [r["gate"], r["up"], dact], row_outs=[(ffw, BF16)], tr=128, name="swiglu_bwd")[0]
    gw["w_ffn_in"] = _hosted(comm, _mm, dgu, r["xm_b"], ta=True, out_dtype=BF16, name="mm_ffn_in_dw")
    dxm = _hosted(comm, _mm, dgu, p["w_ffn_in"], add=dxm, name="mm_ffn_in_dx")
    early = make_early({n: gw[n] for n in EARLY_KINDS})
    dx, dmo, gs["ln1_g"], gs["ln1_b"] = _rowmap(
        _post_norm_bwd, [r["x"], r["mo"], dxm], consts=[p["ln1_g"], p["ln1_b"]],
        row_outs=[(d, F32), (d, BF16)], red_outs=[(1, d)] * 2, name="norm1_bwd")
    gw["w_o"] = _hosted(comm, _mm, r["merged"], dmo, ta=True, out_dtype=BF16, name="mm_o_dw")
    dmerged = _hosted(comm, _mm, dmo, p["w_o"], tb=True, name="mm_o_dx")
    dgl, dpa, dpb, dpc, db_gl = _rowmap(
        _merge_bwd, [r["gl"], r["pa"], r["pb"], r["pc"], dmerged],
        row_outs=[(3 * d, BF16), (d, BF16), (d, BF16), (d, BF16)], red_outs=[(1, 3 * d)], tr=128, name="merge_bwd")
    gw["w_pa"] = _mm(dpa, r["ya_b"], ta=True, out_dtype=BF16, name="mm_pa_dw")
    gw["w_pb"] = _mm(dpb, r["yb"], ta=True, out_dtype=BF16, name="mm_pb_dw")
    gw["w_pc"] = _mm(dpc, r["yc"], ta=True, out_dtype=BF16, name="mm_pc_dw")
    dya = _mm(dpa, p["w_pa"], name="mm_pa_dx")
    dyb = _mm(dpb, p["w_pb"], name="mm_pb_dx")
    dyc = _mm(dpc, p["w_pc"], name="mm_pc_dx")
    dycp, dglin, gs["b_glu"] = _rowmap(_glu_bwd, [r["ycp"], r["glin"], dyc], row_outs=[(WIDTH_C, F32), (WIDTH_C, BF16)],
                                       red_outs=[(1, WIDTH_C)], tr=512, name="glu_bwd")
    gw["w_glu"] = _mm(r["ycp"], dglin, ta=True, out_dtype=BF16, name="mm_glu_dw")
    dycp = _mm(dglin, p["w_glu"], tb=True, add=dycp, name="mm_glu_dx")
    dys, duc, gs["d_skip"] = _rowmap(_s5_out_bwd, [r["ys"], r["uc"], dycp], consts=[p["d_skip"]],
                                     row_outs=[(WIDTH_C, F32)] * 2, red_outs=[(1, WIDTH_C)], tr=512, name="s5_out_act_bwd")
    duc, d_bmat_re, d_bmat_im, d_cmat_re, d_cmat_im, da_re, da_im = _s5_bwd(
        dys, duc, r["uc"], r["xr"], r["xi"], r["s5_mats"], r["a_re"], r["a_im"])
    d_bb_re_t, d_bb_im_t, gs["c_re"], gs["c_im"] = _block_diag_extract(
        [d_bmat_re, d_bmat_im, d_cmat_re, d_cmat_im], SSM_GROUP, SSM_STATE)
    cts = (da_re.reshape(N_GROUPS_C, 1, SSM_STATE), da_im.reshape(N_GROUPS_C, 1, SSM_STATE), d_bb_re_t, d_bb_im_t)
    d_lr, d_li, d_ldt, d_br_t, d_bi_t = _s5_disc_bwd(p["lam_re"], p["lam_im"], p["log_dt"], p["b_re_t"], p["b_im_t"], cts)
    gs["lam_re"], gs["lam_im"], gs["log_dt"] = d_lr[:, 0, :], d_li[:, 0, :], d_ldt[:, 0, 0]
    gs["b_re"], gs["b_im"] = d_br_t.transpose(0, 2, 1), d_bi_t.transpose(0, 2, 1)
    dzb, db_zb, gs["sgu_ln_g"], gs["sgu_ln_b"], gs["w_s"], dbs_t = _gmlp_bwd(
        r["zb"], dyb, p["sgu_ln_g"], p["sgu_ln_b"], p["w_s"], p["b_s_t"], consts["group_sel"])
    gs["b_s"] = dbs_t[:, :N_GROUPS_B].T
    do_corr = _rowmap(_combine_bwd, r["ol"] + [r["ya"], dya], consts=[consts["head_ones"]],
                      row_outs=[(WIDTH_A, F32)] * 6, tr=512, name="attn_combine_bwd")
    dq, dk, dv, dbias = [], [], [], []
    for g, dil in enumerate(ATT_DILATIONS):
        do_g, corr_g, lse_g = do_corr[g], do_corr[3 + g], r["ol"][2 * g + 1]
        dq_g, dk_g, dv_g, db_g = _attn_bwd(r["qkv"], biases[g], do_g, lse_g, corr_g, g, dil)
        dq.append(dq_g)
        dk.append(dk_g)
        dv.append(dv_g)
        dbias.append(db_g)
    cast_colsum = lambda a: (a, jnp.sum(a, axis=0, keepdims=True))
    dqkv, db_qkv = _rowmap(cast_colsum, [jnp.concatenate(dq + dk + dv, axis=1)], row_outs=[(3 * QKV_WIDTH, BF16)],
                           red_outs=[(1, 3 * QKV_WIDTH)], tr=512, name="cast_colsum_qkv")
    duc, db_uc = _rowmap(cast_colsum, [duc], row_outs=[(WIDTH_C, BF16)], red_outs=[(1, WIDTH_C)], tr=512,
                         name="cast_colsum_uc")
    dpieces = {"qkv": dqkv, "zb": dzb, "uc": duc, "gl": dgl}
    rows_in = p["w_in"].shape[0]
    dw_in = None
    for piece, off, n in p["in_pieces"]:
        dw_in = _hosted(early, _mm, dpieces[piece], r["xb"], ta=True, out_dtype=BF16, into=(rows_in, off, dw_in),
                        name="mm_in_dw_" + piece)
        dx = _hosted(early, _mm, dpieces[piece], p["w_in"], b_off=off, add=dx, name="mm_in_dx_" + piece)
    gw["w_in"] = dw_in
    gs["b_in"] = jnp.concatenate([db_qkv, db_zb, db_uc, db_gl], axis=1)[0]
    for n in ("sgu_ln_g", "sgu_ln_b", "d_skip", "b_glu", "ln1_g", "ln1_b", "ln2_g", "ln2_b"):
        gs[n] = gs[n][0]
    return dx, gw, gs, dbias, early.bufs


def _cast_bf16(w):
    w2 = w.reshape(-1, w.shape[-1])
    out = _rowmap(lambda a: a, [w2], row_outs=[(w2.shape[1], BF16)], tr=512, name="cast_bf16")[0]
    return out.reshape(w.shape)


def _static_consts():
    head_ones = np.kron(np.eye(HEADS_PER_GROUP, dtype=np.float32), np.ones((HEAD_DIM, HEAD_DIM), np.float32))
    group_sel = np.zeros((WIDTH_B, LANES), np.float32)
    group_sel[np.arange(WIDTH_B), np.arange(WIDTH_B) // CHUNK] = 1.0
    return {"head_ones": jnp.asarray(head_ones), "group_sel": jnp.asarray(group_sel)}


def _step(x, tgt, w, m, v):
    shapes = {n: w[n].shape for n in WEIGHTS}
    consts = _static_consts()
    mine_bf = {n: _cast_bf16(w[n].transpose(0, 2, 1) if n in TRANSPOSED else w[n]) for n in SHARDED}
    small = {n: w[n] for n in SMALL}
    buckets = [jnp.asarray(_bucket_table(dil)) for dil in ATT_DILATIONS]
    biases = [_bias_fwd(w["rel_bias"], buckets[g], g) for g in range(len(ATT_DILATIONS))]
    params, saved = [], []
    h, hb = _rowmap(_twice(lambda a: a), [x], row_outs=[(x.shape[1], F32), (x.shape[1], BF16)], name="cast_x")
    gathered = dict(zip(SHARDED, _gather_layer([mine_bf[n] for n in SHARDED], 0, name="gather_layer0")))
    for l in range(DEPTH):
        p = _layer_params(l, {n: g.reshape(-1, g.shape[2]) for n, g in gathered.items()}, small)
        ahead = _Carried(GATHER_PLAN, {n: mine_bf[n][l + 1] for n in SHARDED} if l + 1 < DEPTH else None,
                         _gather_ride, forwards_too=True)
        h, hb, r = _layer_fwd(h, hb, p, biases, ahead)
        gathered = ahead.bufs
        params.append(p)
        saved.append(r)
    dy, loss_part = _rowmap(_loss_fn, [h, tgt], row_outs=[(h.shape[1], F32)], red_outs=[(1, LANES)], name="loss")
    loss = lax.psum(loss_part[0, 0], MESH_AXES)
    g_mine, gs_layers = {n: [None] * DEPTH for n in SHARDED}, [None] * DEPTH
    dbias_sum = None
    def scatter_steps(plan, grads, tag):
        sums = _chip_sums(list(grads.values()), tag)
        return _Carried(plan, dict(zip(grads, sums)), _scatter_ride, forwards_too=False)

    behind = _Carried(SCATTER_PLAN, None, _scatter_ride, forwards_too=False)
    for l in reversed(range(DEPTH)):
        dy, gw, gs_layers[l], dbias, early_parts = _layer_bwd(
            dy, saved[l], params[l], biases, consts, behind,
            functools.partial(scatter_steps, SCATTER_EARLY_PLAN, tag="early"))
        saved[l] = None
        for n in EARLY_KINDS:
            g_mine[n][l] = _sum_chips(early_parts[n])
        if behind.blocks is not None:
            for n in LATE_KINDS:
                g_mine[n][l + 1] = _sum_chips(behind.bufs[n])
        behind = scatter_steps(SCATTER_PLAN, {n: gw[n] for n in LATE_KINDS}, "late")
        if l == 0:
            last = _scatter_chips([behind.blocks[n] for n in LATE_KINDS], name="scatter_chips_layer0")
            for n, parts in zip(LATE_KINDS, last):
                g_mine[n][0] = _sum_chips(parts)
        if dbias_sum is None:
            dbias_sum = dbias
        else:
            dbias_sum = [_rowmap(lambda a, b: a + b, [a.reshape(-1, 2 * ATT_BLOCK), b.reshape(-1, 2 * ATT_BLOCK)],
                                 row_outs=[(2 * ATT_BLOCK, F32)], name="dbias_add")[0].reshape(a.shape)
                         for a, b in zip(dbias_sum, dbias)]
    drel = [_bias_bwd(dbias_sum[g], buckets[g], g) for g in range(len(ATT_DILATIONS))]
    drel = _rowmap(lambda a, b, c: a + b + c, drel, row_outs=[(LANES, F32)], name="drel_add")[0]
    grad_small_local = {n: jnp.stack([gs_layers[l][n] for l in range(DEPTH)]) for n in SMALL if n != "rel_bias"}
    grad_small_local["rel_bias"] = drel[:, :shapes["rel_bias"][1]]
    out_g, out_d, out_m, out_v = {}, {}, {}, {}
    for n in SHARDED:
        g = jnp.stack(g_mine[n])
        out_g[n] = g.transpose(0, 2, 1) if n in TRANSPOSED else g
        cols = shapes[n][-1]
        res = _rowmap(_adamw, [a.reshape(-1, cols) for a in (w[n], out_g[n], m[n], v[n])],
                      row_outs=[(cols, F32)] * 3, tr=128, name="adamw_" + n)
        out_d[n], out_m[n], out_v[n] = [a.reshape(shapes[n]) for a in res]
    packed = _pack_small(grad_small_local)
    rows = packed.shape[0] // N_DEV
    parts = _exchange(packed.reshape(N_DEV, rows, LANES), gather=False, name="scatter_small")
    mine = _rowmap(_sum_parts_fn, [], stacks=[parts], row_outs=[(LANES, F32)], name="sum_small")[0]
    g_small = _exchange(mine, gather=True, name="gather_small").reshape(-1, LANES)
    res = _rowmap(lambda w_, g_, m_, v_: _adamw(w_, g_, m_, v_),
                  [_pack_small(w), g_small, _pack_small(m), _pack_small(v)],
                  row_outs=[(LANES, F32)] * 3, name="adamw_small")
    small_shapes = {n: shapes[n] for n in SMALL}
    out_g.update(_unpack_small(g_small, small_shapes))
    for dst, packed_res in zip((out_d, out_m, out_v), res):
        dst.update(_unpack_small(packed_res, small_shapes))
    return loss, dy, out_g, out_d, out_m, out_v


def kernel(x, w_in, b_in, rel_bias, sgu_ln_g, sgu_ln_b, w_s, b_s, lam_re, lam_im, log_dt, b_re, b_im, c_re, c_im, d_skip, w_glu, b_glu, w_pa, w_pb, w_pc, w_o, ln1_g, ln1_b, w_ffn_in, w_ffn_out, ln2_g, ln2_b, loss_target, m_w_in, m_b_in, m_rel_bias, m_sgu_ln_g, m_sgu_ln_b, m_w_s, m_b_s, m_lam_re, m_lam_im, m_log_dt, m_b_re, m_b_im, m_c_re, m_c_im, m_d_skip, m_w_glu, m_b_glu, m_w_pa, m_w_pb, m_w_pc, m_w_o, m_ln1_g, m_ln1_b, m_w_ffn_in, m_w_ffn_out, m_ln2_g, m_ln2_b, v_w_in, v_b_in, v_rel_bias, v_sgu_ln_g, v_sgu_ln_b, v_w_s, v_b_s, v_lam_re, v_lam_im, v_log_dt, v_b_re, v_b_im, v_c_re, v_c_im, v_d_skip, v_w_glu, v_b_glu, v_w_pa, v_w_pb, v_w_pc, v_w_o, v_ln1_g, v_ln1_b, v_w_ffn_in, v_w_ffn_out, v_ln2_g, v_ln2_b):
    args = dict(locals())
    w = {n: args[n] for n in WEIGHTS}
    m = {n: args["m_" + n] for n in WEIGHTS}
    v = {n: args["v_" + n] for n in WEIGHTS}
    loss, dx, g, d, nm, nv = _step(x[0], loss_target[0], w, m, v)
    return (loss, dx[None], *[g[n] for n in WEIGHTS], *[d[n] for n in WEIGHTS],
            *[nm[n] for n in WEIGHTS], *[nv[n] for n in WEIGHTS])
```

```python
import functools
import math
from typing import Callable, NamedTuple

import numpy as np
import jax
import jax.numpy as jnp
from jax import lax
from jax.experimental import pallas as pl
from jax.experimental.pallas import tpu as pltpu

F32 = jnp.float32
BF16 = jnp.bfloat16

MESH_AXES = ("x", "y", "c")
N_DEV = 8
DEPTH = 4

ATT_DILATIONS = (1, 4, 16)
ATT_STEPS = 128
HEADS_PER_GROUP = 8
HEAD_DIM = 64
QKV_WIDTH = 1536
WIDTH_A = HEADS_PER_GROUP * HEAD_DIM
ATT_BLOCK = 128
N_REL_BUCKETS = 32
REL_MAX_DIST = 2048
NEG_INF = -1e30
CHUNK = 128
WIDTH_B = 768
N_GROUPS_B = 6
WIDTH_C = 768
SSM_GROUP = 16
N_GROUPS_C = 48
SSM_STATE = 64
SSM_PACK = 8
N_SSM_BLOCKS = N_GROUPS_C // SSM_PACK
SSM_COLS = N_GROUPS_C * SSM_STATE
ALPHA = (2 * DEPTH) ** 0.25

ADAM_LR = 0.001
ADAM_B1 = 0.9
ADAM_B2 = 0.999
ADAM_EPS = 1e-08
ADAM_WD = 0.01
ADAM_STEP = 10

LANES = 128
SUBLANES = 8
VMEM_LIMIT = 48 * 1024 * 1024

SHARDED = ("w_in", "w_glu", "w_pa", "w_pb", "w_pc", "w_o", "w_ffn_in", "w_ffn_out")
TRANSPOSED = ("w_in", "w_pa", "w_pb", "w_pc", "w_ffn_in")
SMALL = ("b_in", "rel_bias", "sgu_ln_g", "sgu_ln_b", "w_s", "b_s", "lam_re", "lam_im", "log_dt",
         "b_re", "b_im", "c_re", "c_im", "d_skip", "b_glu", "ln1_g", "ln1_b", "ln2_g", "ln2_b")
WEIGHTS = ("w_in", "b_in", "rel_bias", "sgu_ln_g", "sgu_ln_b", "w_s", "b_s", "lam_re", "lam_im",
           "log_dt", "b_re", "b_im", "c_re", "c_im", "d_skip", "w_glu", "b_glu", "w_pa", "w_pb",
           "w_pc", "w_o", "ln1_g", "ln1_b", "w_ffn_in", "w_ffn_out", "ln2_g", "ln2_b")


def _pick(dim, target, mult):
    best = None
    for t in range(mult, min(dim, target) + 1, mult):
        if dim % t == 0:
            best = t
    return dim if best is None else best


def _cparams(*sem):
    return pltpu.CompilerParams(dimension_semantics=sem, vmem_limit_bytes=VMEM_LIMIT)


def _zero_map(ndim):
    return lambda *_: (0,) * ndim


_HBM = pl.BlockSpec(memory_space=pl.ANY)
_MESH_ID = pl.DeviceIdType.MESH


class _Ride(NamedTuple):
    inputs: tuple
    carried: tuple
    created: tuple
    n_sems: int
    start: Callable
    finish: Callable


def _pallas(body, *, grid, in_specs, out_specs, out_shape, args, scratch=(), semantics, rides=(), aliases=None, name):
    rides = [r for r in rides if r is not None]
    n_in, n_out, n_scr = len(args), len(out_shape), len(scratch)
    r_args, r_shapes, aliases, spans = [], [], dict(aliases or {}), []
    for r in rides:
        i0, o0 = len(r_args), len(r_shapes)
        r_args += [*r.inputs, *r.carried]
        for k, a in enumerate(r.carried):
            aliases[n_in + i0 + len(r.inputs) + k] = n_out + o0 + k
        r_shapes += [jax.ShapeDtypeStruct(a.shape, a.dtype) for a in r.carried] + list(r.created)
        spans.append((i0, len(r.inputs), o0, len(r.carried) + len(r.created)))

    def full_body(*refs):
        host_in, ride_in = refs[:n_in], refs[n_in:n_in + len(r_args)]
        p = n_in + len(r_args)
        host_out, ride_out = refs[p:p + n_out], refs[p + n_out:p + n_out + len(r_shapes)]
        p += n_out + len(r_shapes)
        host_scr, ride_sems = refs[p:p + n_scr], refs[p + n_scr:]
        ids = [pl.program_id(k) for k in range(len(grid))]
        first = functools.reduce(jnp.logical_and, [i == 0 for i in ids])
        last = functools.reduce(jnp.logical_and, [i == g - 1 for i, g in zip(ids, grid)])

        def each(method):
            for r, (i0, ni, o0, no), sems in zip(rides, spans, ride_sems):
                getattr(r, method)(ride_in[i0:i0 + ni], ride_out[o0:o0 + no], sems)

        if rides:
            pl.when(first)(lambda: each("start"))
        body(*host_in, *host_out, *host_scr)
        if rides:
            pl.when(last)(lambda: each("finish"))

    if rides:
        semantics = ("arbitrary",) * len(grid)
    outs = pl.pallas_call(
        full_body, grid=grid, in_specs=list(in_specs) + [_HBM] * len(r_args),
        out_specs=list(out_specs) + [_HBM] * len(r_shapes), out_shape=list(out_shape) + r_shapes,
        scratch_shapes=list(scratch) + [pltpu.SemaphoreType.DMA((r.n_sems,)) for r in rides],
        input_output_aliases=aliases, compiler_params=_cparams(*semantics), name=name)(*args, *r_args)
    return outs[:n_out], [outs[n_out + o0:n_out + o0 + no] for _, _, o0, no in spans]


def _rowmap(fn, rows, consts=(), stacks=(), row_outs=(), red_outs=(), tr=256, name=None, rides=None):
    t = rows[0].shape[0] if rows else stacks[0].shape[1]
    dtypes = [a.dtype for a in (*rows, *stacks)] + [dt for _, dt in row_outs]
    packed = any(jnp.dtype(dt).itemsize < 4 for dt in dtypes)
    tr = _pick(t, tr, 2 * SUBLANES if packed else SUBLANES)
    n_r, n_c, n_s, n_o = len(rows), len(consts), len(stacks), len(row_outs)

    def body(*refs):
        ins = [r[...] for r in refs[:n_r + n_c + n_s]]
        outs = refs[n_r + n_c + n_s:n_r + n_c + n_s + n_o]
        reds = refs[n_r + n_c + n_s + n_o:]
        res = fn(*ins)
        if not isinstance(res, (tuple, list)):
            res = (res,)
        for o, v in zip(outs, res[:n_o]):
            o[...] = v.astype(o.dtype)
        if reds:
            @pl.when(pl.program_id(0) == 0)
            def _():
                for r in reds:
                    r[...] = jnp.zeros_like(r)
            for r, v in zip(reds, res[n_o:]):
                r[...] += v

    in_specs = [pl.BlockSpec((tr, r.shape[1]), lambda i: (i, 0)) for r in rows]
    in_specs += [pl.BlockSpec(c.shape, _zero_map(c.ndim)) for c in consts]
    in_specs += [pl.BlockSpec((s.shape[0], tr, s.shape[2]), lambda i: (0, i, 0)) for s in stacks]
    out_specs = [pl.BlockSpec((tr, w), lambda i: (i, 0)) for w, _ in row_outs]
    out_specs += [pl.BlockSpec(s, _zero_map(len(s))) for s in red_outs]
    out_shape = [jax.ShapeDtypeStruct((t, w), dt) for w, dt in row_outs]
    out_shape += [jax.ShapeDtypeStruct(s, F32) for s in red_outs]
    outs, ride_outs = _pallas(body, grid=(t // tr,), in_specs=in_specs, out_specs=out_specs, out_shape=out_shape,
                              args=[*rows, *consts, *stacks], semantics=("arbitrary",), rides=rides or (), name=name)
    return outs if rides is None else (outs, ride_outs)


MM_VMEM_BUDGET = 36 * 1024 * 1024


def _divisors(dim, mult, must_divide=0):
    out = [t for t in range(dim, 0, -mult) if t % mult == 0 and dim % t == 0 and must_divide % t == 0]
    return out or [dim]


def _mm_tiles(m, n, k, a_bytes, b_bytes, out_bytes, extra_bytes, ta, b_off_n, b_off_k, out_off, tm, tn):
    tms = _divisors(m, LANES if ta else SUBLANES, out_off)
    tm = next((t for t in tms if t <= tm), tms[-1])
    tns = [t for t in _divisors(n, LANES, b_off_n) if t <= tn] or [_divisors(n, LANES, b_off_n)[-1]]
    for tn_ in tns:
        for tk in _divisors(k, LANES, b_off_k):
            acc = 0 if tk == k else tm * tn_ * 4
            need = 2 * (tm * tk * a_bytes + tk * tn_ * b_bytes + tm * tn_ * (out_bytes + extra_bytes)) + acc
            if need <= MM_VMEM_BUDGET:
                return tm, tn_, tk
    return tm, tns[-1], _divisors(k, LANES, b_off_k)[-1]


def _mm(a, b, *, ta=False, tb=False, bias=None, add=None, out_dtype=F32, b_off=0, n=None, tm=1024, tn=1024, name=None,
        rides=None, into=None):
    k, m = a.shape if ta else a.shape[::-1]
    if tb:
        n = b.shape[0] if n is None else n
        assert b.shape[1] == k and b_off + n <= b.shape[0]
    else:
        n = b.shape[1]
        assert b_off + k <= b.shape[0]
    out_rows, out_off, out_buf = (m, 0, None) if into is None else into
    extra = 4 if add is not None else 0
    tm, tn, tk = _mm_tiles(m, n, k, a.dtype.itemsize, b.dtype.itemsize, jnp.dtype(out_dtype).itemsize, extra, ta,
                           b_off if tb else 0, 0 if tb else b_off, out_off, tm, tn)
    nk = k // tk
    off_n, off_k = (b_off // tn, 0) if tb else (0, b_off // tk)
    off_m = out_off // tm
    dims = (((0 if ta else 1,), (1 if tb else 0,)), ((), ()))

    def body(*refs):
        a_ref, b_ref = refs[0], refs[1]
        rest = list(refs[2:])
        bias_ref = rest.pop(0) if bias is not None else None
        add_ref = rest.pop(0) if add is not None else None
        o_ref = rest.pop(0)
        part = lax.dot_general(a_ref[...].astype(BF16), b_ref[...].astype(BF16), dims, preferred_element_type=F32)

        def finish(r):
            if bias_ref is not None:
                r = r + bias_ref[...]
            if add_ref is not None:
                r = r + add_ref[...]
            o_ref[...] = r.astype(o_ref.dtype)

        if nk == 1:
            finish(part)
        else:
            acc_ref = rest.pop(0)
            kk = pl.program_id(2)

            @pl.when(kk == 0)
            def _():
                acc_ref[...] = part

            @pl.when(kk > 0)
            def _():
                acc_ref[...] += part

            @pl.when(kk == nk - 1)
            def _():
                finish(acc_ref[...])

    a_spec = pl.BlockSpec((tk, tm), lambda i, j, q: (q, i)) if ta else pl.BlockSpec((tm, tk), lambda i, j, q: (i, q))
    if tb:
        b_spec = pl.BlockSpec((tn, tk), lambda i, j, q: (j + off_n, q))
    else:
        b_spec = pl.BlockSpec((tk, tn), lambda i, j, q: (q + off_k, j))
    in_specs, args = [a_spec, b_spec], [a, b]
    if bias is not None:
        in_specs.append(pl.BlockSpec((1, tn), lambda i, j, q: (0, j)))
        args.append(bias)
    if add is not None:
        in_specs.append(pl.BlockSpec((tm, tn), lambda i, j, q: (i, j)))
        args.append(add)
    aliases = {}
    if out_buf is not None:
        assert out_buf.shape == (out_rows, n) and out_buf.dtype == jnp.dtype(out_dtype)
        in_specs.append(_HBM)
        args.append(out_buf)
        aliases = {len(args) - 1: 0}

    def body_in_place(*refs):
        body(*refs[:len(args) - 1], *refs[len(args):])

    outs, ride_outs = _pallas(
        body if out_buf is None else body_in_place, grid=(m // tm, n // tn, nk), in_specs=in_specs,
        out_specs=[pl.BlockSpec((tm, tn), lambda i, j, q: (i + off_m, j))],
        out_shape=[jax.ShapeDtypeStruct((out_rows, n), out_dtype)], args=args,
        scratch=[] if nk == 1 else [pltpu.VMEM((tm, tn), F32)],
        semantics=("parallel", "parallel", "arbitrary"), rides=rides or (), aliases=aliases, name=name)
    return outs[0] if rides is None else (outs[0], ride_outs)


def _mm_swiglu(a, w_t, *, name, rides=None):
    m, k = a.shape
    f = w_t.shape[0] // 2
    tm, tn, tk = _mm_tiles(m, f, k, a.dtype.itemsize, 2 * w_t.dtype.itemsize, 3 * 2, 0, False, 0, 0, 0, 1024, 512)
    assert tk == k, "the fused activation needs the whole contraction in one block"

    def body(a_ref, g_ref, u_ref, gate_ref, up_ref, act_ref):
        av = a_ref[...].astype(BF16)
        gate = lax.dot_general(av, g_ref[...].astype(BF16), _NT, preferred_element_type=F32)
        up = lax.dot_general(av, u_ref[...].astype(BF16), _NT, preferred_element_type=F32)
        gate_ref[...] = gate.astype(gate_ref.dtype)
        up_ref[...] = up.astype(up_ref.dtype)
        act_ref[...] = _swiglu2(gate, up).astype(act_ref.dtype)

    out_spec = pl.BlockSpec((tm, tn), lambda i, j: (i, j))
    outs, ride_outs = _pallas(
        body, grid=(m // tm, f // tn),
        in_specs=[pl.BlockSpec((tm, k), lambda i, j: (i, 0)), pl.BlockSpec((tn, k), lambda i, j: (j, 0)),
                  pl.BlockSpec((tn, k), lambda i, j: (j + f // tn, 0))],
        out_specs=[out_spec] * 3, out_shape=[jax.ShapeDtypeStruct((m, f), BF16)] * 3, args=[a, w_t, w_t],
        semantics=("parallel", "parallel"), rides=rides or (), name=name)
    return outs if rides is None else (outs, ride_outs)


def _ln(x, g, b, eps=1e-5):
    mu = jnp.mean(x, axis=-1, keepdims=True)
    var = jnp.mean(jnp.square(x - mu), axis=-1, keepdims=True)
    return (x - mu) * lax.rsqrt(var + eps) * g + b


def _post_norm(x, f, g, b):
    return _ln(ALPHA * x + f, g, b)


def _post_norm_bwd(x, f, dy, g, b):
    _, vjp = jax.vjp(_post_norm, x, f, g, b)
    return vjp(dy)


def _merge3(g0, g1, g2, pa, pb, pc):
    return jax.nn.sigmoid(g0) * pa + jax.nn.sigmoid(g1) * pb + jax.nn.sigmoid(g2) * pc


def _merge_args(gl, pa, pb, pc):
    d = pa.shape[1]
    return [a.astype(F32) for a in (gl[:, :d], gl[:, d:2 * d], gl[:, 2 * d:], pa, pb, pc)]


def _merge(gl, pa, pb, pc):
    return _merge3(*_merge_args(gl, pa, pb, pc))


def _merge_bwd(gl, pa, pb, pc, dm):
    _, vjp = jax.vjp(_merge3, *_merge_args(gl, pa, pb, pc))
    d0, d1, d2, dpa, dpb, dpc = vjp(dm)
    dgl = jnp.concatenate([d0, d1, d2], axis=1)
    return dgl, dpa, dpb, dpc, jnp.sum(dgl, axis=0, keepdims=True)


def _swiglu2(gate, up):
    return jax.nn.silu(gate) * up


def _swiglu_bwd(gate, up, dact):
    _, vjp = jax.vjp(_swiglu2, gate.astype(F32), up.astype(F32))
    dg, du = vjp(dact.astype(F32))
    return jnp.concatenate([dg, du], axis=1)


def _glu(ycp, lin):
    return ycp * jax.nn.sigmoid(lin)


def _glu_bwd(ycp, lin, dyc):
    _, vjp = jax.vjp(_glu, ycp, lin)
    dycp, dlin = vjp(dyc)
    return dycp, dlin, jnp.sum(dlin, axis=0, keepdims=True)


def _s5_out_bwd(ys, uc, dycp, dskip):
    _, vjp = jax.vjp(jax.nn.gelu, ys)
    dys = vjp(dycp)[0]
    return dys, dys * dskip, jnp.sum(dys * uc, axis=0, keepdims=True)


def _combine(o0, l0, o1, l1, o2, l2):
    m = jnp.maximum(jnp.maximum(l0, l1), l2)
    e0, e1, e2 = jnp.exp(l0 - m), jnp.exp(l1 - m), jnp.exp(l2 - m)
    s = e0 + e1 + e2
    return (e0 / s) * o0 + (e1 / s) * o1 + (e2 / s) * o2


def _combine_bwd(o0, l0, o1, l1, o2, l2, ya, dya, head_ones):
    m = jnp.maximum(jnp.maximum(l0, l1), l2)
    e0, e1, e2 = jnp.exp(l0 - m), jnp.exp(l1 - m), jnp.exp(l2 - m)
    s = e0 + e1 + e2
    dot_ya = jnp.dot(dya * ya, head_ones, precision=lax.Precision.HIGHEST, preferred_element_type=F32)
    w0, w1, w2 = e0 / s, e1 / s, e2 / s
    return w0 * dya, w1 * dya, w2 * dya, -w0 * dot_ya, -w1 * dot_ya, -w2 * dot_ya


def _loss_fn(y, tgt):
    err = y - tgt
    part = jnp.sum(jnp.sum(jnp.square(err), axis=1, keepdims=True), axis=0, keepdims=True) * (0.5 / y.shape[1])
    return err * (1.0 / y.shape[1]), jnp.broadcast_to(part, (1, LANES))


def _adamw(w, g, m, v):
    m = ADAM_B1 * m + (1.0 - ADAM_B1) * g
    v = ADAM_B2 * v + (1.0 - ADAM_B2) * jnp.square(g)
    m_hat = m / (1.0 - ADAM_B1 ** ADAM_STEP)
    v_hat = v / (1.0 - ADAM_B2 ** ADAM_STEP)
    delta = -ADAM_LR * (m_hat / (jnp.sqrt(v_hat) + ADAM_EPS) + ADAM_WD * w)
    return delta, m, v


def _sum_parts_fn(parts):
    g = parts[0].astype(F32)
    for j in range(1, parts.shape[0]):
        g = g + parts[j].astype(F32)
    return g


def _t5_bucket(dist):
    max_exact = N_REL_BUCKETS // 2
    d = np.maximum(dist, 1).astype(np.float32)
    scale = (N_REL_BUCKETS - max_exact) / math.log(REL_MAX_DIST / max_exact)
    large = max_exact + (np.log(d / max_exact) * scale).astype(np.int32)
    large = np.minimum(large, N_REL_BUCKETS - 1)
    return np.where(dist < max_exact, dist, large).astype(np.int32)


def _bucket_table(dilation):
    i = np.arange(ATT_BLOCK)[:, None]
    kk = np.arange(2 * ATT_BLOCK)[None, :]
    steps = ATT_BLOCK + i - kk
    return _t5_bucket(np.maximum(steps, 0) * dilation)


def _bias_fwd(rel_bias, buckets, g):
    def body(rel_ref, bk_ref, o_ref):
        bk = bk_ref[...]
        for h in range(HEADS_PER_GROUP):
            acc = jnp.zeros(bk.shape, F32)
            for b in range(N_REL_BUCKETS):
                acc = jnp.where(bk == b, rel_ref[b, g * HEADS_PER_GROUP + h], acc)
            o_ref[h] = acc

    return pl.pallas_call(
        body, in_specs=[pl.BlockSpec(memory_space=pltpu.SMEM), pl.BlockSpec(memory_space=pltpu.VMEM)],
        out_specs=pl.BlockSpec(memory_space=pltpu.VMEM),
        out_shape=jax.ShapeDtypeStruct((HEADS_PER_GROUP, ATT_BLOCK, 2 * ATT_BLOCK), F32),
        name=f"rel_bias_fwd{g}")(rel_bias, buckets)


def _bias_bwd(dbias, buckets, g):
    def body(db_ref, bk_ref, o_ref):
        bk = bk_ref[...]
        row = lax.broadcasted_iota(jnp.int32, (N_REL_BUCKETS, LANES), 0)
        col = lax.broadcasted_iota(jnp.int32, (N_REL_BUCKETS, LANES), 1)
        acc = jnp.zeros((N_REL_BUCKETS, LANES), F32)
        for h in range(HEADS_PER_GROUP):
            d = db_ref[h]
            for b in range(N_REL_BUCKETS):
                s = jnp.sum(jnp.sum(jnp.where(bk == b, d, 0.0), axis=1, keepdims=True), axis=0, keepdims=True)
                acc = acc + jnp.where((row == b) & (col == g * HEADS_PER_GROUP + h), s, 0.0)
        o_ref[...] = acc

    return pl.pallas_call(
        body, in_specs=[pl.BlockSpec(memory_space=pltpu.VMEM), pl.BlockSpec(memory_space=pltpu.VMEM)],
        out_specs=pl.BlockSpec(memory_space=pltpu.VMEM),
        out_shape=jax.ShapeDtypeStruct((N_REL_BUCKETS, LANES), F32), name=f"rel_bias_bwd{g}")(dbias, buckets)


_NT = (((1,), (1,)), ((), ()))
_TN = (((0,), (0,)), ((), ()))
_QKV_BLOCKS = 3 * QKV_WIDTH // WIDTH_A


def _band_mask(n_is_first):
    i = lax.broadcasted_iota(jnp.int32, (ATT_BLOCK, 2 * ATT_BLOCK), 0)
    kk = lax.broadcasted_iota(jnp.int32, (ATT_BLOCK, 2 * ATT_BLOCK), 1)
    return (kk >= i) & (kk <= i + ATT_STEPS) & ((kk >= ATT_BLOCK) | jnp.logical_not(n_is_first))


def _head(ref, h):
    return ref[:, h * HEAD_DIM:(h + 1) * HEAD_DIM]


def _attn_specs(g, d):
    blk = (ATT_BLOCK, WIDTH_A)
    q = pl.BlockSpec(blk, lambda c, n: (n, c * _QKV_BLOCKS + g))
    kp = pl.BlockSpec(blk, lambda c, n: (jnp.maximum(n - 1, 0), c * _QKV_BLOCKS + 3 + g))
    kc = pl.BlockSpec(blk, lambda c, n: (n, c * _QKV_BLOCKS + 3 + g))
    vp = pl.BlockSpec(blk, lambda c, n: (jnp.maximum(n - 1, 0), c * _QKV_BLOCKS + 6 + g))
    vc = pl.BlockSpec(blk, lambda c, n: (n, c * _QKV_BLOCKS + 6 + g))
    return [q, kp, kc, vp, vc]


def _attn_fwd(qkv, bias, g, d, comm):
    t = qkv.shape[0]
    lq = t // d
    nb = lq // ATT_BLOCK
    scale = HEAD_DIM ** -0.5

    def body(q_ref, kp_ref, kc_ref, vp_ref, vc_ref, b_ref, o_ref, l_ref):
        mask = _band_mask(pl.program_id(1) == 0)
        for h in range(HEADS_PER_GROUP):
            qh = _head(q_ref, h).astype(BF16)
            kh = jnp.concatenate([_head(kp_ref, h), _head(kc_ref, h)], axis=0).astype(BF16)
            vh = jnp.concatenate([_head(vp_ref, h), _head(vc_ref, h)], axis=0).astype(BF16)
            s = lax.dot_general(qh, kh, _NT, preferred_element_type=F32) * scale + b_ref[h]
            s = jnp.where(mask, s, NEG_INF)
            m = jnp.max(s, axis=1, keepdims=True)
            p = jnp.exp(s - m)
            den = jnp.sum(p, axis=1, keepdims=True)
            o = jnp.dot(p.astype(BF16), vh, preferred_element_type=F32) / den
            o_ref[:, h * HEAD_DIM:(h + 1) * HEAD_DIM] = o
            l_ref[:, h * HEAD_DIM:(h + 1) * HEAD_DIM] = jnp.broadcast_to(m + jnp.log(den), (ATT_BLOCK, HEAD_DIM))

    out_spec = pl.BlockSpec((ATT_BLOCK, WIDTH_A), lambda c, n: (n, c))
    (o, lse), ride_outs = _pallas(
        body, grid=(d, nb),
        in_specs=_attn_specs(g, d) + [pl.BlockSpec(bias.shape, _zero_map(3))],
        out_specs=[out_spec, out_spec],
        out_shape=[jax.ShapeDtypeStruct((lq, d * WIDTH_A), F32)] * 2,
        args=[*([qkv.reshape(lq, d * 3 * QKV_WIDTH)] * 5), bias],
        semantics=("parallel", "parallel"), rides=[comm.ride(f"attn_fwd{g}")], name=f"attn_fwd{g}")
    comm.took(ride_outs)
    return o.reshape(t, WIDTH_A), lse.reshape(t, WIDTH_A)


def _attn_bwd(qkv, bias, do, lse, corr, g, d):
    t = qkv.shape[0]
    lq = t // d
    nb = lq // ATT_BLOCK
    scale = HEAD_DIM ** -0.5

    def body(k_ref, v_ref, q0_ref, q1_ref, do0_ref, do1_ref, l0_ref, l1_ref, c0_ref, c1_ref, b_ref,
             dq_ref, dk_ref, dv_ref, db_ref, dq_prev):
        c, j = pl.program_id(0), pl.program_id(1)

        @pl.when((c == 0) & (j == 0))
        def _():
            db_ref[...] = jnp.zeros_like(db_ref)

        @pl.when(j == 0)
        def _():
            dq_prev[...] = jnp.zeros_like(dq_prev)

        i = lax.broadcasted_iota(jnp.int32, (ATT_BLOCK, ATT_BLOCK), 0)
        kk = lax.broadcasted_iota(jnp.int32, (ATT_BLOCK, ATT_BLOCK), 1)
        mask0 = kk <= i
        mask1 = (kk >= i) & (j + 1 < nb)
        for h in range(HEADS_PER_GROUP):
            kh = _head(k_ref, h).astype(BF16)
            vh = _head(v_ref, h).astype(BF16)
            cols = slice(h * HEAD_DIM, (h + 1) * HEAD_DIM)
            dk = jnp.zeros((ATT_BLOCK, HEAD_DIM), F32)
            dv = jnp.zeros((ATT_BLOCK, HEAD_DIM), F32)
            dq_parts = []
            parts = ((q0_ref, do0_ref, l0_ref, c0_ref, mask0, ATT_BLOCK), (q1_ref, do1_ref, l1_ref, c1_ref, mask1, 0))
            for q_ref, do_ref, l_ref, c_ref, mask, off in parts:
                qh = _head(q_ref, h).astype(BF16)
                doh = _head(do_ref, h).astype(BF16)
                s = lax.dot_general(qh, kh, _NT, preferred_element_type=F32) * scale + b_ref[h, :, off:off + ATT_BLOCK]
                s = jnp.where(mask, s, NEG_INF)
                p = jnp.exp(s - l_ref[:, h * HEAD_DIM:h * HEAD_DIM + 1])
                dp = lax.dot_general(doh, vh, _NT, preferred_element_type=F32)
                ds = p * (dp + c_ref[:, h * HEAD_DIM:h * HEAD_DIM + 1])
                dsb = ds.astype(BF16)
                dv = dv + lax.dot_general(p.astype(BF16), doh, _TN, preferred_element_type=F32)
                dk = dk + lax.dot_general(dsb, qh, _TN, preferred_element_type=F32)
                dq_parts.append(jnp.dot(dsb, kh, preferred_element_type=F32))
                db_ref[h, :, off:off + ATT_BLOCK] += ds
            dk_ref[:, cols] = dk * scale
            dv_ref[:, cols] = dv
            dq_ref[:, cols] = (dq_prev[:, cols] + dq_parts[0]) * scale
            dq_prev[:, cols] = dq_parts[1]

    blk = (ATT_BLOCK, WIDTH_A)
    nxt = lambda n: jnp.minimum(n + 1, nb - 1)
    k_spec = pl.BlockSpec(blk, lambda c, n: (n, c * _QKV_BLOCKS + 3 + g))
    v_spec = pl.BlockSpec(blk, lambda c, n: (n, c * _QKV_BLOCKS + 6 + g))
    q0_spec = pl.BlockSpec(blk, lambda c, n: (n, c * _QKV_BLOCKS + g))
    q1_spec = pl.BlockSpec(blk, lambda c, n: (nxt(n), c * _QKV_BLOCKS + g))
    r0 = pl.BlockSpec(blk, lambda c, n: (n, c))
    r1 = pl.BlockSpec(blk, lambda c, n: (nxt(n), c))
    view = lambda a: a.reshape(lq, d * WIDTH_A)
    qv = qkv.reshape(lq, d * 3 * QKV_WIDTH)
    dq, dk, dv, dbias = pl.pallas_call(
        body, grid=(d, nb),
        in_specs=[k_spec, v_spec, q0_spec, q1_spec, r0, r1, r0, r1, r0, r1, pl.BlockSpec(bias.shape, _zero_map(3))],
        out_specs=[r0, r0, r0, pl.BlockSpec(bias.shape, _zero_map(3))],
        out_shape=[jax.ShapeDtypeStruct((lq, d * WIDTH_A), F32)] * 3 + [jax.ShapeDtypeStruct(bias.shape, F32)],
        scratch_shapes=[pltpu.VMEM(blk, F32)],
        compiler_params=_cparams("arbitrary", "arbitrary"), name=f"attn_bwd{g}",
    )(qv, qv, qv, qv, view(do), view(do), view(lse), view(lse), view(corr), view(corr), bias)
    return dq.reshape(t, WIDTH_A), dk.reshape(t, WIDTH_A), dv.reshape(t, WIDTH_A), dbias


def _tril_mask():
    r = lax.broadcasted_iota(jnp.int32, (CHUNK, CHUNK), 0)
    c = lax.broadcasted_iota(jnp.int32, (CHUNK, CHUNK), 1)
    return c <= r


def _gmlp_fwd(zb, ln_g, ln_b, w_s, b_s_t):
    t = zb.shape[0]
    tr = _pick(t, 2 * CHUNK, CHUNK)

    def body(z_ref, g_ref, b_ref, ws_ref, bs_ref, o_ref):
        tri = _tril_mask()
        z = jax.nn.gelu(z_ref[...])
        u = z[:, :WIDTH_B]
        vn = _ln(z[:, WIDTH_B:], g_ref[...], b_ref[...])
        for ch in range(tr // CHUNK):
            rows = slice(ch * CHUNK, (ch + 1) * CHUNK)
            for gi in range(N_GROUPS_B):
                cols = slice(gi * CHUNK, (gi + 1) * CHUNK)
                w = jnp.where(tri, ws_ref[gi], 0.0).astype(BF16)
                mixed = jnp.dot(w, vn[rows, cols].astype(BF16), preferred_element_type=F32) + bs_ref[:, gi:gi + 1]
                o_ref[rows, cols] = (u[rows, cols] * mixed).astype(o_ref.dtype)

    return pl.pallas_call(
        body, grid=(t // tr,),
        in_specs=[pl.BlockSpec((tr, 2 * WIDTH_B), lambda i: (i, 0)), pl.BlockSpec(ln_g.shape, _zero_map(2)),
                  pl.BlockSpec(ln_b.shape, _zero_map(2)), pl.BlockSpec(w_s.shape, _zero_map(3)),
                  pl.BlockSpec(b_s_t.shape, _zero_map(2))],
        out_specs=pl.BlockSpec((tr, WIDTH_B), lambda i: (i, 0)),
        out_shape=jax.ShapeDtypeStruct((t, WIDTH_B), BF16),
        compiler_params=_cparams("parallel"), name="gmlp_fwd")(zb, ln_g, ln_b, w_s, b_s_t)


def _gmlp_bwd(zb, dyb, ln_g, ln_b, w_s, b_s_t, group_sel):
    t = zb.shape[0]
    tr = _pick(t, 2 * CHUNK, CHUNK)

    def body(z_ref, dy_ref, g_ref, b_ref, ws_ref, bs_ref, sel_ref, dz_ref, dzs_ref, dg_ref, db_ref, dws_ref, dbs_ref,
             du_s, dvn_s, dm_s):
        @pl.when(pl.program_id(0) == 0)
        def _():
            dzs_ref[...] = jnp.zeros_like(dzs_ref)
            dg_ref[...] = jnp.zeros_like(dg_ref)
            db_ref[...] = jnp.zeros_like(db_ref)
            dws_ref[...] = jnp.zeros_like(dws_ref)
            dbs_ref[...] = jnp.zeros_like(dbs_ref)

        tri = _tril_mask()
        z, gelu_vjp = jax.vjp(jax.nn.gelu, z_ref[...])
        u = z[:, :WIDTH_B]
        vn, ln_vjp = jax.vjp(_ln, z[:, WIDTH_B:], g_ref[...], b_ref[...])
        dy = dy_ref[...]
        for ch in range(tr // CHUNK):
            rows = slice(ch * CHUNK, (ch + 1) * CHUNK)
            for gi in range(N_GROUPS_B):
                cols = slice(gi * CHUNK, (gi + 1) * CHUNK)
                w = jnp.where(tri, ws_ref[gi], 0.0).astype(BF16)
                vg = vn[rows, cols].astype(BF16)
                mixed = jnp.dot(w, vg, preferred_element_type=F32) + bs_ref[:, gi:gi + 1]
                dyg = dy[rows, cols]
                dm = dyg * u[rows, cols]
                dmb = dm.astype(BF16)
                du_s[rows, cols] = dyg * mixed
                dm_s[rows, cols] = dm
                dvn_s[rows, cols] = lax.dot_general(w, dmb, _TN, preferred_element_type=F32)
                dws_ref[gi] += jnp.where(tri, lax.dot_general(dmb, vg, _NT, preferred_element_type=F32), 0.0)
            dbs_ref[...] += jnp.dot(dm_s[rows, :], sel_ref[...], precision=lax.Precision.HIGHEST,
                                    preferred_element_type=F32)
        dv, dg, db = ln_vjp(dvn_s[...])
        dg_ref[...] += dg
        db_ref[...] += db
        dz = gelu_vjp(jnp.concatenate([du_s[...], dv], axis=1))[0]
        dz_ref[...] = dz.astype(dz_ref.dtype)
        dzs_ref[...] += jnp.sum(dz, axis=0, keepdims=True)

    full = lambda a: pl.BlockSpec(a.shape, _zero_map(a.ndim))
    return pl.pallas_call(
        body, grid=(t // tr,),
        in_specs=[pl.BlockSpec((tr, 2 * WIDTH_B), lambda i: (i, 0)), pl.BlockSpec((tr, WIDTH_B), lambda i: (i, 0)),
                  full(ln_g), full(ln_b), full(w_s), full(b_s_t), full(group_sel)],
        out_specs=[pl.BlockSpec((tr, 2 * WIDTH_B), lambda i: (i, 0)), pl.BlockSpec((1, 2 * WIDTH_B), _zero_map(2)),
                   full(ln_g), full(ln_b), full(w_s), pl.BlockSpec((CHUNK, LANES), _zero_map(2))],
        out_shape=[jax.ShapeDtypeStruct((t, 2 * WIDTH_B), BF16), jax.ShapeDtypeStruct((1, 2 * WIDTH_B), F32),
                   jax.ShapeDtypeStruct(ln_g.shape, F32),
                   jax.ShapeDtypeStruct(ln_b.shape, F32), jax.ShapeDtypeStruct(w_s.shape, F32),
                   jax.ShapeDtypeStruct((CHUNK, LANES), F32)],
        scratch_shapes=[pltpu.VMEM((tr, WIDTH_B), F32)] * 3,
        compiler_params=_cparams("arbitrary"), name="gmlp_bwd")(zb, dyb, ln_g, ln_b, w_s, b_s_t, group_sel)


def _s5_disc(lr, li, ldt, br_t, bi_t):
    dt = jnp.exp(ldt)
    mag = jnp.exp(lr * dt)
    ab_re = mag * jnp.cos(li * dt)
    ab_im = mag * jnp.sin(li * dt)
    nrm = lr * lr + li * li
    cr = ((ab_re - 1.0) * lr + ab_im * li) / nrm
    ci = (ab_im * lr - (ab_re - 1.0) * li) / nrm
    return ab_re, ab_im, cr * br_t - ci * bi_t, cr * bi_t + ci * br_t


def _vmem_call(fn, args, out_shape, name):
    def body(*refs):
        res = fn(*[r[...] for r in refs[:len(args)]])
        for o, v in zip(refs[len(args):], res):
            o[...] = v

    vm = pl.BlockSpec(memory_space=pltpu.VMEM)
    return pl.pallas_call(body, in_specs=[vm] * len(args), out_specs=[vm] * len(out_shape),
                          out_shape=out_shape, name=name)(*args)


def _s5_disc_fwd(lr, li, ldt, br_t, bi_t):
    s1 = jax.ShapeDtypeStruct(lr.shape, F32)
    s2 = jax.ShapeDtypeStruct(br_t.shape, F32)
    return _vmem_call(_s5_disc, [lr, li, ldt, br_t, bi_t], [s1, s1, s2, s2], "s5_disc_fwd")


def _s5_disc_bwd(lr, li, ldt, br_t, bi_t, cts):
    def fn(lr, li, ldt, br_t, bi_t, d0, d1, d2, d3):
        _, vjp = jax.vjp(_s5_disc, lr, li, ldt, br_t, bi_t)
        return vjp((d0, d1, d2, d3))

    shp = [jax.ShapeDtypeStruct(a.shape, F32) for a in (lr, li, ldt, br_t, bi_t)]
    return _vmem_call(fn, [lr, li, ldt, br_t, bi_t, *cts], shp, "s5_disc_bwd")


_SCAN_ROWS = SSM_COLS // LANES
_SCAN_CHUNK = 128
_SSM_IN = SSM_PACK * SSM_GROUP
_SSM_ST = SSM_PACK * SSM_STATE


def _packed_in(xb, m_ref):
    return jnp.concatenate([jnp.dot(xb[:, j * _SSM_IN:(j + 1) * _SSM_IN], m_ref[j], preferred_element_type=F32)
                            for j in range(N_SSM_BLOCKS)], axis=1)


def _packed_out(xb, m_ref):
    return jnp.concatenate([lax.dot_general(xb[:, j * _SSM_ST:(j + 1) * _SSM_ST], m_ref[j], _NT, preferred_element_type=F32)
                            for j in range(N_SSM_BLOCKS)], axis=1)


def _s5_fwd(uc, mats, are, aim, d_skip, comm):
    t = uc.shape[0]
    tc = _pick(t, _SCAN_CHUNK, SUBLANES)

    def body(u_ref, br_ref, bi_ref, cr_ref, ci_ref, ar_ref, ai_ref, d_ref, xr_ref, xi_ref, ys_ref, ycp_ref,
             sr, si, st_ref):
        @pl.when(pl.program_id(0) == 0)
        def _():
            st_ref[...] = jnp.zeros_like(st_ref)

        u = u_ref[...]
        ub = u.astype(BF16)
        sr[...] = _packed_in(ub, br_ref).reshape(tc, _SCAN_ROWS, LANES)
        si[...] = _packed_in(ub, bi_ref).reshape(tc, _SCAN_ROWS, LANES)
        ar, ai = ar_ref[...], ai_ref[...]

        def step(i, carry):
            xr, xi = carry
            nr = ar * xr - ai * xi + sr[i]
            ni = ar * xi + ai * xr + si[i]
            sr[i] = nr
            si[i] = ni
            return nr, ni

        xr, xi = lax.fori_loop(0, tc, step, (st_ref[0], st_ref[1]), unroll=8)
        st_ref[0] = xr
        st_ref[1] = xi
        x_re = sr[...].reshape(tc, SSM_COLS)
        x_im = si[...].reshape(tc, SSM_COLS)
        xr_ref[...] = x_re
        xi_ref[...] = x_im
        ys = _packed_out(x_re.astype(BF16), cr_ref) - _packed_out(x_im.astype(BF16), ci_ref) + d_ref[...] * u
        ys_ref[...] = ys
        ycp_ref[...] = jax.nn.gelu(ys)

    row = lambda w: pl.BlockSpec((tc, w), lambda i: (i, 0))
    mat = pl.BlockSpec(mats[0].shape, _zero_map(3))
    par = pl.BlockSpec((_SCAN_ROWS, LANES), _zero_map(2))
    wide, narrow = jax.ShapeDtypeStruct((t, SSM_COLS), F32), jax.ShapeDtypeStruct((t, WIDTH_C), F32)
    outs, ride_outs = _pallas(
        body, grid=(t // tc,), in_specs=[row(WIDTH_C), mat, mat, mat, mat, par, par, pl.BlockSpec(d_skip.shape, _zero_map(2))],
        out_specs=[row(SSM_COLS), row(SSM_COLS), row(WIDTH_C), row(WIDTH_C)], out_shape=[wide, wide, narrow, narrow],
        args=[uc, *mats, are, aim, d_skip],
        scratch=[pltpu.VMEM((tc, _SCAN_ROWS, LANES), F32)] * 2 + [pltpu.VMEM((2, _SCAN_ROWS, LANES), F32)],
        semantics=("arbitrary",), rides=[comm.ride("s5_fwd")], name="s5_fwd")
    comm.took(ride_outs)
    return outs


def _s5_bwd(dys, duc_skip, uc, xr, xi, mats, are, aim):
    t = dys.shape[0]
    tc = _pick(t, _SCAN_CHUNK, SUBLANES)
    nc = t // tc

    def body(dy_ref, ds_ref, u_ref, xr_ref, xi_ref, pr_ref, pi_ref, br_ref, bi_ref, cr_ref, ci_ref, ar_ref, ai_ref,
             du_ref, dbr_ref, dbi_ref, dcr_ref, dci_ref, dar_ref, dai_ref, gr, gi, x3r, x3i, st_ref):
        step_id = pl.program_id(0)

        @pl.when(step_id == 0)
        def _():
            st_ref[...] = jnp.zeros_like(st_ref)
            for ref in (dbr_ref, dbi_ref, dcr_ref, dci_ref, dar_ref, dai_ref):
                ref[...] = jnp.zeros_like(ref)

        dyb = dy_ref[...].astype(BF16)
        x_re, x_im = xr_ref[...], xi_ref[...]
        gr[...] = _packed_in(dyb, cr_ref).reshape(tc, _SCAN_ROWS, LANES)
        gi[...] = (-_packed_in(dyb, ci_ref)).reshape(tc, _SCAN_ROWS, LANES)
        x3r[...] = x_re.reshape(tc, _SCAN_ROWS, LANES)
        x3i[...] = x_im.reshape(tc, _SCAN_ROWS, LANES)
        ar, ai = ar_ref[...], ai_ref[...]

        def update(i, carry, pxr, pxi):
            g_r, g_i, dar, dai = carry
            ngr = gr[i] + ar * g_r + ai * g_i
            ngi = gi[i] - ai * g_r + ar * g_i
            gr[i] = ngr
            gi[i] = ngi
            return ngr, ngi, dar + ngr * pxr + ngi * pxi, dai - ngr * pxi + ngi * pxr

        def step(s, carry):
            i = tc - 1 - s
            return update(i, carry, x3r[i - 1], x3i[i - 1])

        zero = jnp.zeros((_SCAN_ROWS, LANES), F32)
        carry = lax.fori_loop(0, tc - 1, step, (st_ref[0], st_ref[1], zero, zero), unroll=8)
        has_prev = (step_id < nc - 1).astype(F32)
        last = SUBLANES - 1
        p_re = pr_ref[last:, :].reshape(1, _SCAN_ROWS, LANES)[0] * has_prev
        p_im = pi_ref[last:, :].reshape(1, _SCAN_ROWS, LANES)[0] * has_prev
        g_r, g_i, dar, dai = update(0, carry, p_re, p_im)
        st_ref[0] = g_r
        st_ref[1] = g_i
        dar_ref[...] += dar
        dai_ref[...] += dai

        g_re = gr[...].reshape(tc, SSM_COLS).astype(BF16)
        g_im = gi[...].reshape(tc, SSM_COLS).astype(BF16)
        du_ref[...] = ds_ref[...] + _packed_out(g_re, br_ref) + _packed_out(g_im, bi_ref)
        ub, xrb, xib = u_ref[...].astype(BF16), x_re.astype(BF16), x_im.astype(BF16)
        for j in range(N_SSM_BLOCKS):
            narrow, wide = slice(j * _SSM_IN, (j + 1) * _SSM_IN), slice(j * _SSM_ST, (j + 1) * _SSM_ST)
            dbr_ref[j] += lax.dot_general(ub[:, narrow], g_re[:, wide], _TN, preferred_element_type=F32)
            dbi_ref[j] += lax.dot_general(ub[:, narrow], g_im[:, wide], _TN, preferred_element_type=F32)
            dcr_ref[j] += lax.dot_general(dyb[:, narrow], xrb[:, wide], _TN, preferred_element_type=F32)
            dci_ref[j] -= lax.dot_general(dyb[:, narrow], xib[:, wide], _TN, preferred_element_type=F32)

    rev = lambda w: pl.BlockSpec((tc, w), lambda i: (nc - 1 - i, 0))
    prev = pl.BlockSpec((SUBLANES, SSM_COLS), lambda i: (jnp.maximum((nc - 1 - i) * (tc // SUBLANES) - 1, 0), 0))
    mat = pl.BlockSpec(mats[0].shape, _zero_map(3))
    par = pl.BlockSpec((_SCAN_ROWS, LANES), _zero_map(2))
    msh = jax.ShapeDtypeStruct(mats[0].shape, F32)
    psh = jax.ShapeDtypeStruct((_SCAN_ROWS, LANES), F32)
    return pl.pallas_call(
        body, grid=(nc,),
        in_specs=[rev(WIDTH_C), rev(WIDTH_C), rev(WIDTH_C), rev(SSM_COLS), rev(SSM_COLS), prev, prev,
                  mat, mat, mat, mat, par, par],
        out_specs=[rev(WIDTH_C), mat, mat, mat, mat, par, par],
        out_shape=[jax.ShapeDtypeStruct((t, WIDTH_C), F32), msh, msh, msh, msh, psh, psh],
        scratch_shapes=[pltpu.VMEM((tc, _SCAN_ROWS, LANES), F32)] * 4 + [pltpu.VMEM((2, _SCAN_ROWS, LANES), F32)],
        compiler_params=_cparams("arbitrary"), name="s5_bwd")(dys, duc_skip, uc, xr, xi, xr, xi, *mats, are, aim)


def _diag_blocks(a, b):
    return [(j, i, slice(i * a, (i + 1) * a), slice(i * b, (i + 1) * b))
            for j in range(N_SSM_BLOCKS) for i in range(SSM_PACK)]


def _block_diag(ms):
    _, a, b = ms[0].shape

    def body(*refs):
        for m_ref, o_ref in zip(refs[:len(ms)], refs[len(ms):]):
            o_ref[...] = jnp.zeros_like(o_ref)
            for j, i, rows, cols in _diag_blocks(a, b):
                o_ref[j, rows, cols] = m_ref[j * SSM_PACK + i].astype(o_ref.dtype)

    vm = pl.BlockSpec(memory_space=pltpu.VMEM)
    shape = jax.ShapeDtypeStruct((N_SSM_BLOCKS, SSM_PACK * a, SSM_PACK * b), BF16)
    return pl.pallas_call(body, in_specs=[vm] * len(ms), out_specs=[vm] * len(ms), out_shape=[shape] * len(ms),
                          name="s5_block_diag")(*ms)


def _block_diag_extract(ms, a, b):
    def body(*refs):
        for m_ref, o_ref in zip(refs[:len(ms)], refs[len(ms):]):
            for j, i, rows, cols in _diag_blocks(a, b):
                o_ref[j * SSM_PACK + i] = m_ref[j, rows, cols]

    vm = pl.BlockSpec(memory_space=pltpu.VMEM)
    shape = jax.ShapeDtypeStruct((N_GROUPS_C, a, b), F32)
    return pl.pallas_call(body, in_specs=[vm] * len(ms), out_specs=[vm] * len(ms), out_shape=[shape] * len(ms),
                          name="s5_block_diag_extract")(*ms)


def _exchange(src, *, gather, name):
    shape = src.shape if gather else src.shape[1:]

    def body(src_ref, out_ref, send_sems, recv_sems, local_sem):
        x, y, c = lax.axis_index("x"), lax.axis_index("y"), lax.axis_index("c")
        me = 4 * x + 2 * y + c
        copies = []
        for r in range(1, N_DEV):
            px = 1 - x if r & 4 else x
            py = 1 - y if r & 2 else y
            pc = 1 - c if r & 1 else c
            piece = src_ref if gather else src_ref.at[4 * px + 2 * py + pc]
            cp = pltpu.make_async_remote_copy(
                src_ref=piece, dst_ref=out_ref.at[me], send_sem=send_sems.at[r - 1], recv_sem=recv_sems.at[r - 1],
                device_id=(px, py, pc), device_id_type=pl.DeviceIdType.MESH)
            cp.start()
            copies.append(cp)
        mine = pltpu.make_async_copy(src_ref if gather else src_ref.at[me], out_ref.at[me], local_sem)
        mine.start()
        for cp in copies:
            cp.wait()
        mine.wait()

    hbm = pl.BlockSpec(memory_space=pl.ANY)
    return pl.pallas_call(
        body, in_specs=[hbm], out_specs=hbm, out_shape=jax.ShapeDtypeStruct((N_DEV,) + tuple(shape), src.dtype),
        scratch_shapes=[pltpu.SemaphoreType.DMA((N_DEV - 1,)), pltpu.SemaphoreType.DMA((N_DEV - 1,)),
                        pltpu.SemaphoreType.DMA(())],
        name=name)(src)


def _mesh_place():
    x, y, c = lax.axis_index("x"), lax.axis_index("y"), lax.axis_index("c")
    other_chips = [(1 - x, y), (x, 1 - y), (1 - x, 1 - y)]
    return x, y, c, other_chips


def _gather_layer(srcs, layer, name):
    n = len(srcs)

    def body(*refs):
        src = [r.at[layer] for r in refs[:n]]
        out = refs[n:2 * n]
        send_sems, recv_sems, local_sems = refs[2 * n:]
        x, y, c, chips = _mesh_place()
        me, sibling = (x, y, c), (x, y, 1 - c)

        def copy(t, k, block, to, from_src=False):
            slot = 4 * block[0] + 2 * block[1] + block[2]
            return pltpu.make_async_remote_copy(
                src_ref=src[t] if from_src else out[t].at[slot], dst_ref=out[t].at[slot],
                send_sem=send_sems.at[t, k], recv_sem=recv_sems.at[t, k], device_id=to, device_id_type=_MESH_ID)

        mine = [pltpu.make_async_copy(src[t], out[t].at[4 * x + 2 * y + c], local_sems.at[t]) for t in range(n)]
        for cp in mine:
            cp.start()
        first = []
        for t in range(n):
            first.append(copy(t, 0, me, sibling, True))
            first += [copy(t, 1 + j, me, (*chip, c), True) for j, chip in enumerate(chips)]
        for cp in first:
            cp.start()
        passed = []
        for j, chip in enumerate(chips):
            for t in range(n):
                copy(t, 1 + j, (*chip, c), me).wait_recv()
                fwd = copy(t, 4 + j, (*chip, c), sibling)
                fwd.start()
                passed.append(fwd)
        for t in range(n):
            copy(t, 0, sibling, me).wait_recv()
            for j, chip in enumerate(chips):
                copy(t, 4 + j, (*chip, 1 - c), me).wait_recv()
        for cp in first + passed:
            cp.wait_send()
        for cp in mine:
            cp.wait()

    return pl.pallas_call(
        body, in_specs=[_HBM] * n, out_specs=[_HBM] * n,
        out_shape=[jax.ShapeDtypeStruct((N_DEV,) + s.shape[1:], s.dtype) for s in srcs],
        scratch_shapes=[pltpu.SemaphoreType.DMA((n, N_DEV - 1)), pltpu.SemaphoreType.DMA((n, N_DEV - 1)),
                        pltpu.SemaphoreType.DMA((n,))],
        name=name)(*srcs)


def _scatter_pair(srcs, name):
    n = len(srcs)

    def body(*refs):
        src, out = refs[:n], refs[n:2 * n]
        send_sems, recv_sems = refs[2 * n:]
        x, y, c, _ = _mesh_place()
        copies = [pltpu.make_async_remote_copy(
            src_ref=src[t].at[:, 1 - c], dst_ref=out[t], send_sem=send_sems.at[t], recv_sem=recv_sems.at[t],
            device_id=(x, y, 1 - c), device_id_type=_MESH_ID) for t in range(n)]
        for cp in copies:
            cp.start()
        for cp in copies:
            cp.wait()

    return pl.pallas_call(
        body, in_specs=[_HBM] * n, out_specs=[_HBM] * n,
        out_shape=[jax.ShapeDtypeStruct((s.shape[0],) + s.shape[2:], s.dtype) for s in srcs],
        scratch_shapes=[pltpu.SemaphoreType.DMA((n,)), pltpu.SemaphoreType.DMA((n,))], name=name)(*srcs)


def _pair_add(src, recv, name):
    nchip, _, r, cdim = src.shape
    tr = _pick(r, 256, 2 * SUBLANES)
    core = lax.axis_index("c").astype(jnp.int32).reshape(1)

    def body(core_ref, s_ref, r_ref, o_ref):
        o_ref[...] = (s_ref[...].astype(F32) + r_ref[...].astype(F32)).astype(o_ref.dtype)

    grid_spec = pltpu.PrefetchScalarGridSpec(
        num_scalar_prefetch=1, grid=(nchip, r // tr),
        in_specs=[pl.BlockSpec((None, None, tr, cdim), lambda k, i, core_ref: (k, core_ref[0], i, 0)),
                  pl.BlockSpec((None, tr, cdim), lambda k, i, core_ref: (k, i, 0))],
        out_specs=pl.BlockSpec((None, tr, cdim), lambda k, i, core_ref: (k, i, 0)))
    return pl.pallas_call(body, grid_spec=grid_spec, out_shape=jax.ShapeDtypeStruct(recv.shape, recv.dtype),
                          compiler_params=_cparams("parallel", "parallel"), name=name)(core, src, recv)


def _scatter_chips(srcs, name):
    n = len(srcs)

    def body(*refs):
        src, out = refs[:n], refs[n:2 * n]
        send_sems, recv_sems, local_sems = refs[2 * n:]
        x, y, c, chips = _mesh_place()
        my_chip = 2 * x + y
        mine = [pltpu.make_async_copy(src[t].at[my_chip], out[t].at[my_chip], local_sems.at[t]) for t in range(n)]
        copies = [pltpu.make_async_remote_copy(
            src_ref=src[t].at[2 * chip[0] + chip[1]], dst_ref=out[t].at[my_chip],
            send_sem=send_sems.at[t, j], recv_sem=recv_sems.at[t, j], device_id=(*chip, c), device_id_type=_MESH_ID)
            for t in range(n) for j, chip in enumerate(chips)]
        for cp in mine + copies:
            cp.start()
        for cp in copies + mine:
            cp.wait()

    return pl.pallas_call(
        body, in_specs=[_HBM] * n, out_specs=[_HBM] * n,
        out_shape=[jax.ShapeDtypeStruct(s.shape, s.dtype) for s in srcs],
        scratch_shapes=[pltpu.SemaphoreType.DMA((n, 3)), pltpu.SemaphoreType.DMA((n, 3)), pltpu.SemaphoreType.DMA((n,))],
        name=name)(*srcs)


def _by_chip_and_core(g):
    return g.reshape(N_DEV // 2, 2, g.shape[0] // N_DEV, g.shape[1])


def _chip_sums(grads, tag, from_sibling=None):
    views = [_by_chip_and_core(g) for g in grads]
    if from_sibling is None:
        from_sibling = _scatter_pair(views, name="scatter_pair_" + tag)
    return [_pair_add(v, s, name="pair_add") for v, s in zip(views, from_sibling)]


def _pair_ride(grads):
    views = [_by_chip_and_core(g) for g in grads]

    def copies(in_refs, buf_refs, sems):
        x, y, c, _ = _mesh_place()
        return [pltpu.make_async_remote_copy(src_ref=src.at[:, 1 - c], dst_ref=dst, send_sem=sems.at[2 * t],
                                             recv_sem=sems.at[2 * t + 1], device_id=(x, y, 1 - c), device_id_type=_MESH_ID)
                for t, (src, dst) in enumerate(zip(in_refs, buf_refs))]

    def start(in_refs, buf_refs, sems):
        for cp in copies(in_refs, buf_refs, sems):
            cp.start()

    def finish(in_refs, buf_refs, sems):
        for cp in copies(in_refs, buf_refs, sems):
            cp.wait()

    created = tuple(jax.ShapeDtypeStruct((v.shape[0],) + v.shape[2:], v.dtype) for v in views)
    return _Ride(inputs=tuple(views), carried=(), created=created, n_sems=2 * len(views), start=start, finish=finish)


def _sum_chips(parts):
    return _rowmap(_sum_parts_fn, [], stacks=[parts], row_outs=[(parts.shape[2], F32)], tr=128, name="sum_chips")[0]


def _buffer_roles(kinds, bufs):
    carried = [k for k in kinds if k in bufs]
    return carried, [k for k in kinds if k not in bufs]


def _gather_ride(sends, forwards, bufs):
    carried, created = _buffer_roles(list(dict.fromkeys([s[0] for s in sends] + [f[0] for f in forwards])), bufs)
    shape_of = {s[0]: jax.ShapeDtypeStruct((N_DEV,) + s[1].shape, s[1].dtype) for s in sends}
    order = carried + created

    def copies(in_refs, buf_refs, sems):
        x, y, c, chips = _mesh_place()
        buf = dict(zip(order, buf_refs))
        out, s0 = [], 0
        for (kind, _, r0, nr), src in zip(sends, in_refs):
            mine, dst = src.at[pl.ds(r0, nr)], buf[kind].at[4 * x + 2 * y + c, pl.ds(r0, nr)]
            out.append(pltpu.make_async_copy(mine, dst, sems.at[s0 + 8]))
            for k, peer in enumerate([(x, y, 1 - c)] + [(*chip, c) for chip in chips]):
                out.append(pltpu.make_async_remote_copy(src_ref=mine, dst_ref=dst, send_sem=sems.at[s0 + k],
                                                        recv_sem=sems.at[s0 + 4 + k], device_id=peer, device_id_type=_MESH_ID))
            s0 += 9
        for kind, r0, nr in forwards:
            for j, chip in enumerate(chips):
                blk = buf[kind].at[4 * chip[0] + 2 * chip[1] + c, pl.ds(r0, nr)]
                out.append(pltpu.make_async_remote_copy(src_ref=blk, dst_ref=blk, send_sem=sems.at[s0 + j],
                                                        recv_sem=sems.at[s0 + 3 + j], device_id=(x, y, 1 - c),
                                                        device_id_type=_MESH_ID))
            s0 += 6
        return out

    def start(in_refs, buf_refs, sems):
        for cp in copies(in_refs, buf_refs, sems):
            cp.start()

    def finish(in_refs, buf_refs, sems):
        for cp in copies(in_refs, buf_refs, sems):
            cp.wait()

    ride = _Ride(inputs=tuple(s[1] for s in sends), carried=tuple(bufs[k] for k in carried),
                 created=tuple(shape_of[k] for k in created), n_sems=9 * len(sends) + 6 * len(forwards),
                 start=start, finish=finish)
    return ride, order


def _scatter_ride(pieces, bufs):
    carried, created = _buffer_roles(list(dict.fromkeys(p[0] for p in pieces)), bufs)
    shape_of = {p[0]: jax.ShapeDtypeStruct(p[1].shape, p[1].dtype) for p in pieces}
    order = carried + created

    def copies(in_refs, buf_refs, sems):
        x, y, c, chips = _mesh_place()
        buf = dict(zip(order, buf_refs))
        out, s0 = [], 0
        for (kind, _, r0, nr), src in zip(pieces, in_refs):
            dst = buf[kind].at[2 * x + y, pl.ds(r0, nr)]
            out.append(pltpu.make_async_copy(src.at[2 * x + y, pl.ds(r0, nr)], dst, sems.at[s0 + 6]))
            for j, chip in enumerate(chips):
                out.append(pltpu.make_async_remote_copy(
                    src_ref=src.at[2 * chip[0] + chip[1], pl.ds(r0, nr)], dst_ref=dst, send_sem=sems.at[s0 + j],
                    recv_sem=sems.at[s0 + 3 + j], device_id=(*chip, c), device_id_type=_MESH_ID))
            s0 += 7
        return out

    def start(in_refs, buf_refs, sems):
        for cp in copies(in_refs, buf_refs, sems):
            cp.start()

    def finish(in_refs, buf_refs, sems):
        for cp in copies(in_refs, buf_refs, sems):
            cp.wait()

    ride = _Ride(inputs=tuple(p[1] for p in pieces), carried=tuple(bufs[k] for k in carried),
                 created=tuple(shape_of[k] for k in created), n_sems=7 * len(pieces), start=start, finish=finish)
    return ride, order


GATHER_PLAN = (
    ("mm_in_qkv", (("w_in", 0, 3),)),
    ("mm_in_gl", (("w_in", 1, 3),)),
    ("attn_fwd0", (("w_pa", 0, 1), ("w_pb", 0, 1))),
    ("attn_fwd1", (("w_pc", 0, 1), ("w_glu", 0, 1))),
    ("attn_fwd2", (("w_o", 0, 1),)),
    ("s5_fwd", (("w_ffn_out", 0, 2),)),
    ("mm_o", (("w_ffn_out", 1, 2),)),
    ("mm_ffn_in", (("w_in", 2, 3), ("w_ffn_in", 0, 2))),
    ("mm_ffn_out", (("w_ffn_in", 1, 2),)),
    ("norm2", ()),
)
EARLY_KINDS = ("w_ffn_out", "w_ffn_in")
LATE_KINDS = tuple(n for n in SHARDED if n not in EARLY_KINDS)
SCATTER_EARLY_PLAN = (
    ("mm_in_dw_qkv", (("w_ffn_out", 0, 2),)),
    ("mm_in_dx_qkv", (("w_ffn_out", 1, 2),)),
    ("mm_in_dw_gl", (("w_ffn_in", 0, 4), ("w_ffn_in", 3, 4))),
    ("mm_in_dx_gl", (("w_ffn_in", 1, 4), ("w_ffn_in", 2, 4))),
)
SCATTER_PLAN = (
    ("mm_ffn_out_dw", (("w_in", 0, 3),)),
    ("mm_ffn_out_dx", (("w_in", 1, 3),)),
    ("mm_ffn_in_dw", (("w_in", 2, 3), ("w_o", 0, 1), ("w_pa", 0, 1), ("w_pb", 0, 1), ("w_pc", 0, 1), ("w_glu", 0, 1))),
)


def _row_part(rows, part, parts):
    assert rows % (parts * 2 * SUBLANES) == 0
    return part * (rows // parts), rows // parts


class _Carried:
    def __init__(self, plan, blocks, make_ride, forwards_too):
        self.plan, self.blocks, self.make_ride, self.forwards_too = dict(plan), blocks, make_ride, forwards_too
        self.bufs, self.to_forward, self.order = {}, [], []

    def ride(self, host):
        if self.blocks is None or host not in self.plan:
            self.order = []
            return None
        sends = [(k, self.blocks[k], *_row_part(self.blocks[k].shape[-2], part, parts)) for k, part, parts in self.plan[host]]
        if self.forwards_too:
            ride, self.order = self.make_ride(sends, self.to_forward, self.bufs)
            self.to_forward = [(k, r0, nr) for k, _, r0, nr in sends]
        else:
            ride, self.order = self.make_ride(sends, self.bufs)
        return ride

    def took(self, ride_outs):
        for k, buf in zip(self.order, ride_outs[0] if ride_outs else []):
            self.bufs[k] = buf


class _OwnScatter:
    def __init__(self, enabled):
        self.enabled = enabled

    def pair_ride(self, grads):
        return _pair_ride(list(grads.values())) if self.enabled else None

    def steps(self, grads, ride_outs):
        if not self.enabled:
            return _Carried(SCATTER_EARLY_PLAN, None, _scatter_ride, forwards_too=False)
        sums = _chip_sums(list(grads.values()), "early", from_sibling=ride_outs[0])
        return _Carried(SCATTER_EARLY_PLAN, dict(zip(grads, sums)), _scatter_ride, forwards_too=False)


def _hosted(comm, fn, *args, name, **kwargs):
    res, ride_outs = fn(*args, name=name, rides=[comm.ride(name)], **kwargs)
    comm.took(ride_outs)
    return res


def _small_sizes(shapes):
    return [int(np.prod(shapes[n])) for n in SMALL]


def _pack_small(vals):
    flat = jnp.concatenate([vals[n].reshape(-1).astype(F32) for n in SMALL])
    rows = -(-flat.shape[0] // (LANES * N_DEV * SUBLANES)) * (N_DEV * SUBLANES)
    return jnp.pad(flat, (0, rows * LANES - flat.shape[0])).reshape(rows, LANES)


def _unpack_small(packed, shapes):
    flat = packed.reshape(-1)
    out, off = {}, 0
    for n, size in zip(SMALL, _small_sizes(shapes)):
        out[n] = flat[off:off + size].reshape(shapes[n])
        off += size
    return out


def _row(v):
    return v.reshape(1, -1)


def _layer_params(l, full, small):
    o1, o2, o3 = 3 * QKV_WIDTH, 3 * QKV_WIDTH + 2 * WIDTH_B, 3 * QKV_WIDTH + 2 * WIDTH_B + WIDTH_C
    b_in = small["b_in"][l]
    p = {
        "in_pieces": (("qkv", 0, o1), ("zb", o1, o2 - o1), ("uc", o2, o3 - o2), ("gl", o3, b_in.shape[0] - o3)),
        "b_qkv": _row(b_in[:o1]), "b_zb": _row(b_in[o1:o2]), "b_uc": _row(b_in[o2:o3]), "b_gl": _row(b_in[o3:]),
        "sgu_ln_g": _row(small["sgu_ln_g"][l]), "sgu_ln_b": _row(small["sgu_ln_b"][l]),
        "w_s": small["w_s"][l], "b_s_t": small["b_s"][l].T,
        "lam_re": small["lam_re"][l][:, None, :], "lam_im": small["lam_im"][l][:, None, :],
        "log_dt": small["log_dt"][l][:, None, None],
        "b_re_t": small["b_re"][l].transpose(0, 2, 1), "b_im_t": small["b_im"][l].transpose(0, 2, 1),
        "c_re": small["c_re"][l], "c_im": small["c_im"][l],
        "d_skip": _row(small["d_skip"][l]), "b_glu": _row(small["b_glu"][l]),
        "ln1_g": _row(small["ln1_g"][l]), "ln1_b": _row(small["ln1_b"][l]),
        "ln2_g": _row(small["ln2_g"][l]), "ln2_b": _row(small["ln2_b"][l]),
    }
    for n in SHARDED:
        p[n] = full[n]
    return p


def _twice(fn):
    def both(*args):
        y = fn(*args)
        return y, y
    return both


def _layer_fwd(x, xb, p, biases, comm):
    t, d = x.shape
    r = {"x": x, "xb": xb}
    for piece, off, n in p["in_pieces"]:
        r[piece] = _hosted(comm, _mm, xb, p["w_in"], tb=True, b_off=off, n=n, bias=p["b_" + piece],
                           out_dtype=BF16 if piece in ("qkv", "gl") else F32, name="mm_in_" + piece)
    ol = []
    for g, dil in enumerate(ATT_DILATIONS):
        ol += list(_attn_fwd(r["qkv"], biases[g], g, dil, comm))
    r["ol"] = ol
    r["ya"], r["ya_b"] = _rowmap(_twice(_combine), ol, row_outs=[(WIDTH_A, F32), (WIDTH_A, BF16)], tr=512,
                                 name="attn_combine")
    r["yb"] = _gmlp_fwd(r["zb"], p["sgu_ln_g"], p["sgu_ln_b"], p["w_s"], p["b_s_t"])
    ab_re, ab_im, bb_re_t, bb_im_t = _s5_disc_fwd(p["lam_re"], p["lam_im"], p["log_dt"], p["b_re_t"], p["b_im_t"])
    r["a_re"], r["a_im"] = ab_re.reshape(_SCAN_ROWS, LANES), ab_im.reshape(_SCAN_ROWS, LANES)
    r["s5_mats"] = _block_diag([bb_re_t, bb_im_t, p["c_re"], p["c_im"]])
    r["xr"], r["xi"], r["ys"], r["ycp"] = _s5_fwd(r["uc"], r["s5_mats"], r["a_re"], r["a_im"], p["d_skip"], comm)
    r["glin"] = _mm(r["ycp"], p["w_glu"], bias=p["b_glu"], name="mm_glu")
    r["yc"] = _rowmap(_glu, [r["ycp"], r["glin"]], row_outs=[(WIDTH_C, BF16)], tr=512, name="glu")[0]
    r["pa"] = _mm(r["ya_b"], p["w_pa"], tb=True, out_dtype=BF16, name="mm_pa")
    r["pb"] = _mm(r["yb"], p["w_pb"], tb=True, out_dtype=BF16, name="mm_pb")
    r["pc"] = _mm(r["yc"], p["w_pc"], tb=True, out_dtype=BF16, name="mm_pc")
    r["merged"] = _rowmap(_merge, [r["gl"], r["pa"], r["pb"], r["pc"]], row_outs=[(d, BF16)], name="merge")[0]
    r["mo"] = _hosted(comm, _mm, r["merged"], p["w_o"], name="mm_o")
    r["xm"], r["xm_b"] = _rowmap(_twice(_post_norm), [x, r["mo"]], consts=[p["ln1_g"], p["ln1_b"]],
                                 row_outs=[(d, F32), (d, BF16)], name="norm1")
    r["gate"], r["up"], r["act"] = _hosted(comm, _mm_swiglu, r["xm_b"], p["w_ffn_in"], name="mm_ffn_in")
    r["f"] = _hosted(comm, _mm, r["act"], p["w_ffn_out"], name="mm_ffn_out")
    out, out_b = _hosted(comm, _rowmap, _twice(_post_norm), [r["xm"], r["f"]], consts=[p["ln2_g"], p["ln2_b"]],
                         row_outs=[(d, F32), (d, BF16)], name="norm2")
    return out, out_b, r


def _layer_bwd(dout, r, p, biases, consts, comm, own):
    t, d = dout.shape
    gw, gs = {}, {}
    ffw = 2 * r["gate"].shape[1]
    dxm, df, gs["ln2_g"], gs["ln2_b"] = _rowmap(
        _post_norm_bwd, [r["xm"], r["f"], dout], consts=[p["ln2_g"], p["ln2_b"]],
        row_outs=[(d, F32), (d, BF16)], red_outs=[(1, d)] * 2, name="norm2_bwd")
    gw["w_ffn_out"] = _hosted(comm, _mm, r["act"], df, ta=True, out_dtype=BF16, name="mm_ffn_out_dw")
    dact = _hosted(comm, _mm, df, p["w_ffn_out"], tb=True, out_dtype=BF16, name="mm_ffn_out_dx")
    dgu = _rowmap(_swiglu_bwd, [r["gate"], r["up"], dact], row_outs=[(ffw, BF16)], tr=128, name="swiglu_bwd")[0]
    gw["w_ffn_in"] = _hosted(comm, _mm, dgu, r["xm_b"], ta=True, out_dtype=BF16, name="mm_ffn_in_dw")
    early_grads = {n: gw[n] for n in EARLY_KINDS}
    dxm, from_sibling = _mm(dgu, p["w_ffn_in"], add=dxm, name="mm_ffn_in_dx", rides=[own.pair_ride(early_grads)])
    early = own.steps(early_grads, from_sibling)
    dx, dmo, gs["ln1_g"], gs["ln1_b"] = _rowmap(
        _post_norm_bwd, [r["x"], r["mo"], dxm], consts=[p["ln1_g"], p["ln1_b"]],
        row_outs=[(d, F32), (d, BF16)], red_outs=[(1, d)] * 2, name="norm1_bwd")
    gw["w_o"] = _hosted(comm, _mm, r["merged"], dmo, ta=True, out_dtype=BF16, name="mm_o_dw")
    dmerged = _hosted(comm, _mm, dmo, p["w_o"], tb=True, name="mm_o_dx")
    dgl, dpa, dpb, dpc, db_gl = _rowmap(
        _merge_bwd, [r["gl"], r["pa"], r["pb"], r["pc"], dmerged],
        row_outs=[(3 * d, BF16), (d, BF16), (d, BF16), (d, BF16)], red_outs=[(1, 3 * d)], tr=128, name="merge_bwd")
    gw["w_pa"] = _mm(dpa, r["ya_b"], ta=True, out_dtype=BF16, name="mm_pa_dw")
    gw["w_pb"] = _mm(dpb, r["yb"], ta=True, out_dtype=BF16, name="mm_pb_dw")
    gw["w_pc"] = _mm(dpc, r["yc"], ta=True, out_dtype=BF16, name="mm_pc_dw")
    dya = _mm(dpa, p["w_pa"], name="mm_pa_dx")
    dyb = _mm(dpb, p["w_pb"], name="mm_pb_dx")
    dyc = _mm(dpc, p["w_pc"], name="mm_pc_dx")
    dycp, dglin, gs["b_glu"] = _rowmap(_glu_bwd, [r["ycp"], r["glin"], dyc], row_outs=[(WIDTH_C, F32), (WIDTH_C, BF16)],
                                       red_outs=[(1, WIDTH_C)], tr=512, name="glu_bwd")
    gw["w_glu"] = _mm(r["ycp"], dglin, ta=True, out_dtype=BF16, name="mm_glu_dw")
    dycp = _mm(dglin, p["w_glu"], tb=True, add=dycp, name="mm_glu_dx")
    dys, duc, gs["d_skip"] = _rowmap(_s5_out_bwd, [r["ys"], r["uc"], dycp], consts=[p["d_skip"]],
                                     row_outs=[(WIDTH_C, F32)] * 2, red_outs=[(1, WIDTH_C)], tr=512, name="s5_out_act_bwd")
    duc, d_bmat_re, d_bmat_im, d_cmat_re, d_cmat_im, da_re, da_im = _s5_bwd(
        dys, duc, r["uc"], r["xr"], r["xi"], r["s5_mats"], r["a_re"], r["a_im"])
    d_bb_re_t, d_bb_im_t, gs["c_re"], gs["c_im"] = _block_diag_extract(
        [d_bmat_re, d_bmat_im, d_cmat_re, d_cmat_im], SSM_GROUP, SSM_STATE)
    cts = (da_re.reshape(N_GROUPS_C, 1, SSM_STATE), da_im.reshape(N_GROUPS_C, 1, SSM_STATE), d_bb_re_t, d_bb_im_t)
    d_lr, d_li, d_ldt, d_br_t, d_bi_t = _s5_disc_bwd(p["lam_re"], p["lam_im"], p["log_dt"], p["b_re_t"], p["b_im_t"], cts)
    gs["lam_re"], gs["lam_im"], gs["log_dt"] = d_lr[:, 0, :], d_li[:, 0, :], d_ldt[:, 0, 0]
    gs["b_re"], gs["b_im"] = d_br_t.transpose(0, 2, 1), d_bi_t.transpose(0, 2, 1)
    dzb, db_zb, gs["sgu_ln_g"], gs["sgu_ln_b"], gs["w_s"], dbs_t = _gmlp_bwd(
        r["zb"], dyb, p["sgu_ln_g"], p["sgu_ln_b"], p["w_s"], p["b_s_t"], consts["group_sel"])
    gs["b_s"] = dbs_t[:, :N_GROUPS_B].T
    do_corr = _rowmap(_combine_bwd, r["ol"] + [r["ya"], dya], consts=[consts["head_ones"]],
                      row_outs=[(WIDTH_A, F32)] * 6, tr=512, name="attn_combine_bwd")
    dq, dk, dv, dbias = [], [], [], []
    for g, dil in enumerate(ATT_DILATIONS):
        do_g, corr_g, lse_g = do_corr[g], do_corr[3 + g], r["ol"][2 * g + 1]
        dq_g, dk_g, dv_g, db_g = _attn_bwd(r["qkv"], biases[g], do_g, lse_g, corr_g, g, dil)
        dq.append(dq_g)
        dk.append(dk_g)
        dv.append(dv_g)
        dbias.append(db_g)
    def cast_colsum(*pieces):
        a = pieces[0] if len(pieces) == 1 else jnp.concatenate(pieces, axis=1)
        return a, jnp.sum(a, axis=0, keepdims=True)

    dqkv, db_qkv = _rowmap(cast_colsum, dq + dk + dv, row_outs=[(3 * QKV_WIDTH, BF16)],
                           red_outs=[(1, 3 * QKV_WIDTH)], tr=256, name="cast_colsum_qkv")
    duc, db_uc = _rowmap(cast_colsum, [duc], row_outs=[(WIDTH_C, BF16)], red_outs=[(1, WIDTH_C)], tr=512,
                         name="cast_colsum_uc")
    dpieces = {"qkv": dqkv, "zb": dzb, "uc": duc, "gl": dgl}
    rows_in = p["w_in"].shape[0]
    dw_in = None
    for piece, off, n in p["in_pieces"]:
        dw_in = _hosted(early, _mm, dpieces[piece], r["xb"], ta=True, out_dtype=BF16, into=(rows_in, off, dw_in),
                        name="mm_in_dw_" + piece)
        dx = _hosted(early, _mm, dpieces[piece], p["w_in"], b_off=off, add=dx, name="mm_in_dx_" + piece)
    gw["w_in"] = dw_in
    gs["b_in"] = jnp.concatenate([db_qkv, db_zb, db_uc, db_gl], axis=1)[0]
    for n in ("sgu_ln_g", "sgu_ln_b", "d_skip", "b_glu", "ln1_g", "ln1_b", "ln2_g", "ln2_b"):
        gs[n] = gs[n][0]
    return dx, gw, gs, dbias, early.bufs


def _cast_bf16(w):
    w2 = w.reshape(-1, w.shape[-1])
    out = _rowmap(lambda a: a, [w2], row_outs=[(w2.shape[1], BF16)], tr=512, name="cast_bf16")[0]
    return out.reshape(w.shape)


def _static_consts():
    head_ones = np.kron(np.eye(HEADS_PER_GROUP, dtype=np.float32), np.ones((HEAD_DIM, HEAD_DIM), np.float32))
    group_sel = np.zeros((WIDTH_B, LANES), np.float32)
    group_sel[np.arange(WIDTH_B), np.arange(WIDTH_B) // CHUNK] = 1.0
    return {"head_ones": jnp.asarray(head_ones), "group_sel": jnp.asarray(group_sel)}


def _step(x, tgt, w, m, v):
    shapes = {n: w[n].shape for n in WEIGHTS}
    consts = _static_consts()
    mine_bf = {n: _cast_bf16(w[n].transpose(0, 2, 1) if n in TRANSPOSED else w[n]) for n in SHARDED}
    small = {n: w[n] for n in SMALL}
    buckets = [jnp.asarray(_bucket_table(dil)) for dil in ATT_DILATIONS]
    biases = [_bias_fwd(w["rel_bias"], buckets[g], g) for g in range(len(ATT_DILATIONS))]
    params, saved = [], []
    h, hb = _rowmap(_twice(lambda a: a), [x], row_outs=[(x.shape[1], F32), (x.shape[1], BF16)], name="cast_x")
    gathered = dict(zip(SHARDED, _gather_layer([mine_bf[n] for n in SHARDED], 0, name="gather_layer0")))
    for l in range(DEPTH):
        p = _layer_params(l, {n: g.reshape(-1, g.shape[2]) for n, g in gathered.items()}, small)
        ahead = _Carried(GATHER_PLAN, {n: mine_bf[n][l + 1] for n in SHARDED} if l + 1 < DEPTH else None,
                         _gather_ride, forwards_too=True)
        h, hb, r = _layer_fwd(h, hb, p, biases, ahead)
        gathered = ahead.bufs
        params.append(p)
        saved.append(r)
    dy, loss_part = _rowmap(_loss_fn, [h, tgt], row_outs=[(h.shape[1], F32)], red_outs=[(1, LANES)], name="loss")
    loss = lax.psum(loss_part[0, 0], MESH_AXES)
    g_mine, gs_layers = {n: [None] * DEPTH for n in SHARDED}, [None] * DEPTH
    dbias_sum = None
    def scatter_steps(plan, grads, tag):
        sums = _chip_sums(list(grads.values()), tag)
        return _Carried(plan, dict(zip(grads, sums)), _scatter_ride, forwards_too=False)

    behind = _Carried(SCATTER_PLAN, None, _scatter_ride, forwards_too=False)
    for l in reversed(range(DEPTH)):
        dy, gw, gs_layers[l], dbias, early_parts = _layer_bwd(
            dy, saved[l], params[l], biases, consts, behind, _OwnScatter(enabled=True))
        saved[l] = None
        for n in EARLY_KINDS:
            g_mine[n][l] = _sum_chips(early_parts[n])
        if behind.blocks is not None:
            for n in LATE_KINDS:
                g_mine[n][l + 1] = _sum_chips(behind.bufs[n])
        behind = scatter_steps(SCATTER_PLAN, {n: gw[n] for n in LATE_KINDS}, "late")
        if l == 0:
            last = _scatter_chips([behind.blocks[n] for n in LATE_KINDS], name="scatter_chips_layer0")
            for n, parts in zip(LATE_KINDS, last):
                g_mine[n][0] = _sum_chips(parts)
        if dbias_sum is None:
            dbias_sum = dbias
        else:
            dbias_sum = [_rowmap(lambda a, b: a + b, [a.reshape(-1, 2 * ATT_BLOCK), b.reshape(-1, 2 * ATT_BLOCK)],
                                 row_outs=[(2 * ATT_BLOCK, F32)], name="dbias_add")[0].reshape(a.shape)
                         for a, b in zip(dbias_sum, dbias)]
    drel = [_bias_bwd(dbias_sum[g], buckets[g], g) for g in range(len(ATT_DILATIONS))]
    drel = _rowmap(lambda a, b, c: a + b + c, drel, row_outs=[(LANES, F32)], name="drel_add")[0]
    grad_small_local = {n: jnp.stack([gs_layers[l][n] for l in range(DEPTH)]) for n in SMALL if n != "rel_bias"}
    grad_small_local["rel_bias"] = drel[:, :shapes["rel_bias"][1]]
    out_g, out_d, out_m, out_v = {}, {}, {}, {}
    for n in SHARDED:
        g = jnp.stack(g_mine[n])
        out_g[n] = g.transpose(0, 2, 1) if n in TRANSPOSED else g
        cols = shapes[n][-1]
        res = _rowmap(_adamw, [a.reshape(-1, cols) for a in (w[n], out_g[n], m[n], v[n])],
                      row_outs=[(cols, F32)] * 3, tr=128, name="adamw_" + n)
        out_d[n], out_m[n], out_v[n] = [a.reshape(shapes[n]) for a in res]
    packed = _pack_small(grad_small_local)
    rows = packed.shape[0] // N_DEV
    parts = _exchange(packed.reshape(N_DEV, rows, LANES), gather=False, name="scatter_small")
    mine = _rowmap(_sum_parts_fn, [], stacks=[parts], row_outs=[(LANES, F32)], name="sum_small")[0]
    g_small = _exchange(mine, gather=True, name="gather_small").reshape(-1, LANES)
    res = _rowmap(lambda w_, g_, m_, v_: _adamw(w_, g_, m_, v_),
                  [_pack_small(w), g_small, _pack_small(m), _pack_small(v)],
                  row_outs=[(LANES, F32)] * 3, name="adamw_small")
    small_shapes = {n: shapes[n] for n in SMALL}
    out_g.update(_unpack_small(g_small, small_shapes))
    for dst, packed_res in zip((out_d, out_m, out_v), res):
        dst.update(_unpack_small(packed_res, small_shapes))
    return loss, dy, out_g, out_d, out_m, out_v


def kernel(x, w_in, b_in, rel_bias, sgu_ln_g, sgu_ln_b, w_s, b_s, lam_re, lam_im, log_dt, b_re, b_im, c_re, c_im, d_skip, w_glu, b_glu, w_pa, w_pb, w_pc, w_o, ln1_g, ln1_b, w_ffn_in, w_ffn_out, ln2_g, ln2_b, loss_target, m_w_in, m_b_in, m_rel_bias, m_sgu_ln_g, m_sgu_ln_b, m_w_s, m_b_s, m_lam_re, m_lam_im, m_log_dt, m_b_re, m_b_im, m_c_re, m_c_im, m_d_skip, m_w_glu, m_b_glu, m_w_pa, m_w_pb, m_w_pc, m_w_o, m_ln1_g, m_ln1_b, m_w_ffn_in, m_w_ffn_out, m_ln2_g, m_ln2_b, v_w_in, v_b_in, v_rel_bias, v_sgu_ln_g, v_sgu_ln_b, v_w_s, v_b_s, v_lam_re, v_lam_im, v_log_dt, v_b_re, v_b_im, v_c_re, v_c_im, v_d_skip, v_w_glu, v_b_glu, v_w_pa, v_w_pb, v_w_pc, v_w_o, v_ln1_g, v_ln1_b, v_w_ffn_in, v_w_ffn_out, v_ln2_g, v_ln2_b):
    args = dict(locals())
    w = {n: args[n] for n in WEIGHTS}
    m = {n: args["m_" + n] for n in WEIGHTS}
    v = {n: args["v_" + n] for n in WEIGHTS}
    loss, dx, g, d, nm, nv = _step(x[0], loss_target[0], w, m, v)
    return (loss, dx[None], *[g[n] for n in WEIGHTS], *[d[n] for n in WEIGHTS],
            *[nm[n] for n in WEIGHTS], *[nv[n] for n in WEIGHTS])
```

```python
import functools
import math
from typing import Callable, NamedTuple

import numpy as np
import jax
import jax.numpy as jnp
from jax import lax
from jax.experimental import pallas as pl
from jax.experimental.pallas import tpu as pltpu

F32 = jnp.float32
BF16 = jnp.bfloat16

MESH_AXES = ("x", "y", "c")
N_DEV = 8
DEPTH = 4

ATT_DILATIONS = (1, 4, 16)
ATT_STEPS = 128
HEADS_PER_GROUP = 8
HEAD_DIM = 64
QKV_WIDTH = 1536
WIDTH_A = HEADS_PER_GROUP * HEAD_DIM
ATT_BLOCK = 128
N_REL_BUCKETS = 32
REL_MAX_DIST = 2048
NEG_INF = -1e30
CHUNK = 128
WIDTH_B = 768
N_GROUPS_B = 6
WIDTH_C = 768
SSM_GROUP = 16
N_GROUPS_C = 48
SSM_STATE = 64
SSM_PACK = 8
N_SSM_BLOCKS = N_GROUPS_C // SSM_PACK
SSM_COLS = N_GROUPS_C * SSM_STATE
ALPHA = (2 * DEPTH) ** 0.25

ADAM_LR = 0.001
ADAM_B1 = 0.9
ADAM_B2 = 0.999
ADAM_EPS = 1e-08
ADAM_WD = 0.01
ADAM_STEP = 10

LANES = 128
SUBLANES = 8
VMEM_LIMIT = 48 * 1024 * 1024

SHARDED = ("w_in", "w_glu", "w_pa", "w_pb", "w_pc", "w_o", "w_ffn_in", "w_ffn_out")
TRANSPOSED = ("w_in", "w_pa", "w_pb", "w_pc", "w_ffn_in")
SMALL = ("b_in", "rel_bias", "sgu_ln_g", "sgu_ln_b", "w_s", "b_s", "lam_re", "lam_im", "log_dt",
         "b_re", "b_im", "c_re", "c_im", "d_skip", "b_glu", "ln1_g", "ln1_b", "ln2_g", "ln2_b")
WEIGHTS = ("w_in", "b_in", "rel_bias", "sgu_ln_g", "sgu_ln_b", "w_s", "b_s", "lam_re", "lam_im",
           "log_dt", "b_re", "b_im", "c_re", "c_im", "d_skip", "w_glu", "b_glu", "w_pa", "w_pb",
           "w_pc", "w_o", "ln1_g", "ln1_b", "w_ffn_in", "w_ffn_out", "ln2_g", "ln2_b")


def _pick(dim, target, mult):
    best = None
    for t in range(mult, min(dim, target) + 1, mult):
        if dim % t == 0:
            best = t
    return dim if best is None else best


def _cparams(*sem):
    return pltpu.CompilerParams(dimension_semantics=sem, vmem_limit_bytes=VMEM_LIMIT)


def _zero_map(ndim):
    return lambda *_: (0,) * ndim


_HBM = pl.BlockSpec(memory_space=pl.ANY)
_MESH_ID = pl.DeviceIdType.MESH


class _Ride(NamedTuple):
    inputs: tuple
    carried: tuple
    created: tuple
    n_sems: int
    start: Callable
    finish: Callable


def _pallas(body, *, grid, in_specs, out_specs, out_shape, args, scratch=(), semantics, rides=(), aliases=None, name):
    rides = [r for r in rides if r is not None]
    n_in, n_out, n_scr = len(args), len(out_shape), len(scratch)
    r_args, r_shapes, aliases, spans = [], [], dict(aliases or {}), []
    for r in rides:
        i0, o0 = len(r_args), len(r_shapes)
        r_args += [*r.inputs, *r.carried]
        for k, a in enumerate(r.carried):
            aliases[n_in + i0 + len(r.inputs) + k] = n_out + o0 + k
        r_shapes += [jax.ShapeDtypeStruct(a.shape, a.dtype) for a in r.carried] + list(r.created)
        spans.append((i0, len(r.inputs), o0, len(r.carried) + len(r.created)))

    def full_body(*refs):
        host_in, ride_in = refs[:n_in], refs[n_in:n_in + len(r_args)]
        p = n_in + len(r_args)
        host_out, ride_out = refs[p:p + n_out], refs[p + n_out:p + n_out + len(r_shapes)]
        p += n_out + len(r_shapes)
        host_scr, ride_sems = refs[p:p + n_scr], refs[p + n_scr:]
        ids = [pl.program_id(k) for k in range(len(grid))]
        first = functools.reduce(jnp.logical_and, [i == 0 for i in ids])
        last = functools.reduce(jnp.logical_and, [i == g - 1 for i, g in zip(ids, grid)])

        def each(method):
            for r, (i0, ni, o0, no), sems in zip(rides, spans, ride_sems):
                getattr(r, method)(ride_in[i0:i0 + ni], ride_out[o0:o0 + no], sems)

        if rides:
            pl.when(first)(lambda: each("start"))
        body(*host_in, *host_out, *host_scr)
        if rides:
            pl.when(last)(lambda: each("finish"))

    if rides:
        semantics = ("arbitrary",) * len(grid)
    outs = pl.pallas_call(
        full_body, grid=grid, in_specs=list(in_specs) + [_HBM] * len(r_args),
        out_specs=list(out_specs) + [_HBM] * len(r_shapes), out_shape=list(out_shape) + r_shapes,
        scratch_shapes=list(scratch) + [pltpu.SemaphoreType.DMA((r.n_sems,)) for r in rides],
        input_output_aliases=aliases, compiler_params=_cparams(*semantics), name=name)(*args, *r_args)
    return outs[:n_out], [outs[n_out + o0:n_out + o0 + no] for _, _, o0, no in spans]


def _rowmap(fn, rows, consts=(), stacks=(), row_outs=(), red_outs=(), tr=256, name=None, rides=None):
    t = rows[0].shape[0] if rows else stacks[0].shape[1]
    dtypes = [a.dtype for a in (*rows, *stacks)] + [dt for _, dt in row_outs]
    packed = any(jnp.dtype(dt).itemsize < 4 for dt in dtypes)
    tr = _pick(t, tr, 2 * SUBLANES if packed else SUBLANES)
    n_r, n_c, n_s, n_o = len(rows), len(consts), len(stacks), len(row_outs)

    def body(*refs):
        ins = [r[...] for r in refs[:n_r + n_c + n_s]]
        outs = refs[n_r + n_c + n_s:n_r + n_c + n_s + n_o]
        reds = refs[n_r + n_c + n_s + n_o:]
        res = fn(*ins)
        if not isinstance(res, (tuple, list)):
            res = (res,)
        for o, v in zip(outs, res[:n_o]):
            o[...] = v.astype(o.dtype)
        if reds:
            @pl.when(pl.program_id(0) == 0)
            def _():
                for r in reds:
                    r[...] = jnp.zeros_like(r)
            for r, v in zip(reds, res[n_o:]):
                r[...] += v

    in_specs = [pl.BlockSpec((tr, r.shape[1]), lambda i: (i, 0)) for r in rows]
    in_specs += [pl.BlockSpec(c.shape, _zero_map(c.ndim)) for c in consts]
    in_specs += [pl.BlockSpec((s.shape[0], tr, s.shape[2]), lambda i: (0, i, 0)) for s in stacks]
    out_specs = [pl.BlockSpec((tr, w), lambda i: (i, 0)) for w, _ in row_outs]
    out_specs += [pl.BlockSpec(s, _zero_map(len(s))) for s in red_outs]
    out_shape = [jax.ShapeDtypeStruct((t, w), dt) for w, dt in row_outs]
    out_shape += [jax.ShapeDtypeStruct(s, F32) for s in red_outs]
    outs, ride_outs = _pallas(body, grid=(t // tr,), in_specs=in_specs, out_specs=out_specs, out_shape=out_shape,
                              args=[*rows, *consts, *stacks], semantics=("arbitrary",), rides=rides or (), name=name)
    return outs if rides is None else (outs, ride_outs)


MM_VMEM_BUDGET = 36 * 1024 * 1024


def _divisors(dim, mult, must_divide=0):
    out = [t for t in range(dim, 0, -mult) if t % mult == 0 and dim % t == 0 and must_divide % t == 0]
    return out or [dim]


def _mm_tiles(m, n, k, a_bytes, b_bytes, out_bytes, extra_bytes, ta, b_off_n, b_off_k, out_off, tm, tn):
    tms = _divisors(m, LANES if ta else SUBLANES, out_off)
    tm = next((t for t in tms if t <= tm), tms[-1])
    tns = [t for t in _divisors(n, LANES, b_off_n) if t <= tn] or [_divisors(n, LANES, b_off_n)[-1]]
    for tn_ in tns:
        for tk in _divisors(k, LANES, b_off_k):
            acc = 0 if tk == k else tm * tn_ * 4
            need = 2 * (tm * tk * a_bytes + tk * tn_ * b_bytes + tm * tn_ * (out_bytes + extra_bytes)) + acc
            if need <= MM_VMEM_BUDGET:
                return tm, tn_, tk
    return tm, tns[-1], _divisors(k, LANES, b_off_k)[-1]


def _mm(a, b, *, ta=False, tb=False, bias=None, add=None, out_dtype=F32, b_off=0, n=None, tm=1024, tn=1024, name=None,
        rides=None, into=None):
    k, m = a.shape if ta else a.shape[::-1]
    if tb:
        n = b.shape[0] if n is None else n
        assert b.shape[1] == k and b_off + n <= b.shape[0]
    else:
        n = b.shape[1]
        assert b_off + k <= b.shape[0]
    out_rows, out_off, out_buf = (m, 0, None) if into is None else into
    extra = 4 if add is not None else 0
    tm, tn, tk = _mm_tiles(m, n, k, a.dtype.itemsize, b.dtype.itemsize, jnp.dtype(out_dtype).itemsize, extra, ta,
                           b_off if tb else 0, 0 if tb else b_off, out_off, tm, tn)
    nk = k // tk
    off_n, off_k = (b_off // tn, 0) if tb else (0, b_off // tk)
    off_m = out_off // tm
    dims = (((0 if ta else 1,), (1 if tb else 0,)), ((), ()))

    def body(*refs):
        a_ref, b_ref = refs[0], refs[1]
        rest = list(refs[2:])
        bias_ref = rest.pop(0) if bias is not None else None
        add_ref = rest.pop(0) if add is not None else None
        o_ref = rest.pop(0)
        part = lax.dot_general(a_ref[...].astype(BF16), b_ref[...].astype(BF16), dims, preferred_element_type=F32)

        def finish(r):
            if bias_ref is not None:
                r = r + bias_ref[...]
            if add_ref is not None:
                r = r + add_ref[...]
            o_ref[...] = r.astype(o_ref.dtype)

        if nk == 1:
            finish(part)
        else:
            acc_ref = rest.pop(0)
            kk = pl.program_id(2)

            @pl.when(kk == 0)
            def _():
                acc_ref[...] = part

            @pl.when(kk > 0)
            def _():
                acc_ref[...] += part

            @pl.when(kk == nk - 1)
            def _():
                finish(acc_ref[...])

    a_spec = pl.BlockSpec((tk, tm), lambda i, j, q: (q, i)) if ta else pl.BlockSpec((tm, tk), lambda i, j, q: (i, q))
    if tb:
        b_spec = pl.BlockSpec((tn, tk), lambda i, j, q: (j + off_n, q))
    else:
        b_spec = pl.BlockSpec((tk, tn), lambda i, j, q: (q + off_k, j))
    in_specs, args = [a_spec, b_spec], [a, b]
    if bias is not None:
        in_specs.append(pl.BlockSpec((1, tn), lambda i, j, q: (0, j)))
        args.append(bias)
    if add is not None:
        in_specs.append(pl.BlockSpec((tm, tn), lambda i, j, q: (i, j)))
        args.append(add)
    aliases = {}
    if out_buf is not None:
        assert out_buf.shape == (out_rows, n) and out_buf.dtype == jnp.dtype(out_dtype)
        in_specs.append(_HBM)
        args.append(out_buf)
        aliases = {len(args) - 1: 0}

    def body_in_place(*refs):
        body(*refs[:len(args) - 1], *refs[len(args):])

    outs, ride_outs = _pallas(
        body if out_buf is None else body_in_place, grid=(m // tm, n // tn, nk), in_specs=in_specs,
        out_specs=[pl.BlockSpec((tm, tn), lambda i, j, q: (i + off_m, j))],
        out_shape=[jax.ShapeDtypeStruct((out_rows, n), out_dtype)], args=args,
        scratch=[] if nk == 1 else [pltpu.VMEM((tm, tn), F32)],
        semantics=("parallel", "parallel", "arbitrary"), rides=rides or (), aliases=aliases, name=name)
    return outs[0] if rides is None else (outs[0], ride_outs)


def _mm_swiglu(a, w_t, *, name, rides=None):
    m, k = a.shape
    f = w_t.shape[0] // 2
    tm, tn, tk = _mm_tiles(m, f, k, a.dtype.itemsize, 2 * w_t.dtype.itemsize, 3 * 2, 0, False, 0, 0, 0, 1024, 512)
    assert tk == k, "the fused activation needs the whole contraction in one block"

    def body(a_ref, g_ref, u_ref, gate_ref, up_ref, act_ref):
        av = a_ref[...].astype(BF16)
        gate = lax.dot_general(av, g_ref[...].astype(BF16), _NT, preferred_element_type=F32)
        up = lax.dot_general(av, u_ref[...].astype(BF16), _NT, preferred_element_type=F32)
        gate_ref[...] = gate.astype(gate_ref.dtype)
        up_ref[...] = up.astype(up_ref.dtype)
        act_ref[...] = _swiglu2(gate, up).astype(act_ref.dtype)

    out_spec = pl.BlockSpec((tm, tn), lambda i, j: (i, j))
    outs, ride_outs = _pallas(
        body, grid=(m // tm, f // tn),
        in_specs=[pl.BlockSpec((tm, k), lambda i, j: (i, 0)), pl.BlockSpec((tn, k), lambda i, j: (j, 0)),
                  pl.BlockSpec((tn, k), lambda i, j: (j + f // tn, 0))],
        out_specs=[out_spec] * 3, out_shape=[jax.ShapeDtypeStruct((m, f), BF16)] * 3, args=[a, w_t, w_t],
        semantics=("parallel", "parallel"), rides=rides or (), name=name)
    return outs if rides is None else (outs, ride_outs)


def _ln(x, g, b, eps=1e-5):
    mu = jnp.mean(x, axis=-1, keepdims=True)
    var = jnp.mean(jnp.square(x - mu), axis=-1, keepdims=True)
    return (x - mu) * lax.rsqrt(var + eps) * g + b


def _post_norm(x, f, g, b):
    return _ln(ALPHA * x + f, g, b)


def _post_norm_bwd(x, f, dy, g, b):
    _, vjp = jax.vjp(_post_norm, x, f, g, b)
    return vjp(dy)


def _merge3(g0, g1, g2, pa, pb, pc):
    return jax.nn.sigmoid(g0) * pa + jax.nn.sigmoid(g1) * pb + jax.nn.sigmoid(g2) * pc


def _merge_args(gl, pa, pb, pc):
    d = pa.shape[1]
    return [a.astype(F32) for a in (gl[:, :d], gl[:, d:2 * d], gl[:, 2 * d:], pa, pb, pc)]


def _merge(gl, pa, pb, pc):
    return _merge3(*_merge_args(gl, pa, pb, pc))


def _merge_bwd(gl, pa, pb, pc, dm):
    _, vjp = jax.vjp(_merge3, *_merge_args(gl, pa, pb, pc))
    d0, d1, d2, dpa, dpb, dpc = vjp(dm)
    dgl = jnp.concatenate([d0, d1, d2], axis=1)
    return dgl, dpa, dpb, dpc, jnp.sum(dgl, axis=0, keepdims=True)


def _swiglu2(gate, up):
    return jax.nn.silu(gate) * up


def _swiglu_bwd(gate, up, dact):
    _, vjp = jax.vjp(_swiglu2, gate.astype(F32), up.astype(F32))
    dg, du = vjp(dact.astype(F32))
    return jnp.concatenate([dg, du], axis=1)


def _glu(ycp, lin):
    return ycp * jax.nn.sigmoid(lin)


def _glu_bwd(ycp, lin, dyc):
    _, vjp = jax.vjp(_glu, ycp, lin)
    dycp, dlin = vjp(dyc)
    return dycp, dlin, jnp.sum(dlin, axis=0, keepdims=True)


def _s5_out_bwd(ys, uc, dycp, dskip):
    _, vjp = jax.vjp(jax.nn.gelu, ys)
    dys = vjp(dycp)[0]
    return dys, dys * dskip, jnp.sum(dys * uc, axis=0, keepdims=True)


def _combine(o0, l0, o1, l1, o2, l2):
    m = jnp.maximum(jnp.maximum(l0, l1), l2)
    e0, e1, e2 = jnp.exp(l0 - m), jnp.exp(l1 - m), jnp.exp(l2 - m)
    s = e0 + e1 + e2
    return (e0 / s) * o0 + (e1 / s) * o1 + (e2 / s) * o2


def _combine_bwd(o0, l0, o1, l1, o2, l2, ya, dya, head_ones):
    m = jnp.maximum(jnp.maximum(l0, l1), l2)
    e0, e1, e2 = jnp.exp(l0 - m), jnp.exp(l1 - m), jnp.exp(l2 - m)
    s = e0 + e1 + e2
    dot_ya = jnp.dot(dya * ya, head_ones, precision=lax.Precision.HIGHEST, preferred_element_type=F32)
    w0, w1, w2 = e0 / s, e1 / s, e2 / s
    return w0 * dya, w1 * dya, w2 * dya, -w0 * dot_ya, -w1 * dot_ya, -w2 * dot_ya


def _loss_fn(y, tgt):
    err = y - tgt
    part = jnp.sum(jnp.sum(jnp.square(err), axis=1, keepdims=True), axis=0, keepdims=True) * (0.5 / y.shape[1])
    return err * (1.0 / y.shape[1]), jnp.broadcast_to(part, (1, LANES))


def _adamw(w, g, m, v):
    m = ADAM_B1 * m + (1.0 - ADAM_B1) * g
    v = ADAM_B2 * v + (1.0 - ADAM_B2) * jnp.square(g)
    m_hat = m / (1.0 - ADAM_B1 ** ADAM_STEP)
    v_hat = v / (1.0 - ADAM_B2 ** ADAM_STEP)
    delta = -ADAM_LR * (m_hat / (jnp.sqrt(v_hat) + ADAM_EPS) + ADAM_WD * w)
    return delta, m, v


def _sum_parts_fn(parts):
    g = parts[0].astype(F32)
    for j in range(1, parts.shape[0]):
        g = g + parts[j].astype(F32)
    return g


def _t5_bucket(dist):
    max_exact = N_REL_BUCKETS // 2
    d = np.maximum(dist, 1).astype(np.float32)
    scale = (N_REL_BUCKETS - max_exact) / math.log(REL_MAX_DIST / max_exact)
    large = max_exact + (np.log(d / max_exact) * scale).astype(np.int32)
    large = np.minimum(large, N_REL_BUCKETS - 1)
    return np.where(dist < max_exact, dist, large).astype(np.int32)


def _bucket_table(dilation):
    i = np.arange(ATT_BLOCK)[:, None]
    kk = np.arange(2 * ATT_BLOCK)[None, :]
    steps = ATT_BLOCK + i - kk
    return _t5_bucket(np.maximum(steps, 0) * dilation)


def _bias_fwd(rel_bias, buckets, g):
    def body(rel_ref, bk_ref, o_ref):
        bk = bk_ref[...]
        for h in range(HEADS_PER_GROUP):
            acc = jnp.zeros(bk.shape, F32)
            for b in range(N_REL_BUCKETS):
                acc = jnp.where(bk == b, rel_ref[b, g * HEADS_PER_GROUP + h], acc)
            o_ref[h] = acc

    return pl.pallas_call(
        body, in_specs=[pl.BlockSpec(memory_space=pltpu.SMEM), pl.BlockSpec(memory_space=pltpu.VMEM)],
        out_specs=pl.BlockSpec(memory_space=pltpu.VMEM),
        out_shape=jax.ShapeDtypeStruct((HEADS_PER_GROUP, ATT_BLOCK, 2 * ATT_BLOCK), F32),
        name=f"rel_bias_fwd{g}")(rel_bias, buckets)


def _bias_bwd(dbias, buckets, g):
    def body(db_ref, bk_ref, o_ref):
        bk = bk_ref[...]
        row = lax.broadcasted_iota(jnp.int32, (N_REL_BUCKETS, LANES), 0)
        col = lax.broadcasted_iota(jnp.int32, (N_REL_BUCKETS, LANES), 1)
        acc = jnp.zeros((N_REL_BUCKETS, LANES), F32)
        for h in range(HEADS_PER_GROUP):
            d = db_ref[h]
            for b in range(N_REL_BUCKETS):
                s = jnp.sum(jnp.sum(jnp.where(bk == b, d, 0.0), axis=1, keepdims=True), axis=0, keepdims=True)
                acc = acc + jnp.where((row == b) & (col == g * HEADS_PER_GROUP + h), s, 0.0)
        o_ref[...] = acc

    return pl.pallas_call(
        body, in_specs=[pl.BlockSpec(memory_space=pltpu.VMEM), pl.BlockSpec(memory_space=pltpu.VMEM)],
        out_specs=pl.BlockSpec(memory_space=pltpu.VMEM),
        out_shape=jax.ShapeDtypeStruct((N_REL_BUCKETS, LANES), F32), name=f"rel_bias_bwd{g}")(dbias, buckets)


_NT = (((1,), (1,)), ((), ()))
_TN = (((0,), (0,)), ((), ()))
_QKV_BLOCKS = 3 * QKV_WIDTH // WIDTH_A


def _band_mask(n_is_first):
    i = lax.broadcasted_iota(jnp.int32, (ATT_BLOCK, 2 * ATT_BLOCK), 0)
    kk = lax.broadcasted_iota(jnp.int32, (ATT_BLOCK, 2 * ATT_BLOCK), 1)
    return (kk >= i) & (kk <= i + ATT_STEPS) & ((kk >= ATT_BLOCK) | jnp.logical_not(n_is_first))


def _head(ref, h):
    return ref[:, h * HEAD_DIM:(h + 1) * HEAD_DIM]


def _attn_specs(g, d):
    blk = (ATT_BLOCK, WIDTH_A)
    q = pl.BlockSpec(blk, lambda c, n: (n, c * _QKV_BLOCKS + g))
    kp = pl.BlockSpec(blk, lambda c, n: (jnp.maximum(n - 1, 0), c * _QKV_BLOCKS + 3 + g))
    kc = pl.BlockSpec(blk, lambda c, n: (n, c * _QKV_BLOCKS + 3 + g))
    vp = pl.BlockSpec(blk, lambda c, n: (jnp.maximum(n - 1, 0), c * _QKV_BLOCKS + 6 + g))
    vc = pl.BlockSpec(blk, lambda c, n: (n, c * _QKV_BLOCKS + 6 + g))
    return [q, kp, kc, vp, vc]


def _attn_fwd(qkv, bias, g, d, comm):
    t = qkv.shape[0]
    lq = t // d
    nb = lq // ATT_BLOCK
    scale = HEAD_DIM ** -0.5

    def body(q_ref, kp_ref, kc_ref, vp_ref, vc_ref, b_ref, o_ref, l_ref):
        mask = _band_mask(pl.program_id(1) == 0)
        for h in range(HEADS_PER_GROUP):
            qh = _head(q_ref, h).astype(BF16)
            kh = jnp.concatenate([_head(kp_ref, h), _head(kc_ref, h)], axis=0).astype(BF16)
            vh = jnp.concatenate([_head(vp_ref, h), _head(vc_ref, h)], axis=0).astype(BF16)
            s = lax.dot_general(qh, kh, _NT, preferred_element_type=F32) * scale + b_ref[h]
            s = jnp.where(mask, s, NEG_INF)
            m = jnp.max(s, axis=1, keepdims=True)
            p = jnp.exp(s - m)
            den = jnp.sum(p, axis=1, keepdims=True)
            o = jnp.dot(p.astype(BF16), vh, preferred_element_type=F32) / den
            o_ref[:, h * HEAD_DIM:(h + 1) * HEAD_DIM] = o
            l_ref[:, h * HEAD_DIM:(h + 1) * HEAD_DIM] = jnp.broadcast_to(m + jnp.log(den), (ATT_BLOCK, HEAD_DIM))

    out_spec = pl.BlockSpec((ATT_BLOCK, WIDTH_A), lambda c, n: (n, c))
    (o, lse), ride_outs = _pallas(
        body, grid=(d, nb),
        in_specs=_attn_specs(g, d) + [pl.BlockSpec(bias.shape, _zero_map(3))],
        out_specs=[out_spec, out_spec],
        out_shape=[jax.ShapeDtypeStruct((lq, d * WIDTH_A), F32)] * 2,
        args=[*([qkv.reshape(lq, d * 3 * QKV_WIDTH)] * 5), bias],
        semantics=("parallel", "parallel"), rides=[comm.ride(f"attn_fwd{g}")], name=f"attn_fwd{g}")
    comm.took(ride_outs)
    return o.reshape(t, WIDTH_A), lse.reshape(t, WIDTH_A)


def _attn_bwd(qkv, bias, do, lse, corr, g, d):
    t = qkv.shape[0]
    lq = t // d
    nb = lq // ATT_BLOCK
    scale = HEAD_DIM ** -0.5

    def body(k_ref, v_ref, q0_ref, q1_ref, do0_ref, do1_ref, l0_ref, l1_ref, c0_ref, c1_ref, b_ref,
             dq_ref, dk_ref, dv_ref, db_ref, dq_prev):
        c, j = pl.program_id(0), pl.program_id(1)

        @pl.when((c == 0) & (j == 0))
        def _():
            db_ref[...] = jnp.zeros_like(db_ref)

        @pl.when(j == 0)
        def _():
            dq_prev[...] = jnp.zeros_like(dq_prev)

        i = lax.broadcasted_iota(jnp.int32, (ATT_BLOCK, ATT_BLOCK), 0)
        kk = lax.broadcasted_iota(jnp.int32, (ATT_BLOCK, ATT_BLOCK), 1)
        mask0 = kk <= i
        mask1 = (kk >= i) & (j + 1 < nb)
        for h in range(HEADS_PER_GROUP):
            kh = _head(k_ref, h).astype(BF16)
            vh = _head(v_ref, h).astype(BF16)
            cols = slice(h * HEAD_DIM, (h + 1) * HEAD_DIM)
            dk = jnp.zeros((ATT_BLOCK, HEAD_DIM), F32)
            dv = jnp.zeros((ATT_BLOCK, HEAD_DIM), F32)
            dq_parts = []
            parts = ((q0_ref, do0_ref, l0_ref, c0_ref, mask0, ATT_BLOCK), (q1_ref, do1_ref, l1_ref, c1_ref, mask1, 0))
            for q_ref, do_ref, l_ref, c_ref, mask, off in parts:
                qh = _head(q_ref, h).astype(BF16)
                doh = _head(do_ref, h).astype(BF16)
                s = lax.dot_general(qh, kh, _NT, preferred_element_type=F32) * scale + b_ref[h, :, off:off + ATT_BLOCK]
                s = jnp.where(mask, s, NEG_INF)
                p = jnp.exp(s - l_ref[:, h * HEAD_DIM:h * HEAD_DIM + 1])
                dp = lax.dot_general(doh, vh, _NT, preferred_element_type=F32)
                ds = p * (dp + c_ref[:, h * HEAD_DIM:h * HEAD_DIM + 1])
                dsb = ds.astype(BF16)
                dv = dv + lax.dot_general(p.astype(BF16), doh, _TN, preferred_element_type=F32)
                dk = dk + lax.dot_general(dsb, qh, _TN, preferred_element_type=F32)
                dq_parts.append(jnp.dot(dsb, kh, preferred_element_type=F32))
                db_ref[h, :, off:off + ATT_BLOCK] += ds
            dk_ref[:, cols] = dk * scale
            dv_ref[:, cols] = dv
            dq_ref[:, cols] = (dq_prev[:, cols] + dq_parts[0]) * scale
            dq_prev[:, cols] = dq_parts[1]

    blk = (ATT_BLOCK, WIDTH_A)
    nxt = lambda n: jnp.minimum(n + 1, nb - 1)
    k_spec = pl.BlockSpec(blk, lambda c, n: (n, c * _QKV_BLOCKS + 3 + g))
    v_spec = pl.BlockSpec(blk, lambda c, n: (n, c * _QKV_BLOCKS + 6 + g))
    q0_spec = pl.BlockSpec(blk, lambda c, n: (n, c * _QKV_BLOCKS + g))
    q1_spec = pl.BlockSpec(blk, lambda c, n: (nxt(n), c * _QKV_BLOCKS + g))
    r0 = pl.BlockSpec(blk, lambda c, n: (n, c))
    r1 = pl.BlockSpec(blk, lambda c, n: (nxt(n), c))
    view = lambda a: a.reshape(lq, d * WIDTH_A)
    qv = qkv.reshape(lq, d * 3 * QKV_WIDTH)
    dq, dk, dv, dbias = pl.pallas_call(
        body, grid=(d, nb),
        in_specs=[k_spec, v_spec, q0_spec, q1_spec, r0, r1, r0, r1, r0, r1, pl.BlockSpec(bias.shape, _zero_map(3))],
        out_specs=[r0, r0, r0, pl.BlockSpec(bias.shape, _zero_map(3))],
        out_shape=[jax.ShapeDtypeStruct((lq, d * WIDTH_A), F32)] * 3 + [jax.ShapeDtypeStruct(bias.shape, F32)],
        scratch_shapes=[pltpu.VMEM(blk, F32)],
        compiler_params=_cparams("arbitrary", "arbitrary"), name=f"attn_bwd{g}",
    )(qv, qv, qv, qv, view(do), view(do), view(lse), view(lse), view(corr), view(corr), bias)
    return dq.reshape(t, WIDTH_A), dk.reshape(t, WIDTH_A), dv.reshape(t, WIDTH_A), dbias


def _tril_mask():
    r = lax.broadcasted_iota(jnp.int32, (CHUNK, CHUNK), 0)
    c = lax.broadcasted_iota(jnp.int32, (CHUNK, CHUNK), 1)
    return c <= r


def _gmlp_fwd(zb, ln_g, ln_b, w_s, b_s_t):
    t = zb.shape[0]
    tr = _pick(t, 2 * CHUNK, CHUNK)

    def body(z_ref, g_ref, b_ref, ws_ref, bs_ref, o_ref):
        tri = _tril_mask()
        z = jax.nn.gelu(z_ref[...])
        u = z[:, :WIDTH_B]
        vn = _ln(z[:, WIDTH_B:], g_ref[...], b_ref[...])
        for ch in range(tr // CHUNK):
            rows = slice(ch * CHUNK, (ch + 1) * CHUNK)
            for gi in range(N_GROUPS_B):
                cols = slice(gi * CHUNK, (gi + 1) * CHUNK)
                w = jnp.where(tri, ws_ref[gi], 0.0).astype(BF16)
                mixed = jnp.dot(w, vn[rows, cols].astype(BF16), preferred_element_type=F32) + bs_ref[:, gi:gi + 1]
                o_ref[rows, cols] = (u[rows, cols] * mixed).astype(o_ref.dtype)

    return pl.pallas_call(
        body, grid=(t // tr,),
        in_specs=[pl.BlockSpec((tr, 2 * WIDTH_B), lambda i: (i, 0)), pl.BlockSpec(ln_g.shape, _zero_map(2)),
                  pl.BlockSpec(ln_b.shape, _zero_map(2)), pl.BlockSpec(w_s.shape, _zero_map(3)),
                  pl.BlockSpec(b_s_t.shape, _zero_map(2))],
        out_specs=pl.BlockSpec((tr, WIDTH_B), lambda i: (i, 0)),
        out_shape=jax.ShapeDtypeStruct((t, WIDTH_B), BF16),
        compiler_params=_cparams("parallel"), name="gmlp_fwd")(zb, ln_g, ln_b, w_s, b_s_t)


def _gmlp_bwd(zb, dyb, ln_g, ln_b, w_s, b_s_t, group_sel):
    t = zb.shape[0]
    tr = _pick(t, 2 * CHUNK, CHUNK)

    def body(z_ref, dy_ref, g_ref, b_ref, ws_ref, bs_ref, sel_ref, dz_ref, dzs_ref, dg_ref, db_ref, dws_ref, dbs_ref,
             du_s, dvn_s, dm_s):
        @pl.when(pl.program_id(0) == 0)
        def _():
            dzs_ref[...] = jnp.zeros_like(dzs_ref)
            dg_ref[...] = jnp.zeros_like(dg_ref)
            db_ref[...] = jnp.zeros_like(db_ref)
            dws_ref[...] = jnp.zeros_like(dws_ref)
            dbs_ref[...] = jnp.zeros_like(dbs_ref)

        tri = _tril_mask()
        z, gelu_vjp = jax.vjp(jax.nn.gelu, z_ref[...])
        u = z[:, :WIDTH_B]
        vn, ln_vjp = jax.vjp(_ln, z[:, WIDTH_B:], g_ref[...], b_ref[...])
        dy = dy_ref[...]
        for ch in range(tr // CHUNK):
            rows = slice(ch * CHUNK, (ch + 1) * CHUNK)
            for gi in range(N_GROUPS_B):
                cols = slice(gi * CHUNK, (gi + 1) * CHUNK)
                w = jnp.where(tri, ws_ref[gi], 0.0).astype(BF16)
                vg = vn[rows, cols].astype(BF16)
                mixed = jnp.dot(w, vg, preferred_element_type=F32) + bs_ref[:, gi:gi + 1]
                dyg = dy[rows, cols]
                dm = dyg * u[rows, cols]
                dmb = dm.astype(BF16)
                du_s[rows, cols] = dyg * mixed
                dm_s[rows, cols] = dm
                dvn_s[rows, cols] = lax.dot_general(w, dmb, _TN, preferred_element_type=F32)
                dws_ref[gi] += jnp.where(tri, lax.dot_general(dmb, vg, _NT, preferred_element_type=F32), 0.0)
            dbs_ref[...] += jnp.dot(dm_s[rows, :], sel_ref[...], precision=lax.Precision.HIGHEST,
                                    preferred_element_type=F32)
        dv, dg, db = ln_vjp(dvn_s[...])
        dg_ref[...] += dg
        db_ref[...] += db
        dz = gelu_vjp(jnp.concatenate([du_s[...], dv], axis=1))[0]
        dz_ref[...] = dz.astype(dz_ref.dtype)
        dzs_ref[...] += jnp.sum(dz, axis=0, keepdims=True)

    full = lambda a: pl.BlockSpec(a.shape, _zero_map(a.ndim))
    return pl.pallas_call(
        body, grid=(t // tr,),
        in_specs=[pl.BlockSpec((tr, 2 * WIDTH_B), lambda i: (i, 0)), pl.BlockSpec((tr, WIDTH_B), lambda i: (i, 0)),
                  full(ln_g), full(ln_b), full(w_s), full(b_s_t), full(group_sel)],
        out_specs=[pl.BlockSpec((tr, 2 * WIDTH_B), lambda i: (i, 0)), pl.BlockSpec((1, 2 * WIDTH_B), _zero_map(2)),
                   full(ln_g), full(ln_b), full(w_s), pl.BlockSpec((CHUNK, LANES), _zero_map(2))],
        out_shape=[jax.ShapeDtypeStruct((t, 2 * WIDTH_B), BF16), jax.ShapeDtypeStruct((1, 2 * WIDTH_B), F32),
                   jax.ShapeDtypeStruct(ln_g.shape, F32),
                   jax.ShapeDtypeStruct(ln_b.shape, F32), jax.ShapeDtypeStruct(w_s.shape, F32),
                   jax.ShapeDtypeStruct((CHUNK, LANES), F32)],
        scratch_shapes=[pltpu.VMEM((tr, WIDTH_B), F32)] * 3,
        compiler_params=_cparams("arbitrary"), name="gmlp_bwd")(zb, dyb, ln_g, ln_b, w_s, b_s_t, group_sel)


def _s5_disc(lr, li, ldt, br_t, bi_t):
    dt = jnp.exp(ldt)
    mag = jnp.exp(lr * dt)
    ab_re = mag * jnp.cos(li * dt)
    ab_im = mag * jnp.sin(li * dt)
    nrm = lr * lr + li * li
    cr = ((ab_re - 1.0) * lr + ab_im * li) / nrm
    ci = (ab_im * lr - (ab_re - 1.0) * li) / nrm
    return ab_re, ab_im, cr * br_t - ci * bi_t, cr * bi_t + ci * br_t


def _vmem_call(fn, args, out_shape, name):
    def body(*refs):
        res = fn(*[r[...] for r in refs[:len(args)]])
        for o, v in zip(refs[len(args):], res):
            o[...] = v

    vm = pl.BlockSpec(memory_space=pltpu.VMEM)
    return pl.pallas_call(body, in_specs=[vm] * len(args), out_specs=[vm] * len(out_shape),
                          out_shape=out_shape, name=name)(*args)


def _s5_disc_fwd(lr, li, ldt, br_t, bi_t):
    s1 = jax.ShapeDtypeStruct(lr.shape, F32)
    s2 = jax.ShapeDtypeStruct(br_t.shape, F32)
    return _vmem_call(_s5_disc, [lr, li, ldt, br_t, bi_t], [s1, s1, s2, s2], "s5_disc_fwd")


def _s5_disc_bwd(lr, li, ldt, br_t, bi_t, cts):
    def fn(lr, li, ldt, br_t, bi_t, d0, d1, d2, d3):
        _, vjp = jax.vjp(_s5_disc, lr, li, ldt, br_t, bi_t)
        return vjp((d0, d1, d2, d3))

    shp = [jax.ShapeDtypeStruct(a.shape, F32) for a in (lr, li, ldt, br_t, bi_t)]
    return _vmem_call(fn, [lr, li, ldt, br_t, bi_t, *cts], shp, "s5_disc_bwd")


_SCAN_ROWS = SSM_COLS // LANES
_SCAN_CHUNK = 128
_SSM_IN = SSM_PACK * SSM_GROUP
_SSM_ST = SSM_PACK * SSM_STATE


def _packed_in(xb, m_ref):
    return jnp.concatenate([jnp.dot(xb[:, j * _SSM_IN:(j + 1) * _SSM_IN], m_ref[j], preferred_element_type=F32)
                            for j in range(N_SSM_BLOCKS)], axis=1)


def _packed_out(xb, m_ref):
    return jnp.concatenate([lax.dot_general(xb[:, j * _SSM_ST:(j + 1) * _SSM_ST], m_ref[j], _NT, preferred_element_type=F32)
                            for j in range(N_SSM_BLOCKS)], axis=1)


def _s5_fwd(uc, mats, are, aim, d_skip, comm):
    t = uc.shape[0]
    tc = _pick(t, _SCAN_CHUNK, SUBLANES)

    def body(u_ref, br_ref, bi_ref, cr_ref, ci_ref, ar_ref, ai_ref, d_ref, xr_ref, xi_ref, ys_ref, ycp_ref,
             sr, si, st_ref):
        @pl.when(pl.program_id(0) == 0)
        def _():
            st_ref[...] = jnp.zeros_like(st_ref)

        u = u_ref[...]
        ub = u.astype(BF16)
        sr[...] = _packed_in(ub, br_ref).reshape(tc, _SCAN_ROWS, LANES)
        si[...] = _packed_in(ub, bi_ref).reshape(tc, _SCAN_ROWS, LANES)
        ar, ai = ar_ref[...], ai_ref[...]

        def step(i, carry):
            xr, xi = carry
            nr = ar * xr - ai * xi + sr[i]
            ni = ar * xi + ai * xr + si[i]
            sr[i] = nr
            si[i] = ni
            return nr, ni

        xr, xi = lax.fori_loop(0, tc, step, (st_ref[0], st_ref[1]), unroll=8)
        st_ref[0] = xr
        st_ref[1] = xi
        x_re = sr[...].reshape(tc, SSM_COLS)
        x_im = si[...].reshape(tc, SSM_COLS)
        xr_ref[...] = x_re
        xi_ref[...] = x_im
        ys = _packed_out(x_re.astype(BF16), cr_ref) - _packed_out(x_im.astype(BF16), ci_ref) + d_ref[...] * u
        ys_ref[...] = ys
        ycp_ref[...] = jax.nn.gelu(ys)

    row = lambda w: pl.BlockSpec((tc, w), lambda i: (i, 0))
    mat = pl.BlockSpec(mats[0].shape, _zero_map(3))
    par = pl.BlockSpec((_SCAN_ROWS, LANES), _zero_map(2))
    wide, narrow = jax.ShapeDtypeStruct((t, SSM_COLS), F32), jax.ShapeDtypeStruct((t, WIDTH_C), F32)
    outs, ride_outs = _pallas(
        body, grid=(t // tc,), in_specs=[row(WIDTH_C), mat, mat, mat, mat, par, par, pl.BlockSpec(d_skip.shape, _zero_map(2))],
        out_specs=[row(SSM_COLS), row(SSM_COLS), row(WIDTH_C), row(WIDTH_C)], out_shape=[wide, wide, narrow, narrow],
        args=[uc, *mats, are, aim, d_skip],
        scratch=[pltpu.VMEM((tc, _SCAN_ROWS, LANES), F32)] * 2 + [pltpu.VMEM((2, _SCAN_ROWS, LANES), F32)],
        semantics=("arbitrary",), rides=[comm.ride("s5_fwd")], name="s5_fwd")
    comm.took(ride_outs)
    return outs


def _s5_bwd(dys, duc_skip, uc, xr, xi, mats, are, aim):
    t = dys.shape[0]
    tc = _pick(t, _SCAN_CHUNK, SUBLANES)
    nc = t // tc

    def body(dy_ref, ds_ref, u_ref, xr_ref, xi_ref, pr_ref, pi_ref, br_ref, bi_ref, cr_ref, ci_ref, ar_ref, ai_ref,
             du_ref, dbr_ref, dbi_ref, dcr_ref, dci_ref, dar_ref, dai_ref, gr, gi, x3r, x3i, st_ref):
        step_id = pl.program_id(0)

        @pl.when(step_id == 0)
        def _():
            st_ref[...] = jnp.zeros_like(st_ref)
            for ref in (dbr_ref, dbi_ref, dcr_ref, dci_ref, dar_ref, dai_ref):
                ref[...] = jnp.zeros_like(ref)

        dyb = dy_ref[...].astype(BF16)
        x_re, x_im = xr_ref[...], xi_ref[...]
        gr[...] = _packed_in(dyb, cr_ref).reshape(tc, _SCAN_ROWS, LANES)
        gi[...] = (-_packed_in(dyb, ci_ref)).reshape(tc, _SCAN_ROWS, LANES)
        x3r[...] = x_re.reshape(tc, _SCAN_ROWS, LANES)
        x3i[...] = x_im.reshape(tc, _SCAN_ROWS, LANES)
        ar, ai = ar_ref[...], ai_ref[...]

        def update(i, carry, pxr, pxi):
            g_r, g_i, dar, dai = carry
            ngr = gr[i] + ar * g_r + ai * g_i
            ngi = gi[i] - ai * g_r + ar * g_i
            gr[i] = ngr
            gi[i] = ngi
            return ngr, ngi, dar + ngr * pxr + ngi * pxi, dai - ngr * pxi + ngi * pxr

        def step(s, carry):
            i = tc - 1 - s
            return update(i, carry, x3r[i - 1], x3i[i - 1])

        zero = jnp.zeros((_SCAN_ROWS, LANES), F32)
        carry = lax.fori_loop(0, tc - 1, step, (st_ref[0], st_ref[1], zero, zero), unroll=8)
        has_prev = (step_id < nc - 1).astype(F32)
        last = SUBLANES - 1
        p_re = pr_ref[last:, :].reshape(1, _SCAN_ROWS, LANES)[0] * has_prev
        p_im = pi_ref[last:, :].reshape(1, _SCAN_ROWS, LANES)[0] * has_prev
        g_r, g_i, dar, dai = update(0, carry, p_re, p_im)
        st_ref[0] = g_r
        st_ref[1] = g_i
        dar_ref[...] += dar
        dai_ref[...] += dai

        g_re = gr[...].reshape(tc, SSM_COLS).astype(BF16)
        g_im = gi[...].reshape(tc, SSM_COLS).astype(BF16)
        du_ref[...] = ds_ref[...] + _packed_out(g_re, br_ref) + _packed_out(g_im, bi_ref)
        ub, xrb, xib = u_ref[...].astype(BF16), x_re.astype(BF16), x_im.astype(BF16)
        for j in range(N_SSM_BLOCKS):
            narrow, wide = slice(j * _SSM_IN, (j + 1) * _SSM_IN), slice(j * _SSM_ST, (j + 1) * _SSM_ST)
            dbr_ref[j] += lax.dot_general(ub[:, narrow], g_re[:, wide], _TN, preferred_element_type=F32)
            dbi_ref[j] += lax.dot_general(ub[:, narrow], g_im[:, wide], _TN, preferred_element_type=F32)
            dcr_ref[j] += lax.dot_general(dyb[:, narrow], xrb[:, wide], _TN, preferred_element_type=F32)
            dci_ref[j] -= lax.dot_general(dyb[:, narrow], xib[:, wide], _TN, preferred_element_type=F32)

    rev = lambda w: pl.BlockSpec((tc, w), lambda i: (nc - 1 - i, 0))
    prev = pl.BlockSpec((SUBLANES, SSM_COLS), lambda i: (jnp.maximum((nc - 1 - i) * (tc // SUBLANES) - 1, 0), 0))
    mat = pl.BlockSpec(mats[0].shape, _zero_map(3))
    par = pl.BlockSpec((_SCAN_ROWS, LANES), _zero_map(2))
    msh = jax.ShapeDtypeStruct(mats[0].shape, F32)
    psh = jax.ShapeDtypeStruct((_SCAN_ROWS, LANES), F32)
    return pl.pallas_call(
        body, grid=(nc,),
        in_specs=[rev(WIDTH_C), rev(WIDTH_C), rev(WIDTH_C), rev(SSM_COLS), rev(SSM_COLS), prev, prev,
                  mat, mat, mat, mat, par, par],
        out_specs=[rev(WIDTH_C), mat, mat, mat, mat, par, par],
        out_shape=[jax.ShapeDtypeStruct((t, WIDTH_C), F32), msh, msh, msh, msh, psh, psh],
        scratch_shapes=[pltpu.VMEM((tc, _SCAN_ROWS, LANES), F32)] * 4 + [pltpu.VMEM((2, _SCAN_ROWS, LANES), F32)],
        compiler_params=_cparams("arbitrary"), name="s5_bwd")(dys, duc_skip, uc, xr, xi, xr, xi, *mats, are, aim)


def _diag_blocks(a, b):
    return [(j, i, slice(i * a, (i + 1) * a), slice(i * b, (i + 1) * b))
            for j in range(N_SSM_BLOCKS) for i in range(SSM_PACK)]


def _block_diag(ms):
    _, a, b = ms[0].shape

    def body(*refs):
        for m_ref, o_ref in zip(refs[:len(ms)], refs[len(ms):]):
            o_ref[...] = jnp.zeros_like(o_ref)
            for j, i, rows, cols in _diag_blocks(a, b):
                o_ref[j, rows, cols] = m_ref[j * SSM_PACK + i].astype(o_ref.dtype)

    vm = pl.BlockSpec(memory_space=pltpu.VMEM)
    shape = jax.ShapeDtypeStruct((N_SSM_BLOCKS, SSM_PACK * a, SSM_PACK * b), BF16)
    return pl.pallas_call(body, in_specs=[vm] * len(ms), out_specs=[vm] * len(ms), out_shape=[shape] * len(ms),
                          name="s5_block_diag")(*ms)


def _block_diag_extract(ms, a, b):
    def body(*refs):
        for m_ref, o_ref in zip(refs[:len(ms)], refs[len(ms):]):
            for j, i, rows, cols in _diag_blocks(a, b):
                o_ref[j * SSM_PACK + i] = m_ref[j, rows, cols]

    vm = pl.BlockSpec(memory_space=pltpu.VMEM)
    shape = jax.ShapeDtypeStruct((N_GROUPS_C, a, b), F32)
    return pl.pallas_call(body, in_specs=[vm] * len(ms), out_specs=[vm] * len(ms), out_shape=[shape] * len(ms),
                          name="s5_block_diag_extract")(*ms)


def _exchange(src, *, gather, name):
    shape = src.shape if gather else src.shape[1:]

    def body(src_ref, out_ref, send_sems, recv_sems, local_sem):
        x, y, c = lax.axis_index("x"), lax.axis_index("y"), lax.axis_index("c")
        me = 4 * x + 2 * y + c
        copies = []
        for r in range(1, N_DEV):
            px = 1 - x if r & 4 else x
            py = 1 - y if r & 2 else y
            pc = 1 - c if r & 1 else c
            piece = src_ref if gather else src_ref.at[4 * px + 2 * py + pc]
            cp = pltpu.make_async_remote_copy(
                src_ref=piece, dst_ref=out_ref.at[me], send_sem=send_sems.at[r - 1], recv_sem=recv_sems.at[r - 1],
                device_id=(px, py, pc), device_id_type=pl.DeviceIdType.MESH)
            cp.start()
            copies.append(cp)
        mine = pltpu.make_async_copy(src_ref if gather else src_ref.at[me], out_ref.at[me], local_sem)
        mine.start()
        for cp in copies:
            cp.wait()
        mine.wait()

    hbm = pl.BlockSpec(memory_space=pl.ANY)
    return pl.pallas_call(
        body, in_specs=[hbm], out_specs=hbm, out_shape=jax.ShapeDtypeStruct((N_DEV,) + tuple(shape), src.dtype),
        scratch_shapes=[pltpu.SemaphoreType.DMA((N_DEV - 1,)), pltpu.SemaphoreType.DMA((N_DEV - 1,)),
                        pltpu.SemaphoreType.DMA(())],
        name=name)(src)


def _mesh_place():
    x, y, c = lax.axis_index("x"), lax.axis_index("y"), lax.axis_index("c")
    other_chips = [(1 - x, y), (x, 1 - y), (1 - x, 1 - y)]
    return x, y, c, other_chips


def _gather_layer(srcs, layer, name):
    n = len(srcs)

    def body(*refs):
        src = [r.at[layer] for r in refs[:n]]
        out = refs[n:2 * n]
        send_sems, recv_sems, local_sems = refs[2 * n:]
        x, y, c, chips = _mesh_place()
        me, sibling = (x, y, c), (x, y, 1 - c)

        def copy(t, k, block, to, from_src=False):
            slot = 4 * block[0] + 2 * block[1] + block[2]
            return pltpu.make_async_remote_copy(
                src_ref=src[t] if from_src else out[t].at[slot], dst_ref=out[t].at[slot],
                send_sem=send_sems.at[t, k], recv_sem=recv_sems.at[t, k], device_id=to, device_id_type=_MESH_ID)

        mine = [pltpu.make_async_copy(src[t], out[t].at[4 * x + 2 * y + c], local_sems.at[t]) for t in range(n)]
        for cp in mine:
            cp.start()
        first = []
        for t in range(n):
            first.append(copy(t, 0, me, sibling, True))
            first += [copy(t, 1 + j, me, (*chip, c), True) for j, chip in enumerate(chips)]
        for cp in first:
            cp.start()
        passed = []
        for j, chip in enumerate(chips):
            for t in range(n):
                copy(t, 1 + j, (*chip, c), me).wait_recv()
                fwd = copy(t, 4 + j, (*chip, c), sibling)
                fwd.start()
                passed.append(fwd)
        for t in range(n):
            copy(t, 0, sibling, me).wait_recv()
            for j, chip in enumerate(chips):
                copy(t, 4 + j, (*chip, 1 - c), me).wait_recv()
        for cp in first + passed:
            cp.wait_send()
        for cp in mine:
            cp.wait()

    return pl.pallas_call(
        body, in_specs=[_HBM] * n, out_specs=[_HBM] * n,
        out_shape=[jax.ShapeDtypeStruct((N_DEV,) + s.shape[1:], s.dtype) for s in srcs],
        scratch_shapes=[pltpu.SemaphoreType.DMA((n, N_DEV - 1)), pltpu.SemaphoreType.DMA((n, N_DEV - 1)),
                        pltpu.SemaphoreType.DMA((n,))],
        name=name)(*srcs)


def _scatter_pair(srcs, name):
    n = len(srcs)

    def body(*refs):
        src, out = refs[:n], refs[n:2 * n]
        send_sems, recv_sems = refs[2 * n:]
        x, y, c, _ = _mesh_place()
        copies = [pltpu.make_async_remote_copy(
            src_ref=src[t].at[:, 1 - c], dst_ref=out[t], send_sem=send_sems.at[t], recv_sem=recv_sems.at[t],
            device_id=(x, y, 1 - c), device_id_type=_MESH_ID) for t in range(n)]
        for cp in copies:
            cp.start()
        for cp in copies:
            cp.wait()

    return pl.pallas_call(
        body, in_specs=[_HBM] * n, out_specs=[_HBM] * n,
        out_shape=[jax.ShapeDtypeStruct((s.shape[0],) + s.shape[2:], s.dtype) for s in srcs],
        scratch_shapes=[pltpu.SemaphoreType.DMA((n,)), pltpu.SemaphoreType.DMA((n,))], name=name)(*srcs)


def _pair_add(src, recv, name):
    nchip, _, r, cdim = src.shape
    tr = _pick(r, 256, 2 * SUBLANES)
    core = lax.axis_index("c").astype(jnp.int32).reshape(1)

    def body(core_ref, s_ref, r_ref, o_ref):
        o_ref[...] = (s_ref[...].astype(F32) + r_ref[...].astype(F32)).astype(o_ref.dtype)

    grid_spec = pltpu.PrefetchScalarGridSpec(
        num_scalar_prefetch=1, grid=(nchip, r // tr),
        in_specs=[pl.BlockSpec((None, None, tr, cdim), lambda k, i, core_ref: (k, core_ref[0], i, 0)),
                  pl.BlockSpec((None, tr, cdim), lambda k, i, core_ref: (k, i, 0))],
        out_specs=pl.BlockSpec((None, tr, cdim), lambda k, i, core_ref: (k, i, 0)))
    return pl.pallas_call(body, grid_spec=grid_spec, out_shape=jax.ShapeDtypeStruct(recv.shape, recv.dtype),
                          compiler_params=_cparams("parallel", "parallel"), name=name)(core, src, recv)


def _scatter_chips(srcs, name):
    n = len(srcs)

    def body(*refs):
        src, out = refs[:n], refs[n:2 * n]
        send_sems, recv_sems, local_sems = refs[2 * n:]
        x, y, c, chips = _mesh_place()
        my_chip = 2 * x + y
        mine = [pltpu.make_async_copy(src[t].at[my_chip], out[t].at[my_chip], local_sems.at[t]) for t in range(n)]
        copies = [pltpu.make_async_remote_copy(
            src_ref=src[t].at[2 * chip[0] + chip[1]], dst_ref=out[t].at[my_chip],
            send_sem=send_sems.at[t, j], recv_sem=recv_sems.at[t, j], device_id=(*chip, c), device_id_type=_MESH_ID)
            for t in range(n) for j, chip in enumerate(chips)]
        for cp in mine + copies:
            cp.start()
        for cp in copies + mine:
            cp.wait()

    return pl.pallas_call(
        body, in_specs=[_HBM] * n, out_specs=[_HBM] * n,
        out_shape=[jax.ShapeDtypeStruct(s.shape, s.dtype) for s in srcs],
        scratch_shapes=[pltpu.SemaphoreType.DMA((n, 3)), pltpu.SemaphoreType.DMA((n, 3)), pltpu.SemaphoreType.DMA((n,))],
        name=name)(*srcs)


def _by_chip_and_core(g):
    return g.reshape(N_DEV // 2, 2, g.shape[0] // N_DEV, g.shape[1])


def _chip_sums(grads, tag, from_sibling=None):
    views = [_by_chip_and_core(g) for g in grads]
    if from_sibling is None:
        from_sibling = _scatter_pair(views, name="scatter_pair_" + tag)
    return [_pair_add(v, s, name="pair_add") for v, s in zip(views, from_sibling)]


def _pair_ride(grads):
    views = [_by_chip_and_core(g) for g in grads]

    def copies(in_refs, buf_refs, sems):
        x, y, c, _ = _mesh_place()
        return [pltpu.make_async_remote_copy(src_ref=src.at[:, 1 - c], dst_ref=dst, send_sem=sems.at[2 * t],
                                             recv_sem=sems.at[2 * t + 1], device_id=(x, y, 1 - c), device_id_type=_MESH_ID)
                for t, (src, dst) in enumerate(zip(in_refs, buf_refs))]

    def start(in_refs, buf_refs, sems):
        for cp in copies(in_refs, buf_refs, sems):
            cp.start()

    def finish(in_refs, buf_refs, sems):
        for cp in copies(in_refs, buf_refs, sems):
            cp.wait()

    created = tuple(jax.ShapeDtypeStruct((v.shape[0],) + v.shape[2:], v.dtype) for v in views)
    return _Ride(inputs=tuple(views), carried=(), created=created, n_sems=2 * len(views), start=start, finish=finish)


def _sum_chips(parts):
    return _rowmap(_sum_parts_fn, [], stacks=[parts], row_outs=[(parts.shape[2], F32)], tr=128, name="sum_chips")[0]


def _buffer_roles(kinds, bufs):
    carried = [k for k in kinds if k in bufs]
    return carried, [k for k in kinds if k not in bufs]


def _gather_ride(sends, forwards, bufs):
    carried, created = _buffer_roles(list(dict.fromkeys([s[0] for s in sends] + [f[0] for f in forwards])), bufs)
    shape_of = {s[0]: jax.ShapeDtypeStruct((N_DEV,) + s[1].shape, s[1].dtype) for s in sends}
    order = carried + created

    def copies(in_refs, buf_refs, sems):
        x, y, c, chips = _mesh_place()
        buf = dict(zip(order, buf_refs))
        out, s0 = [], 0
        for (kind, _, r0, nr), src in zip(sends, in_refs):
            mine, dst = src.at[pl.ds(r0, nr)], buf[kind].at[4 * x + 2 * y + c, pl.ds(r0, nr)]
            out.append(pltpu.make_async_copy(mine, dst, sems.at[s0 + 8]))
            for k, peer in enumerate([(x, y, 1 - c)] + [(*chip, c) for chip in chips]):
                out.append(pltpu.make_async_remote_copy(src_ref=mine, dst_ref=dst, send_sem=sems.at[s0 + k],
                                                        recv_sem=sems.at[s0 + 4 + k], device_id=peer, device_id_type=_MESH_ID))
            s0 += 9
        for kind, r0, nr in forwards:
            for j, chip in enumerate(chips):
                blk = buf[kind].at[4 * chip[0] + 2 * chip[1] + c, pl.ds(r0, nr)]
                out.append(pltpu.make_async_remote_copy(src_ref=blk, dst_ref=blk, send_sem=sems.at[s0 + j],
                                                        recv_sem=sems.at[s0 + 3 + j], device_id=(x, y, 1 - c),
                                                        device_id_type=_MESH_ID))
            s0 += 6
        return out

    def start(in_refs, buf_refs, sems):
        for cp in copies(in_refs, buf_refs, sems):
            cp.start()

    def finish(in_refs, buf_refs, sems):
        for cp in copies(in_refs, buf_refs, sems):
            cp.wait()

    ride = _Ride(inputs=tuple(s[1] for s in sends), carried=tuple(bufs[k] for k in carried),
                 created=tuple(shape_of[k] for k in created), n_sems=9 * len(sends) + 6 * len(forwards),
                 start=start, finish=finish)
    return ride, order


def _scatter_ride(pieces, bufs):
    carried, created = _buffer_roles(list(dict.fromkeys(p[0] for p in pieces)), bufs)
    shape_of = {p[0]: jax.ShapeDtypeStruct(p[1].shape, p[1].dtype) for p in pieces}
    order = carried + created

    def copies(in_refs, buf_refs, sems):
        x, y, c, chips = _mesh_place()
        buf = dict(zip(order, buf_refs))
        out, s0 = [], 0
        for (kind, _, r0, nr), src in zip(pieces, in_refs):
            dst = buf[kind].at[2 * x + y, pl.ds(r0, nr)]
            out.append(pltpu.make_async_copy(src.at[2 * x + y, pl.ds(r0, nr)], dst, sems.at[s0 + 6]))
            for j, chip in enumerate(chips):
                out.append(pltpu.make_async_remote_copy(
                    src_ref=src.at[2 * chip[0] + chip[1], pl.ds(r0, nr)], dst_ref=dst, send_sem=sems.at[s0 + j],
                    recv_sem=sems.at[s0 + 3 + j], device_id=(*chip, c), device_id_type=_MESH_ID))
            s0 += 7
        return out

    def start(in_refs, buf_refs, sems):
        for cp in copies(in_refs, buf_refs, sems):
            cp.start()

    def finish(in_refs, buf_refs, sems):
        for cp in copies(in_refs, buf_refs, sems):
            cp.wait()

    ride = _Ride(inputs=tuple(p[1] for p in pieces), carried=tuple(bufs[k] for k in carried),
                 created=tuple(shape_of[k] for k in created), n_sems=7 * len(pieces), start=start, finish=finish)
    return ride, order


GATHER_PLAN = (
    ("mm_in_qkv", (("w_in", 0, 3),)),
    ("mm_in_gl", (("w_in", 1, 3),)),
    ("attn_fwd0", (("w_pa", 0, 1), ("w_pb", 0, 1))),
    ("attn_fwd1", (("w_pc", 0, 1), ("w_glu", 0, 1))),
    ("attn_fwd2", (("w_o", 0, 1),)),
    ("s5_fwd", (("w_ffn_out", 0, 2),)),
    ("mm_o", (("w_ffn_out", 1, 2),)),
    ("mm_ffn_in", (("w_in", 2, 3), ("w_ffn_in", 0, 2))),
    ("mm_ffn_out", (("w_ffn_in", 1, 2),)),
    ("norm2", ()),
)
EARLY_KINDS = ("w_ffn_out", "w_ffn_in")
LATE_KINDS = tuple(n for n in SHARDED if n not in EARLY_KINDS)
SCATTER_EARLY_PLAN = (
    ("mm_in_dw_qkv", (("w_ffn_out", 0, 2),)),
    ("mm_in_dx_qkv", (("w_ffn_out", 1, 2),)),
    ("mm_in_dw_gl", (("w_ffn_in", 0, 4), ("w_ffn_in", 3, 4))),
    ("mm_in_dx_gl", (("w_ffn_in", 1, 4), ("w_ffn_in", 2, 4))),
)
SCATTER_PLAN = (
    ("mm_ffn_out_dw", (("w_in", 0, 3),)),
    ("mm_ffn_out_dx", (("w_in", 1, 3),)),
    ("mm_ffn_in_dw", (("w_in", 2, 3), ("w_o", 0, 1), ("w_pa", 0, 1), ("w_pb", 0, 1), ("w_pc", 0, 1), ("w_glu", 0, 1))),
)


def _row_part(rows, part, parts):
    assert rows % (parts * 2 * SUBLANES) == 0
    return part * (rows // parts), rows // parts


class _Carried:
    def __init__(self, plan, blocks, make_ride, forwards_too):
        self.plan, self.blocks, self.make_ride, self.forwards_too = dict(plan), blocks, make_ride, forwards_too
        self.bufs, self.to_forward, self.order = {}, [], []

    def ride(self, host):
        if self.blocks is None or host not in self.plan:
            self.order = []
            return None
        sends = [(k, self.blocks[k], *_row_part(self.blocks[k].shape[-2], part, parts)) for k, part, parts in self.plan[host]]
        if self.forwards_too:
            ride, self.order = self.make_ride(sends, self.to_forward, self.bufs)
            self.to_forward = [(k, r0, nr) for k, _, r0, nr in sends]
        else:
            ride, self.order = self.make_ride(sends, self.bufs)
        return ride

    def took(self, ride_outs):
        for k, buf in zip(self.order, ride_outs[0] if ride_outs else []):
            self.bufs[k] = buf


class _OwnScatter:
    def __init__(self, enabled):
        self.enabled = enabled

    def pair_ride(self, grads):
        return _pair_ride(list(grads.values())) if self.enabled else None

    def steps(self, grads, ride_outs):
        if not self.enabled:
            return _Carried(SCATTER_EARLY_PLAN, None, _scatter_ride, forwards_too=False)
        sums = _chip_sums(list(grads.values()), "early", from_sibling=ride_outs[0])
        return _Carried(SCATTER_EARLY_PLAN, dict(zip(grads, sums)), _scatter_ride, forwards_too=False)


class _PrevScatter:
    PAIR_HOST = "norm2_bwd"

    def __init__(self, grads):
        self.grads = grads
        self.inner = _Carried(SCATTER_PLAN, None, _scatter_ride, forwards_too=False)
        self.pair_pending = False

    @property
    def bufs(self):
        return self.inner.bufs

    def ride(self, host):
        if self.grads is not None and host == self.PAIR_HOST:
            self.pair_pending = True
            return _pair_ride(list(self.grads.values()))
        return self.inner.ride(host)

    def took(self, ride_outs):
        if not self.pair_pending:
            return self.inner.took(ride_outs)
        self.pair_pending = False
        sums = _chip_sums(list(self.grads.values()), "late", from_sibling=ride_outs[0])
        self.inner = _Carried(SCATTER_PLAN, dict(zip(self.grads, sums)), _scatter_ride, forwards_too=False)


def _hosted(comm, fn, *args, name, **kwargs):
    res, ride_outs = fn(*args, name=name, rides=[comm.ride(name)], **kwargs)
    comm.took(ride_outs)
    return res


def _small_sizes(shapes):
    return [int(np.prod(shapes[n])) for n in SMALL]


def _pack_small(vals):
    flat = jnp.concatenate([vals[n].reshape(-1).astype(F32) for n in SMALL])
    rows = -(-flat.shape[0] // (LANES * N_DEV * SUBLANES)) * (N_DEV * SUBLANES)
    return jnp.pad(flat, (0, rows * LANES - flat.shape[0])).reshape(rows, LANES)


def _unpack_small(packed, shapes):
    flat = packed.reshape(-1)
    out, off = {}, 0
    for n, size in zip(SMALL, _small_sizes(shapes)):
        out[n] = flat[off:off + size].reshape(shapes[n])
        off += size
    return out


def _row(v):
    return v.reshape(1, -1)


def _layer_params(l, full, small):
    o1, o2, o3 = 3 * QKV_WIDTH, 3 * QKV_WIDTH + 2 * WIDTH_B, 3 * QKV_WIDTH + 2 * WIDTH_B + WIDTH_C
    b_in = small["b_in"][l]
    p = {
        "in_pieces": (("qkv", 0, o1), ("zb", o1, o2 - o1), ("uc", o2, o3 - o2), ("gl", o3, b_in.shape[0] - o3)),
        "b_qkv": _row(b_in[:o1]), "b_zb": _row(b_in[o1:o2]), "b_uc": _row(b_in[o2:o3]), "b_gl": _row(b_in[o3:]),
        "sgu_ln_g": _row(small["sgu_ln_g"][l]), "sgu_ln_b": _row(small["sgu_ln_b"][l]),
        "w_s": small["w_s"][l], "b_s_t": small["b_s"][l].T,
        "lam_re": small["lam_re"][l][:, None, :], "lam_im": small["lam_im"][l][:, None, :],
        "log_dt": small["log_dt"][l][:, None, None],
        "b_re_t": small["b_re"][l].transpose(0, 2, 1), "b_im_t": small["b_im"][l].transpose(0, 2, 1),
        "c_re": small["c_re"][l], "c_im": small["c_im"][l],
        "d_skip": _row(small["d_skip"][l]), "b_glu": _row(small["b_glu"][l]),
        "ln1_g": _row(small["ln1_g"][l]), "ln1_b": _row(small["ln1_b"][l]),
        "ln2_g": _row(small["ln2_g"][l]), "ln2_b": _row(small["ln2_b"][l]),
    }
    for n in SHARDED:
        p[n] = full[n]
    return p


def _twice(fn):
    def both(*args):
        y = fn(*args)
        return y, y
    return both


def _layer_fwd(x, xb, p, biases, comm):
    t, d = x.shape
    r = {"x": x, "xb": xb}
    for piece, off, n in p["in_pieces"]:
        r[piece] = _hosted(comm, _mm, xb, p["w_in"], tb=True, b_off=off, n=n, bias=p["b_" + piece],
                           out_dtype=BF16 if piece in ("qkv", "gl") else F32, name="mm_in_" + piece)
    ol = []
    for g, dil in enumerate(ATT_DILATIONS):
        ol += list(_attn_fwd(r["qkv"], biases[g], g, dil, comm))
    r["ol"] = ol
    r["ya"], r["ya_b"] = _rowmap(_twice(_combine), ol, row_outs=[(WIDTH_A, F32), (WIDTH_A, BF16)], tr=512,
                                 name="attn_combine")
    r["yb"] = _gmlp_fwd(r["zb"], p["sgu_ln_g"], p["sgu_ln_b"], p["w_s"], p["b_s_t"])
    ab_re, ab_im, bb_re_t, bb_im_t = _s5_disc_fwd(p["lam_re"], p["lam_im"], p["log_dt"], p["b_re_t"], p["b_im_t"])
    r["a_re"], r["a_im"] = ab_re.reshape(_SCAN_ROWS, LANES), ab_im.reshape(_SCAN_ROWS, LANES)
    r["s5_mats"] = _block_diag([bb_re_t, bb_im_t, p["c_re"], p["c_im"]])
    r["xr"], r["xi"], r["ys"], r["ycp"] = _s5_fwd(r["uc"], r["s5_mats"], r["a_re"], r["a_im"], p["d_skip"], comm)
    r["glin"] = _mm(r["ycp"], p["w_glu"], bias=p["b_glu"], name="mm_glu")
    r["yc"] = _rowmap(_glu, [r["ycp"], r["glin"]], row_outs=[(WIDTH_C, BF16)], tr=512, name="glu")[0]
    r["pa"] = _mm(r["ya_b"], p["w_pa"], tb=True, out_dtype=BF16, name="mm_pa")
    r["pb"] = _mm(r["yb"], p["w_pb"], tb=True, out_dtype=BF16, name="mm_pb")
    r["pc"] = _mm(r["yc"], p["w_pc"], tb=True, out_dtype=BF16, name="mm_pc")
    r["merged"] = _rowmap(_merge, [r["gl"], r["pa"], r["pb"], r["pc"]], row_outs=[(d, BF16)], name="merge")[0]
    r["mo"] = _hosted(comm, _mm, r["merged"], p["w_o"], name="mm_o")
    r["xm"], r["xm_b"] = _rowmap(_twice(_post_norm), [x, r["mo"]], consts=[p["ln1_g"], p["ln1_b"]],
                                 row_outs=[(d, F32), (d, BF16)], name="norm1")
    r["gate"], r["up"], r["act"] = _hosted(comm, _mm_swiglu, r["xm_b"], p["w_ffn_in"], name="mm_ffn_in")
    r["f"] = _hosted(comm, _mm, r["act"], p["w_ffn_out"], name="mm_ffn_out")
    out, out_b = _hosted(comm, _rowmap, _twice(_post_norm), [r["xm"], r["f"]], consts=[p["ln2_g"], p["ln2_b"]],
                         row_outs=[(d, F32), (d, BF16)], name="norm2")
    return out, out_b, r


def _layer_bwd(dout, r, p, biases, consts, comm, own):
    t, d = dout.shape
    gw, gs = {}, {}
    ffw = 2 * r["gate"].shape[1]
    dxm, df, gs["ln2_g"], gs["ln2_b"] = _hosted(
        comm, _rowmap, _post_norm_bwd, [r["xm"], r["f"], dout], consts=[p["ln2_g"], p["ln2_b"]],
        row_outs=[(d, F32), (d, BF16)], red_outs=[(1, d)] * 2, name="norm2_bwd")
    gw["w_ffn_out"] = _hosted(comm, _mm, r["act"], df, ta=True, out_dtype=BF16, name="mm_ffn_out_dw")
    dact = _hosted(comm, _mm, df, p["w_ffn_out"], tb=True, out_dtype=BF16, name="mm_ffn_out_dx")
    dgu = _rowmap(_swiglu_bwd, [r["gate"], r["up"], dact], row_outs=[(ffw, BF16)], tr=128, name="swiglu_bwd")[0]
    gw["w_ffn_in"] = _hosted(comm, _mm, dgu, r["xm_b"], ta=True, out_dtype=BF16, name="mm_ffn_in_dw")
    early_grads = {n: gw[n] for n in EARLY_KINDS}
    dxm, from_sibling = _mm(dgu, p["w_ffn_in"], add=dxm, name="mm_ffn_in_dx", rides=[own.pair_ride(early_grads)])
    early = own.steps(early_grads, from_sibling)
    dx, dmo, gs["ln1_g"], gs["ln1_b"] = _rowmap(
        _post_norm_bwd, [r["x"], r["mo"], dxm], consts=[p["ln1_g"], p["ln1_b"]],
        row_outs=[(d, F32), (d, BF16)], red_outs=[(1, d)] * 2, name="norm1_bwd")
    gw["w_o"] = _hosted(comm, _mm, r["merged"], dmo, ta=True, out_dtype=BF16, name="mm_o_dw")
    dmerged = _hosted(comm, _mm, dmo, p["w_o"], tb=True, name="mm_o_dx")
    dgl, dpa, dpb, dpc, db_gl = _rowmap(
        _merge_bwd, [r["gl"], r["pa"], r["pb"], r["pc"], dmerged],
        row_outs=[(3 * d, BF16), (d, BF16), (d, BF16), (d, BF16)], red_outs=[(1, 3 * d)], tr=128, name="merge_bwd")
    gw["w_pa"] = _mm(dpa, r["ya_b"], ta=True, out_dtype=BF16, name="mm_pa_dw")
    gw["w_pb"] = _mm(dpb, r["yb"], ta=True, out_dtype=BF16, name="mm_pb_dw")
    gw["w_pc"] = _mm(dpc, r["yc"], ta=True, out_dtype=BF16, name="mm_pc_dw")
    dya = _mm(dpa, p["w_pa"], name="mm_pa_dx")
    dyb = _mm(dpb, p["w_pb"], name="mm_pb_dx")
    dyc = _mm(dpc, p["w_pc"], name="mm_pc_dx")
    dycp, dglin, gs["b_glu"] = _rowmap(_glu_bwd, [r["ycp"], r["glin"], dyc], row_outs=[(WIDTH_C, F32), (WIDTH_C, BF16)],
                                       red_outs=[(1, WIDTH_C)], tr=512, name="glu_bwd")
    gw["w_glu"] = _mm(r["ycp"], dglin, ta=True, out_dtype=BF16, name="mm_glu_dw")
    dycp = _mm(dglin, p["w_glu"], tb=True, add=dycp, name="mm_glu_dx")
    dys, duc, gs["d_skip"] = _rowmap(_s5_out_bwd, [r["ys"], r["uc"], dycp], consts=[p["d_skip"]],
                                     row_outs=[(WIDTH_C, F32)] * 2, red_outs=[(1, WIDTH_C)], tr=512, name="s5_out_act_bwd")
    duc, d_bmat_re, d_bmat_im, d_cmat_re, d_cmat_im, da_re, da_im = _s5_bwd(
        dys, duc, r["uc"], r["xr"], r["xi"], r["s5_mats"], r["a_re"], r["a_im"])
    d_bb_re_t, d_bb_im_t, gs["c_re"], gs["c_im"] = _block_diag_extract(
        [d_bmat_re, d_bmat_im, d_cmat_re, d_cmat_im], SSM_GROUP, SSM_STATE)
    cts = (da_re.reshape(N_GROUPS_C, 1, SSM_STATE), da_im.reshape(N_GROUPS_C, 1, SSM_STATE), d_bb_re_t, d_bb_im_t)
    d_lr, d_li, d_ldt, d_br_t, d_bi_t = _s5_disc_bwd(p["lam_re"], p["lam_im"], p["log_dt"], p["b_re_t"], p["b_im_t"], cts)
    gs["lam_re"], gs["lam_im"], gs["log_dt"] = d_lr[:, 0, :], d_li[:, 0, :], d_ldt[:, 0, 0]
    gs["b_re"], gs["b_im"] = d_br_t.transpose(0, 2, 1), d_bi_t.transpose(0, 2, 1)
    dzb, db_zb, gs["sgu_ln_g"], gs["sgu_ln_b"], gs["w_s"], dbs_t = _gmlp_bwd(
        r["zb"], dyb, p["sgu_ln_g"], p["sgu_ln_b"], p["w_s"], p["b_s_t"], consts["group_sel"])
    gs["b_s"] = dbs_t[:, :N_GROUPS_B].T
    do_corr = _rowmap(_combine_bwd, r["ol"] + [r["ya"], dya], consts=[consts["head_ones"]],
                      row_outs=[(WIDTH_A, F32)] * 6, tr=512, name="attn_combine_bwd")
    dq, dk, dv, dbias = [], [], [], []
    for g, dil in enumerate(ATT_DILATIONS):
        do_g, corr_g, lse_g = do_corr[g], do_corr[3 + g], r["ol"][2 * g + 1]
        dq_g, dk_g, dv_g, db_g = _attn_bwd(r["qkv"], biases[g], do_g, lse_g, corr_g, g, dil)
        dq.append(dq_g)
        dk.append(dk_g)
        dv.append(dv_g)
        dbias.append(db_g)
    def cast_colsum(*pieces):
        a = pieces[0] if len(pieces) == 1 else jnp.concatenate(pieces, axis=1)
        return a, jnp.sum(a, axis=0, keepdims=True)

    dqkv, db_qkv = _rowmap(cast_colsum, dq + dk + dv, row_outs=[(3 * QKV_WIDTH, BF16)],
                           red_outs=[(1, 3 * QKV_WIDTH)], tr=256, name="cast_colsum_qkv")
    duc, db_uc = _rowmap(cast_colsum, [duc], row_outs=[(WIDTH_C, BF16)], red_outs=[(1, WIDTH_C)], tr=512,
                         name="cast_colsum_uc")
    dpieces = {"qkv": dqkv, "zb": dzb, "uc": duc, "gl": dgl}
    rows_in = p["w_in"].shape[0]
    dw_in = None
    for piece, off, n in p["in_pieces"]:
        dw_in = _hosted(early, _mm, dpieces[piece], r["xb"], ta=True, out_dtype=BF16, into=(rows_in, off, dw_in),
                        name="mm_in_dw_" + piece)
        dx = _hosted(early, _mm, dpieces[piece], p["w_in"], b_off=off, add=dx, name="mm_in_dx_" + piece)
    gw["w_in"] = dw_in
    gs["b_in"] = jnp.concatenate([db_qkv, db_zb, db_uc, db_gl], axis=1)[0]
    for n in ("sgu_ln_g", "sgu_ln_b", "d_skip", "b_glu", "ln1_g", "ln1_b", "ln2_g", "ln2_b"):
        gs[n] = gs[n][0]
    return dx, gw, gs, dbias, early.bufs


def _cast_bf16(w):
    w2 = w.reshape(-1, w.shape[-1])
    out = _rowmap(lambda a: a, [w2], row_outs=[(w2.shape[1], BF16)], tr=512, name="cast_bf16")[0]
    return out.reshape(w.shape)


def _static_consts():
    head_ones = np.kron(np.eye(HEADS_PER_GROUP, dtype=np.float32), np.ones((HEAD_DIM, HEAD_DIM), np.float32))
    group_sel = np.zeros((WIDTH_B, LANES), np.float32)
    group_sel[np.arange(WIDTH_B), np.arange(WIDTH_B) // CHUNK] = 1.0
    return {"head_ones": jnp.asarray(head_ones), "group_sel": jnp.asarray(group_sel)}


def _step(x, tgt, w, m, v):
    shapes = {n: w[n].shape for n in WEIGHTS}
    consts = _static_consts()
    mine_bf = {n: _cast_bf16(w[n].transpose(0, 2, 1) if n in TRANSPOSED else w[n]) for n in SHARDED}
    small = {n: w[n] for n in SMALL}
    buckets = [jnp.asarray(_bucket_table(dil)) for dil in ATT_DILATIONS]
    biases = [_bias_fwd(w["rel_bias"], buckets[g], g) for g in range(len(ATT_DILATIONS))]
    params, saved = [], []
    h, hb = _rowmap(_twice(lambda a: a), [x], row_outs=[(x.shape[1], F32), (x.shape[1], BF16)], name="cast_x")
    gathered = dict(zip(SHARDED, _gather_layer([mine_bf[n] for n in SHARDED], 0, name="gather_layer0")))
    for l in range(DEPTH):
        p = _layer_params(l, {n: g.reshape(-1, g.shape[2]) for n, g in gathered.items()}, small)
        ahead = _Carried(GATHER_PLAN, {n: mine_bf[n][l + 1] for n in SHARDED} if l + 1 < DEPTH else None,
                         _gather_ride, forwards_too=True)
        h, hb, r = _layer_fwd(h, hb, p, biases, ahead)
        gathered = ahead.bufs
        params.append(p)
        saved.append(r)
    dy, loss_part = _rowmap(_loss_fn, [h, tgt], row_outs=[(h.shape[1], F32)], red_outs=[(1, LANES)], name="loss")
    loss = lax.psum(loss_part[0, 0], MESH_AXES)
    g_mine, gs_layers = {n: [None] * DEPTH for n in SHARDED}, [None] * DEPTH
    dbias_sum = None
    behind = _PrevScatter(None)
    for l in reversed(range(DEPTH)):
        dy, gw, gs_layers[l], dbias, early_parts = _layer_bwd(
            dy, saved[l], params[l], biases, consts, behind, _OwnScatter(enabled=True))
        saved[l] = None
        for n in EARLY_KINDS:
            g_mine[n][l] = _sum_chips(early_parts[n])
        if behind.grads is not None:
            for n in LATE_KINDS:
                g_mine[n][l + 1] = _sum_chips(behind.bufs[n])
        behind = _PrevScatter({n: gw[n] for n in LATE_KINDS})
        if l == 0:
            sums = _chip_sums(list(behind.grads.values()), "late")
            for n, parts in zip(LATE_KINDS, _scatter_chips(sums, name="scatter_chips_layer0")):
                g_mine[n][0] = _sum_chips(parts)
        if dbias_sum is None:
            dbias_sum = dbias
        else:
            dbias_sum = [_rowmap(lambda a, b: a + b, [a.reshape(-1, 2 * ATT_BLOCK), b.reshape(-1, 2 * ATT_BLOCK)],
                                 row_outs=[(2 * ATT_BLOCK, F32)], name="dbias_add")[0].reshape(a.shape)
                         for a, b in zip(dbias_sum, dbias)]
    drel = [_bias_bwd(dbias_sum[g], buckets[g], g) for g in range(len(ATT_DILATIONS))]
    drel = _rowmap(lambda a, b, c: a + b + c, drel, row_outs=[(LANES, F32)], name="drel_add")[0]
    grad_small_local = {n: jnp.stack([gs_layers[l][n] for l in range(DEPTH)]) for n in SMALL if n != "rel_bias"}
    grad_small_local["rel_bias"] = drel[:, :shapes["rel_bias"][1]]
    out_g, out_d, out_m, out_v = {}, {}, {}, {}
    for n in SHARDED:
        g = jnp.stack(g_mine[n])
        out_g[n] = g.transpose(0, 2, 1) if n in TRANSPOSED else g
        cols = shapes[n][-1]
        res = _rowmap(_adamw, [a.reshape(-1, cols) for a in (w[n], out_g[n], m[n], v[n])],
                      row_outs=[(cols, F32)] * 3, tr=128, name="adamw_" + n)
        out_d[n], out_m[n], out_v[n] = [a.reshape(shapes[n]) for a in res]
    packed = _pack_small(grad_small_local)
    rows = packed.shape[0] // N_DEV
    parts = _exchange(packed.reshape(N_DEV, rows, LANES), gather=False, name="scatter_small")
    mine = _rowmap(_sum_parts_fn, [], stacks=[parts], row_outs=[(LANES, F32)], name="sum_small")[0]
    g_small = _exchange(mine, gather=True, name="gather_small").reshape(-1, LANES)
    out_g.update(_unpack_small(g_small, {n: shapes[n] for n in SMALL}))
    for n in SMALL:
        cols = shapes[n][-1]
        res = _rowmap(_adamw, [a.reshape(-1, cols) for a in (w[n], out_g[n], m[n], v[n])],
                      row_outs=[(cols, F32)] * 3, name="adamw_" + n)
        out_d[n], out_m[n], out_v[n] = [a.reshape(shapes[n]) for a in res]
    return loss, dy, out_g, out_d, out_m, out_v


def kernel(x, w_in, b_in, rel_bias, sgu_ln_g, sgu_ln_b, w_s, b_s, lam_re, lam_im, log_dt, b_re, b_im, c_re, c_im, d_skip, w_glu, b_glu, w_pa, w_pb, w_pc, w_o, ln1_g, ln1_b, w_ffn_in, w_ffn_out, ln2_g, ln2_b, loss_target, m_w_in, m_b_in, m_rel_bias, m_sgu_ln_g, m_sgu_ln_b, m_w_s, m_b_s, m_lam_re, m_lam_im, m_log_dt, m_b_re, m_b_im, m_c_re, m_c_im, m_d_skip, m_w_glu, m_b_glu, m_w_pa, m_w_pb, m_w_pc, m_w_o, m_ln1_g, m_ln1_b, m_w_ffn_in, m_w_ffn_out, m_ln2_g, m_ln2_b, v_w_in, v_b_in, v_rel_bias, v_sgu_ln_g, v_sgu_ln_b, v_w_s, v_b_s, v_lam_re, v_lam_im, v_log_dt, v_b_re, v_b_im, v_c_re, v_c_im, v_d_skip, v_w_glu, v_b_glu, v_w_pa, v_w_pb, v_w_pc, v_w_o, v_ln1_g, v_ln1_b, v_w_ffn_in, v_w_ffn_out, v_ln2_g, v_ln2_b):
    args = dict(locals())
    w = {n: args[n] for n in WEIGHTS}
    m = {n: args["m_" + n] for n in WEIGHTS}
    v = {n: args["v_" + n] for n in WEIGHTS}
    loss, dx, g, d, nm, nv = _step(x[0], loss_target[0], w, m, v)
    return (loss, dx[None], *[g[n] for n in WEIGHTS], *[d[n] for n in WEIGHTS],
            *[nm[n] for n in WEIGHTS], *[nv[n] for n in WEIGHTS])
```

```python
import functools
import math
from typing import Callable, NamedTuple

import numpy as np
import jax
import jax.numpy as jnp
from jax import lax
from jax.experimental import pallas as pl
from jax.experimental.pallas import tpu as pltpu

F32 = jnp.float32
BF16 = jnp.bfloat16

MESH_AXES = ("x", "y", "c")
N_DEV = 8
DEPTH = 4

ATT_DILATIONS = (1, 4, 16)
ATT_STEPS = 128
HEADS_PER_GROUP = 8
HEAD_DIM = 64
QKV_WIDTH = 1536
WIDTH_A = HEADS_PER_GROUP * HEAD_DIM
ATT_BLOCK = 128
N_REL_BUCKETS = 32
REL_MAX_DIST = 2048
NEG_INF = -1e30
CHUNK = 128
WIDTH_B = 768
N_GROUPS_B = 6
WIDTH_C = 768
SSM_GROUP = 16
N_GROUPS_C = 48
SSM_STATE = 64
SSM_PACK = 8
N_SSM_BLOCKS = N_GROUPS_C // SSM_PACK
SSM_COLS = N_GROUPS_C * SSM_STATE
ALPHA = (2 * DEPTH) ** 0.25

ADAM_LR = 0.001
ADAM_B1 = 0.9
ADAM_B2 = 0.999
ADAM_EPS = 1e-08
ADAM_WD = 0.01
ADAM_STEP = 10

LANES = 128
SUBLANES = 8
VMEM_LIMIT = 48 * 1024 * 1024

SHARDED = ("w_in", "w_glu", "w_pa", "w_pb", "w_pc", "w_o", "w_ffn_in", "w_ffn_out")
TRANSPOSED = ("w_in", "w_pa", "w_pb", "w_pc", "w_ffn_in")
SMALL = ("b_in", "rel_bias", "sgu_ln_g", "sgu_ln_b", "w_s", "b_s", "lam_re", "lam_im", "log_dt",
         "b_re", "b_im", "c_re", "c_im", "d_skip", "b_glu", "ln1_g", "ln1_b", "ln2_g", "ln2_b")
WEIGHTS = ("w_in", "b_in", "rel_bias", "sgu_ln_g", "sgu_ln_b", "w_s", "b_s", "lam_re", "lam_im",
           "log_dt", "b_re", "b_im", "c_re", "c_im", "d_skip", "w_glu", "b_glu", "w_pa", "w_pb",
           "w_pc", "w_o", "ln1_g", "ln1_b", "w_ffn_in", "w_ffn_out", "ln2_g", "ln2_b")


def _pick(dim, target, mult):
    best = None
    for t in range(mult, min(dim, target) + 1, mult):
        if dim % t == 0:
            best = t
    return dim if best is None else best


def _cparams(*sem):
    return pltpu.CompilerParams(dimension_semantics=sem, vmem_limit_bytes=VMEM_LIMIT)


def _zero_map(ndim):
    return lambda *_: (0,) * ndim


_HBM = pl.BlockSpec(memory_space=pl.ANY)
_MESH_ID = pl.DeviceIdType.MESH


class _Ride(NamedTuple):
    inputs: tuple
    carried: tuple
    created: tuple
    n_sems: int
    start: Callable
    finish: Callable


def _pallas(body, *, grid, in_specs, out_specs, out_shape, args, scratch=(), semantics, rides=(), aliases=None, name):
    rides = [r for r in rides if r is not None]
    n_in, n_out, n_scr = len(args), len(out_shape), len(scratch)
    r_args, r_shapes, aliases, spans = [], [], dict(aliases or {}), []
    for r in rides:
        i0, o0 = len(r_args), len(r_shapes)
        r_args += [*r.inputs, *r.carried]
        for k, a in enumerate(r.carried):
            aliases[n_in + i0 + len(r.inputs) + k] = n_out + o0 + k
        r_shapes += [jax.ShapeDtypeStruct(a.shape, a.dtype) for a in r.carried] + list(r.created)
        spans.append((i0, len(r.inputs), o0, len(r.carried) + len(r.created)))

    def full_body(*refs):
        host_in, ride_in = refs[:n_in], refs[n_in:n_in + len(r_args)]
        p = n_in + len(r_args)
        host_out, ride_out = refs[p:p + n_out], refs[p + n_out:p + n_out + len(r_shapes)]
        p += n_out + len(r_shapes)
        host_scr, ride_sems = refs[p:p + n_scr], refs[p + n_scr:]
        ids = [pl.program_id(k) for k in range(len(grid))]
        first = functools.reduce(jnp.logical_and, [i == 0 for i in ids])
        last = functools.reduce(jnp.logical_and, [i == g - 1 for i, g in zip(ids, grid)])

        def each(method):
            for r, (i0, ni, o0, no), sems in zip(rides, spans, ride_sems):
                getattr(r, method)(ride_in[i0:i0 + ni], ride_out[o0:o0 + no], sems)

        if rides:
            pl.when(first)(lambda: each("start"))
        body(*host_in, *host_out, *host_scr)
        if rides:
            pl.when(last)(lambda: each("finish"))

    if rides:
        semantics = ("arbitrary",) * len(grid)
    outs = pl.pallas_call(
        full_body, grid=grid, in_specs=list(in_specs) + [_HBM] * len(r_args),
        out_specs=list(out_specs) + [_HBM] * len(r_shapes), out_shape=list(out_shape) + r_shapes,
        scratch_shapes=list(scratch) + [pltpu.SemaphoreType.DMA((r.n_sems,)) for r in rides],
        input_output_aliases=aliases, compiler_params=_cparams(*semantics), name=name)(*args, *r_args)
    return outs[:n_out], [outs[n_out + o0:n_out + o0 + no] for _, _, o0, no in spans]


def _rowmap(fn, rows, consts=(), stacks=(), row_outs=(), red_outs=(), tr=256, name=None, rides=None):
    t = rows[0].shape[0] if rows else stacks[0].shape[1]
    dtypes = [a.dtype for a in (*rows, *stacks)] + [dt for _, dt in row_outs]
    packed = any(jnp.dtype(dt).itemsize < 4 for dt in dtypes)
    tr = _pick(t, tr, 2 * SUBLANES if packed else SUBLANES)
    n_r, n_c, n_s, n_o = len(rows), len(consts), len(stacks), len(row_outs)

    def body(*refs):
        ins = [r[...] for r in refs[:n_r + n_c + n_s]]
        outs = refs[n_r + n_c + n_s:n_r + n_c + n_s + n_o]
        reds = refs[n_r + n_c + n_s + n_o:]
        res = fn(*ins)
        if not isinstance(res, (tuple, list)):
            res = (res,)
        for o, v in zip(outs, res[:n_o]):
            o[...] = v.astype(o.dtype)
        if reds:
            @pl.when(pl.program_id(0) == 0)
            def _():
                for r in reds:
                    r[...] = jnp.zeros_like(r)
            for r, v in zip(reds, res[n_o:]):
                r[...] += v

    in_specs = [pl.BlockSpec((tr, r.shape[1]), lambda i: (i, 0)) for r in rows]
    in_specs += [pl.BlockSpec(c.shape, _zero_map(c.ndim)) for c in consts]
    in_specs += [pl.BlockSpec((s.shape[0], tr, s.shape[2]), lambda i: (0, i, 0)) for s in stacks]
    out_specs = [pl.BlockSpec((tr, w), lambda i: (i, 0)) for w, _ in row_outs]
    out_specs += [pl.BlockSpec(s, _zero_map(len(s))) for s in red_outs]
    out_shape = [jax.ShapeDtypeStruct((t, w), dt) for w, dt in row_outs]
    out_shape += [jax.ShapeDtypeStruct(s, F32) for s in red_outs]
    outs, ride_outs = _pallas(body, grid=(t // tr,), in_specs=in_specs, out_specs=out_specs, out_shape=out_shape,
                              args=[*rows, *consts, *stacks], semantics=("arbitrary",), rides=rides or (), name=name)
    return outs if rides is None else (outs, ride_outs)


MM_VMEM_BUDGET = 36 * 1024 * 1024


def _divisors(dim, mult, must_divide=0):
    out = [t for t in range(dim, 0, -mult) if t % mult == 0 and dim % t == 0 and must_divide % t == 0]
    return out or [dim]


def _mm_tiles(m, n, k, a_bytes, b_bytes, out_bytes, extra_bytes, ta, b_off_n, b_off_k, out_off, tm, tn):
    tms = _divisors(m, LANES if ta else SUBLANES, out_off)
    tm = next((t for t in tms if t <= tm), tms[-1])
    tns = [t for t in _divisors(n, LANES, b_off_n) if t <= tn] or [_divisors(n, LANES, b_off_n)[-1]]
    for tn_ in tns:
        for tk in _divisors(k, LANES, b_off_k):
            acc = 0 if tk == k else tm * tn_ * 4
            need = 2 * (tm * tk * a_bytes + tk * tn_ * b_bytes + tm * tn_ * (out_bytes + extra_bytes)) + acc
            if need <= MM_VMEM_BUDGET:
                return tm, tn_, tk
    return tm, tns[-1], _divisors(k, LANES, b_off_k)[-1]


def _mm(a, b, *, ta=False, tb=False, bias=None, add=None, out_dtype=F32, b_off=0, n=None, tm=1024, tn=1024, name=None,
        rides=None, into=None):
    k, m = a.shape if ta else a.shape[::-1]
    if tb:
        n = b.shape[0] if n is None else n
        assert b.shape[1] == k and b_off + n <= b.shape[0]
    else:
        n = b.shape[1]
        assert b_off + k <= b.shape[0]
    out_rows, out_off, out_buf = (m, 0, None) if into is None else into
    extra = 4 if add is not None else 0
    tm, tn, tk = _mm_tiles(m, n, k, a.dtype.itemsize, b.dtype.itemsize, jnp.dtype(out_dtype).itemsize, extra, ta,
                           b_off if tb else 0, 0 if tb else b_off, out_off, tm, tn)
    nk = k // tk
    off_n, off_k = (b_off // tn, 0) if tb else (0, b_off // tk)
    off_m = out_off // tm
    dims = (((0 if ta else 1,), (1 if tb else 0,)), ((), ()))

    def body(*refs):
        a_ref, b_ref = refs[0], refs[1]
        rest = list(refs[2:])
        bias_ref = rest.pop(0) if bias is not None else None
        add_ref = rest.pop(0) if add is not None else None
        o_ref = rest.pop(0)
        part = lax.dot_general(a_ref[...].astype(BF16), b_ref[...].astype(BF16), dims, preferred_element_type=F32)

        def finish(r):
            if bias_ref is not None:
                r = r + bias_ref[...]
            if add_ref is not None:
                r = r + add_ref[...]
            o_ref[...] = r.astype(o_ref.dtype)

        if nk == 1:
            finish(part)
        else:
            acc_ref = rest.pop(0)
            kk = pl.program_id(2)

            @pl.when(kk == 0)
            def _():
                acc_ref[...] = part

            @pl.when(kk > 0)
            def _():
                acc_ref[...] += part

            @pl.when(kk == nk - 1)
            def _():
                finish(acc_ref[...])

    a_spec = pl.BlockSpec((tk, tm), lambda i, j, q: (q, i)) if ta else pl.BlockSpec((tm, tk), lambda i, j, q: (i, q))
    if tb:
        b_spec = pl.BlockSpec((tn, tk), lambda i, j, q: (j + off_n, q))
    else:
        b_spec = pl.BlockSpec((tk, tn), lambda i, j, q: (q + off_k, j))
    in_specs, args = [a_spec, b_spec], [a, b]
    if bias is not None:
        in_specs.append(pl.BlockSpec((1, tn), lambda i, j, q: (0, j)))
        args.append(bias)
    if add is not None:
        in_specs.append(pl.BlockSpec((tm, tn), lambda i, j, q: (i, j)))
        args.append(add)
    aliases = {}
    if out_buf is not None:
        assert out_buf.shape == (out_rows, n) and out_buf.dtype == jnp.dtype(out_dtype)
        in_specs.append(_HBM)
        args.append(out_buf)
        aliases = {len(args) - 1: 0}

    def body_in_place(*refs):
        body(*refs[:len(args) - 1], *refs[len(args):])

    outs, ride_outs = _pallas(
        body if out_buf is None else body_in_place, grid=(m // tm, n // tn, nk), in_specs=in_specs,
        out_specs=[pl.BlockSpec((tm, tn), lambda i, j, q: (i + off_m, j))],
        out_shape=[jax.ShapeDtypeStruct((out_rows, n), out_dtype)], args=args,
        scratch=[] if nk == 1 else [pltpu.VMEM((tm, tn), F32)],
        semantics=("parallel", "parallel", "arbitrary"), rides=rides or (), aliases=aliases, name=name)
    return outs[0] if rides is None else (outs[0], ride_outs)


def _mm_swiglu(a, w_t, *, name, rides=None):
    m, k = a.shape
    f = w_t.shape[0] // 2
    tm, tn, tk = _mm_tiles(m, f, k, a.dtype.itemsize, 2 * w_t.dtype.itemsize, 3 * 2, 0, False, 0, 0, 0, 1024, 512)
    assert tk == k, "the fused activation needs the whole contraction in one block"

    def body(a_ref, g_ref, u_ref, gate_ref, up_ref, act_ref):
        av = a_ref[...].astype(BF16)
        gate = lax.dot_general(av, g_ref[...].astype(BF16), _NT, preferred_element_type=F32)
        up = lax.dot_general(av, u_ref[...].astype(BF16), _NT, preferred_element_type=F32)
        gate_ref[...] = gate.astype(gate_ref.dtype)
        up_ref[...] = up.astype(up_ref.dtype)
        act_ref[...] = _swiglu2(gate, up).astype(act_ref.dtype)

    out_spec = pl.BlockSpec((tm, tn), lambda i, j: (i, j))
    outs, ride_outs = _pallas(
        body, grid=(m // tm, f // tn),
        in_specs=[pl.BlockSpec((tm, k), lambda i, j: (i, 0)), pl.BlockSpec((tn, k), lambda i, j: (j, 0)),
                  pl.BlockSpec((tn, k), lambda i, j: (j + f // tn, 0))],
        out_specs=[out_spec] * 3, out_shape=[jax.ShapeDtypeStruct((m, f), BF16)] * 3, args=[a, w_t, w_t],
        semantics=("parallel", "parallel"), rides=rides or (), name=name)
    return outs if rides is None else (outs, ride_outs)


def _ln(x, g, b, eps=1e-5):
    mu = jnp.mean(x, axis=-1, keepdims=True)
    var = jnp.mean(jnp.square(x - mu), axis=-1, keepdims=True)
    return (x - mu) * lax.rsqrt(var + eps) * g + b


def _post_norm(x, f, g, b):
    return _ln(ALPHA * x + f, g, b)


def _post_norm_bwd(x, f, dy, g, b):
    _, vjp = jax.vjp(_post_norm, x, f, g, b)
    return vjp(dy)


def _merge3(g0, g1, g2, pa, pb, pc):
    return jax.nn.sigmoid(g0) * pa + jax.nn.sigmoid(g1) * pb + jax.nn.sigmoid(g2) * pc


def _merge_args(gl, pa, pb, pc):
    d = pa.shape[1]
    return [a.astype(F32) for a in (gl[:, :d], gl[:, d:2 * d], gl[:, 2 * d:], pa, pb, pc)]


def _merge(gl, pa, pb, pc):
    return _merge3(*_merge_args(gl, pa, pb, pc))


def _merge_bwd(gl, pa, pb, pc, dm):
    _, vjp = jax.vjp(_merge3, *_merge_args(gl, pa, pb, pc))
    d0, d1, d2, dpa, dpb, dpc = vjp(dm)
    dgl = jnp.concatenate([d0, d1, d2], axis=1)
    return dgl, dpa, dpb, dpc, jnp.sum(dgl, axis=0, keepdims=True)


def _swiglu2(gate, up):
    return jax.nn.silu(gate) * up


def _swiglu_bwd(gate, up, dact):
    _, vjp = jax.vjp(_swiglu2, gate.astype(F32), up.astype(F32))
    dg, du = vjp(dact.astype(F32))
    return jnp.concatenate([dg, du], axis=1)


def _glu(ycp, lin):
    return ycp * jax.nn.sigmoid(lin)


def _glu_bwd(ycp, lin, dyc):
    _, vjp = jax.vjp(_glu, ycp, lin)
    dycp, dlin = vjp(dyc)
    return dycp, dlin, jnp.sum(dlin, axis=0, keepdims=True)


def _s5_out_bwd(ys, uc, dycp, dskip):
    _, vjp = jax.vjp(jax.nn.gelu, ys)
    dys = vjp(dycp)[0]
    return dys, dys * dskip, jnp.sum(dys * uc, axis=0, keepdims=True)


def _combine(o0, l0, o1, l1, o2, l2):
    m = jnp.maximum(jnp.maximum(l0, l1), l2)
    e0, e1, e2 = jnp.exp(l0 - m), jnp.exp(l1 - m), jnp.exp(l2 - m)
    s = e0 + e1 + e2
    return (e0 / s) * o0 + (e1 / s) * o1 + (e2 / s) * o2


def _combine_bwd(o0, l0, o1, l1, o2, l2, ya, dya, head_ones):
    m = jnp.maximum(jnp.maximum(l0, l1), l2)
    e0, e1, e2 = jnp.exp(l0 - m), jnp.exp(l1 - m), jnp.exp(l2 - m)
    s = e0 + e1 + e2
    dot_ya = jnp.dot(dya * ya, head_ones, precision=lax.Precision.HIGHEST, preferred_element_type=F32)
    w0, w1, w2 = e0 / s, e1 / s, e2 / s
    return w0 * dya, w1 * dya, w2 * dya, -w0 * dot_ya, -w1 * dot_ya, -w2 * dot_ya


def _loss_fn(y, tgt):
    err = y - tgt
    part = jnp.sum(jnp.sum(jnp.square(err), axis=1, keepdims=True), axis=0, keepdims=True) * (0.5 / y.shape[1])
    return err * (1.0 / y.shape[1]), jnp.broadcast_to(part, (1, LANES))


def _adamw(w, g, m, v):
    m = ADAM_B1 * m + (1.0 - ADAM_B1) * g
    v = ADAM_B2 * v + (1.0 - ADAM_B2) * jnp.square(g)
    m_hat = m / (1.0 - ADAM_B1 ** ADAM_STEP)
    v_hat = v / (1.0 - ADAM_B2 ** ADAM_STEP)
    delta = -ADAM_LR * (m_hat / (jnp.sqrt(v_hat) + ADAM_EPS) + ADAM_WD * w)
    return delta, m, v


def _sum_parts_fn(parts):
    g = parts[0].astype(F32)
    for j in range(1, parts.shape[0]):
        g = g + parts[j].astype(F32)
    return g


def _t5_bucket(dist):
    max_exact = N_REL_BUCKETS // 2
    d = np.maximum(dist, 1).astype(np.float32)
    scale = (N_REL_BUCKETS - max_exact) / math.log(REL_MAX_DIST / max_exact)
    large = max_exact + (np.log(d / max_exact) * scale).astype(np.int32)
    large = np.minimum(large, N_REL_BUCKETS - 1)
    return np.where(dist < max_exact, dist, large).astype(np.int32)


def _bucket_table(dilation):
    i = np.arange(ATT_BLOCK)[:, None]
    kk = np.arange(2 * ATT_BLOCK)[None, :]
    steps = ATT_BLOCK + i - kk
    return _t5_bucket(np.maximum(steps, 0) * dilation)


def _bias_fwd(rel_bias, buckets, g):
    def body(rel_ref, bk_ref, o_ref):
        bk = bk_ref[...]
        for h in range(HEADS_PER_GROUP):
            acc = jnp.zeros(bk.shape, F32)
            for b in range(N_REL_BUCKETS):
                acc = jnp.where(bk == b, rel_ref[b, g * HEADS_PER_GROUP + h], acc)
            o_ref[h] = acc

    return pl.pallas_call(
        body, in_specs=[pl.BlockSpec(memory_space=pltpu.SMEM), pl.BlockSpec(memory_space=pltpu.VMEM)],
        out_specs=pl.BlockSpec(memory_space=pltpu.VMEM),
        out_shape=jax.ShapeDtypeStruct((HEADS_PER_GROUP, ATT_BLOCK, 2 * ATT_BLOCK), F32),
        name=f"rel_bias_fwd{g}")(rel_bias, buckets)


def _bias_bwd(dbias, buckets, g):
    def body(db_ref, bk_ref, o_ref):
        bk = bk_ref[...]
        row = lax.broadcasted_iota(jnp.int32, (N_REL_BUCKETS, LANES), 0)
        col = lax.broadcasted_iota(jnp.int32, (N_REL_BUCKETS, LANES), 1)
        acc = jnp.zeros((N_REL_BUCKETS, LANES), F32)
        for h in range(HEADS_PER_GROUP):
            d = db_ref[h]
            for b in range(N_REL_BUCKETS):
                s = jnp.sum(jnp.sum(jnp.where(bk == b, d, 0.0), axis=1, keepdims=True), axis=0, keepdims=True)
                acc = acc + jnp.where((row == b) & (col == g * HEADS_PER_GROUP + h), s, 0.0)
        o_ref[...] = acc

    return pl.pallas_call(
        body, in_specs=[pl.BlockSpec(memory_space=pltpu.VMEM), pl.BlockSpec(memory_space=pltpu.VMEM)],
        out_specs=pl.BlockSpec(memory_space=pltpu.VMEM),
        out_shape=jax.ShapeDtypeStruct((N_REL_BUCKETS, LANES), F32), name=f"rel_bias_bwd{g}")(dbias, buckets)


_NT = (((1,), (1,)), ((), ()))
_TN = (((0,), (0,)), ((), ()))
_QKV_BLOCKS = 3 * QKV_WIDTH // WIDTH_A


def _band_mask(n_is_first):
    i = lax.broadcasted_iota(jnp.int32, (ATT_BLOCK, 2 * ATT_BLOCK), 0)
    kk = lax.broadcasted_iota(jnp.int32, (ATT_BLOCK, 2 * ATT_BLOCK), 1)
    return (kk >= i) & (kk <= i + ATT_STEPS) & ((kk >= ATT_BLOCK) | jnp.logical_not(n_is_first))


def _head(ref, h):
    return ref[:, h * HEAD_DIM:(h + 1) * HEAD_DIM]


def _attn_specs(g, d):
    blk = (ATT_BLOCK, WIDTH_A)
    q = pl.BlockSpec(blk, lambda c, n: (n, c * _QKV_BLOCKS + g))
    kp = pl.BlockSpec(blk, lambda c, n: (jnp.maximum(n - 1, 0), c * _QKV_BLOCKS + 3 + g))
    kc = pl.BlockSpec(blk, lambda c, n: (n, c * _QKV_BLOCKS + 3 + g))
    vp = pl.BlockSpec(blk, lambda c, n: (jnp.maximum(n - 1, 0), c * _QKV_BLOCKS + 6 + g))
    vc = pl.BlockSpec(blk, lambda c, n: (n, c * _QKV_BLOCKS + 6 + g))
    return [q, kp, kc, vp, vc]


def _attn_fwd(qkv, bias, g, d, comm):
    t = qkv.shape[0]
    lq = t // d
    nb = lq // ATT_BLOCK
    scale = HEAD_DIM ** -0.5

    def body(q_ref, kp_ref, kc_ref, vp_ref, vc_ref, b_ref, o_ref, l_ref):
        mask = _band_mask(pl.program_id(1) == 0)
        for h in range(HEADS_PER_GROUP):
            qh = _head(q_ref, h).astype(BF16)
            kh = jnp.concatenate([_head(kp_ref, h), _head(kc_ref, h)], axis=0).astype(BF16)
            vh = jnp.concatenate([_head(vp_ref, h), _head(vc_ref, h)], axis=0).astype(BF16)
            s = lax.dot_general(qh, kh, _NT, preferred_element_type=F32) * scale + b_ref[h]
            s = jnp.where(mask, s, NEG_INF)
            m = jnp.max(s, axis=1, keepdims=True)
            p = jnp.exp(s - m)
            den = jnp.sum(p, axis=1, keepdims=True)
            o = jnp.dot(p.astype(BF16), vh, preferred_element_type=F32) / den
            o_ref[:, h * HEAD_DIM:(h + 1) * HEAD_DIM] = o
            l_ref[:, h * HEAD_DIM:(h + 1) * HEAD_DIM] = jnp.broadcast_to(m + jnp.log(den), (ATT_BLOCK, HEAD_DIM))

    out_spec = pl.BlockSpec((ATT_BLOCK, WIDTH_A), lambda c, n: (n, c))
    (o, lse), ride_outs = _pallas(
        body, grid=(d, nb),
        in_specs=_attn_specs(g, d) + [pl.BlockSpec(bias.shape, _zero_map(3))],
        out_specs=[out_spec, out_spec],
        out_shape=[jax.ShapeDtypeStruct((lq, d * WIDTH_A), F32)] * 2,
        args=[*([qkv.reshape(lq, d * 3 * QKV_WIDTH)] * 5), bias],
        semantics=("parallel", "parallel"), rides=[comm.ride(f"attn_fwd{g}")], name=f"attn_fwd{g}")
    comm.took(ride_outs)
    return o.reshape(t, WIDTH_A), lse.reshape(t, WIDTH_A)


def _attn_bwd(qkv, bias, do, lse, corr, g, d):
    t = qkv.shape[0]
    lq = t // d
    nb = lq // ATT_BLOCK
    scale = HEAD_DIM ** -0.5

    def body(k_ref, v_ref, q0_ref, q1_ref, do0_ref, do1_ref, l0_ref, l1_ref, c0_ref, c1_ref, b_ref,
             dq_ref, dk_ref, dv_ref, db_ref, dq_prev):
        c, j = pl.program_id(0), pl.program_id(1)

        @pl.when((c == 0) & (j == 0))
        def _():
            db_ref[...] = jnp.zeros_like(db_ref)

        @pl.when(j == 0)
        def _():
            dq_prev[...] = jnp.zeros_like(dq_prev)

        i = lax.broadcasted_iota(jnp.int32, (ATT_BLOCK, ATT_BLOCK), 0)
        kk = lax.broadcasted_iota(jnp.int32, (ATT_BLOCK, ATT_BLOCK), 1)
        mask0 = kk <= i
        mask1 = (kk >= i) & (j + 1 < nb)
        for h in range(HEADS_PER_GROUP):
            kh = _head(k_ref, h).astype(BF16)
            vh = _head(v_ref, h).astype(BF16)
            cols = slice(h * HEAD_DIM, (h + 1) * HEAD_DIM)
            dk = jnp.zeros((ATT_BLOCK, HEAD_DIM), F32)
            dv = jnp.zeros((ATT_BLOCK, HEAD_DIM), F32)
            dq_parts = []
            parts = ((q0_ref, do0_ref, l0_ref, c0_ref, mask0, ATT_BLOCK), (q1_ref, do1_ref, l1_ref, c1_ref, mask1, 0))
            for q_ref, do_ref, l_ref, c_ref, mask, off in parts:
                qh = _head(q_ref, h).astype(BF16)
                doh = _head(do_ref, h).astype(BF16)
                s = lax.dot_general(qh, kh, _NT, preferred_element_type=F32) * scale + b_ref[h, :, off:off + ATT_BLOCK]
                s = jnp.where(mask, s, NEG_INF)
                p = jnp.exp(s - l_ref[:, h * HEAD_DIM:h * HEAD_DIM + 1])
                dp = lax.dot_general(doh, vh, _NT, preferred_element_type=F32)
                ds = p * (dp + c_ref[:, h * HEAD_DIM:h * HEAD_DIM + 1])
                dsb = ds.astype(BF16)
                dv = dv + lax.dot_general(p.astype(BF16), doh, _TN, preferred_element_type=F32)
                dk = dk + lax.dot_general(dsb, qh, _TN, preferred_element_type=F32)
                dq_parts.append(jnp.dot(dsb, kh, preferred_element_type=F32))
                db_ref[h, :, off:off + ATT_BLOCK] += ds
            dk_ref[:, cols] = dk * scale
            dv_ref[:, cols] = dv
            dq_ref[:, cols] = (dq_prev[:, cols] + dq_parts[0]) * scale
            dq_prev[:, cols] = dq_parts[1]

    blk = (ATT_BLOCK, WIDTH_A)
    nxt = lambda n: jnp.minimum(n + 1, nb - 1)
    k_spec = pl.BlockSpec(blk, lambda c, n: (n, c * _QKV_BLOCKS + 3 + g))
    v_spec = pl.BlockSpec(blk, lambda c, n: (n, c * _QKV_BLOCKS + 6 + g))
    q0_spec = pl.BlockSpec(blk, lambda c, n: (n, c * _QKV_BLOCKS + g))
    q1_spec = pl.BlockSpec(blk, lambda c, n: (nxt(n), c * _QKV_BLOCKS + g))
    r0 = pl.BlockSpec(blk, lambda c, n: (n, c))
    r1 = pl.BlockSpec(blk, lambda c, n: (nxt(n), c))
    view = lambda a: a.reshape(lq, d * WIDTH_A)
    qv = qkv.reshape(lq, d * 3 * QKV_WIDTH)
    dq, dk, dv, dbias = pl.pallas_call(
        body, grid=(d, nb),
        in_specs=[k_spec, v_spec, q0_spec, q1_spec, r0, r1, r0, r1, r0, r1, pl.BlockSpec(bias.shape, _zero_map(3))],
        out_specs=[r0, r0, r0, pl.BlockSpec(bias.shape, _zero_map(3))],
        out_shape=[jax.ShapeDtypeStruct((lq, d * WIDTH_A), F32)] * 3 + [jax.ShapeDtypeStruct(bias.shape, F32)],
        scratch_shapes=[pltpu.VMEM(blk, F32)],
        compiler_params=_cparams("arbitrary", "arbitrary"), name=f"attn_bwd{g}",
    )(qv, qv, qv, qv, view(do), view(do), view(lse), view(lse), view(corr), view(corr), bias)
    return dq.reshape(t, WIDTH_A), dk.reshape(t, WIDTH_A), dv.reshape(t, WIDTH_A), dbias


def _tril_mask():
    r = lax.broadcasted_iota(jnp.int32, (CHUNK, CHUNK), 0)
    c = lax.broadcasted_iota(jnp.int32, (CHUNK, CHUNK), 1)
    return c <= r


def _gmlp_fwd(zb, ln_g, ln_b, w_s, b_s_t):
    t = zb.shape[0]
    tr = _pick(t, 2 * CHUNK, CHUNK)

    def body(z_ref, g_ref, b_ref, ws_ref, bs_ref, o_ref):
        tri = _tril_mask()
        z = jax.nn.gelu(z_ref[...])
        u = z[:, :WIDTH_B]
        vn = _ln(z[:, WIDTH_B:], g_ref[...], b_ref[...])
        for ch in range(tr // CHUNK):
            rows = slice(ch * CHUNK, (ch + 1) * CHUNK)
            for gi in range(N_GROUPS_B):
                cols = slice(gi * CHUNK, (gi + 1) * CHUNK)
                w = jnp.where(tri, ws_ref[gi], 0.0).astype(BF16)
                mixed = jnp.dot(w, vn[rows, cols].astype(BF16), preferred_element_type=F32) + bs_ref[:, gi:gi + 1]
                o_ref[rows, cols] = (u[rows, cols] * mixed).astype(o_ref.dtype)

    return pl.pallas_call(
        body, grid=(t // tr,),
        in_specs=[pl.BlockSpec((tr, 2 * WIDTH_B), lambda i: (i, 0)), pl.BlockSpec(ln_g.shape, _zero_map(2)),
                  pl.BlockSpec(ln_b.shape, _zero_map(2)), pl.BlockSpec(w_s.shape, _zero_map(3)),
                  pl.BlockSpec(b_s_t.shape, _zero_map(2))],
        out_specs=pl.BlockSpec((tr, WIDTH_B), lambda i: (i, 0)),
        out_shape=jax.ShapeDtypeStruct((t, WIDTH_B), BF16),
        compiler_params=_cparams("parallel"), name="gmlp_fwd")(zb, ln_g, ln_b, w_s, b_s_t)


def _gmlp_bwd(zb, dyb, ln_g, ln_b, w_s, b_s_t, group_sel):
    t = zb.shape[0]
    tr = _pick(t, 2 * CHUNK, CHUNK)

    def body(z_ref, dy_ref, g_ref, b_ref, ws_ref, bs_ref, sel_ref, dz_ref, dzs_ref, dg_ref, db_ref, dws_ref, dbs_ref,
             du_s, dvn_s, dm_s):
        @pl.when(pl.program_id(0) == 0)
        def _():
            dzs_ref[...] = jnp.zeros_like(dzs_ref)
            dg_ref[...] = jnp.zeros_like(dg_ref)
            db_ref[...] = jnp.zeros_like(db_ref)
            dws_ref[...] = jnp.zeros_like(dws_ref)
            dbs_ref[...] = jnp.zeros_like(dbs_ref)

        tri = _tril_mask()
        z, gelu_vjp = jax.vjp(jax.nn.gelu, z_ref[...])
        u = z[:, :WIDTH_B]
        vn, ln_vjp = jax.vjp(_ln, z[:, WIDTH_B:], g_ref[...], b_ref[...])
        dy = dy_ref[...]
        for ch in range(tr // CHUNK):
            rows = slice(ch * CHUNK, (ch + 1) * CHUNK)
            for gi in range(N_GROUPS_B):
                cols = slice(gi * CHUNK, (gi + 1) * CHUNK)
                w = jnp.where(tri, ws_ref[gi], 0.0).astype(BF16)
                vg = vn[rows, cols].astype(BF16)
                mixed = jnp.dot(w, vg, preferred_element_type=F32) + bs_ref[:, gi:gi + 1]
                dyg = dy[rows, cols]
                dm = dyg * u[rows, cols]
                dmb = dm.astype(BF16)
                du_s[rows, cols] = dyg * mixed
                dm_s[rows, cols] = dm
                dvn_s[rows, cols] = lax.dot_general(w, dmb, _TN, preferred_element_type=F32)
                dws_ref[gi] += jnp.where(tri, lax.dot_general(dmb, vg, _NT, preferred_element_type=F32), 0.0)
            dbs_ref[...] += jnp.dot(dm_s[rows, :], sel_ref[...], precision=lax.Precision.HIGHEST,
                                    preferred_element_type=F32)
        dv, dg, db = ln_vjp(dvn_s[...])
        dg_ref[...] += dg
        db_ref[...] += db
        dz = gelu_vjp(jnp.concatenate([du_s[...], dv], axis=1))[0]
        dz_ref[...] = dz.astype(dz_ref.dtype)
        dzs_ref[...] += jnp.sum(dz, axis=0, keepdims=True)

    full = lambda a: pl.BlockSpec(a.shape, _zero_map(a.ndim))
    return pl.pallas_call(
        body, grid=(t // tr,),
        in_specs=[pl.BlockSpec((tr, 2 * WIDTH_B), lambda i: (i, 0)), pl.BlockSpec((tr, WIDTH_B), lambda i: (i, 0)),
                  full(ln_g), full(ln_b), full(w_s), full(b_s_t), full(group_sel)],
        out_specs=[pl.BlockSpec((tr, 2 * WIDTH_B), lambda i: (i, 0)), pl.BlockSpec((1, 2 * WIDTH_B), _zero_map(2)),
                   full(ln_g), full(ln_b), full(w_s), pl.BlockSpec((CHUNK, LANES), _zero_map(2))],
        out_shape=[jax.ShapeDtypeStruct((t, 2 * WIDTH_B), BF16), jax.ShapeDtypeStruct((1, 2 * WIDTH_B), F32),
                   jax.ShapeDtypeStruct(ln_g.shape, F32),
                   jax.ShapeDtypeStruct(ln_b.shape, F32), jax.ShapeDtypeStruct(w_s.shape, F32),
                   jax.ShapeDtypeStruct((CHUNK, LANES), F32)],
        scratch_shapes=[pltpu.VMEM((tr, WIDTH_B), F32)] * 3,
        compiler_params=_cparams("arbitrary"), name="gmlp_bwd")(zb, dyb, ln_g, ln_b, w_s, b_s_t, group_sel)


def _s5_disc(lr, li, ldt, br_t, bi_t):
    dt = jnp.exp(ldt)
    mag = jnp.exp(lr * dt)
    ab_re = mag * jnp.cos(li * dt)
    ab_im = mag * jnp.sin(li * dt)
    nrm = lr * lr + li * li
    cr = ((ab_re - 1.0) * lr + ab_im * li) / nrm
    ci = (ab_im * lr - (ab_re - 1.0) * li) / nrm
    return ab_re, ab_im, cr * br_t - ci * bi_t, cr * bi_t + ci * br_t


def _vmem_call(fn, args, out_shape, name):
    def body(*refs):
        res = fn(*[r[...] for r in refs[:len(args)]])
        for o, v in zip(refs[len(args):], res):
            o[...] = v

    vm = pl.BlockSpec(memory_space=pltpu.VMEM)
    return pl.pallas_call(body, in_specs=[vm] * len(args), out_specs=[vm] * len(out_shape),
                          out_shape=out_shape, name=name)(*args)


def _s5_disc_fwd(lr, li, ldt, br_t, bi_t):
    s1 = jax.ShapeDtypeStruct(lr.shape, F32)
    s2 = jax.ShapeDtypeStruct(br_t.shape, F32)
    return _vmem_call(_s5_disc, [lr, li, ldt, br_t, bi_t], [s1, s1, s2, s2], "s5_disc_fwd")


def _s5_disc_bwd(lr, li, ldt, br_t, bi_t, cts):
    def fn(lr, li, ldt, br_t, bi_t, d0, d1, d2, d3):
        _, vjp = jax.vjp(_s5_disc, lr, li, ldt, br_t, bi_t)
        return vjp((d0, d1, d2, d3))

    shp = [jax.ShapeDtypeStruct(a.shape, F32) for a in (lr, li, ldt, br_t, bi_t)]
    return _vmem_call(fn, [lr, li, ldt, br_t, bi_t, *cts], shp, "s5_disc_bwd")


_SCAN_ROWS = SSM_COLS // LANES
_SCAN_CHUNK = 128
_SSM_IN = SSM_PACK * SSM_GROUP
_SSM_ST = SSM_PACK * SSM_STATE


def _packed_in(xb, m_ref):
    return jnp.concatenate([jnp.dot(xb[:, j * _SSM_IN:(j + 1) * _SSM_IN], m_ref[j], preferred_element_type=F32)
                            for j in range(N_SSM_BLOCKS)], axis=1)


def _packed_out(xb, m_ref):
    return jnp.concatenate([lax.dot_general(xb[:, j * _SSM_ST:(j + 1) * _SSM_ST], m_ref[j], _NT, preferred_element_type=F32)
                            for j in range(N_SSM_BLOCKS)], axis=1)


def _s5_fwd(uc, mats, are, aim, d_skip, comm):
    t = uc.shape[0]
    tc = _pick(t, _SCAN_CHUNK, SUBLANES)

    def body(u_ref, br_ref, bi_ref, cr_ref, ci_ref, ar_ref, ai_ref, d_ref, xr_ref, xi_ref, ys_ref, ycp_ref,
             sr, si, st_ref):
        @pl.when(pl.program_id(0) == 0)
        def _():
            st_ref[...] = jnp.zeros_like(st_ref)

        u = u_ref[...]
        ub = u.astype(BF16)
        sr[...] = _packed_in(ub, br_ref).reshape(tc, _SCAN_ROWS, LANES)
        si[...] = _packed_in(ub, bi_ref).reshape(tc, _SCAN_ROWS, LANES)
        ar, ai = ar_ref[...], ai_ref[...]

        def step(i, carry):
            xr, xi = carry
            nr = ar * xr - ai * xi + sr[i]
            ni = ar * xi + ai * xr + si[i]
            sr[i] = nr
            si[i] = ni
            return nr, ni

        xr, xi = lax.fori_loop(0, tc, step, (st_ref[0], st_ref[1]), unroll=8)
        st_ref[0] = xr
        st_ref[1] = xi
        x_re = sr[...].reshape(tc, SSM_COLS)
        x_im = si[...].reshape(tc, SSM_COLS)
        xr_ref[...] = x_re
        xi_ref[...] = x_im
        ys = _packed_out(x_re.astype(BF16), cr_ref) - _packed_out(x_im.astype(BF16), ci_ref) + d_ref[...] * u
        ys_ref[...] = ys
        ycp_ref[...] = jax.nn.gelu(ys)

    row = lambda w: pl.BlockSpec((tc, w), lambda i: (i, 0))
    mat = pl.BlockSpec(mats[0].shape, _zero_map(3))
    par = pl.BlockSpec((_SCAN_ROWS, LANES), _zero_map(2))
    wide, narrow = jax.ShapeDtypeStruct((t, SSM_COLS), F32), jax.ShapeDtypeStruct((t, WIDTH_C), F32)
    outs, ride_outs = _pallas(
        body, grid=(t // tc,), in_specs=[row(WIDTH_C), mat, mat, mat, mat, par, par, pl.BlockSpec(d_skip.shape, _zero_map(2))],
        out_specs=[row(SSM_COLS), row(SSM_COLS), row(WIDTH_C), row(WIDTH_C)], out_shape=[wide, wide, narrow, narrow],
        args=[uc, *mats, are, aim, d_skip],
        scratch=[pltpu.VMEM((tc, _SCAN_ROWS, LANES), F32)] * 2 + [pltpu.VMEM((2, _SCAN_ROWS, LANES), F32)],
        semantics=("arbitrary",), rides=[comm.ride("s5_fwd")], name="s5_fwd")
    comm.took(ride_outs)
    return outs


def _s5_bwd(dys, duc_skip, uc, xr, xi, mats, are, aim):
    t = dys.shape[0]
    tc = _pick(t, _SCAN_CHUNK, SUBLANES)
    nc = t // tc

    def body(dy_ref, ds_ref, u_ref, xr_ref, xi_ref, pr_ref, pi_ref, br_ref, bi_ref, cr_ref, ci_ref, ar_ref, ai_ref,
             du_ref, dbr_ref, dbi_ref, dcr_ref, dci_ref, dar_ref, dai_ref, gr, gi, x3r, x3i, st_ref):
        step_id = pl.program_id(0)

        @pl.when(step_id == 0)
        def _():
            st_ref[...] = jnp.zeros_like(st_ref)
            for ref in (dbr_ref, dbi_ref, dcr_ref, dci_ref, dar_ref, dai_ref):
                ref[...] = jnp.zeros_like(ref)

        dyb = dy_ref[...].astype(BF16)
        x_re, x_im = xr_ref[...], xi_ref[...]
        gr[...] = _packed_in(dyb, cr_ref).reshape(tc, _SCAN_ROWS, LANES)
        gi[...] = (-_packed_in(dyb, ci_ref)).reshape(tc, _SCAN_ROWS, LANES)
        x3r[...] = x_re.reshape(tc, _SCAN_ROWS, LANES)
        x3i[...] = x_im.reshape(tc, _SCAN_ROWS, LANES)
        ar, ai = ar_ref[...], ai_ref[...]

        def update(i, carry, pxr, pxi):
            g_r, g_i, dar, dai = carry
            ngr = gr[i] + ar * g_r + ai * g_i
            ngi = gi[i] - ai * g_r + ar * g_i
            gr[i] = ngr
            gi[i] = ngi
            return ngr, ngi, dar + ngr * pxr + ngi * pxi, dai - ngr * pxi + ngi * pxr

        def step(s, carry):
            i = tc - 1 - s
            return update(i, carry, x3r[i - 1], x3i[i - 1])

        zero = jnp.zeros((_SCAN_ROWS, LANES), F32)
        carry = lax.fori_loop(0, tc - 1, step, (st_ref[0], st_ref[1], zero, zero), unroll=8)
        has_prev = (step_id < nc - 1).astype(F32)
        last = SUBLANES - 1
        p_re = pr_ref[last:, :].reshape(1, _SCAN_ROWS, LANES)[0] * has_prev
        p_im = pi_ref[last:, :].reshape(1, _SCAN_ROWS, LANES)[0] * has_prev
        g_r, g_i, dar, dai = update(0, carry, p_re, p_im)
        st_ref[0] = g_r
        st_ref[1] = g_i
        dar_ref[...] += dar
        dai_ref[...] += dai

        g_re = gr[...].reshape(tc, SSM_COLS).astype(BF16)
        g_im = gi[...].reshape(tc, SSM_COLS).astype(BF16)
        du_ref[...] = ds_ref[...] + _packed_out(g_re, br_ref) + _packed_out(g_im, bi_ref)
        ub, xrb, xib = u_ref[...].astype(BF16), x_re.astype(BF16), x_im.astype(BF16)
        for j in range(N_SSM_BLOCKS):
            narrow, wide = slice(j * _SSM_IN, (j + 1) * _SSM_IN), slice(j * _SSM_ST, (j + 1) * _SSM_ST)
            dbr_ref[j] += lax.dot_general(ub[:, narrow], g_re[:, wide], _TN, preferred_element_type=F32)
            dbi_ref[j] += lax.dot_general(ub[:, narrow], g_im[:, wide], _TN, preferred_element_type=F32)
            dcr_ref[j] += lax.dot_general(dyb[:, narrow], xrb[:, wide], _TN, preferred_element_type=F32)
            dci_ref[j] -= lax.dot_general(dyb[:, narrow], xib[:, wide], _TN, preferred_element_type=F32)

    rev = lambda w: pl.BlockSpec((tc, w), lambda i: (nc - 1 - i, 0))
    prev = pl.BlockSpec((SUBLANES, SSM_COLS), lambda i: (jnp.maximum((nc - 1 - i) * (tc // SUBLANES) - 1, 0), 0))
    mat = pl.BlockSpec(mats[0].shape, _zero_map(3))
    par = pl.BlockSpec((_SCAN_ROWS, LANES), _zero_map(2))
    msh = jax.ShapeDtypeStruct(mats[0].shape, F32)
    psh = jax.ShapeDtypeStruct((_SCAN_ROWS, LANES), F32)
    return pl.pallas_call(
        body, grid=(nc,),
        in_specs=[rev(WIDTH_C), rev(WIDTH_C), rev(WIDTH_C), rev(SSM_COLS), rev(SSM_COLS), prev, prev,
                  mat, mat, mat, mat, par, par],
        out_specs=[rev(WIDTH_C), mat, mat, mat, mat, par, par],
        out_shape=[jax.ShapeDtypeStruct((t, WIDTH_C), F32), msh, msh, msh, msh, psh, psh],
        scratch_shapes=[pltpu.VMEM((tc, _SCAN_ROWS, LANES), F32)] * 4 + [pltpu.VMEM((2, _SCAN_ROWS, LANES), F32)],
        compiler_params=_cparams("arbitrary"), name="s5_bwd")(dys, duc_skip, uc, xr, xi, xr, xi, *mats, are, aim)


def _diag_blocks(a, b):
    return [(j, i, slice(i * a, (i + 1) * a), slice(i * b, (i + 1) * b))
            for j in range(N_SSM_BLOCKS) for i in range(SSM_PACK)]


def _block_diag(ms):
    _, a, b = ms[0].shape

    def body(*refs):
        for m_ref, o_ref in zip(refs[:len(ms)], refs[len(ms):]):
            o_ref[...] = jnp.zeros_like(o_ref)
            for j, i, rows, cols in _diag_blocks(a, b):
                o_ref[j, rows, cols] = m_ref[j * SSM_PACK + i].astype(o_ref.dtype)

    vm = pl.BlockSpec(memory_space=pltpu.VMEM)
    shape = jax.ShapeDtypeStruct((N_SSM_BLOCKS, SSM_PACK * a, SSM_PACK * b), BF16)
    return pl.pallas_call(body, in_specs=[vm] * len(ms), out_specs=[vm] * len(ms), out_shape=[shape] * len(ms),
                          name="s5_block_diag")(*ms)


def _block_diag_extract(ms, a, b):
    def body(*refs):
        for m_ref, o_ref in zip(refs[:len(ms)], refs[len(ms):]):
            for j, i, rows, cols in _diag_blocks(a, b):
                o_ref[j * SSM_PACK + i] = m_ref[j, rows, cols]

    vm = pl.BlockSpec(memory_space=pltpu.VMEM)
    shape = jax.ShapeDtypeStruct((N_GROUPS_C, a, b), F32)
    return pl.pallas_call(body, in_specs=[vm] * len(ms), out_specs=[vm] * len(ms), out_shape=[shape] * len(ms),
                          name="s5_block_diag_extract")(*ms)


def _exchange(src, *, gather, name):
    shape = src.shape if gather else src.shape[1:]

    def body(src_ref, out_ref, send_sems, recv_sems, local_sem):
        x, y, c = lax.axis_index("x"), lax.axis_index("y"), lax.axis_index("c")
        me = 4 * x + 2 * y + c
        copies = []
        for r in range(1, N_DEV):
            px = 1 - x if r & 4 else x
            py = 1 - y if r & 2 else y
            pc = 1 - c if r & 1 else c
            piece = src_ref if gather else src_ref.at[4 * px + 2 * py + pc]
            cp = pltpu.make_async_remote_copy(
                src_ref=piece, dst_ref=out_ref.at[me], send_sem=send_sems.at[r - 1], recv_sem=recv_sems.at[r - 1],
                device_id=(px, py, pc), device_id_type=pl.DeviceIdType.MESH)
            cp.start()
            copies.append(cp)
        mine = pltpu.make_async_copy(src_ref if gather else src_ref.at[me], out_ref.at[me], local_sem)
        mine.start()
        for cp in copies:
            cp.wait()
        mine.wait()

    hbm = pl.BlockSpec(memory_space=pl.ANY)
    return pl.pallas_call(
        body, in_specs=[hbm], out_specs=hbm, out_shape=jax.ShapeDtypeStruct((N_DEV,) + tuple(shape), src.dtype),
        scratch_shapes=[pltpu.SemaphoreType.DMA((N_DEV - 1,)), pltpu.SemaphoreType.DMA((N_DEV - 1,)),
                        pltpu.SemaphoreType.DMA(())],
        name=name)(src)


def _mesh_place():
    x, y, c = lax.axis_index("x"), lax.axis_index("y"), lax.axis_index("c")
    other_chips = [(1 - x, y), (x, 1 - y), (1 - x, 1 - y)]
    return x, y, c, other_chips


def _gather_layer(srcs, layer, name):
    n = len(srcs)

    def body(*refs):
        src = [r.at[layer] for r in refs[:n]]
        out = refs[n:2 * n]
        send_sems, recv_sems, local_sems = refs[2 * n:]
        x, y, c, chips = _mesh_place()
        me, sibling = (x, y, c), (x, y, 1 - c)

        def copy(t, k, block, to, from_src=False):
            slot = 4 * block[0] + 2 * block[1] + block[2]
            return pltpu.make_async_remote_copy(
                src_ref=src[t] if from_src else out[t].at[slot], dst_ref=out[t].at[slot],
                send_sem=send_sems.at[t, k], recv_sem=recv_sems.at[t, k], device_id=to, device_id_type=_MESH_ID)

        mine = [pltpu.make_async_copy(src[t], out[t].at[4 * x + 2 * y + c], local_sems.at[t]) for t in range(n)]
        for cp in mine:
            cp.start()
        first = []
        for t in range(n):
            first.append(copy(t, 0, me, sibling, True))
            first += [copy(t, 1 + j, me, (*chip, c), True) for j, chip in enumerate(chips)]
        for cp in first:
            cp.start()
        passed = []
        for j, chip in enumerate(chips):
            for t in range(n):
                copy(t, 1 + j, (*chip, c), me).wait_recv()
                fwd = copy(t, 4 + j, (*chip, c), sibling)
                fwd.start()
                passed.append(fwd)
        for t in range(n):
            copy(t, 0, sibling, me).wait_recv()
            for j, chip in enumerate(chips):
                copy(t, 4 + j, (*chip, 1 - c), me).wait_recv()
        for cp in first + passed:
            cp.wait_send()
        for cp in mine:
            cp.wait()

    return pl.pallas_call(
        body, in_specs=[_HBM] * n, out_specs=[_HBM] * n,
        out_shape=[jax.ShapeDtypeStruct((N_DEV,) + s.shape[1:], s.dtype) for s in srcs],
        scratch_shapes=[pltpu.SemaphoreType.DMA((n, N_DEV - 1)), pltpu.SemaphoreType.DMA((n, N_DEV - 1)),
                        pltpu.SemaphoreType.DMA((n,))],
        name=name)(*srcs)


def _scatter_pair(srcs, name):
    n = len(srcs)

    def body(*refs):
        src, out = refs[:n], refs[n:2 * n]
        send_sems, recv_sems = refs[2 * n:]
        x, y, c, _ = _mesh_place()
        copies = [pltpu.make_async_remote_copy(
            src_ref=src[t].at[:, 1 - c], dst_ref=out[t], send_sem=send_sems.at[t], recv_sem=recv_sems.at[t],
            device_id=(x, y, 1 - c), device_id_type=_MESH_ID) for t in range(n)]
        for cp in copies:
            cp.start()
        for cp in copies:
            cp.wait()

    return pl.pallas_call(
        body, in_specs=[_HBM] * n, out_specs=[_HBM] * n,
        out_shape=[jax.ShapeDtypeStruct((s.shape[0],) + s.shape[2:], s.dtype) for s in srcs],
        scratch_shapes=[pltpu.SemaphoreType.DMA((n,)), pltpu.SemaphoreType.DMA((n,))], name=name)(*srcs)


def _pair_add(src, recv, name):
    nchip, _, r, cdim = src.shape
    tr = _pick(r, 256, 2 * SUBLANES)
    core = lax.axis_index("c").astype(jnp.int32).reshape(1)

    def body(core_ref, s_ref, r_ref, o_ref):
        o_ref[...] = (s_ref[...].astype(F32) + r_ref[...].astype(F32)).astype(o_ref.dtype)

    grid_spec = pltpu.PrefetchScalarGridSpec(
        num_scalar_prefetch=1, grid=(nchip, r // tr),
        in_specs=[pl.BlockSpec((None, None, tr, cdim), lambda k, i, core_ref: (k, core_ref[0], i, 0)),
                  pl.BlockSpec((None, tr, cdim), lambda k, i, core_ref: (k, i, 0))],
        out_specs=pl.BlockSpec((None, tr, cdim), lambda k, i, core_ref: (k, i, 0)))
    return pl.pallas_call(body, grid_spec=grid_spec, out_shape=jax.ShapeDtypeStruct(recv.shape, recv.dtype),
                          compiler_params=_cparams("parallel", "parallel"), name=name)(core, src, recv)


def _by_chip_and_core(g):
    return g.reshape(N_DEV // 2, 2, g.shape[0] // N_DEV, g.shape[1])


def _chip_sums(grads, tag, from_sibling=None):
    views = [_by_chip_and_core(g) for g in grads]
    if from_sibling is None:
        from_sibling = _scatter_pair(views, name="scatter_pair_" + tag)
    return [_pair_add(v, s, name="pair_add") for v, s in zip(views, from_sibling)]


def _pair_ride(grads):
    views = [_by_chip_and_core(g) for g in grads]

    def copies(in_refs, buf_refs, sems):
        x, y, c, _ = _mesh_place()
        return [pltpu.make_async_remote_copy(src_ref=src.at[:, 1 - c], dst_ref=dst, send_sem=sems.at[2 * t],
                                             recv_sem=sems.at[2 * t + 1], device_id=(x, y, 1 - c), device_id_type=_MESH_ID)
                for t, (src, dst) in enumerate(zip(in_refs, buf_refs))]

    def start(in_refs, buf_refs, sems):
        for cp in copies(in_refs, buf_refs, sems):
            cp.start()

    def finish(in_refs, buf_refs, sems):
        for cp in copies(in_refs, buf_refs, sems):
            cp.wait()

    created = tuple(jax.ShapeDtypeStruct((v.shape[0],) + v.shape[2:], v.dtype) for v in views)
    return _Ride(inputs=tuple(views), carried=(), created=created, n_sems=2 * len(views), start=start, finish=finish)


def _sum_chips(parts):
    return _rowmap(_sum_parts_fn, [], stacks=[parts], row_outs=[(parts.shape[2], F32)], tr=128, name="sum_chips")[0]


def _buffer_roles(kinds, bufs):
    carried = [k for k in kinds if k in bufs]
    return carried, [k for k in kinds if k not in bufs]


def _gather_ride(sends, forwards, bufs):
    carried, created = _buffer_roles(list(dict.fromkeys([s[0] for s in sends] + [f[0] for f in forwards])), bufs)
    shape_of = {s[0]: jax.ShapeDtypeStruct((N_DEV,) + s[1].shape, s[1].dtype) for s in sends}
    order = carried + created

    def copies(in_refs, buf_refs, sems):
        x, y, c, chips = _mesh_place()
        buf = dict(zip(order, buf_refs))
        out, s0 = [], 0
        for (kind, _, r0, nr), src in zip(sends, in_refs):
            mine, dst = src.at[pl.ds(r0, nr)], buf[kind].at[4 * x + 2 * y + c, pl.ds(r0, nr)]
            out.append(pltpu.make_async_copy(mine, dst, sems.at[s0 + 8]))
            for k, peer in enumerate([(x, y, 1 - c)] + [(*chip, c) for chip in chips]):
                out.append(pltpu.make_async_remote_copy(src_ref=mine, dst_ref=dst, send_sem=sems.at[s0 + k],
                                                        recv_sem=sems.at[s0 + 4 + k], device_id=peer, device_id_type=_MESH_ID))
            s0 += 9
        for kind, r0, nr in forwards:
            for j, chip in enumerate(chips):
                blk = buf[kind].at[4 * chip[0] + 2 * chip[1] + c, pl.ds(r0, nr)]
                out.append(pltpu.make_async_remote_copy(src_ref=blk, dst_ref=blk, send_sem=sems.at[s0 + j],
                                                        recv_sem=sems.at[s0 + 3 + j], device_id=(x, y, 1 - c),
                                                        device_id_type=_MESH_ID))
            s0 += 6
        return out

    def start(in_refs, buf_refs, sems):
        for cp in copies(in_refs, buf_refs, sems):
            cp.start()

    def finish(in_refs, buf_refs, sems):
        for cp in copies(in_refs, buf_refs, sems):
            cp.wait()

    ride = _Ride(inputs=tuple(s[1] for s in sends), carried=tuple(bufs[k] for k in carried),
                 created=tuple(shape_of[k] for k in created), n_sems=9 * len(sends) + 6 * len(forwards),
                 start=start, finish=finish)
    return ride, order


def _scatter_ride(pieces, bufs):
    carried, created = _buffer_roles(list(dict.fromkeys(p[0] for p in pieces)), bufs)
    shape_of = {p[0]: jax.ShapeDtypeStruct(p[1].shape, p[1].dtype) for p in pieces}
    order = carried + created

    def copies(in_refs, buf_refs, sems):
        x, y, c, chips = _mesh_place()
        buf = dict(zip(order, buf_refs))
        out, s0 = [], 0
        for (kind, _, r0, nr), src in zip(pieces, in_refs):
            dst = buf[kind].at[2 * x + y, pl.ds(r0, nr)]
            out.append(pltpu.make_async_copy(src.at[2 * x + y, pl.ds(r0, nr)], dst, sems.at[s0 + 6]))
            for j, chip in enumerate(chips):
                out.append(pltpu.make_async_remote_copy(
                    src_ref=src.at[2 * chip[0] + chip[1], pl.ds(r0, nr)], dst_ref=dst, send_sem=sems.at[s0 + j],
                    recv_sem=sems.at[s0 + 3 + j], device_id=(*chip, c), device_id_type=_MESH_ID))
            s0 += 7
        return out

    def start(in_refs, buf_refs, sems):
        for cp in copies(in_refs, buf_refs, sems):
            cp.start()

    def finish(in_refs, buf_refs, sems):
        for cp in copies(in_refs, buf_refs, sems):
            cp.wait()

    ride = _Ride(inputs=tuple(p[1] for p in pieces), carried=tuple(bufs[k] for k in carried),
                 created=tuple(shape_of[k] for k in created), n_sems=7 * len(pieces), start=start, finish=finish)
    return ride, order


GATHER_PLAN = (
    ("mm_in_qkv", (("w_in", 0, 3),)),
    ("mm_in_gl", (("w_in", 1, 3),)),
    ("attn_fwd0", (("w_pa", 0, 1), ("w_pb", 0, 1))),
    ("attn_fwd1", (("w_pc", 0, 1), ("w_glu", 0, 1))),
    ("attn_fwd2", (("w_o", 0, 1),)),
    ("s5_fwd", (("w_ffn_out", 0, 2),)),
    ("mm_o", (("w_ffn_out", 1, 2),)),
    ("mm_ffn_in", (("w_in", 2, 3), ("w_ffn_in", 0, 2))),
    ("mm_ffn_out", (("w_ffn_in", 1, 2),)),
    ("norm2", ()),
)
EARLY_KINDS = ("w_ffn_out", "w_ffn_in")
LATE_KINDS = tuple(n for n in SHARDED if n not in EARLY_KINDS)
SCATTER_EARLY_PLAN = (
    ("mm_in_dw_qkv", (("w_ffn_out", 0, 2),)),
    ("mm_in_dx_qkv", (("w_ffn_out", 1, 2),)),
    ("mm_in_dw_gl", (("w_ffn_in", 0, 4), ("w_ffn_in", 3, 4))),
    ("mm_in_dx_gl", (("w_ffn_in", 1, 4), ("w_ffn_in", 2, 4))),
)
FINAL_SCATTER_PLAN = (
    ("adamw_w_ffn_out", (("w_in", 0, 3),)),
    ("adamw_w_ffn_in", (("w_in", 1, 3), ("w_in", 2, 3), ("w_o", 0, 1), ("w_pa", 0, 1), ("w_pb", 0, 1), ("w_pc", 0, 1),
                        ("w_glu", 0, 1))),
)
SCATTER_PLAN = (
    ("mm_ffn_out_dw", (("w_in", 0, 3),)),
    ("mm_ffn_out_dx", (("w_in", 1, 3),)),
    ("mm_ffn_in_dw", (("w_in", 2, 3), ("w_o", 0, 1), ("w_pa", 0, 1), ("w_pb", 0, 1), ("w_pc", 0, 1), ("w_glu", 0, 1))),
)


def _row_part(rows, part, parts):
    assert rows % (parts * 2 * SUBLANES) == 0
    return part * (rows // parts), rows // parts


class _Carried:
    def __init__(self, plan, blocks, make_ride, forwards_too):
        self.plan, self.blocks, self.make_ride, self.forwards_too = dict(plan), blocks, make_ride, forwards_too
        self.bufs, self.to_forward, self.order = {}, [], []

    def ride(self, host):
        if self.blocks is None or host not in self.plan:
            self.order = []
            return None
        sends = [(k, self.blocks[k], *_row_part(self.blocks[k].shape[-2], part, parts)) for k, part, parts in self.plan[host]]
        if self.forwards_too:
            ride, self.order = self.make_ride(sends, self.to_forward, self.bufs)
            self.to_forward = [(k, r0, nr) for k, _, r0, nr in sends]
        else:
            ride, self.order = self.make_ride(sends, self.bufs)
        return ride

    def took(self, ride_outs):
        for k, buf in zip(self.order, ride_outs[0] if ride_outs else []):
            self.bufs[k] = buf


class _OwnScatter:
    def __init__(self, enabled):
        self.enabled = enabled

    def pair_ride(self, grads):
        return _pair_ride(list(grads.values())) if self.enabled else None

    def steps(self, grads, ride_outs):
        if not self.enabled:
            return _Carried(SCATTER_EARLY_PLAN, None, _scatter_ride, forwards_too=False)
        sums = _chip_sums(list(grads.values()), "early", from_sibling=ride_outs[0])
        return _Carried(SCATTER_EARLY_PLAN, dict(zip(grads, sums)), _scatter_ride, forwards_too=False)


class _PrevScatter:
    PAIR_HOST = "norm2_bwd"

    def __init__(self, grads):
        self.grads = grads
        self.inner = _Carried(SCATTER_PLAN, None, _scatter_ride, forwards_too=False)
        self.pair_pending = False

    @property
    def bufs(self):
        return self.inner.bufs

    def ride(self, host):
        if self.grads is not None and host == self.PAIR_HOST:
            self.pair_pending = True
            return _pair_ride(list(self.grads.values()))
        return self.inner.ride(host)

    def took(self, ride_outs):
        if not self.pair_pending:
            return self.inner.took(ride_outs)
        self.pair_pending = False
        sums = _chip_sums(list(self.grads.values()), "late", from_sibling=ride_outs[0])
        self.inner = _Carried(SCATTER_PLAN, dict(zip(self.grads, sums)), _scatter_ride, forwards_too=False)


def _hosted(comm, fn, *args, name, **kwargs):
    res, ride_outs = fn(*args, name=name, rides=[comm.ride(name)], **kwargs)
    comm.took(ride_outs)
    return res


def _small_sizes(shapes):
    return [int(np.prod(shapes[n])) for n in SMALL]


def _pack_small(vals):
    flat = jnp.concatenate([vals[n].reshape(-1).astype(F32) for n in SMALL])
    rows = -(-flat.shape[0] // (LANES * N_DEV * SUBLANES)) * (N_DEV * SUBLANES)
    return jnp.pad(flat, (0, rows * LANES - flat.shape[0])).reshape(rows, LANES)


def _unpack_small(packed, shapes):
    flat = packed.reshape(-1)
    out, off = {}, 0
    for n, size in zip(SMALL, _small_sizes(shapes)):
        out[n] = flat[off:off + size].reshape(shapes[n])
        off += size
    return out


def _row(v):
    return v.reshape(1, -1)


def _layer_params(l, full, small):
    o1, o2, o3 = 3 * QKV_WIDTH, 3 * QKV_WIDTH + 2 * WIDTH_B, 3 * QKV_WIDTH + 2 * WIDTH_B + WIDTH_C
    b_in = small["b_in"][l]
    p = {
        "in_pieces": (("qkv", 0, o1), ("zb", o1, o2 - o1), ("uc", o2, o3 - o2), ("gl", o3, b_in.shape[0] - o3)),
        "b_qkv": _row(b_in[:o1]), "b_zb": _row(b_in[o1:o2]), "b_uc": _row(b_in[o2:o3]), "b_gl": _row(b_in[o3:]),
        "sgu_ln_g": _row(small["sgu_ln_g"][l]), "sgu_ln_b": _row(small["sgu_ln_b"][l]),
        "w_s": small["w_s"][l], "b_s_t": small["b_s"][l].T,
        "lam_re": small["lam_re"][l][:, None, :], "lam_im": small["lam_im"][l][:, None, :],
        "log_dt": small["log_dt"][l][:, None, None],
        "b_re_t": small["b_re"][l].transpose(0, 2, 1), "b_im_t": small["b_im"][l].transpose(0, 2, 1),
        "c_re": small["c_re"][l], "c_im": small["c_im"][l],
        "d_skip": _row(small["d_skip"][l]), "b_glu": _row(small["b_glu"][l]),
        "ln1_g": _row(small["ln1_g"][l]), "ln1_b": _row(small["ln1_b"][l]),
        "ln2_g": _row(small["ln2_g"][l]), "ln2_b": _row(small["ln2_b"][l]),
    }
    for n in SHARDED:
        p[n] = full[n]
    return p


def _twice(fn):
    def both(*args):
        y = fn(*args)
        return y, y
    return both


def _layer_fwd(x, xb, p, biases, comm):
    t, d = x.shape
    r = {"x": x, "xb": xb}
    for piece, off, n in p["in_pieces"]:
        r[piece] = _hosted(comm, _mm, xb, p["w_in"], tb=True, b_off=off, n=n, bias=p["b_" + piece],
                           out_dtype=BF16 if piece in ("qkv", "gl") else F32, name="mm_in_" + piece)
    ol = []
    for g, dil in enumerate(ATT_DILATIONS):
        ol += list(_attn_fwd(r["qkv"], biases[g], g, dil, comm))
    r["ol"] = ol
    r["ya"], r["ya_b"] = _rowmap(_twice(_combine), ol, row_outs=[(WIDTH_A, F32), (WIDTH_A, BF16)], tr=512,
                                 name="attn_combine")
    r["yb"] = _gmlp_fwd(r["zb"], p["sgu_ln_g"], p["sgu_ln_b"], p["w_s"], p["b_s_t"])
    ab_re, ab_im, bb_re_t, bb_im_t = _s5_disc_fwd(p["lam_re"], p["lam_im"], p["log_dt"], p["b_re_t"], p["b_im_t"])
    r["a_re"], r["a_im"] = ab_re.reshape(_SCAN_ROWS, LANES), ab_im.reshape(_SCAN_ROWS, LANES)
    r["s5_mats"] = _block_diag([bb_re_t, bb_im_t, p["c_re"], p["c_im"]])
    r["xr"], r["xi"], r["ys"], r["ycp"] = _s5_fwd(r["uc"], r["s5_mats"], r["a_re"], r["a_im"], p["d_skip"], comm)
    r["glin"] = _mm(r["ycp"], p["w_glu"], bias=p["b_glu"], name="mm_glu")
    r["yc"] = _rowmap(_glu, [r["ycp"], r["glin"]], row_outs=[(WIDTH_C, BF16)], tr=512, name="glu")[0]
    r["pa"] = _mm(r["ya_b"], p["w_pa"], tb=True, out_dtype=BF16, name="mm_pa")
    r["pb"] = _mm(r["yb"], p["w_pb"], tb=True, out_dtype=BF16, name="mm_pb")
    r["pc"] = _mm(r["yc"], p["w_pc"], tb=True, out_dtype=BF16, name="mm_pc")
    r["merged"] = _rowmap(_merge, [r["gl"], r["pa"], r["pb"], r["pc"]], row_outs=[(d, BF16)], name="merge")[0]
    r["mo"] = _hosted(comm, _mm, r["merged"], p["w_o"], name="mm_o")
    r["xm"], r["xm_b"] = _rowmap(_twice(_post_norm), [x, r["mo"]], consts=[p["ln1_g"], p["ln1_b"]],
                                 row_outs=[(d, F32), (d, BF16)], name="norm1")
    r["gate"], r["up"], r["act"] = _hosted(comm, _mm_swiglu, r["xm_b"], p["w_ffn_in"], name="mm_ffn_in")
    r["f"] = _hosted(comm, _mm, r["act"], p["w_ffn_out"], name="mm_ffn_out")
    out, out_b = _hosted(comm, _rowmap, _twice(_post_norm), [r["xm"], r["f"]], consts=[p["ln2_g"], p["ln2_b"]],
                         row_outs=[(d, F32), (d, BF16)], name="norm2")
    return out, out_b, r


def _layer_bwd(dout, r, p, biases, consts, comm, own):
    t, d = dout.shape
    gw, gs = {}, {}
    ffw = 2 * r["gate"].shape[1]
    dxm, df, gs["ln2_g"], gs["ln2_b"] = _hosted(
        comm, _rowmap, _post_norm_bwd, [r["xm"], r["f"], dout], consts=[p["ln2_g"], p["ln2_b"]],
        row_outs=[(d, F32), (d, BF16)], red_outs=[(1, d)] * 2, name="norm2_bwd")
    gw["w_ffn_out"] = _hosted(comm, _mm, r["act"], df, ta=True, out_dtype=BF16, name="mm_ffn_out_dw")
    dact = _hosted(comm, _mm, df, p["w_ffn_out"], tb=True, out_dtype=BF16, name="mm_ffn_out_dx")
    dgu = _rowmap(_swiglu_bwd, [r["gate"], r["up"], dact], row_outs=[(ffw, BF16)], tr=128, name="swiglu_bwd")[0]
    gw["w_ffn_in"] = _hosted(comm, _mm, dgu, r["xm_b"], ta=True, out_dtype=BF16, name="mm_ffn_in_dw")
    early_grads = {n: gw[n] for n in EARLY_KINDS}
    dxm, from_sibling = _mm(dgu, p["w_ffn_in"], add=dxm, name="mm_ffn_in_dx", rides=[own.pair_ride(early_grads)])
    early = own.steps(early_grads, from_sibling)
    dx, dmo, gs["ln1_g"], gs["ln1_b"] = _rowmap(
        _post_norm_bwd, [r["x"], r["mo"], dxm], consts=[p["ln1_g"], p["ln1_b"]],
        row_outs=[(d, F32), (d, BF16)], red_outs=[(1, d)] * 2, name="norm1_bwd")
    gw["w_o"] = _hosted(comm, _mm, r["merged"], dmo, ta=True, out_dtype=BF16, name="mm_o_dw")
    dmerged = _hosted(comm, _mm, dmo, p["w_o"], tb=True, name="mm_o_dx")
    dgl, dpa, dpb, dpc, db_gl = _rowmap(
        _merge_bwd, [r["gl"], r["pa"], r["pb"], r["pc"], dmerged],
        row_outs=[(3 * d, BF16), (d, BF16), (d, BF16), (d, BF16)], red_outs=[(1, 3 * d)], tr=128, name="merge_bwd")
    gw["w_pa"] = _mm(dpa, r["ya_b"], ta=True, out_dtype=BF16, name="mm_pa_dw")
    gw["w_pb"] = _mm(dpb, r["yb"], ta=True, out_dtype=BF16, name="mm_pb_dw")
    gw["w_pc"] = _mm(dpc, r["yc"], ta=True, out_dtype=BF16, name="mm_pc_dw")
    dya = _mm(dpa, p["w_pa"], name="mm_pa_dx")
    dyb = _mm(dpb, p["w_pb"], name="mm_pb_dx")
    dyc = _mm(dpc, p["w_pc"], name="mm_pc_dx")
    dycp, dglin, gs["b_glu"] = _rowmap(_glu_bwd, [r["ycp"], r["glin"], dyc], row_outs=[(WIDTH_C, F32), (WIDTH_C, BF16)],
                                       red_outs=[(1, WIDTH_C)], tr=512, name="glu_bwd")
    gw["w_glu"] = _mm(r["ycp"], dglin, ta=True, out_dtype=BF16, name="mm_glu_dw")
    dycp = _mm(dglin, p["w_glu"], tb=True, add=dycp, name="mm_glu_dx")
    dys, duc, gs["d_skip"] = _rowmap(_s5_out_bwd, [r["ys"], r["uc"], dycp], consts=[p["d_skip"]],
                                     row_outs=[(WIDTH_C, F32)] * 2, red_outs=[(1, WIDTH_C)], tr=512, name="s5_out_act_bwd")
    duc, d_bmat_re, d_bmat_im, d_cmat_re, d_cmat_im, da_re, da_im = _s5_bwd(
        dys, duc, r["uc"], r["xr"], r["xi"], r["s5_mats"], r["a_re"], r["a_im"])
    d_bb_re_t, d_bb_im_t, gs["c_re"], gs["c_im"] = _block_diag_extract(
        [d_bmat_re, d_bmat_im, d_cmat_re, d_cmat_im], SSM_GROUP, SSM_STATE)
    cts = (da_re.reshape(N_GROUPS_C, 1, SSM_STATE), da_im.reshape(N_GROUPS_C, 1, SSM_STATE), d_bb_re_t, d_bb_im_t)
    d_lr, d_li, d_ldt, d_br_t, d_bi_t = _s5_disc_bwd(p["lam_re"], p["lam_im"], p["log_dt"], p["b_re_t"], p["b_im_t"], cts)
    gs["lam_re"], gs["lam_im"], gs["log_dt"] = d_lr[:, 0, :], d_li[:, 0, :], d_ldt[:, 0, 0]
    gs["b_re"], gs["b_im"] = d_br_t.transpose(0, 2, 1), d_bi_t.transpose(0, 2, 1)
    dzb, db_zb, gs["sgu_ln_g"], gs["sgu_ln_b"], gs["w_s"], dbs_t = _gmlp_bwd(
        r["zb"], dyb, p["sgu_ln_g"], p["sgu_ln_b"], p["w_s"], p["b_s_t"], consts["group_sel"])
    gs["b_s"] = dbs_t[:, :N_GROUPS_B].T
    do_corr = _rowmap(_combine_bwd, r["ol"] + [r["ya"], dya], consts=[consts["head_ones"]],
                      row_outs=[(WIDTH_A, F32)] * 6, tr=512, name="attn_combine_bwd")
    dq, dk, dv, dbias = [], [], [], []
    for g, dil in enumerate(ATT_DILATIONS):
        do_g, corr_g, lse_g = do_corr[g], do_corr[3 + g], r["ol"][2 * g + 1]
        dq_g, dk_g, dv_g, db_g = _attn_bwd(r["qkv"], biases[g], do_g, lse_g, corr_g, g, dil)
        dq.append(dq_g)
        dk.append(dk_g)
        dv.append(dv_g)
        dbias.append(db_g)
    def cast_colsum(*pieces):
        a = pieces[0] if len(pieces) == 1 else jnp.concatenate(pieces, axis=1)
        return a, jnp.sum(a, axis=0, keepdims=True)

    dqkv, db_qkv = _rowmap(cast_colsum, dq + dk + dv, row_outs=[(3 * QKV_WIDTH, BF16)],
                           red_outs=[(1, 3 * QKV_WIDTH)], tr=256, name="cast_colsum_qkv")
    duc, db_uc = _rowmap(cast_colsum, [duc], row_outs=[(WIDTH_C, BF16)], red_outs=[(1, WIDTH_C)], tr=512,
                         name="cast_colsum_uc")
    dpieces = {"qkv": dqkv, "zb": dzb, "uc": duc, "gl": dgl}
    rows_in = p["w_in"].shape[0]
    dw_in = None
    for piece, off, n in p["in_pieces"]:
        dw_in = _hosted(early, _mm, dpieces[piece], r["xb"], ta=True, out_dtype=BF16, into=(rows_in, off, dw_in),
                        name="mm_in_dw_" + piece)
        dx = _hosted(early, _mm, dpieces[piece], p["w_in"], b_off=off, add=dx, name="mm_in_dx_" + piece)
    gw["w_in"] = dw_in
    gs["b_in"] = jnp.concatenate([db_qkv, db_zb, db_uc, db_gl], axis=1)[0]
    for n in ("sgu_ln_g", "sgu_ln_b", "d_skip", "b_glu", "ln1_g", "ln1_b", "ln2_g", "ln2_b"):
        gs[n] = gs[n][0]
    return dx, gw, gs, dbias, early.bufs


def _cast_bf16(w):
    w2 = w.reshape(-1, w.shape[-1])
    out = _rowmap(lambda a: a, [w2], row_outs=[(w2.shape[1], BF16)], tr=512, name="cast_bf16")[0]
    return out.reshape(w.shape)


def _static_consts():
    head_ones = np.kron(np.eye(HEADS_PER_GROUP, dtype=np.float32), np.ones((HEAD_DIM, HEAD_DIM), np.float32))
    group_sel = np.zeros((WIDTH_B, LANES), np.float32)
    group_sel[np.arange(WIDTH_B), np.arange(WIDTH_B) // CHUNK] = 1.0
    return {"head_ones": jnp.asarray(head_ones), "group_sel": jnp.asarray(group_sel)}


def _step(x, tgt, w, m, v):
    shapes = {n: w[n].shape for n in WEIGHTS}
    consts = _static_consts()
    mine_bf = {n: _cast_bf16(w[n].transpose(0, 2, 1) if n in TRANSPOSED else w[n]) for n in SHARDED}
    small = {n: w[n] for n in SMALL}
    buckets = [jnp.asarray(_bucket_table(dil)) for dil in ATT_DILATIONS]
    biases = [_bias_fwd(w["rel_bias"], buckets[g], g) for g in range(len(ATT_DILATIONS))]
    params, saved = [], []
    h, hb = _rowmap(_twice(lambda a: a), [x], row_outs=[(x.shape[1], F32), (x.shape[1], BF16)], name="cast_x")
    gathered = dict(zip(SHARDED, _gather_layer([mine_bf[n] for n in SHARDED], 0, name="gather_layer0")))
    for l in range(DEPTH):
        p = _layer_params(l, {n: g.reshape(-1, g.shape[2]) for n, g in gathered.items()}, small)
        ahead = _Carried(GATHER_PLAN, {n: mine_bf[n][l + 1] for n in SHARDED} if l + 1 < DEPTH else None,
                         _gather_ride, forwards_too=True)
        h, hb, r = _layer_fwd(h, hb, p, biases, ahead)
        gathered = ahead.bufs
        params.append(p)
        saved.append(r)
    dy, loss_part = _rowmap(_loss_fn, [h, tgt], row_outs=[(h.shape[1], F32)], red_outs=[(1, LANES)], name="loss")
    loss = lax.psum(loss_part[0, 0], MESH_AXES)
    g_mine, gs_layers = {n: [None] * DEPTH for n in SHARDED}, [None] * DEPTH
    dbias_sum = None
    behind = _PrevScatter(None)
    for l in reversed(range(DEPTH)):
        dy, gw, gs_layers[l], dbias, early_parts = _layer_bwd(
            dy, saved[l], params[l], biases, consts, behind, _OwnScatter(enabled=True))
        saved[l] = None
        for n in EARLY_KINDS:
            g_mine[n][l] = _sum_chips(early_parts[n])
        if behind.grads is not None:
            for n in LATE_KINDS:
                g_mine[n][l + 1] = _sum_chips(behind.bufs[n])
        behind = _PrevScatter({n: gw[n] for n in LATE_KINDS})
        if dbias_sum is None:
            dbias_sum = dbias
        else:
            dbias_sum = [_rowmap(lambda a, b: a + b, [a.reshape(-1, 2 * ATT_BLOCK), b.reshape(-1, 2 * ATT_BLOCK)],
                                 row_outs=[(2 * ATT_BLOCK, F32)], name="dbias_add")[0].reshape(a.shape)
                         for a, b in zip(dbias_sum, dbias)]
    drel = [_bias_bwd(dbias_sum[g], buckets[g], g) for g in range(len(ATT_DILATIONS))]
    drel = _rowmap(lambda a, b, c: a + b + c, drel, row_outs=[(LANES, F32)], name="drel_add")[0]
    grad_small_local = {n: jnp.stack([gs_layers[l][n] for l in range(DEPTH)]) for n in SMALL if n != "rel_bias"}
    grad_small_local["rel_bias"] = drel[:, :shapes["rel_bias"][1]]
    out_g, out_d, out_m, out_v = {}, {}, {}, {}
    sums = _chip_sums(list(behind.grads.values()), "late")
    final = _Carried(FINAL_SCATTER_PLAN, dict(zip(LATE_KINDS, sums)), _scatter_ride, forwards_too=False)
    for n in EARLY_KINDS + LATE_KINDS:
        if n == LATE_KINDS[0]:
            for k in LATE_KINDS:
                g_mine[k][0] = _sum_chips(final.bufs[k])
        g = jnp.stack(g_mine[n])
        out_g[n] = g.transpose(0, 2, 1) if n in TRANSPOSED else g
        cols = shapes[n][-1]
        res = _hosted(final, _rowmap, _adamw, [a.reshape(-1, cols) for a in (w[n], out_g[n], m[n], v[n])],
                      row_outs=[(cols, F32)] * 3, tr=128, name="adamw_" + n)
        out_d[n], out_m[n], out_v[n] = [a.reshape(shapes[n]) for a in res]
    packed = _pack_small(grad_small_local)
    rows = packed.shape[0] // N_DEV
    parts = _exchange(packed.reshape(N_DEV, rows, LANES), gather=False, name="scatter_small")
    mine = _rowmap(_sum_parts_fn, [], stacks=[parts], row_outs=[(LANES, F32)], name="sum_small")[0]
    g_small = _exchange(mine, gather=True, name="gather_small").reshape(-1, LANES)
    out_g.update(_unpack_small(g_small, {n: shapes[n] for n in SMALL}))
    for n in SMALL:
        cols = shapes[n][-1]
        res = _rowmap(_adamw, [a.reshape(-1, cols) for a in (w[n], out_g[n], m[n], v[n])],
                      row_outs=[(cols, F32)] * 3, name="adamw_" + n)
        out_d[n], out_m[n], out_v[n] = [a.reshape(shapes[n]) for a in res]
    return loss, dy, out_g, out_d, out_m, out_v


def kernel(x, w_in, b_in, rel_bias, sgu_ln_g, sgu_ln_b, w_s, b_s, lam_re, lam_im, log_dt, b_re, b_im, c_re, c_im, d_skip, w_glu, b_glu, w_pa, w_pb, w_pc, w_o, ln1_g, ln1_b, w_ffn_in, w_ffn_out, ln2_g, ln2_b, loss_target, m_w_in, m_b_in, m_rel_bias, m_sgu_ln_g, m_sgu_ln_b, m_w_s, m_b_s, m_lam_re, m_lam_im, m_log_dt, m_b_re, m_b_im, m_c_re, m_c_im, m_d_skip, m_w_glu, m_b_glu, m_w_pa, m_w_pb, m_w_pc, m_w_o, m_ln1_g, m_ln1_b, m_w_ffn_in, m_w_ffn_out, m_ln2_g, m_ln2_b, v_w_in, v_b_in, v_rel_bias, v_sgu_ln_g, v_sgu_ln_b, v_w_s, v_b_s, v_lam_re, v_lam_im, v_log_dt, v_b_re, v_b_im, v_c_re, v_c_im, v_d_skip, v_w_glu, v_b_glu, v_w_pa, v_w_pb, v_w_pc, v_w_o, v_ln1_g, v_ln1_b, v_w_ffn_in, v_w_ffn_out, v_ln2_g, v_ln2_b):
    args = dict(locals())
    w = {n: args[n] for n in WEIGHTS}
    m = {n: args["m_" + n] for n in WEIGHTS}
    v = {n: args["v_" + n] for n in WEIGHTS}
    loss, dx, g, d, nm, nv = _step(x[0], loss_target[0], w, m, v)
    return (loss, dx[None], *[g[n] for n in WEIGHTS], *[d[n] for n in WEIGHTS],
            *[nm[n] for n in WEIGHTS], *[nv[n] for n in WEIGHTS])
```

```python
import functools
import math
from typing import Callable, NamedTuple

import numpy as np
import jax
import jax.numpy as jnp
from jax import lax
from jax.experimental import pallas as pl
from jax.experimental.pallas import tpu as pltpu

F32 = jnp.float32
BF16 = jnp.bfloat16

MESH_AXES = ("x", "y", "c")
N_DEV = 8
DEPTH = 4

ATT_DILATIONS = (1, 4, 16)
ATT_STEPS = 128
HEADS_PER_GROUP = 8
HEAD_DIM = 64
QKV_WIDTH = 1536
WIDTH_A = HEADS_PER_GROUP * HEAD_DIM
ATT_BLOCK = 128
N_REL_BUCKETS = 32
REL_MAX_DIST = 2048
NEG_INF = -1e30
CHUNK = 128
WIDTH_B = 768
N_GROUPS_B = 6
WIDTH_C = 768
SSM_GROUP = 16
N_GROUPS_C = 48
SSM_STATE = 64
SSM_PACK = 8
N_SSM_BLOCKS = N_GROUPS_C // SSM_PACK
SSM_COLS = N_GROUPS_C * SSM_STATE
ALPHA = (2 * DEPTH) ** 0.25

ADAM_LR = 0.001
ADAM_B1 = 0.9
ADAM_B2 = 0.999
ADAM_EPS = 1e-08
ADAM_WD = 0.01
ADAM_STEP = 10

LANES = 128
SUBLANES = 8
VMEM_LIMIT = 48 * 1024 * 1024

SHARDED = ("w_in", "w_glu", "w_pa", "w_pb", "w_pc", "w_o", "w_ffn_in", "w_ffn_out")
TRANSPOSED = ("w_in", "w_pa", "w_pb", "w_pc", "w_ffn_in")
SMALL = ("b_in", "rel_bias", "sgu_ln_g", "sgu_ln_b", "w_s", "b_s", "lam_re", "lam_im", "log_dt",
         "b_re", "b_im", "c_re", "c_im", "d_skip", "b_glu", "ln1_g", "ln1_b", "ln2_g", "ln2_b")
WEIGHTS = ("w_in", "b_in", "rel_bias", "sgu_ln_g", "sgu_ln_b", "w_s", "b_s", "lam_re", "lam_im",
           "log_dt", "b_re", "b_im", "c_re", "c_im", "d_skip", "w_glu", "b_glu", "w_pa", "w_pb",
           "w_pc", "w_o", "ln1_g", "ln1_b", "w_ffn_in", "w_ffn_out", "ln2_g", "ln2_b")


def _pick(dim, target, mult):
    best = None
    for t in range(mult, min(dim, target) + 1, mult):
        if dim % t == 0:
            best = t
    return dim if best is None else best


def _cparams(*sem):
    return pltpu.CompilerParams(dimension_semantics=sem, vmem_limit_bytes=VMEM_LIMIT)


def _zero_map(ndim):
    return lambda *_: (0,) * ndim


_HBM = pl.BlockSpec(memory_space=pl.ANY)
_MESH_ID = pl.DeviceIdType.MESH


class _Ride(NamedTuple):
    inputs: tuple
    carried: tuple
    created: tuple
    n_sems: int
    start: Callable
    finish: Callable


def _pallas(body, *, grid, in_specs, out_specs, out_shape, args, scratch=(), semantics, rides=(), aliases=None, name):
    rides = [r for r in rides if r is not None]
    n_in, n_out, n_scr = len(args), len(out_shape), len(scratch)
    r_args, r_shapes, aliases, spans = [], [], dict(aliases or {}), []
    for r in rides:
        i0, o0 = len(r_args), len(r_shapes)
        r_args += [*r.inputs, *r.carried]
        for k, a in enumerate(r.carried):
            aliases[n_in + i0 + len(r.inputs) + k] = n_out + o0 + k
        r_shapes += [jax.ShapeDtypeStruct(a.shape, a.dtype) for a in r.carried] + list(r.created)
        spans.append((i0, len(r.inputs), o0, len(r.carried) + len(r.created)))

    def full_body(*refs):
        host_in, ride_in = refs[:n_in], refs[n_in:n_in + len(r_args)]
        p = n_in + len(r_args)
        host_out, ride_out = refs[p:p + n_out], refs[p + n_out:p + n_out + len(r_shapes)]
        p += n_out + len(r_shapes)
        host_scr, ride_sems = refs[p:p + n_scr], refs[p + n_scr:]
        ids = [pl.program_id(k) for k in range(len(grid))]
        first = functools.reduce(jnp.logical_and, [i == 0 for i in ids])
        last = functools.reduce(jnp.logical_and, [i == g - 1 for i, g in zip(ids, grid)])

        def each(method):
            for r, (i0, ni, o0, no), sems in zip(rides, spans, ride_sems):
                getattr(r, method)(ride_in[i0:i0 + ni], ride_out[o0:o0 + no], sems)

        if rides:
            pl.when(first)(lambda: each("start"))
        body(*host_in, *host_out, *host_scr)
        if rides:
            pl.when(last)(lambda: each("finish"))

    if rides:
        semantics = ("arbitrary",) * len(grid)
    outs = pl.pallas_call(
        full_body, grid=grid, in_specs=list(in_specs) + [_HBM] * len(r_args),
        out_specs=list(out_specs) + [_HBM] * len(r_shapes), out_shape=list(out_shape) + r_shapes,
        scratch_shapes=list(scratch) + [pltpu.SemaphoreType.DMA((r.n_sems,)) for r in rides],
        input_output_aliases=aliases, compiler_params=_cparams(*semantics), name=name)(*args, *r_args)
    return outs[:n_out], [outs[n_out + o0:n_out + o0 + no] for _, _, o0, no in spans]


def _rowmap(fn, rows, consts=(), stacks=(), row_outs=(), red_outs=(), tr=256, name=None, rides=None):
    t = rows[0].shape[0] if rows else stacks[0].shape[1]
    dtypes = [a.dtype for a in (*rows, *stacks)] + [dt for _, dt in row_outs]
    packed = any(jnp.dtype(dt).itemsize < 4 for dt in dtypes)
    tr = _pick(t, tr, 2 * SUBLANES if packed else SUBLANES)
    n_r, n_c, n_s, n_o = len(rows), len(consts), len(stacks), len(row_outs)

    def body(*refs):
        ins = [r[...] for r in refs[:n_r + n_c + n_s]]
        outs = refs[n_r + n_c + n_s:n_r + n_c + n_s + n_o]
        reds = refs[n_r + n_c + n_s + n_o:]
        res = fn(*ins)
        if not isinstance(res, (tuple, list)):
            res = (res,)
        for o, v in zip(outs, res[:n_o]):
            o[...] = v.astype(o.dtype)
        if reds:
            @pl.when(pl.program_id(0) == 0)
            def _():
                for r in reds:
                    r[...] = jnp.zeros_like(r)
            for r, v in zip(reds, res[n_o:]):
                r[...] += v

    in_specs = [pl.BlockSpec((tr, r.shape[1]), lambda i: (i, 0)) for r in rows]
    in_specs += [pl.BlockSpec(c.shape, _zero_map(c.ndim)) for c in consts]
    in_specs += [pl.BlockSpec((s.shape[0], tr, s.shape[2]), lambda i: (0, i, 0)) for s in stacks]
    out_specs = [pl.BlockSpec((tr, w), lambda i: (i, 0)) for w, _ in row_outs]
    out_specs += [pl.BlockSpec(s, _zero_map(len(s))) for s in red_outs]
    out_shape = [jax.ShapeDtypeStruct((t, w), dt) for w, dt in row_outs]
    out_shape += [jax.ShapeDtypeStruct(s, F32) for s in red_outs]
    outs, ride_outs = _pallas(body, grid=(t // tr,), in_specs=in_specs, out_specs=out_specs, out_shape=out_shape,
                              args=[*rows, *consts, *stacks], semantics=("arbitrary",), rides=rides or (), name=name)
    return outs if rides is None else (outs, ride_outs)


MM_VMEM_BUDGET = 36 * 1024 * 1024


def _divisors(dim, mult, must_divide=0):
    out = [t for t in range(dim, 0, -mult) if t % mult == 0 and dim % t == 0 and must_divide % t == 0]
    return out or [dim]


def _mm_tiles(m, n, k, a_bytes, b_bytes, out_bytes, extra_bytes, ta, b_off_n, b_off_k, out_off, tm, tn):
    tms = _divisors(m, LANES if ta else SUBLANES, out_off)
    tm = next((t for t in tms if t <= tm), tms[-1])
    tns = [t for t in _divisors(n, LANES, b_off_n) if t <= tn] or [_divisors(n, LANES, b_off_n)[-1]]
    for tn_ in tns:
        for tk in _divisors(k, LANES, b_off_k):
            acc = 0 if tk == k else tm * tn_ * 4
            need = 2 * (tm * tk * a_bytes + tk * tn_ * b_bytes + tm * tn_ * (out_bytes + extra_bytes)) + acc
            if need <= MM_VMEM_BUDGET:
                return tm, tn_, tk
    return tm, tns[-1], _divisors(k, LANES, b_off_k)[-1]


def _mm(a, b, *, ta=False, tb=False, bias=None, add=None, out_dtype=F32, b_off=0, n=None, tm=1024, tn=1024, name=None,
        rides=None, into=None):
    k, m = a.shape if ta else a.shape[::-1]
    if tb:
        n = b.shape[0] if n is None else n
        assert b.shape[1] == k and b_off + n <= b.shape[0]
    else:
        n = b.shape[1]
        assert b_off + k <= b.shape[0]
    out_rows, out_off, out_buf = (m, 0, None) if into is None else into
    extra = 4 if add is not None else 0
    tm, tn, tk = _mm_tiles(m, n, k, a.dtype.itemsize, b.dtype.itemsize, jnp.dtype(out_dtype).itemsize, extra, ta,
                           b_off if tb else 0, 0 if tb else b_off, out_off, tm, tn)
    nk = k // tk
    off_n, off_k = (b_off // tn, 0) if tb else (0, b_off // tk)
    off_m = out_off // tm
    dims = (((0 if ta else 1,), (1 if tb else 0,)), ((), ()))

    def body(*refs):
        a_ref, b_ref = refs[0], refs[1]
        rest = list(refs[2:])
        bias_ref = rest.pop(0) if bias is not None else None
        add_ref = rest.pop(0) if add is not None else None
        o_ref = rest.pop(0)
        part = lax.dot_general(a_ref[...].astype(BF16), b_ref[...].astype(BF16), dims, preferred_element_type=F32)

        def finish(r):
            if bias_ref is not None:
                r = r + bias_ref[...]
            if add_ref is not None:
                r = r + add_ref[...]
            o_ref[...] = r.astype(o_ref.dtype)

        if nk == 1:
            finish(part)
        else:
            acc_ref = rest.pop(0)
            kk = pl.program_id(2)

            @pl.when(kk == 0)
            def _():
                acc_ref[...] = part

            @pl.when(kk > 0)
            def _():
                acc_ref[...] += part

            @pl.when(kk == nk - 1)
            def _():
                finish(acc_ref[...])

    a_spec = pl.BlockSpec((tk, tm), lambda i, j, q: (q, i)) if ta else pl.BlockSpec((tm, tk), lambda i, j, q: (i, q))
    if tb:
        b_spec = pl.BlockSpec((tn, tk), lambda i, j, q: (j + off_n, q))
    else:
        b_spec = pl.BlockSpec((tk, tn), lambda i, j, q: (q + off_k, j))
    in_specs, args = [a_spec, b_spec], [a, b]
    if bias is not None:
        in_specs.append(pl.BlockSpec((1, tn), lambda i, j, q: (0, j)))
        args.append(bias)
    if add is not None:
        in_specs.append(pl.BlockSpec((tm, tn), lambda i, j, q: (i, j)))
        args.append(add)
    aliases = {}
    if out_buf is not None:
        assert out_buf.shape == (out_rows, n) and out_buf.dtype == jnp.dtype(out_dtype)
        in_specs.append(_HBM)
        args.append(out_buf)
        aliases = {len(args) - 1: 0}

    def body_in_place(*refs):
        body(*refs[:len(args) - 1], *refs[len(args):])

    outs, ride_outs = _pallas(
        body if out_buf is None else body_in_place, grid=(m // tm, n // tn, nk), in_specs=in_specs,
        out_specs=[pl.BlockSpec((tm, tn), lambda i, j, q: (i + off_m, j))],
        out_shape=[jax.ShapeDtypeStruct((out_rows, n), out_dtype)], args=args,
        scratch=[] if nk == 1 else [pltpu.VMEM((tm, tn), F32)],
        semantics=("parallel", "parallel", "arbitrary"), rides=rides or (), aliases=aliases, name=name)
    return outs[0] if rides is None else (outs[0], ride_outs)


def _mm_swiglu(a, w_t, *, name, rides=None):
    m, k = a.shape
    f = w_t.shape[0] // 2
    tm, tn, tk = _mm_tiles(m, f, k, a.dtype.itemsize, 2 * w_t.dtype.itemsize, 3 * 2, 0, False, 0, 0, 0, 1024, 512)
    assert tk == k, "the fused activation needs the whole contraction in one block"

    def body(a_ref, g_ref, u_ref, gate_ref, up_ref, act_ref):
        av = a_ref[...].astype(BF16)
        gate = lax.dot_general(av, g_ref[...].astype(BF16), _NT, preferred_element_type=F32)
        up = lax.dot_general(av, u_ref[...].astype(BF16), _NT, preferred_element_type=F32)
        gate_ref[...] = gate.astype(gate_ref.dtype)
        up_ref[...] = up.astype(up_ref.dtype)
        act_ref[...] = _swiglu2(gate, up).astype(act_ref.dtype)

    out_spec = pl.BlockSpec((tm, tn), lambda i, j: (i, j))
    outs, ride_outs = _pallas(
        body, grid=(m // tm, f // tn),
        in_specs=[pl.BlockSpec((tm, k), lambda i, j: (i, 0)), pl.BlockSpec((tn, k), lambda i, j: (j, 0)),
                  pl.BlockSpec((tn, k), lambda i, j: (j + f // tn, 0))],
        out_specs=[out_spec] * 3, out_shape=[jax.ShapeDtypeStruct((m, f), BF16)] * 3, args=[a, w_t, w_t],
        semantics=("parallel", "parallel"), rides=rides or (), name=name)
    return outs if rides is None else (outs, ride_outs)


def _ln(x, g, b, eps=1e-5):
    mu = jnp.mean(x, axis=-1, keepdims=True)
    var = jnp.mean(jnp.square(x - mu), axis=-1, keepdims=True)
    return (x - mu) * lax.rsqrt(var + eps) * g + b


def _post_norm(x, f, g, b):
    return _ln(ALPHA * x + f, g, b)


def _post_norm_bwd(x, f, dy, g, b):
    _, vjp = jax.vjp(_post_norm, x, f, g, b)
    return vjp(dy)


def _merge3(g0, g1, g2, pa, pb, pc):
    return jax.nn.sigmoid(g0) * pa + jax.nn.sigmoid(g1) * pb + jax.nn.sigmoid(g2) * pc


def _merge_args(gl, pa, pb, pc):
    d = pa.shape[1]
    return [a.astype(F32) for a in (gl[:, :d], gl[:, d:2 * d], gl[:, 2 * d:], pa, pb, pc)]


def _merge(gl, pa, pb, pc):
    return _merge3(*_merge_args(gl, pa, pb, pc))


def _merge_bwd(gl, pa, pb, pc, dm):
    _, vjp = jax.vjp(_merge3, *_merge_args(gl, pa, pb, pc))
    d0, d1, d2, dpa, dpb, dpc = vjp(dm)
    dgl = jnp.concatenate([d0, d1, d2], axis=1)
    return dgl, dpa, dpb, dpc, jnp.sum(dgl, axis=0, keepdims=True)


def _swiglu2(gate, up):
    return jax.nn.silu(gate) * up


def _swiglu_bwd(gate, up, dact):
    _, vjp = jax.vjp(_swiglu2, gate.astype(F32), up.astype(F32))
    dg, du = vjp(dact.astype(F32))
    return jnp.concatenate([dg, du], axis=1)


def _glu(ycp, lin):
    return ycp * jax.nn.sigmoid(lin)


def _glu_bwd(ycp, lin, dyc):
    _, vjp = jax.vjp(_glu, ycp, lin)
    dycp, dlin = vjp(dyc)
    return dycp, dlin, jnp.sum(dlin, axis=0, keepdims=True)


def _s5_out_bwd(ys, uc, dycp, dskip):
    _, vjp = jax.vjp(jax.nn.gelu, ys)
    dys = vjp(dycp)[0]
    return dys, dys * dskip, jnp.sum(dys * uc, axis=0, keepdims=True)


def _combine(o0, l0, o1, l1, o2, l2):
    m = jnp.maximum(jnp.maximum(l0, l1), l2)
    e0, e1, e2 = jnp.exp(l0 - m), jnp.exp(l1 - m), jnp.exp(l2 - m)
    s = e0 + e1 + e2
    return (e0 / s) * o0 + (e1 / s) * o1 + (e2 / s) * o2


def _combine_bwd(o0, l0, o1, l1, o2, l2, ya, dya, head_ones):
    m = jnp.maximum(jnp.maximum(l0, l1), l2)
    e0, e1, e2 = jnp.exp(l0 - m), jnp.exp(l1 - m), jnp.exp(l2 - m)
    s = e0 + e1 + e2
    dot_ya = jnp.dot(dya * ya, head_ones, precision=lax.Precision.HIGHEST, preferred_element_type=F32)
    w0, w1, w2 = e0 / s, e1 / s, e2 / s
    return w0 * dya, w1 * dya, w2 * dya, -w0 * dot_ya, -w1 * dot_ya, -w2 * dot_ya


def _loss_fn(y, tgt):
    err = y - tgt
    part = jnp.sum(jnp.sum(jnp.square(err), axis=1, keepdims=True), axis=0, keepdims=True) * (0.5 / y.shape[1])
    return err * (1.0 / y.shape[1]), jnp.broadcast_to(part, (1, LANES))


def _adamw(w, g, m, v):
    m = ADAM_B1 * m + (1.0 - ADAM_B1) * g
    v = ADAM_B2 * v + (1.0 - ADAM_B2) * jnp.square(g)
    m_hat = m / (1.0 - ADAM_B1 ** ADAM_STEP)
    v_hat = v / (1.0 - ADAM_B2 ** ADAM_STEP)
    delta = -ADAM_LR * (m_hat / (jnp.sqrt(v_hat) + ADAM_EPS) + ADAM_WD * w)
    return delta, m, v


def _sum_parts_fn(parts):
    g = parts[0].astype(F32)
    for j in range(1, parts.shape[0]):
        g = g + parts[j].astype(F32)
    return g


def _t5_bucket(dist):
    max_exact = N_REL_BUCKETS // 2
    d = np.maximum(dist, 1).astype(np.float32)
    scale = (N_REL_BUCKETS - max_exact) / math.log(REL_MAX_DIST / max_exact)
    large = max_exact + (np.log(d / max_exact) * scale).astype(np.int32)
    large = np.minimum(large, N_REL_BUCKETS - 1)
    return np.where(dist < max_exact, dist, large).astype(np.int32)


def _bucket_table(dilation):
    i = np.arange(ATT_BLOCK)[:, None]
    kk = np.arange(2 * ATT_BLOCK)[None, :]
    steps = ATT_BLOCK + i - kk
    return _t5_bucket(np.maximum(steps, 0) * dilation)


def _bias_fwd(rel_bias, buckets, g):
    def body(rel_ref, bk_ref, o_ref):
        bk = bk_ref[...]
        for h in range(HEADS_PER_GROUP):
            acc = jnp.zeros(bk.shape, F32)
            for b in range(N_REL_BUCKETS):
                acc = jnp.where(bk == b, rel_ref[b, g * HEADS_PER_GROUP + h], acc)
            o_ref[h] = acc

    return pl.pallas_call(
        body, in_specs=[pl.BlockSpec(memory_space=pltpu.SMEM), pl.BlockSpec(memory_space=pltpu.VMEM)],
        out_specs=pl.BlockSpec(memory_space=pltpu.VMEM),
        out_shape=jax.ShapeDtypeStruct((HEADS_PER_GROUP, ATT_BLOCK, 2 * ATT_BLOCK), F32),
        name=f"rel_bias_fwd{g}")(rel_bias, buckets)


def _bias_bwd(dbias, buckets, g):
    def body(db_ref, bk_ref, o_ref):
        bk = bk_ref[...]
        row = lax.broadcasted_iota(jnp.int32, (N_REL_BUCKETS, LANES), 0)
        col = lax.broadcasted_iota(jnp.int32, (N_REL_BUCKETS, LANES), 1)
        acc = jnp.zeros((N_REL_BUCKETS, LANES), F32)
        for h in range(HEADS_PER_GROUP):
            d = db_ref[h]
            for b in range(N_REL_BUCKETS):
                s = jnp.sum(jnp.sum(jnp.where(bk == b, d, 0.0), axis=1, keepdims=True), axis=0, keepdims=True)
                acc = acc + jnp.where((row == b) & (col == g * HEADS_PER_GROUP + h), s, 0.0)
        o_ref[...] = acc

    return pl.pallas_call(
        body, in_specs=[pl.BlockSpec(memory_space=pltpu.VMEM), pl.BlockSpec(memory_space=pltpu.VMEM)],
        out_specs=pl.BlockSpec(memory_space=pltpu.VMEM),
        out_shape=jax.ShapeDtypeStruct((N_REL_BUCKETS, LANES), F32), name=f"rel_bias_bwd{g}")(dbias, buckets)


_NT = (((1,), (1,)), ((), ()))
_TN = (((0,), (0,)), ((), ()))
_QKV_BLOCKS = 3 * QKV_WIDTH // WIDTH_A


def _band_mask(n_is_first):
    i = lax.broadcasted_iota(jnp.int32, (ATT_BLOCK, 2 * ATT_BLOCK), 0)
    kk = lax.broadcasted_iota(jnp.int32, (ATT_BLOCK, 2 * ATT_BLOCK), 1)
    return (kk >= i) & (kk <= i + ATT_STEPS) & ((kk >= ATT_BLOCK) | jnp.logical_not(n_is_first))


def _head(ref, h):
    return ref[:, h * HEAD_DIM:(h + 1) * HEAD_DIM]


def _attn_specs(g, d):
    blk = (ATT_BLOCK, WIDTH_A)
    q = pl.BlockSpec(blk, lambda c, n: (n, c * _QKV_BLOCKS + g))
    kp = pl.BlockSpec(blk, lambda c, n: (jnp.maximum(n - 1, 0), c * _QKV_BLOCKS + 3 + g))
    kc = pl.BlockSpec(blk, lambda c, n: (n, c * _QKV_BLOCKS + 3 + g))
    vp = pl.BlockSpec(blk, lambda c, n: (jnp.maximum(n - 1, 0), c * _QKV_BLOCKS + 6 + g))
    vc = pl.BlockSpec(blk, lambda c, n: (n, c * _QKV_BLOCKS + 6 + g))
    return [q, kp, kc, vp, vc]


def _attn_fwd(qkv, bias, g, d, comm):
    t = qkv.shape[0]
    lq = t // d
    nb = lq // ATT_BLOCK
    scale = HEAD_DIM ** -0.5

    def body(q_ref, kp_ref, kc_ref, vp_ref, vc_ref, b_ref, o_ref, l_ref):
        mask = _band_mask(pl.program_id(1) == 0)
        for h in range(HEADS_PER_GROUP):
            qh = _head(q_ref, h).astype(BF16)
            kh = jnp.concatenate([_head(kp_ref, h), _head(kc_ref, h)], axis=0).astype(BF16)
            vh = jnp.concatenate([_head(vp_ref, h), _head(vc_ref, h)], axis=0).astype(BF16)
            s = lax.dot_general(qh, kh, _NT, preferred_element_type=F32) * scale + b_ref[h]
            s = jnp.where(mask, s, NEG_INF)
            m = jnp.max(s, axis=1, keepdims=True)
            p = jnp.exp(s - m)
            den = jnp.sum(p, axis=1, keepdims=True)
            o = jnp.dot(p.astype(BF16), vh, preferred_element_type=F32) / den
            o_ref[:, h * HEAD_DIM:(h + 1) * HEAD_DIM] = o
            l_ref[:, h * HEAD_DIM:(h + 1) * HEAD_DIM] = jnp.broadcast_to(m + jnp.log(den), (ATT_BLOCK, HEAD_DIM))

    out_spec = pl.BlockSpec((ATT_BLOCK, WIDTH_A), lambda c, n: (n, c))
    (o, lse), ride_outs = _pallas(
        body, grid=(d, nb),
        in_specs=_attn_specs(g, d) + [pl.BlockSpec(bias.shape, _zero_map(3))],
        out_specs=[out_spec, out_spec],
        out_shape=[jax.ShapeDtypeStruct((lq, d * WIDTH_A), F32)] * 2,
        args=[*([qkv.reshape(lq, d * 3 * QKV_WIDTH)] * 5), bias],
        semantics=("parallel", "parallel"), rides=[comm.ride(f"attn_fwd{g}")], name=f"attn_fwd{g}")
    comm.took(ride_outs)
    return o.reshape(t, WIDTH_A), lse.reshape(t, WIDTH_A)


def _attn_bwd(qkv, bias, do, lse, corr, g, d):
    t = qkv.shape[0]
    lq = t // d
    nb = lq // ATT_BLOCK
    scale = HEAD_DIM ** -0.5

    def body(k_ref, v_ref, q0_ref, q1_ref, do0_ref, do1_ref, l0_ref, l1_ref, c0_ref, c1_ref, b_ref,
             dq_ref, dk_ref, dv_ref, db_ref, dq_prev):
        c, j = pl.program_id(0), pl.program_id(1)

        @pl.when((c == 0) & (j == 0))
        def _():
            db_ref[...] = jnp.zeros_like(db_ref)

        @pl.when(j == 0)
        def _():
            dq_prev[...] = jnp.zeros_like(dq_prev)

        i = lax.broadcasted_iota(jnp.int32, (ATT_BLOCK, ATT_BLOCK), 0)
        kk = lax.broadcasted_iota(jnp.int32, (ATT_BLOCK, ATT_BLOCK), 1)
        mask0 = kk <= i
        mask1 = (kk >= i) & (j + 1 < nb)
        for h in range(HEADS_PER_GROUP):
            kh = _head(k_ref, h).astype(BF16)
            vh = _head(v_ref, h).astype(BF16)
            cols = slice(h * HEAD_DIM, (h + 1) * HEAD_DIM)
            dk = jnp.zeros((ATT_BLOCK, HEAD_DIM), F32)
            dv = jnp.zeros((ATT_BLOCK, HEAD_DIM), F32)
            dq_parts = []
            parts = ((q0_ref, do0_ref, l0_ref, c0_ref, mask0, ATT_BLOCK), (q1_ref, do1_ref, l1_ref, c1_ref, mask1, 0))
            for q_ref, do_ref, l_ref, c_ref, mask, off in parts:
                qh = _head(q_ref, h).astype(BF16)
                doh = _head(do_ref, h).astype(BF16)
                s = lax.dot_general(qh, kh, _NT, preferred_element_type=F32) * scale + b_ref[h, :, off:off + ATT_BLOCK]
                s = jnp.where(mask, s, NEG_INF)
                p = jnp.exp(s - l_ref[:, h * HEAD_DIM:h * HEAD_DIM + 1])
                dp = lax.dot_general(doh, vh, _NT, preferred_element_type=F32)
                ds = p * (dp + c_ref[:, h * HEAD_DIM:h * HEAD_DIM + 1])
                dsb = ds.astype(BF16)
                dv = dv + lax.dot_general(p.astype(BF16), doh, _TN, preferred_element_type=F32)
                dk = dk + lax.dot_general(dsb, qh, _TN, preferred_element_type=F32)
                dq_parts.append(jnp.dot(dsb, kh, preferred_element_type=F32))
                db_ref[h, :, off:off + ATT_BLOCK] += ds
            dk_ref[:, cols] = dk * scale
            dv_ref[:, cols] = dv
            dq_ref[:, cols] = (dq_prev[:, cols] + dq_parts[0]) * scale
            dq_prev[:, cols] = dq_parts[1]

    blk = (ATT_BLOCK, WIDTH_A)
    nxt = lambda n: jnp.minimum(n + 1, nb - 1)
    k_spec = pl.BlockSpec(blk, lambda c, n: (n, c * _QKV_BLOCKS + 3 + g))
    v_spec = pl.BlockSpec(blk, lambda c, n: (n, c * _QKV_BLOCKS + 6 + g))
    q0_spec = pl.BlockSpec(blk, lambda c, n: (n, c * _QKV_BLOCKS + g))
    q1_spec = pl.BlockSpec(blk, lambda c, n: (nxt(n), c * _QKV_BLOCKS + g))
    r0 = pl.BlockSpec(blk, lambda c, n: (n, c))
    r1 = pl.BlockSpec(blk, lambda c, n: (nxt(n), c))
    view = lambda a: a.reshape(lq, d * WIDTH_A)
    qv = qkv.reshape(lq, d * 3 * QKV_WIDTH)
    dq, dk, dv, dbias = pl.pallas_call(
        body, grid=(d, nb),
        in_specs=[k_spec, v_spec, q0_spec, q1_spec, r0, r1, r0, r1, r0, r1, pl.BlockSpec(bias.shape, _zero_map(3))],
        out_specs=[r0, r0, r0, pl.BlockSpec(bias.shape, _zero_map(3))],
        out_shape=[jax.ShapeDtypeStruct((lq, d * WIDTH_A), F32)] * 3 + [jax.ShapeDtypeStruct(bias.shape, F32)],
        scratch_shapes=[pltpu.VMEM(blk, F32)],
        compiler_params=_cparams("arbitrary", "arbitrary"), name=f"attn_bwd{g}",
    )(qv, qv, qv, qv, view(do), view(do), view(lse), view(lse), view(corr), view(corr), bias)
    return dq.reshape(t, WIDTH_A), dk.reshape(t, WIDTH_A), dv.reshape(t, WIDTH_A), dbias


def _tril_mask():
    r = lax.broadcasted_iota(jnp.int32, (CHUNK, CHUNK), 0)
    c = lax.broadcasted_iota(jnp.int32, (CHUNK, CHUNK), 1)
    return c <= r


def _gmlp_fwd(zb, ln_g, ln_b, w_s, b_s_t):
    t = zb.shape[0]
    tr = _pick(t, 2 * CHUNK, CHUNK)

    def body(z_ref, g_ref, b_ref, ws_ref, bs_ref, o_ref):
        tri = _tril_mask()
        z = jax.nn.gelu(z_ref[...])
        u = z[:, :WIDTH_B]
        vn = _ln(z[:, WIDTH_B:], g_ref[...], b_ref[...])
        for ch in range(tr // CHUNK):
            rows = slice(ch * CHUNK, (ch + 1) * CHUNK)
            for gi in range(N_GROUPS_B):
                cols = slice(gi * CHUNK, (gi + 1) * CHUNK)
                w = jnp.where(tri, ws_ref[gi], 0.0).astype(BF16)
                mixed = jnp.dot(w, vn[rows, cols].astype(BF16), preferred_element_type=F32) + bs_ref[:, gi:gi + 1]
                o_ref[rows, cols] = (u[rows, cols] * mixed).astype(o_ref.dtype)

    return pl.pallas_call(
        body, grid=(t // tr,),
        in_specs=[pl.BlockSpec((tr, 2 * WIDTH_B), lambda i: (i, 0)), pl.BlockSpec(ln_g.shape, _zero_map(2)),
                  pl.BlockSpec(ln_b.shape, _zero_map(2)), pl.BlockSpec(w_s.shape, _zero_map(3)),
                  pl.BlockSpec(b_s_t.shape, _zero_map(2))],
        out_specs=pl.BlockSpec((tr, WIDTH_B), lambda i: (i, 0)),
        out_shape=jax.ShapeDtypeStruct((t, WIDTH_B), BF16),
        compiler_params=_cparams("parallel"), name="gmlp_fwd")(zb, ln_g, ln_b, w_s, b_s_t)


def _gmlp_bwd(zb, dyb, ln_g, ln_b, w_s, b_s_t, group_sel):
    t = zb.shape[0]
    tr = _pick(t, 2 * CHUNK, CHUNK)

    def body(z_ref, dy_ref, g_ref, b_ref, ws_ref, bs_ref, sel_ref, dz_ref, dzs_ref, dg_ref, db_ref, dws_ref, dbs_ref,
             du_s, dvn_s, dm_s):
        @pl.when(pl.program_id(0) == 0)
        def _():
            dzs_ref[...] = jnp.zeros_like(dzs_ref)
            dg_ref[...] = jnp.zeros_like(dg_ref)
            db_ref[...] = jnp.zeros_like(db_ref)
            dws_ref[...] = jnp.zeros_like(dws_ref)
            dbs_ref[...] = jnp.zeros_like(dbs_ref)

        tri = _tril_mask()
        z, gelu_vjp = jax.vjp(jax.nn.gelu, z_ref[...])
        u = z[:, :WIDTH_B]
        vn, ln_vjp = jax.vjp(_ln, z[:, WIDTH_B:], g_ref[...], b_ref[...])
        dy = dy_ref[...]
        for ch in range(tr // CHUNK):
            rows = slice(ch * CHUNK, (ch + 1) * CHUNK)
            for gi in range(N_GROUPS_B):
                cols = slice(gi * CHUNK, (gi + 1) * CHUNK)
                w = jnp.where(tri, ws_ref[gi], 0.0).astype(BF16)
                vg = vn[rows, cols].astype(BF16)
                mixed = jnp.dot(w, vg, preferred_element_type=F32) + bs_ref[:, gi:gi + 1]
                dyg = dy[rows, cols]
                dm = dyg * u[rows, cols]
                dmb = dm.astype(BF16)
                du_s[rows, cols] = dyg * mixed
                dm_s[rows, cols] = dm
                dvn_s[rows, cols] = lax.dot_general(w, dmb, _TN, preferred_element_type=F32)
                dws_ref[gi] += jnp.where(tri, lax.dot_general(dmb, vg, _NT, preferred_element_type=F32), 0.0)
            dbs_ref[...] += jnp.dot(dm_s[rows, :], sel_ref[...], precision=lax.Precision.HIGHEST,
                                    preferred_element_type=F32)
        dv, dg, db = ln_vjp(dvn_s[...])
        dg_ref[...] += dg
        db_ref[...] += db
        dz = gelu_vjp(jnp.concatenate([du_s[...], dv], axis=1))[0]
        dz_ref[...] = dz.astype(dz_ref.dtype)
        dzs_ref[...] += jnp.sum(dz, axis=0, keepdims=True)

    full = lambda a: pl.BlockSpec(a.shape, _zero_map(a.ndim))
    return pl.pallas_call(
        body, grid=(t // tr,),
        in_specs=[pl.BlockSpec((tr, 2 * WIDTH_B), lambda i: (i, 0)), pl.BlockSpec((tr, WIDTH_B), lambda i: (i, 0)),
                  full(ln_g), full(ln_b), full(w_s), full(b_s_t), full(group_sel)],
        out_specs=[pl.BlockSpec((tr, 2 * WIDTH_B), lambda i: (i, 0)), pl.BlockSpec((1, 2 * WIDTH_B), _zero_map(2)),
                   full(ln_g), full(ln_b), full(w_s), pl.BlockSpec((CHUNK, LANES), _zero_map(2))],
        out_shape=[jax.ShapeDtypeStruct((t, 2 * WIDTH_B), BF16), jax.ShapeDtypeStruct((1, 2 * WIDTH_B), F32),
                   jax.ShapeDtypeStruct(ln_g.shape, F32),
                   jax.ShapeDtypeStruct(ln_b.shape, F32), jax.ShapeDtypeStruct(w_s.shape, F32),
                   jax.ShapeDtypeStruct((CHUNK, LANES), F32)],
        scratch_shapes=[pltpu.VMEM((tr, WIDTH_B), F32)] * 3,
        compiler_params=_cparams("arbitrary"), name="gmlp_bwd")(zb, dyb, ln_g, ln_b, w_s, b_s_t, group_sel)


def _s5_disc(lr, li, ldt, br_t, bi_t):
    dt = jnp.exp(ldt)
    mag = jnp.exp(lr * dt)
    ab_re = mag * jnp.cos(li * dt)
    ab_im = mag * jnp.sin(li * dt)
    nrm = lr * lr + li * li
    cr = ((ab_re - 1.0) * lr + ab_im * li) / nrm
    ci = (ab_im * lr - (ab_re - 1.0) * li) / nrm
    return ab_re, ab_im, cr * br_t - ci * bi_t, cr * bi_t + ci * br_t


def _vmem_call(fn, args, out_shape, name):
    def body(*refs):
        res = fn(*[r[...] for r in refs[:len(args)]])
        for o, v in zip(refs[len(args):], res):
            o[...] = v

    vm = pl.BlockSpec(memory_space=pltpu.VMEM)
    return pl.pallas_call(body, in_specs=[vm] * len(args), out_specs=[vm] * len(out_shape),
                          out_shape=out_shape, name=name)(*args)


def _s5_disc_fwd(lr, li, ldt, br_t, bi_t):
    s1 = jax.ShapeDtypeStruct(lr.shape, F32)
    s2 = jax.ShapeDtypeStruct(br_t.shape, F32)
    return _vmem_call(_s5_disc, [lr, li, ldt, br_t, bi_t], [s1, s1, s2, s2], "s5_disc_fwd")


def _s5_disc_bwd(lr, li, ldt, br_t, bi_t, cts):
    def fn(lr, li, ldt, br_t, bi_t, d0, d1, d2, d3):
        _, vjp = jax.vjp(_s5_disc, lr, li, ldt, br_t, bi_t)
        return vjp((d0, d1, d2, d3))

    shp = [jax.ShapeDtypeStruct(a.shape, F32) for a in (lr, li, ldt, br_t, bi_t)]
    return _vmem_call(fn, [lr, li, ldt, br_t, bi_t, *cts], shp, "s5_disc_bwd")


_SCAN_ROWS = SSM_COLS // LANES
_SCAN_CHUNK = 128
_SSM_IN = SSM_PACK * SSM_GROUP
_SSM_ST = SSM_PACK * SSM_STATE


def _packed_in(xb, m_ref):
    return jnp.concatenate([jnp.dot(xb[:, j * _SSM_IN:(j + 1) * _SSM_IN], m_ref[j], preferred_element_type=F32)
                            for j in range(N_SSM_BLOCKS)], axis=1)


def _packed_out(xb, m_ref):
    return jnp.concatenate([lax.dot_general(xb[:, j * _SSM_ST:(j + 1) * _SSM_ST], m_ref[j], _NT, preferred_element_type=F32)
                            for j in range(N_SSM_BLOCKS)], axis=1)


def _s5_fwd(uc, mats, are, aim, d_skip, comm):
    t = uc.shape[0]
    tc = _pick(t, _SCAN_CHUNK, SUBLANES)

    def body(u_ref, br_ref, bi_ref, cr_ref, ci_ref, ar_ref, ai_ref, d_ref, xr_ref, xi_ref, ys_ref, ycp_ref,
             sr, si, st_ref):
        @pl.when(pl.program_id(0) == 0)
        def _():
            st_ref[...] = jnp.zeros_like(st_ref)

        u = u_ref[...]
        ub = u.astype(BF16)
        sr[...] = _packed_in(ub, br_ref).reshape(tc, _SCAN_ROWS, LANES)
        si[...] = _packed_in(ub, bi_ref).reshape(tc, _SCAN_ROWS, LANES)
        ar, ai = ar_ref[...], ai_ref[...]

        def step(i, carry):
            xr, xi = carry
            nr = ar * xr - ai * xi + sr[i]
            ni = ar * xi + ai * xr + si[i]
            sr[i] = nr
            si[i] = ni
            return nr, ni

        xr, xi = lax.fori_loop(0, tc, step, (st_ref[0], st_ref[1]), unroll=8)
        st_ref[0] = xr
        st_ref[1] = xi
        x_re = sr[...].reshape(tc, SSM_COLS)
        x_im = si[...].reshape(tc, SSM_COLS)
        xr_ref[...] = x_re
        xi_ref[...] = x_im
        ys = _packed_out(x_re.astype(BF16), cr_ref) - _packed_out(x_im.astype(BF16), ci_ref) + d_ref[...] * u
        ys_ref[...] = ys
        ycp_ref[...] = jax.nn.gelu(ys)

    row = lambda w: pl.BlockSpec((tc, w), lambda i: (i, 0))
    mat = pl.BlockSpec(mats[0].shape, _zero_map(3))
    par = pl.BlockSpec((_SCAN_ROWS, LANES), _zero_map(2))
    wide, narrow = jax.ShapeDtypeStruct((t, SSM_COLS), F32), jax.ShapeDtypeStruct((t, WIDTH_C), F32)
    outs, ride_outs = _pallas(
        body, grid=(t // tc,), in_specs=[row(WIDTH_C), mat, mat, mat, mat, par, par, pl.BlockSpec(d_skip.shape, _zero_map(2))],
        out_specs=[row(SSM_COLS), row(SSM_COLS), row(WIDTH_C), row(WIDTH_C)], out_shape=[wide, wide, narrow, narrow],
        args=[uc, *mats, are, aim, d_skip],
        scratch=[pltpu.VMEM((tc, _SCAN_ROWS, LANES), F32)] * 2 + [pltpu.VMEM((2, _SCAN_ROWS, LANES), F32)],
        semantics=("arbitrary",), rides=[comm.ride("s5_fwd")], name="s5_fwd")
    comm.took(ride_outs)
    return outs


def _s5_bwd(dys, duc_skip, uc, xr, xi, mats, are, aim):
    t = dys.shape[0]
    tc = _pick(t, _SCAN_CHUNK, SUBLANES)
    nc = t // tc

    def body(dy_ref, ds_ref, u_ref, xr_ref, xi_ref, pr_ref, pi_ref, br_ref, bi_ref, cr_ref, ci_ref, ar_ref, ai_ref,
             du_ref, dbr_ref, dbi_ref, dcr_ref, dci_ref, dar_ref, dai_ref, gr, gi, x3r, x3i, st_ref):
        step_id = pl.program_id(0)

        @pl.when(step_id == 0)
        def _():
            st_ref[...] = jnp.zeros_like(st_ref)
            for ref in (dbr_ref, dbi_ref, dcr_ref, dci_ref, dar_ref, dai_ref):
                ref[...] = jnp.zeros_like(ref)

        dyb = dy_ref[...].astype(BF16)
        x_re, x_im = xr_ref[...], xi_ref[...]
        gr[...] = _packed_in(dyb, cr_ref).reshape(tc, _SCAN_ROWS, LANES)
        gi[...] = (-_packed_in(dyb, ci_ref)).reshape(tc, _SCAN_ROWS, LANES)
        x3r[...] = x_re.reshape(tc, _SCAN_ROWS, LANES)
        x3i[...] = x_im.reshape(tc, _SCAN_ROWS, LANES)
        ar, ai = ar_ref[...], ai_ref[...]

        def update(i, carry, pxr, pxi):
            g_r, g_i, dar, dai = carry
            ngr = gr[i] + ar * g_r + ai * g_i
            ngi = gi[i] - ai * g_r + ar * g_i
            gr[i] = ngr
            gi[i] = ngi
            return ngr, ngi, dar + ngr * pxr + ngi * pxi, dai - ngr * pxi + ngi * pxr

        def step(s, carry):
            i = tc - 1 - s
            return update(i, carry, x3r[i - 1], x3i[i - 1])

        zero = jnp.zeros((_SCAN_ROWS, LANES), F32)
        carry = lax.fori_loop(0, tc - 1, step, (st_ref[0], st_ref[1], zero, zero), unroll=8)
        has_prev = (step_id < nc - 1).astype(F32)
        last = SUBLANES - 1
        p_re = pr_ref[last:, :].reshape(1, _SCAN_ROWS, LANES)[0] * has_prev
        p_im = pi_ref[last:, :].reshape(1, _SCAN_ROWS, LANES)[0] * has_prev
        g_r, g_i, dar, dai = update(0, carry, p_re, p_im)
        st_ref[0] = g_r
        st_ref[1] = g_i
        dar_ref[...] += dar
        dai_ref[...] += dai

        g_re = gr[...].reshape(tc, SSM_COLS).astype(BF16)
        g_im = gi[...].reshape(tc, SSM_COLS).astype(BF16)
        du_ref[...] = ds_ref[...] + _packed_out(g_re, br_ref) + _packed_out(g_im, bi_ref)
        ub, xrb, xib = u_ref[...].astype(BF16), x_re.astype(BF16), x_im.astype(BF16)
        for j in range(N_SSM_BLOCKS):
            narrow, wide = slice(j * _SSM_IN, (j + 1) * _SSM_IN), slice(j * _SSM_ST, (j + 1) * _SSM_ST)
            dbr_ref[j] += lax.dot_general(ub[:, narrow], g_re[:, wide], _TN, preferred_element_type=F32)
            dbi_ref[j] += lax.dot_general(ub[:, narrow], g_im[:, wide], _TN, preferred_element_type=F32)
            dcr_ref[j] += lax.dot_general(dyb[:, narrow], xrb[:, wide], _TN, preferred_element_type=F32)
            dci_ref[j] -= lax.dot_general(dyb[:, narrow], xib[:, wide], _TN, preferred_element_type=F32)

    rev = lambda w: pl.BlockSpec((tc, w), lambda i: (nc - 1 - i, 0))
    prev = pl.BlockSpec((SUBLANES, SSM_COLS), lambda i: (jnp.maximum((nc - 1 - i) * (tc // SUBLANES) - 1, 0), 0))
    mat = pl.BlockSpec(mats[0].shape, _zero_map(3))
    par = pl.BlockSpec((_SCAN_ROWS, LANES), _zero_map(2))
    msh = jax.ShapeDtypeStruct(mats[0].shape, F32)
    psh = jax.ShapeDtypeStruct((_SCAN_ROWS, LANES), F32)
    return pl.pallas_call(
        body, grid=(nc,),
        in_specs=[rev(WIDTH_C), rev(WIDTH_C), rev(WIDTH_C), rev(SSM_COLS), rev(SSM_COLS), prev, prev,
                  mat, mat, mat, mat, par, par],
        out_specs=[rev(WIDTH_C), mat, mat, mat, mat, par, par],
        out_shape=[jax.ShapeDtypeStruct((t, WIDTH_C), F32), msh, msh, msh, msh, psh, psh],
        scratch_shapes=[pltpu.VMEM((tc, _SCAN_ROWS, LANES), F32)] * 4 + [pltpu.VMEM((2, _SCAN_ROWS, LANES), F32)],
        compiler_params=_cparams("arbitrary"), name="s5_bwd")(dys, duc_skip, uc, xr, xi, xr, xi, *mats, are, aim)


def _diag_blocks(a, b):
    return [(j, i, slice(i * a, (i + 1) * a), slice(i * b, (i + 1) * b))
            for j in range(N_SSM_BLOCKS) for i in range(SSM_PACK)]


def _block_diag(ms):
    _, a, b = ms[0].shape

    def body(*refs):
        for m_ref, o_ref in zip(refs[:len(ms)], refs[len(ms):]):
            o_ref[...] = jnp.zeros_like(o_ref)
            for j, i, rows, cols in _diag_blocks(a, b):
                o_ref[j, rows, cols] = m_ref[j * SSM_PACK + i].astype(o_ref.dtype)

    vm = pl.BlockSpec(memory_space=pltpu.VMEM)
    shape = jax.ShapeDtypeStruct((N_SSM_BLOCKS, SSM_PACK * a, SSM_PACK * b), BF16)
    return pl.pallas_call(body, in_specs=[vm] * len(ms), out_specs=[vm] * len(ms), out_shape=[shape] * len(ms),
                          name="s5_block_diag")(*ms)


def _block_diag_extract(ms, a, b):
    def body(*refs):
        for m_ref, o_ref in zip(refs[:len(ms)], refs[len(ms):]):
            for j, i, rows, cols in _diag_blocks(a, b):
                o_ref[j * SSM_PACK + i] = m_ref[j, rows, cols]

    vm = pl.BlockSpec(memory_space=pltpu.VMEM)
    shape = jax.ShapeDtypeStruct((N_GROUPS_C, a, b), F32)
    return pl.pallas_call(body, in_specs=[vm] * len(ms), out_specs=[vm] * len(ms), out_shape=[shape] * len(ms),
                          name="s5_block_diag_extract")(*ms)


def _exchange(src, *, gather, name):
    shape = src.shape if gather else src.shape[1:]

    def body(src_ref, out_ref, send_sems, recv_sems, local_sem):
        x, y, c = lax.axis_index("x"), lax.axis_index("y"), lax.axis_index("c")
        me = 4 * x + 2 * y + c
        copies = []
        for r in range(1, N_DEV):
            px = 1 - x if r & 4 else x
            py = 1 - y if r & 2 else y
            pc = 1 - c if r & 1 else c
            piece = src_ref if gather else src_ref.at[4 * px + 2 * py + pc]
            cp = pltpu.make_async_remote_copy(
                src_ref=piece, dst_ref=out_ref.at[me], send_sem=send_sems.at[r - 1], recv_sem=recv_sems.at[r - 1],
                device_id=(px, py, pc), device_id_type=pl.DeviceIdType.MESH)
            cp.start()
            copies.append(cp)
        mine = pltpu.make_async_copy(src_ref if gather else src_ref.at[me], out_ref.at[me], local_sem)
        mine.start()
        for cp in copies:
            cp.wait()
        mine.wait()

    hbm = pl.BlockSpec(memory_space=pl.ANY)
    return pl.pallas_call(
        body, in_specs=[hbm], out_specs=hbm, out_shape=jax.ShapeDtypeStruct((N_DEV,) + tuple(shape), src.dtype),
        scratch_shapes=[pltpu.SemaphoreType.DMA((N_DEV - 1,)), pltpu.SemaphoreType.DMA((N_DEV - 1,)),
                        pltpu.SemaphoreType.DMA(())],
        name=name)(src)


def _mesh_place():
    x, y, c = lax.axis_index("x"), lax.axis_index("y"), lax.axis_index("c")
    other_chips = [(1 - x, y), (x, 1 - y), (1 - x, 1 - y)]
    return x, y, c, other_chips


def _gather_layer(srcs, layer, name):
    n = len(srcs)

    def body(*refs):
        src = [r.at[layer] for r in refs[:n]]
        out = refs[n:2 * n]
        send_sems, recv_sems, local_sems = refs[2 * n:]
        x, y, c, chips = _mesh_place()
        me, sibling = (x, y, c), (x, y, 1 - c)

        def copy(t, k, block, to, from_src=False):
            slot = 4 * block[0] + 2 * block[1] + block[2]
            return pltpu.make_async_remote_copy(
                src_ref=src[t] if from_src else out[t].at[slot], dst_ref=out[t].at[slot],
                send_sem=send_sems.at[t, k], recv_sem=recv_sems.at[t, k], device_id=to, device_id_type=_MESH_ID)

        mine = [pltpu.make_async_copy(src[t], out[t].at[4 * x + 2 * y + c], local_sems.at[t]) for t in range(n)]
        for cp in mine:
            cp.start()
        first = []
        for t in range(n):
            first.append(copy(t, 0, me, sibling, True))
            first += [copy(t, 1 + j, me, (*chip, c), True) for j, chip in enumerate(chips)]
        for cp in first:
            cp.start()
        passed = []
        for j, chip in enumerate(chips):
            for t in range(n):
                copy(t, 1 + j, (*chip, c), me).wait_recv()
                fwd = copy(t, 4 + j, (*chip, c), sibling)
                fwd.start()
                passed.append(fwd)
        for t in range(n):
            copy(t, 0, sibling, me).wait_recv()
            for j, chip in enumerate(chips):
                copy(t, 4 + j, (*chip, 1 - c), me).wait_recv()
        for cp in first + passed:
            cp.wait_send()
        for cp in mine:
            cp.wait()

    return pl.pallas_call(
        body, in_specs=[_HBM] * n, out_specs=[_HBM] * n,
        out_shape=[jax.ShapeDtypeStruct((N_DEV,) + s.shape[1:], s.dtype) for s in srcs],
        scratch_shapes=[pltpu.SemaphoreType.DMA((n, N_DEV - 1)), pltpu.SemaphoreType.DMA((n, N_DEV - 1)),
                        pltpu.SemaphoreType.DMA((n,))],
        name=name)(*srcs)


def _scatter_pair(srcs, name):
    n = len(srcs)

    def body(*refs):
        src, out = refs[:n], refs[n:2 * n]
        send_sems, recv_sems = refs[2 * n:]
        x, y, c, _ = _mesh_place()
        copies = [pltpu.make_async_remote_copy(
            src_ref=src[t].at[:, 1 - c], dst_ref=out[t], send_sem=send_sems.at[t], recv_sem=recv_sems.at[t],
            device_id=(x, y, 1 - c), device_id_type=_MESH_ID) for t in range(n)]
        for cp in copies:
            cp.start()
        for cp in copies:
            cp.wait()

    return pl.pallas_call(
        body, in_specs=[_HBM] * n, out_specs=[_HBM] * n,
        out_shape=[jax.ShapeDtypeStruct((s.shape[0],) + s.shape[2:], s.dtype) for s in srcs],
        scratch_shapes=[pltpu.SemaphoreType.DMA((n,)), pltpu.SemaphoreType.DMA((n,))], name=name)(*srcs)


def _pair_add(src, recv, name):
    nchip, _, r, cdim = src.shape
    tr = _pick(r, 256, 2 * SUBLANES)
    core = lax.axis_index("c").astype(jnp.int32).reshape(1)

    def body(core_ref, s_ref, r_ref, o_ref):
        o_ref[...] = (s_ref[...].astype(F32) + r_ref[...].astype(F32)).astype(o_ref.dtype)

    grid_spec = pltpu.PrefetchScalarGridSpec(
        num_scalar_prefetch=1, grid=(nchip, r // tr),
        in_specs=[pl.BlockSpec((None, None, tr, cdim), lambda k, i, core_ref: (k, core_ref[0], i, 0)),
                  pl.BlockSpec((None, tr, cdim), lambda k, i, core_ref: (k, i, 0))],
        out_specs=pl.BlockSpec((None, tr, cdim), lambda k, i, core_ref: (k, i, 0)))
    return pl.pallas_call(body, grid_spec=grid_spec, out_shape=jax.ShapeDtypeStruct(recv.shape, recv.dtype),
                          compiler_params=_cparams("parallel", "parallel"), name=name)(core, src, recv)


def _scatter_chips(srcs, name):
    n = len(srcs)

    def body(*refs):
        src, out = refs[:n], refs[n:2 * n]
        send_sems, recv_sems, local_sems = refs[2 * n:]
        x, y, c, chips = _mesh_place()
        my_chip = 2 * x + y
        mine = [pltpu.make_async_copy(src[t].at[my_chip], out[t].at[my_chip], local_sems.at[t]) for t in range(n)]
        copies = [pltpu.make_async_remote_copy(
            src_ref=src[t].at[2 * chip[0] + chip[1]], dst_ref=out[t].at[my_chip],
            send_sem=send_sems.at[t, j], recv_sem=recv_sems.at[t, j], device_id=(*chip, c), device_id_type=_MESH_ID)
            for t in range(n) for j, chip in enumerate(chips)]
        for cp in mine + copies:
            cp.start()
        for cp in copies + mine:
            cp.wait()

    return pl.pallas_call(
        body, in_specs=[_HBM] * n, out_specs=[_HBM] * n,
        out_shape=[jax.ShapeDtypeStruct(s.shape, s.dtype) for s in srcs],
        scratch_shapes=[pltpu.SemaphoreType.DMA((n, 3)), pltpu.SemaphoreType.DMA((n, 3)), pltpu.SemaphoreType.DMA((n,))],
        name=name)(*srcs)


def _by_chip_and_core(g):
    return g.reshape(N_DEV // 2, 2, g.shape[0] // N_DEV, g.shape[1])


def _chip_sums(grads, tag, from_sibling=None):
    views = [_by_chip_and_core(g) for g in grads]
    if from_sibling is None:
        from_sibling = _scatter_pair(views, name="scatter_pair_" + tag)
    return [_pair_add(v, s, name="pair_add") for v, s in zip(views, from_sibling)]


def _pair_ride(grads):
    views = [_by_chip_and_core(g) for g in grads]

    def copies(in_refs, buf_refs, sems):
        x, y, c, _ = _mesh_place()
        return [pltpu.make_async_remote_copy(src_ref=src.at[:, 1 - c], dst_ref=dst, send_sem=sems.at[2 * t],
                                             recv_sem=sems.at[2 * t + 1], device_id=(x, y, 1 - c), device_id_type=_MESH_ID)
                for t, (src, dst) in enumerate(zip(in_refs, buf_refs))]

    def start(in_refs, buf_refs, sems):
        for cp in copies(in_refs, buf_refs, sems):
            cp.start()

    def finish(in_refs, buf_refs, sems):
        for cp in copies(in_refs, buf_refs, sems):
            cp.wait()

    created = tuple(jax.ShapeDtypeStruct((v.shape[0],) + v.shape[2:], v.dtype) for v in views)
    return _Ride(inputs=tuple(views), carried=(), created=created, n_sems=2 * len(views), start=start, finish=finish)


def _sum_chips(parts):
    return _rowmap(_sum_parts_fn, [], stacks=[parts], row_outs=[(parts.shape[2], F32)], tr=128, name="sum_chips")[0]


def _buffer_roles(kinds, bufs):
    carried = [k for k in kinds if k in bufs]
    return carried, [k for k in kinds if k not in bufs]


def _gather_ride(sends, forwards, bufs):
    carried, created = _buffer_roles(list(dict.fromkeys([s[0] for s in sends] + [f[0] for f in forwards])), bufs)
    shape_of = {s[0]: jax.ShapeDtypeStruct((N_DEV,) + s[1].shape, s[1].dtype) for s in sends}
    order = carried + created

    def copies(in_refs, buf_refs, sems):
        x, y, c, chips = _mesh_place()
        buf = dict(zip(order, buf_refs))
        out, s0 = [], 0
        for (kind, _, r0, nr), src in zip(sends, in_refs):
            mine, dst = src.at[pl.ds(r0, nr)], buf[kind].at[4 * x + 2 * y + c, pl.ds(r0, nr)]
            out.append(pltpu.make_async_copy(mine, dst, sems.at[s0 + 8]))
            for k, peer in enumerate([(x, y, 1 - c)] + [(*chip, c) for chip in chips]):
                out.append(pltpu.make_async_remote_copy(src_ref=mine, dst_ref=dst, send_sem=sems.at[s0 + k],
                                                        recv_sem=sems.at[s0 + 4 + k], device_id=peer, device_id_type=_MESH_ID))
            s0 += 9
        for kind, r0, nr in forwards:
            for j, chip in enumerate(chips):
                blk = buf[kind].at[4 * chip[0] + 2 * chip[1] + c, pl.ds(r0, nr)]
                out.append(pltpu.make_async_remote_copy(src_ref=blk, dst_ref=blk, send_sem=sems.at[s0 + j],
                                                        recv_sem=sems.at[s0 + 3 + j], device_id=(x, y, 1 - c),
                                                        device_id_type=_MESH_ID))
            s0 += 6
        return out

    def start(in_refs, buf_refs, sems):
        for cp in copies(in_refs, buf_refs, sems):
            cp.start()

    def finish(in_refs, buf_refs, sems):
        for cp in copies(in_refs, buf_refs, sems):
            cp.wait()

    ride = _Ride(inputs=tuple(s[1] for s in sends), carried=tuple(bufs[k] for k in carried),
                 created=tuple(shape_of[k] for k in created), n_sems=9 * len(sends) + 6 * len(forwards),
                 start=start, finish=finish)
    return ride, order


def _scatter_ride(pieces, bufs):
    carried, created = _buffer_roles(list(dict.fromkeys(p[0] for p in pieces)), bufs)
    shape_of = {p[0]: jax.ShapeDtypeStruct(p[1].shape, p[1].dtype) for p in pieces}
    order = carried + created

    def copies(in_refs, buf_refs, sems):
        x, y, c, chips = _mesh_place()
        buf = dict(zip(order, buf_refs))
        out, s0 = [], 0
        for (kind, _, r0, nr), src in zip(pieces, in_refs):
            dst = buf[kind].at[2 * x + y, pl.ds(r0, nr)]
            out.append(pltpu.make_async_copy(src.at[2 * x + y, pl.ds(r0, nr)], dst, sems.at[s0 + 6]))
            for j, chip in enumerate(chips):
                out.append(pltpu.make_async_remote_copy(
                    src_ref=src.at[2 * chip[0] + chip[1], pl.ds(r0, nr)], dst_ref=dst, send_sem=sems.at[s0 + j],
                    recv_sem=sems.at[s0 + 3 + j], device_id=(*chip, c), device_id_type=_MESH_ID))
            s0 += 7
        return out

    def start(in_refs, buf_refs, sems):
        for cp in copies(in_refs, buf_refs, sems):
            cp.start()

    def finish(in_refs, buf_refs, sems):
        for cp in copies(in_refs, buf_refs, sems):
            cp.wait()

    ride = _Ride(inputs=tuple(p[1] for p in pieces), carried=tuple(bufs[k] for k in carried),
                 created=tuple(shape_of[k] for k in created), n_sems=7 * len(pieces), start=start, finish=finish)
    return ride, order


GATHER_PLAN = (
    ("mm_in_qkv", (("w_in", 0, 3),)),
    ("mm_in_gl", (("w_in", 1, 3), ("w_glu", 0, 1), ("w_pa", 0, 1))),
    ("attn_fwd0", (("w_pb", 0, 1), ("w_ffn_out", 0, 4))),
    ("attn_fwd1", (("w_pc", 0, 1), ("w_ffn_out", 1, 4))),
    ("attn_fwd2", (("w_o", 0, 1),)),
    ("s5_fwd", (("w_ffn_out", 2, 4), ("w_ffn_out", 3, 4))),
    ("mm_ffn_in", (("w_in", 2, 3), ("w_ffn_in", 0, 2))),
    ("mm_ffn_out", (("w_ffn_in", 1, 2),)),
    ("norm2", ()),
)
EARLY_KINDS = ("w_ffn_out", "w_ffn_in")
LATE_KINDS = tuple(n for n in SHARDED if n not in EARLY_KINDS)
SCATTER_EARLY_PLAN = (
    ("mm_in_dw_qkv", (("w_ffn_out", 0, 2),)),
    ("mm_in_dx_qkv", (("w_ffn_out", 1, 2),)),
    ("mm_in_dw_gl", (("w_ffn_in", 0, 4), ("w_ffn_in", 3, 4))),
    ("mm_in_dx_gl", (("w_ffn_in", 1, 4), ("w_ffn_in", 2, 4))),
)
SCATTER_PLAN = (
    ("mm_ffn_out_dw", (("w_in", 0, 3),)),
    ("mm_ffn_out_dx", (("w_in", 1, 3),)),
    ("mm_ffn_in_dw", (("w_in", 2, 3), ("w_o", 0, 1), ("w_pa", 0, 1), ("w_pb", 0, 1), ("w_pc", 0, 1), ("w_glu", 0, 1))),
)


def _row_part(rows, part, parts):
    assert rows % (parts * 2 * SUBLANES) == 0
    return part * (rows // parts), rows // parts


class _Carried:
    def __init__(self, plan, blocks, make_ride, forwards_too):
        self.plan, self.blocks, self.make_ride, self.forwards_too = dict(plan), blocks, make_ride, forwards_too
        self.bufs, self.to_forward, self.order = {}, [], []

    def ride(self, host):
        if self.blocks is None or host not in self.plan:
            self.order = []
            return None
        sends = [(k, self.blocks[k], *_row_part(self.blocks[k].shape[-2], part, parts)) for k, part, parts in self.plan[host]]
        if self.forwards_too:
            ride, self.order = self.make_ride(sends, self.to_forward, self.bufs)
            self.to_forward = [(k, r0, nr) for k, _, r0, nr in sends]
        else:
            ride, self.order = self.make_ride(sends, self.bufs)
        return ride

    def took(self, ride_outs):
        for k, buf in zip(self.order, ride_outs[0] if ride_outs else []):
            self.bufs[k] = buf


class _OwnScatter:
    def __init__(self, enabled):
        self.enabled = enabled

    def pair_ride(self, grads):
        return _pair_ride(list(grads.values())) if self.enabled else None

    def steps(self, grads, ride_outs):
        if not self.enabled:
            return _Carried(SCATTER_EARLY_PLAN, None, _scatter_ride, forwards_too=False)
        sums = _chip_sums(list(grads.values()), "early", from_sibling=ride_outs[0])
        return _Carried(SCATTER_EARLY_PLAN, dict(zip(grads, sums)), _scatter_ride, forwards_too=False)


class _PrevScatter:
    PAIR_HOST = "norm2_bwd"

    def __init__(self, grads):
        self.grads = grads
        self.inner = _Carried(SCATTER_PLAN, None, _scatter_ride, forwards_too=False)
        self.pair_pending = False

    @property
    def bufs(self):
        return self.inner.bufs

    def ride(self, host):
        if self.grads is not None and host == self.PAIR_HOST:
            self.pair_pending = True
            return _pair_ride(list(self.grads.values()))
        return self.inner.ride(host)

    def took(self, ride_outs):
        if not self.pair_pending:
            return self.inner.took(ride_outs)
        self.pair_pending = False
        sums = _chip_sums(list(self.grads.values()), "late", from_sibling=ride_outs[0])
        self.inner = _Carried(SCATTER_PLAN, dict(zip(self.grads, sums)), _scatter_ride, forwards_too=False)


def _hosted(comm, fn, *args, name, **kwargs):
    res, ride_outs = fn(*args, name=name, rides=[comm.ride(name)], **kwargs)
    comm.took(ride_outs)
    return res


def _small_sizes(shapes):
    return [int(np.prod(shapes[n])) for n in SMALL]


def _pack_small(vals):
    flat = jnp.concatenate([vals[n].reshape(-1).astype(F32) for n in SMALL])
    rows = -(-flat.shape[0] // (LANES * N_DEV * SUBLANES)) * (N_DEV * SUBLANES)
    return jnp.pad(flat, (0, rows * LANES - flat.shape[0])).reshape(rows, LANES)


def _unpack_small(packed, shapes):
    flat = packed.reshape(-1)
    out, off = {}, 0
    for n, size in zip(SMALL, _small_sizes(shapes)):
        out[n] = flat[off:off + size].reshape(shapes[n])
        off += size
    return out


def _row(v):
    return v.reshape(1, -1)


def _layer_params(l, full, small):
    o1, o2, o3 = 3 * QKV_WIDTH, 3 * QKV_WIDTH + 2 * WIDTH_B, 3 * QKV_WIDTH + 2 * WIDTH_B + WIDTH_C
    b_in = small["b_in"][l]
    p = {
        "in_pieces": (("qkv", 0, o1), ("zb", o1, o2 - o1), ("uc", o2, o3 - o2), ("gl", o3, b_in.shape[0] - o3)),
        "b_qkv": _row(b_in[:o1]), "b_zb": _row(b_in[o1:o2]), "b_uc": _row(b_in[o2:o3]), "b_gl": _row(b_in[o3:]),
        "sgu_ln_g": _row(small["sgu_ln_g"][l]), "sgu_ln_b": _row(small["sgu_ln_b"][l]),
        "w_s": small["w_s"][l], "b_s_t": small["b_s"][l].T,
        "lam_re": small["lam_re"][l][:, None, :], "lam_im": small["lam_im"][l][:, None, :],
        "log_dt": small["log_dt"][l][:, None, None],
        "b_re_t": small["b_re"][l].transpose(0, 2, 1), "b_im_t": small["b_im"][l].transpose(0, 2, 1),
        "c_re": small["c_re"][l], "c_im": small["c_im"][l],
        "d_skip": _row(small["d_skip"][l]), "b_glu": _row(small["b_glu"][l]),
        "ln1_g": _row(small["ln1_g"][l]), "ln1_b": _row(small["ln1_b"][l]),
        "ln2_g": _row(small["ln2_g"][l]), "ln2_b": _row(small["ln2_b"][l]),
    }
    for n in SHARDED:
        p[n] = full[n]
    return p


def _twice(fn):
    def both(*args):
        y = fn(*args)
        return y, y
    return both


def _layer_fwd(x, xb, p, biases, comm):
    t, d = x.shape
    r = {"x": x, "xb": xb}
    for piece, off, n in p["in_pieces"]:
        r[piece] = _hosted(comm, _mm, xb, p["w_in"], tb=True, b_off=off, n=n, bias=p["b_" + piece],
                           out_dtype=BF16 if piece in ("qkv", "gl") else F32, name="mm_in_" + piece)
    ol = []
    for g, dil in enumerate(ATT_DILATIONS):
        ol += list(_attn_fwd(r["qkv"], biases[g], g, dil, comm))
    r["ol"] = ol
    r["ya"], r["ya_b"] = _rowmap(_twice(_combine), ol, row_outs=[(WIDTH_A, F32), (WIDTH_A, BF16)], tr=512,
                                 name="attn_combine")
    r["yb"] = _gmlp_fwd(r["zb"], p["sgu_ln_g"], p["sgu_ln_b"], p["w_s"], p["b_s_t"])
    ab_re, ab_im, bb_re_t, bb_im_t = _s5_disc_fwd(p["lam_re"], p["lam_im"], p["log_dt"], p["b_re_t"], p["b_im_t"])
    r["a_re"], r["a_im"] = ab_re.reshape(_SCAN_ROWS, LANES), ab_im.reshape(_SCAN_ROWS, LANES)
    r["s5_mats"] = _block_diag([bb_re_t, bb_im_t, p["c_re"], p["c_im"]])
    r["xr"], r["xi"], r["ys"], r["ycp"] = _s5_fwd(r["uc"], r["s5_mats"], r["a_re"], r["a_im"], p["d_skip"], comm)
    r["glin"] = _mm(r["ycp"], p["w_glu"], bias=p["b_glu"], name="mm_glu")
    r["yc"] = _rowmap(_glu, [r["ycp"], r["glin"]], row_outs=[(WIDTH_C, BF16)], tr=512, name="glu")[0]
    r["pa"] = _mm(r["ya_b"], p["w_pa"], tb=True, out_dtype=BF16, name="mm_pa")
    r["pb"] = _mm(r["yb"], p["w_pb"], tb=True, out_dtype=BF16, name="mm_pb")
    r["pc"] = _mm(r["yc"], p["w_pc"], tb=True, out_dtype=BF16, name="mm_pc")
    r["merged"] = _rowmap(_merge, [r["gl"], r["pa"], r["pb"], r["pc"]], row_outs=[(d, BF16)], name="merge")[0]
    r["mo"] = _hosted(comm, _mm, r["merged"], p["w_o"], name="mm_o")
    r["xm"], r["xm_b"] = _rowmap(_twice(_post_norm), [x, r["mo"]], consts=[p["ln1_g"], p["ln1_b"]],
                                 row_outs=[(d, F32), (d, BF16)], name="norm1")
    r["gate"], r["up"], r["act"] = _hosted(comm, _mm_swiglu, r["xm_b"], p["w_ffn_in"], name="mm_ffn_in")
    r["f"] = _hosted(comm, _mm, r["act"], p["w_ffn_out"], name="mm_ffn_out")
    out, out_b = _hosted(comm, _rowmap, _twice(_post_norm), [r["xm"], r["f"]], consts=[p["ln2_g"], p["ln2_b"]],
                         row_outs=[(d, F32), (d, BF16)], name="norm2")
    return out, out_b, r


def _layer_bwd(dout, r, p, biases, consts, comm, own):
    t, d = dout.shape
    gw, gs = {}, {}
    ffw = 2 * r["gate"].shape[1]
    dxm, df, gs["ln2_g"], gs["ln2_b"] = _hosted(
        comm, _rowmap, _post_norm_bwd, [r["xm"], r["f"], dout], consts=[p["ln2_g"], p["ln2_b"]],
        row_outs=[(d, F32), (d, BF16)], red_outs=[(1, d)] * 2, name="norm2_bwd")
    gw["w_ffn_out"] = _hosted(comm, _mm, r["act"], df, ta=True, out_dtype=BF16, name="mm_ffn_out_dw")
    dact = _hosted(comm, _mm, df, p["w_ffn_out"], tb=True, out_dtype=BF16, name="mm_ffn_out_dx")
    dgu = _rowmap(_swiglu_bwd, [r["gate"], r["up"], dact], row_outs=[(ffw, BF16)], tr=128, name="swiglu_bwd")[0]
    gw["w_ffn_in"] = _hosted(comm, _mm, dgu, r["xm_b"], ta=True, out_dtype=BF16, name="mm_ffn_in_dw")
    early_grads = {n: gw[n] for n in EARLY_KINDS}
    dxm, from_sibling = _mm(dgu, p["w_ffn_in"], add=dxm, name="mm_ffn_in_dx", rides=[own.pair_ride(early_grads)])
    early = own.steps(early_grads, from_sibling)
    dx, dmo, gs["ln1_g"], gs["ln1_b"] = _rowmap(
        _post_norm_bwd, [r["x"], r["mo"], dxm], consts=[p["ln1_g"], p["ln1_b"]],
        row_outs=[(d, F32), (d, BF16)], red_outs=[(1, d)] * 2, name="norm1_bwd")
    gw["w_o"] = _hosted(comm, _mm, r["merged"], dmo, ta=True, out_dtype=BF16, name="mm_o_dw")
    dmerged = _hosted(comm, _mm, dmo, p["w_o"], tb=True, name="mm_o_dx")
    dgl, dpa, dpb, dpc, db_gl = _rowmap(
        _merge_bwd, [r["gl"], r["pa"], r["pb"], r["pc"], dmerged],
        row_outs=[(3 * d, BF16), (d, BF16), (d, BF16), (d, BF16)], red_outs=[(1, 3 * d)], tr=128, name="merge_bwd")
    gw["w_pa"] = _mm(dpa, r["ya_b"], ta=True, out_dtype=BF16, name="mm_pa_dw")
    gw["w_pb"] = _mm(dpb, r["yb"], ta=True, out_dtype=BF16, name="mm_pb_dw")
    gw["w_pc"] = _mm(dpc, r["yc"], ta=True, out_dtype=BF16, name="mm_pc_dw")
    dya = _mm(dpa, p["w_pa"], name="mm_pa_dx")
    dyb = _mm(dpb, p["w_pb"], name="mm_pb_dx")
    dyc = _mm(dpc, p["w_pc"], name="mm_pc_dx")
    dycp, dglin, gs["b_glu"] = _rowmap(_glu_bwd, [r["ycp"], r["glin"], dyc], row_outs=[(WIDTH_C, F32), (WIDTH_C, BF16)],
                                       red_outs=[(1, WIDTH_C)], tr=512, name="glu_bwd")
    gw["w_glu"] = _mm(r["ycp"], dglin, ta=True, out_dtype=BF16, name="mm_glu_dw")
    dycp = _mm(dglin, p["w_glu"], tb=True, add=dycp, name="mm_glu_dx")
    dys, duc, gs["d_skip"] = _rowmap(_s5_out_bwd, [r["ys"], r["uc"], dycp], consts=[p["d_skip"]],
                                     row_outs=[(WIDTH_C, F32)] * 2, red_outs=[(1, WIDTH_C)], tr=512, name="s5_out_act_bwd")
    duc, d_bmat_re, d_bmat_im, d_cmat_re, d_cmat_im, da_re, da_im = _s5_bwd(
        dys, duc, r["uc"], r["xr"], r["xi"], r["s5_mats"], r["a_re"], r["a_im"])
    d_bb_re_t, d_bb_im_t, gs["c_re"], gs["c_im"] = _block_diag_extract(
        [d_bmat_re, d_bmat_im, d_cmat_re, d_cmat_im], SSM_GROUP, SSM_STATE)
    cts = (da_re.reshape(N_GROUPS_C, 1, SSM_STATE), da_im.reshape(N_GROUPS_C, 1, SSM_STATE), d_bb_re_t, d_bb_im_t)
    d_lr, d_li, d_ldt, d_br_t, d_bi_t = _s5_disc_bwd(p["lam_re"], p["lam_im"], p["log_dt"], p["b_re_t"], p["b_im_t"], cts)
    gs["lam_re"], gs["lam_im"], gs["log_dt"] = d_lr[:, 0, :], d_li[:, 0, :], d_ldt[:, 0, 0]
    gs["b_re"], gs["b_im"] = d_br_t.transpose(0, 2, 1), d_bi_t.transpose(0, 2, 1)
    dzb, db_zb, gs["sgu_ln_g"], gs["sgu_ln_b"], gs["w_s"], dbs_t = _gmlp_bwd(
        r["zb"], dyb, p["sgu_ln_g"], p["sgu_ln_b"], p["w_s"], p["b_s_t"], consts["group_sel"])
    gs["b_s"] = dbs_t[:, :N_GROUPS_B].T
    do_corr = _rowmap(_combine_bwd, r["ol"] + [r["ya"], dya], consts=[consts["head_ones"]],
                      row_outs=[(WIDTH_A, F32)] * 6, tr=512, name="attn_combine_bwd")
    dq, dk, dv, dbias = [], [], [], []
    for g, dil in enumerate(ATT_DILATIONS):
        do_g, corr_g, lse_g = do_corr[g], do_corr[3 + g], r["ol"][2 * g + 1]
        dq_g, dk_g, dv_g, db_g = _attn_bwd(r["qkv"], biases[g], do_g, lse_g, corr_g, g, dil)
        dq.append(dq_g)
        dk.append(dk_g)
        dv.append(dv_g)
        dbias.append(db_g)
    def cast_colsum(*pieces):
        a = pieces[0] if len(pieces) == 1 else jnp.concatenate(pieces, axis=1)
        return a, jnp.sum(a, axis=0, keepdims=True)

    dqkv, db_qkv = _rowmap(cast_colsum, dq + dk + dv, row_outs=[(3 * QKV_WIDTH, BF16)],
                           red_outs=[(1, 3 * QKV_WIDTH)], tr=256, name="cast_colsum_qkv")
    duc, db_uc = _rowmap(cast_colsum, [duc], row_outs=[(WIDTH_C, BF16)], red_outs=[(1, WIDTH_C)], tr=512,
                         name="cast_colsum_uc")
    dpieces = {"qkv": dqkv, "zb": dzb, "uc": duc, "gl": dgl}
    rows_in = p["w_in"].shape[0]
    dw_in = None
    for piece, off, n in p["in_pieces"]:
        dw_in = _hosted(early, _mm, dpieces[piece], r["xb"], ta=True, out_dtype=BF16, into=(rows_in, off, dw_in),
                        name="mm_in_dw_" + piece)
        dx = _hosted(early, _mm, dpieces[piece], p["w_in"], b_off=off, add=dx, name="mm_in_dx_" + piece)
    gw["w_in"] = dw_in
    gs["b_in"] = jnp.concatenate([db_qkv, db_zb, db_uc, db_gl], axis=1)[0]
    for n in ("sgu_ln_g", "sgu_ln_b", "d_skip", "b_glu", "ln1_g", "ln1_b", "ln2_g", "ln2_b"):
        gs[n] = gs[n][0]
    return dx, gw, gs, dbias, early.bufs


def _cast_bf16(w):
    w2 = w.reshape(-1, w.shape[-1])
    out = _rowmap(lambda a: a, [w2], row_outs=[(w2.shape[1], BF16)], tr=512, name="cast_bf16")[0]
    return out.reshape(w.shape)


def _static_consts():
    head_ones = np.kron(np.eye(HEADS_PER_GROUP, dtype=np.float32), np.ones((HEAD_DIM, HEAD_DIM), np.float32))
    group_sel = np.zeros((WIDTH_B, LANES), np.float32)
    group_sel[np.arange(WIDTH_B), np.arange(WIDTH_B) // CHUNK] = 1.0
    return {"head_ones": jnp.asarray(head_ones), "group_sel": jnp.asarray(group_sel)}


def _step(x, tgt, w, m, v):
    shapes = {n: w[n].shape for n in WEIGHTS}
    consts = _static_consts()
    mine_bf = {n: _cast_bf16(w[n].transpose(0, 2, 1) if n in TRANSPOSED else w[n]) for n in SHARDED}
    small = {n: w[n] for n in SMALL}
    buckets = [jnp.asarray(_bucket_table(dil)) for dil in ATT_DILATIONS]
    biases = [_bias_fwd(w["rel_bias"], buckets[g], g) for g in range(len(ATT_DILATIONS))]
    params, saved = [], []
    h, hb = _rowmap(_twice(lambda a: a), [x], row_outs=[(x.shape[1], F32), (x.shape[1], BF16)], name="cast_x")
    gathered = dict(zip(SHARDED, _gather_layer([mine_bf[n] for n in SHARDED], 0, name="gather_layer0")))
    for l in range(DEPTH):
        p = _layer_params(l, {n: g.reshape(-1, g.shape[2]) for n, g in gathered.items()}, small)
        ahead = _Carried(GATHER_PLAN, {n: mine_bf[n][l + 1] for n in SHARDED} if l + 1 < DEPTH else None,
                         _gather_ride, forwards_too=True)
        h, hb, r = _layer_fwd(h, hb, p, biases, ahead)
        gathered = ahead.bufs
        params.append(p)
        saved.append(r)
    dy, loss_part = _rowmap(_loss_fn, [h, tgt], row_outs=[(h.shape[1], F32)], red_outs=[(1, LANES)], name="loss")
    loss = lax.psum(loss_part[0, 0], MESH_AXES)
    g_mine, gs_layers = {n: [None] * DEPTH for n in SHARDED}, [None] * DEPTH
    dbias_sum = None
    behind = _PrevScatter(None)
    for l in reversed(range(DEPTH)):
        dy, gw, gs_layers[l], dbias, early_parts = _layer_bwd(
            dy, saved[l], params[l], biases, consts, behind, _OwnScatter(enabled=True))
        saved[l] = None
        for n in EARLY_KINDS:
            g_mine[n][l] = _sum_chips(early_parts[n])
        if behind.grads is not None:
            for n in LATE_KINDS:
                g_mine[n][l + 1] = _sum_chips(behind.bufs[n])
        behind = _PrevScatter({n: gw[n] for n in LATE_KINDS})
        if l == 0:
            sums = _chip_sums(list(behind.grads.values()), "late")
            for n, parts in zip(LATE_KINDS, _scatter_chips(sums, name="scatter_chips_layer0")):
                g_mine[n][0] = _sum_chips(parts)
        if dbias_sum is None:
            dbias_sum = dbias
        else:
            dbias_sum = [_rowmap(lambda a, b: a + b, [a.reshape(-1, 2 * ATT_BLOCK), b.reshape(-1, 2 * ATT_BLOCK)],
                                 row_outs=[(2 * ATT_BLOCK, F32)], name="dbias_add")[0].reshape(a.shape)
                         for a, b in zip(dbias_sum, dbias)]
    drel = [_bias_bwd(dbias_sum[g], buckets[g], g) for g in range(len(ATT_DILATIONS))]
    drel = _rowmap(lambda a, b, c: a + b + c, drel, row_outs=[(LANES, F32)], name="drel_add")[0]
    grad_small_local = {n: jnp.stack([gs_layers[l][n] for l in range(DEPTH)]) for n in SMALL if n != "rel_bias"}
    grad_small_local["rel_bias"] = drel[:, :shapes["rel_bias"][1]]
    out_g, out_d, out_m, out_v = {}, {}, {}, {}
    for n in SHARDED:
        g = jnp.stack(g_mine[n])
        out_g[n] = g.transpose(0, 2, 1) if n in TRANSPOSED else g
        cols = shapes[n][-1]
        res = _rowmap(_adamw, [a.reshape(-1, cols) for a in (w[n], out_g[n], m[n], v[n])],
                      row_outs=[(cols, F32)] * 3, tr=128, name="adamw_" + n)
        out_d[n], out_m[n], out_v[n] = [a.reshape(shapes[n]) for a in res]
    packed = _pack_small(grad_small_local)
    rows = packed.shape[0] // N_DEV
    parts = _exchange(packed.reshape(N_DEV, rows, LANES), gather=False, name="scatter_small")
    mine = _rowmap(_sum_parts_fn, [], stacks=[parts], row_outs=[(LANES, F32)], name="sum_small")[0]
    g_small = _exchange(mine, gather=True, name="gather_small").reshape(-1, LANES)
    out_g.update(_unpack_small(g_small, {n: shapes[n] for n in SMALL}))
    for n in SMALL:
        cols = shapes[n][-1]
        res = _rowmap(_adamw, [a.reshape(-1, cols) for a in (w[n], out_g[n], m[n], v[n])],
                      row_outs=[(cols, F32)] * 3, name="adamw_" + n)
        out_d[n], out_m[n], out_v[n] = [a.reshape(shapes[n]) for a in res]
    return loss, dy, out_g, out_d, out_m, out_v


def kernel(x, w_in, b_in, rel_bias, sgu_ln_g, sgu_ln_b, w_s, b_s, lam_re, lam_im, log_dt, b_re, b_im, c_re, c_im, d_skip, w_glu, b_glu, w_pa, w_pb, w_pc, w_o, ln1_g, ln1_b, w_ffn_in, w_ffn_out, ln2_g, ln2_b, loss_target, m_w_in, m_b_in, m_rel_bias, m_sgu_ln_g, m_sgu_ln_b, m_w_s, m_b_s, m_lam_re, m_lam_im, m_log_dt, m_b_re, m_b_im, m_c_re, m_c_im, m_d_skip, m_w_glu, m_b_glu, m_w_pa, m_w_pb, m_w_pc, m_w_o, m_ln1_g, m_ln1_b, m_w_ffn_in, m_w_ffn_out, m_ln2_g, m_ln2_b, v_w_in, v_b_in, v_rel_bias, v_sgu_ln_g, v_sgu_ln_b, v_w_s, v_b_s, v_lam_re, v_lam_im, v_log_dt, v_b_re, v_b_im, v_c_re, v_c_im, v_d_skip, v_w_glu, v_b_glu, v_w_pa, v_w_pb, v_w_pc, v_w_o, v_ln1_g, v_ln1_b, v_w_ffn_in, v_w_ffn_out, v_ln2_g, v_ln2_b):
    args = dict(locals())
    w = {n: args[n] for n in WEIGHTS}
    m = {n: args["m_" + n] for n in WEIGHTS}
    v = {n: args["v_" + n] for n in WEIGHTS}
    loss, dx, g, d, nm, nv = _step(x[0], loss_target[0], w, m, v)
    return (loss, dx[None], *[g[n] for n in WEIGHTS], *[d[n] for n in WEIGHTS],
            *[nm[n] for n in WEIGHTS], *[nv[n] for n in WEIGHTS])
```

```python
import functools
import math
from typing import Callable, NamedTuple

import numpy as np
import jax
import jax.numpy as jnp
from jax import lax
from jax.experimental import pallas as pl
from jax.experimental.pallas import tpu as pltpu

F32 = jnp.float32
BF16 = jnp.bfloat16

MESH_AXES = ("x", "y", "c")
N_DEV = 8
DEPTH = 4

ATT_DILATIONS = (1, 4, 16)
ATT_STEPS = 128
HEADS_PER_GROUP = 8
HEAD_DIM = 64
QKV_WIDTH = 1536
WIDTH_A = HEADS_PER_GROUP * HEAD_DIM
ATT_BLOCK = 128
N_REL_BUCKETS = 32
REL_MAX_DIST = 2048
NEG_INF = -1e30
CHUNK = 128
WIDTH_B = 768
N_GROUPS_B = 6
WIDTH_C = 768
SSM_GROUP = 16
N_GROUPS_C = 48
SSM_STATE = 64
SSM_PACK = 8
N_SSM_BLOCKS = N_GROUPS_C // SSM_PACK
SSM_COLS = N_GROUPS_C * SSM_STATE
ALPHA = (2 * DEPTH) ** 0.25

ADAM_LR = 0.001
ADAM_B1 = 0.9
ADAM_B2 = 0.999
ADAM_EPS = 1e-08
ADAM_WD = 0.01
ADAM_STEP = 10

LANES = 128
SUBLANES = 8
VMEM_LIMIT = 48 * 1024 * 1024

SHARDED = ("w_in", "w_glu", "w_pa", "w_pb", "w_pc", "w_o", "w_ffn_in", "w_ffn_out")
TRANSPOSED = ("w_in", "w_pa", "w_pb", "w_pc", "w_ffn_in")
SMALL = ("b_in", "rel_bias", "sgu_ln_g", "sgu_ln_b", "w_s", "b_s", "lam_re", "lam_im", "log_dt",
         "b_re", "b_im", "c_re", "c_im", "d_skip", "b_glu", "ln1_g", "ln1_b", "ln2_g", "ln2_b")
WEIGHTS = ("w_in", "b_in", "rel_bias", "sgu_ln_g", "sgu_ln_b", "w_s", "b_s", "lam_re", "lam_im",
           "log_dt", "b_re", "b_im", "c_re", "c_im", "d_skip", "w_glu", "b_glu", "w_pa", "w_pb",
           "w_pc", "w_o", "ln1_g", "ln1_b", "w_ffn_in", "w_ffn_out", "ln2_g", "ln2_b")


def _pick(dim, target, mult):
    best = None
    for t in range(mult, min(dim, target) + 1, mult):
        if dim % t == 0:
            best = t
    return dim if best is None else best


def _cparams(*sem):
    return pltpu.CompilerParams(dimension_semantics=sem, vmem_limit_bytes=VMEM_LIMIT)


def _zero_map(ndim):
    return lambda *_: (0,) * ndim


_HBM = pl.BlockSpec(memory_space=pl.ANY)
_MESH_ID = pl.DeviceIdType.MESH


class _Ride(NamedTuple):
    inputs: tuple
    carried: tuple
    created: tuple
    n_sems: int
    start: Callable
    finish: Callable


def _pallas(body, *, grid, in_specs, out_specs, out_shape, args, scratch=(), semantics, rides=(), aliases=None, name):
    rides = [r for r in rides if r is not None]
    n_in, n_out, n_scr = len(args), len(out_shape), len(scratch)
    r_args, r_shapes, aliases, spans = [], [], dict(aliases or {}), []
    for r in rides:
        i0, o0 = len(r_args), len(r_shapes)
        r_args += [*r.inputs, *r.carried]
        for k, a in enumerate(r.carried):
            aliases[n_in + i0 + len(r.inputs) + k] = n_out + o0 + k
        r_shapes += [jax.ShapeDtypeStruct(a.shape, a.dtype) for a in r.carried] + list(r.created)
        spans.append((i0, len(r.inputs), o0, len(r.carried) + len(r.created)))

    def full_body(*refs):
        host_in, ride_in = refs[:n_in], refs[n_in:n_in + len(r_args)]
        p = n_in + len(r_args)
        host_out, ride_out = refs[p:p + n_out], refs[p + n_out:p + n_out + len(r_shapes)]
        p += n_out + len(r_shapes)
        host_scr, ride_sems = refs[p:p + n_scr], refs[p + n_scr:]
        ids = [pl.program_id(k) for k in range(len(grid))]
        first = functools.reduce(jnp.logical_and, [i == 0 for i in ids])
        last = functools.reduce(jnp.logical_and, [i == g - 1 for i, g in zip(ids, grid)])

        def each(method):
            for r, (i0, ni, o0, no), sems in zip(rides, spans, ride_sems):
                getattr(r, method)(ride_in[i0:i0 + ni], ride_out[o0:o0 + no], sems)

        if rides:
            pl.when(first)(lambda: each("start"))
        body(*host_in, *host_out, *host_scr)
        if rides:
            pl.when(last)(lambda: each("finish"))

    if rides:
        semantics = ("arbitrary",) * len(grid)
    outs = pl.pallas_call(
        full_body, grid=grid, in_specs=list(in_specs) + [_HBM] * len(r_args),
        out_specs=list(out_specs) + [_HBM] * len(r_shapes), out_shape=list(out_shape) + r_shapes,
        scratch_shapes=list(scratch) + [pltpu.SemaphoreType.DMA((r.n_sems,)) for r in rides],
        input_output_aliases=aliases, compiler_params=_cparams(*semantics), name=name)(*args, *r_args)
    return outs[:n_out], [outs[n_out + o0:n_out + o0 + no] for _, _, o0, no in spans]


def _rowmap(fn, rows, consts=(), stacks=(), row_outs=(), red_outs=(), tr=256, name=None, rides=None):
    t = rows[0].shape[0] if rows else stacks[0].shape[1]
    dtypes = [a.dtype for a in (*rows, *stacks)] + [dt for _, dt in row_outs]
    packed = any(jnp.dtype(dt).itemsize < 4 for dt in dtypes)
    tr = _pick(t, tr, 2 * SUBLANES if packed else SUBLANES)
    n_r, n_c, n_s, n_o = len(rows), len(consts), len(stacks), len(row_outs)

    def body(*refs):
        ins = [r[...] for r in refs[:n_r + n_c + n_s]]
        outs = refs[n_r + n_c + n_s:n_r + n_c + n_s + n_o]
        reds = refs[n_r + n_c + n_s + n_o:]
        res = fn(*ins)
        if not isinstance(res, (tuple, list)):
            res = (res,)
        for o, v in zip(outs, res[:n_o]):
            o[...] = v.astype(o.dtype)
        if reds:
            @pl.when(pl.program_id(0) == 0)
            def _():
                for r in reds:
                    r[...] = jnp.zeros_like(r)
            for r, v in zip(reds, res[n_o:]):
                r[...] += v

    in_specs = [pl.BlockSpec((tr, r.shape[1]), lambda i: (i, 0)) for r in rows]
    in_specs += [pl.BlockSpec(c.shape, _zero_map(c.ndim)) for c in consts]
    in_specs += [pl.BlockSpec((s.shape[0], tr, s.shape[2]), lambda i: (0, i, 0)) for s in stacks]
    out_specs = [pl.BlockSpec((tr, w), lambda i: (i, 0)) for w, _ in row_outs]
    out_specs += [pl.BlockSpec(s, _zero_map(len(s))) for s in red_outs]
    out_shape = [jax.ShapeDtypeStruct((t, w), dt) for w, dt in row_outs]
    out_shape += [jax.ShapeDtypeStruct(s, F32) for s in red_outs]
    outs, ride_outs = _pallas(body, grid=(t // tr,), in_specs=in_specs, out_specs=out_specs, out_shape=out_shape,
                              args=[*rows, *consts, *stacks], semantics=("arbitrary",), rides=rides or (), name=name)
    return outs if rides is None else (outs, ride_outs)


MM_VMEM_BUDGET = 36 * 1024 * 1024


def _divisors(dim, mult, must_divide=0):
    out = [t for t in range(dim, 0, -mult) if t % mult == 0 and dim % t == 0 and must_divide % t == 0]
    return out or [dim]


def _mm_tiles(m, n, k, a_bytes, b_bytes, out_bytes, extra_bytes, ta, b_off_n, b_off_k, out_off, tm, tn):
    tms = _divisors(m, LANES if ta else SUBLANES, out_off)
    tm = next((t for t in tms if t <= tm), tms[-1])
    tns = [t for t in _divisors(n, LANES, b_off_n) if t <= tn] or [_divisors(n, LANES, b_off_n)[-1]]
    for tn_ in tns:
        for tk in _divisors(k, LANES, b_off_k):
            acc = 0 if tk == k else tm * tn_ * 4
            need = 2 * (tm * tk * a_bytes + tk * tn_ * b_bytes + tm * tn_ * (out_bytes + extra_bytes)) + acc
            if need <= MM_VMEM_BUDGET:
                return tm, tn_, tk
    return tm, tns[-1], _divisors(k, LANES, b_off_k)[-1]


def _mm(a, b, *, ta=False, tb=False, bias=None, add=None, out_dtype=F32, b_off=0, n=None, tm=1024, tn=1024, name=None,
        rides=None, into=None):
    k, m = a.shape if ta else a.shape[::-1]
    if tb:
        n = b.shape[0] if n is None else n
        assert b.shape[1] == k and b_off + n <= b.shape[0]
    else:
        n = b.shape[1]
        assert b_off + k <= b.shape[0]
    out_rows, out_off, out_buf = (m, 0, None) if into is None else into
    extra = 4 if add is not None else 0
    tm, tn, tk = _mm_tiles(m, n, k, a.dtype.itemsize, b.dtype.itemsize, jnp.dtype(out_dtype).itemsize, extra, ta,
                           b_off if tb else 0, 0 if tb else b_off, out_off, tm, tn)
    nk = k // tk
    off_n, off_k = (b_off // tn, 0) if tb else (0, b_off // tk)
    off_m = out_off // tm
    dims = (((0 if ta else 1,), (1 if tb else 0,)), ((), ()))

    def body(*refs):
        a_ref, b_ref = refs[0], refs[1]
        rest = list(refs[2:])
        bias_ref = rest.pop(0) if bias is not None else None
        add_ref = rest.pop(0) if add is not None else None
        o_ref = rest.pop(0)
        part = lax.dot_general(a_ref[...].astype(BF16), b_ref[...].astype(BF16), dims, preferred_element_type=F32)

        def finish(r):
            if bias_ref is not None:
                r = r + bias_ref[...]
            if add_ref is not None:
                r = r + add_ref[...]
            o_ref[...] = r.astype(o_ref.dtype)

        if nk == 1:
            finish(part)
        else:
            acc_ref = rest.pop(0)
            kk = pl.program_id(2)

            @pl.when(kk == 0)
            def _():
                acc_ref[...] = part

            @pl.when(kk > 0)
            def _():
                acc_ref[...] += part

            @pl.when(kk == nk - 1)
            def _():
                finish(acc_ref[...])

    a_spec = pl.BlockSpec((tk, tm), lambda i, j, q: (q, i)) if ta else pl.BlockSpec((tm, tk), lambda i, j, q: (i, q))
    if tb:
        b_spec = pl.BlockSpec((tn, tk), lambda i, j, q: (j + off_n, q))
    else:
        b_spec = pl.BlockSpec((tk, tn), lambda i, j, q: (q + off_k, j))
    in_specs, args = [a_spec, b_spec], [a, b]
    if bias is not None:
        in_specs.append(pl.BlockSpec((1, tn), lambda i, j, q: (0, j)))
        args.append(bias)
    if add is not None:
        in_specs.append(pl.BlockSpec((tm, tn), lambda i, j, q: (i, j)))
        args.append(add)
    aliases = {}
    if out_buf is not None:
        assert out_buf.shape == (out_rows, n) and out_buf.dtype == jnp.dtype(out_dtype)
        in_specs.append(_HBM)
        args.append(out_buf)
        aliases = {len(args) - 1: 0}

    def body_in_place(*refs):
        body(*refs[:len(args) - 1], *refs[len(args):])

    outs, ride_outs = _pallas(
        body if out_buf is None else body_in_place, grid=(m // tm, n // tn, nk), in_specs=in_specs,
        out_specs=[pl.BlockSpec((tm, tn), lambda i, j, q: (i + off_m, j))],
        out_shape=[jax.ShapeDtypeStruct((out_rows, n), out_dtype)], args=args,
        scratch=[] if nk == 1 else [pltpu.VMEM((tm, tn), F32)],
        semantics=("parallel", "parallel", "arbitrary"), rides=rides or (), aliases=aliases, name=name)
    return outs[0] if rides is None else (outs[0], ride_outs)


def _mm_swiglu(a, w_t, *, name, rides=None):
    m, k = a.shape
    f = w_t.shape[0] // 2
    tm, tn, tk = _mm_tiles(m, f, k, a.dtype.itemsize, 2 * w_t.dtype.itemsize, 3 * 2, 0, False, 0, 0, 0, 1024, 512)
    assert tk == k, "the fused activation needs the whole contraction in one block"

    def body(a_ref, g_ref, u_ref, gate_ref, up_ref, act_ref):
        av = a_ref[...].astype(BF16)
        gate = lax.dot_general(av, g_ref[...].astype(BF16), _NT, preferred_element_type=F32)
        up = lax.dot_general(av, u_ref[...].astype(BF16), _NT, preferred_element_type=F32)
        gate_ref[...] = gate.astype(gate_ref.dtype)
        up_ref[...] = up.astype(up_ref.dtype)
        act_ref[...] = _swiglu2(gate, up).astype(act_ref.dtype)

    out_spec = pl.BlockSpec((tm, tn), lambda i, j: (i, j))
    outs, ride_outs = _pallas(
        body, grid=(m // tm, f // tn),
        in_specs=[pl.BlockSpec((tm, k), lambda i, j: (i, 0)), pl.BlockSpec((tn, k), lambda i, j: (j, 0)),
                  pl.BlockSpec((tn, k), lambda i, j: (j + f // tn, 0))],
        out_specs=[out_spec] * 3, out_shape=[jax.ShapeDtypeStruct((m, f), BF16)] * 3, args=[a, w_t, w_t],
        semantics=("parallel", "parallel"), rides=rides or (), name=name)
    return outs if rides is None else (outs, ride_outs)


def _ln(x, g, b, eps=1e-5):
    mu = jnp.mean(x, axis=-1, keepdims=True)
    var = jnp.mean(jnp.square(x - mu), axis=-1, keepdims=True)
    return (x - mu) * lax.rsqrt(var + eps) * g + b


def _post_norm(x, f, g, b):
    return _ln(ALPHA * x + f, g, b)


def _post_norm_bwd(x, f, dy, g, b):
    _, vjp = jax.vjp(_post_norm, x, f, g, b)
    return vjp(dy)


def _merge3(g0, g1, g2, pa, pb, pc):
    return jax.nn.sigmoid(g0) * pa + jax.nn.sigmoid(g1) * pb + jax.nn.sigmoid(g2) * pc


def _merge_args(gl, pa, pb, pc):
    d = pa.shape[1]
    return [a.astype(F32) for a in (gl[:, :d], gl[:, d:2 * d], gl[:, 2 * d:], pa, pb, pc)]


def _merge(gl, pa, pb, pc):
    return _merge3(*_merge_args(gl, pa, pb, pc))


def _merge_bwd(gl, pa, pb, pc, dm):
    _, vjp = jax.vjp(_merge3, *_merge_args(gl, pa, pb, pc))
    d0, d1, d2, dpa, dpb, dpc = vjp(dm)
    dgl = jnp.concatenate([d0, d1, d2], axis=1)
    return dgl, dpa, dpb, dpc, jnp.sum(dgl, axis=0, keepdims=True)


def _swiglu2(gate, up):
    return jax.nn.silu(gate) * up


def _swiglu_bwd(gate, up, dact):
    _, vjp = jax.vjp(_swiglu2, gate.astype(F32), up.astype(F32))
    dg, du = vjp(dact.astype(F32))
    return jnp.concatenate([dg, du], axis=1)


def _glu(ycp, lin):
    return ycp * jax.nn.sigmoid(lin)


def _glu_bwd(ycp, lin, dyc):
    _, vjp = jax.vjp(_glu, ycp, lin)
    dycp, dlin = vjp(dyc)
    return dycp, dlin, jnp.sum(dlin, axis=0, keepdims=True)


def _s5_out_bwd(ys, uc, dycp, dskip):
    _, vjp = jax.vjp(jax.nn.gelu, ys)
    dys = vjp(dycp)[0]
    return dys, dys * dskip, jnp.sum(dys * uc, axis=0, keepdims=True)


def _combine(o0, l0, o1, l1, o2, l2):
    m = jnp.maximum(jnp.maximum(l0, l1), l2)
    e0, e1, e2 = jnp.exp(l0 - m), jnp.exp(l1 - m), jnp.exp(l2 - m)
    s = e0 + e1 + e2
    return (e0 / s) * o0 + (e1 / s) * o1 + (e2 / s) * o2


def _combine_bwd(o0, l0, o1, l1, o2, l2, ya, dya, head_ones):
    m = jnp.maximum(jnp.maximum(l0, l1), l2)
    e0, e1, e2 = jnp.exp(l0 - m), jnp.exp(l1 - m), jnp.exp(l2 - m)
    s = e0 + e1 + e2
    dot_ya = jnp.dot(dya * ya, head_ones, precision=lax.Precision.HIGHEST, preferred_element_type=F32)
    w0, w1, w2 = e0 / s, e1 / s, e2 / s
    return w0 * dya, w1 * dya, w2 * dya, -w0 * dot_ya, -w1 * dot_ya, -w2 * dot_ya


def _loss_fn(y, tgt):
    err = y - tgt
    part = jnp.sum(jnp.sum(jnp.square(err), axis=1, keepdims=True), axis=0, keepdims=True) * (0.5 / y.shape[1])
    return err * (1.0 / y.shape[1]), jnp.broadcast_to(part, (1, LANES))


def _adamw(w, g, m, v):
    m = ADAM_B1 * m + (1.0 - ADAM_B1) * g
    v = ADAM_B2 * v + (1.0 - ADAM_B2) * jnp.square(g)
    m_hat = m / (1.0 - ADAM_B1 ** ADAM_STEP)
    v_hat = v / (1.0 - ADAM_B2 ** ADAM_STEP)
    delta = -ADAM_LR * (m_hat / (jnp.sqrt(v_hat) + ADAM_EPS) + ADAM_WD * w)
    return delta, m, v


def _sum_parts_fn(parts):
    g = parts[0].astype(F32)
    for j in range(1, parts.shape[0]):
        g = g + parts[j].astype(F32)
    return g


def _t5_bucket(dist):
    max_exact = N_REL_BUCKETS // 2
    d = np.maximum(dist, 1).astype(np.float32)
    scale = (N_REL_BUCKETS - max_exact) / math.log(REL_MAX_DIST / max_exact)
    large = max_exact + (np.log(d / max_exact) * scale).astype(np.int32)
    large = np.minimum(large, N_REL_BUCKETS - 1)
    return np.where(dist < max_exact, dist, large).astype(np.int32)


def _bucket_table(dilation):
    i = np.arange(ATT_BLOCK)[:, None]
    kk = np.arange(2 * ATT_BLOCK)[None, :]
    steps = ATT_BLOCK + i - kk
    return _t5_bucket(np.maximum(steps, 0) * dilation)


def _bias_fwd(rel_bias, buckets, g):
    def body(rel_ref, bk_ref, o_ref):
        bk = bk_ref[...]
        for h in range(HEADS_PER_GROUP):
            acc = jnp.zeros(bk.shape, F32)
            for b in range(N_REL_BUCKETS):
                acc = jnp.where(bk == b, rel_ref[b, g * HEADS_PER_GROUP + h], acc)
            o_ref[h] = acc

    return pl.pallas_call(
        body, in_specs=[pl.BlockSpec(memory_space=pltpu.SMEM), pl.BlockSpec(memory_space=pltpu.VMEM)],
        out_specs=pl.BlockSpec(memory_space=pltpu.VMEM),
        out_shape=jax.ShapeDtypeStruct((HEADS_PER_GROUP, ATT_BLOCK, 2 * ATT_BLOCK), F32),
        name=f"rel_bias_fwd{g}")(rel_bias, buckets)


def _bias_bwd(dbias, buckets, g):
    def body(db_ref, bk_ref, o_ref):
        bk = bk_ref[...]
        row = lax.broadcasted_iota(jnp.int32, (N_REL_BUCKETS, LANES), 0)
        col = lax.broadcasted_iota(jnp.int32, (N_REL_BUCKETS, LANES), 1)
        acc = jnp.zeros((N_REL_BUCKETS, LANES), F32)
        for h in range(HEADS_PER_GROUP):
            d = db_ref[h]
            for b in range(N_REL_BUCKETS):
                s = jnp.sum(jnp.sum(jnp.where(bk == b, d, 0.0), axis=1, keepdims=True), axis=0, keepdims=True)
                acc = acc + jnp.where((row == b) & (col == g * HEADS_PER_GROUP + h), s, 0.0)
        o_ref[...] = acc

    return pl.pallas_call(
        body, in_specs=[pl.BlockSpec(memory_space=pltpu.VMEM), pl.BlockSpec(memory_space=pltpu.VMEM)],
        out_specs=pl.BlockSpec(memory_space=pltpu.VMEM),
        out_shape=jax.ShapeDtypeStruct((N_REL_BUCKETS, LANES), F32), name=f"rel_bias_bwd{g}")(dbias, buckets)


_NT = (((1,), (1,)), ((), ()))
_TN = (((0,), (0,)), ((), ()))
_QKV_BLOCKS = 3 * QKV_WIDTH // WIDTH_A


def _band_mask(n_is_first):
    i = lax.broadcasted_iota(jnp.int32, (ATT_BLOCK, 2 * ATT_BLOCK), 0)
    kk = lax.broadcasted_iota(jnp.int32, (ATT_BLOCK, 2 * ATT_BLOCK), 1)
    return (kk >= i) & (kk <= i + ATT_STEPS) & ((kk >= ATT_BLOCK) | jnp.logical_not(n_is_first))


def _head(ref, h):
    return ref[:, h * HEAD_DIM:(h + 1) * HEAD_DIM]


def _attn_specs(g, d):
    blk = (ATT_BLOCK, WIDTH_A)
    q = pl.BlockSpec(blk, lambda c, n: (n, c * _QKV_BLOCKS + g))
    kp = pl.BlockSpec(blk, lambda c, n: (jnp.maximum(n - 1, 0), c * _QKV_BLOCKS + 3 + g))
    kc = pl.BlockSpec(blk, lambda c, n: (n, c * _QKV_BLOCKS + 3 + g))
    vp = pl.BlockSpec(blk, lambda c, n: (jnp.maximum(n - 1, 0), c * _QKV_BLOCKS + 6 + g))
    vc = pl.BlockSpec(blk, lambda c, n: (n, c * _QKV_BLOCKS + 6 + g))
    return [q, kp, kc, vp, vc]


def _attn_fwd(qkv, bias, g, d, comm):
    t = qkv.shape[0]
    lq = t // d
    nb = lq // ATT_BLOCK
    scale = HEAD_DIM ** -0.5

    def body(q_ref, kp_ref, kc_ref, vp_ref, vc_ref, b_ref, o_ref, l_ref):
        mask = _band_mask(pl.program_id(1) == 0)
        for h in range(HEADS_PER_GROUP):
            qh = _head(q_ref, h).astype(BF16)
            kh = jnp.concatenate([_head(kp_ref, h), _head(kc_ref, h)], axis=0).astype(BF16)
            vh = jnp.concatenate([_head(vp_ref, h), _head(vc_ref, h)], axis=0).astype(BF16)
            s = lax.dot_general(qh, kh, _NT, preferred_element_type=F32) * scale + b_ref[h]
            s = jnp.where(mask, s, NEG_INF)
            m = jnp.max(s, axis=1, keepdims=True)
            p = jnp.exp(s - m)
            den = jnp.sum(p, axis=1, keepdims=True)
            o = jnp.dot(p.astype(BF16), vh, preferred_element_type=F32) / den
            o_ref[:, h * HEAD_DIM:(h + 1) * HEAD_DIM] = o
            l_ref[:, h * HEAD_DIM:(h + 1) * HEAD_DIM] = jnp.broadcast_to(m + jnp.log(den), (ATT_BLOCK, HEAD_DIM))

    out_spec = pl.BlockSpec((ATT_BLOCK, WIDTH_A), lambda c, n: (n, c))
    (o, lse), ride_outs = _pallas(
        body, grid=(d, nb),
        in_specs=_attn_specs(g, d) + [pl.BlockSpec(bias.shape, _zero_map(3))],
        out_specs=[out_spec, out_spec],
        out_shape=[jax.ShapeDtypeStruct((lq, d * WIDTH_A), F32)] * 2,
        args=[*([qkv.reshape(lq, d * 3 * QKV_WIDTH)] * 5), bias],
        semantics=("parallel", "parallel"), rides=[comm.ride(f"attn_fwd{g}")], name=f"attn_fwd{g}")
    comm.took(ride_outs)
    return o.reshape(t, WIDTH_A), lse.reshape(t, WIDTH_A)


def _attn_bwd(qkv, bias, do, lse, corr, g, d):
    t = qkv.shape[0]
    lq = t // d
    nb = lq // ATT_BLOCK
    scale = HEAD_DIM ** -0.5

    def body(k_ref, v_ref, q0_ref, q1_ref, do0_ref, do1_ref, l0_ref, l1_ref, c0_ref, c1_ref, b_ref,
             dq_ref, dk_ref, dv_ref, db_ref, dq_prev):
        c, j = pl.program_id(0), pl.program_id(1)

        @pl.when((c == 0) & (j == 0))
        def _():
            db_ref[...] = jnp.zeros_like(db_ref)

        @pl.when(j == 0)
        def _():
            dq_prev[...] = jnp.zeros_like(dq_prev)

        i = lax.broadcasted_iota(jnp.int32, (ATT_BLOCK, ATT_BLOCK), 0)
        kk = lax.broadcasted_iota(jnp.int32, (ATT_BLOCK, ATT_BLOCK), 1)
        mask0 = kk <= i
        mask1 = (kk >= i) & (j + 1 < nb)
        for h in range(HEADS_PER_GROUP):
            kh = _head(k_ref, h).astype(BF16)
            vh = _head(v_ref, h).astype(BF16)
            cols = slice(h * HEAD_DIM, (h + 1) * HEAD_DIM)
            dk = jnp.zeros((ATT_BLOCK, HEAD_DIM), F32)
            dv = jnp.zeros((ATT_BLOCK, HEAD_DIM), F32)
            dq_parts = []
            parts = ((q0_ref, do0_ref, l0_ref, c0_ref, mask0, ATT_BLOCK), (q1_ref, do1_ref, l1_ref, c1_ref, mask1, 0))
            for q_ref, do_ref, l_ref, c_ref, mask, off in parts:
                qh = _head(q_ref, h).astype(BF16)
                doh = _head(do_ref, h).astype(BF16)
                s = lax.dot_general(qh, kh, _NT, preferred_element_type=F32) * scale + b_ref[h, :, off:off + ATT_BLOCK]
                s = jnp.where(mask, s, NEG_INF)
                p = jnp.exp(s - l_ref[:, h * HEAD_DIM:h * HEAD_DIM + 1])
                dp = lax.dot_general(doh, vh, _NT, preferred_element_type=F32)
                ds = p * (dp + c_ref[:, h * HEAD_DIM:h * HEAD_DIM + 1])
                dsb = ds.astype(BF16)
                dv = dv + lax.dot_general(p.astype(BF16), doh, _TN, preferred_element_type=F32)
                dk = dk + lax.dot_general(dsb, qh, _TN, preferred_element_type=F32)
                dq_parts.append(jnp.dot(dsb, kh, preferred_element_type=F32))
                db_ref[h, :, off:off + ATT_BLOCK] += ds
            dk_ref[:, cols] = dk * scale
            dv_ref[:, cols] = dv
            dq_ref[:, cols] = (dq_prev[:, cols] + dq_parts[0]) * scale
            dq_prev[:, cols] = dq_parts[1]

    blk = (ATT_BLOCK, WIDTH_A)
    nxt = lambda n: jnp.minimum(n + 1, nb - 1)
    k_spec = pl.BlockSpec(blk, lambda c, n: (n, c * _QKV_BLOCKS + 3 + g))
    v_spec = pl.BlockSpec(blk, lambda c, n: (n, c * _QKV_BLOCKS + 6 + g))
    q0_spec = pl.BlockSpec(blk, lambda c, n: (n, c * _QKV_BLOCKS + g))
    q1_spec = pl.BlockSpec(blk, lambda c, n: (nxt(n), c * _QKV_BLOCKS + g))
    r0 = pl.BlockSpec(blk, lambda c, n: (n, c))
    r1 = pl.BlockSpec(blk, lambda c, n: (nxt(n), c))
    view = lambda a: a.reshape(lq, d * WIDTH_A)
    qv = qkv.reshape(lq, d * 3 * QKV_WIDTH)
    dq, dk, dv, dbias = pl.pallas_call(
        body, grid=(d, nb),
        in_specs=[k_spec, v_spec, q0_spec, q1_spec, r0, r1, r0, r1, r0, r1, pl.BlockSpec(bias.shape, _zero_map(3))],
        out_specs=[r0, r0, r0, pl.BlockSpec(bias.shape, _zero_map(3))],
        out_shape=[jax.ShapeDtypeStruct((lq, d * WIDTH_A), F32)] * 3 + [jax.ShapeDtypeStruct(bias.shape, F32)],
        scratch_shapes=[pltpu.VMEM(blk, F32)],
        compiler_params=_cparams("arbitrary", "arbitrary"), name=f"attn_bwd{g}",
    )(qv, qv, qv, qv, view(do), view(do), view(lse), view(lse), view(corr), view(corr), bias)
    return dq.reshape(t, WIDTH_A), dk.reshape(t, WIDTH_A), dv.reshape(t, WIDTH_A), dbias


def _tril_mask():
    r = lax.broadcasted_iota(jnp.int32, (CHUNK, CHUNK), 0)
    c = lax.broadcasted_iota(jnp.int32, (CHUNK, CHUNK), 1)
    return c <= r


def _gmlp_fwd(zb, ln_g, ln_b, w_s, b_s_t):
    t = zb.shape[0]
    tr = _pick(t, 2 * CHUNK, CHUNK)

    def body(z_ref, g_ref, b_ref, ws_ref, bs_ref, o_ref):
        tri = _tril_mask()
        z = jax.nn.gelu(z_ref[...])
        u = z[:, :WIDTH_B]
        vn = _ln(z[:, WIDTH_B:], g_ref[...], b_ref[...])
        for ch in range(tr // CHUNK):
            rows = slice(ch * CHUNK, (ch + 1) * CHUNK)
            for gi in range(N_GROUPS_B):
                cols = slice(gi * CHUNK, (gi + 1) * CHUNK)
                w = jnp.where(tri, ws_ref[gi], 0.0).astype(BF16)
                mixed = jnp.dot(w, vn[rows, cols].astype(BF16), preferred_element_type=F32) + bs_ref[:, gi:gi + 1]
                o_ref[rows, cols] = (u[rows, cols] * mixed).astype(o_ref.dtype)

    return pl.pallas_call(
        body, grid=(t // tr,),
        in_specs=[pl.BlockSpec((tr, 2 * WIDTH_B), lambda i: (i, 0)), pl.BlockSpec(ln_g.shape, _zero_map(2)),
                  pl.BlockSpec(ln_b.shape, _zero_map(2)), pl.BlockSpec(w_s.shape, _zero_map(3)),
                  pl.BlockSpec(b_s_t.shape, _zero_map(2))],
        out_specs=pl.BlockSpec((tr, WIDTH_B), lambda i: (i, 0)),
        out_shape=jax.ShapeDtypeStruct((t, WIDTH_B), BF16),
        compiler_params=_cparams("parallel"), name="gmlp_fwd")(zb, ln_g, ln_b, w_s, b_s_t)


def _gmlp_bwd(zb, dyb, ln_g, ln_b, w_s, b_s_t, group_sel):
    t = zb.shape[0]
    tr = _pick(t, 2 * CHUNK, CHUNK)

    def body(z_ref, dy_ref, g_ref, b_ref, ws_ref, bs_ref, sel_ref, dz_ref, dzs_ref, dg_ref, db_ref, dws_ref, dbs_ref,
             du_s, dvn_s, dm_s):
        @pl.when(pl.program_id(0) == 0)
        def _():
            dzs_ref[...] = jnp.zeros_like(dzs_ref)
            dg_ref[...] = jnp.zeros_like(dg_ref)
            db_ref[...] = jnp.zeros_like(db_ref)
            dws_ref[...] = jnp.zeros_like(dws_ref)
            dbs_ref[...] = jnp.zeros_like(dbs_ref)

        tri = _tril_mask()
        z, gelu_vjp = jax.vjp(jax.nn.gelu, z_ref[...])
        u = z[:, :WIDTH_B]
        vn, ln_vjp = jax.vjp(_ln, z[:, WIDTH_B:], g_ref[...], b_ref[...])
        dy = dy_ref[...]
        for ch in range(tr // CHUNK):
            rows = slice(ch * CHUNK, (ch + 1) * CHUNK)
            for gi in range(N_GROUPS_B):
                cols = slice(gi * CHUNK, (gi + 1) * CHUNK)
                w = jnp.where(tri, ws_ref[gi], 0.0).astype(BF16)
                vg = vn[rows, cols].astype(BF16)
                mixed = jnp.dot(w, vg, preferred_element_type=F32) + bs_ref[:, gi:gi + 1]
                dyg = dy[rows, cols]
                dm = dyg * u[rows, cols]
                dmb = dm.astype(BF16)
                du_s[rows, cols] = dyg * mixed
                dm_s[rows, cols] = dm
                dvn_s[rows, cols] = lax.dot_general(w, dmb, _TN, preferred_element_type=F32)
                dws_ref[gi] += jnp.where(tri, lax.dot_general(dmb, vg, _NT, preferred_element_type=F32), 0.0)
            dbs_ref[...] += jnp.dot(dm_s[rows, :], sel_ref[...], precision=lax.Precision.HIGHEST,
                                    preferred_element_type=F32)
        dv, dg, db = ln_vjp(dvn_s[...])
        dg_ref[...] += dg
        db_ref[...] += db
        dz = gelu_vjp(jnp.concatenate([du_s[...], dv], axis=1))[0]
        dz_ref[...] = dz.astype(dz_ref.dtype)
        dzs_ref[...] += jnp.sum(dz, axis=0, keepdims=True)

    full = lambda a: pl.BlockSpec(a.shape, _zero_map(a.ndim))
    return pl.pallas_call(
        body, grid=(t // tr,),
        in_specs=[pl.BlockSpec((tr, 2 * WIDTH_B), lambda i: (i, 0)), pl.BlockSpec((tr, WIDTH_B), lambda i: (i, 0)),
                  full(ln_g), full(ln_b), full(w_s), full(b_s_t), full(group_sel)],
        out_specs=[pl.BlockSpec((tr, 2 * WIDTH_B), lambda i: (i, 0)), pl.BlockSpec((1, 2 * WIDTH_B), _zero_map(2)),
                   full(ln_g), full(ln_b), full(w_s), pl.BlockSpec((CHUNK, LANES), _zero_map(2))],
        out_shape=[jax.ShapeDtypeStruct((t, 2 * WIDTH_B), BF16), jax.ShapeDtypeStruct((1, 2 * WIDTH_B), F32),
                   jax.ShapeDtypeStruct(ln_g.shape, F32),
                   jax.ShapeDtypeStruct(ln_b.shape, F32), jax.ShapeDtypeStruct(w_s.shape, F32),
                   jax.ShapeDtypeStruct((CHUNK, LANES), F32)],
        scratch_shapes=[pltpu.VMEM((tr, WIDTH_B), F32)] * 3,
        compiler_params=_cparams("arbitrary"), name="gmlp_bwd")(zb, dyb, ln_g, ln_b, w_s, b_s_t, group_sel)


def _s5_disc(lr, li, ldt, br_t, bi_t):
    dt = jnp.exp(ldt)
    mag = jnp.exp(lr * dt)
    ab_re = mag * jnp.cos(li * dt)
    ab_im = mag * jnp.sin(li * dt)
    nrm = lr * lr + li * li
    cr = ((ab_re - 1.0) * lr + ab_im * li) / nrm
    ci = (ab_im * lr - (ab_re - 1.0) * li) / nrm
    return ab_re, ab_im, cr * br_t - ci * bi_t, cr * bi_t + ci * br_t


def _vmem_call(fn, args, out_shape, name):
    def body(*refs):
        res = fn(*[r[...] for r in refs[:len(args)]])
        for o, v in zip(refs[len(args):], res):
            o[...] = v

    vm = pl.BlockSpec(memory_space=pltpu.VMEM)
    return pl.pallas_call(body, in_specs=[vm] * len(args), out_specs=[vm] * len(out_shape),
                          out_shape=out_shape, name=name)(*args)


def _s5_disc_fwd(lr, li, ldt, br_t, bi_t):
    s1 = jax.ShapeDtypeStruct(lr.shape, F32)
    s2 = jax.ShapeDtypeStruct(br_t.shape, F32)
    return _vmem_call(_s5_disc, [lr, li, ldt, br_t, bi_t], [s1, s1, s2, s2], "s5_disc_fwd")


def _s5_disc_bwd(lr, li, ldt, br_t, bi_t, cts):
    def fn(lr, li, ldt, br_t, bi_t, d0, d1, d2, d3):
        _, vjp = jax.vjp(_s5_disc, lr, li, ldt, br_t, bi_t)
        return vjp((d0, d1, d2, d3))

    shp = [jax.ShapeDtypeStruct(a.shape, F32) for a in (lr, li, ldt, br_t, bi_t)]
    return _vmem_call(fn, [lr, li, ldt, br_t, bi_t, *cts], shp, "s5_disc_bwd")


_SCAN_ROWS = SSM_COLS // LANES
_SCAN_CHUNK = 128
_SSM_IN = SSM_PACK * SSM_GROUP
_SSM_ST = SSM_PACK * SSM_STATE


def _packed_in(xb, m_ref):
    return jnp.concatenate([jnp.dot(xb[:, j * _SSM_IN:(j + 1) * _SSM_IN], m_ref[j], preferred_element_type=F32)
                            for j in range(N_SSM_BLOCKS)], axis=1)


def _packed_out(xb, m_ref):
    return jnp.concatenate([lax.dot_general(xb[:, j * _SSM_ST:(j + 1) * _SSM_ST], m_ref[j], _NT, preferred_element_type=F32)
                            for j in range(N_SSM_BLOCKS)], axis=1)


def _s5_fwd(uc, mats, are, aim, d_skip, comm):
    t = uc.shape[0]
    tc = _pick(t, _SCAN_CHUNK, SUBLANES)

    def body(u_ref, br_ref, bi_ref, cr_ref, ci_ref, ar_ref, ai_ref, d_ref, xr_ref, xi_ref, ys_ref, ycp_ref,
             sr, si, st_ref):
        @pl.when(pl.program_id(0) == 0)
        def _():
            st_ref[...] = jnp.zeros_like(st_ref)

        u = u_ref[...]
        ub = u.astype(BF16)
        sr[...] = _packed_in(ub, br_ref).reshape(tc, _SCAN_ROWS, LANES)
        si[...] = _packed_in(ub, bi_ref).reshape(tc, _SCAN_ROWS, LANES)
        ar, ai = ar_ref[...], ai_ref[...]

        def step(i, carry):
            xr, xi = carry
            nr = ar * xr - ai * xi + sr[i]
            ni = ar * xi + ai * xr + si[i]
            sr[i] = nr
            si[i] = ni
            return nr, ni

        xr, xi = lax.fori_loop(0, tc, step, (st_ref[0], st_ref[1]), unroll=8)
        st_ref[0] = xr
        st_ref[1] = xi
        x_re = sr[...].reshape(tc, SSM_COLS)
        x_im = si[...].reshape(tc, SSM_COLS)
        xr_ref[...] = x_re
        xi_ref[...] = x_im
        ys = _packed_out(x_re.astype(BF16), cr_ref) - _packed_out(x_im.astype(BF16), ci_ref) + d_ref[...] * u
        ys_ref[...] = ys
        ycp_ref[...] = jax.nn.gelu(ys)

    row = lambda w: pl.BlockSpec((tc, w), lambda i: (i, 0))
    mat = pl.BlockSpec(mats[0].shape, _zero_map(3))
    par = pl.BlockSpec((_SCAN_ROWS, LANES), _zero_map(2))
    wide, narrow = jax.ShapeDtypeStruct((t, SSM_COLS), F32), jax.ShapeDtypeStruct((t, WIDTH_C), F32)
    outs, ride_outs = _pallas(
        body, grid=(t // tc,), in_specs=[row(WIDTH_C), mat, mat, mat, mat, par, par, pl.BlockSpec(d_skip.shape, _zero_map(2))],
        out_specs=[row(SSM_COLS), row(SSM_COLS), row(WIDTH_C), row(WIDTH_C)], out_shape=[wide, wide, narrow, narrow],
        args=[uc, *mats, are, aim, d_skip],
        scratch=[pltpu.VMEM((tc, _SCAN_ROWS, LANES), F32)] * 2 + [pltpu.VMEM((2, _SCAN_ROWS, LANES), F32)],
        semantics=("arbitrary",), rides=[comm.ride("s5_fwd")], name="s5_fwd")
    comm.took(ride_outs)
    return outs


def _s5_bwd(dys, duc_skip, uc, xr, xi, mats, are, aim):
    t = dys.shape[0]
    tc = _pick(t, _SCAN_CHUNK, SUBLANES)
    nc = t // tc

    def body(dy_ref, ds_ref, u_ref, xr_ref, xi_ref, pr_ref, pi_ref, br_ref, bi_ref, cr_ref, ci_ref, ar_ref, ai_ref,
             du_ref, dbr_ref, dbi_ref, dcr_ref, dci_ref, dar_ref, dai_ref, gr, gi, x3r, x3i, st_ref):
        step_id = pl.program_id(0)

        @pl.when(step_id == 0)
        def _():
            st_ref[...] = jnp.zeros_like(st_ref)
            for ref in (dbr_ref, dbi_ref, dcr_ref, dci_ref, dar_ref, dai_ref):
                ref[...] = jnp.zeros_like(ref)

        dyb = dy_ref[...].astype(BF16)
        x_re, x_im = xr_ref[...], xi_ref[...]
        gr[...] = _packed_in(dyb, cr_ref).reshape(tc, _SCAN_ROWS, LANES)
        gi[...] = (-_packed_in(dyb, ci_ref)).reshape(tc, _SCAN_ROWS, LANES)
        x3r[...] = x_re.reshape(tc, _SCAN_ROWS, LANES)
        x3i[...] = x_im.reshape(tc, _SCAN_ROWS, LANES)
        ar, ai = ar_ref[...], ai_ref[...]

        def update(i, carry, pxr, pxi):
            g_r, g_i, dar, dai = carry
            ngr = gr[i] + ar * g_r + ai * g_i
            ngi = gi[i] - ai * g_r + ar * g_i
            gr[i] = ngr
            gi[i] = ngi
            return ngr, ngi, dar + ngr * pxr + ngi * pxi, dai - ngr * pxi + ngi * pxr

        def step(s, carry):
            i = tc - 1 - s
            return update(i, carry, x3r[i - 1], x3i[i - 1])

        zero = jnp.zeros((_SCAN_ROWS, LANES), F32)
        carry = lax.fori_loop(0, tc - 1, step, (st_ref[0], st_ref[1], zero, zero), unroll=8)
        has_prev = (step_id < nc - 1).astype(F32)
        last = SUBLANES - 1
        p_re = pr_ref[last:, :].reshape(1, _SCAN_ROWS, LANES)[0] * has_prev
        p_im = pi_ref[last:, :].reshape(1, _SCAN_ROWS, LANES)[0] * has_prev
        g_r, g_i, dar, dai = update(0, carry, p_re, p_im)
        st_ref[0] = g_r
        st_ref[1] = g_i
        dar_ref[...] += dar
        dai_ref[...] += dai

        g_re = gr[...].reshape(tc, SSM_COLS).astype(BF16)
        g_im = gi[...].reshape(tc, SSM_COLS).astype(BF16)
        du_ref[...] = ds_ref[...] + _packed_out(g_re, br_ref) + _packed_out(g_im, bi_ref)
        ub, xrb, xib = u_ref[...].astype(BF16), x_re.astype(BF16), x_im.astype(BF16)
        for j in range(N_SSM_BLOCKS):
            narrow, wide = slice(j * _SSM_IN, (j + 1) * _SSM_IN), slice(j * _SSM_ST, (j + 1) * _SSM_ST)
            dbr_ref[j] += lax.dot_general(ub[:, narrow], g_re[:, wide], _TN, preferred_element_type=F32)
            dbi_ref[j] += lax.dot_general(ub[:, narrow], g_im[:, wide], _TN, preferred_element_type=F32)
            dcr_ref[j] += lax.dot_general(dyb[:, narrow], xrb[:, wide], _TN, preferred_element_type=F32)
            dci_ref[j] -= lax.dot_general(dyb[:, narrow], xib[:, wide], _TN, preferred_element_type=F32)

    rev = lambda w: pl.BlockSpec((tc, w), lambda i: (nc - 1 - i, 0))
    prev = pl.BlockSpec((SUBLANES, SSM_COLS), lambda i: (jnp.maximum((nc - 1 - i) * (tc // SUBLANES) - 1, 0), 0))
    mat = pl.BlockSpec(mats[0].shape, _zero_map(3))
    par = pl.BlockSpec((_SCAN_ROWS, LANES), _zero_map(2))
    msh = jax.ShapeDtypeStruct(mats[0].shape, F32)
    psh = jax.ShapeDtypeStruct((_SCAN_ROWS, LANES), F32)
    return pl.pallas_call(
        body, grid=(nc,),
        in_specs=[rev(WIDTH_C), rev(WIDTH_C), rev(WIDTH_C), rev(SSM_COLS), rev(SSM_COLS), prev, prev,
                  mat, mat, mat, mat, par, par],
        out_specs=[rev(WIDTH_C), mat, mat, mat, mat, par, par],
        out_shape=[jax.ShapeDtypeStruct((t, WIDTH_C), F32), msh, msh, msh, msh, psh, psh],
        scratch_shapes=[pltpu.VMEM((tc, _SCAN_ROWS, LANES), F32)] * 4 + [pltpu.VMEM((2, _SCAN_ROWS, LANES), F32)],
        compiler_params=_cparams("arbitrary"), name="s5_bwd")(dys, duc_skip, uc, xr, xi, xr, xi, *mats, are, aim)


def _diag_blocks(a, b):
    return [(j, i, slice(i * a, (i + 1) * a), slice(i * b, (i + 1) * b))
            for j in range(N_SSM_BLOCKS) for i in range(SSM_PACK)]


def _block_diag(ms):
    _, a, b = ms[0].shape

    def body(*refs):
        for m_ref, o_ref in zip(refs[:len(ms)], refs[len(ms):]):
            o_ref[...] = jnp.zeros_like(o_ref)
            for j, i, rows, cols in _diag_blocks(a, b):
                o_ref[j, rows, cols] = m_ref[j * SSM_PACK + i].astype(o_ref.dtype)

    vm = pl.BlockSpec(memory_space=pltpu.VMEM)
    shape = jax.ShapeDtypeStruct((N_SSM_BLOCKS, SSM_PACK * a, SSM_PACK * b), BF16)
    return pl.pallas_call(body, in_specs=[vm] * len(ms), out_specs=[vm] * len(ms), out_shape=[shape] * len(ms),
                          name="s5_block_diag")(*ms)


def _block_diag_extract(ms, a, b):
    def body(*refs):
        for m_ref, o_ref in zip(refs[:len(ms)], refs[len(ms):]):
            for j, i, rows, cols in _diag_blocks(a, b):
                o_ref[j * SSM_PACK + i] = m_ref[j, rows, cols]

    vm = pl.BlockSpec(memory_space=pltpu.VMEM)
    shape = jax.ShapeDtypeStruct((N_GROUPS_C, a, b), F32)
    return pl.pallas_call(body, in_specs=[vm] * len(ms), out_specs=[vm] * len(ms), out_shape=[shape] * len(ms),
                          name="s5_block_diag_extract")(*ms)


def _exchange(src, *, gather, name):
    shape = src.shape if gather else src.shape[1:]

    def body(src_ref, out_ref, send_sems, recv_sems, local_sem):
        x, y, c = lax.axis_index("x"), lax.axis_index("y"), lax.axis_index("c")
        me = 4 * x + 2 * y + c
        copies = []
        for r in range(1, N_DEV):
            px = 1 - x if r & 4 else x
            py = 1 - y if r & 2 else y
            pc = 1 - c if r & 1 else c
            piece = src_ref if gather else src_ref.at[4 * px + 2 * py + pc]
            cp = pltpu.make_async_remote_copy(
                src_ref=piece, dst_ref=out_ref.at[me], send_sem=send_sems.at[r - 1], recv_sem=recv_sems.at[r - 1],
                device_id=(px, py, pc), device_id_type=pl.DeviceIdType.MESH)
            cp.start()
            copies.append(cp)
        mine = pltpu.make_async_copy(src_ref if gather else src_ref.at[me], out_ref.at[me], local_sem)
        mine.start()
        for cp in copies:
            cp.wait()
        mine.wait()

    hbm = pl.BlockSpec(memory_space=pl.ANY)
    return pl.pallas_call(
        body, in_specs=[hbm], out_specs=hbm, out_shape=jax.ShapeDtypeStruct((N_DEV,) + tuple(shape), src.dtype),
        scratch_shapes=[pltpu.SemaphoreType.DMA((N_DEV - 1,)), pltpu.SemaphoreType.DMA((N_DEV - 1,)),
                        pltpu.SemaphoreType.DMA(())],
        name=name)(src)


def _mesh_place():
    x, y, c = lax.axis_index("x"), lax.axis_index("y"), lax.axis_index("c")
    other_chips = [(1 - x, y), (x, 1 - y), (1 - x, 1 - y)]
    return x, y, c, other_chips


def _gather_layer(srcs, layer, name):
    n = len(srcs)

    def body(*refs):
        src = [r.at[layer] for r in refs[:n]]
        out = refs[n:2 * n]
        send_sems, recv_sems, local_sems = refs[2 * n:]
        x, y, c, chips = _mesh_place()
        me, sibling = (x, y, c), (x, y, 1 - c)

        def copy(t, k, block, to, from_src=False):
            slot = 4 * block[0] + 2 * block[1] + block[2]
            return pltpu.make_async_remote_copy(
                src_ref=src[t] if from_src else out[t].at[slot], dst_ref=out[t].at[slot],
                send_sem=send_sems.at[t, k], recv_sem=recv_sems.at[t, k], device_id=to, device_id_type=_MESH_ID)

        mine = [pltpu.make_async_copy(src[t], out[t].at[4 * x + 2 * y + c], local_sems.at[t]) for t in range(n)]
        for cp in mine:
            cp.start()
        first = []
        for t in range(n):
            first.append(copy(t, 0, me, sibling, True))
            first += [copy(t, 1 + j, me, (*chip, c), True) for j, chip in enumerate(chips)]
        for cp in first:
            cp.start()
        passed = []
        for j, chip in enumerate(chips):
            for t in range(n):
                copy(t, 1 + j, (*chip, c), me).wait_recv()
                fwd = copy(t, 4 + j, (*chip, c), sibling)
                fwd.start()
                passed.append(fwd)
        for t in range(n):
            copy(t, 0, sibling, me).wait_recv()
            for j, chip in enumerate(chips):
                copy(t, 4 + j, (*chip, 1 - c), me).wait_recv()
        for cp in first + passed:
            cp.wait_send()
        for cp in mine:
            cp.wait()

    return pl.pallas_call(
        body, in_specs=[_HBM] * n, out_specs=[_HBM] * n,
        out_shape=[jax.ShapeDtypeStruct((N_DEV,) + s.shape[1:], s.dtype) for s in srcs],
        scratch_shapes=[pltpu.SemaphoreType.DMA((n, N_DEV - 1)), pltpu.SemaphoreType.DMA((n, N_DEV - 1)),
                        pltpu.SemaphoreType.DMA((n,))],
        name=name)(*srcs)


def _scatter_pair(srcs, name):
    n = len(srcs)

    def body(*refs):
        src, out = refs[:n], refs[n:2 * n]
        send_sems, recv_sems = refs[2 * n:]
        x, y, c, _ = _mesh_place()
        copies = [pltpu.make_async_remote_copy(
            src_ref=src[t].at[:, 1 - c], dst_ref=out[t], send_sem=send_sems.at[t], recv_sem=recv_sems.at[t],
            device_id=(x, y, 1 - c), device_id_type=_MESH_ID) for t in range(n)]
        for cp in copies:
            cp.start()
        for cp in copies:
            cp.wait()

    return pl.pallas_call(
        body, in_specs=[_HBM] * n, out_specs=[_HBM] * n,
        out_shape=[jax.ShapeDtypeStruct((s.shape[0],) + s.shape[2:], s.dtype) for s in srcs],
        scratch_shapes=[pltpu.SemaphoreType.DMA((n,)), pltpu.SemaphoreType.DMA((n,))], name=name)(*srcs)


def _pair_add(src, recv, name):
    nchip, _, r, cdim = src.shape
    tr = _pick(r, 256, 2 * SUBLANES)
    core = lax.axis_index("c").astype(jnp.int32).reshape(1)

    def body(core_ref, s_ref, r_ref, o_ref):
        o_ref[...] = (s_ref[...].astype(F32) + r_ref[...].astype(F32)).astype(o_ref.dtype)

    grid_spec = pltpu.PrefetchScalarGridSpec(
        num_scalar_prefetch=1, grid=(nchip, r // tr),
        in_specs=[pl.BlockSpec((None, None, tr, cdim), lambda k, i, core_ref: (k, core_ref[0], i, 0)),
                  pl.BlockSpec((None, tr, cdim), lambda k, i, core_ref: (k, i, 0))],
        out_specs=pl.BlockSpec((None, tr, cdim), lambda k, i, core_ref: (k, i, 0)))
    return pl.pallas_call(body, grid_spec=grid_spec, out_shape=jax.ShapeDtypeStruct(recv.shape, recv.dtype),
                          compiler_params=_cparams("parallel", "parallel"), name=name)(core, src, recv)


def _scatter_chips(srcs, name):
    n = len(srcs)

    def body(*refs):
        src, out = refs[:n], refs[n:2 * n]
        send_sems, recv_sems, local_sems = refs[2 * n:]
        x, y, c, chips = _mesh_place()
        my_chip = 2 * x + y
        mine = [pltpu.make_async_copy(src[t].at[my_chip], out[t].at[my_chip], local_sems.at[t]) for t in range(n)]
        copies = [pltpu.make_async_remote_copy(
            src_ref=src[t].at[2 * chip[0] + chip[1]], dst_ref=out[t].at[my_chip],
            send_sem=send_sems.at[t, j], recv_sem=recv_sems.at[t, j], device_id=(*chip, c), device_id_type=_MESH_ID)
            for t in range(n) for j, chip in enumerate(chips)]
        for cp in mine + copies:
            cp.start()
        for cp in copies + mine:
            cp.wait()

    return pl.pallas_call(
        body, in_specs=[_HBM] * n, out_specs=[_HBM] * n,
        out_shape=[jax.ShapeDtypeStruct(s.shape, s.dtype) for s in srcs],
        scratch_shapes=[pltpu.SemaphoreType.DMA((n, 3)), pltpu.SemaphoreType.DMA((n, 3)), pltpu.SemaphoreType.DMA((n,))],
        name=name)(*srcs)


def _by_chip_and_core(g):
    return g.reshape(N_DEV // 2, 2, g.shape[0] // N_DEV, g.shape[1])


def _chip_sums(grads, tag, from_sibling=None):
    views = [_by_chip_and_core(g) for g in grads]
    if from_sibling is None:
        from_sibling = _scatter_pair(views, name="scatter_pair_" + tag)
    return [_pair_add(v, s, name="pair_add") for v, s in zip(views, from_sibling)]


def _pair_ride(grads):
    views = [_by_chip_and_core(g) for g in grads]

    def copies(in_refs, buf_refs, sems):
        x, y, c, _ = _mesh_place()
        return [pltpu.make_async_remote_copy(src_ref=src.at[:, 1 - c], dst_ref=dst, send_sem=sems.at[2 * t],
                                             recv_sem=sems.at[2 * t + 1], device_id=(x, y, 1 - c), device_id_type=_MESH_ID)
                for t, (src, dst) in enumerate(zip(in_refs, buf_refs))]

    def start(in_refs, buf_refs, sems):
        for cp in copies(in_refs, buf_refs, sems):
            cp.start()

    def finish(in_refs, buf_refs, sems):
        for cp in copies(in_refs, buf_refs, sems):
            cp.wait()

    created = tuple(jax.ShapeDtypeStruct((v.shape[0],) + v.shape[2:], v.dtype) for v in views)
    return _Ride(inputs=tuple(views), carried=(), created=created, n_sems=2 * len(views), start=start, finish=finish)


def _sum_chips(parts):
    return _rowmap(_sum_parts_fn, [], stacks=[parts], row_outs=[(parts.shape[2], F32)], tr=128, name="sum_chips")[0]


def _buffer_roles(kinds, bufs):
    carried = [k for k in kinds if k in bufs]
    return carried, [k for k in kinds if k not in bufs]


def _gather_ride(sends, forwards, bufs):
    carried, created = _buffer_roles(list(dict.fromkeys([s[0] for s in sends] + [f[0] for f in forwards])), bufs)
    shape_of = {s[0]: jax.ShapeDtypeStruct((N_DEV,) + s[1].shape, s[1].dtype) for s in sends}
    order = carried + created

    def copies(in_refs, buf_refs, sems):
        x, y, c, chips = _mesh_place()
        buf = dict(zip(order, buf_refs))
        out, s0 = [], 0
        for (kind, _, r0, nr), src in zip(sends, in_refs):
            mine, dst = src.at[pl.ds(r0, nr)], buf[kind].at[4 * x + 2 * y + c, pl.ds(r0, nr)]
            out.append(pltpu.make_async_copy(mine, dst, sems.at[s0 + 8]))
            for k, peer in enumerate([(x, y, 1 - c)] + [(*chip, c) for chip in chips]):
                out.append(pltpu.make_async_remote_copy(src_ref=mine, dst_ref=dst, send_sem=sems.at[s0 + k],
                                                        recv_sem=sems.at[s0 + 4 + k], device_id=peer, device_id_type=_MESH_ID))
            s0 += 9
        for kind, r0, nr in forwards:
            for j, chip in enumerate(chips):
                blk = buf[kind].at[4 * chip[0] + 2 * chip[1] + c, pl.ds(r0, nr)]
                out.append(pltpu.make_async_remote_copy(src_ref=blk, dst_ref=blk, send_sem=sems.at[s0 + j],
                                                        recv_sem=sems.at[s0 + 3 + j], device_id=(x, y, 1 - c),
                                                        device_id_type=_MESH_ID))
            s0 += 6
        return out

    def start(in_refs, buf_refs, sems):
        for cp in copies(in_refs, buf_refs, sems):
            cp.start()

    def finish(in_refs, buf_refs, sems):
        for cp in copies(in_refs, buf_refs, sems):
            cp.wait()

    ride = _Ride(inputs=tuple(s[1] for s in sends), carried=tuple(bufs[k] for k in carried),
                 created=tuple(shape_of[k] for k in created), n_sems=9 * len(sends) + 6 * len(forwards),
                 start=start, finish=finish)
    return ride, order


def _scatter_ride(pieces, bufs):
    carried, created = _buffer_roles(list(dict.fromkeys(p[0] for p in pieces)), bufs)
    shape_of = {p[0]: jax.ShapeDtypeStruct(p[1].shape, p[1].dtype) for p in pieces}
    order = carried + created

    def copies(in_refs, buf_refs, sems):
        x, y, c, chips = _mesh_place()
        buf = dict(zip(order, buf_refs))
        out, s0 = [], 0
        for (kind, _, r0, nr), src in zip(pieces, in_refs):
            dst = buf[kind].at[2 * x + y, pl.ds(r0, nr)]
            out.append(pltpu.make_async_copy(src.at[2 * x + y, pl.ds(r0, nr)], dst, sems.at[s0 + 6]))
            for j, chip in enumerate(chips):
                out.append(pltpu.make_async_remote_copy(
                    src_ref=src.at[2 * chip[0] + chip[1], pl.ds(r0, nr)], dst_ref=dst, send_sem=sems.at[s0 + j],
                    recv_sem=sems.at[s0 + 3 + j], device_id=(*chip, c), device_id_type=_MESH_ID))
            s0 += 7
        return out

    def start(in_refs, buf_refs, sems):
        for cp in copies(in_refs, buf_refs, sems):
            cp.start()

    def finish(in_refs, buf_refs, sems):
        for cp in copies(in_refs, buf_refs, sems):
            cp.wait()

    ride = _Ride(inputs=tuple(p[1] for p in pieces), carried=tuple(bufs[k] for k in carried),
                 created=tuple(shape_of[k] for k in created), n_sems=7 * len(pieces), start=start, finish=finish)
    return ride, order


GATHER_PLAN = (
    ("mm_in_qkv", (("w_in", 0, 3),)),
    ("mm_in_gl", (("w_in", 1, 3), ("w_glu", 0, 1), ("w_pa", 0, 1))),
    ("attn_fwd0", (("w_pb", 0, 1), ("w_ffn_out", 0, 4))),
    ("attn_fwd1", (("w_pc", 0, 1), ("w_ffn_out", 1, 4))),
    ("attn_fwd2", (("w_o", 0, 1),)),
    ("s5_fwd", (("w_ffn_out", 2, 4), ("w_ffn_out", 3, 4))),
    ("mm_ffn_in", (("w_in", 2, 3), ("w_ffn_in", 0, 2))),
    ("mm_ffn_out", (("w_ffn_in", 1, 2),)),
    ("norm2", ()),
)
EARLY_KINDS = ("w_ffn_out", "w_ffn_in")
LATE_KINDS = tuple(n for n in SHARDED if n not in EARLY_KINDS)
SCATTER_EARLY_PLAN = (
    ("mm_in_dw_qkv", (("w_ffn_out", 0, 2), ("w_ffn_in", 7, 8))),
    ("mm_in_dx_qkv", (("w_ffn_out", 1, 2), ("w_ffn_in", 6, 8))),
    ("mm_in_dw_gl", (("w_ffn_in", 0, 8), ("w_ffn_in", 1, 8), ("w_ffn_in", 2, 8))),
    ("mm_in_dx_gl", (("w_ffn_in", 3, 8), ("w_ffn_in", 4, 8), ("w_ffn_in", 5, 8))),
)
SCATTER_PLAN = (
    ("mm_ffn_out_dw", (("w_in", 0, 3),)),
    ("mm_ffn_out_dx", (("w_in", 1, 3),)),
    ("mm_ffn_in_dw", (("w_in", 2, 3), ("w_o", 0, 1), ("w_pa", 0, 1), ("w_pb", 0, 1), ("w_pc", 0, 1), ("w_glu", 0, 1))),
)


def _row_part(rows, part, parts):
    assert rows % (parts * 2 * SUBLANES) == 0
    return part * (rows // parts), rows // parts


class _Carried:
    def __init__(self, plan, blocks, make_ride, forwards_too):
        self.plan, self.blocks, self.make_ride, self.forwards_too = dict(plan), blocks, make_ride, forwards_too
        self.bufs, self.to_forward, self.order = {}, [], []

    def ride(self, host):
        if self.blocks is None or host not in self.plan:
            self.order = []
            return None
        sends = [(k, self.blocks[k], *_row_part(self.blocks[k].shape[-2], part, parts)) for k, part, parts in self.plan[host]]
        if self.forwards_too:
            ride, self.order = self.make_ride(sends, self.to_forward, self.bufs)
            self.to_forward = [(k, r0, nr) for k, _, r0, nr in sends]
        else:
            ride, self.order = self.make_ride(sends, self.bufs)
        return ride

    def took(self, ride_outs):
        for k, buf in zip(self.order, ride_outs[0] if ride_outs else []):
            self.bufs[k] = buf


class _OwnScatter:
    def __init__(self, enabled):
        self.enabled = enabled

    def pair_ride(self, grads):
        return _pair_ride(list(grads.values())) if self.enabled else None

    def steps(self, grads, ride_outs):
        if not self.enabled:
            return _Carried(SCATTER_EARLY_PLAN, None, _scatter_ride, forwards_too=False)
        sums = _chip_sums(list(grads.values()), "early", from_sibling=ride_outs[0])
        return _Carried(SCATTER_EARLY_PLAN, dict(zip(grads, sums)), _scatter_ride, forwards_too=False)


class _PrevScatter:
    PAIR_HOST = "norm2_bwd"

    def __init__(self, grads):
        self.grads = grads
        self.inner = _Carried(SCATTER_PLAN, None, _scatter_ride, forwards_too=False)
        self.pair_pending = False

    @property
    def bufs(self):
        return self.inner.bufs

    def ride(self, host):
        if self.grads is not None and host == self.PAIR_HOST:
            self.pair_pending = True
            return _pair_ride(list(self.grads.values()))
        return self.inner.ride(host)

    def took(self, ride_outs):
        if not self.pair_pending:
            return self.inner.took(ride_outs)
        self.pair_pending = False
        sums = _chip_sums(list(self.grads.values()), "late", from_sibling=ride_outs[0])
        self.inner = _Carried(SCATTER_PLAN, dict(zip(self.grads, sums)), _scatter_ride, forwards_too=False)


def _hosted(comm, fn, *args, name, **kwargs):
    res, ride_outs = fn(*args, name=name, rides=[comm.ride(name)], **kwargs)
    comm.took(ride_outs)
    return res


def _small_sizes(shapes):
    return [int(np.prod(shapes[n])) for n in SMALL]


def _pack_small(vals):
    flat = jnp.concatenate([vals[n].reshape(-1).astype(F32) for n in SMALL])
    rows = -(-flat.shape[0] // (LANES * N_DEV * SUBLANES)) * (N_DEV * SUBLANES)
    return jnp.pad(flat, (0, rows * LANES - flat.shape[0])).reshape(rows, LANES)


def _unpack_small(packed, shapes):
    flat = packed.reshape(-1)
    out, off = {}, 0
    for n, size in zip(SMALL, _small_sizes(shapes)):
        out[n] = flat[off:off + size].reshape(shapes[n])
        off += size
    return out


def _row(v):
    return v.reshape(1, -1)


def _layer_params(l, full, small):
    o1, o2, o3 = 3 * QKV_WIDTH, 3 * QKV_WIDTH + 2 * WIDTH_B, 3 * QKV_WIDTH + 2 * WIDTH_B + WIDTH_C
    b_in = small["b_in"][l]
    p = {
        "in_pieces": (("qkv", 0, o1), ("zb", o1, o2 - o1), ("uc", o2, o3 - o2), ("gl", o3, b_in.shape[0] - o3)),
        "b_qkv": _row(b_in[:o1]), "b_zb": _row(b_in[o1:o2]), "b_uc": _row(b_in[o2:o3]), "b_gl": _row(b_in[o3:]),
        "sgu_ln_g": _row(small["sgu_ln_g"][l]), "sgu_ln_b": _row(small["sgu_ln_b"][l]),
        "w_s": small["w_s"][l], "b_s_t": small["b_s"][l].T,
        "lam_re": small["lam_re"][l][:, None, :], "lam_im": small["lam_im"][l][:, None, :],
        "log_dt": small["log_dt"][l][:, None, None],
        "b_re_t": small["b_re"][l].transpose(0, 2, 1), "b_im_t": small["b_im"][l].transpose(0, 2, 1),
        "c_re": small["c_re"][l], "c_im": small["c_im"][l],
        "d_skip": _row(small["d_skip"][l]), "b_glu": _row(small["b_glu"][l]),
        "ln1_g": _row(small["ln1_g"][l]), "ln1_b": _row(small["ln1_b"][l]),
        "ln2_g": _row(small["ln2_g"][l]), "ln2_b": _row(small["ln2_b"][l]),
    }
    for n in SHARDED:
        p[n] = full[n]
    return p


def _twice(fn):
    def both(*args):
        y = fn(*args)
        return y, y
    return both


def _layer_fwd(x, xb, p, biases, comm):
    t, d = x.shape
    r = {"x": x, "xb": xb}
    for piece, off, n in p["in_pieces"]:
        r[piece] = _hosted(comm, _mm, xb, p["w_in"], tb=True, b_off=off, n=n, bias=p["b_" + piece],
                           out_dtype=BF16 if piece in ("qkv", "gl") else F32, name="mm_in_" + piece)
    ol = []
    for g, dil in enumerate(ATT_DILATIONS):
        ol += list(_attn_fwd(r["qkv"], biases[g], g, dil, comm))
    r["ol"] = ol
    r["ya"], r["ya_b"] = _rowmap(_twice(_combine), ol, row_outs=[(WIDTH_A, F32), (WIDTH_A, BF16)], tr=512,
                                 name="attn_combine")
    r["yb"] = _gmlp_fwd(r["zb"], p["sgu_ln_g"], p["sgu_ln_b"], p["w_s"], p["b_s_t"])
    ab_re, ab_im, bb_re_t, bb_im_t = _s5_disc_fwd(p["lam_re"], p["lam_im"], p["log_dt"], p["b_re_t"], p["b_im_t"])
    r["a_re"], r["a_im"] = ab_re.reshape(_SCAN_ROWS, LANES), ab_im.reshape(_SCAN_ROWS, LANES)
    r["s5_mats"] = _block_diag([bb_re_t, bb_im_t, p["c_re"], p["c_im"]])
    r["xr"], r["xi"], r["ys"], r["ycp"] = _s5_fwd(r["uc"], r["s5_mats"], r["a_re"], r["a_im"], p["d_skip"], comm)
    r["glin"] = _mm(r["ycp"], p["w_glu"], bias=p["b_glu"], name="mm_glu")
    r["yc"] = _rowmap(_glu, [r["ycp"], r["glin"]], row_outs=[(WIDTH_C, BF16)], tr=512, name="glu")[0]
    r["pa"] = _mm(r["ya_b"], p["w_pa"], tb=True, out_dtype=BF16, name="mm_pa")
    r["pb"] = _mm(r["yb"], p["w_pb"], tb=True, out_dtype=BF16, name="mm_pb")
    r["pc"] = _mm(r["yc"], p["w_pc"], tb=True, out_dtype=BF16, name="mm_pc")
    r["merged"] = _rowmap(_merge, [r["gl"], r["pa"], r["pb"], r["pc"]], row_outs=[(d, BF16)], name="merge")[0]
    r["mo"] = _hosted(comm, _mm, r["merged"], p["w_o"], name="mm_o")
    r["xm"], r["xm_b"] = _rowmap(_twice(_post_norm), [x, r["mo"]], consts=[p["ln1_g"], p["ln1_b"]],
                                 row_outs=[(d, F32), (d, BF16)], name="norm1")
    r["gate"], r["up"], r["act"] = _hosted(comm, _mm_swiglu, r["xm_b"], p["w_ffn_in"], name="mm_ffn_in")
    r["f"] = _hosted(comm, _mm, r["act"], p["w_ffn_out"], name="mm_ffn_out")
    out, out_b = _hosted(comm, _rowmap, _twice(_post_norm), [r["xm"], r["f"]], consts=[p["ln2_g"], p["ln2_b"]],
                         row_outs=[(d, F32), (d, BF16)], name="norm2")
    return out, out_b, r


def _layer_bwd(dout, r, p, biases, consts, comm, own):
    t, d = dout.shape
    gw, gs = {}, {}
    ffw = 2 * r["gate"].shape[1]
    dxm, df, gs["ln2_g"], gs["ln2_b"] = _hosted(
        comm, _rowmap, _post_norm_bwd, [r["xm"], r["f"], dout], consts=[p["ln2_g"], p["ln2_b"]],
        row_outs=[(d, F32), (d, BF16)], red_outs=[(1, d)] * 2, name="norm2_bwd")
    gw["w_ffn_out"] = _hosted(comm, _mm, r["act"], df, ta=True, out_dtype=BF16, name="mm_ffn_out_dw")
    dact = _hosted(comm, _mm, df, p["w_ffn_out"], tb=True, out_dtype=BF16, name="mm_ffn_out_dx")
    dgu = _rowmap(_swiglu_bwd, [r["gate"], r["up"], dact], row_outs=[(ffw, BF16)], tr=128, name="swiglu_bwd")[0]
    gw["w_ffn_in"] = _hosted(comm, _mm, dgu, r["xm_b"], ta=True, out_dtype=BF16, name="mm_ffn_in_dw")
    early_grads = {n: gw[n] for n in EARLY_KINDS}
    dxm, from_sibling = _mm(dgu, p["w_ffn_in"], add=dxm, name="mm_ffn_in_dx", rides=[own.pair_ride(early_grads)])
    early = own.steps(early_grads, from_sibling)
    dx, dmo, gs["ln1_g"], gs["ln1_b"] = _rowmap(
        _post_norm_bwd, [r["x"], r["mo"], dxm], consts=[p["ln1_g"], p["ln1_b"]],
        row_outs=[(d, F32), (d, BF16)], red_outs=[(1, d)] * 2, name="norm1_bwd")
    gw["w_o"] = _hosted(comm, _mm, r["merged"], dmo, ta=True, out_dtype=BF16, name="mm_o_dw")
    dmerged = _hosted(comm, _mm, dmo, p["w_o"], tb=True, name="mm_o_dx")
    dgl, dpa, dpb, dpc, db_gl = _rowmap(
        _merge_bwd, [r["gl"], r["pa"], r["pb"], r["pc"], dmerged],
        row_outs=[(3 * d, BF16), (d, BF16), (d, BF16), (d, BF16)], red_outs=[(1, 3 * d)], tr=128, name="merge_bwd")
    gw["w_pa"] = _mm(dpa, r["ya_b"], ta=True, out_dtype=BF16, name="mm_pa_dw")
    gw["w_pb"] = _mm(dpb, r["yb"], ta=True, out_dtype=BF16, name="mm_pb_dw")
    gw["w_pc"] = _mm(dpc, r["yc"], ta=True, out_dtype=BF16, name="mm_pc_dw")
    dya = _mm(dpa, p["w_pa"], name="mm_pa_dx")
    dyb = _mm(dpb, p["w_pb"], name="mm_pb_dx")
    dyc = _mm(dpc, p["w_pc"], name="mm_pc_dx")
    dycp, dglin, gs["b_glu"] = _rowmap(_glu_bwd, [r["ycp"], r["glin"], dyc], row_outs=[(WIDTH_C, F32), (WIDTH_C, BF16)],
                                       red_outs=[(1, WIDTH_C)], tr=512, name="glu_bwd")
    gw["w_glu"] = _mm(r["ycp"], dglin, ta=True, out_dtype=BF16, name="mm_glu_dw")
    dycp = _mm(dglin, p["w_glu"], tb=True, add=dycp, name="mm_glu_dx")
    dys, duc, gs["d_skip"] = _rowmap(_s5_out_bwd, [r["ys"], r["uc"], dycp], consts=[p["d_skip"]],
                                     row_outs=[(WIDTH_C, F32)] * 2, red_outs=[(1, WIDTH_C)], tr=512, name="s5_out_act_bwd")
    duc, d_bmat_re, d_bmat_im, d_cmat_re, d_cmat_im, da_re, da_im = _s5_bwd(
        dys, duc, r["uc"], r["xr"], r["xi"], r["s5_mats"], r["a_re"], r["a_im"])
    d_bb_re_t, d_bb_im_t, gs["c_re"], gs["c_im"] = _block_diag_extract(
        [d_bmat_re, d_bmat_im, d_cmat_re, d_cmat_im], SSM_GROUP, SSM_STATE)
    cts = (da_re.reshape(N_GROUPS_C, 1, SSM_STATE), da_im.reshape(N_GROUPS_C, 1, SSM_STATE), d_bb_re_t, d_bb_im_t)
    d_lr, d_li, d_ldt, d_br_t, d_bi_t = _s5_disc_bwd(p["lam_re"], p["lam_im"], p["log_dt"], p["b_re_t"], p["b_im_t"], cts)
    gs["lam_re"], gs["lam_im"], gs["log_dt"] = d_lr[:, 0, :], d_li[:, 0, :], d_ldt[:, 0, 0]
    gs["b_re"], gs["b_im"] = d_br_t.transpose(0, 2, 1), d_bi_t.transpose(0, 2, 1)
    dzb, db_zb, gs["sgu_ln_g"], gs["sgu_ln_b"], gs["w_s"], dbs_t = _gmlp_bwd(
        r["zb"], dyb, p["sgu_ln_g"], p["sgu_ln_b"], p["w_s"], p["b_s_t"], consts["group_sel"])
    gs["b_s"] = dbs_t[:, :N_GROUPS_B].T
    do_corr = _rowmap(_combine_bwd, r["ol"] + [r["ya"], dya], consts=[consts["head_ones"]],
                      row_outs=[(WIDTH_A, F32)] * 6, tr=512, name="attn_combine_bwd")
    dq, dk, dv, dbias = [], [], [], []
    for g, dil in enumerate(ATT_DILATIONS):
        do_g, corr_g, lse_g = do_corr[g], do_corr[3 + g], r["ol"][2 * g + 1]
        dq_g, dk_g, dv_g, db_g = _attn_bwd(r["qkv"], biases[g], do_g, lse_g, corr_g, g, dil)
        dq.append(dq_g)
        dk.append(dk_g)
        dv.append(dv_g)
        dbias.append(db_g)
    def cast_colsum(*pieces):
        a = pieces[0] if len(pieces) == 1 else jnp.concatenate(pieces, axis=1)
        return a, jnp.sum(a, axis=0, keepdims=True)

    dqkv, db_qkv = _rowmap(cast_colsum, dq + dk + dv, row_outs=[(3 * QKV_WIDTH, BF16)],
                           red_outs=[(1, 3 * QKV_WIDTH)], tr=256, name="cast_colsum_qkv")
    duc, db_uc = _rowmap(cast_colsum, [duc], row_outs=[(WIDTH_C, BF16)], red_outs=[(1, WIDTH_C)], tr=512,
                         name="cast_colsum_uc")
    dpieces = {"qkv": dqkv, "zb": dzb, "uc": duc, "gl": dgl}
    rows_in = p["w_in"].shape[0]
    dw_in = None
    for piece, off, n in p["in_pieces"]:
        dw_in = _hosted(early, _mm, dpieces[piece], r["xb"], ta=True, out_dtype=BF16, into=(rows_in, off, dw_in),
                        name="mm_in_dw_" + piece)
        dx = _hosted(early, _mm, dpieces[piece], p["w_in"], b_off=off, add=dx, name="mm_in_dx_" + piece)
    gw["w_in"] = dw_in
    gs["b_in"] = jnp.concatenate([db_qkv, db_zb, db_uc, db_gl], axis=1)[0]
    for n in ("sgu_ln_g", "sgu_ln_b", "d_skip", "b_glu", "ln1_g", "ln1_b", "ln2_g", "ln2_b"):
        gs[n] = gs[n][0]
    return dx, gw, gs, dbias, early.bufs


def _cast_bf16(w):
    w2 = w.reshape(-1, w.shape[-1])
    out = _rowmap(lambda a: a, [w2], row_outs=[(w2.shape[1], BF16)], tr=512, name="cast_bf16")[0]
    return out.reshape(w.shape)


def _static_consts():
    head_ones = np.kron(np.eye(HEADS_PER_GROUP, dtype=np.float32), np.ones((HEAD_DIM, HEAD_DIM), np.float32))
    group_sel = np.zeros((WIDTH_B, LANES), np.float32)
    group_sel[np.arange(WIDTH_B), np.arange(WIDTH_B) // CHUNK] = 1.0
    return {"head_ones": jnp.asarray(head_ones), "group_sel": jnp.asarray(group_sel)}


def _step(x, tgt, w, m, v):
    shapes = {n: w[n].shape for n in WEIGHTS}
    consts = _static_consts()
    mine_bf = {n: _cast_bf16(w[n].transpose(0, 2, 1) if n in TRANSPOSED else w[n]) for n in SHARDED}
    small = {n: w[n] for n in SMALL}
    buckets = [jnp.asarray(_bucket_table(dil)) for dil in ATT_DILATIONS]
    biases = [_bias_fwd(w["rel_bias"], buckets[g], g) for g in range(len(ATT_DILATIONS))]
    params, saved = [], []
    h, hb = _rowmap(_twice(lambda a: a), [x], row_outs=[(x.shape[1], F32), (x.shape[1], BF16)], name="cast_x")
    gathered = dict(zip(SHARDED, _gather_layer([mine_bf[n] for n in SHARDED], 0, name="gather_layer0")))
    for l in range(DEPTH):
        p = _layer_params(l, {n: g.reshape(-1, g.shape[2]) for n, g in gathered.items()}, small)
        ahead = _Carried(GATHER_PLAN, {n: mine_bf[n][l + 1] for n in SHARDED} if l + 1 < DEPTH else None,
                         _gather_ride, forwards_too=True)
        h, hb, r = _layer_fwd(h, hb, p, biases, ahead)
        gathered = ahead.bufs
        params.append(p)
        saved.append(r)
    dy, loss_part = _rowmap(_loss_fn, [h, tgt], row_outs=[(h.shape[1], F32)], red_outs=[(1, LANES)], name="loss")
    loss = lax.psum(loss_part[0, 0], MESH_AXES)
    g_mine, gs_layers = {n: [None] * DEPTH for n in SHARDED}, [None] * DEPTH
    dbias_sum = None
    behind = _PrevScatter(None)
    for l in reversed(range(DEPTH)):
        dy, gw, gs_layers[l], dbias, early_parts = _layer_bwd(
            dy, saved[l], params[l], biases, consts, behind, _OwnScatter(enabled=True))
        saved[l] = None
        for n in EARLY_KINDS:
            g_mine[n][l] = _sum_chips(early_parts[n])
        if behind.grads is not None:
            for n in LATE_KINDS:
                g_mine[n][l + 1] = _sum_chips(behind.bufs[n])
        behind = _PrevScatter({n: gw[n] for n in LATE_KINDS})
        if l == 0:
            sums = _chip_sums(list(behind.grads.values()), "late")
            for n, parts in zip(LATE_KINDS, _scatter_chips(sums, name="scatter_chips_layer0")):
                g_mine[n][0] = _sum_chips(parts)
        if dbias_sum is None:
            dbias_sum = dbias
        else:
            dbias_sum = [_rowmap(lambda a, b: a + b, [a.reshape(-1, 2 * ATT_BLOCK), b.reshape(-1, 2 * ATT_BLOCK)],
                                 row_outs=[(2 * ATT_BLOCK, F32)], name="dbias_add")[0].reshape(a.shape)
                         for a, b in zip(dbias_sum, dbias)]
    drel = [_bias_bwd(dbias_sum[g], buckets[g], g) for g in range(len(ATT_DILATIONS))]
    drel = _rowmap(lambda a, b, c: a + b + c, drel, row_outs=[(LANES, F32)], name="drel_add")[0]
    grad_small_local = {n: jnp.stack([gs_layers[l][n] for l in range(DEPTH)]) for n in SMALL if n != "rel_bias"}
    grad_small_local["rel_bias"] = drel[:, :shapes["rel_bias"][1]]
    out_g, out_d, out_m, out_v = {}, {}, {}, {}
    for n in SHARDED:
        g = jnp.stack(g_mine[n])
        out_g[n] = g.transpose(0, 2, 1) if n in TRANSPOSED else g
        cols = shapes[n][-1]
        res = _rowmap(_adamw, [a.reshape(-1, cols) for a in (w[n], out_g[n], m[n], v[n])],
                      row_outs=[(cols, F32)] * 3, tr=128, name="adamw_" + n)
        out_d[n], out_m[n], out_v[n] = [a.reshape(shapes[n]) for a in res]
    packed = _pack_small(grad_small_local)
    rows = packed.shape[0] // N_DEV
    parts = _exchange(packed.reshape(N_DEV, rows, LANES), gather=False, name="scatter_small")
    mine = _rowmap(_sum_parts_fn, [], stacks=[parts], row_outs=[(LANES, F32)], name="sum_small")[0]
    g_small = _exchange(mine, gather=True, name="gather_small").reshape(-1, LANES)
    out_g.update(_unpack_small(g_small, {n: shapes[n] for n in SMALL}))
    for n in SMALL:
        cols = shapes[n][-1]
        res = _rowmap(_adamw, [a.reshape(-1, cols) for a in (w[n], out_g[n], m[n], v[n])],
                      row_outs=[(cols, F32)] * 3, name="adamw_" + n)
        out_d[n], out_m[n], out_v[n] = [a.reshape(shapes[n]) for a in res]
    return loss, dy, out_g, out_d, out_m, out_v


def kernel(x, w_in, b_in, rel_bias, sgu_ln_g, sgu_ln_b, w_s, b_s, lam_re, lam_im, log_dt, b_re, b_im, c_re, c_im, d_skip, w_glu, b_glu, w_pa, w_pb, w_pc, w_o, ln1_g, ln1_b, w_ffn_in, w_ffn_out, ln2_g, ln2_b, loss_target, m_w_in, m_b_in, m_rel_bias, m_sgu_ln_g, m_sgu_ln_b, m_w_s, m_b_s, m_lam_re, m_lam_im, m_log_dt, m_b_re, m_b_im, m_c_re, m_c_im, m_d_skip, m_w_glu, m_b_glu, m_w_pa, m_w_pb, m_w_pc, m_w_o, m_ln1_g, m_ln1_b, m_w_ffn_in, m_w_ffn_out, m_ln2_g, m_ln2_b, v_w_in, v_b_in, v_rel_bias, v_sgu_ln_g, v_sgu_ln_b, v_w_s, v_b_s, v_lam_re, v_lam_im, v_log_dt, v_b_re, v_b_im, v_c_re, v_c_im, v_d_skip, v_w_glu, v_b_glu, v_w_pa, v_w_pb, v_w_pc, v_w_o, v_ln1_g, v_ln1_b, v_w_ffn_in, v_w_ffn_out, v_ln2_g, v_ln2_b):
    args = dict(locals())
    w = {n: args[n] for n in WEIGHTS}
    m = {n: args["m_" + n] for n in WEIGHTS}
    v = {n: args["v_" + n] for n in WEIGHTS}
    loss, dx, g, d, nm, nv = _step(x[0], loss_target[0], w, m, v)
    return (loss, dx[None], *[g[n] for n in WEIGHTS], *[d[n] for n in WEIGHTS],
            *[nm[n] for n in WEIGHTS], *[nv[n] for n in WEIGHTS])
```

```python
import functools
import math
from typing import Callable, NamedTuple

import numpy as np
import jax
import jax.numpy as jnp
from jax import lax
from jax.experimental import pallas as pl
from jax.experimental.pallas import tpu as pltpu

F32 = jnp.float32
BF16 = jnp.bfloat16

MESH_AXES = ("x", "y", "c")
N_DEV = 8
DEPTH = 4

ATT_DILATIONS = (1, 4, 16)
ATT_STEPS = 128
HEADS_PER_GROUP = 8
HEAD_DIM = 64
QKV_WIDTH = 1536
WIDTH_A = HEADS_PER_GROUP * HEAD_DIM
ATT_BLOCK = 128
N_REL_BUCKETS = 32
REL_MAX_DIST = 2048
NEG_INF = -1e30
CHUNK = 128
WIDTH_B = 768
N_GROUPS_B = 6
WIDTH_C = 768
SSM_GROUP = 16
N_GROUPS_C = 48
SSM_STATE = 64
SSM_PACK = 8
N_SSM_BLOCKS = N_GROUPS_C // SSM_PACK
SSM_COLS = N_GROUPS_C * SSM_STATE
ALPHA = (2 * DEPTH) ** 0.25

ADAM_LR = 0.001
ADAM_B1 = 0.9
ADAM_B2 = 0.999
ADAM_EPS = 1e-08
ADAM_WD = 0.01
ADAM_STEP = 10

LANES = 128
SUBLANES = 8
VMEM_LIMIT = 48 * 1024 * 1024

SHARDED = ("w_in", "w_glu", "w_pa", "w_pb", "w_pc", "w_o", "w_ffn_in", "w_ffn_out")
TRANSPOSED = ("w_in", "w_pa", "w_pb", "w_pc", "w_ffn_in")
SMALL = ("b_in", "rel_bias", "sgu_ln_g", "sgu_ln_b", "w_s", "b_s", "lam_re", "lam_im", "log_dt",
         "b_re", "b_im", "c_re", "c_im", "d_skip", "b_glu", "ln1_g", "ln1_b", "ln2_g", "ln2_b")
WEIGHTS = ("w_in", "b_in", "rel_bias", "sgu_ln_g", "sgu_ln_b", "w_s", "b_s", "lam_re", "lam_im",
           "log_dt", "b_re", "b_im", "c_re", "c_im", "d_skip", "w_glu", "b_glu", "w_pa", "w_pb",
           "w_pc", "w_o", "ln1_g", "ln1_b", "w_ffn_in", "w_ffn_out", "ln2_g", "ln2_b")


def _pick(dim, target, mult):
    best = None
    for t in range(mult, min(dim, target) + 1, mult):
        if dim % t == 0:
            best = t
    return dim if best is None else best


def _cparams(*sem):
    return pltpu.CompilerParams(dimension_semantics=sem, vmem_limit_bytes=VMEM_LIMIT)


def _zero_map(ndim):
    return lambda *_: (0,) * ndim


_HBM = pl.BlockSpec(memory_space=pl.ANY)
_MESH_ID = pl.DeviceIdType.MESH


class _Ride(NamedTuple):
    inputs: tuple
    carried: tuple
    created: tuple
    n_sems: int
    start: Callable
    finish: Callable


def _pallas(body, *, grid, in_specs, out_specs, out_shape, args, scratch=(), semantics, rides=(), aliases=None, name):
    rides = [r for r in rides if r is not None]
    n_in, n_out, n_scr = len(args), len(out_shape), len(scratch)
    r_args, r_shapes, aliases, spans = [], [], dict(aliases or {}), []
    for r in rides:
        i0, o0 = len(r_args), len(r_shapes)
        r_args += [*r.inputs, *r.carried]
        for k, a in enumerate(r.carried):
            aliases[n_in + i0 + len(r.inputs) + k] = n_out + o0 + k
        r_shapes += [jax.ShapeDtypeStruct(a.shape, a.dtype) for a in r.carried] + list(r.created)
        spans.append((i0, len(r.inputs), o0, len(r.carried) + len(r.created)))

    def full_body(*refs):
        host_in, ride_in = refs[:n_in], refs[n_in:n_in + len(r_args)]
        p = n_in + len(r_args)
        host_out, ride_out = refs[p:p + n_out], refs[p + n_out:p + n_out + len(r_shapes)]
        p += n_out + len(r_shapes)
        host_scr, ride_sems = refs[p:p + n_scr], refs[p + n_scr:]
        ids = [pl.program_id(k) for k in range(len(grid))]
        first = functools.reduce(jnp.logical_and, [i == 0 for i in ids])
        last = functools.reduce(jnp.logical_and, [i == g - 1 for i, g in zip(ids, grid)])

        def each(method):
            for r, (i0, ni, o0, no), sems in zip(rides, spans, ride_sems):
                getattr(r, method)(ride_in[i0:i0 + ni], ride_out[o0:o0 + no], sems)

        if rides:
            pl.when(first)(lambda: each("start"))
        body(*host_in, *host_out, *host_scr)
        if rides:
            pl.when(last)(lambda: each("finish"))

    if rides:
        semantics = ("arbitrary",) * len(grid)
    outs = pl.pallas_call(
        full_body, grid=grid, in_specs=list(in_specs) + [_HBM] * len(r_args),
        out_specs=list(out_specs) + [_HBM] * len(r_shapes), out_shape=list(out_shape) + r_shapes,
        scratch_shapes=list(scratch) + [pltpu.SemaphoreType.DMA((r.n_sems,)) for r in rides],
        input_output_aliases=aliases, compiler_params=_cparams(*semantics), name=name)(*args, *r_args)
    return outs[:n_out], [outs[n_out + o0:n_out + o0 + no] for _, _, o0, no in spans]


def _rowmap(fn, rows, consts=(), stacks=(), row_outs=(), red_outs=(), tr=256, name=None, rides=None):
    t = rows[0].shape[0] if rows else stacks[0].shape[1]
    dtypes = [a.dtype for a in (*rows, *stacks)] + [dt for _, dt in row_outs]
    packed = any(jnp.dtype(dt).itemsize < 4 for dt in dtypes)
    tr = _pick(t, tr, 2 * SUBLANES if packed else SUBLANES)
    n_r, n_c, n_s, n_o = len(rows), len(consts), len(stacks), len(row_outs)

    def body(*refs):
        ins = [r[...] for r in refs[:n_r + n_c + n_s]]
        outs = refs[n_r + n_c + n_s:n_r + n_c + n_s + n_o]
        reds = refs[n_r + n_c + n_s + n_o:]
        res = fn(*ins)
        if not isinstance(res, (tuple, list)):
            res = (res,)
        for o, v in zip(outs, res[:n_o]):
            o[...] = v.astype(o.dtype)
        if reds:
            @pl.when(pl.program_id(0) == 0)
            def _():
                for r in reds:
                    r[...] = jnp.zeros_like(r)
            for r, v in zip(reds, res[n_o:]):
                r[...] += v

    in_specs = [pl.BlockSpec((tr, r.shape[1]), lambda i: (i, 0)) for r in rows]
    in_specs += [pl.BlockSpec(c.shape, _zero_map(c.ndim)) for c in consts]
    in_specs += [pl.BlockSpec((s.shape[0], tr, s.shape[2]), lambda i: (0, i, 0)) for s in stacks]
    out_specs = [pl.BlockSpec((tr, w), lambda i: (i, 0)) for w, _ in row_outs]
    out_specs += [pl.BlockSpec(s, _zero_map(len(s))) for s in red_outs]
    out_shape = [jax.ShapeDtypeStruct((t, w), dt) for w, dt in row_outs]
    out_shape += [jax.ShapeDtypeStruct(s, F32) for s in red_outs]
    outs, ride_outs = _pallas(body, grid=(t // tr,), in_specs=in_specs, out_specs=out_specs, out_shape=out_shape,
                              args=[*rows, *consts, *stacks], semantics=("arbitrary",), rides=rides or (), name=name)
    return outs if rides is None else (outs, ride_outs)


MM_VMEM_BUDGET = 36 * 1024 * 1024


def _divisors(dim, mult, must_divide=0):
    out = [t for t in range(dim, 0, -mult) if t % mult == 0 and dim % t == 0 and must_divide % t == 0]
    return out or [dim]


def _mm_tiles(m, n, k, a_bytes, b_bytes, out_bytes, extra_bytes, ta, b_off_n, b_off_k, out_off, tm, tn):
    tms = _divisors(m, LANES if ta else SUBLANES, out_off)
    tm = next((t for t in tms if t <= tm), tms[-1])
    tns = [t for t in _divisors(n, LANES, b_off_n) if t <= tn] or [_divisors(n, LANES, b_off_n)[-1]]
    for tn_ in tns:
        for tk in _divisors(k, LANES, b_off_k):
            acc = 0 if tk == k else tm * tn_ * 4
            need = 2 * (tm * tk * a_bytes + tk * tn_ * b_bytes + tm * tn_ * (out_bytes + extra_bytes)) + acc
            if need <= MM_VMEM_BUDGET:
                return tm, tn_, tk
    return tm, tns[-1], _divisors(k, LANES, b_off_k)[-1]


def _mm(a, b, *, ta=False, tb=False, bias=None, add=None, out_dtype=F32, b_off=0, n=None, tm=1024, tn=1024, name=None,
        rides=None, into=None, glu_of=None):
    k, m = a.shape if ta else a.shape[::-1]
    if tb:
        n = b.shape[0] if n is None else n
        assert b.shape[1] == k and b_off + n <= b.shape[0]
    else:
        n = b.shape[1]
        assert b_off + k <= b.shape[0]
    out_rows, out_off, out_buf = (m, 0, None) if into is None else into
    extra = 4 if add is not None else 0
    tm, tn, tk = _mm_tiles(m, n, k, a.dtype.itemsize, b.dtype.itemsize, jnp.dtype(out_dtype).itemsize, extra, ta,
                           b_off if tb else 0, 0 if tb else b_off, out_off, tm, tn)
    nk = k // tk
    off_n, off_k = (b_off // tn, 0) if tb else (0, b_off // tk)
    off_m = out_off // tm
    dims = (((0 if ta else 1,), (1 if tb else 0,)), ((), ()))

    def body(*refs):
        a_ref, b_ref = refs[0], refs[1]
        rest = list(refs[2:])
        bias_ref = rest.pop(0) if bias is not None else None
        add_ref = rest.pop(0) if add is not None else None
        glu_ref = rest.pop(0) if glu_of is not None else None
        o_ref = rest.pop(0)
        gated_ref = rest.pop(0) if glu_of is not None else None
        part = lax.dot_general(a_ref[...].astype(BF16), b_ref[...].astype(BF16), dims, preferred_element_type=F32)

        def finish(r):
            if bias_ref is not None:
                r = r + bias_ref[...]
            if add_ref is not None:
                r = r + add_ref[...]
            o_ref[...] = r.astype(o_ref.dtype)
            if gated_ref is not None:
                gated_ref[...] = _glu(glu_ref[...], r).astype(gated_ref.dtype)

        if nk == 1:
            finish(part)
        else:
            acc_ref = rest.pop(0)
            kk = pl.program_id(2)

            @pl.when(kk == 0)
            def _():
                acc_ref[...] = part

            @pl.when(kk > 0)
            def _():
                acc_ref[...] += part

            @pl.when(kk == nk - 1)
            def _():
                finish(acc_ref[...])

    a_spec = pl.BlockSpec((tk, tm), lambda i, j, q: (q, i)) if ta else pl.BlockSpec((tm, tk), lambda i, j, q: (i, q))
    if tb:
        b_spec = pl.BlockSpec((tn, tk), lambda i, j, q: (j + off_n, q))
    else:
        b_spec = pl.BlockSpec((tk, tn), lambda i, j, q: (q + off_k, j))
    in_specs, args = [a_spec, b_spec], [a, b]
    if bias is not None:
        in_specs.append(pl.BlockSpec((1, tn), lambda i, j, q: (0, j)))
        args.append(bias)
    if add is not None:
        in_specs.append(pl.BlockSpec((tm, tn), lambda i, j, q: (i, j)))
        args.append(add)
    if glu_of is not None:
        assert out_buf is None and glu_of.shape == (m, n)
        in_specs.append(pl.BlockSpec((tm, tn), lambda i, j, q: (i, j)))
        args.append(glu_of)
    aliases = {}
    if out_buf is not None:
        assert out_buf.shape == (out_rows, n) and out_buf.dtype == jnp.dtype(out_dtype)
        in_specs.append(_HBM)
        args.append(out_buf)
        aliases = {len(args) - 1: 0}

    def body_in_place(*refs):
        body(*refs[:len(args) - 1], *refs[len(args):])

    outs, ride_outs = _pallas(
        body if out_buf is None else body_in_place, grid=(m // tm, n // tn, nk), in_specs=in_specs,
        out_specs=[pl.BlockSpec((tm, tn), lambda i, j, q: (i + off_m, j))] * (1 if glu_of is None else 2),
        out_shape=[jax.ShapeDtypeStruct((out_rows, n), out_dtype)]
        + ([] if glu_of is None else [jax.ShapeDtypeStruct((m, n), BF16)]), args=args,
        scratch=[] if nk == 1 else [pltpu.VMEM((tm, tn), F32)],
        semantics=("parallel", "parallel", "arbitrary"), rides=rides or (), aliases=aliases, name=name)
    res = outs[0] if glu_of is None else tuple(outs)
    return res if rides is None else (res, ride_outs)


def _mm_swiglu(a, w_t, *, name, rides=None):
    m, k = a.shape
    f = w_t.shape[0] // 2
    tm, tn, tk = _mm_tiles(m, f, k, a.dtype.itemsize, 2 * w_t.dtype.itemsize, 3 * 2, 0, False, 0, 0, 0, 1024, 512)
    assert tk == k, "the fused activation needs the whole contraction in one block"

    def body(a_ref, g_ref, u_ref, gate_ref, up_ref, act_ref):
        av = a_ref[...].astype(BF16)
        gate = lax.dot_general(av, g_ref[...].astype(BF16), _NT, preferred_element_type=F32)
        up = lax.dot_general(av, u_ref[...].astype(BF16), _NT, preferred_element_type=F32)
        gate_ref[...] = gate.astype(gate_ref.dtype)
        up_ref[...] = up.astype(up_ref.dtype)
        act_ref[...] = _swiglu2(gate, up).astype(act_ref.dtype)

    out_spec = pl.BlockSpec((tm, tn), lambda i, j: (i, j))
    outs, ride_outs = _pallas(
        body, grid=(m // tm, f // tn),
        in_specs=[pl.BlockSpec((tm, k), lambda i, j: (i, 0)), pl.BlockSpec((tn, k), lambda i, j: (j, 0)),
                  pl.BlockSpec((tn, k), lambda i, j: (j + f // tn, 0))],
        out_specs=[out_spec] * 3, out_shape=[jax.ShapeDtypeStruct((m, f), BF16)] * 3, args=[a, w_t, w_t],
        semantics=("parallel", "parallel"), rides=rides or (), name=name)
    return outs if rides is None else (outs, ride_outs)


def _ln(x, g, b, eps=1e-5):
    mu = jnp.mean(x, axis=-1, keepdims=True)
    var = jnp.mean(jnp.square(x - mu), axis=-1, keepdims=True)
    return (x - mu) * lax.rsqrt(var + eps) * g + b


def _post_norm(x, f, g, b):
    return _ln(ALPHA * x + f, g, b)


def _post_norm_bwd(x, f, dy, g, b):
    _, vjp = jax.vjp(_post_norm, x, f, g, b)
    return vjp(dy)


def _merge3(g0, g1, g2, pa, pb, pc):
    return jax.nn.sigmoid(g0) * pa + jax.nn.sigmoid(g1) * pb + jax.nn.sigmoid(g2) * pc


def _merge_args(gl, pa, pb, pc):
    d = pa.shape[1]
    return [a.astype(F32) for a in (gl[:, :d], gl[:, d:2 * d], gl[:, 2 * d:], pa, pb, pc)]


def _merge(gl, pa, pb, pc):
    return _merge3(*_merge_args(gl, pa, pb, pc))


def _merge_bwd(gl, pa, pb, pc, dm):
    _, vjp = jax.vjp(_merge3, *_merge_args(gl, pa, pb, pc))
    d0, d1, d2, dpa, dpb, dpc = vjp(dm)
    dgl = jnp.concatenate([d0, d1, d2], axis=1)
    return dgl, dpa, dpb, dpc, jnp.sum(dgl, axis=0, keepdims=True)


def _swiglu2(gate, up):
    return jax.nn.silu(gate) * up


def _swiglu_bwd(gate, up, dact):
    _, vjp = jax.vjp(_swiglu2, gate.astype(F32), up.astype(F32))
    dg, du = vjp(dact.astype(F32))
    return jnp.concatenate([dg, du], axis=1)


def _glu(ycp, lin):
    return ycp * jax.nn.sigmoid(lin)


def _glu_bwd(ycp, lin, dyc):
    _, vjp = jax.vjp(_glu, ycp, lin)
    dycp, dlin = vjp(dyc)
    return dycp, dlin, jnp.sum(dlin, axis=0, keepdims=True)


def _s5_out_bwd(ys, uc, dycp, dskip):
    _, vjp = jax.vjp(jax.nn.gelu, ys)
    dys = vjp(dycp)[0]
    return dys, dys * dskip, jnp.sum(dys * uc, axis=0, keepdims=True)


def _combine(o0, l0, o1, l1, o2, l2):
    m = jnp.maximum(jnp.maximum(l0, l1), l2)
    e0, e1, e2 = jnp.exp(l0 - m), jnp.exp(l1 - m), jnp.exp(l2 - m)
    s = e0 + e1 + e2
    return (e0 / s) * o0 + (e1 / s) * o1 + (e2 / s) * o2


def _combine_bwd(o0, l0, o1, l1, o2, l2, ya, dya, head_ones):
    m = jnp.maximum(jnp.maximum(l0, l1), l2)
    e0, e1, e2 = jnp.exp(l0 - m), jnp.exp(l1 - m), jnp.exp(l2 - m)
    s = e0 + e1 + e2
    dot_ya = jnp.dot(dya * ya, head_ones, precision=lax.Precision.HIGHEST, preferred_element_type=F32)
    w0, w1, w2 = e0 / s, e1 / s, e2 / s
    return w0 * dya, w1 * dya, w2 * dya, -w0 * dot_ya, -w1 * dot_ya, -w2 * dot_ya


def _loss_fn(y, tgt):
    err = y - tgt
    part = jnp.sum(jnp.sum(jnp.square(err), axis=1, keepdims=True), axis=0, keepdims=True) * (0.5 / y.shape[1])
    return err * (1.0 / y.shape[1]), jnp.broadcast_to(part, (1, LANES))


def _adamw(w, g, m, v):
    m = ADAM_B1 * m + (1.0 - ADAM_B1) * g
    v = ADAM_B2 * v + (1.0 - ADAM_B2) * jnp.square(g)
    m_hat = m / (1.0 - ADAM_B1 ** ADAM_STEP)
    v_hat = v / (1.0 - ADAM_B2 ** ADAM_STEP)
    delta = -ADAM_LR * (m_hat / (jnp.sqrt(v_hat) + ADAM_EPS) + ADAM_WD * w)
    return delta, m, v


def _sum_parts_fn(parts):
    g = parts[0].astype(F32)
    for j in range(1, parts.shape[0]):
        g = g + parts[j].astype(F32)
    return g


def _t5_bucket(dist):
    max_exact = N_REL_BUCKETS // 2
    d = np.maximum(dist, 1).astype(np.float32)
    scale = (N_REL_BUCKETS - max_exact) / math.log(REL_MAX_DIST / max_exact)
    large = max_exact + (np.log(d / max_exact) * scale).astype(np.int32)
    large = np.minimum(large, N_REL_BUCKETS - 1)
    return np.where(dist < max_exact, dist, large).astype(np.int32)


def _bucket_table(dilation):
    i = np.arange(ATT_BLOCK)[:, None]
    kk = np.arange(2 * ATT_BLOCK)[None, :]
    steps = ATT_BLOCK + i - kk
    return _t5_bucket(np.maximum(steps, 0) * dilation)


def _bias_fwd(rel_bias, buckets, g):
    def body(rel_ref, bk_ref, o_ref):
        bk = bk_ref[...]
        for h in range(HEADS_PER_GROUP):
            acc = jnp.zeros(bk.shape, F32)
            for b in range(N_REL_BUCKETS):
                acc = jnp.where(bk == b, rel_ref[b, g * HEADS_PER_GROUP + h], acc)
            o_ref[h] = acc

    return pl.pallas_call(
        body, in_specs=[pl.BlockSpec(memory_space=pltpu.SMEM), pl.BlockSpec(memory_space=pltpu.VMEM)],
        out_specs=pl.BlockSpec(memory_space=pltpu.VMEM),
        out_shape=jax.ShapeDtypeStruct((HEADS_PER_GROUP, ATT_BLOCK, 2 * ATT_BLOCK), F32),
        name=f"rel_bias_fwd{g}")(rel_bias, buckets)


def _bias_bwd(dbias, buckets, g):
    def body(db_ref, bk_ref, o_ref):
        bk = bk_ref[...]
        row = lax.broadcasted_iota(jnp.int32, (N_REL_BUCKETS, LANES), 0)
        col = lax.broadcasted_iota(jnp.int32, (N_REL_BUCKETS, LANES), 1)
        acc = jnp.zeros((N_REL_BUCKETS, LANES), F32)
        for h in range(HEADS_PER_GROUP):
            d = db_ref[h]
            for b in range(N_REL_BUCKETS):
                s = jnp.sum(jnp.sum(jnp.where(bk == b, d, 0.0), axis=1, keepdims=True), axis=0, keepdims=True)
                acc = acc + jnp.where((row == b) & (col == g * HEADS_PER_GROUP + h), s, 0.0)
        o_ref[...] = acc

    return pl.pallas_call(
        body, in_specs=[pl.BlockSpec(memory_space=pltpu.VMEM), pl.BlockSpec(memory_space=pltpu.VMEM)],
        out_specs=pl.BlockSpec(memory_space=pltpu.VMEM),
        out_shape=jax.ShapeDtypeStruct((N_REL_BUCKETS, LANES), F32), name=f"rel_bias_bwd{g}")(dbias, buckets)


_NT = (((1,), (1,)), ((), ()))
_TN = (((0,), (0,)), ((), ()))
_QKV_BLOCKS = 3 * QKV_WIDTH // WIDTH_A


def _band_mask(n_is_first):
    i = lax.broadcasted_iota(jnp.int32, (ATT_BLOCK, 2 * ATT_BLOCK), 0)
    kk = lax.broadcasted_iota(jnp.int32, (ATT_BLOCK, 2 * ATT_BLOCK), 1)
    return (kk >= i) & (kk <= i + ATT_STEPS) & ((kk >= ATT_BLOCK) | jnp.logical_not(n_is_first))


def _head(ref, h):
    return ref[:, h * HEAD_DIM:(h + 1) * HEAD_DIM]


def _attn_specs(g, d):
    blk = (ATT_BLOCK, WIDTH_A)
    q = pl.BlockSpec(blk, lambda c, n: (n, c * _QKV_BLOCKS + g))
    kp = pl.BlockSpec(blk, lambda c, n: (jnp.maximum(n - 1, 0), c * _QKV_BLOCKS + 3 + g))
    kc = pl.BlockSpec(blk, lambda c, n: (n, c * _QKV_BLOCKS + 3 + g))
    vp = pl.BlockSpec(blk, lambda c, n: (jnp.maximum(n - 1, 0), c * _QKV_BLOCKS + 6 + g))
    vc = pl.BlockSpec(blk, lambda c, n: (n, c * _QKV_BLOCKS + 6 + g))
    return [q, kp, kc, vp, vc]


def _attn_fwd(qkv, bias, g, d, comm):
    t = qkv.shape[0]
    lq = t // d
    nb = lq // ATT_BLOCK
    scale = HEAD_DIM ** -0.5

    def body(q_ref, kp_ref, kc_ref, vp_ref, vc_ref, b_ref, o_ref, l_ref):
        mask = _band_mask(pl.program_id(1) == 0)
        for h in range(HEADS_PER_GROUP):
            qh = _head(q_ref, h).astype(BF16)
            kh = jnp.concatenate([_head(kp_ref, h), _head(kc_ref, h)], axis=0).astype(BF16)
            vh = jnp.concatenate([_head(vp_ref, h), _head(vc_ref, h)], axis=0).astype(BF16)
            s = lax.dot_general(qh, kh, _NT, preferred_element_type=F32) * scale + b_ref[h]
            s = jnp.where(mask, s, NEG_INF)
            m = jnp.max(s, axis=1, keepdims=True)
            p = jnp.exp(s - m)
            den = jnp.sum(p, axis=1, keepdims=True)
            o = jnp.dot(p.astype(BF16), vh, preferred_element_type=F32) / den
            o_ref[:, h * HEAD_DIM:(h + 1) * HEAD_DIM] = o
            l_ref[:, h * HEAD_DIM:(h + 1) * HEAD_DIM] = jnp.broadcast_to(m + jnp.log(den), (ATT_BLOCK, HEAD_DIM))

    out_spec = pl.BlockSpec((ATT_BLOCK, WIDTH_A), lambda c, n: (n, c))
    (o, lse), ride_outs = _pallas(
        body, grid=(d, nb),
        in_specs=_attn_specs(g, d) + [pl.BlockSpec(bias.shape, _zero_map(3))],
        out_specs=[out_spec, out_spec],
        out_shape=[jax.ShapeDtypeStruct((lq, d * WIDTH_A), F32)] * 2,
        args=[*([qkv.reshape(lq, d * 3 * QKV_WIDTH)] * 5), bias],
        semantics=("parallel", "parallel"), rides=[comm.ride(f"attn_fwd{g}")], name=f"attn_fwd{g}")
    comm.took(ride_outs)
    return o.reshape(t, WIDTH_A), lse.reshape(t, WIDTH_A)


def _attn_bwd(qkv, bias, do, lse, corr, g, d):
    t = qkv.shape[0]
    lq = t // d
    nb = lq // ATT_BLOCK
    scale = HEAD_DIM ** -0.5

    def body(k_ref, v_ref, q0_ref, q1_ref, do0_ref, do1_ref, l0_ref, l1_ref, c0_ref, c1_ref, b_ref,
             dq_ref, dk_ref, dv_ref, db_ref, dq_prev):
        c, j = pl.program_id(0), pl.program_id(1)

        @pl.when((c == 0) & (j == 0))
        def _():
            db_ref[...] = jnp.zeros_like(db_ref)

        @pl.when(j == 0)
        def _():
            dq_prev[...] = jnp.zeros_like(dq_prev)

        i = lax.broadcasted_iota(jnp.int32, (ATT_BLOCK, ATT_BLOCK), 0)
        kk = lax.broadcasted_iota(jnp.int32, (ATT_BLOCK, ATT_BLOCK), 1)
        mask0 = kk <= i
        mask1 = (kk >= i) & (j + 1 < nb)
        for h in range(HEADS_PER_GROUP):
            kh = _head(k_ref, h).astype(BF16)
            vh = _head(v_ref, h).astype(BF16)
            cols = slice(h * HEAD_DIM, (h + 1) * HEAD_DIM)
            dk = jnp.zeros((ATT_BLOCK, HEAD_DIM), F32)
            dv = jnp.zeros((ATT_BLOCK, HEAD_DIM), F32)
            dq_parts = []
            parts = ((q0_ref, do0_ref, l0_ref, c0_ref, mask0, ATT_BLOCK), (q1_ref, do1_ref, l1_ref, c1_ref, mask1, 0))
            for q_ref, do_ref, l_ref, c_ref, mask, off in parts:
                qh = _head(q_ref, h).astype(BF16)
                doh = _head(do_ref, h).astype(BF16)
                s = lax.dot_general(qh, kh, _NT, preferred_element_type=F32) * scale + b_ref[h, :, off:off + ATT_BLOCK]
                s = jnp.where(mask, s, NEG_INF)
                p = jnp.exp(s - l_ref[:, h * HEAD_DIM:h * HEAD_DIM + 1])
                dp = lax.dot_general(doh, vh, _NT, preferred_element_type=F32)
                ds = p * (dp + c_ref[:, h * HEAD_DIM:h * HEAD_DIM + 1])
                dsb = ds.astype(BF16)
                dv = dv + lax.dot_general(p.astype(BF16), doh, _TN, preferred_element_type=F32)
                dk = dk + lax.dot_general(dsb, qh, _TN, preferred_element_type=F32)
                dq_parts.append(jnp.dot(dsb, kh, preferred_element_type=F32))
                db_ref[h, :, off:off + ATT_BLOCK] += ds
            dk_ref[:, cols] = dk * scale
            dv_ref[:, cols] = dv
            dq_ref[:, cols] = (dq_prev[:, cols] + dq_parts[0]) * scale
            dq_prev[:, cols] = dq_parts[1]

    blk = (ATT_BLOCK, WIDTH_A)
    nxt = lambda n: jnp.minimum(n + 1, nb - 1)
    k_spec = pl.BlockSpec(blk, lambda c, n: (n, c * _QKV_BLOCKS + 3 + g))
    v_spec = pl.BlockSpec(blk, lambda c, n: (n, c * _QKV_BLOCKS + 6 + g))
    q0_spec = pl.BlockSpec(blk, lambda c, n: (n, c * _QKV_BLOCKS + g))
    q1_spec = pl.BlockSpec(blk, lambda c, n: (nxt(n), c * _QKV_BLOCKS + g))
    r0 = pl.BlockSpec(blk, lambda c, n: (n, c))
    r1 = pl.BlockSpec(blk, lambda c, n: (nxt(n), c))
    view = lambda a: a.reshape(lq, d * WIDTH_A)
    qv = qkv.reshape(lq, d * 3 * QKV_WIDTH)
    dq, dk, dv, dbias = pl.pallas_call(
        body, grid=(d, nb),
        in_specs=[k_spec, v_spec, q0_spec, q1_spec, r0, r1, r0, r1, r0, r1, pl.BlockSpec(bias.shape, _zero_map(3))],
        out_specs=[r0, r0, r0, pl.BlockSpec(bias.shape, _zero_map(3))],
        out_shape=[jax.ShapeDtypeStruct((lq, d * WIDTH_A), F32)] * 3 + [jax.ShapeDtypeStruct(bias.shape, F32)],
        scratch_shapes=[pltpu.VMEM(blk, F32)],
        compiler_params=_cparams("arbitrary", "arbitrary"), name=f"attn_bwd{g}",
    )(qv, qv, qv, qv, view(do), view(do), view(lse), view(lse), view(corr), view(corr), bias)
    return dq.reshape(t, WIDTH_A), dk.reshape(t, WIDTH_A), dv.reshape(t, WIDTH_A), dbias


def _tril_mask():
    r = lax.broadcasted_iota(jnp.int32, (CHUNK, CHUNK), 0)
    c = lax.broadcasted_iota(jnp.int32, (CHUNK, CHUNK), 1)
    return c <= r


def _gmlp_fwd(zb, ln_g, ln_b, w_s, b_s_t):
    t = zb.shape[0]
    tr = _pick(t, 2 * CHUNK, CHUNK)

    def body(z_ref, g_ref, b_ref, ws_ref, bs_ref, o_ref):
        tri = _tril_mask()
        z = jax.nn.gelu(z_ref[...])
        u = z[:, :WIDTH_B]
        vn = _ln(z[:, WIDTH_B:], g_ref[...], b_ref[...])
        for ch in range(tr // CHUNK):
            rows = slice(ch * CHUNK, (ch + 1) * CHUNK)
            for gi in range(N_GROUPS_B):
                cols = slice(gi * CHUNK, (gi + 1) * CHUNK)
                w = jnp.where(tri, ws_ref[gi], 0.0).astype(BF16)
                mixed = jnp.dot(w, vn[rows, cols].astype(BF16), preferred_element_type=F32) + bs_ref[:, gi:gi + 1]
                o_ref[rows, cols] = (u[rows, cols] * mixed).astype(o_ref.dtype)

    return pl.pallas_call(
        body, grid=(t // tr,),
        in_specs=[pl.BlockSpec((tr, 2 * WIDTH_B), lambda i: (i, 0)), pl.BlockSpec(ln_g.shape, _zero_map(2)),
                  pl.BlockSpec(ln_b.shape, _zero_map(2)), pl.BlockSpec(w_s.shape, _zero_map(3)),
                  pl.BlockSpec(b_s_t.shape, _zero_map(2))],
        out_specs=pl.BlockSpec((tr, WIDTH_B), lambda i: (i, 0)),
        out_shape=jax.ShapeDtypeStruct((t, WIDTH_B), BF16),
        compiler_params=_cparams("parallel"), name="gmlp_fwd")(zb, ln_g, ln_b, w_s, b_s_t)


def _gmlp_bwd(zb, dyb, ln_g, ln_b, w_s, b_s_t, group_sel):
    t = zb.shape[0]
    tr = _pick(t, 2 * CHUNK, CHUNK)

    def body(z_ref, dy_ref, g_ref, b_ref, ws_ref, bs_ref, sel_ref, dz_ref, dzs_ref, dg_ref, db_ref, dws_ref, dbs_ref,
             du_s, dvn_s, dm_s):
        @pl.when(pl.program_id(0) == 0)
        def _():
            dzs_ref[...] = jnp.zeros_like(dzs_ref)
            dg_ref[...] = jnp.zeros_like(dg_ref)
            db_ref[...] = jnp.zeros_like(db_ref)
            dws_ref[...] = jnp.zeros_like(dws_ref)
            dbs_ref[...] = jnp.zeros_like(dbs_ref)

        tri = _tril_mask()
        z, gelu_vjp = jax.vjp(jax.nn.gelu, z_ref[...])
        u = z[:, :WIDTH_B]
        vn, ln_vjp = jax.vjp(_ln, z[:, WIDTH_B:], g_ref[...], b_ref[...])
        dy = dy_ref[...]
        for ch in range(tr // CHUNK):
            rows = slice(ch * CHUNK, (ch + 1) * CHUNK)
            for gi in range(N_GROUPS_B):
                cols = slice(gi * CHUNK, (gi + 1) * CHUNK)
                w = jnp.where(tri, ws_ref[gi], 0.0).astype(BF16)
                vg = vn[rows, cols].astype(BF16)
                mixed = jnp.dot(w, vg, preferred_element_type=F32) + bs_ref[:, gi:gi + 1]
                dyg = dy[rows, cols]
                dm = dyg * u[rows, cols]
                dmb = dm.astype(BF16)
                du_s[rows, cols] = dyg * mixed
                dm_s[rows, cols] = dm
                dvn_s[rows, cols] = lax.dot_general(w, dmb, _TN, preferred_element_type=F32)
                dws_ref[gi] += jnp.where(tri, lax.dot_general(dmb, vg, _NT, preferred_element_type=F32), 0.0)
            dbs_ref[...] += jnp.dot(dm_s[rows, :], sel_ref[...], precision=lax.Precision.HIGHEST,
                                    preferred_element_type=F32)
        dv, dg, db = ln_vjp(dvn_s[...])
        dg_ref[...] += dg
        db_ref[...] += db
        dz = gelu_vjp(jnp.concatenate([du_s[...], dv], axis=1))[0]
        dz_ref[...] = dz.astype(dz_ref.dtype)
        dzs_ref[...] += jnp.sum(dz, axis=0, keepdims=True)

    full = lambda a: pl.BlockSpec(a.shape, _zero_map(a.ndim))
    return pl.pallas_call(
        body, grid=(t // tr,),
        in_specs=[pl.BlockSpec((tr, 2 * WIDTH_B), lambda i: (i, 0)), pl.BlockSpec((tr, WIDTH_B), lambda i: (i, 0)),
                  full(ln_g), full(ln_b), full(w_s), full(b_s_t), full(group_sel)],
        out_specs=[pl.BlockSpec((tr, 2 * WIDTH_B), lambda i: (i, 0)), pl.BlockSpec((1, 2 * WIDTH_B), _zero_map(2)),
                   full(ln_g), full(ln_b), full(w_s), pl.BlockSpec((CHUNK, LANES), _zero_map(2))],
        out_shape=[jax.ShapeDtypeStruct((t, 2 * WIDTH_B), BF16), jax.ShapeDtypeStruct((1, 2 * WIDTH_B), F32),
                   jax.ShapeDtypeStruct(ln_g.shape, F32),
                   jax.ShapeDtypeStruct(ln_b.shape, F32), jax.ShapeDtypeStruct(w_s.shape, F32),
                   jax.ShapeDtypeStruct((CHUNK, LANES), F32)],
        scratch_shapes=[pltpu.VMEM((tr, WIDTH_B), F32)] * 3,
        compiler_params=_cparams("arbitrary"), name="gmlp_bwd")(zb, dyb, ln_g, ln_b, w_s, b_s_t, group_sel)


def _s5_disc(lr, li, ldt, br_t, bi_t):
    dt = jnp.exp(ldt)
    mag = jnp.exp(lr * dt)
    ab_re = mag * jnp.cos(li * dt)
    ab_im = mag * jnp.sin(li * dt)
    nrm = lr * lr + li * li
    cr = ((ab_re - 1.0) * lr + ab_im * li) / nrm
    ci = (ab_im * lr - (ab_re - 1.0) * li) / nrm
    return ab_re, ab_im, cr * br_t - ci * bi_t, cr * bi_t + ci * br_t


def _vmem_call(fn, args, out_shape, name):
    def body(*refs):
        res = fn(*[r[...] for r in refs[:len(args)]])
        for o, v in zip(refs[len(args):], res):
            o[...] = v

    vm = pl.BlockSpec(memory_space=pltpu.VMEM)
    return pl.pallas_call(body, in_specs=[vm] * len(args), out_specs=[vm] * len(out_shape),
                          out_shape=out_shape, name=name)(*args)


def _s5_disc_fwd(lr, li, ldt, br_t, bi_t):
    s1 = jax.ShapeDtypeStruct(lr.shape, F32)
    s2 = jax.ShapeDtypeStruct(br_t.shape, F32)
    return _vmem_call(_s5_disc, [lr, li, ldt, br_t, bi_t], [s1, s1, s2, s2], "s5_disc_fwd")


def _s5_disc_bwd(lr, li, ldt, br_t, bi_t, cts):
    def fn(lr, li, ldt, br_t, bi_t, d0, d1, d2, d3):
        _, vjp = jax.vjp(_s5_disc, lr, li, ldt, br_t, bi_t)
        return vjp((d0, d1, d2, d3))

    shp = [jax.ShapeDtypeStruct(a.shape, F32) for a in (lr, li, ldt, br_t, bi_t)]
    return _vmem_call(fn, [lr, li, ldt, br_t, bi_t, *cts], shp, "s5_disc_bwd")


_SCAN_ROWS = SSM_COLS // LANES
_SCAN_CHUNK = 128
_SSM_IN = SSM_PACK * SSM_GROUP
_SSM_ST = SSM_PACK * SSM_STATE


def _packed_in(xb, m_ref):
    return jnp.concatenate([jnp.dot(xb[:, j * _SSM_IN:(j + 1) * _SSM_IN], m_ref[j], preferred_element_type=F32)
                            for j in range(N_SSM_BLOCKS)], axis=1)


def _packed_out(xb, m_ref):
    return jnp.concatenate([lax.dot_general(xb[:, j * _SSM_ST:(j + 1) * _SSM_ST], m_ref[j], _NT, preferred_element_type=F32)
                            for j in range(N_SSM_BLOCKS)], axis=1)


def _s5_fwd(uc, mats, are, aim, d_skip, comm):
    t = uc.shape[0]
    tc = _pick(t, _SCAN_CHUNK, SUBLANES)

    def body(u_ref, br_ref, bi_ref, cr_ref, ci_ref, ar_ref, ai_ref, d_ref, xr_ref, xi_ref, ys_ref, ycp_ref,
             sr, si, st_ref):
        @pl.when(pl.program_id(0) == 0)
        def _():
            st_ref[...] = jnp.zeros_like(st_ref)

        u = u_ref[...]
        ub = u.astype(BF16)
        sr[...] = _packed_in(ub, br_ref).reshape(tc, _SCAN_ROWS, LANES)
        si[...] = _packed_in(ub, bi_ref).reshape(tc, _SCAN_ROWS, LANES)
        ar, ai = ar_ref[...], ai_ref[...]

        def step(i, carry):
            xr, xi = carry
            nr = ar * xr - ai * xi + sr[i]
            ni = ar * xi + ai * xr + si[i]
            sr[i] = nr
            si[i] = ni
            return nr, ni

        xr, xi = lax.fori_loop(0, tc, step, (st_ref[0], st_ref[1]), unroll=8)
        st_ref[0] = xr
        st_ref[1] = xi
        x_re = sr[...].reshape(tc, SSM_COLS)
        x_im = si[...].reshape(tc, SSM_COLS)
        xr_ref[...] = x_re
        xi_ref[...] = x_im
        ys = _packed_out(x_re.astype(BF16), cr_ref) - _packed_out(x_im.astype(BF16), ci_ref) + d_ref[...] * u
        ys_ref[...] = ys
        ycp_ref[...] = jax.nn.gelu(ys)

    row = lambda w: pl.BlockSpec((tc, w), lambda i: (i, 0))
    mat = pl.BlockSpec(mats[0].shape, _zero_map(3))
    par = pl.BlockSpec((_SCAN_ROWS, LANES), _zero_map(2))
    wide, narrow = jax.ShapeDtypeStruct((t, SSM_COLS), F32), jax.ShapeDtypeStruct((t, WIDTH_C), F32)
    outs, ride_outs = _pallas(
        body, grid=(t // tc,), in_specs=[row(WIDTH_C), mat, mat, mat, mat, par, par, pl.BlockSpec(d_skip.shape, _zero_map(2))],
        out_specs=[row(SSM_COLS), row(SSM_COLS), row(WIDTH_C), row(WIDTH_C)], out_shape=[wide, wide, narrow, narrow],
        args=[uc, *mats, are, aim, d_skip],
        scratch=[pltpu.VMEM((tc, _SCAN_ROWS, LANES), F32)] * 2 + [pltpu.VMEM((2, _SCAN_ROWS, LANES), F32)],
        semantics=("arbitrary",), rides=[comm.ride("s5_fwd")], name="s5_fwd")
    comm.took(ride_outs)
    return outs


def _s5_bwd(dys, duc_skip, uc, xr, xi, mats, are, aim):
    t = dys.shape[0]
    tc = _pick(t, _SCAN_CHUNK, SUBLANES)
    nc = t // tc

    def body(dy_ref, ds_ref, u_ref, xr_ref, xi_ref, pr_ref, pi_ref, br_ref, bi_ref, cr_ref, ci_ref, ar_ref, ai_ref,
             du_ref, dbr_ref, dbi_ref, dcr_ref, dci_ref, dar_ref, dai_ref, gr, gi, x3r, x3i, st_ref):
        step_id = pl.program_id(0)

        @pl.when(step_id == 0)
        def _():
            st_ref[...] = jnp.zeros_like(st_ref)
            for ref in (dbr_ref, dbi_ref, dcr_ref, dci_ref, dar_ref, dai_ref):
                ref[...] = jnp.zeros_like(ref)

        dyb = dy_ref[...].astype(BF16)
        x_re, x_im = xr_ref[...], xi_ref[...]
        gr[...] = _packed_in(dyb, cr_ref).reshape(tc, _SCAN_ROWS, LANES)
        gi[...] = (-_packed_in(dyb, ci_ref)).reshape(tc, _SCAN_ROWS, LANES)
        x3r[...] = x_re.reshape(tc, _SCAN_ROWS, LANES)
        x3i[...] = x_im.reshape(tc, _SCAN_ROWS, LANES)
        ar, ai = ar_ref[...], ai_ref[...]

        def update(i, carry, pxr, pxi):
            g_r, g_i, dar, dai = carry
            ngr = gr[i] + ar * g_r + ai * g_i
            ngi = gi[i] - ai * g_r + ar * g_i
            gr[i] = ngr
            gi[i] = ngi
            return ngr, ngi, dar + ngr * pxr + ngi * pxi, dai - ngr * pxi + ngi * pxr

        def step(s, carry):
            i = tc - 1 - s
            return update(i, carry, x3r[i - 1], x3i[i - 1])

        zero = jnp.zeros((_SCAN_ROWS, LANES), F32)
        carry = lax.fori_loop(0, tc - 1, step, (st_ref[0], st_ref[1], zero, zero), unroll=8)
        has_prev = (step_id < nc - 1).astype(F32)
        last = SUBLANES - 1
        p_re = pr_ref[last:, :].reshape(1, _SCAN_ROWS, LANES)[0] * has_prev
        p_im = pi_ref[last:, :].reshape(1, _SCAN_ROWS, LANES)[0] * has_prev
        g_r, g_i, dar, dai = update(0, carry, p_re, p_im)
        st_ref[0] = g_r
        st_ref[1] = g_i
        dar_ref[...] += dar
        dai_ref[...] += dai

        g_re = gr[...].reshape(tc, SSM_COLS).astype(BF16)
        g_im = gi[...].reshape(tc, SSM_COLS).astype(BF16)
        du_ref[...] = ds_ref[...] + _packed_out(g_re, br_ref) + _packed_out(g_im, bi_ref)
        ub, xrb, xib = u_ref[...].astype(BF16), x_re.astype(BF16), x_im.astype(BF16)
        for j in range(N_SSM_BLOCKS):
            narrow, wide = slice(j * _SSM_IN, (j + 1) * _SSM_IN), slice(j * _SSM_ST, (j + 1) * _SSM_ST)
            dbr_ref[j] += lax.dot_general(ub[:, narrow], g_re[:, wide], _TN, preferred_element_type=F32)
            dbi_ref[j] += lax.dot_general(ub[:, narrow], g_im[:, wide], _TN, preferred_element_type=F32)
            dcr_ref[j] += lax.dot_general(dyb[:, narrow], xrb[:, wide], _TN, preferred_element_type=F32)
            dci_ref[j] -= lax.dot_general(dyb[:, narrow], xib[:, wide], _TN, preferred_element_type=F32)

    rev = lambda w: pl.BlockSpec((tc, w), lambda i: (nc - 1 - i, 0))
    prev = pl.BlockSpec((SUBLANES, SSM_COLS), lambda i: (jnp.maximum((nc - 1 - i) * (tc // SUBLANES) - 1, 0), 0))
    mat = pl.BlockSpec(mats[0].shape, _zero_map(3))
    par = pl.BlockSpec((_SCAN_ROWS, LANES), _zero_map(2))
    msh = jax.ShapeDtypeStruct(mats[0].shape, F32)
    psh = jax.ShapeDtypeStruct((_SCAN_ROWS, LANES), F32)
    return pl.pallas_call(
        body, grid=(nc,),
        in_specs=[rev(WIDTH_C), rev(WIDTH_C), rev(WIDTH_C), rev(SSM_COLS), rev(SSM_COLS), prev, prev,
                  mat, mat, mat, mat, par, par],
        out_specs=[rev(WIDTH_C), mat, mat, mat, mat, par, par],
        out_shape=[jax.ShapeDtypeStruct((t, WIDTH_C), F32), msh, msh, msh, msh, psh, psh],
        scratch_shapes=[pltpu.VMEM((tc, _SCAN_ROWS, LANES), F32)] * 4 + [pltpu.VMEM((2, _SCAN_ROWS, LANES), F32)],
        compiler_params=_cparams("arbitrary"), name="s5_bwd")(dys, duc_skip, uc, xr, xi, xr, xi, *mats, are, aim)


def _diag_blocks(a, b):
    return [(j, i, slice(i * a, (i + 1) * a), slice(i * b, (i + 1) * b))
            for j in range(N_SSM_BLOCKS) for i in range(SSM_PACK)]


def _block_diag(ms):
    _, a, b = ms[0].shape

    def body(*refs):
        for m_ref, o_ref in zip(refs[:len(ms)], refs[len(ms):]):
            o_ref[...] = jnp.zeros_like(o_ref)
            for j, i, rows, cols in _diag_blocks(a, b):
                o_ref[j, rows, cols] = m_ref[j * SSM_PACK + i].astype(o_ref.dtype)

    vm = pl.BlockSpec(memory_space=pltpu.VMEM)
    shape = jax.ShapeDtypeStruct((N_SSM_BLOCKS, SSM_PACK * a, SSM_PACK * b), BF16)
    return pl.pallas_call(body, in_specs=[vm] * len(ms), out_specs=[vm] * len(ms), out_shape=[shape] * len(ms),
                          name="s5_block_diag")(*ms)


def _block_diag_extract(ms, a, b):
    def body(*refs):
        for m_ref, o_ref in zip(refs[:len(ms)], refs[len(ms):]):
            for j, i, rows, cols in _diag_blocks(a, b):
                o_ref[j * SSM_PACK + i] = m_ref[j, rows, cols]

    vm = pl.BlockSpec(memory_space=pltpu.VMEM)
    shape = jax.ShapeDtypeStruct((N_GROUPS_C, a, b), F32)
    return pl.pallas_call(body, in_specs=[vm] * len(ms), out_specs=[vm] * len(ms), out_shape=[shape] * len(ms),
                          name="s5_block_diag_extract")(*ms)


def _exchange(src, *, gather, name):
    shape = src.shape if gather else src.shape[1:]

    def body(src_ref, out_ref, send_sems, recv_sems, local_sem):
        x, y, c = lax.axis_index("x"), lax.axis_index("y"), lax.axis_index("c")
        me = 4 * x + 2 * y + c
        copies = []
        for r in range(1, N_DEV):
            px = 1 - x if r & 4 else x
            py = 1 - y if r & 2 else y
            pc = 1 - c if r & 1 else c
            piece = src_ref if gather else src_ref.at[4 * px + 2 * py + pc]
            cp = pltpu.make_async_remote_copy(
                src_ref=piece, dst_ref=out_ref.at[me], send_sem=send_sems.at[r - 1], recv_sem=recv_sems.at[r - 1],
                device_id=(px, py, pc), device_id_type=pl.DeviceIdType.MESH)
            cp.start()
            copies.append(cp)
        mine = pltpu.make_async_copy(src_ref if gather else src_ref.at[me], out_ref.at[me], local_sem)
        mine.start()
        for cp in copies:
            cp.wait()
        mine.wait()

    hbm = pl.BlockSpec(memory_space=pl.ANY)
    return pl.pallas_call(
        body, in_specs=[hbm], out_specs=hbm, out_shape=jax.ShapeDtypeStruct((N_DEV,) + tuple(shape), src.dtype),
        scratch_shapes=[pltpu.SemaphoreType.DMA((N_DEV - 1,)), pltpu.SemaphoreType.DMA((N_DEV - 1,)),
                        pltpu.SemaphoreType.DMA(())],
        name=name)(src)


def _mesh_place():
    x, y, c = lax.axis_index("x"), lax.axis_index("y"), lax.axis_index("c")
    other_chips = [(1 - x, y), (x, 1 - y), (1 - x, 1 - y)]
    return x, y, c, other_chips


def _gather_layer(srcs, layer, name):
    n = len(srcs)

    def body(*refs):
        src = [r.at[layer] for r in refs[:n]]
        out = refs[n:2 * n]
        send_sems, recv_sems, local_sems = refs[2 * n:]
        x, y, c, chips = _mesh_place()
        me, sibling = (x, y, c), (x, y, 1 - c)

        def copy(t, k, block, to, from_src=False):
            slot = 4 * block[0] + 2 * block[1] + block[2]
            return pltpu.make_async_remote_copy(
                src_ref=src[t] if from_src else out[t].at[slot], dst_ref=out[t].at[slot],
                send_sem=send_sems.at[t, k], recv_sem=recv_sems.at[t, k], device_id=to, device_id_type=_MESH_ID)

        mine = [pltpu.make_async_copy(src[t], out[t].at[4 * x + 2 * y + c], local_sems.at[t]) for t in range(n)]
        for cp in mine:
            cp.start()
        first = []
        for t in range(n):
            first.append(copy(t, 0, me, sibling, True))
            first += [copy(t, 1 + j, me, (*chip, c), True) for j, chip in enumerate(chips)]
        for cp in first:
            cp.start()
        passed = []
        for j, chip in enumerate(chips):
            for t in range(n):
                copy(t, 1 + j, (*chip, c), me).wait_recv()
                fwd = copy(t, 4 + j, (*chip, c), sibling)
                fwd.start()
                passed.append(fwd)
        for t in range(n):
            copy(t, 0, sibling, me).wait_recv()
            for j, chip in enumerate(chips):
                copy(t, 4 + j, (*chip, 1 - c), me).wait_recv()
        for cp in first + passed:
            cp.wait_send()
        for cp in mine:
            cp.wait()

    return pl.pallas_call(
        body, in_specs=[_HBM] * n, out_specs=[_HBM] * n,
        out_shape=[jax.ShapeDtypeStruct((N_DEV,) + s.shape[1:], s.dtype) for s in srcs],
        scratch_shapes=[pltpu.SemaphoreType.DMA((n, N_DEV - 1)), pltpu.SemaphoreType.DMA((n, N_DEV - 1)),
                        pltpu.SemaphoreType.DMA((n,))],
        name=name)(*srcs)


def _scatter_pair(srcs, name):
    n = len(srcs)

    def body(*refs):
        src, out = refs[:n], refs[n:2 * n]
        send_sems, recv_sems = refs[2 * n:]
        x, y, c, _ = _mesh_place()
        copies = [pltpu.make_async_remote_copy(
            src_ref=src[t].at[:, 1 - c], dst_ref=out[t], send_sem=send_sems.at[t], recv_sem=recv_sems.at[t],
            device_id=(x, y, 1 - c), device_id_type=_MESH_ID) for t in range(n)]
        for cp in copies:
            cp.start()
        for cp in copies:
            cp.wait()

    return pl.pallas_call(
        body, in_specs=[_HBM] * n, out_specs=[_HBM] * n,
        out_shape=[jax.ShapeDtypeStruct((s.shape[0],) + s.shape[2:], s.dtype) for s in srcs],
        scratch_shapes=[pltpu.SemaphoreType.DMA((n,)), pltpu.SemaphoreType.DMA((n,))], name=name)(*srcs)


def _pair_add(src, recv, name):
    nchip, _, r, cdim = src.shape
    tr = _pick(r, 256, 2 * SUBLANES)
    core = lax.axis_index("c").astype(jnp.int32).reshape(1)

    def body(core_ref, s_ref, r_ref, o_ref):
        o_ref[...] = (s_ref[...].astype(F32) + r_ref[...].astype(F32)).astype(o_ref.dtype)

    grid_spec = pltpu.PrefetchScalarGridSpec(
        num_scalar_prefetch=1, grid=(nchip, r // tr),
        in_specs=[pl.BlockSpec((None, None, tr, cdim), lambda k, i, core_ref: (k, core_ref[0], i, 0)),
                  pl.BlockSpec((None, tr, cdim), lambda k, i, core_ref: (k, i, 0))],
        out_specs=pl.BlockSpec((None, tr, cdim), lambda k, i, core_ref: (k, i, 0)))
    return pl.pallas_call(body, grid_spec=grid_spec, out_shape=jax.ShapeDtypeStruct(recv.shape, recv.dtype),
                          compiler_params=_cparams("parallel", "parallel"), name=name)(core, src, recv)


def _scatter_chips(srcs, name):
    n = len(srcs)

    def body(*refs):
        src, out = refs[:n], refs[n:2 * n]
        send_sems, recv_sems, local_sems = refs[2 * n:]
        x, y, c, chips = _mesh_place()
        my_chip = 2 * x + y
        mine = [pltpu.make_async_copy(src[t].at[my_chip], out[t].at[my_chip], local_sems.at[t]) for t in range(n)]
        copies = [pltpu.make_async_remote_copy(
            src_ref=src[t].at[2 * chip[0] + chip[1]], dst_ref=out[t].at[my_chip],
            send_sem=send_sems.at[t, j], recv_sem=recv_sems.at[t, j], device_id=(*chip, c), device_id_type=_MESH_ID)
            for t in range(n) for j, chip in enumerate(chips)]
        for cp in mine + copies:
            cp.start()
        for cp in copies + mine:
            cp.wait()

    return pl.pallas_call(
        body, in_specs=[_HBM] * n, out_specs=[_HBM] * n,
        out_shape=[jax.ShapeDtypeStruct(s.shape, s.dtype) for s in srcs],
        scratch_shapes=[pltpu.SemaphoreType.DMA((n, 3)), pltpu.SemaphoreType.DMA((n, 3)), pltpu.SemaphoreType.DMA((n,))],
        name=name)(*srcs)


def _by_chip_and_core(g):
    return g.reshape(N_DEV // 2, 2, g.shape[0] // N_DEV, g.shape[1])


def _chip_sums(grads, tag, from_sibling=None):
    views = [_by_chip_and_core(g) for g in grads]
    if from_sibling is None:
        from_sibling = _scatter_pair(views, name="scatter_pair_" + tag)
    return [_pair_add(v, s, name="pair_add") for v, s in zip(views, from_sibling)]


def _pair_ride(grads):
    views = [_by_chip_and_core(g) for g in grads]

    def copies(in_refs, buf_refs, sems):
        x, y, c, _ = _mesh_place()
        return [pltpu.make_async_remote_copy(src_ref=src.at[:, 1 - c], dst_ref=dst, send_sem=sems.at[2 * t],
                                             recv_sem=sems.at[2 * t + 1], device_id=(x, y, 1 - c), device_id_type=_MESH_ID)
                for t, (src, dst) in enumerate(zip(in_refs, buf_refs))]

    def start(in_refs, buf_refs, sems):
        for cp in copies(in_refs, buf_refs, sems):
            cp.start()

    def finish(in_refs, buf_refs, sems):
        for cp in copies(in_refs, buf_refs, sems):
            cp.wait()

    created = tuple(jax.ShapeDtypeStruct((v.shape[0],) + v.shape[2:], v.dtype) for v in views)
    return _Ride(inputs=tuple(views), carried=(), created=created, n_sems=2 * len(views), start=start, finish=finish)


def _sum_chips(parts):
    return _rowmap(_sum_parts_fn, [], stacks=[parts], row_outs=[(parts.shape[2], F32)], tr=128, name="sum_chips")[0]


def _buffer_roles(kinds, bufs):
    carried = [k for k in kinds if k in bufs]
    return carried, [k for k in kinds if k not in bufs]


def _gather_ride(sends, forwards, bufs):
    carried, created = _buffer_roles(list(dict.fromkeys([s[0] for s in sends] + [f[0] for f in forwards])), bufs)
    shape_of = {s[0]: jax.ShapeDtypeStruct((N_DEV,) + s[1].shape, s[1].dtype) for s in sends}
    order = carried + created

    def copies(in_refs, buf_refs, sems):
        x, y, c, chips = _mesh_place()
        buf = dict(zip(order, buf_refs))
        out, s0 = [], 0
        for (kind, _, r0, nr), src in zip(sends, in_refs):
            mine, dst = src.at[pl.ds(r0, nr)], buf[kind].at[4 * x + 2 * y + c, pl.ds(r0, nr)]
            out.append(pltpu.make_async_copy(mine, dst, sems.at[s0 + 8]))
            for k, peer in enumerate([(x, y, 1 - c)] + [(*chip, c) for chip in chips]):
                out.append(pltpu.make_async_remote_copy(src_ref=mine, dst_ref=dst, send_sem=sems.at[s0 + k],
                                                        recv_sem=sems.at[s0 + 4 + k], device_id=peer, device_id_type=_MESH_ID))
            s0 += 9
        for kind, r0, nr in forwards:
            for j, chip in enumerate(chips):
                blk = buf[kind].at[4 * chip[0] + 2 * chip[1] + c, pl.ds(r0, nr)]
                out.append(pltpu.make_async_remote_copy(src_ref=blk, dst_ref=blk, send_sem=sems.at[s0 + j],
                                                        recv_sem=sems.at[s0 + 3 + j], device_id=(x, y, 1 - c),
                                                        device_id_type=_MESH_ID))
            s0 += 6
        return out

    def start(in_refs, buf_refs, sems):
        for cp in copies(in_refs, buf_refs, sems):
            cp.start()

    def finish(in_refs, buf_refs, sems):
        for cp in copies(in_refs, buf_refs, sems):
            cp.wait()

    ride = _Ride(inputs=tuple(s[1] for s in sends), carried=tuple(bufs[k] for k in carried),
                 created=tuple(shape_of[k] for k in created), n_sems=9 * len(sends) + 6 * len(forwards),
                 start=start, finish=finish)
    return ride, order


def _scatter_ride(pieces, bufs):
    carried, created = _buffer_roles(list(dict.fromkeys(p[0] for p in pieces)), bufs)
    shape_of = {p[0]: jax.ShapeDtypeStruct(p[1].shape, p[1].dtype) for p in pieces}
    order = carried + created

    def copies(in_refs, buf_refs, sems):
        x, y, c, chips = _mesh_place()
        buf = dict(zip(order, buf_refs))
        out, s0 = [], 0
        for (kind, _, r0, nr), src in zip(pieces, in_refs):
            dst = buf[kind].at[2 * x + y, pl.ds(r0, nr)]
            out.append(pltpu.make_async_copy(src.at[2 * x + y, pl.ds(r0, nr)], dst, sems.at[s0 + 6]))
            for j, chip in enumerate(chips):
                out.append(pltpu.make_async_remote_copy(
                    src_ref=src.at[2 * chip[0] + chip[1], pl.ds(r0, nr)], dst_ref=dst, send_sem=sems.at[s0 + j],
                    recv_sem=sems.at[s0 + 3 + j], device_id=(*chip, c), device_id_type=_MESH_ID))
            s0 += 7
        return out

    def start(in_refs, buf_refs, sems):
        for cp in copies(in_refs, buf_refs, sems):
            cp.start()

    def finish(in_refs, buf_refs, sems):
        for cp in copies(in_refs, buf_refs, sems):
            cp.wait()

    ride = _Ride(inputs=tuple(p[1] for p in pieces), carried=tuple(bufs[k] for k in carried),
                 created=tuple(shape_of[k] for k in created), n_sems=7 * len(pieces), start=start, finish=finish)
    return ride, order


GATHER_PLAN = (
    ("mm_in_qkv", (("w_in", 0, 3),)),
    ("mm_in_gl", (("w_in", 1, 3), ("w_glu", 0, 1), ("w_pa", 0, 1))),
    ("attn_fwd0", (("w_pb", 0, 1), ("w_ffn_out", 0, 4))),
    ("attn_fwd1", (("w_pc", 0, 1), ("w_ffn_out", 1, 4))),
    ("attn_fwd2", (("w_o", 0, 1),)),
    ("s5_fwd", (("w_ffn_out", 2, 4), ("w_ffn_out", 3, 4))),
    ("mm_ffn_in", (("w_in", 2, 3), ("w_ffn_in", 0, 2))),
    ("mm_ffn_out", (("w_ffn_in", 1, 2),)),
    ("norm2", ()),
)
EARLY_KINDS = ("w_ffn_out", "w_ffn_in")
LATE_KINDS = tuple(n for n in SHARDED if n not in EARLY_KINDS)
SCATTER_EARLY_PLAN = (
    ("mm_in_dw_qkv", (("w_ffn_out", 0, 2), ("w_ffn_in", 7, 8))),
    ("mm_in_dx_qkv", (("w_ffn_out", 1, 2), ("w_ffn_in", 6, 8))),
    ("mm_in_dw_gl", (("w_ffn_in", 0, 8), ("w_ffn_in", 1, 8), ("w_ffn_in", 2, 8))),
    ("mm_in_dx_gl", (("w_ffn_in", 3, 8), ("w_ffn_in", 4, 8), ("w_ffn_in", 5, 8))),
)
SCATTER_PLAN = (
    ("mm_ffn_out_dw", (("w_in", 0, 3),)),
    ("mm_ffn_out_dx", (("w_in", 1, 3),)),
    ("mm_ffn_in_dw", (("w_in", 2, 3), ("w_o", 0, 1), ("w_pa", 0, 1), ("w_pb", 0, 1), ("w_pc", 0, 1), ("w_glu", 0, 1))),
)


def _row_part(rows, part, parts):
    assert rows % (parts * 2 * SUBLANES) == 0
    return part * (rows // parts), rows // parts


class _Carried:
    def __init__(self, plan, blocks, make_ride, forwards_too):
        self.plan, self.blocks, self.make_ride, self.forwards_too = dict(plan), blocks, make_ride, forwards_too
        self.bufs, self.to_forward, self.order = {}, [], []

    def ride(self, host):
        if self.blocks is None or host not in self.plan:
            self.order = []
            return None
        sends = [(k, self.blocks[k], *_row_part(self.blocks[k].shape[-2], part, parts)) for k, part, parts in self.plan[host]]
        if self.forwards_too:
            ride, self.order = self.make_ride(sends, self.to_forward, self.bufs)
            self.to_forward = [(k, r0, nr) for k, _, r0, nr in sends]
        else:
            ride, self.order = self.make_ride(sends, self.bufs)
        return ride

    def took(self, ride_outs):
        for k, buf in zip(self.order, ride_outs[0] if ride_outs else []):
            self.bufs[k] = buf


class _OwnScatter:
    def __init__(self, enabled):
        self.enabled = enabled

    def pair_ride(self, grads):
        return _pair_ride(list(grads.values())) if self.enabled else None

    def steps(self, grads, ride_outs):
        if not self.enabled:
            return _Carried(SCATTER_EARLY_PLAN, None, _scatter_ride, forwards_too=False)
        sums = _chip_sums(list(grads.values()), "early", from_sibling=ride_outs[0])
        return _Carried(SCATTER_EARLY_PLAN, dict(zip(grads, sums)), _scatter_ride, forwards_too=False)


class _PrevScatter:
    PAIR_HOST = "norm2_bwd"

    def __init__(self, grads):
        self.grads = grads
        self.inner = _Carried(SCATTER_PLAN, None, _scatter_ride, forwards_too=False)
        self.pair_pending = False

    @property
    def bufs(self):
        return self.inner.bufs

    def ride(self, host):
        if self.grads is not None and host == self.PAIR_HOST:
            self.pair_pending = True
            return _pair_ride(list(self.grads.values()))
        return self.inner.ride(host)

    def took(self, ride_outs):
        if not self.pair_pending:
            return self.inner.took(ride_outs)
        self.pair_pending = False
        sums = _chip_sums(list(self.grads.values()), "late", from_sibling=ride_outs[0])
        self.inner = _Carried(SCATTER_PLAN, dict(zip(self.grads, sums)), _scatter_ride, forwards_too=False)


def _hosted(comm, fn, *args, name, **kwargs):
    res, ride_outs = fn(*args, name=name, rides=[comm.ride(name)], **kwargs)
    comm.took(ride_outs)
    return res


def _small_sizes(shapes):
    return [int(np.prod(shapes[n])) for n in SMALL]


def _pack_small(vals):
    flat = jnp.concatenate([vals[n].reshape(-1).astype(F32) for n in SMALL])
    rows = -(-flat.shape[0] // (LANES * N_DEV * SUBLANES)) * (N_DEV * SUBLANES)
    return jnp.pad(flat, (0, rows * LANES - flat.shape[0])).reshape(rows, LANES)


def _unpack_small(packed, shapes):
    flat = packed.reshape(-1)
    out, off = {}, 0
    for n, size in zip(SMALL, _small_sizes(shapes)):
        out[n] = flat[off:off + size].reshape(shapes[n])
        off += size
    return out


def _row(v):
    return v.reshape(1, -1)


def _layer_params(l, full, small):
    o1, o2, o3 = 3 * QKV_WIDTH, 3 * QKV_WIDTH + 2 * WIDTH_B, 3 * QKV_WIDTH + 2 * WIDTH_B + WIDTH_C
    b_in = small["b_in"][l]
    p = {
        "in_pieces": (("qkv", 0, o1), ("zb", o1, o2 - o1), ("uc", o2, o3 - o2), ("gl", o3, b_in.shape[0] - o3)),
        "b_qkv": _row(b_in[:o1]), "b_zb": _row(b_in[o1:o2]), "b_uc": _row(b_in[o2:o3]), "b_gl": _row(b_in[o3:]),
        "sgu_ln_g": _row(small["sgu_ln_g"][l]), "sgu_ln_b": _row(small["sgu_ln_b"][l]),
        "w_s": small["w_s"][l], "b_s_t": small["b_s"][l].T,
        "lam_re": small["lam_re"][l][:, None, :], "lam_im": small["lam_im"][l][:, None, :],
        "log_dt": small["log_dt"][l][:, None, None],
        "b_re_t": small["b_re"][l].transpose(0, 2, 1), "b_im_t": small["b_im"][l].transpose(0, 2, 1),
        "c_re": small["c_re"][l], "c_im": small["c_im"][l],
        "d_skip": _row(small["d_skip"][l]), "b_glu": _row(small["b_glu"][l]),
        "ln1_g": _row(small["ln1_g"][l]), "ln1_b": _row(small["ln1_b"][l]),
        "ln2_g": _row(small["ln2_g"][l]), "ln2_b": _row(small["ln2_b"][l]),
    }
    for n in SHARDED:
        p[n] = full[n]
    return p


def _twice(fn):
    def both(*args):
        y = fn(*args)
        return y, y
    return both


def _layer_fwd(x, xb, p, biases, comm):
    t, d = x.shape
    r = {"x": x, "xb": xb}
    for piece, off, n in p["in_pieces"]:
        r[piece] = _hosted(comm, _mm, xb, p["w_in"], tb=True, b_off=off, n=n, bias=p["b_" + piece],
                           out_dtype=BF16 if piece in ("qkv", "gl") else F32, name="mm_in_" + piece)
    ol = []
    for g, dil in enumerate(ATT_DILATIONS):
        ol += list(_attn_fwd(r["qkv"], biases[g], g, dil, comm))
    r["ol"] = ol
    r["ya"], r["ya_b"] = _rowmap(_twice(_combine), ol, row_outs=[(WIDTH_A, F32), (WIDTH_A, BF16)], tr=512,
                                 name="attn_combine")
    r["yb"] = _gmlp_fwd(r["zb"], p["sgu_ln_g"], p["sgu_ln_b"], p["w_s"], p["b_s_t"])
    ab_re, ab_im, bb_re_t, bb_im_t = _s5_disc_fwd(p["lam_re"], p["lam_im"], p["log_dt"], p["b_re_t"], p["b_im_t"])
    r["a_re"], r["a_im"] = ab_re.reshape(_SCAN_ROWS, LANES), ab_im.reshape(_SCAN_ROWS, LANES)
    r["s5_mats"] = _block_diag([bb_re_t, bb_im_t, p["c_re"], p["c_im"]])
    r["xr"], r["xi"], r["ys"], r["ycp"] = _s5_fwd(r["uc"], r["s5_mats"], r["a_re"], r["a_im"], p["d_skip"], comm)
    r["glin"], r["yc"] = _mm(r["ycp"], p["w_glu"], bias=p["b_glu"], glu_of=r["ycp"], name="mm_glu")
    r["pa"] = _mm(r["ya_b"], p["w_pa"], tb=True, out_dtype=BF16, name="mm_pa")
    r["pb"] = _mm(r["yb"], p["w_pb"], tb=True, out_dtype=BF16, name="mm_pb")
    r["pc"] = _mm(r["yc"], p["w_pc"], tb=True, out_dtype=BF16, name="mm_pc")
    r["merged"] = _rowmap(_merge, [r["gl"], r["pa"], r["pb"], r["pc"]], row_outs=[(d, BF16)], name="merge")[0]
    r["mo"] = _hosted(comm, _mm, r["merged"], p["w_o"], name="mm_o")
    r["xm"], r["xm_b"] = _rowmap(_twice(_post_norm), [x, r["mo"]], consts=[p["ln1_g"], p["ln1_b"]],
                                 row_outs=[(d, F32), (d, BF16)], name="norm1")
    r["gate"], r["up"], r["act"] = _hosted(comm, _mm_swiglu, r["xm_b"], p["w_ffn_in"], name="mm_ffn_in")
    r["f"] = _hosted(comm, _mm, r["act"], p["w_ffn_out"], name="mm_ffn_out")
    out, out_b = _hosted(comm, _rowmap, _twice(_post_norm), [r["xm"], r["f"]], consts=[p["ln2_g"], p["ln2_b"]],
                         row_outs=[(d, F32), (d, BF16)], name="norm2")
    return out, out_b, r


def _layer_bwd(dout, r, p, biases, consts, comm, own):
    t, d = dout.shape
    gw, gs = {}, {}
    ffw = 2 * r["gate"].shape[1]
    dxm, df, gs["ln2_g"], gs["ln2_b"] = _hosted(
        comm, _rowmap, _post_norm_bwd, [r["xm"], r["f"], dout], consts=[p["ln2_g"], p["ln2_b"]],
        row_outs=[(d, F32), (d, BF16)], red_outs=[(1, d)] * 2, name="norm2_bwd")
    gw["w_ffn_out"] = _hosted(comm, _mm, r["act"], df, ta=True, out_dtype=BF16, name="mm_ffn_out_dw")
    dact = _hosted(comm, _mm, df, p["w_ffn_out"], tb=True, out_dtype=BF16, name="mm_ffn_out_dx")
    dgu = _rowmap(_swiglu_bwd, [r["gate"], r["up"], dact], row_outs=[(ffw, BF16)], tr=128, name="swiglu_bwd")[0]
    gw["w_ffn_in"] = _hosted(comm, _mm, dgu, r["xm_b"], ta=True, out_dtype=BF16, name="mm_ffn_in_dw")
    early_grads = {n: gw[n] for n in EARLY_KINDS}
    dxm, from_sibling = _mm(dgu, p["w_ffn_in"], add=dxm, name="mm_ffn_in_dx", rides=[own.pair_ride(early_grads)])
    early = own.steps(early_grads, from_sibling)
    dx, dmo, gs["ln1_g"], gs["ln1_b"] = _rowmap(
        _post_norm_bwd, [r["x"], r["mo"], dxm], consts=[p["ln1_g"], p["ln1_b"]],
        row_outs=[(d, F32), (d, BF16)], red_outs=[(1, d)] * 2, name="norm1_bwd")
    gw["w_o"] = _hosted(comm, _mm, r["merged"], dmo, ta=True, out_dtype=BF16, name="mm_o_dw")
    dmerged = _hosted(comm, _mm, dmo, p["w_o"], tb=True, name="mm_o_dx")
    dgl, dpa, dpb, dpc, db_gl = _rowmap(
        _merge_bwd, [r["gl"], r["pa"], r["pb"], r["pc"], dmerged],
        row_outs=[(3 * d, BF16), (d, BF16), (d, BF16), (d, BF16)], red_outs=[(1, 3 * d)], tr=128, name="merge_bwd")
    gw["w_pa"] = _mm(dpa, r["ya_b"], ta=True, out_dtype=BF16, name="mm_pa_dw")
    gw["w_pb"] = _mm(dpb, r["yb"], ta=True, out_dtype=BF16, name="mm_pb_dw")
    gw["w_pc"] = _mm(dpc, r["yc"], ta=True, out_dtype=BF16, name="mm_pc_dw")
    dya = _mm(dpa, p["w_pa"], name="mm_pa_dx")
    dyb = _mm(dpb, p["w_pb"], name="mm_pb_dx")
    dyc = _mm(dpc, p["w_pc"], name="mm_pc_dx")
    dycp, dglin, gs["b_glu"] = _rowmap(_glu_bwd, [r["ycp"], r["glin"], dyc], row_outs=[(WIDTH_C, F32), (WIDTH_C, BF16)],
                                       red_outs=[(1, WIDTH_C)], tr=512, name="glu_bwd")
    gw["w_glu"] = _mm(r["ycp"], dglin, ta=True, out_dtype=BF16, name="mm_glu_dw")
    dycp = _mm(dglin, p["w_glu"], tb=True, add=dycp, name="mm_glu_dx")
    dys, duc, gs["d_skip"] = _rowmap(_s5_out_bwd, [r["ys"], r["uc"], dycp], consts=[p["d_skip"]],
                                     row_outs=[(WIDTH_C, F32)] * 2, red_outs=[(1, WIDTH_C)], tr=512, name="s5_out_act_bwd")
    duc, d_bmat_re, d_bmat_im, d_cmat_re, d_cmat_im, da_re, da_im = _s5_bwd(
        dys, duc, r["uc"], r["xr"], r["xi"], r["s5_mats"], r["a_re"], r["a_im"])
    d_bb_re_t, d_bb_im_t, gs["c_re"], gs["c_im"] = _block_diag_extract(
        [d_bmat_re, d_bmat_im, d_cmat_re, d_cmat_im], SSM_GROUP, SSM_STATE)
    cts = (da_re.reshape(N_GROUPS_C, 1, SSM_STATE), da_im.reshape(N_GROUPS_C, 1, SSM_STATE), d_bb_re_t, d_bb_im_t)
    d_lr, d_li, d_ldt, d_br_t, d_bi_t = _s5_disc_bwd(p["lam_re"], p["lam_im"], p["log_dt"], p["b_re_t"], p["b_im_t"], cts)
    gs["lam_re"], gs["lam_im"], gs["log_dt"] = d_lr[:, 0, :], d_li[:, 0, :], d_ldt[:, 0, 0]
    gs["b_re"], gs["b_im"] = d_br_t.transpose(0, 2, 1), d_bi_t.transpose(0, 2, 1)
    dzb, db_zb, gs["sgu_ln_g"], gs["sgu_ln_b"], gs["w_s"], dbs_t = _gmlp_bwd(
        r["zb"], dyb, p["sgu_ln_g"], p["sgu_ln_b"], p["w_s"], p["b_s_t"], consts["group_sel"])
    gs["b_s"] = dbs_t[:, :N_GROUPS_B].T
    do_corr = _rowmap(_combine_bwd, r["ol"] + [r["ya"], dya], consts=[consts["head_ones"]],
                      row_outs=[(WIDTH_A, F32)] * 6, tr=512, name="attn_combine_bwd")
    dq, dk, dv, dbias = [], [], [], []
    for g, dil in enumerate(ATT_DILATIONS):
        do_g, corr_g, lse_g = do_corr[g], do_corr[3 + g], r["ol"][2 * g + 1]
        dq_g, dk_g, dv_g, db_g = _attn_bwd(r["qkv"], biases[g], do_g, lse_g, corr_g, g, dil)
        dq.append(dq_g)
        dk.append(dk_g)
        dv.append(dv_g)
        dbias.append(db_g)
    def cast_colsum(*pieces):
        a = pieces[0] if len(pieces) == 1 else jnp.concatenate(pieces, axis=1)
        return a, jnp.sum(a, axis=0, keepdims=True)

    dqkv, db_qkv = _rowmap(cast_colsum, dq + dk + dv, row_outs=[(3 * QKV_WIDTH, BF16)],
                           red_outs=[(1, 3 * QKV_WIDTH)], tr=256, name="cast_colsum_qkv")
    duc, db_uc = _rowmap(cast_colsum, [duc], row_outs=[(WIDTH_C, BF16)], red_outs=[(1, WIDTH_C)], tr=512,
                         name="cast_colsum_uc")
    dpieces = {"qkv": dqkv, "zb": dzb, "uc": duc, "gl": dgl}
    rows_in = p["w_in"].shape[0]
    dw_in = None
    for piece, off, n in p["in_pieces"]:
        dw_in = _hosted(early, _mm, dpieces[piece], r["xb"], ta=True, out_dtype=BF16, into=(rows_in, off, dw_in),
                        name="mm_in_dw_" + piece)
        dx = _hosted(early, _mm, dpieces[piece], p["w_in"], b_off=off, add=dx, name="mm_in_dx_" + piece)
    gw["w_in"] = dw_in
    gs["b_in"] = jnp.concatenate([db_qkv, db_zb, db_uc, db_gl], axis=1)[0]
    for n in ("sgu_ln_g", "sgu_ln_b", "d_skip", "b_glu", "ln1_g", "ln1_b", "ln2_g", "ln2_b"):
        gs[n] = gs[n][0]
    return dx, gw, gs, dbias, early.bufs


def _cast_bf16(w):
    w2 = w.reshape(-1, w.shape[-1])
    out = _rowmap(lambda a: a, [w2], row_outs=[(w2.shape[1], BF16)], tr=512, name="cast_bf16")[0]
    return out.reshape(w.shape)


def _static_consts():
    head_ones = np.kron(np.eye(HEADS_PER_GROUP, dtype=np.float32), np.ones((HEAD_DIM, HEAD_DIM), np.float32))
    group_sel = np.zeros((WIDTH_B, LANES), np.float32)
    group_sel[np.arange(WIDTH_B), np.arange(WIDTH_B) // CHUNK] = 1.0
    return {"head_ones": jnp.asarray(head_ones), "group_sel": jnp.asarray(group_sel)}


def _step(x, tgt, w, m, v):
    shapes = {n: w[n].shape for n in WEIGHTS}
    consts = _static_consts()
    mine_bf = {n: _cast_bf16(w[n].transpose(0, 2, 1) if n in TRANSPOSED else w[n]) for n in SHARDED}
    small = {n: w[n] for n in SMALL}
    buckets = [jnp.asarray(_bucket_table(dil)) for dil in ATT_DILATIONS]
    biases = [_bias_fwd(w["rel_bias"], buckets[g], g) for g in range(len(ATT_DILATIONS))]
    params, saved = [], []
    h, hb = _rowmap(_twice(lambda a: a), [x], row_outs=[(x.shape[1], F32), (x.shape[1], BF16)], name="cast_x")
    gathered = dict(zip(SHARDED, _gather_layer([mine_bf[n] for n in SHARDED], 0, name="gather_layer0")))
    for l in range(DEPTH):
        p = _layer_params(l, {n: g.reshape(-1, g.shape[2]) for n, g in gathered.items()}, small)
        ahead = _Carried(GATHER_PLAN, {n: mine_bf[n][l + 1] for n in SHARDED} if l + 1 < DEPTH else None,
                         _gather_ride, forwards_too=True)
        h, hb, r = _layer_fwd(h, hb, p, biases, ahead)
        gathered = ahead.bufs
        params.append(p)
        saved.append(r)
    dy, loss_part = _rowmap(_loss_fn, [h, tgt], row_outs=[(h.shape[1], F32)], red_outs=[(1, LANES)], name="loss")
    loss = lax.psum(loss_part[0, 0], MESH_AXES)
    g_mine, gs_layers = {n: [None] * DEPTH for n in SHARDED}, [None] * DEPTH
    dbias_sum = None
    behind = _PrevScatter(None)
    for l in reversed(range(DEPTH)):
        dy, gw, gs_layers[l], dbias, early_parts = _layer_bwd(
            dy, saved[l], params[l], biases, consts, behind, _OwnScatter(enabled=True))
        saved[l] = None
        for n in EARLY_KINDS:
            g_mine[n][l] = _sum_chips(early_parts[n])
        if behind.grads is not None:
            for n in LATE_KINDS:
                g_mine[n][l + 1] = _sum_chips(behind.bufs[n])
        behind = _PrevScatter({n: gw[n] for n in LATE_KINDS})
        if l == 0:
            sums = _chip_sums(list(behind.grads.values()), "late")
            for n, parts in zip(LATE_KINDS, _scatter_chips(sums, name="scatter_chips_layer0")):
                g_mine[n][0] = _sum_chips(parts)
        if dbias_sum is None:
            dbias_sum = dbias
        else:
            dbias_sum = [_rowmap(lambda a, b: a + b, [a.reshape(-1, 2 * ATT_BLOCK), b.reshape(-1, 2 * ATT_BLOCK)],
                                 row_outs=[(2 * ATT_BLOCK, F32)], name="dbias_add")[0].reshape(a.shape)
                         for a, b in zip(dbias_sum, dbias)]
    drel = [_bias_bwd(dbias_sum[g], buckets[g], g) for g in range(len(ATT_DILATIONS))]
    drel = _rowmap(lambda a, b, c: a + b + c, drel, row_outs=[(LANES, F32)], name="drel_add")[0]
    grad_small_local = {n: jnp.stack([gs_layers[l][n] for l in range(DEPTH)]) for n in SMALL if n != "rel_bias"}
    grad_small_local["rel_bias"] = drel[:, :shapes["rel_bias"][1]]
    out_g, out_d, out_m, out_v = {}, {}, {}, {}
    for n in SHARDED:
        g = jnp.stack(g_mine[n])
        out_g[n] = g.transpose(0, 2, 1) if n in TRANSPOSED else g
        cols = shapes[n][-1]
        res = _rowmap(_adamw, [a.reshape(-1, cols) for a in (w[n], out_g[n], m[n], v[n])],
                      row_outs=[(cols, F32)] * 3, tr=128, name="adamw_" + n)
        out_d[n], out_m[n], out_v[n] = [a.reshape(shapes[n]) for a in res]
    packed = _pack_small(grad_small_local)
    rows = packed.shape[0] // N_DEV
    parts = _exchange(packed.reshape(N_DEV, rows, LANES), gather=False, name="scatter_small")
    mine = _rowmap(_sum_parts_fn, [], stacks=[parts], row_outs=[(LANES, F32)], name="sum_small")[0]
    g_small = _exchange(mine, gather=True, name="gather_small").reshape(-1, LANES)
    out_g.update(_unpack_small(g_small, {n: shapes[n] for n in SMALL}))
    for n in SMALL:
        cols = shapes[n][-1]
        res = _rowmap(_adamw, [a.reshape(-1, cols) for a in (w[n], out_g[n], m[n], v[n])],
                      row_outs=[(cols, F32)] * 3, name="adamw_" + n)
        out_d[n], out_m[n], out_v[n] = [a.reshape(shapes[n]) for a in res]
    return loss, dy, out_g, out_d, out_m, out_v


def kernel(x, w_in, b_in, rel_bias, sgu_ln_g, sgu_ln_b, w_s, b_s, lam_re, lam_im, log_dt, b_re, b_im, c_re, c_im, d_skip, w_glu, b_glu, w_pa, w_pb, w_pc, w_o, ln1_g, ln1_b, w_ffn_in, w_ffn_out, ln2_g, ln2_b, loss_target, m_w_in, m_b_in, m_rel_bias, m_sgu_ln_g, m_sgu_ln_b, m_w_s, m_b_s, m_lam_re, m_lam_im, m_log_dt, m_b_re, m_b_im, m_c_re, m_c_im, m_d_skip, m_w_glu, m_b_glu, m_w_pa, m_w_pb, m_w_pc, m_w_o, m_ln1_g, m_ln1_b, m_w_ffn_in, m_w_ffn_out, m_ln2_g, m_ln2_b, v_w_in, v_b_in, v_rel_bias, v_sgu_ln_g, v_sgu_ln_b, v_w_s, v_b_s, v_lam_re, v_lam_im, v_log_dt, v_b_re, v_b_im, v_c_re, v_c_im, v_d_skip, v_w_glu, v_b_glu, v_w_pa, v_w_pb, v_w_pc, v_w_o, v_ln1_g, v_ln1_b, v_w_ffn_in, v_w_ffn_out, v_ln2_g, v_ln2_b):
    args = dict(locals())
    w = {n: args[n] for n in WEIGHTS}
    m = {n: args["m_" + n] for n in WEIGHTS}
    v = {n: args["v_" + n] for n in WEIGHTS}
    loss, dx, g, d, nm, nv = _step(x[0], loss_target[0], w, m, v)
    return (loss, dx[None], *[g[n] for n in WEIGHTS], *[d[n] for n in WEIGHTS],
            *[nm[n] for n in WEIGHTS], *[nv[n] for n in WEIGHTS])
```
